```python
import math
import jax, jax.numpy as jnp
from jax import lax
import numpy as np

D_MODEL = 1024
BATCH = 8
SEQ = 4096
DEPTH = 1

HEAD_DIM = 64
N_ATTN_HEADS = 8
N_KV_HEADS = 2
KV_GROUP = N_ATTN_HEADS // N_KV_HEADS
N_RWKV_HEADS = 8
ATTN_WIDTH = N_ATTN_HEADS * HEAD_DIM
KV_WIDTH = N_KV_HEADS * HEAD_DIM
RWKV_WIDTH = N_RWKV_HEADS * HEAD_DIM
D_MIX = ATTN_WIDTH + RWKV_WIDTH
WINDOW = 128
ATTN_BLOCK = 128
ROT_DIM = HEAD_DIM // 4
ROPE_THETA = 500000.0
DECAY_LORA = 32
AAA_LORA = 32
GATE_LORA = 96
N_EXPERTS = 32
TOP_K = 4
D_FF = D_MODEL
SWIGLU_LIMIT = 7.0
SWIGLU_ALPHA = 1.702
MOE_BLOCK = 256
LN_EPS = 1e-5
RWKV_GN_EPS = 64e-5
NEG_INF = -1e30
DEEPNORM_ALPHA = (2 * DEPTH) ** 0.25
DEEPNORM_BETA = (8 * DEPTH) ** -0.25

IN_SIZES = (ATTN_WIDTH, KV_WIDTH, KV_WIDTH,
            RWKV_WIDTH, RWKV_WIDTH, RWKV_WIDTH,
            DECAY_LORA, AAA_LORA, GATE_LORA)
D_IN = sum(IN_SIZES)
ATTN_IN = ATTN_WIDTH + 2 * KV_WIDTH
RWKV_IN = D_IN - ATTN_IN

kernel_name = "hybrid_swa_rwkv7_moe_deepnorm_adaln"


def _layer_norm(x, g=None, b=None):
    xf = x.astype(jnp.float32)
    mu = jnp.mean(xf, axis=-1, keepdims=True)
    var = jnp.mean(jnp.square(xf - mu), axis=-1, keepdims=True)
    y = (xf - mu) * lax.rsqrt(var + LN_EPS)
    if g is not None:
        y = y * g.astype(jnp.float32) + b.astype(jnp.float32)
    return y.astype(x.dtype)


def _partial_rotary(t, positions):
    inv_freq = ROPE_THETA ** (-jnp.arange(0, ROT_DIM, 2, dtype=jnp.float32) / ROT_DIM)
    ang = positions.astype(jnp.float32)[..., None] * inv_freq
    cos, sin = jnp.cos(ang)[:, :, None, :], jnp.sin(ang)[:, :, None, :]
    tr = t[..., :ROT_DIM].astype(jnp.float32)
    t1, t2 = tr[..., :ROT_DIM // 2], tr[..., ROT_DIM // 2:]
    rot = jnp.concatenate([t1 * cos - t2 * sin, t2 * cos + t1 * sin], axis=-1).astype(t.dtype)
    return jnp.concatenate([rot, t[..., ROT_DIM:]], axis=-1)


def _sliding_window_attention(q, k, v, sinks):
    B, S = q.shape[0], q.shape[1]
    nb = S // ATTN_BLOCK
    qb = q.reshape(B, nb, ATTN_BLOCK, N_KV_HEADS, KV_GROUP, HEAD_DIM).astype(jnp.float32)

    def band(t):
        tp = jnp.pad(t, ((0, 0), (ATTN_BLOCK, 0), (0, 0), (0, 0)))
        tb = tp.reshape(B, nb + 1, ATTN_BLOCK, N_KV_HEADS, HEAD_DIM)
        return jnp.concatenate([tb[:, :-1], tb[:, 1:]], axis=2).astype(jnp.float32)

    kb, vb = band(k), band(v)
    s = jnp.einsum("bnqhgd,bnkhd->bhgnqk", qb, kb) / math.sqrt(HEAD_DIM)
    qi = jnp.arange(ATTN_BLOCK)[:, None]
    kj = jnp.arange(2 * ATTN_BLOCK)[None, :]
    rel = qi + ATTN_BLOCK - kj
    key_pos = jnp.arange(nb)[:, None, None] * ATTN_BLOCK - ATTN_BLOCK + kj[None]
    allowed = (rel >= 0) & (rel < WINDOW) & (key_pos >= 0)
    s = jnp.where(allowed, s, NEG_INF)
    sink = sinks.astype(jnp.float32).reshape(N_KV_HEADS, KV_GROUP)[None, :, :, None, None, None]
    m = jnp.maximum(jnp.max(s, axis=-1, keepdims=True), sink)
    p = jnp.exp(s - m)
    denom = jnp.sum(p, axis=-1, keepdims=True) + jnp.exp(sink - m)
    o = jnp.einsum("bhgnqk,bnkhd->bnqhgd", p / denom, vb)
    return o.reshape(B, S, ATTN_WIDTH).astype(q.dtype)


def _rwkv7_time_mix(r, k, v, wl, al, gl, w0, w2, a0, a2, g2, k_k, k_a, r_k, ln_w, ln_b):
    B, S, _ = r.shape
    heads = lambda t: t.reshape(B, S, N_RWKV_HEADS, HEAD_DIM)
    w = -jax.nn.softplus(-(w0 + jnp.tanh(wl) @ w2)) - 0.5
    decay = jnp.exp(-jnp.exp(w.astype(jnp.float32)))
    a = jax.nn.sigmoid(a0 + al @ a2)
    g = jax.nn.sigmoid(gl) @ g2
    kk = heads(k * k_k).astype(jnp.float32)
    kk = kk / jnp.maximum(jnp.sqrt(jnp.sum(kk * kk, axis=-1, keepdims=True)), 1e-12)
    k = k * (1.0 + (a - 1.0) * k_a)
    a_h = heads(a).astype(jnp.float32)

    to_time = lambda t: jnp.moveaxis(t.astype(jnp.float32), 1, 0)
    xs = (to_time(heads(r)), to_time(heads(decay)), to_time(heads(k)), to_time(heads(v)),
          to_time(-kk), to_time(kk * a_h))

    def step(state, inp):
        r_t, w_t, k_t, v_t, a_t, b_t = inp
        sa = jnp.einsum("bhvk,bhk->bhv", state, a_t)
        state = (state * w_t[:, :, None, :] + sa[..., None] * b_t[:, :, None, :]
                 + v_t[..., None] * k_t[:, :, None, :])
        return state, jnp.einsum("bhvk,bhk->bhv", state, r_t)

    s0 = jnp.zeros((B, N_RWKV_HEADS, HEAD_DIM, HEAD_DIM), jnp.float32)
    _, ys = lax.scan(step, s0, xs)
    y = jnp.moveaxis(ys, 0, 1)
    mu = jnp.mean(y, axis=-1, keepdims=True)
    var = jnp.mean(jnp.square(y - mu), axis=-1, keepdims=True)
    y = ((y - mu) * lax.rsqrt(var + RWKV_GN_EPS)).reshape(B, S, RWKV_WIDTH)
    y = y * ln_w.astype(jnp.float32) + ln_b.astype(jnp.float32)
    bonus = jnp.sum(heads(r * k).astype(jnp.float32) * r_k.astype(jnp.float32), axis=-1, keepdims=True)
    y = y + (bonus * heads(v).astype(jnp.float32)).reshape(B, S, RWKV_WIDTH)
    return (y * g.astype(jnp.float32)).astype(r.dtype)


def _moe(h, w_router, b_router, w_gate_up, b_gate_up, w_down, b_down):
    T = h.shape[0]
    n_assign = T * TOP_K
    logits = (h @ w_router + b_router).astype(jnp.float32)
    top_v, top_i = lax.top_k(logits, TOP_K)
    gates = jax.nn.softmax(top_v, axis=-1).astype(h.dtype)
    flat_e = top_i.reshape(n_assign).astype(jnp.int32)
    flat_g = gates.reshape(n_assign)
    flat_tok = jnp.arange(n_assign, dtype=jnp.int32) // TOP_K
    order = jnp.argsort(flat_e)
    se, stok, sg = flat_e[order], flat_tok[order], flat_g[order]
    counts = jax.ops.segment_sum(jnp.ones_like(flat_e), flat_e, num_segments=N_EXPERTS)
    padded = (counts + MOE_BLOCK - 1) // MOE_BLOCK * MOE_BLOCK
    start = jnp.cumsum(counts) - counts
    pend = jnp.cumsum(padded)
    pstart = pend - padded
    dest = pstart[se] + (jnp.arange(n_assign, dtype=jnp.int32) - start[se])
    n_blocks = n_assign // MOE_BLOCK + N_EXPERTS
    tok_buf = jnp.zeros((n_blocks * MOE_BLOCK,), jnp.int32).at[dest].set(stok)
    g_buf = jnp.zeros((n_blocks * MOE_BLOCK,), h.dtype).at[dest].set(sg)
    blk_e = jnp.clip(jnp.searchsorted(pend, jnp.arange(n_blocks, dtype=jnp.int32) * MOE_BLOCK,
                                      side="right"), 0, N_EXPERTS - 1).astype(jnp.int32)

    def body(out, blk):
        tok, gw, e = blk
        xb = h[tok]
        gu = xb @ w_gate_up[e] + b_gate_up[e]
        gate = jnp.minimum(gu[:, :D_FF], SWIGLU_LIMIT)
        up = jnp.clip(gu[:, D_FF:], -SWIGLU_LIMIT, SWIGLU_LIMIT)
        act = (up + 1.0) * (gate * jax.nn.sigmoid(SWIGLU_ALPHA * gate))
        yb = act @ w_down[e] + b_down[e]
        return out.at[tok].add(yb * gw[:, None]), None

    out, _ = lax.scan(body, jnp.zeros_like(h),
                      (tok_buf.reshape(n_blocks, MOE_BLOCK), g_buf.reshape(n_blocks, MOE_BLOCK), blk_e))
    return out


def _hybrid_layer(x, c, positions, w_ada, b_ada, w_in, shift_mu, rwkv_w0, rwkv_w2, rwkv_a0,
                  rwkv_a2, rwkv_g2, rwkv_k_k, rwkv_k_a, rwkv_r_k, rwkv_ln_w, rwkv_ln_b,
                  attn_sinks, w_out, ln1_g, ln1_b, w_router, b_router, w_gate_up, b_gate_up,
                  w_down, b_down, ln2_g, ln2_b):
    B, S, D = x.shape
    mod = (jax.nn.silu(c) @ w_ada + b_ada)[:, None, :]
    shift1, scale1, gate1, shift2, scale2, gate2 = jnp.split(mod, 6, axis=-1)

    h = _layer_norm(x) * (1.0 + scale1) + shift1
    proj = h @ w_in
    q, ka, va = jnp.split(proj[..., :ATTN_IN], [ATTN_WIDTH, ATTN_WIDTH + KV_WIDTH], axis=-1)
    rw = proj[..., ATTN_IN:]
    rw_prev = jnp.pad(rw, ((0, 0), (1, 0), (0, 0)))[:, :-1]
    rw = rw + (rw_prev - rw) * shift_mu
    r, kr, vr, wl, al, gl = jnp.split(
        rw, list(np.cumsum([RWKV_WIDTH, RWKV_WIDTH, RWKV_WIDTH, DECAY_LORA, AAA_LORA])), axis=-1)

    q = _partial_rotary(q.reshape(B, S, N_ATTN_HEADS, HEAD_DIM), positions)
    ka = _partial_rotary(ka.reshape(B, S, N_KV_HEADS, HEAD_DIM), positions)
    va = va.reshape(B, S, N_KV_HEADS, HEAD_DIM)
    attn_out = _sliding_window_attention(q, ka, va, attn_sinks)
    rwkv_out = _rwkv7_time_mix(r, kr, vr, wl, al, gl, rwkv_w0, rwkv_w2, rwkv_a0, rwkv_a2,
                               rwkv_g2, rwkv_k_k, rwkv_k_a, rwkv_r_k, rwkv_ln_w, rwkv_ln_b)
    y = jnp.concatenate([attn_out, rwkv_out], axis=-1) @ w_out
    x = _layer_norm(DEEPNORM_ALPHA * x + (1.0 + gate1) * y, ln1_g, ln1_b)

    h = _layer_norm(x) * (1.0 + scale2) + shift2
    y = _moe(h.reshape(B * S, D), w_router, b_router, w_gate_up, b_gate_up,
             w_down, b_down).reshape(B, S, D)
    return _layer_norm(DEEPNORM_ALPHA * x + (1.0 + gate2) * y, ln2_g, ln2_b)


def setup_inputs(seed: int = 0) -> dict:
    key = jax.random.key(seed)
    ks = jax.random.split(key, 32)
    L = DEPTH
    f32 = jnp.float32
    nrm = lambda k, shape, scale: jax.random.normal(k, shape, f32) * scale
    x = nrm(ks[0], (BATCH, SEQ, D_MODEL), 1.0)
    c = nrm(ks[1], (BATCH, D_MODEL), 1.0)
    positions = (jax.random.randint(ks[2], (BATCH, 1), 0, 4096, dtype=jnp.int32)
                 + jnp.arange(SEQ, dtype=jnp.int32)[None, :])
    w_ada = nrm(ks[3], (L, D_MODEL, 6 * D_MODEL), 0.1 * D_MODEL ** -0.5)
    b_ada = nrm(ks[4], (L, 6 * D_MODEL), 0.01)
    v_cols = (2, 5)
    col_scale = jnp.concatenate([jnp.full((n,), DEEPNORM_BETA if i in v_cols else 1.0, f32)
                                 for i, n in enumerate(IN_SIZES)])
    w_in = nrm(ks[5], (L, D_MODEL, D_IN), D_MODEL ** -0.5) * col_scale
    shift_mu = jax.random.uniform(ks[6], (L, RWKV_IN), f32)
    rwkv_w0 = jax.random.uniform(ks[7], (L, RWKV_WIDTH), f32, minval=-6.0, maxval=1.0)
    rwkv_w2 = nrm(ks[8], (L, DECAY_LORA, RWKV_WIDTH), 0.1 * DECAY_LORA ** -0.5)
    rwkv_a0 = nrm(ks[9], (L, RWKV_WIDTH), 0.5)
    rwkv_a2 = nrm(ks[10], (L, AAA_LORA, RWKV_WIDTH), AAA_LORA ** -0.5)
    rwkv_g2 = nrm(ks[11], (L, GATE_LORA, RWKV_WIDTH), GATE_LORA ** -0.5)
    rwkv_k_k = 0.85 + nrm(ks[12], (L, RWKV_WIDTH), 0.05)
    rwkv_k_a = 1.0 + nrm(ks[13], (L, RWKV_WIDTH), 0.05)
    rwkv_r_k = nrm(ks[14], (L, N_RWKV_HEADS, HEAD_DIM), 0.1)
    rwkv_ln_w = 1.0 + nrm(ks[15], (L, RWKV_WIDTH), 0.05)
    rwkv_ln_b = nrm(ks[16], (L, RWKV_WIDTH), 0.01)
    attn_sinks = nrm(ks[17], (L, N_ATTN_HEADS), 0.5)
    w_out = nrm(ks[18], (L, D_MIX, D_MODEL), D_MIX ** -0.5 * DEEPNORM_BETA)
    ln1_g = 1.0 + nrm(ks[19], (L, D_MODEL), 0.05)
    ln1_b = nrm(ks[20], (L, D_MODEL), 0.01)
    w_router = nrm(ks[21], (L, D_MODEL, N_EXPERTS), D_MODEL ** -0.5)
    b_router = nrm(ks[22], (L, N_EXPERTS), 0.01)
    w_gate_up = nrm(ks[23], (L, N_EXPERTS, D_MODEL, 2 * D_FF), D_MODEL ** -0.5 * DEEPNORM_BETA)
    b_gate_up = nrm(ks[24], (L, N_EXPERTS, 2 * D_FF), 0.01)
    w_down = nrm(ks[25], (L, N_EXPERTS, D_FF, D_MODEL), D_FF ** -0.5 * DEEPNORM_BETA)
    b_down = nrm(ks[26], (L, N_EXPERTS, D_MODEL), 0.01)
    ln2_g = 1.0 + nrm(ks[27], (L, D_MODEL), 0.05)
    ln2_b = nrm(ks[28], (L, D_MODEL), 0.01)
    return {"x": x, "c": c, "positions": positions, "w_ada": w_ada, "b_ada": b_ada,
            "w_in": w_in, "shift_mu": shift_mu, "rwkv_w0": rwkv_w0, "rwkv_w2": rwkv_w2,
            "rwkv_a0": rwkv_a0, "rwkv_a2": rwkv_a2, "rwkv_g2": rwkv_g2, "rwkv_k_k": rwkv_k_k,
            "rwkv_k_a": rwkv_k_a, "rwkv_r_k": rwkv_r_k, "rwkv_ln_w": rwkv_ln_w,
            "rwkv_ln_b": rwkv_ln_b, "attn_sinks": attn_sinks, "w_out": w_out,
            "ln1_g": ln1_g, "ln1_b": ln1_b, "w_router": w_router, "b_router": b_router,
            "w_gate_up": w_gate_up, "b_gate_up": b_gate_up, "w_down": w_down,
            "b_down": b_down, "ln2_g": ln2_g, "ln2_b": ln2_b}


def reference(x, c, positions, w_ada, b_ada, w_in, shift_mu, rwkv_w0, rwkv_w2, rwkv_a0,
              rwkv_a2, rwkv_g2, rwkv_k_k, rwkv_k_a, rwkv_r_k, rwkv_ln_w, rwkv_ln_b,
              attn_sinks, w_out, ln1_g, ln1_b, w_router, b_router, w_gate_up, b_gate_up,
              w_down, b_down, ln2_g, ln2_b):
    for l in range(DEPTH):
        x = _hybrid_layer(x, c, positions, w_ada[l], b_ada[l], w_in[l], shift_mu[l],
                          rwkv_w0[l], rwkv_w2[l], rwkv_a0[l], rwkv_a2[l], rwkv_g2[l],
                          rwkv_k_k[l], rwkv_k_a[l], rwkv_r_k[l], rwkv_ln_w[l], rwkv_ln_b[l],
                          attn_sinks[l], w_out[l], ln1_g[l], ln1_b[l], w_router[l],
                          b_router[l], w_gate_up[l], b_gate_up[l], w_down[l], b_down[l],
                          ln2_g[l], ln2_b[l])
    return x
```

```python
import functools
import math

import jax
import jax.numpy as jnp
from jax import lax
from jax.experimental import pallas as pl
from jax.experimental.pallas import tpu as pltpu

F32 = jnp.float32
BF16 = jnp.bfloat16

HEAD_DIM = 64
N_ATTN_HEADS = 8
N_KV_HEADS = 2
N_RWKV_HEADS = 8
ATTN_WIDTH = N_ATTN_HEADS * HEAD_DIM
KV_WIDTH = N_KV_HEADS * HEAD_DIM
RWKV_WIDTH = N_RWKV_HEADS * HEAD_DIM
ATTN_BLOCK = 128
ROT_DIM = HEAD_DIM // 4
ROPE_THETA = 500000.0
DECAY_LORA = 32
AAA_LORA = 32
GATE_LORA = 96
N_EXPERTS = 32
TOP_K = 4
SWIGLU_LIMIT = 7.0
SWIGLU_ALPHA = 1.702
LN_EPS = 1e-5
RWKV_GN_EPS = 64e-5
NEG_INF = -1e30
DEPTH = 1
DEEPNORM_ALPHA = (2 * DEPTH) ** 0.25

LANES = 128
RWKV_CHUNK = 64
MOE_BLOCK = 256
ATTN_PROJ = ATTN_WIDTH + 4 * KV_WIDTH
RWKV_PROJ = 3 * RWKV_WIDTH + 3 * LANES
VMEM_LIMIT = 48 * 1024 * 1024


def _dot(a, b):
    return jnp.dot(a, b, preferred_element_type=F32)


def _dot_nt(a, b):
    return lax.dot_general(a, b, (((1,), (1,)), ((), ())), preferred_element_type=F32)


def _dot_tn(a, b):
    return lax.dot_general(a, b, (((0,), (0,)), ((), ())), preferred_element_type=F32)


def _split3(x):
    h = x.astype(BF16)
    r1 = x - h.astype(F32)
    m = r1.astype(BF16)
    lo = (r1 - m.astype(F32)).astype(BF16)
    return h, m, lo


def _dot_exact_rhs(x, m_bf16):
    h, m, lo = _split3(x)
    return _dot(h, m_bf16) + _dot(m, m_bf16) + _dot(lo, m_bf16)


def _dot_exact_lhs(m_bf16, x):
    h, m, lo = _split3(x)
    return _dot(m_bf16, h) + _dot(m_bf16, m) + _dot(m_bf16, lo)


def _layer_norm(x):
    mu = jnp.mean(x, axis=-1, keepdims=True)
    xc = x - mu
    var = jnp.mean(xc * xc, axis=-1, keepdims=True)
    return xc * lax.rsqrt(var + LN_EPS)


def _sigmoid(x):
    return 1.0 / (1.0 + jnp.exp(-x))


def _mod_kernel(c_ref, w_ref, b_ref, o_ref):
    c = c_ref[...]
    s = c * _sigmoid(c)
    o_ref[...] = jnp.dot(s, w_ref[...], preferred_element_type=F32,
                         precision=lax.Precision.HIGHEST) + b_ref[...]


def _mod(c, w_ada, b_ada):
    B, D = c.shape
    n = w_ada.shape[1] // D
    return pl.pallas_call(
        _mod_kernel,
        grid=(n,),
        in_specs=[pl.BlockSpec((B, D), lambda i: (0, 0)),
                  pl.BlockSpec((D, D), lambda i: (0, i)),
                  pl.BlockSpec((1, D), lambda i: (0, i))],
        out_specs=pl.BlockSpec((B, D), lambda i: (0, i)),
        out_shape=jax.ShapeDtypeStruct((B, n * D), F32),
        compiler_params=pltpu.CompilerParams(dimension_semantics=("arbitrary",),
                                             vmem_limit_bytes=VMEM_LIMIT),
        name="mod",
    )(c, w_ada, b_ada.reshape(1, -1))


def _inproj_kernel(x_ref, pos_ref, mod_ref, wa_ref, wr_ref, mu_ref, rt_ref,
                   qkv_ref, rw_ref, carry_ref):
    j = pl.program_id(1)
    x = x_ref[0]
    tm = x.shape[0]
    mod = mod_ref[0]
    h = _layer_norm(x) * (1.0 + mod[1:2, :]) + mod[0:1, :]
    hb = h.astype(BF16)

    pa = _dot(hb, wa_ref[...])
    ang = pos_ref[0].astype(F32) * rt_ref[0:1, :]
    cs = jnp.cos(ang)
    sn = jnp.sin(ang)
    m_lo = rt_ref[1:2, :]
    m_hi = rt_ref[2:3, :]
    n_q = ATTN_WIDTH // LANES
    n_rot = (ATTN_WIDTH + 2 * KV_WIDTH) // LANES
    for ch in range(n_rot):
        t = pa[:, ch * LANES:(ch + 1) * LANES]
        if ch < n_q:
            t = t * (1.0 / math.sqrt(HEAD_DIM))
        up = pltpu.roll(t, LANES - ROT_DIM // 2, 1)
        dn = pltpu.roll(t, ROT_DIM // 2, 1)
        o = t * cs + sn * (m_lo * up + m_hi * dn)
        qkv_ref[:, ch * LANES:(ch + 1) * LANES] = o.astype(BF16)
    qkv_ref[:, n_rot * LANES:] = pa[:, n_rot * LANES:].astype(BF16)

    pr = _dot(hb, wr_ref[...])
    prev = pltpu.roll(pr, 1, 0)
    row = lax.broadcasted_iota(jnp.int32, (tm, 1), 0)
    carry = jnp.where(j == 0, 0.0, carry_ref[...])
    prev = jnp.where(row == 0, carry, prev)
    carry_ref[...] = pr[tm - 1:tm, :]
    rw_ref[...] = pr + (prev - pr) * mu_ref[...]


def _inproj(x, positions, mod, w_attn, w_rwkv, mu, rot_tab, tm):
    B, S, D = x.shape
    nt = S // tm
    return pl.pallas_call(
        _inproj_kernel,
        grid=(B, nt),
        in_specs=[pl.BlockSpec((1, tm, D), lambda b, j: (b, j, 0)),
                  pl.BlockSpec((1, tm, 1), lambda b, j: (b, j, 0)),
                  pl.BlockSpec((1, 6, D), lambda b, j: (b, 0, 0)),
                  pl.BlockSpec((D, ATTN_PROJ), lambda b, j: (0, 0)),
                  pl.BlockSpec((D, RWKV_PROJ), lambda b, j: (0, 0)),
                  pl.BlockSpec((1, RWKV_PROJ), lambda b, j: (0, 0)),
                  pl.BlockSpec((8, LANES), lambda b, j: (0, 0))],
        out_specs=[pl.BlockSpec((tm, ATTN_PROJ), lambda b, j: (b * nt + j, 0)),
                   pl.BlockSpec((tm, RWKV_PROJ), lambda b, j: (b * nt + j, 0))],
        out_shape=[jax.ShapeDtypeStruct((B * S, ATTN_PROJ), BF16),
                   jax.ShapeDtypeStruct((B * S, RWKV_PROJ), F32)],
        scratch_shapes=[pltpu.VMEM((1, RWKV_PROJ), F32)],
        compiler_params=pltpu.CompilerParams(dimension_semantics=("arbitrary", "arbitrary"),
                                             vmem_limit_bytes=VMEM_LIMIT),
        name="inproj",
    )(x, positions.reshape(B, S, 1), mod, w_attn, w_rwkv, mu, rot_tab)


def _attn_kernel(q_ref, kc_ref, kp_ref, vc_ref, vp_ref, sink_ref, o_ref):
    i = pl.program_id(1)
    blk = ATTN_BLOCK
    qi = lax.broadcasted_iota(jnp.int32, (blk, 2 * blk), 0)
    kj = lax.broadcasted_iota(jnp.int32, (blk, 2 * blk), 1)
    allowed = (kj > qi) & (kj <= qi + blk) & ((kj >= blk) | (i > 0))
    lane = lax.broadcasted_iota(jnp.int32, (1, LANES), 1)
    lo = (lane < HEAD_DIM).astype(BF16)
    hi = (lane >= HEAD_DIM).astype(BF16)
    for g in range(N_KV_HEADS):
        sl = slice(g * LANES, (g + 1) * LANES)
        kcat = jnp.concatenate([kp_ref[:, sl], kc_ref[:, sl]], axis=0)
        vcat = jnp.concatenate([vp_ref[:, sl], vc_ref[:, sl]], axis=0)
        halves = ((kcat * lo, vcat * lo), (kcat * hi, vcat * hi))
        for jj in range(2):
            c = 2 * g + jj
            qc = q_ref[:, c * LANES:(c + 1) * LANES]
            acc = jnp.zeros((blk, LANES), F32)
            for half in range(2):
                kk, vv = halves[half]
                sink = sink_ref[2 * c + half]
                s = _dot_nt(qc, kk)
                s = jnp.where(allowed, s, NEG_INF)
                m = jnp.maximum(jnp.max(s, axis=-1, keepdims=True), sink)
                p = jnp.exp(s - m)
                denom = jnp.sum(p, axis=-1, keepdims=True) + jnp.exp(sink - m)
                acc = acc + _dot(p.astype(BF16), vv) / denom
            o_ref[:, c * LANES:(c + 1) * LANES] = acc.astype(BF16)


def _attention(qkv, sinks, B, S):
    nb = S // ATTN_BLOCK
    blk = ATTN_BLOCK
    kcol = ATTN_WIDTH // (2 * KV_WIDTH)
    vcol = kcol + 1
    cur = lambda col: (lambda b, i: (b * nb + i, col))
    prv = lambda col: (lambda b, i: (jnp.maximum(b * nb + i - 1, 0), col))
    return pl.pallas_call(
        _attn_kernel,
        grid=(B, nb),
        in_specs=[pl.BlockSpec((blk, ATTN_WIDTH), lambda b, i: (b * nb + i, 0)),
                  pl.BlockSpec((blk, 2 * KV_WIDTH), cur(kcol)),
                  pl.BlockSpec((blk, 2 * KV_WIDTH), prv(kcol)),
                  pl.BlockSpec((blk, 2 * KV_WIDTH), cur(vcol)),
                  pl.BlockSpec((blk, 2 * KV_WIDTH), prv(vcol)),
                  pl.BlockSpec(memory_space=pltpu.SMEM)],
        out_specs=pl.BlockSpec((blk, ATTN_WIDTH), lambda b, i: (b * nb + i, 0)),
        out_shape=jax.ShapeDtypeStruct((B * S, ATTN_WIDTH), BF16),
        compiler_params=pltpu.CompilerParams(dimension_semantics=("arbitrary", "arbitrary"),
                                             vmem_limit_bytes=VMEM_LIMIT),
        name="attn",
    )(qkv, qkv, qkv, qkv, qkv, sinks)


def _rwkv_kernel(rw_ref, vec_ref, w2_ref, a2_ref, g2_ref, o_ref, state_ref, *, n_chunk):
    j = pl.program_id(1)
    C = RWKV_CHUNK
    W = RWKV_WIDTH
    n_pair = W // LANES

    @pl.when(j == 0)
    def _():
        state_ref[...] = jnp.zeros_like(state_ref)

    w0 = vec_ref[0:1, :]
    a0 = vec_ref[1:2, :]
    k_k = vec_ref[2:3, :]
    k_a = vec_ref[3:4, :]
    r_k = vec_ref[4:5, :]
    ln_w = vec_ref[5:6, :]
    ln_b = vec_ref[6:7, :]

    r = rw_ref[:, 0:W]
    k = rw_ref[:, W:2 * W]
    v = rw_ref[:, 2 * W:3 * W]
    wl = rw_ref[:, 3 * W:3 * W + LANES]
    al = rw_ref[:, 3 * W + LANES:3 * W + 2 * LANES]
    gl = rw_ref[:, 3 * W + 2 * LANES:3 * W + 3 * LANES]

    ri = lax.broadcasted_iota(jnp.int32, (LANES, LANES), 0)
    ci = lax.broadcasted_iota(jnp.int32, (LANES, LANES), 1)
    same = (ri // HEAD_DIM) == (ci // HEAD_DIM)
    strict = same & ((ri % HEAD_DIM) > (ci % HEAD_DIM))
    incl = same & ((ri % HEAD_DIM) >= (ci % HEAD_DIM))
    ones_bd = same.astype(BF16)
    lane = lax.broadcasted_iota(jnp.int32, (1, LANES), 1)
    m0 = (lane < HEAD_DIM).astype(F32)
    m1 = 1.0 - m0
    tri = (lax.broadcasted_iota(jnp.int32, (C, C), 0) >= lax.broadcasted_iota(jnp.int32, (C, C), 1)).astype(BF16)

    def head_sum(xv):
        return jnp.concatenate(
            [_dot_exact_rhs(xv[:, p * LANES:(p + 1) * LANES], ones_bd) for p in range(n_pair)], axis=1)

    def stack2(xp):
        return jnp.concatenate([xp * m0, xp * m1], axis=0)

    z = w0 + _dot(jnp.tanh(wl).astype(BF16), w2_ref[...])
    lw = -math.exp(-0.5) * _sigmoid(z)
    a = _sigmoid(a0 + _dot(al.astype(BF16), a2_ref[...]))
    g = _dot(_sigmoid(gl).astype(BF16), g2_ref[...])
    kk = k * k_k
    kkn = kk / jnp.maximum(jnp.sqrt(head_sum(kk * kk)), 1e-12)
    k2 = k * (1.0 + (a - 1.0) * k_a)
    av = -kkn
    bv = kkn * a
    bonus = head_sum(r * k2 * r_k) * v

    for c in range(n_chunk):
        rows = slice(c * C, (c + 1) * C)
        lwc = lw[rows]
        cw = _dot_exact_lhs(tri, lwc)
        cwl = cw[C - 1:C, :]
        e_in = jnp.exp(cw)
        e_neg = jnp.exp(-cw)
        e_rem = jnp.exp(cwl - cw)
        wc = jnp.exp(cwl)
        Rt = r[rows] * e_in
        At = av[rows] * jnp.exp(cw - lwc)
        Bb = bv[rows] * e_neg
        Kb = k2[rows] * e_neg
        Bh = bv[rows] * e_rem
        Kh = k2[rows] * e_rem
        vc = v[rows]
        ys = []
        for p in range(n_pair):
            sl = slice(p * LANES, (p + 1) * LANES)
            lhs = jnp.concatenate([stack2(At[:, sl]), stack2(Rt[:, sl])], axis=0).astype(BF16)
            rhs = jnp.concatenate([stack2(Bb[:, sl]), stack2(Kb[:, sl])], axis=0).astype(BF16)
            G = _dot_nt(lhs, rhs)
            a_ab = jnp.where(strict, G[0:2 * C, 0:2 * C], 0.0)
            a_ak = jnp.where(strict, G[0:2 * C, 2 * C:4 * C], 0.0)
            a_rb = jnp.where(incl, G[2 * C:4 * C, 0:2 * C], 0.0)
            a_rk = jnp.where(incl, G[2 * C:4 * C, 2 * C:4 * C], 0.0)
            S = state_ref[p]
            Z = _dot_nt(jnp.concatenate([At[:, sl], Rt[:, sl]], axis=0).astype(BF16), S.astype(BF16))
            v_bd = stack2(vc[:, sl])
            U = stack2(Z[0:C]) + _dot(a_ak.astype(BF16), v_bd.astype(BF16))
            X = a_ab
            n_lvl = int(math.log2(C))
            for lvl in range(n_lvl):
                if lvl < n_lvl - 1:
                    Wm = _dot(X.astype(BF16), jnp.concatenate([U, X], axis=1).astype(BF16))
                    U = U + Wm[:, 0:LANES]
                    X = Wm[:, LANES:2 * LANES]
                else:
                    U = U + _dot(X.astype(BF16), U.astype(BF16))
            y_bd = stack2(Z[C:2 * C]) + _dot(jnp.concatenate([a_rb, a_rk], axis=1).astype(BF16),
                                             jnp.concatenate([U, v_bd], axis=0).astype(BF16))
            ys.append(y_bd[0:C] + y_bd[C:2 * C])
            u_pair = U[0:C] + U[C:2 * C]
            upd = _dot_tn(jnp.concatenate([u_pair, vc[:, sl]], axis=0).astype(BF16),
                          jnp.concatenate([Bh[:, sl], Kh[:, sl]], axis=0).astype(BF16))
            state_ref[p] = S * wc[:, sl] + jnp.where(same, upd, 0.0)
        y = jnp.concatenate(ys, axis=1)
        mu = head_sum(y) * (1.0 / HEAD_DIM)
        yc = y - mu
        var = head_sum(yc * yc) * (1.0 / HEAD_DIM)
        yn = yc * lax.rsqrt(var + RWKV_GN_EPS) * ln_w + ln_b
        o_ref[rows, :] = ((yn + bonus[rows]) * g[rows]).astype(BF16)


def _rwkv(rw, vecs, w2, a2, g2, B, S, lb):
    nt = S // lb
    return pl.pallas_call(
        functools.partial(_rwkv_kernel, n_chunk=lb // RWKV_CHUNK),
        grid=(B, nt),
        in_specs=[pl.BlockSpec((lb, RWKV_PROJ), lambda b, j: (b * nt + j, 0)),
                  pl.BlockSpec((8, RWKV_WIDTH), lambda b, j: (0, 0)),
                  pl.BlockSpec((LANES, RWKV_WIDTH), lambda b, j: (0, 0)),
                  pl.BlockSpec((LANES, RWKV_WIDTH), lambda b, j: (0, 0)),
                  pl.BlockSpec((LANES, RWKV_WIDTH), lambda b, j: (0, 0))],
        out_specs=pl.BlockSpec((lb, RWKV_WIDTH), lambda b, j: (b * nt + j, 0)),
        out_shape=jax.ShapeDtypeStruct((B * S, RWKV_WIDTH), BF16),
        scratch_shapes=[pltpu.VMEM((RWKV_WIDTH // LANES, LANES, LANES), F32)],
        compiler_params=pltpu.CompilerParams(dimension_semantics=("arbitrary", "arbitrary"),
                                             vmem_limit_bytes=VMEM_LIMIT),
        name="rwkv",
    )(rw, vecs, w2, a2, g2)


def _mix_kernel(at_ref, rk_ref, x_ref, mod_ref, wo_a_ref, wo_r_ref, ln_ref, wr_ref, br_ref,
                x1_ref, h2_ref, ti_ref, gt_ref, rank_ref, cnt_ref, base_ref):
    first = (pl.program_id(0) == 0) & (pl.program_id(1) == 0)

    @pl.when(first)
    def _():
        base_ref[...] = jnp.zeros_like(base_ref)

    mod = mod_ref[0]
    y = _dot(at_ref[...], wo_a_ref[...]) + _dot(rk_ref[...], wo_r_ref[...])
    x = x_ref[...]
    tm = x.shape[0]
    x1 = _layer_norm(DEEPNORM_ALPHA * x + (1.0 + mod[2:3, :]) * y) * ln_ref[0:1, :] + ln_ref[1:2, :]
    h2 = _layer_norm(x1) * (1.0 + mod[4:5, :]) + mod[3:4, :]
    x1_ref[...] = x1
    h2_ref[...] = h2

    logits = jnp.dot(h2, wr_ref[...], preferred_element_type=F32,
                     precision=lax.Precision.HIGHEST) + br_ref[...]
    lt = jnp.transpose(logits)[0:N_EXPERTS, :]
    erow = lax.broadcasted_iota(jnp.int32, (N_EXPERTS, tm), 0).astype(F32)
    cur = lt
    vals, idxs = [], []
    for _ in range(TOP_K):
        m = jnp.max(cur, axis=0, keepdims=True)
        idx = jnp.min(jnp.where(cur == m, erow, float(N_EXPERTS)), axis=0, keepdims=True)
        vals.append(m)
        idxs.append(idx)
        cur = jnp.where(erow == idx, -jnp.inf, cur)
    tv = jnp.concatenate(vals, axis=0)
    e = jnp.exp(tv - tv[0:1, :])
    gt_ref[...] = e / jnp.sum(e, axis=0, keepdims=True)
    ti_ref[...] = jnp.concatenate(idxs, axis=0).astype(jnp.int32)

    onehot = jnp.zeros((N_EXPERTS, tm), F32)
    for idx in idxs:
        onehot = onehot + (erow == idx).astype(F32)
    before = (lax.broadcasted_iota(jnp.int32, (tm, tm), 0)
              < lax.broadcasted_iota(jnp.int32, (tm, tm), 1)).astype(BF16)
    tot = base_ref[:, 0:1] + _dot(onehot.astype(BF16), before)
    ranks = [jnp.sum(jnp.where(erow == idx, tot, 0.0), axis=0, keepdims=True) for idx in idxs]
    rank_ref[...] = jnp.concatenate(ranks, axis=0).astype(jnp.int32)
    base_ref[...] = base_ref[...] + jnp.sum(onehot, axis=1, keepdims=True)
    cnt_ref[...] = base_ref[...].astype(jnp.int32)


def _mix(attn_out, rwkv_out, x2d, mod, wo_a, wo_r, ln1, w_router, b_router, B, S, tm):
    D = x2d.shape[1]
    nt = S // tm
    T = B * S
    tok = lambda b, j: (b * nt + j, 0)
    col = lambda b, j: (0, b * nt + j)
    fixed = lambda b, j: (0, 0)
    return pl.pallas_call(
        _mix_kernel,
        grid=(B, nt),
        in_specs=[pl.BlockSpec((tm, ATTN_WIDTH), tok),
                  pl.BlockSpec((tm, RWKV_WIDTH), tok),
                  pl.BlockSpec((tm, D), tok),
                  pl.BlockSpec((1, 6, D), lambda b, j: (b, 0, 0)),
                  pl.BlockSpec((ATTN_WIDTH, D), fixed),
                  pl.BlockSpec((RWKV_WIDTH, D), fixed),
                  pl.BlockSpec((2, D), fixed),
                  pl.BlockSpec((D, LANES), fixed),
                  pl.BlockSpec((1, LANES), fixed)],
        out_specs=[pl.BlockSpec((tm, D), tok),
                   pl.BlockSpec((tm, D), tok),
                   pl.BlockSpec((TOP_K, tm), col),
                   pl.BlockSpec((TOP_K, tm), col),
                   pl.BlockSpec((TOP_K, tm), col),
                   pl.BlockSpec((N_EXPERTS, LANES), fixed)],
        out_shape=[jax.ShapeDtypeStruct((T, D), F32),
                   jax.ShapeDtypeStruct((T, D), F32),
                   jax.ShapeDtypeStruct((TOP_K, T), jnp.int32),
                   jax.ShapeDtypeStruct((TOP_K, T), F32),
                   jax.ShapeDtypeStruct((TOP_K, T), jnp.int32),
                   jax.ShapeDtypeStruct((N_EXPERTS, LANES), jnp.int32)],
        scratch_shapes=[pltpu.VMEM((N_EXPERTS, LANES), F32)],
        compiler_params=pltpu.CompilerParams(dimension_semantics=("arbitrary", "arbitrary"),
                                             vmem_limit_bytes=VMEM_LIMIT),
        name="mix",
    )(attn_out, rwkv_out, x2d, mod, wo_a, wo_r, ln1, w_router, b_router)


def _row_copy(src, src_row, dst, dst_row, sem):
    return pltpu.make_async_copy(src.at[pl.ds(src_row, 1)], dst.at[pl.ds(dst_row, 1)], sem)


def _dispatch_kernel(dest_ref, h2_ref, xs_in_ref, xs_ref, sem):
    del xs_in_ref
    tm = dest_ref.shape[1]
    base = pl.program_id(0) * tm

    def issue(t, carry):
        for k in range(TOP_K):
            _row_copy(h2_ref, base + t, xs_ref, dest_ref[k, t], sem).start()
        return carry

    def drain(t, carry):
        for k in range(TOP_K):
            _row_copy(h2_ref, base + t, xs_ref, dest_ref[k, t], sem).wait()
        return carry

    lax.fori_loop(0, tm, issue, 0)
    lax.fori_loop(0, tm, drain, 0)


def _dispatch(dest, h2, n_rows, tm):
    T, D = h2.shape
    xs0 = jnp.zeros((n_rows, D), h2.dtype)
    return pl.pallas_call(
        _dispatch_kernel,
        grid=(T // tm,),
        in_specs=[pl.BlockSpec((TOP_K, tm), lambda i: (0, i), memory_space=pltpu.SMEM),
                  pl.BlockSpec(memory_space=pl.ANY),
                  pl.BlockSpec(memory_space=pl.ANY)],
        out_specs=pl.BlockSpec(memory_space=pl.ANY),
        out_shape=jax.ShapeDtypeStruct((n_rows, D), h2.dtype),
        scratch_shapes=[pltpu.SemaphoreType.DMA(())],
        input_output_aliases={2: 0},
        compiler_params=pltpu.CompilerParams(dimension_semantics=("arbitrary",)),
        name="dispatch",
    )(dest, h2, xs0)


def _experts_kernel(blk_e_ref, n_used_ref, xs_ref, wgu_ref, bgu_ref, wd_ref, bd_ref, ys_ref):
    i = pl.program_id(0)
    d_ff = wd_ref.shape[1]

    @pl.when(i < n_used_ref[0])
    def _():
        xb = xs_ref[...].astype(BF16)
        gu = _dot(xb, wgu_ref[0]) + bgu_ref[0]
        gate = jnp.minimum(gu[:, :d_ff], SWIGLU_LIMIT)
        up = jnp.clip(gu[:, d_ff:], -SWIGLU_LIMIT, SWIGLU_LIMIT)
        act = (up + 1.0) * (gate * _sigmoid(SWIGLU_ALPHA * gate))
        ys_ref[...] = _dot(act.astype(BF16), wd_ref[0]) + bd_ref[0]

    @pl.when(i >= n_used_ref[0])
    def _():
        ys_ref[...] = jnp.zeros_like(ys_ref)


def _experts(blk_e, n_used, xs, wgu, bgu, wd, bd):
    n_rows, D = xs.shape
    n_blocks = n_rows // MOE_BLOCK
    d_ff = wd.shape[1]

    def row_map(i, blk_e_ref, n_used_ref):
        return (jnp.minimum(i, n_used_ref[0] - 1), 0)

    def exp_map(i, blk_e_ref, n_used_ref):
        return (blk_e_ref[jnp.minimum(i, n_used_ref[0] - 1)], 0, 0)

    grid_spec = pltpu.PrefetchScalarGridSpec(
        num_scalar_prefetch=2,
        grid=(n_blocks,),
        in_specs=[pl.BlockSpec((MOE_BLOCK, D), row_map),
                  pl.BlockSpec((1, D, 2 * d_ff), exp_map),
                  pl.BlockSpec((1, 1, 2 * d_ff), exp_map),
                  pl.BlockSpec((1, d_ff, D), exp_map),
                  pl.BlockSpec((1, 1, D), exp_map)],
        out_specs=pl.BlockSpec((MOE_BLOCK, D), lambda i, blk_e_ref, n_used_ref: (i, 0)),
    )
    return pl.pallas_call(
        _experts_kernel,
        grid_spec=grid_spec,
        out_shape=jax.ShapeDtypeStruct((n_rows, D), F32),
        compiler_params=pltpu.CompilerParams(dimension_semantics=("arbitrary",),
                                             vmem_limit_bytes=VMEM_LIMIT),
        name="experts",
    )(blk_e, n_used, xs, wgu, bgu, wd, bd)


def _combine_kernel(dest_ref, ys_ref, x1_ref, gt_ref, mod_ref, ln_ref, o_ref, buf, sem):
    tm = x1_ref.shape[0]

    def issue(t, carry):
        for k in range(TOP_K):
            _row_copy(ys_ref, dest_ref[k, t], buf.at[k], t, sem).start()
        return carry

    def drain(t, carry):
        for k in range(TOP_K):
            _row_copy(ys_ref, dest_ref[k, t], buf.at[k], t, sem).wait()
        return carry

    lax.fori_loop(0, tm, issue, 0)
    lax.fori_loop(0, tm, drain, 0)
    gt = gt_ref[...]
    y = gt[:, 0:1] * buf[0]
    for k in range(1, TOP_K):
        y = y + gt[:, k:k + 1] * buf[k]
    mod = mod_ref[0]
    z = DEEPNORM_ALPHA * x1_ref[...] + (1.0 + mod[5:6, :]) * y
    o_ref[...] = _layer_norm(z) * ln_ref[0:1, :] + ln_ref[1:2, :]


def _combine(dest, ys, x1, gates_t, mod, ln2, B, S, tm):
    T, D = x1.shape
    nt = S // tm
    tok = lambda b, j: (b * nt + j, 0)
    return pl.pallas_call(
        _combine_kernel,
        grid=(B, nt),
        in_specs=[pl.BlockSpec((TOP_K, tm), lambda b, j: (0, b * nt + j), memory_space=pltpu.SMEM),
                  pl.BlockSpec(memory_space=pl.ANY),
                  pl.BlockSpec((tm, D), tok),
                  pl.BlockSpec((tm, TOP_K), tok),
                  pl.BlockSpec((1, 6, D), lambda b, j: (b, 0, 0)),
                  pl.BlockSpec((2, D), lambda b, j: (0, 0))],
        out_specs=pl.BlockSpec((tm, D), tok),
        out_shape=jax.ShapeDtypeStruct((T, D), F32),
        scratch_shapes=[pltpu.VMEM((TOP_K, tm, D), F32), pltpu.SemaphoreType.DMA(())],
        compiler_params=pltpu.CompilerParams(dimension_semantics=("arbitrary", "arbitrary"),
                                             vmem_limit_bytes=VMEM_LIMIT),
        name="combine",
    )(dest, ys, x1, gates_t, mod, ln2)


def _pad_rows(w, rows):
    return jnp.pad(w, ((0, rows - w.shape[0]), (0, 0)))


def _pad_cols(w, cols):
    return jnp.pad(w, ((0, 0), (0, cols - w.shape[1])))


def _layer(x, c, positions, w_ada, b_ada, w_in, shift_mu, rwkv_w0, rwkv_w2, rwkv_a0, rwkv_a2, rwkv_g2,
           rwkv_k_k, rwkv_k_a, rwkv_r_k, rwkv_ln_w, rwkv_ln_b, attn_sinks, w_out, ln1_g, ln1_b,
           w_router, b_router, w_gate_up, b_gate_up, w_down, b_down, ln2_g, ln2_b):
    B, S, D = x.shape
    T = B * S
    tile = min(256, S)

    q0, k0, v0 = 0, ATTN_WIDTH, ATTN_WIDTH + KV_WIDTH
    r0 = ATTN_WIDTH + 2 * KV_WIDTH
    heads = lambda base: [w_in[:, base + h * HEAD_DIM: base + (h + 1) * HEAD_DIM] for h in range(N_KV_HEADS)]
    dup = lambda hs: [w for w in hs for _ in range(2)]
    w_attn = jnp.concatenate([w_in[:, q0:q0 + ATTN_WIDTH]] + dup(heads(k0)) + dup(heads(v0)), axis=1).astype(BF16)
    lora0 = r0 + 3 * RWKV_WIDTH
    lora = (DECAY_LORA, AAA_LORA, GATE_LORA)
    pieces_w = [w_in[:, r0:lora0]]
    pieces_mu = [shift_mu[None, 0:3 * RWKV_WIDTH]]
    off = lora0
    for n in lora:
        pieces_w.append(_pad_cols(w_in[:, off:off + n], LANES))
        pieces_mu.append(_pad_cols(shift_mu[None, off - r0:off - r0 + n], LANES))
        off += n
    w_rwkv = jnp.concatenate(pieces_w, axis=1).astype(BF16)
    mu = jnp.concatenate(pieces_mu, axis=1)
    inv_freq = ROPE_THETA ** (-jnp.arange(0, ROT_DIM, 2, dtype=F32) / ROT_DIM)
    lane_p = jnp.arange(LANES) % HEAD_DIM
    rot_tab = jnp.zeros((8, LANES), F32)
    rot_tab = rot_tab.at[0].set(jnp.where(lane_p < ROT_DIM, inv_freq[lane_p % (ROT_DIM // 2)], 0.0))
    rot_tab = rot_tab.at[1].set(jnp.where(lane_p < ROT_DIM // 2, -1.0, 0.0))
    rot_tab = rot_tab.at[2].set(jnp.where((lane_p >= ROT_DIM // 2) & (lane_p < ROT_DIM), 1.0, 0.0))
    vecs = jnp.stack([rwkv_w0, rwkv_a0, rwkv_k_k, rwkv_k_a, rwkv_r_k.reshape(-1), rwkv_ln_w, rwkv_ln_b,
                      jnp.zeros_like(rwkv_w0)])
    w2 = _pad_rows(rwkv_w2, LANES).astype(BF16)
    a2 = _pad_rows(rwkv_a2, LANES).astype(BF16)
    g2 = _pad_rows(rwkv_g2, LANES).astype(BF16)
    wo_a = w_out[:ATTN_WIDTH].astype(BF16)
    wo_r = w_out[ATTN_WIDTH:].astype(BF16)
    w_r = _pad_cols(w_router, LANES)
    b_r = jnp.concatenate([b_router, jnp.full((LANES - N_EXPERTS,), NEG_INF, F32)])[None, :]
    wgu = w_gate_up.astype(BF16)
    wd = w_down.astype(BF16)

    mod = _mod(c, w_ada, b_ada).reshape(B, 6, D)
    qkv, rw = _inproj(x, positions, mod, w_attn, w_rwkv, mu, rot_tab, tile)
    attn_out = _attention(qkv, attn_sinks, B, S)
    rwkv_out = _rwkv(rw, vecs, w2, a2, g2, B, S, min(128, S))

    x1, h2, top_i, gates, rank, cnt = _mix(attn_out, rwkv_out, x.reshape(T, D), mod, wo_a, wo_r,
                                            jnp.stack([ln1_g, ln1_b]), w_r, b_r, B, S, tile)

    counts = cnt[:, 0]
    padded = (counts + MOE_BLOCK - 1) // MOE_BLOCK * MOE_BLOCK
    pend = jnp.cumsum(padded)
    pstart = pend - padded
    dest = pstart[top_i] + rank
    n_blocks = T * TOP_K // MOE_BLOCK + N_EXPERTS
    blk_e = jnp.clip(jnp.searchsorted(pend, jnp.arange(n_blocks, dtype=jnp.int32) * MOE_BLOCK, side="right"),
                     0, N_EXPERTS - 1).astype(jnp.int32)
    n_used = (pend[-1:] // MOE_BLOCK).astype(jnp.int32)

    xs = _dispatch(dest, h2, n_blocks * MOE_BLOCK, tile)
    ys = _experts(blk_e, n_used, xs, wgu, b_gate_up[:, None, :], wd, b_down[:, None, :])
    out = _combine(dest, ys, x1, jnp.transpose(gates), mod, jnp.stack([ln2_g, ln2_b]), B, S, tile)
    return out.reshape(B, S, D)


def kernel(x, c, positions, w_ada, b_ada, w_in, shift_mu, rwkv_w0, rwkv_w2, rwkv_a0, rwkv_a2, rwkv_g2,
           rwkv_k_k, rwkv_k_a, rwkv_r_k, rwkv_ln_w, rwkv_ln_b, attn_sinks, w_out, ln1_g, ln1_b,
           w_router, b_router, w_gate_up, b_gate_up, w_down, b_down, ln2_g, ln2_b):
    for l in range(DEPTH):
        x = _layer(x, c, positions, w_ada[l], b_ada[l], w_in[l], shift_mu[l], rwkv_w0[l], rwkv_w2[l],
                   rwkv_a0[l], rwkv_a2[l], rwkv_g2[l], rwkv_k_k[l], rwkv_k_a[l], rwkv_r_k[l], rwkv_ln_w[l],
                   rwkv_ln_b[l], attn_sinks[l], w_out[l], ln1_g[l], ln1_b[l], w_router[l], b_router[l],
                   w_gate_up[l], b_gate_up[l], w_down[l], b_down[l], ln2_g[l], ln2_b[l])
    return x
```

```python
import functools
import math

import jax
import jax.numpy as jnp
from jax import lax
from jax.experimental import pallas as pl
from jax.experimental.pallas import tpu as pltpu

F32 = jnp.float32
BF16 = jnp.bfloat16

HEAD_DIM = 64
N_ATTN_HEADS = 8
N_KV_HEADS = 2
N_RWKV_HEADS = 8
ATTN_WIDTH = N_ATTN_HEADS * HEAD_DIM
KV_WIDTH = N_KV_HEADS * HEAD_DIM
RWKV_WIDTH = N_RWKV_HEADS * HEAD_DIM
ATTN_BLOCK = 128
ROT_DIM = HEAD_DIM // 4
ROPE_THETA = 500000.0
DECAY_LORA = 32
AAA_LORA = 32
GATE_LORA = 96
N_EXPERTS = 32
TOP_K = 4
SWIGLU_LIMIT = 7.0
SWIGLU_ALPHA = 1.702
LN_EPS = 1e-5
RWKV_GN_EPS = 64e-5
NEG_INF = -1e30
DEPTH = 1
DEEPNORM_ALPHA = (2 * DEPTH) ** 0.25

LANES = 128
RWKV_CHUNK = 64
MOE_BLOCK = 256
ATTN_PROJ = ATTN_WIDTH + 4 * KV_WIDTH
RWKV_PROJ = 3 * RWKV_WIDTH + 3 * LANES
VMEM_LIMIT = 48 * 1024 * 1024


def _dot(a, b):
    return jnp.dot(a, b, preferred_element_type=F32)


def _dot_nt(a, b):
    return lax.dot_general(a, b, (((1,), (1,)), ((), ())), preferred_element_type=F32)


def _dot_tn(a, b):
    return lax.dot_general(a, b, (((0,), (0,)), ((), ())), preferred_element_type=F32)


def _split3(x):
    h = x.astype(BF16)
    r1 = x - h.astype(F32)
    m = r1.astype(BF16)
    lo = (r1 - m.astype(F32)).astype(BF16)
    return h, m, lo


def _dot_exact_rhs(x, m_bf16):
    h, m, lo = _split3(x)
    return _dot(h, m_bf16) + _dot(m, m_bf16) + _dot(lo, m_bf16)


def _dot_exact_lhs(m_bf16, x):
    h, m, lo = _split3(x)
    return _dot(m_bf16, h) + _dot(m_bf16, m) + _dot(m_bf16, lo)


def _layer_norm(x):
    mu = jnp.mean(x, axis=-1, keepdims=True)
    xc = x - mu
    var = jnp.mean(xc * xc, axis=-1, keepdims=True)
    return xc * lax.rsqrt(var + LN_EPS)


def _sigmoid(x):
    return 1.0 / (1.0 + jnp.exp(-x))


def _mod_kernel(c_ref, w_ref, b_ref, o_ref):
    c = c_ref[...]
    s = c * _sigmoid(c)
    o_ref[...] = jnp.dot(s, w_ref[...], preferred_element_type=F32,
                         precision=lax.Precision.HIGHEST) + b_ref[...]


def _mod(c, w_ada, b_ada):
    B, D = c.shape
    n = w_ada.shape[1] // D
    return pl.pallas_call(
        _mod_kernel,
        grid=(n,),
        in_specs=[pl.BlockSpec((B, D), lambda i: (0, 0)),
                  pl.BlockSpec((D, D), lambda i: (0, i)),
                  pl.BlockSpec((1, D), lambda i: (0, i))],
        out_specs=pl.BlockSpec((B, D), lambda i: (0, i)),
        out_shape=jax.ShapeDtypeStruct((B, n * D), F32),
        compiler_params=pltpu.CompilerParams(dimension_semantics=("arbitrary",),
                                             vmem_limit_bytes=VMEM_LIMIT),
        name="mod",
    )(c, w_ada, b_ada.reshape(1, -1))


def _inproj_kernel(x_ref, pos_ref, mod_ref, wa_ref, wr_ref, mu_ref, rt_ref,
                   qkv_ref, rw_ref, carry_ref):
    j = pl.program_id(1)
    x = x_ref[0]
    tm = x.shape[0]
    mod = mod_ref[0]
    h = _layer_norm(x) * (1.0 + mod[1:2, :]) + mod[0:1, :]
    hb = h.astype(BF16)

    pa = _dot(hb, wa_ref[...])
    ang = pos_ref[0].astype(F32) * rt_ref[0:1, :]
    cs = jnp.cos(ang)
    sn = jnp.sin(ang)
    m_lo = rt_ref[1:2, :]
    m_hi = rt_ref[2:3, :]
    n_q = ATTN_WIDTH // LANES
    n_rot = (ATTN_WIDTH + 2 * KV_WIDTH) // LANES
    for ch in range(n_rot):
        t = pa[:, ch * LANES:(ch + 1) * LANES]
        if ch < n_q:
            t = t * (1.0 / math.sqrt(HEAD_DIM))
        up = pltpu.roll(t, LANES - ROT_DIM // 2, 1)
        dn = pltpu.roll(t, ROT_DIM // 2, 1)
        o = t * cs + sn * (m_lo * up + m_hi * dn)
        qkv_ref[:, ch * LANES:(ch + 1) * LANES] = o.astype(BF16)
    qkv_ref[:, n_rot * LANES:] = pa[:, n_rot * LANES:].astype(BF16)

    pr = _dot(hb, wr_ref[...])
    prev = pltpu.roll(pr, 1, 0)
    row = lax.broadcasted_iota(jnp.int32, (tm, 1), 0)
    carry = jnp.where(j == 0, 0.0, carry_ref[...])
    prev = jnp.where(row == 0, carry, prev)
    carry_ref[...] = pr[tm - 1:tm, :]
    rw_ref[...] = pr + (prev - pr) * mu_ref[...]


def _inproj(x, positions, mod, w_attn, w_rwkv, mu, rot_tab, tm):
    B, S, D = x.shape
    nt = S // tm
    return pl.pallas_call(
        _inproj_kernel,
        grid=(B, nt),
        in_specs=[pl.BlockSpec((1, tm, D), lambda b, j: (b, j, 0)),
                  pl.BlockSpec((1, tm, 1), lambda b, j: (b, j, 0)),
                  pl.BlockSpec((1, 6, D), lambda b, j: (b, 0, 0)),
                  pl.BlockSpec((D, ATTN_PROJ), lambda b, j: (0, 0)),
                  pl.BlockSpec((D, RWKV_PROJ), lambda b, j: (0, 0)),
                  pl.BlockSpec((1, RWKV_PROJ), lambda b, j: (0, 0)),
                  pl.BlockSpec((8, LANES), lambda b, j: (0, 0))],
        out_specs=[pl.BlockSpec((tm, ATTN_PROJ), lambda b, j: (b * nt + j, 0)),
                   pl.BlockSpec((tm, RWKV_PROJ), lambda b, j: (b * nt + j, 0))],
        out_shape=[jax.ShapeDtypeStruct((B * S, ATTN_PROJ), BF16),
                   jax.ShapeDtypeStruct((B * S, RWKV_PROJ), F32)],
        scratch_shapes=[pltpu.VMEM((1, RWKV_PROJ), F32)],
        compiler_params=pltpu.CompilerParams(dimension_semantics=("arbitrary", "arbitrary"),
                                             vmem_limit_bytes=VMEM_LIMIT),
        name="inproj",
    )(x, positions.reshape(B, S, 1), mod, w_attn, w_rwkv, mu, rot_tab)


def _attn_kernel(q_ref, kc_ref, kp_ref, vc_ref, vp_ref, sink_ref, o_ref):
    i = pl.program_id(1)
    blk = ATTN_BLOCK
    qi = lax.broadcasted_iota(jnp.int32, (blk, 2 * blk), 0)
    kj = lax.broadcasted_iota(jnp.int32, (blk, 2 * blk), 1)
    allowed = (kj > qi) & (kj <= qi + blk) & ((kj >= blk) | (i > 0))
    lane = lax.broadcasted_iota(jnp.int32, (1, LANES), 1)
    lo = (lane < HEAD_DIM).astype(BF16)
    hi = (lane >= HEAD_DIM).astype(BF16)
    for g in range(N_KV_HEADS):
        sl = slice(g * LANES, (g + 1) * LANES)
        kcat = jnp.concatenate([kp_ref[:, sl], kc_ref[:, sl]], axis=0)
        vcat = jnp.concatenate([vp_ref[:, sl], vc_ref[:, sl]], axis=0)
        halves = ((kcat * lo, vcat * lo), (kcat * hi, vcat * hi))
        for jj in range(2):
            c = 2 * g + jj
            qc = q_ref[:, c * LANES:(c + 1) * LANES]
            acc = jnp.zeros((blk, LANES), F32)
            for half in range(2):
                kk, vv = halves[half]
                sink = sink_ref[2 * c + half]
                s = _dot_nt(qc, kk)
                s = jnp.where(allowed, s, NEG_INF)
                m = jnp.maximum(jnp.max(s, axis=-1, keepdims=True), sink)
                p = jnp.exp(s - m)
                denom = jnp.sum(p, axis=-1, keepdims=True) + jnp.exp(sink - m)
                acc = acc + _dot(p.astype(BF16), vv) / denom
            o_ref[:, c * LANES:(c + 1) * LANES] = acc.astype(BF16)


def _attention(qkv, sinks, B, S):
    nb = S // ATTN_BLOCK
    blk = ATTN_BLOCK
    kcol = ATTN_WIDTH // (2 * KV_WIDTH)
    vcol = kcol + 1
    cur = lambda col: (lambda b, i: (b * nb + i, col))
    prv = lambda col: (lambda b, i: (jnp.maximum(b * nb + i - 1, 0), col))
    return pl.pallas_call(
        _attn_kernel,
        grid=(B, nb),
        in_specs=[pl.BlockSpec((blk, ATTN_WIDTH), lambda b, i: (b * nb + i, 0)),
                  pl.BlockSpec((blk, 2 * KV_WIDTH), cur(kcol)),
                  pl.BlockSpec((blk, 2 * KV_WIDTH), prv(kcol)),
                  pl.BlockSpec((blk, 2 * KV_WIDTH), cur(vcol)),
                  pl.BlockSpec((blk, 2 * KV_WIDTH), prv(vcol)),
                  pl.BlockSpec(memory_space=pltpu.SMEM)],
        out_specs=pl.BlockSpec((blk, ATTN_WIDTH), lambda b, i: (b * nb + i, 0)),
        out_shape=jax.ShapeDtypeStruct((B * S, ATTN_WIDTH), BF16),
        compiler_params=pltpu.CompilerParams(dimension_semantics=("arbitrary", "arbitrary"),
                                             vmem_limit_bytes=VMEM_LIMIT),
        name="attn",
    )(qkv, qkv, qkv, qkv, qkv, sinks)


def _rwkv_kernel(rw_ref, vec_ref, w2_ref, a2_ref, g2_ref, o_ref, state_ref, *, n_chunk):
    j = pl.program_id(1)
    C = RWKV_CHUNK
    W = RWKV_WIDTH
    n_pair = W // LANES

    @pl.when(j == 0)
    def _():
        state_ref[...] = jnp.zeros_like(state_ref)

    w0 = vec_ref[0:1, :]
    a0 = vec_ref[1:2, :]
    k_k = vec_ref[2:3, :]
    k_a = vec_ref[3:4, :]
    r_k = vec_ref[4:5, :]
    ln_w = vec_ref[5:6, :]
    ln_b = vec_ref[6:7, :]

    r = rw_ref[:, 0:W]
    k = rw_ref[:, W:2 * W]
    v = rw_ref[:, 2 * W:3 * W]
    wl = rw_ref[:, 3 * W:3 * W + LANES]
    al = rw_ref[:, 3 * W + LANES:3 * W + 2 * LANES]
    gl = rw_ref[:, 3 * W + 2 * LANES:3 * W + 3 * LANES]

    ri = lax.broadcasted_iota(jnp.int32, (LANES, LANES), 0)
    ci = lax.broadcasted_iota(jnp.int32, (LANES, LANES), 1)
    same = (ri // HEAD_DIM) == (ci // HEAD_DIM)
    strict = same & ((ri % HEAD_DIM) > (ci % HEAD_DIM))
    incl = same & ((ri % HEAD_DIM) >= (ci % HEAD_DIM))
    ones_bd = same.astype(BF16)
    lane = lax.broadcasted_iota(jnp.int32, (1, LANES), 1)
    m0 = (lane < HEAD_DIM).astype(F32)
    m1 = 1.0 - m0
    tri = (lax.broadcasted_iota(jnp.int32, (C, C), 0) >= lax.broadcasted_iota(jnp.int32, (C, C), 1)).astype(BF16)

    def head_sum(xv):
        return jnp.concatenate(
            [_dot_exact_rhs(xv[:, p * LANES:(p + 1) * LANES], ones_bd) for p in range(n_pair)], axis=1)

    def stack2(xp):
        return jnp.concatenate([xp * m0, xp * m1], axis=0)

    z = w0 + _dot(jnp.tanh(wl).astype(BF16), w2_ref[...])
    lw = -math.exp(-0.5) * _sigmoid(z)
    a = _sigmoid(a0 + _dot(al.astype(BF16), a2_ref[...]))
    g = _dot(_sigmoid(gl).astype(BF16), g2_ref[...])
    kk = k * k_k
    kkn = kk / jnp.maximum(jnp.sqrt(head_sum(kk * kk)), 1e-12)
    k2 = k * (1.0 + (a - 1.0) * k_a)
    av = -kkn
    bv = kkn * a
    bonus = head_sum(r * k2 * r_k) * v

    for c in range(n_chunk):
        rows = slice(c * C, (c + 1) * C)
        lwc = lw[rows]
        cw = _dot_exact_lhs(tri, lwc)
        cwl = cw[C - 1:C, :]
        e_in = jnp.exp(cw)
        e_neg = jnp.exp(-cw)
        e_rem = jnp.exp(cwl - cw)
        wc = jnp.exp(cwl)
        Rt = r[rows] * e_in
        At = av[rows] * jnp.exp(cw - lwc)
        Bb = bv[rows] * e_neg
        Kb = k2[rows] * e_neg
        Bh = bv[rows] * e_rem
        Kh = k2[rows] * e_rem
        vc = v[rows]
        ys = []
        for p in range(n_pair):
            sl = slice(p * LANES, (p + 1) * LANES)
            lhs = jnp.concatenate([stack2(At[:, sl]), stack2(Rt[:, sl])], axis=0).astype(BF16)
            rhs = jnp.concatenate([stack2(Bb[:, sl]), stack2(Kb[:, sl])], axis=0).astype(BF16)
            G = _dot_nt(lhs, rhs)
            a_ab = jnp.where(strict, G[0:2 * C, 0:2 * C], 0.0)
            a_ak = jnp.where(strict, G[0:2 * C, 2 * C:4 * C], 0.0)
            a_rb = jnp.where(incl, G[2 * C:4 * C, 0:2 * C], 0.0)
            a_rk = jnp.where(incl, G[2 * C:4 * C, 2 * C:4 * C], 0.0)
            S = state_ref[p]
            Z = _dot_nt(jnp.concatenate([At[:, sl], Rt[:, sl]], axis=0).astype(BF16), S.astype(BF16))
            v_bd = stack2(vc[:, sl])
            U = stack2(Z[0:C]) + _dot(a_ak.astype(BF16), v_bd.astype(BF16))
            X = a_ab
            n_lvl = int(math.log2(C))
            for lvl in range(n_lvl):
                if lvl < n_lvl - 1:
                    Wm = _dot(X.astype(BF16), jnp.concatenate([U, X], axis=1).astype(BF16))
                    U = U + Wm[:, 0:LANES]
                    X = Wm[:, LANES:2 * LANES]
                else:
                    U = U + _dot(X.astype(BF16), U.astype(BF16))
            y_bd = stack2(Z[C:2 * C]) + _dot(jnp.concatenate([a_rb, a_rk], axis=1).astype(BF16),
                                             jnp.concatenate([U, v_bd], axis=0).astype(BF16))
            ys.append(y_bd[0:C] + y_bd[C:2 * C])
            u_pair = U[0:C] + U[C:2 * C]
            upd = _dot_tn(jnp.concatenate([u_pair, vc[:, sl]], axis=0).astype(BF16),
                          jnp.concatenate([Bh[:, sl], Kh[:, sl]], axis=0).astype(BF16))
            state_ref[p] = S * wc[:, sl] + jnp.where(same, upd, 0.0)
        y = jnp.concatenate(ys, axis=1)
        mu = head_sum(y) * (1.0 / HEAD_DIM)
        yc = y - mu
        var = head_sum(yc * yc) * (1.0 / HEAD_DIM)
        yn = yc * lax.rsqrt(var + RWKV_GN_EPS) * ln_w + ln_b
        o_ref[rows, :] = ((yn + bonus[rows]) * g[rows]).astype(BF16)


def _rwkv(rw, vecs, w2, a2, g2, B, S, lb):
    nt = S // lb
    return pl.pallas_call(
        functools.partial(_rwkv_kernel, n_chunk=lb // RWKV_CHUNK),
        grid=(B, nt),
        in_specs=[pl.BlockSpec((lb, RWKV_PROJ), lambda b, j: (b * nt + j, 0)),
                  pl.BlockSpec((8, RWKV_WIDTH), lambda b, j: (0, 0)),
                  pl.BlockSpec((LANES, RWKV_WIDTH), lambda b, j: (0, 0)),
                  pl.BlockSpec((LANES, RWKV_WIDTH), lambda b, j: (0, 0)),
                  pl.BlockSpec((LANES, RWKV_WIDTH), lambda b, j: (0, 0))],
        out_specs=pl.BlockSpec((lb, RWKV_WIDTH), lambda b, j: (b * nt + j, 0)),
        out_shape=jax.ShapeDtypeStruct((B * S, RWKV_WIDTH), BF16),
        scratch_shapes=[pltpu.VMEM((RWKV_WIDTH // LANES, LANES, LANES), F32)],
        compiler_params=pltpu.CompilerParams(dimension_semantics=("arbitrary", "arbitrary"),
                                             vmem_limit_bytes=VMEM_LIMIT),
        name="rwkv",
    )(rw, vecs, w2, a2, g2)


def _mix_kernel(at_ref, rk_ref, x_ref, mod_ref, wo_a_ref, wo_r_ref, ln_ref, wr_ref, br_ref,
                x1_ref, h2_ref, ti_ref, gt_ref, rank_ref, cnt_ref, base_ref):
    first = (pl.program_id(0) == 0) & (pl.program_id(1) == 0)

    @pl.when(first)
    def _():
        base_ref[...] = jnp.zeros_like(base_ref)

    mod = mod_ref[0]
    y = _dot(at_ref[...], wo_a_ref[...]) + _dot(rk_ref[...], wo_r_ref[...])
    x = x_ref[...]
    tm = x.shape[0]
    x1 = _layer_norm(DEEPNORM_ALPHA * x + (1.0 + mod[2:3, :]) * y) * ln_ref[0:1, :] + ln_ref[1:2, :]
    h2 = _layer_norm(x1) * (1.0 + mod[4:5, :]) + mod[3:4, :]
    x1_ref[...] = x1
    h2_ref[...] = h2

    logits = jnp.dot(h2, wr_ref[...], preferred_element_type=F32,
                     precision=lax.Precision.HIGHEST) + br_ref[...]
    lt = jnp.transpose(logits)[0:N_EXPERTS, :]
    erow = lax.broadcasted_iota(jnp.int32, (N_EXPERTS, tm), 0).astype(F32)
    cur = lt
    vals, idxs = [], []
    for _ in range(TOP_K):
        m = jnp.max(cur, axis=0, keepdims=True)
        idx = jnp.min(jnp.where(cur == m, erow, float(N_EXPERTS)), axis=0, keepdims=True)
        vals.append(m)
        idxs.append(idx)
        cur = jnp.where(erow == idx, -jnp.inf, cur)
    tv = jnp.concatenate(vals, axis=0)
    e = jnp.exp(tv - tv[0:1, :])
    gt_ref[...] = e / jnp.sum(e, axis=0, keepdims=True)
    ti_ref[...] = jnp.concatenate(idxs, axis=0).astype(jnp.int32)

    onehot = jnp.zeros((N_EXPERTS, tm), F32)
    for idx in idxs:
        onehot = onehot + (erow == idx).astype(F32)
    before = (lax.broadcasted_iota(jnp.int32, (tm, tm), 0)
              < lax.broadcasted_iota(jnp.int32, (tm, tm), 1)).astype(BF16)
    tot = base_ref[:, 0:1] + _dot(onehot.astype(BF16), before)
    ranks = [jnp.sum(jnp.where(erow == idx, tot, 0.0), axis=0, keepdims=True) for idx in idxs]
    rank_ref[...] = jnp.concatenate(ranks, axis=0).astype(jnp.int32)
    base_ref[...] = base_ref[...] + jnp.sum(onehot, axis=1, keepdims=True)
    cnt_ref[...] = base_ref[...].astype(jnp.int32)


def _mix(attn_out, rwkv_out, x2d, mod, wo_a, wo_r, ln1, w_router, b_router, B, S, tm):
    D = x2d.shape[1]
    nt = S // tm
    T = B * S
    tok = lambda b, j: (b * nt + j, 0)
    col = lambda b, j: (0, b * nt + j)
    fixed = lambda b, j: (0, 0)
    return pl.pallas_call(
        _mix_kernel,
        grid=(B, nt),
        in_specs=[pl.BlockSpec((tm, ATTN_WIDTH), tok),
                  pl.BlockSpec((tm, RWKV_WIDTH), tok),
                  pl.BlockSpec((tm, D), tok),
                  pl.BlockSpec((1, 6, D), lambda b, j: (b, 0, 0)),
                  pl.BlockSpec((ATTN_WIDTH, D), fixed),
                  pl.BlockSpec((RWKV_WIDTH, D), fixed),
                  pl.BlockSpec((2, D), fixed),
                  pl.BlockSpec((D, LANES), fixed),
                  pl.BlockSpec((1, LANES), fixed)],
        out_specs=[pl.BlockSpec((tm, D), tok),
                   pl.BlockSpec((tm, D), tok),
                   pl.BlockSpec((TOP_K, tm), col),
                   pl.BlockSpec((TOP_K, tm), col),
                   pl.BlockSpec((TOP_K, tm), col),
                   pl.BlockSpec((N_EXPERTS, LANES), fixed)],
        out_shape=[jax.ShapeDtypeStruct((T, D), F32),
                   jax.ShapeDtypeStruct((T, D), F32),
                   jax.ShapeDtypeStruct((TOP_K, T), jnp.int32),
                   jax.ShapeDtypeStruct((TOP_K, T), F32),
                   jax.ShapeDtypeStruct((TOP_K, T), jnp.int32),
                   jax.ShapeDtypeStruct((N_EXPERTS, LANES), jnp.int32)],
        scratch_shapes=[pltpu.VMEM((N_EXPERTS, LANES), F32)],
        compiler_params=pltpu.CompilerParams(dimension_semantics=("arbitrary", "arbitrary"),
                                             vmem_limit_bytes=VMEM_LIMIT),
        name="mix",
    )(attn_out, rwkv_out, x2d, mod, wo_a, wo_r, ln1, w_router, b_router)


def _row_copy(src, src_row, dst, dst_row, sem):
    return pltpu.make_async_copy(src.at[pl.ds(src_row, 1)], dst.at[pl.ds(dst_row, 1)], sem)


def _dispatch_kernel(dest_ref, h2_ref, xs_in_ref, xs_ref, sem):
    del xs_in_ref
    tm = dest_ref.shape[1]

    def issue(t, carry):
        for k in range(TOP_K):
            _row_copy(h2_ref, t, xs_ref, dest_ref[k, t], sem).start()
        return carry

    def drain(t, carry):
        for k in range(TOP_K):
            _row_copy(h2_ref, t, xs_ref, dest_ref[k, t], sem).wait()
        return carry

    lax.fori_loop(0, tm, issue, 0)
    lax.fori_loop(0, tm, drain, 0)


def _dispatch(dest, h2, n_rows, tm):
    T, D = h2.shape
    xs0 = jnp.zeros((n_rows, D), h2.dtype)
    return pl.pallas_call(
        _dispatch_kernel,
        grid=(T // tm,),
        in_specs=[pl.BlockSpec((TOP_K, tm), lambda i: (0, i), memory_space=pltpu.SMEM),
                  pl.BlockSpec((tm, D), lambda i: (i, 0)),
                  pl.BlockSpec(memory_space=pl.ANY)],
        out_specs=pl.BlockSpec(memory_space=pl.ANY),
        out_shape=jax.ShapeDtypeStruct((n_rows, D), h2.dtype),
        scratch_shapes=[pltpu.SemaphoreType.DMA(())],
        input_output_aliases={2: 0},
        compiler_params=pltpu.CompilerParams(dimension_semantics=("arbitrary",),
                                             vmem_limit_bytes=VMEM_LIMIT),
        name="dispatch",
    )(dest, h2, xs0)


def _experts_kernel(blk_e_ref, n_used_ref, xs_ref, wgu_ref, bgu_ref, wd_ref, bd_ref, ys_ref):
    i = pl.program_id(0)
    d_ff = wd_ref.shape[1]

    @pl.when(i < n_used_ref[0])
    def _():
        xb = xs_ref[...].astype(BF16)
        gu = _dot(xb, wgu_ref[0]) + bgu_ref[0]
        gate = jnp.minimum(gu[:, :d_ff], SWIGLU_LIMIT)
        up = jnp.clip(gu[:, d_ff:], -SWIGLU_LIMIT, SWIGLU_LIMIT)
        act = (up + 1.0) * (gate * _sigmoid(SWIGLU_ALPHA * gate))
        ys_ref[...] = _dot(act.astype(BF16), wd_ref[0]) + bd_ref[0]

    @pl.when(i >= n_used_ref[0])
    def _():
        ys_ref[...] = jnp.zeros_like(ys_ref)


def _experts(blk_e, n_used, xs, wgu, bgu, wd, bd):
    n_rows, D = xs.shape
    n_blocks = n_rows // MOE_BLOCK
    d_ff = wd.shape[1]

    def row_map(i, blk_e_ref, n_used_ref):
        return (jnp.minimum(i, n_used_ref[0] - 1), 0)

    def exp_map(i, blk_e_ref, n_used_ref):
        return (blk_e_ref[jnp.minimum(i, n_used_ref[0] - 1)], 0, 0)

    grid_spec = pltpu.PrefetchScalarGridSpec(
        num_scalar_prefetch=2,
        grid=(n_blocks,),
        in_specs=[pl.BlockSpec((MOE_BLOCK, D), row_map),
                  pl.BlockSpec((1, D, 2 * d_ff), exp_map),
                  pl.BlockSpec((1, 1, 2 * d_ff), exp_map),
                  pl.BlockSpec((1, d_ff, D), exp_map),
                  pl.BlockSpec((1, 1, D), exp_map)],
        out_specs=pl.BlockSpec((MOE_BLOCK, D), lambda i, blk_e_ref, n_used_ref: (i, 0)),
    )
    return pl.pallas_call(
        _experts_kernel,
        grid_spec=grid_spec,
        out_shape=jax.ShapeDtypeStruct((n_rows, D), F32),
        compiler_params=pltpu.CompilerParams(dimension_semantics=("arbitrary",),
                                             vmem_limit_bytes=VMEM_LIMIT),
        name="experts",
    )(blk_e, n_used, xs, wgu, bgu, wd, bd)


def _combine_kernel(dest_ref, ys_ref, x1_ref, gt_ref, mod_ref, ln_ref, o_ref, buf, sem):
    tm = x1_ref.shape[0]

    def issue(t, carry):
        for k in range(TOP_K):
            _row_copy(ys_ref, dest_ref[k, t], buf.at[k], t, sem).start()
        return carry

    def drain(t, carry):
        for k in range(TOP_K):
            _row_copy(ys_ref, dest_ref[k, t], buf.at[k], t, sem).wait()
        return carry

    lax.fori_loop(0, tm, issue, 0)
    lax.fori_loop(0, tm, drain, 0)
    gt = gt_ref[...]
    y = gt[:, 0:1] * buf[0]
    for k in range(1, TOP_K):
        y = y + gt[:, k:k + 1] * buf[k]
    mod = mod_ref[0]
    z = DEEPNORM_ALPHA * x1_ref[...] + (1.0 + mod[5:6, :]) * y
    o_ref[...] = _layer_norm(z) * ln_ref[0:1, :] + ln_ref[1:2, :]


def _combine(dest, ys, x1, gates_t, mod, ln2, B, S, tm):
    T, D = x1.shape
    nt = S // tm
    tok = lambda b, j: (b * nt + j, 0)
    return pl.pallas_call(
        _combine_kernel,
        grid=(B, nt),
        in_specs=[pl.BlockSpec((TOP_K, tm), lambda b, j: (0, b * nt + j), memory_space=pltpu.SMEM),
                  pl.BlockSpec(memory_space=pl.ANY),
                  pl.BlockSpec((tm, D), tok),
                  pl.BlockSpec((tm, TOP_K), tok),
                  pl.BlockSpec((1, 6, D), lambda b, j: (b, 0, 0)),
                  pl.BlockSpec((2, D), lambda b, j: (0, 0))],
        out_specs=pl.BlockSpec((tm, D), tok),
        out_shape=jax.ShapeDtypeStruct((T, D), F32),
        scratch_shapes=[pltpu.VMEM((TOP_K, tm, D), F32), pltpu.SemaphoreType.DMA(())],
        compiler_params=pltpu.CompilerParams(dimension_semantics=("arbitrary", "arbitrary"),
                                             vmem_limit_bytes=VMEM_LIMIT),
        name="combine",
    )(dest, ys, x1, gates_t, mod, ln2)


def _pad_rows(w, rows):
    return jnp.pad(w, ((0, rows - w.shape[0]), (0, 0)))


def _pad_cols(w, cols):
    return jnp.pad(w, ((0, 0), (0, cols - w.shape[1])))


def _layer(x, c, positions, w_ada, b_ada, w_in, shift_mu, rwkv_w0, rwkv_w2, rwkv_a0, rwkv_a2, rwkv_g2,
           rwkv_k_k, rwkv_k_a, rwkv_r_k, rwkv_ln_w, rwkv_ln_b, attn_sinks, w_out, ln1_g, ln1_b,
           w_router, b_router, w_gate_up, b_gate_up, w_down, b_down, ln2_g, ln2_b):
    B, S, D = x.shape
    T = B * S
    tile = min(256, S)

    q0, k0, v0 = 0, ATTN_WIDTH, ATTN_WIDTH + KV_WIDTH
    r0 = ATTN_WIDTH + 2 * KV_WIDTH
    heads = lambda base: [w_in[:, base + h * HEAD_DIM: base + (h + 1) * HEAD_DIM] for h in range(N_KV_HEADS)]
    dup = lambda hs: [w for w in hs for _ in range(2)]
    w_attn = jnp.concatenate([w_in[:, q0:q0 + ATTN_WIDTH]] + dup(heads(k0)) + dup(heads(v0)), axis=1).astype(BF16)
    lora0 = r0 + 3 * RWKV_WIDTH
    lora = (DECAY_LORA, AAA_LORA, GATE_LORA)
    pieces_w = [w_in[:, r0:lora0]]
    pieces_mu = [shift_mu[None, 0:3 * RWKV_WIDTH]]
    off = lora0
    for n in lora:
        pieces_w.append(_pad_cols(w_in[:, off:off + n], LANES))
        pieces_mu.append(_pad_cols(shift_mu[None, off - r0:off - r0 + n], LANES))
        off += n
    w_rwkv = jnp.concatenate(pieces_w, axis=1).astype(BF16)
    mu = jnp.concatenate(pieces_mu, axis=1)
    inv_freq = ROPE_THETA ** (-jnp.arange(0, ROT_DIM, 2, dtype=F32) / ROT_DIM)
    lane_p = jnp.arange(LANES) % HEAD_DIM
    rot_tab = jnp.zeros((8, LANES), F32)
    rot_tab = rot_tab.at[0].set(jnp.where(lane_p < ROT_DIM, inv_freq[lane_p % (ROT_DIM // 2)], 0.0))
    rot_tab = rot_tab.at[1].set(jnp.where(lane_p < ROT_DIM // 2, -1.0, 0.0))
    rot_tab = rot_tab.at[2].set(jnp.where((lane_p >= ROT_DIM // 2) & (lane_p < ROT_DIM), 1.0, 0.0))
    vecs = jnp.stack([rwkv_w0, rwkv_a0, rwkv_k_k, rwkv_k_a, rwkv_r_k.reshape(-1), rwkv_ln_w, rwkv_ln_b,
                      jnp.zeros_like(rwkv_w0)])
    w2 = _pad_rows(rwkv_w2, LANES).astype(BF16)
    a2 = _pad_rows(rwkv_a2, LANES).astype(BF16)
    g2 = _pad_rows(rwkv_g2, LANES).astype(BF16)
    wo_a = w_out[:ATTN_WIDTH].astype(BF16)
    wo_r = w_out[ATTN_WIDTH:].astype(BF16)
    w_r = _pad_cols(w_router, LANES)
    b_r = jnp.concatenate([b_router, jnp.full((LANES - N_EXPERTS,), NEG_INF, F32)])[None, :]
    wgu = w_gate_up.astype(BF16)
    wd = w_down.astype(BF16)

    mod = _mod(c, w_ada, b_ada).reshape(B, 6, D)
    qkv, rw = _inproj(x, positions, mod, w_attn, w_rwkv, mu, rot_tab, tile)
    attn_out = _attention(qkv, attn_sinks, B, S)
    rwkv_out = _rwkv(rw, vecs, w2, a2, g2, B, S, min(128, S))

    x1, h2, top_i, gates, rank, cnt = _mix(attn_out, rwkv_out, x.reshape(T, D), mod, wo_a, wo_r,
                                            jnp.stack([ln1_g, ln1_b]), w_r, b_r, B, S, tile)

    counts = cnt[:, 0]
    padded = (counts + MOE_BLOCK - 1) // MOE_BLOCK * MOE_BLOCK
    pend = jnp.cumsum(padded)
    pstart = pend - padded
    experts = jnp.arange(N_EXPERTS, dtype=jnp.int32)
    dest = rank + jnp.sum(jnp.where(top_i[None] == experts[:, None, None], pstart[:, None, None], 0), axis=0)
    n_blocks = T * TOP_K // MOE_BLOCK + N_EXPERTS
    blk_row = jnp.arange(n_blocks, dtype=jnp.int32) * MOE_BLOCK
    blk_e = jnp.minimum(jnp.sum((pend[None, :] <= blk_row[:, None]).astype(jnp.int32), axis=1), N_EXPERTS - 1)
    n_used = (pend[-1:] // MOE_BLOCK).astype(jnp.int32)

    xs = _dispatch(dest, h2, n_blocks * MOE_BLOCK, tile)
    ys = _experts(blk_e, n_used, xs, wgu, b_gate_up[:, None, :], wd, b_down[:, None, :])
    out = _combine(dest, ys, x1, jnp.transpose(gates), mod, jnp.stack([ln2_g, ln2_b]), B, S, tile)
    return out.reshape(B, S, D)


def kernel(x, c, positions, w_ada, b_ada, w_in, shift_mu, rwkv_w0, rwkv_w2, rwkv_a0, rwkv_a2, rwkv_g2,
           rwkv_k_k, rwkv_k_a, rwkv_r_k, rwkv_ln_w, rwkv_ln_b, attn_sinks, w_out, ln1_g, ln1_b,
           w_router, b_router, w_gate_up, b_gate_up, w_down, b_down, ln2_g, ln2_b):
    for l in range(DEPTH):
        x = _layer(x, c, positions, w_ada[l], b_ada[l], w_in[l], shift_mu[l], rwkv_w0[l], rwkv_w2[l],
                   rwkv_a0[l], rwkv_a2[l], rwkv_g2[l], rwkv_k_k[l], rwkv_k_a[l], rwkv_r_k[l], rwkv_ln_w[l],
                   rwkv_ln_b[l], attn_sinks[l], w_out[l], ln1_g[l], ln1_b[l], w_router[l], b_router[l],
                   w_gate_up[l], b_gate_up[l], w_down[l], b_down[l], ln2_g[l], ln2_b[l])
    return x
```

```python
import functools
import math

import jax
import jax.numpy as jnp
from jax import lax
from jax.experimental import pallas as pl
from jax.experimental.pallas import tpu as pltpu

F32 = jnp.float32
BF16 = jnp.bfloat16

HEAD_DIM = 64
N_ATTN_HEADS = 8
N_KV_HEADS = 2
N_RWKV_HEADS = 8
ATTN_WIDTH = N_ATTN_HEADS * HEAD_DIM
KV_WIDTH = N_KV_HEADS * HEAD_DIM
RWKV_WIDTH = N_RWKV_HEADS * HEAD_DIM
ATTN_BLOCK = 128
ROT_DIM = HEAD_DIM // 4
ROPE_THETA = 500000.0
DECAY_LORA = 32
AAA_LORA = 32
GATE_LORA = 96
N_EXPERTS = 32
TOP_K = 4
SWIGLU_LIMIT = 7.0
SWIGLU_ALPHA = 1.702
LN_EPS = 1e-5
RWKV_GN_EPS = 64e-5
NEG_INF = -1e30
DEPTH = 1
DEEPNORM_ALPHA = (2 * DEPTH) ** 0.25

LANES = 128
RWKV_CHUNK = 64
MOE_BLOCK = 256
ATTN_PROJ = ATTN_WIDTH + 4 * KV_WIDTH
RWKV_PROJ = 3 * RWKV_WIDTH + 3 * LANES
VMEM_LIMIT = 48 * 1024 * 1024


def _dot(a, b):
    return jnp.dot(a, b, preferred_element_type=F32)


def _dot_nt(a, b):
    return lax.dot_general(a, b, (((1,), (1,)), ((), ())), preferred_element_type=F32)


def _dot_tn(a, b):
    return lax.dot_general(a, b, (((0,), (0,)), ((), ())), preferred_element_type=F32)


def _split3(x):
    h = x.astype(BF16)
    r1 = x - h.astype(F32)
    m = r1.astype(BF16)
    lo = (r1 - m.astype(F32)).astype(BF16)
    return h, m, lo


def _dot_exact_rhs(x, m_bf16):
    h, m, lo = _split3(x)
    return _dot(h, m_bf16) + _dot(m, m_bf16) + _dot(lo, m_bf16)


def _dot_exact_lhs(m_bf16, x):
    h, m, lo = _split3(x)
    return _dot(m_bf16, h) + _dot(m_bf16, m) + _dot(m_bf16, lo)


def _layer_norm(x):
    mu = jnp.mean(x, axis=-1, keepdims=True)
    xc = x - mu
    var = jnp.mean(xc * xc, axis=-1, keepdims=True)
    return xc * lax.rsqrt(var + LN_EPS)


def _sigmoid(x):
    return 1.0 / (1.0 + jnp.exp(-x))


def _mod_kernel(c_ref, w_ref, b_ref, o_ref):
    c = c_ref[...]
    s = c * _sigmoid(c)
    o_ref[...] = jnp.dot(s, w_ref[...], preferred_element_type=F32,
                         precision=lax.Precision.HIGHEST) + b_ref[...]


def _mod(c, w_ada, b_ada):
    B, D = c.shape
    n = w_ada.shape[1] // D
    return pl.pallas_call(
        _mod_kernel,
        grid=(n,),
        in_specs=[pl.BlockSpec((B, D), lambda i: (0, 0)),
                  pl.BlockSpec((D, D), lambda i: (0, i)),
                  pl.BlockSpec((1, D), lambda i: (0, i))],
        out_specs=pl.BlockSpec((B, D), lambda i: (0, i)),
        out_shape=jax.ShapeDtypeStruct((B, n * D), F32),
        compiler_params=pltpu.CompilerParams(dimension_semantics=("arbitrary",),
                                             vmem_limit_bytes=VMEM_LIMIT),
        name="mod",
    )(c, w_ada, b_ada.reshape(1, -1))


def _inproj_kernel(x_ref, pos_ref, mod_ref, wa_ref, wr_ref, mu_ref, rt_ref,
                   qkv_ref, rw_ref, carry_ref):
    j = pl.program_id(1)
    x = x_ref[0]
    tm = x.shape[0]
    mod = mod_ref[0]
    h = _layer_norm(x) * (1.0 + mod[1:2, :]) + mod[0:1, :]
    hb = h.astype(BF16)

    pa = _dot(hb, wa_ref[...])
    ang = pos_ref[0].astype(F32) * rt_ref[0:1, :]
    cs = jnp.cos(ang)
    sn = jnp.sin(ang)
    m_lo = rt_ref[1:2, :]
    m_hi = rt_ref[2:3, :]
    n_q = ATTN_WIDTH // LANES
    n_rot = (ATTN_WIDTH + 2 * KV_WIDTH) // LANES
    for ch in range(n_rot):
        t = pa[:, ch * LANES:(ch + 1) * LANES]
        if ch < n_q:
            t = t * (1.0 / math.sqrt(HEAD_DIM))
        up = pltpu.roll(t, LANES - ROT_DIM // 2, 1)
        dn = pltpu.roll(t, ROT_DIM // 2, 1)
        o = t * cs + sn * (m_lo * up + m_hi * dn)
        qkv_ref[:, ch * LANES:(ch + 1) * LANES] = o.astype(BF16)
    qkv_ref[:, n_rot * LANES:] = pa[:, n_rot * LANES:].astype(BF16)

    pr = _dot(hb, wr_ref[...])
    prev = pltpu.roll(pr, 1, 0)
    row = lax.broadcasted_iota(jnp.int32, (tm, 1), 0)
    carry = jnp.where(j == 0, 0.0, carry_ref[...])
    prev = jnp.where(row == 0, carry, prev)
    carry_ref[...] = pr[tm - 1:tm, :]
    rw_ref[...] = pr + (prev - pr) * mu_ref[...]


def _inproj(x, positions, mod, w_attn, w_rwkv, mu, rot_tab, tm):
    B, S, D = x.shape
    nt = S // tm
    return pl.pallas_call(
        _inproj_kernel,
        grid=(B, nt),
        in_specs=[pl.BlockSpec((1, tm, D), lambda b, j: (b, j, 0)),
                  pl.BlockSpec((1, tm, 1), lambda b, j: (b, j, 0)),
                  pl.BlockSpec((1, 6, D), lambda b, j: (b, 0, 0)),
                  pl.BlockSpec((D, ATTN_PROJ), lambda b, j: (0, 0)),
                  pl.BlockSpec((D, RWKV_PROJ), lambda b, j: (0, 0)),
                  pl.BlockSpec((1, RWKV_PROJ), lambda b, j: (0, 0)),
                  pl.BlockSpec((8, LANES), lambda b, j: (0, 0))],
        out_specs=[pl.BlockSpec((tm, ATTN_PROJ), lambda b, j: (b * nt + j, 0)),
                   pl.BlockSpec((tm, RWKV_PROJ), lambda b, j: (b * nt + j, 0))],
        out_shape=[jax.ShapeDtypeStruct((B * S, ATTN_PROJ), BF16),
                   jax.ShapeDtypeStruct((B * S, RWKV_PROJ), F32)],
        scratch_shapes=[pltpu.VMEM((1, RWKV_PROJ), F32)],
        compiler_params=pltpu.CompilerParams(dimension_semantics=("arbitrary", "arbitrary"),
                                             vmem_limit_bytes=VMEM_LIMIT),
        name="inproj",
    )(x, positions.reshape(B, S, 1), mod, w_attn, w_rwkv, mu, rot_tab)


def _attn_kernel(q_ref, kc_ref, kp_ref, vc_ref, vp_ref, sink_ref, o_ref):
    i = pl.program_id(1)
    blk = ATTN_BLOCK
    qi = lax.broadcasted_iota(jnp.int32, (blk, 2 * blk), 0)
    kj = lax.broadcasted_iota(jnp.int32, (blk, 2 * blk), 1)
    allowed = (kj > qi) & (kj <= qi + blk) & ((kj >= blk) | (i > 0))
    lane = lax.broadcasted_iota(jnp.int32, (1, LANES), 1)
    lo = (lane < HEAD_DIM).astype(BF16)
    hi = (lane >= HEAD_DIM).astype(BF16)
    halves = []
    for g in range(N_KV_HEADS):
        sl = slice(g * LANES, (g + 1) * LANES)
        kcat = jnp.concatenate([kp_ref[:, sl], kc_ref[:, sl]], axis=0)
        vcat = jnp.concatenate([vp_ref[:, sl], vc_ref[:, sl]], axis=0)
        halves.append(((kcat * lo, vcat * lo), (kcat * hi, vcat * hi)))
    heads = [(c, half) for c in range(ATTN_WIDTH // LANES) for half in range(2)]
    scores = [_dot_nt(q_ref[:, c * LANES:(c + 1) * LANES], halves[c // 2][half][0]) for c, half in heads]
    probs, denoms = [], []
    for (c, half), s in zip(heads, scores):
        sink = sink_ref[2 * c + half]
        s = jnp.where(allowed, s, NEG_INF)
        m = jnp.maximum(jnp.max(s, axis=-1, keepdims=True), sink)
        p = jnp.exp(s - m)
        denoms.append(jnp.sum(p, axis=-1, keepdims=True) + jnp.exp(sink - m))
        probs.append(p.astype(BF16))
    outs = [_dot(p, halves[c // 2][half][1]) / d for (c, half), p, d in zip(heads, probs, denoms)]
    for c in range(ATTN_WIDTH // LANES):
        o_ref[:, c * LANES:(c + 1) * LANES] = (outs[2 * c] + outs[2 * c + 1]).astype(BF16)


def _attention(qkv, sinks, B, S):
    nb = S // ATTN_BLOCK
    blk = ATTN_BLOCK
    kcol = ATTN_WIDTH // (2 * KV_WIDTH)
    vcol = kcol + 1
    cur = lambda col: (lambda b, i: (b * nb + i, col))
    prv = lambda col: (lambda b, i: (jnp.maximum(b * nb + i - 1, 0), col))
    return pl.pallas_call(
        _attn_kernel,
        grid=(B, nb),
        in_specs=[pl.BlockSpec((blk, ATTN_WIDTH), lambda b, i: (b * nb + i, 0)),
                  pl.BlockSpec((blk, 2 * KV_WIDTH), cur(kcol)),
                  pl.BlockSpec((blk, 2 * KV_WIDTH), prv(kcol)),
                  pl.BlockSpec((blk, 2 * KV_WIDTH), cur(vcol)),
                  pl.BlockSpec((blk, 2 * KV_WIDTH), prv(vcol)),
                  pl.BlockSpec(memory_space=pltpu.SMEM)],
        out_specs=pl.BlockSpec((blk, ATTN_WIDTH), lambda b, i: (b * nb + i, 0)),
        out_shape=jax.ShapeDtypeStruct((B * S, ATTN_WIDTH), BF16),
        compiler_params=pltpu.CompilerParams(dimension_semantics=("arbitrary", "arbitrary"),
                                             vmem_limit_bytes=VMEM_LIMIT),
        name="attn",
    )(qkv, qkv, qkv, qkv, qkv, sinks)


def _rwkv_kernel(rw_ref, vec_ref, w2_ref, a2_ref, g2_ref, o_ref, state_ref, *, n_chunk):
    j = pl.program_id(1)
    C = RWKV_CHUNK
    W = RWKV_WIDTH
    n_pair = W // LANES

    @pl.when(j == 0)
    def _():
        state_ref[...] = jnp.zeros_like(state_ref)

    w0 = vec_ref[0:1, :]
    a0 = vec_ref[1:2, :]
    k_k = vec_ref[2:3, :]
    k_a = vec_ref[3:4, :]
    r_k = vec_ref[4:5, :]
    ln_w = vec_ref[5:6, :]
    ln_b = vec_ref[6:7, :]

    r = rw_ref[:, 0:W]
    k = rw_ref[:, W:2 * W]
    v = rw_ref[:, 2 * W:3 * W]
    wl = rw_ref[:, 3 * W:3 * W + LANES]
    al = rw_ref[:, 3 * W + LANES:3 * W + 2 * LANES]
    gl = rw_ref[:, 3 * W + 2 * LANES:3 * W + 3 * LANES]

    ri = lax.broadcasted_iota(jnp.int32, (LANES, LANES), 0)
    ci = lax.broadcasted_iota(jnp.int32, (LANES, LANES), 1)
    same = (ri // HEAD_DIM) == (ci // HEAD_DIM)
    strict = same & ((ri % HEAD_DIM) > (ci % HEAD_DIM))
    incl = same & ((ri % HEAD_DIM) >= (ci % HEAD_DIM))
    ones_bd = same.astype(BF16)
    lane = lax.broadcasted_iota(jnp.int32, (1, LANES), 1)
    m0 = (lane < HEAD_DIM).astype(F32)
    m1 = 1.0 - m0
    tri = (lax.broadcasted_iota(jnp.int32, (C, C), 0) >= lax.broadcasted_iota(jnp.int32, (C, C), 1)).astype(BF16)

    def head_sum(xv):
        return jnp.concatenate(
            [_dot_exact_rhs(xv[:, p * LANES:(p + 1) * LANES], ones_bd) for p in range(n_pair)], axis=1)

    def stack2(xp):
        return jnp.concatenate([xp * m0, xp * m1], axis=0)

    z = w0 + _dot(jnp.tanh(wl).astype(BF16), w2_ref[...])
    lw = -math.exp(-0.5) * _sigmoid(z)
    a = _sigmoid(a0 + _dot(al.astype(BF16), a2_ref[...]))
    g = _dot(_sigmoid(gl).astype(BF16), g2_ref[...])
    kk = k * k_k
    kkn = kk / jnp.maximum(jnp.sqrt(head_sum(kk * kk)), 1e-12)
    k2 = k * (1.0 + (a - 1.0) * k_a)
    av = -kkn
    bv = kkn * a
    bonus = head_sum(r * k2 * r_k) * v

    eye = (ri == ci).astype(F32)
    bf = lambda t: t.astype(BF16)

    pre = []
    for c in range(n_chunk):
        rows = slice(c * C, (c + 1) * C)
        lwc = lw[rows]
        cw = _dot_exact_lhs(tri, lwc)
        cwl = cw[C - 1:C, :]
        e_in = jnp.exp(cw)
        e_neg = jnp.exp(-cw)
        e_rem = jnp.exp(cwl - cw)
        wc = jnp.exp(cwl)
        Rt = r[rows] * e_in
        At = av[rows] * jnp.exp(cw - lwc)
        Bb = bv[rows] * e_neg
        Kb = k2[rows] * e_neg
        Bh = bv[rows] * e_rem
        Kh = k2[rows] * e_rem
        vc = v[rows]
        for p in range(n_pair):
            sl = slice(p * LANES, (p + 1) * LANES)
            pre.append(dict(At=At[:, sl], Rt=Rt[:, sl], Bb=Bb[:, sl], Kb=Kb[:, sl], Bh=Bh[:, sl], Kh=Kh[:, sl],
                            v=vc[:, sl], wc=wc[:, sl]))

    for u in pre:
        u["at_bd"] = stack2(u["At"])
        lhs = bf(jnp.concatenate([u["at_bd"], stack2(u["Rt"])], axis=0))
        rhs = bf(jnp.concatenate([stack2(u["Bb"]), stack2(u["Kb"])], axis=0))
        u["G"] = _dot_nt(lhs, rhs)
    for u in pre:
        G = u.pop("G")
        u["a_ab"] = jnp.where(strict, G[0:2 * C, 0:2 * C], 0.0)
        u["a_ak"] = bf(jnp.where(strict, G[0:2 * C, 2 * C:4 * C], 0.0))
        u["a_rb"] = bf(jnp.where(incl, G[2 * C:4 * C, 0:2 * C], 0.0))
        u["a_rk"] = bf(jnp.where(incl, G[2 * C:4 * C, 2 * C:4 * C], 0.0))
        u["v_bd"] = bf(stack2(u["v"]))
    for u in pre:
        xb = bf(u["a_ab"])
        u["P"] = eye + u.pop("a_ab")
        u["X"] = _dot(xb, xb)
        u["M0"] = _dot(u["a_ak"], u["v_bd"])
    for _ in range(int(math.log2(C)) - 2):
        for u in pre:
            Wm = _dot(bf(u["X"]), bf(jnp.concatenate([u["P"], u["X"]], axis=1)))
            u["P"] = u["P"] + Wm[:, 0:LANES]
            u["X"] = Wm[:, LANES:2 * LANES]
    for u in pre:
        u["P"] = bf(u["P"] + _dot(bf(u.pop("X")), bf(u["P"])))
    for u in pre:
        u["M1"] = _dot(u["P"], bf(u.pop("M0")))
        u["Q"] = bf(_dot(u["a_rb"], u["P"]))
        u["PtB"] = _dot_tn(u["P"], bf(stack2(u["Bh"])))
    for u in pre:
        M1 = u.pop("M1")
        u["Y0"] = _dot(jnp.concatenate([u["a_rb"], u["a_rk"]], axis=1),
                       jnp.concatenate([bf(M1), u["v_bd"]], axis=0))
        u["Tm"] = bf(_dot_tn(bf(u["at_bd"]), bf(u.pop("PtB"))))
        m1_pair = M1[0:C] + M1[C:2 * C]
        cst = _dot_tn(bf(jnp.concatenate([m1_pair, u["v"]], axis=0)),
                      bf(jnp.concatenate([u["Bh"], u["Kh"]], axis=0)))
        u["cst"] = jnp.where(same, cst, 0.0)
        u["ar"] = bf(jnp.concatenate([u["At"], u["Rt"]], axis=0))

    states = [state_ref[p] for p in range(n_pair)]
    for c in range(n_chunk):
        rows = slice(c * C, (c + 1) * C)
        us = pre[c * n_pair:(c + 1) * n_pair]
        sbs = [bf(S) for S in states]
        zs = [_dot_nt(u["ar"], sb) for u, sb in zip(us, sbs)]
        new_states = [S * u["wc"] + _dot(sb, u["Tm"]) + u["cst"] for u, S, sb in zip(us, states, sbs)]
        ybds = [stack2(Z[C:2 * C]) + _dot(u["Q"], bf(stack2(Z[0:C]))) + u["Y0"] for u, Z in zip(us, zs)]
        ys = [y_bd[0:C] + y_bd[C:2 * C] for y_bd in ybds]
        states = new_states
        y = jnp.concatenate(ys, axis=1)
        mu = head_sum(y) * (1.0 / HEAD_DIM)
        yc = y - mu
        var = head_sum(yc * yc) * (1.0 / HEAD_DIM)
        yn = yc * lax.rsqrt(var + RWKV_GN_EPS) * ln_w + ln_b
        o_ref[rows, :] = ((yn + bonus[rows]) * g[rows]).astype(BF16)
    for p in range(n_pair):
        state_ref[p] = states[p]


def _rwkv(rw, vecs, w2, a2, g2, B, S, lb):
    nt = S // lb
    return pl.pallas_call(
        functools.partial(_rwkv_kernel, n_chunk=lb // RWKV_CHUNK),
        grid=(B, nt),
        in_specs=[pl.BlockSpec((lb, RWKV_PROJ), lambda b, j: (b * nt + j, 0)),
                  pl.BlockSpec((8, RWKV_WIDTH), lambda b, j: (0, 0)),
                  pl.BlockSpec((LANES, RWKV_WIDTH), lambda b, j: (0, 0)),
                  pl.BlockSpec((LANES, RWKV_WIDTH), lambda b, j: (0, 0)),
                  pl.BlockSpec((LANES, RWKV_WIDTH), lambda b, j: (0, 0))],
        out_specs=pl.BlockSpec((lb, RWKV_WIDTH), lambda b, j: (b * nt + j, 0)),
        out_shape=jax.ShapeDtypeStruct((B * S, RWKV_WIDTH), BF16),
        scratch_shapes=[pltpu.VMEM((RWKV_WIDTH // LANES, LANES, LANES), F32)],
        compiler_params=pltpu.CompilerParams(dimension_semantics=("arbitrary", "arbitrary"),
                                             vmem_limit_bytes=VMEM_LIMIT),
        name="rwkv",
    )(rw, vecs, w2, a2, g2)


def _mix_kernel(at_ref, rk_ref, x_ref, mod_ref, wo_a_ref, wo_r_ref, ln_ref, wr_ref, br_ref,
                x1_ref, h2_ref, ti_ref, gt_ref, rank_ref, cnt_ref, base_ref):
    first = (pl.program_id(0) == 0) & (pl.program_id(1) == 0)

    @pl.when(first)
    def _():
        base_ref[...] = jnp.zeros_like(base_ref)

    mod = mod_ref[0]
    y = _dot(at_ref[...], wo_a_ref[...]) + _dot(rk_ref[...], wo_r_ref[...])
    x = x_ref[...]
    tm = x.shape[0]
    x1 = _layer_norm(DEEPNORM_ALPHA * x + (1.0 + mod[2:3, :]) * y) * ln_ref[0:1, :] + ln_ref[1:2, :]
    h2 = _layer_norm(x1) * (1.0 + mod[4:5, :]) + mod[3:4, :]
    x1_ref[...] = x1
    h2_ref[...] = h2

    logits = jnp.dot(h2, wr_ref[...], preferred_element_type=F32,
                     precision=lax.Precision.HIGHEST) + br_ref[...]
    lt = jnp.transpose(logits)[0:N_EXPERTS, :]
    erow = lax.broadcasted_iota(jnp.int32, (N_EXPERTS, tm), 0).astype(F32)
    cur = lt
    vals, idxs = [], []
    for _ in range(TOP_K):
        m = jnp.max(cur, axis=0, keepdims=True)
        idx = jnp.min(jnp.where(cur == m, erow, float(N_EXPERTS)), axis=0, keepdims=True)
        vals.append(m)
        idxs.append(idx)
        cur = jnp.where(erow == idx, -jnp.inf, cur)
    tv = jnp.concatenate(vals, axis=0)
    e = jnp.exp(tv - tv[0:1, :])
    gt_ref[...] = e / jnp.sum(e, axis=0, keepdims=True)
    ti_ref[...] = jnp.concatenate(idxs, axis=0).astype(jnp.int32)

    onehot = jnp.zeros((N_EXPERTS, tm), F32)
    for idx in idxs:
        onehot = onehot + (erow == idx).astype(F32)
    before = (lax.broadcasted_iota(jnp.int32, (tm, tm), 0)
              < lax.broadcasted_iota(jnp.int32, (tm, tm), 1)).astype(BF16)
    tot = base_ref[:, 0:1] + _dot(onehot.astype(BF16), before)
    ranks = [jnp.sum(jnp.where(erow == idx, tot, 0.0), axis=0, keepdims=True) for idx in idxs]
    rank_ref[...] = jnp.concatenate(ranks, axis=0).astype(jnp.int32)
    base_ref[...] = base_ref[...] + jnp.sum(onehot, axis=1, keepdims=True)
    cnt_ref[...] = base_ref[...].astype(jnp.int32)


def _mix(attn_out, rwkv_out, x2d, mod, wo_a, wo_r, ln1, w_router, b_router, B, S, tm):
    D = x2d.shape[1]
    nt = S // tm
    T = B * S
    tok = lambda b, j: (b * nt + j, 0)
    col = lambda b, j: (0, b * nt + j)
    fixed = lambda b, j: (0, 0)
    return pl.pallas_call(
        _mix_kernel,
        grid=(B, nt),
        in_specs=[pl.BlockSpec((tm, ATTN_WIDTH), tok),
                  pl.BlockSpec((tm, RWKV_WIDTH), tok),
                  pl.BlockSpec((tm, D), tok),
                  pl.BlockSpec((1, 6, D), lambda b, j: (b, 0, 0)),
                  pl.BlockSpec((ATTN_WIDTH, D), fixed),
                  pl.BlockSpec((RWKV_WIDTH, D), fixed),
                  pl.BlockSpec((2, D), fixed),
                  pl.BlockSpec((D, LANES), fixed),
                  pl.BlockSpec((1, LANES), fixed)],
        out_specs=[pl.BlockSpec((tm, D), tok),
                   pl.BlockSpec((tm, D), tok),
                   pl.BlockSpec((TOP_K, tm), col),
                   pl.BlockSpec((TOP_K, tm), col),
                   pl.BlockSpec((TOP_K, tm), col),
                   pl.BlockSpec((N_EXPERTS, LANES), fixed)],
        out_shape=[jax.ShapeDtypeStruct((T, D), F32),
                   jax.ShapeDtypeStruct((T, D), F32),
                   jax.ShapeDtypeStruct((TOP_K, T), jnp.int32),
                   jax.ShapeDtypeStruct((TOP_K, T), F32),
                   jax.ShapeDtypeStruct((TOP_K, T), jnp.int32),
                   jax.ShapeDtypeStruct((N_EXPERTS, LANES), jnp.int32)],
        scratch_shapes=[pltpu.VMEM((N_EXPERTS, LANES), F32)],
        compiler_params=pltpu.CompilerParams(dimension_semantics=("arbitrary", "arbitrary"),
                                             vmem_limit_bytes=VMEM_LIMIT),
        name="mix",
    )(attn_out, rwkv_out, x2d, mod, wo_a, wo_r, ln1, w_router, b_router)


def _row_copy(src, src_row, dst, dst_row, sem):
    return pltpu.make_async_copy(src.at[pl.ds(src_row, 1)], dst.at[pl.ds(dst_row, 1)], sem)


def _dispatch_kernel(dest_ref, h2_ref, xs_in_ref, xs_ref, sem):
    del xs_in_ref
    tm = dest_ref.shape[1]

    def issue(t, carry):
        for k in range(TOP_K):
            _row_copy(h2_ref, t, xs_ref, dest_ref[k, t], sem).start()
        return carry

    def drain(t, carry):
        for k in range(TOP_K):
            _row_copy(h2_ref, t, xs_ref, dest_ref[k, t], sem).wait()
        return carry

    lax.fori_loop(0, tm, issue, 0)
    lax.fori_loop(0, tm, drain, 0)


def _dispatch(dest, h2, n_rows, tm):
    T, D = h2.shape
    xs0 = jnp.zeros((n_rows, D), h2.dtype)
    return pl.pallas_call(
        _dispatch_kernel,
        grid=(T // tm,),
        in_specs=[pl.BlockSpec((TOP_K, tm), lambda i: (0, i), memory_space=pltpu.SMEM),
                  pl.BlockSpec((tm, D), lambda i: (i, 0)),
                  pl.BlockSpec(memory_space=pl.ANY)],
        out_specs=pl.BlockSpec(memory_space=pl.ANY),
        out_shape=jax.ShapeDtypeStruct((n_rows, D), h2.dtype),
        scratch_shapes=[pltpu.SemaphoreType.DMA(())],
        input_output_aliases={2: 0},
        compiler_params=pltpu.CompilerParams(dimension_semantics=("arbitrary",),
                                             vmem_limit_bytes=VMEM_LIMIT),
        name="dispatch",
    )(dest, h2, xs0)


def _experts_kernel(blk_e_ref, n_used_ref, xs_ref, wgu_ref, bgu_ref, wd_ref, bd_ref, ys_ref):
    i = pl.program_id(0)
    d_ff = wd_ref.shape[1]

    @pl.when(i < n_used_ref[0])
    def _():
        xb = xs_ref[...].astype(BF16)
        gu = _dot(xb, wgu_ref[0]) + bgu_ref[0]
        gate = jnp.minimum(gu[:, :d_ff], SWIGLU_LIMIT)
        up = jnp.clip(gu[:, d_ff:], -SWIGLU_LIMIT, SWIGLU_LIMIT)
        act = (up + 1.0) * (gate * _sigmoid(SWIGLU_ALPHA * gate))
        ys_ref[...] = _dot(act.astype(BF16), wd_ref[0]) + bd_ref[0]

    @pl.when(i >= n_used_ref[0])
    def _():
        ys_ref[...] = jnp.zeros_like(ys_ref)


def _experts(blk_e, n_used, xs, wgu, bgu, wd, bd):
    n_rows, D = xs.shape
    n_blocks = n_rows // MOE_BLOCK
    d_ff = wd.shape[1]

    def last_used(i, n_used_ref):
        return jnp.minimum(i, jnp.maximum(n_used_ref[0] - 1, 0))

    def row_map(i, blk_e_ref, n_used_ref):
        return (last_used(i, n_used_ref), 0)

    def exp_map(i, blk_e_ref, n_used_ref):
        return (blk_e_ref[last_used(i, n_used_ref)], 0, 0)

    grid_spec = pltpu.PrefetchScalarGridSpec(
        num_scalar_prefetch=2,
        grid=(n_blocks,),
        in_specs=[pl.BlockSpec((MOE_BLOCK, D), row_map),
                  pl.BlockSpec((1, D, 2 * d_ff), exp_map),
                  pl.BlockSpec((1, 1, 2 * d_ff), exp_map),
                  pl.BlockSpec((1, d_ff, D), exp_map),
                  pl.BlockSpec((1, 1, D), exp_map)],
        out_specs=pl.BlockSpec((MOE_BLOCK, D), lambda i, blk_e_ref, n_used_ref: (i, 0)),
    )
    return pl.pallas_call(
        _experts_kernel,
        grid_spec=grid_spec,
        out_shape=jax.ShapeDtypeStruct((n_rows, D), F32),
        compiler_params=pltpu.CompilerParams(dimension_semantics=("arbitrary",),
                                             vmem_limit_bytes=VMEM_LIMIT),
        name="experts",
    )(blk_e, n_used, xs, wgu, bgu, wd, bd)


def _combine_kernel(dest_ref, ys_ref, x1_ref, gt_ref, mod_ref, ln_ref, o_ref, buf, sem):
    tm = x1_ref.shape[0]

    def issue(t, carry):
        for k in range(TOP_K):
            _row_copy(ys_ref, dest_ref[k, t], buf.at[k], t, sem).start()
        return carry

    def drain(t, carry):
        for k in range(TOP_K):
            _row_copy(ys_ref, dest_ref[k, t], buf.at[k], t, sem).wait()
        return carry

    lax.fori_loop(0, tm, issue, 0)
    lax.fori_loop(0, tm, drain, 0)
    gt = gt_ref[...]
    y = gt[:, 0:1] * buf[0]
    for k in range(1, TOP_K):
        y = y + gt[:, k:k + 1] * buf[k]
    mod = mod_ref[0]
    z = DEEPNORM_ALPHA * x1_ref[...] + (1.0 + mod[5:6, :]) * y
    o_ref[...] = _layer_norm(z) * ln_ref[0:1, :] + ln_ref[1:2, :]


def _combine(dest, ys, x1, gates_t, mod, ln2, B, S, tm):
    T, D = x1.shape
    nt = S // tm
    tok = lambda b, j: (b * nt + j, 0)
    return pl.pallas_call(
        _combine_kernel,
        grid=(B, nt),
        in_specs=[pl.BlockSpec((TOP_K, tm), lambda b, j: (0, b * nt + j), memory_space=pltpu.SMEM),
                  pl.BlockSpec(memory_space=pl.ANY),
                  pl.BlockSpec((tm, D), tok),
                  pl.BlockSpec((tm, TOP_K), tok),
                  pl.BlockSpec((1, 6, D), lambda b, j: (b, 0, 0)),
                  pl.BlockSpec((2, D), lambda b, j: (0, 0))],
        out_specs=pl.BlockSpec((tm, D), tok),
        out_shape=jax.ShapeDtypeStruct((T, D), F32),
        scratch_shapes=[pltpu.VMEM((TOP_K, tm, D), F32), pltpu.SemaphoreType.DMA(())],
        compiler_params=pltpu.CompilerParams(dimension_semantics=("arbitrary", "arbitrary"),
                                             vmem_limit_bytes=VMEM_LIMIT),
        name="combine",
    )(dest, ys, x1, gates_t, mod, ln2)


def _pad_rows(w, rows):
    return jnp.pad(w, ((0, rows - w.shape[0]), (0, 0)))


def _pad_cols(w, cols):
    return jnp.pad(w, ((0, 0), (0, cols - w.shape[1])))


def _layer(x, c, positions, w_ada, b_ada, w_in, shift_mu, rwkv_w0, rwkv_w2, rwkv_a0, rwkv_a2, rwkv_g2,
           rwkv_k_k, rwkv_k_a, rwkv_r_k, rwkv_ln_w, rwkv_ln_b, attn_sinks, w_out, ln1_g, ln1_b,
           w_router, b_router, w_gate_up, b_gate_up, w_down, b_down, ln2_g, ln2_b):
    B, S, D = x.shape
    T = B * S
    tile = min(256, S)

    q0, k0, v0 = 0, ATTN_WIDTH, ATTN_WIDTH + KV_WIDTH
    r0 = ATTN_WIDTH + 2 * KV_WIDTH
    heads = lambda base: [w_in[:, base + h * HEAD_DIM: base + (h + 1) * HEAD_DIM] for h in range(N_KV_HEADS)]
    dup = lambda hs: [w for w in hs for _ in range(2)]
    w_attn = jnp.concatenate([w_in[:, q0:q0 + ATTN_WIDTH]] + dup(heads(k0)) + dup(heads(v0)), axis=1).astype(BF16)
    lora0 = r0 + 3 * RWKV_WIDTH
    lora = (DECAY_LORA, AAA_LORA, GATE_LORA)
    pieces_w = [w_in[:, r0:lora0]]
    pieces_mu = [shift_mu[None, 0:3 * RWKV_WIDTH]]
    off = lora0
    for n in lora:
        pieces_w.append(_pad_cols(w_in[:, off:off + n], LANES))
        pieces_mu.append(_pad_cols(shift_mu[None, off - r0:off - r0 + n], LANES))
        off += n
    w_rwkv = jnp.concatenate(pieces_w, axis=1).astype(BF16)
    mu = jnp.concatenate(pieces_mu, axis=1)
    inv_freq = ROPE_THETA ** (-jnp.arange(0, ROT_DIM, 2, dtype=F32) / ROT_DIM)
    lane_p = jnp.arange(LANES) % HEAD_DIM
    rot_tab = jnp.zeros((8, LANES), F32)
    rot_tab = rot_tab.at[0].set(jnp.where(lane_p < ROT_DIM, inv_freq[lane_p % (ROT_DIM // 2)], 0.0))
    rot_tab = rot_tab.at[1].set(jnp.where(lane_p < ROT_DIM // 2, -1.0, 0.0))
    rot_tab = rot_tab.at[2].set(jnp.where((lane_p >= ROT_DIM // 2) & (lane_p < ROT_DIM), 1.0, 0.0))
    vecs = jnp.stack([rwkv_w0, rwkv_a0, rwkv_k_k, rwkv_k_a, rwkv_r_k.reshape(-1), rwkv_ln_w, rwkv_ln_b,
                      jnp.zeros_like(rwkv_w0)])
    w2 = _pad_rows(rwkv_w2, LANES).astype(BF16)
    a2 = _pad_rows(rwkv_a2, LANES).astype(BF16)
    g2 = _pad_rows(rwkv_g2, LANES).astype(BF16)
    wo_a = w_out[:ATTN_WIDTH].astype(BF16)
    wo_r = w_out[ATTN_WIDTH:].astype(BF16)
    w_r = _pad_cols(w_router, LANES)
    b_r = jnp.concatenate([b_router, jnp.full((LANES - N_EXPERTS,), NEG_INF, F32)])[None, :]
    wgu = w_gate_up.astype(BF16)
    wd = w_down.astype(BF16)

    mod = _mod(c, w_ada, b_ada).reshape(B, 6, D)
    qkv, rw = _inproj(x, positions, mod, w_attn, w_rwkv, mu, rot_tab, tile)
    attn_out = _attention(qkv, attn_sinks, B, S)
    rwkv_out = _rwkv(rw, vecs, w2, a2, g2, B, S, min(128, S))

    x1, h2, top_i, gates, rank, cnt = _mix(attn_out, rwkv_out, x.reshape(T, D), mod, wo_a, wo_r,
                                            jnp.stack([ln1_g, ln1_b]), w_r, b_r, B, S, tile)

    counts = cnt[:, 0]
    padded = (counts + MOE_BLOCK - 1) // MOE_BLOCK * MOE_BLOCK
    pend = jnp.cumsum(padded)
    pstart = pend - padded
    experts = jnp.arange(N_EXPERTS, dtype=jnp.int32)
    dest = rank + jnp.sum(jnp.where(top_i[None] == experts[:, None, None], pstart[:, None, None], 0), axis=0)
    n_blocks = T * TOP_K // MOE_BLOCK + N_EXPERTS
    blk_row = jnp.arange(n_blocks, dtype=jnp.int32) * MOE_BLOCK
    blk_e = jnp.minimum(jnp.sum((pend[None, :] <= blk_row[:, None]).astype(jnp.int32), axis=1), N_EXPERTS - 1)
    n_used = (pend[-1:] // MOE_BLOCK).astype(jnp.int32)

    xs = _dispatch(dest, h2, n_blocks * MOE_BLOCK, tile)
    ys = _experts(blk_e, n_used, xs, wgu, b_gate_up[:, None, :], wd, b_down[:, None, :])
    out = _combine(dest, ys, x1, jnp.transpose(gates), mod, jnp.stack([ln2_g, ln2_b]), B, S, tile)
    return out.reshape(B, S, D)


def kernel(x, c, positions, w_ada, b_ada, w_in, shift_mu, rwkv_w0, rwkv_w2, rwkv_a0, rwkv_a2, rwkv_g2,
           rwkv_k_k, rwkv_k_a, rwkv_r_k, rwkv_ln_w, rwkv_ln_b, attn_sinks, w_out, ln1_g, ln1_b,
           w_router, b_router, w_gate_up, b_gate_up, w_down, b_down, ln2_g, ln2_b):
    for l in range(DEPTH):
        x = _layer(x, c, positions, w_ada[l], b_ada[l], w_in[l], shift_mu[l], rwkv_w0[l], rwkv_w2[l],
                   rwkv_a0[l], rwkv_a2[l], rwkv_g2[l], rwkv_k_k[l], rwkv_k_a[l], rwkv_r_k[l], rwkv_ln_w[l],
                   rwkv_ln_b[l], attn_sinks[l], w_out[l], ln1_g[l], ln1_b[l], w_router[l], b_router[l],
                   w_gate_up[l], b_gate_up[l], w_down[l], b_down[l], ln2_g[l], ln2_b[l])
    return x
```

```python
import functools
import math

import jax
import jax.numpy as jnp
from jax import lax
from jax.experimental import pallas as pl
from jax.experimental.pallas import tpu as pltpu

F32 = jnp.float32
BF16 = jnp.bfloat16

HEAD_DIM = 64
N_ATTN_HEADS = 8
N_KV_HEADS = 2
N_RWKV_HEADS = 8
ATTN_WIDTH = N_ATTN_HEADS * HEAD_DIM
KV_WIDTH = N_KV_HEADS * HEAD_DIM
RWKV_WIDTH = N_RWKV_HEADS * HEAD_DIM
ATTN_BLOCK = 128
ROT_DIM = HEAD_DIM // 4
ROPE_THETA = 500000.0
DECAY_LORA = 32
AAA_LORA = 32
GATE_LORA = 96
N_EXPERTS = 32
TOP_K = 4
SWIGLU_LIMIT = 7.0
SWIGLU_ALPHA = 1.702
LN_EPS = 1e-5
RWKV_GN_EPS = 64e-5
NEG_INF = -1e30
DEPTH = 1
DEEPNORM_ALPHA = (2 * DEPTH) ** 0.25

LANES = 128
RWKV_CHUNK = 64
MOE_BLOCK = 256
MOE_TILE = 256
ATTN_PROJ = ATTN_WIDTH + 4 * KV_WIDTH
RWKV_PROJ = 3 * RWKV_WIDTH + 3 * LANES
VMEM_LIMIT = 48 * 1024 * 1024


def _dot(a, b):
    return jnp.dot(a, b, preferred_element_type=F32)


def _dot_nt(a, b):
    return lax.dot_general(a, b, (((1,), (1,)), ((), ())), preferred_element_type=F32)


def _dot_tn(a, b):
    return lax.dot_general(a, b, (((0,), (0,)), ((), ())), preferred_element_type=F32)


def _split3(x):
    h = x.astype(BF16)
    r1 = x - h.astype(F32)
    m = r1.astype(BF16)
    lo = (r1 - m.astype(F32)).astype(BF16)
    return h, m, lo


def _dot_exact_rhs(x, m_bf16):
    h, m, lo = _split3(x)
    return _dot(h, m_bf16) + _dot(m, m_bf16) + _dot(lo, m_bf16)


def _dot_exact_lhs(m_bf16, x):
    h, m, lo = _split3(x)
    return _dot(m_bf16, h) + _dot(m_bf16, m) + _dot(m_bf16, lo)


def _layer_norm(x):
    mu = jnp.mean(x, axis=-1, keepdims=True)
    xc = x - mu
    var = jnp.mean(xc * xc, axis=-1, keepdims=True)
    return xc * lax.rsqrt(var + LN_EPS)


def _sigmoid(x):
    return 1.0 / (1.0 + jnp.exp(-x))


def _mod_kernel(c_ref, w_ref, b_ref, o_ref):
    c = c_ref[...]
    s = c * _sigmoid(c)
    o_ref[...] = jnp.dot(s, w_ref[...], preferred_element_type=F32,
                         precision=lax.Precision.HIGHEST) + b_ref[...]


def _mod(c, w_ada, b_ada):
    B, D = c.shape
    n = w_ada.shape[1] // D
    return pl.pallas_call(
        _mod_kernel,
        grid=(n,),
        in_specs=[pl.BlockSpec((B, D), lambda i: (0, 0)),
                  pl.BlockSpec((D, D), lambda i: (0, i)),
                  pl.BlockSpec((1, D), lambda i: (0, i))],
        out_specs=pl.BlockSpec((B, D), lambda i: (0, i)),
        out_shape=jax.ShapeDtypeStruct((B, n * D), F32),
        compiler_params=pltpu.CompilerParams(dimension_semantics=("arbitrary",),
                                             vmem_limit_bytes=VMEM_LIMIT),
        name="mod",
    )(c, w_ada, b_ada.reshape(1, -1))


def _inproj_kernel(x_ref, pos_ref, mod_ref, wa_ref, wr_ref, mu_ref, rt_ref,
                   qkv_ref, rw_ref, carry_ref):
    j = pl.program_id(1)
    x = x_ref[0]
    tm = x.shape[0]
    mod = mod_ref[0]
    h = _layer_norm(x) * (1.0 + mod[1:2, :]) + mod[0:1, :]
    hb = h.astype(BF16)

    pa = _dot(hb, wa_ref[...])
    ang = pos_ref[0].astype(F32) * rt_ref[0:1, :]
    cs = jnp.cos(ang)
    sn = jnp.sin(ang)
    m_lo = rt_ref[1:2, :]
    m_hi = rt_ref[2:3, :]
    n_q = ATTN_WIDTH // LANES
    n_rot = (ATTN_WIDTH + 2 * KV_WIDTH) // LANES
    for ch in range(n_rot):
        t = pa[:, ch * LANES:(ch + 1) * LANES]
        if ch < n_q:
            t = t * (1.0 / math.sqrt(HEAD_DIM))
        up = pltpu.roll(t, LANES - ROT_DIM // 2, 1)
        dn = pltpu.roll(t, ROT_DIM // 2, 1)
        o = t * cs + sn * (m_lo * up + m_hi * dn)
        qkv_ref[:, ch * LANES:(ch + 1) * LANES] = o.astype(BF16)
    qkv_ref[:, n_rot * LANES:] = pa[:, n_rot * LANES:].astype(BF16)

    pr = _dot(hb, wr_ref[...])
    prev = pltpu.roll(pr, 1, 0)
    row = lax.broadcasted_iota(jnp.int32, (tm, 1), 0)
    carry = jnp.where(j == 0, 0.0, carry_ref[...])
    prev = jnp.where(row == 0, carry, prev)
    carry_ref[...] = pr[tm - 1:tm, :]
    rw_ref[...] = pr + (prev - pr) * mu_ref[...]


def _inproj(x, positions, mod, w_attn, w_rwkv, mu, rot_tab, tm):
    B, S, D = x.shape
    nt = S // tm
    return pl.pallas_call(
        _inproj_kernel,
        grid=(B, nt),
        in_specs=[pl.BlockSpec((1, tm, D), lambda b, j: (b, j, 0)),
                  pl.BlockSpec((1, tm, 1), lambda b, j: (b, j, 0)),
                  pl.BlockSpec((1, 6, D), lambda b, j: (b, 0, 0)),
                  pl.BlockSpec((D, ATTN_PROJ), lambda b, j: (0, 0)),
                  pl.BlockSpec((D, RWKV_PROJ), lambda b, j: (0, 0)),
                  pl.BlockSpec((1, RWKV_PROJ), lambda b, j: (0, 0)),
                  pl.BlockSpec((8, LANES), lambda b, j: (0, 0))],
        out_specs=[pl.BlockSpec((tm, ATTN_PROJ), lambda b, j: (b * nt + j, 0)),
                   pl.BlockSpec((tm, RWKV_PROJ), lambda b, j: (b * nt + j, 0))],
        out_shape=[jax.ShapeDtypeStruct((B * S, ATTN_PROJ), BF16),
                   jax.ShapeDtypeStruct((B * S, RWKV_PROJ), F32)],
        scratch_shapes=[pltpu.VMEM((1, RWKV_PROJ), F32)],
        compiler_params=pltpu.CompilerParams(dimension_semantics=("arbitrary", "arbitrary"),
                                             vmem_limit_bytes=VMEM_LIMIT),
        name="inproj",
    )(x, positions.reshape(B, S, 1), mod, w_attn, w_rwkv, mu, rot_tab)


def _attn_kernel(q_ref, kc_ref, kp_ref, vc_ref, vp_ref, sink_ref, o_ref):
    i = pl.program_id(1)
    blk = ATTN_BLOCK
    qi = lax.broadcasted_iota(jnp.int32, (blk, 2 * blk), 0)
    kj = lax.broadcasted_iota(jnp.int32, (blk, 2 * blk), 1)
    allowed = (kj > qi) & (kj <= qi + blk) & ((kj >= blk) | (i > 0))
    lane = lax.broadcasted_iota(jnp.int32, (1, LANES), 1)
    lo = (lane < HEAD_DIM).astype(BF16)
    hi = (lane >= HEAD_DIM).astype(BF16)
    halves = []
    for g in range(N_KV_HEADS):
        sl = slice(g * LANES, (g + 1) * LANES)
        kcat = jnp.concatenate([kp_ref[:, sl], kc_ref[:, sl]], axis=0)
        vcat = jnp.concatenate([vp_ref[:, sl], vc_ref[:, sl]], axis=0)
        halves.append(((kcat * lo, vcat * lo), (kcat * hi, vcat * hi)))
    heads = [(c, half) for c in range(ATTN_WIDTH // LANES) for half in range(2)]
    scores = [_dot_nt(q_ref[:, c * LANES:(c + 1) * LANES], halves[c // 2][half][0]) for c, half in heads]
    probs, denoms = [], []
    for (c, half), s in zip(heads, scores):
        sink = sink_ref[2 * c + half]
        s = jnp.where(allowed, s, NEG_INF)
        m = jnp.maximum(jnp.max(s, axis=-1, keepdims=True), sink)
        p = jnp.exp(s - m)
        denoms.append(jnp.sum(p, axis=-1, keepdims=True) + jnp.exp(sink - m))
        probs.append(p.astype(BF16))
    outs = [_dot(p, halves[c // 2][half][1]) / d for (c, half), p, d in zip(heads, probs, denoms)]
    for c in range(ATTN_WIDTH // LANES):
        o_ref[:, c * LANES:(c + 1) * LANES] = (outs[2 * c] + outs[2 * c + 1]).astype(BF16)


def _attention(qkv, sinks, B, S):
    nb = S // ATTN_BLOCK
    blk = ATTN_BLOCK
    kcol = ATTN_WIDTH // (2 * KV_WIDTH)
    vcol = kcol + 1
    cur = lambda col: (lambda b, i: (b * nb + i, col))
    prv = lambda col: (lambda b, i: (jnp.maximum(b * nb + i - 1, 0), col))
    return pl.pallas_call(
        _attn_kernel,
        grid=(B, nb),
        in_specs=[pl.BlockSpec((blk, ATTN_WIDTH), lambda b, i: (b * nb + i, 0)),
                  pl.BlockSpec((blk, 2 * KV_WIDTH), cur(kcol)),
                  pl.BlockSpec((blk, 2 * KV_WIDTH), prv(kcol)),
                  pl.BlockSpec((blk, 2 * KV_WIDTH), cur(vcol)),
                  pl.BlockSpec((blk, 2 * KV_WIDTH), prv(vcol)),
                  pl.BlockSpec(memory_space=pltpu.SMEM)],
        out_specs=pl.BlockSpec((blk, ATTN_WIDTH), lambda b, i: (b * nb + i, 0)),
        out_shape=jax.ShapeDtypeStruct((B * S, ATTN_WIDTH), BF16),
        compiler_params=pltpu.CompilerParams(dimension_semantics=("arbitrary", "arbitrary"),
                                             vmem_limit_bytes=VMEM_LIMIT),
        name="attn",
    )(qkv, qkv, qkv, qkv, qkv, sinks)


def _rwkv_kernel(rw_ref, vec_ref, w2_ref, a2_ref, g2_ref, o_ref, state_ref, *, n_chunk):
    j = pl.program_id(1)
    C = RWKV_CHUNK
    W = RWKV_WIDTH
    n_pair = W // LANES

    @pl.when(j == 0)
    def _():
        state_ref[...] = jnp.zeros_like(state_ref)

    w0 = vec_ref[0:1, :]
    a0 = vec_ref[1:2, :]
    k_k = vec_ref[2:3, :]
    k_a = vec_ref[3:4, :]
    r_k = vec_ref[4:5, :]
    ln_w = vec_ref[5:6, :]
    ln_b = vec_ref[6:7, :]

    r = rw_ref[:, 0:W]
    k = rw_ref[:, W:2 * W]
    v = rw_ref[:, 2 * W:3 * W]
    wl = rw_ref[:, 3 * W:3 * W + LANES]
    al = rw_ref[:, 3 * W + LANES:3 * W + 2 * LANES]
    gl = rw_ref[:, 3 * W + 2 * LANES:3 * W + 3 * LANES]

    ri = lax.broadcasted_iota(jnp.int32, (LANES, LANES), 0)
    ci = lax.broadcasted_iota(jnp.int32, (LANES, LANES), 1)
    same = (ri // HEAD_DIM) == (ci // HEAD_DIM)
    strict = same & ((ri % HEAD_DIM) > (ci % HEAD_DIM))
    incl = same & ((ri % HEAD_DIM) >= (ci % HEAD_DIM))
    ones_bd = same.astype(BF16)
    lane = lax.broadcasted_iota(jnp.int32, (1, LANES), 1)
    m0 = (lane < HEAD_DIM).astype(F32)
    m1 = 1.0 - m0
    tri = (lax.broadcasted_iota(jnp.int32, (C, C), 0) >= lax.broadcasted_iota(jnp.int32, (C, C), 1)).astype(BF16)

    def head_sum(xv):
        return jnp.concatenate(
            [_dot_exact_rhs(xv[:, p * LANES:(p + 1) * LANES], ones_bd) for p in range(n_pair)], axis=1)

    def stack2(xp):
        return jnp.concatenate([xp * m0, xp * m1], axis=0)

    z = w0 + _dot(jnp.tanh(wl).astype(BF16), w2_ref[...])
    lw = -math.exp(-0.5) * _sigmoid(z)
    a = _sigmoid(a0 + _dot(al.astype(BF16), a2_ref[...]))
    g = _dot(_sigmoid(gl).astype(BF16), g2_ref[...])
    kk = k * k_k
    kkn = kk / jnp.maximum(jnp.sqrt(head_sum(kk * kk)), 1e-12)
    k2 = k * (1.0 + (a - 1.0) * k_a)
    av = -kkn
    bv = kkn * a
    bonus = head_sum(r * k2 * r_k) * v

    eye = (ri == ci).astype(F32)
    bf = lambda t: t.astype(BF16)

    pre = []
    for c in range(n_chunk):
        rows = slice(c * C, (c + 1) * C)
        lwc = lw[rows]
        cw = _dot_exact_lhs(tri, lwc)
        cwl = cw[C - 1:C, :]
        e_in = jnp.exp(cw)
        e_neg = jnp.exp(-cw)
        e_rem = jnp.exp(cwl - cw)
        wc = jnp.exp(cwl)
        Rt = r[rows] * e_in
        At = av[rows] * jnp.exp(cw - lwc)
        Bb = bv[rows] * e_neg
        Kb = k2[rows] * e_neg
        Bh = bv[rows] * e_rem
        Kh = k2[rows] * e_rem
        vc = v[rows]
        for p in range(n_pair):
            sl = slice(p * LANES, (p + 1) * LANES)
            pre.append(dict(At=At[:, sl], Rt=Rt[:, sl], Bb=Bb[:, sl], Kb=Kb[:, sl], Bh=Bh[:, sl], Kh=Kh[:, sl],
                            v=vc[:, sl], wc=wc[:, sl]))

    for u in pre:
        u["at_bd"] = stack2(u["At"])
        lhs = bf(jnp.concatenate([u["at_bd"], stack2(u["Rt"])], axis=0))
        rhs = bf(jnp.concatenate([stack2(u["Bb"]), stack2(u["Kb"])], axis=0))
        u["G"] = _dot_nt(lhs, rhs)
    for u in pre:
        G = u.pop("G")
        u["a_ab"] = jnp.where(strict, G[0:2 * C, 0:2 * C], 0.0)
        u["a_ak"] = bf(jnp.where(strict, G[0:2 * C, 2 * C:4 * C], 0.0))
        u["a_rb"] = bf(jnp.where(incl, G[2 * C:4 * C, 0:2 * C], 0.0))
        u["a_rk"] = bf(jnp.where(incl, G[2 * C:4 * C, 2 * C:4 * C], 0.0))
        u["v_bd"] = bf(stack2(u["v"]))
    for u in pre:
        xb = bf(u["a_ab"])
        u["P"] = eye + u.pop("a_ab")
        u["X"] = _dot(xb, xb)
        u["M0"] = _dot(u["a_ak"], u["v_bd"])
    for _ in range(int(math.log2(C)) - 2):
        for u in pre:
            Wm = _dot(bf(u["X"]), bf(jnp.concatenate([u["P"], u["X"]], axis=1)))
            u["P"] = u["P"] + Wm[:, 0:LANES]
            u["X"] = Wm[:, LANES:2 * LANES]
    for u in pre:
        u["P"] = bf(u["P"] + _dot(bf(u.pop("X")), bf(u["P"])))
    for u in pre:
        u["M1"] = _dot(u["P"], bf(u.pop("M0")))
        u["Q"] = bf(_dot(u["a_rb"], u["P"]))
        u["PtB"] = _dot_tn(u["P"], bf(stack2(u["Bh"])))
    for u in pre:
        M1 = u.pop("M1")
        u["Y0"] = _dot(jnp.concatenate([u["a_rb"], u["a_rk"]], axis=1),
                       jnp.concatenate([bf(M1), u["v_bd"]], axis=0))
        u["Tm"] = bf(_dot_tn(bf(u["at_bd"]), bf(u.pop("PtB"))))
        m1_pair = M1[0:C] + M1[C:2 * C]
        cst = _dot_tn(bf(jnp.concatenate([m1_pair, u["v"]], axis=0)),
                      bf(jnp.concatenate([u["Bh"], u["Kh"]], axis=0)))
        u["cst"] = jnp.where(same, cst, 0.0)
        u["ar"] = bf(jnp.concatenate([u["At"], u["Rt"]], axis=0))

    states = [state_ref[p] for p in range(n_pair)]
    for c in range(n_chunk):
        rows = slice(c * C, (c + 1) * C)
        us = pre[c * n_pair:(c + 1) * n_pair]
        sbs = [bf(S) for S in states]
        zs = [_dot_nt(u["ar"], sb) for u, sb in zip(us, sbs)]
        new_states = [S * u["wc"] + _dot(sb, u["Tm"]) + u["cst"] for u, S, sb in zip(us, states, sbs)]
        ybds = [stack2(Z[C:2 * C]) + _dot(u["Q"], bf(stack2(Z[0:C]))) + u["Y0"] for u, Z in zip(us, zs)]
        ys = [y_bd[0:C] + y_bd[C:2 * C] for y_bd in ybds]
        states = new_states
        y = jnp.concatenate(ys, axis=1)
        mu = head_sum(y) * (1.0 / HEAD_DIM)
        yc = y - mu
        var = head_sum(yc * yc) * (1.0 / HEAD_DIM)
        yn = yc * lax.rsqrt(var + RWKV_GN_EPS) * ln_w + ln_b
        o_ref[rows, :] = ((yn + bonus[rows]) * g[rows]).astype(BF16)
    for p in range(n_pair):
        state_ref[p] = states[p]


def _rwkv(rw, vecs, w2, a2, g2, B, S, lb):
    nt = S // lb
    return pl.pallas_call(
        functools.partial(_rwkv_kernel, n_chunk=lb // RWKV_CHUNK),
        grid=(B, nt),
        in_specs=[pl.BlockSpec((lb, RWKV_PROJ), lambda b, j: (b * nt + j, 0)),
                  pl.BlockSpec((8, RWKV_WIDTH), lambda b, j: (0, 0)),
                  pl.BlockSpec((LANES, RWKV_WIDTH), lambda b, j: (0, 0)),
                  pl.BlockSpec((LANES, RWKV_WIDTH), lambda b, j: (0, 0)),
                  pl.BlockSpec((LANES, RWKV_WIDTH), lambda b, j: (0, 0))],
        out_specs=pl.BlockSpec((lb, RWKV_WIDTH), lambda b, j: (b * nt + j, 0)),
        out_shape=jax.ShapeDtypeStruct((B * S, RWKV_WIDTH), BF16),
        scratch_shapes=[pltpu.VMEM((RWKV_WIDTH // LANES, LANES, LANES), F32)],
        compiler_params=pltpu.CompilerParams(dimension_semantics=("arbitrary", "arbitrary"),
                                             vmem_limit_bytes=VMEM_LIMIT),
        name="rwkv",
    )(rw, vecs, w2, a2, g2)


def _mix_kernel(at_ref, rk_ref, x_ref, mod_ref, wo_a_ref, wo_r_ref, ln_ref, wr_ref, br_ref,
                x1_ref, h2_ref, ti_ref, gt_ref, rank_ref, cnt_ref, tb_ref, base_ref):
    first = (pl.program_id(0) == 0) & (pl.program_id(1) == 0)

    @pl.when(first)
    def _():
        base_ref[...] = jnp.zeros_like(base_ref)

    mod = mod_ref[0]
    y = _dot(at_ref[...], wo_a_ref[...]) + _dot(rk_ref[...], wo_r_ref[...])
    x = x_ref[...]
    tm = x.shape[0]
    x1 = _layer_norm(DEEPNORM_ALPHA * x + (1.0 + mod[2:3, :]) * y) * ln_ref[0:1, :] + ln_ref[1:2, :]
    h2 = _layer_norm(x1) * (1.0 + mod[4:5, :]) + mod[3:4, :]
    x1_ref[...] = x1
    h2_ref[...] = h2

    logits = jnp.dot(h2, wr_ref[...], preferred_element_type=F32,
                     precision=lax.Precision.HIGHEST) + br_ref[...]
    lt = jnp.transpose(logits)[0:N_EXPERTS, :]
    erow = lax.broadcasted_iota(jnp.int32, (N_EXPERTS, tm), 0).astype(F32)
    cur = lt
    vals, idxs = [], []
    for _ in range(TOP_K):
        m = jnp.max(cur, axis=0, keepdims=True)
        idx = jnp.min(jnp.where(cur == m, erow, float(N_EXPERTS)), axis=0, keepdims=True)
        vals.append(m)
        idxs.append(idx)
        cur = jnp.where(erow == idx, -jnp.inf, cur)
    tv = jnp.concatenate(vals, axis=0)
    e = jnp.exp(tv - tv[0:1, :])
    gt_ref[...] = e / jnp.sum(e, axis=0, keepdims=True)
    ti_ref[...] = jnp.concatenate(idxs, axis=0).astype(jnp.int32)

    onehot = jnp.zeros((N_EXPERTS, tm), F32)
    for idx in idxs:
        onehot = onehot + (erow == idx).astype(F32)
    before = (lax.broadcasted_iota(jnp.int32, (tm, tm), 0)
              < lax.broadcasted_iota(jnp.int32, (tm, tm), 1)).astype(BF16)
    tot = base_ref[:, 0:1] + _dot(onehot.astype(BF16), before)
    ranks = [jnp.sum(jnp.where(erow == idx, tot, 0.0), axis=0, keepdims=True) for idx in idxs]
    rank_ref[...] = jnp.concatenate(ranks, axis=0).astype(jnp.int32)
    tb_ref[0] = base_ref[...].astype(jnp.int32)
    base_ref[...] = base_ref[...] + jnp.sum(onehot, axis=1, keepdims=True)
    cnt_ref[...] = base_ref[...].astype(jnp.int32)


def _mix(attn_out, rwkv_out, x2d, mod, wo_a, wo_r, ln1, w_router, b_router, B, S, tm):
    D = x2d.shape[1]
    nt = S // tm
    T = B * S
    tok = lambda b, j: (b * nt + j, 0)
    col = lambda b, j: (0, b * nt + j)
    fixed = lambda b, j: (0, 0)
    return pl.pallas_call(
        _mix_kernel,
        grid=(B, nt),
        in_specs=[pl.BlockSpec((tm, ATTN_WIDTH), tok),
                  pl.BlockSpec((tm, RWKV_WIDTH), tok),
                  pl.BlockSpec((tm, D), tok),
                  pl.BlockSpec((1, 6, D), lambda b, j: (b, 0, 0)),
                  pl.BlockSpec((ATTN_WIDTH, D), fixed),
                  pl.BlockSpec((RWKV_WIDTH, D), fixed),
                  pl.BlockSpec((2, D), fixed),
                  pl.BlockSpec((D, LANES), fixed),
                  pl.BlockSpec((1, LANES), fixed)],
        out_specs=[pl.BlockSpec((tm, D), tok),
                   pl.BlockSpec((tm, D), tok),
                   pl.BlockSpec((TOP_K, tm), col),
                   pl.BlockSpec((TOP_K, tm), col),
                   pl.BlockSpec((TOP_K, tm), col),
                   pl.BlockSpec((N_EXPERTS, LANES), fixed),
                   pl.BlockSpec((1, N_EXPERTS, LANES), lambda b, j: (b * nt + j, 0, 0))],
        out_shape=[jax.ShapeDtypeStruct((T, D), F32),
                   jax.ShapeDtypeStruct((T, D), F32),
                   jax.ShapeDtypeStruct((TOP_K, T), jnp.int32),
                   jax.ShapeDtypeStruct((TOP_K, T), F32),
                   jax.ShapeDtypeStruct((TOP_K, T), jnp.int32),
                   jax.ShapeDtypeStruct((N_EXPERTS, LANES), jnp.int32),
                   jax.ShapeDtypeStruct((T // tm, N_EXPERTS, LANES), jnp.int32)],
        scratch_shapes=[pltpu.VMEM((N_EXPERTS, LANES), F32)],
        compiler_params=pltpu.CompilerParams(dimension_semantics=("arbitrary", "arbitrary"),
                                             vmem_limit_bytes=VMEM_LIMIT),
        name="mix",
    )(attn_out, rwkv_out, x2d, mod, wo_a, wo_r, ln1, w_router, b_router)


RUN_PIECES = tuple(2 ** b for b in range(int(math.log2(MOE_TILE)), -1, -1))
SUBLANES = 8


def _to_tiles(ref, x):
    n = x.shape[0]
    for c in range(SUBLANES):
        ref[pl.ds(c, n, stride=SUBLANES), :] = x[:, c * LANES:(c + 1) * LANES]


def _from_tiles(ref):
    n = ref.shape[0] // SUBLANES
    return jnp.concatenate([ref[pl.ds(c, n, stride=SUBLANES), :] for c in range(SUBLANES)], axis=1)


def _run_copies(n, local, local_start, remote, remote_start, sem, to_remote):
    off = 0
    for piece in RUN_PIECES:
        take = (n & piece) != 0

        @pl.when(take)
        def _(off=off, piece=piece):
            lo = pl.multiple_of((local_start + off) * SUBLANES, SUBLANES)
            ro = pl.multiple_of((remote_start + off) * SUBLANES, SUBLANES)
            loc = local.at[pl.ds(lo, piece * SUBLANES)]
            rem = remote.at[pl.ds(ro, piece * SUBLANES)]
            src, dst = (loc, rem) if to_remote else (rem, loc)
            pltpu.make_async_copy(src, dst, sem).start()

        off = off + (n & piece)


def _dispatch_kernel(tcnt_ref, lstart_ref, gstart_ref, lpos_ref, h2_ref, xs_in_ref, xs_ref, xbuf, sem):
    del xs_in_ref
    i = pl.program_id(0)
    tm = h2_ref.shape[0]
    n_loc = TOP_K * tm
    slot = lax.broadcasted_iota(jnp.int32, (n_loc, tm), 0)
    lpos = lpos_ref[...]
    perm = jnp.zeros((n_loc, tm), F32)
    for k in range(TOP_K):
        perm = perm + (slot == lpos[k:k + 1, :]).astype(F32)
    _to_tiles(xbuf, _dot(perm.astype(BF16), h2_ref[...].astype(BF16)))

    def issue(e, carry):
        idx = i * N_EXPERTS + e
        _run_copies(tcnt_ref[idx], xbuf, lstart_ref[idx], xs_ref, gstart_ref[idx], sem, True)
        return carry

    lax.fori_loop(0, N_EXPERTS, issue, 0)
    pltpu.make_async_copy(xbuf, xs_ref.at[pl.ds(0, n_loc * SUBLANES)], sem).wait()


def _dispatch(tcnt, lstart, gstart, lpos, h2, n_rows, tm):
    T, D = h2.shape
    xs0 = jnp.zeros((n_rows * SUBLANES, LANES), F32)
    grid_spec = pltpu.PrefetchScalarGridSpec(
        num_scalar_prefetch=3,
        grid=(T // tm,),
        in_specs=[pl.BlockSpec((TOP_K, tm), lambda i, *_: (0, i)),
                  pl.BlockSpec((tm, D), lambda i, *_: (i, 0)),
                  pl.BlockSpec(memory_space=pl.ANY)],
        out_specs=pl.BlockSpec(memory_space=pl.ANY),
        scratch_shapes=[pltpu.VMEM((TOP_K * tm * SUBLANES, LANES), F32), pltpu.SemaphoreType.DMA(())],
    )
    return pl.pallas_call(
        _dispatch_kernel,
        grid_spec=grid_spec,
        out_shape=jax.ShapeDtypeStruct((n_rows * SUBLANES, LANES), F32),
        input_output_aliases={5: 0},
        compiler_params=pltpu.CompilerParams(dimension_semantics=("arbitrary",),
                                             vmem_limit_bytes=VMEM_LIMIT),
        name="dispatch",
    )(tcnt, lstart, gstart, lpos, h2, xs0)


def _experts_kernel(blk_e_ref, n_used_ref, xs_ref, wgu_ref, bgu_ref, wd_ref, bd_ref, ys_ref):
    i = pl.program_id(0)
    d_ff = wd_ref.shape[1]

    @pl.when(i < n_used_ref[0])
    def _():
        xb = _from_tiles(xs_ref).astype(BF16)
        gu = _dot(xb, wgu_ref[0]) + bgu_ref[0]
        gate = jnp.minimum(gu[:, :d_ff], SWIGLU_LIMIT)
        up = jnp.clip(gu[:, d_ff:], -SWIGLU_LIMIT, SWIGLU_LIMIT)
        act = (up + 1.0) * (gate * _sigmoid(SWIGLU_ALPHA * gate))
        _to_tiles(ys_ref, _dot(act.astype(BF16), wd_ref[0]) + bd_ref[0])

    @pl.when(i >= n_used_ref[0])
    def _():
        ys_ref[...] = jnp.zeros_like(ys_ref)


def _experts(blk_e, n_used, xs, wgu, bgu, wd, bd):
    d_ff, D = wd.shape[1], wd.shape[2]
    n_blocks = xs.shape[0] // (MOE_BLOCK * SUBLANES)
    blk = (MOE_BLOCK * SUBLANES, LANES)

    def last_used(i, n_used_ref):
        return jnp.minimum(i, jnp.maximum(n_used_ref[0] - 1, 0))

    def row_map(i, blk_e_ref, n_used_ref):
        return (last_used(i, n_used_ref), 0)

    def exp_map(i, blk_e_ref, n_used_ref):
        return (blk_e_ref[last_used(i, n_used_ref)], 0, 0)

    grid_spec = pltpu.PrefetchScalarGridSpec(
        num_scalar_prefetch=2,
        grid=(n_blocks,),
        in_specs=[pl.BlockSpec(blk, row_map),
                  pl.BlockSpec((1, D, 2 * d_ff), exp_map),
                  pl.BlockSpec((1, 1, 2 * d_ff), exp_map),
                  pl.BlockSpec((1, d_ff, D), exp_map),
                  pl.BlockSpec((1, 1, D), exp_map)],
        out_specs=pl.BlockSpec(blk, lambda i, blk_e_ref, n_used_ref: (i, 0)),
    )
    return pl.pallas_call(
        _experts_kernel,
        grid_spec=grid_spec,
        out_shape=jax.ShapeDtypeStruct(xs.shape, F32),
        compiler_params=pltpu.CompilerParams(dimension_semantics=("arbitrary",),
                                             vmem_limit_bytes=VMEM_LIMIT),
        name="experts",
    )(blk_e, n_used, xs, wgu, bgu, wd, bd)


def _combine_kernel(tcnt_ref, lstart_ref, gstart_ref, ys_ref, lpos_ref, gt_ref, x1_ref, mod_ref, ln_ref,
                    o_ref, buf, sem):
    nt = pl.num_programs(1)
    i = pl.program_id(0) * nt + pl.program_id(1)
    tm = x1_ref.shape[0]
    n_loc = TOP_K * tm

    def issue(e, carry):
        idx = i * N_EXPERTS + e
        _run_copies(tcnt_ref[idx], buf, lstart_ref[idx], ys_ref, gstart_ref[idx], sem, False)
        return carry

    lax.fori_loop(0, N_EXPERTS, issue, 0)
    slot = lax.broadcasted_iota(jnp.int32, (tm, n_loc), 1)
    lpos = lpos_ref[...]
    gt = gt_ref[...]
    pick = jnp.zeros((tm, n_loc), F32)
    for k in range(TOP_K):
        pick = pick + jnp.where(slot == lpos[:, k:k + 1], gt[:, k:k + 1], 0.0)
    pltpu.make_async_copy(ys_ref.at[pl.ds(0, n_loc * SUBLANES)], buf, sem).wait()
    y = _dot(pick.astype(BF16), _from_tiles(buf).astype(BF16))
    mod = mod_ref[0]
    z = DEEPNORM_ALPHA * x1_ref[...] + (1.0 + mod[5:6, :]) * y
    o_ref[...] = _layer_norm(z) * ln_ref[0:1, :] + ln_ref[1:2, :]


def _combine(tcnt, lstart, gstart, ys, lpos_t, gates_t, x1, mod, ln2, B, S, tm):
    T, D = x1.shape
    nt = S // tm
    tok = lambda b, j, *_: (b * nt + j, 0)
    grid_spec = pltpu.PrefetchScalarGridSpec(
        num_scalar_prefetch=3,
        grid=(B, nt),
        in_specs=[pl.BlockSpec(memory_space=pl.ANY),
                  pl.BlockSpec((tm, TOP_K), tok),
                  pl.BlockSpec((tm, TOP_K), tok),
                  pl.BlockSpec((tm, D), tok),
                  pl.BlockSpec((1, 6, D), lambda b, j, *_: (b, 0, 0)),
                  pl.BlockSpec((2, D), lambda b, j, *_: (0, 0))],
        out_specs=pl.BlockSpec((tm, D), tok),
        scratch_shapes=[pltpu.VMEM((TOP_K * tm * SUBLANES, LANES), F32), pltpu.SemaphoreType.DMA(())],
    )
    return pl.pallas_call(
        _combine_kernel,
        grid_spec=grid_spec,
        out_shape=jax.ShapeDtypeStruct((T, D), F32),
        compiler_params=pltpu.CompilerParams(dimension_semantics=("arbitrary", "arbitrary"),
                                             vmem_limit_bytes=VMEM_LIMIT),
        name="combine",
    )(tcnt, lstart, gstart, ys, lpos_t, gates_t, x1, mod, ln2)


def _pad_rows(w, rows):
    return jnp.pad(w, ((0, rows - w.shape[0]), (0, 0)))


def _pad_cols(w, cols):
    return jnp.pad(w, ((0, 0), (0, cols - w.shape[1])))


def _layer(x, c, positions, w_ada, b_ada, w_in, shift_mu, rwkv_w0, rwkv_w2, rwkv_a0, rwkv_a2, rwkv_g2,
           rwkv_k_k, rwkv_k_a, rwkv_r_k, rwkv_ln_w, rwkv_ln_b, attn_sinks, w_out, ln1_g, ln1_b,
           w_router, b_router, w_gate_up, b_gate_up, w_down, b_down, ln2_g, ln2_b):
    B, S, D = x.shape
    T = B * S
    tile = min(256, S)

    q0, k0, v0 = 0, ATTN_WIDTH, ATTN_WIDTH + KV_WIDTH
    r0 = ATTN_WIDTH + 2 * KV_WIDTH
    heads = lambda base: [w_in[:, base + h * HEAD_DIM: base + (h + 1) * HEAD_DIM] for h in range(N_KV_HEADS)]
    dup = lambda hs: [w for w in hs for _ in range(2)]
    w_attn = jnp.concatenate([w_in[:, q0:q0 + ATTN_WIDTH]] + dup(heads(k0)) + dup(heads(v0)), axis=1).astype(BF16)
    lora0 = r0 + 3 * RWKV_WIDTH
    lora = (DECAY_LORA, AAA_LORA, GATE_LORA)
    pieces_w = [w_in[:, r0:lora0]]
    pieces_mu = [shift_mu[None, 0:3 * RWKV_WIDTH]]
    off = lora0
    for n in lora:
        pieces_w.append(_pad_cols(w_in[:, off:off + n], LANES))
        pieces_mu.append(_pad_cols(shift_mu[None, off - r0:off - r0 + n], LANES))
        off += n
    w_rwkv = jnp.concatenate(pieces_w, axis=1).astype(BF16)
    mu = jnp.concatenate(pieces_mu, axis=1)
    inv_freq = ROPE_THETA ** (-jnp.arange(0, ROT_DIM, 2, dtype=F32) / ROT_DIM)
    lane_p = jnp.arange(LANES) % HEAD_DIM
    rot_tab = jnp.zeros((8, LANES), F32)
    rot_tab = rot_tab.at[0].set(jnp.where(lane_p < ROT_DIM, inv_freq[lane_p % (ROT_DIM // 2)], 0.0))
    rot_tab = rot_tab.at[1].set(jnp.where(lane_p < ROT_DIM // 2, -1.0, 0.0))
    rot_tab = rot_tab.at[2].set(jnp.where((lane_p >= ROT_DIM // 2) & (lane_p < ROT_DIM), 1.0, 0.0))
    vecs = jnp.stack([rwkv_w0, rwkv_a0, rwkv_k_k, rwkv_k_a, rwkv_r_k.reshape(-1), rwkv_ln_w, rwkv_ln_b,
                      jnp.zeros_like(rwkv_w0)])
    w2 = _pad_rows(rwkv_w2, LANES).astype(BF16)
    a2 = _pad_rows(rwkv_a2, LANES).astype(BF16)
    g2 = _pad_rows(rwkv_g2, LANES).astype(BF16)
    wo_a = w_out[:ATTN_WIDTH].astype(BF16)
    wo_r = w_out[ATTN_WIDTH:].astype(BF16)
    w_r = _pad_cols(w_router, LANES)
    b_r = jnp.concatenate([b_router, jnp.full((LANES - N_EXPERTS,), NEG_INF, F32)])[None, :]
    wgu = w_gate_up.astype(BF16)
    wd = w_down.astype(BF16)

    mod = _mod(c, w_ada, b_ada).reshape(B, 6, D)
    qkv, rw = _inproj(x, positions, mod, w_attn, w_rwkv, mu, rot_tab, tile)
    attn_out = _attention(qkv, attn_sinks, B, S)
    rwkv_out = _rwkv(rw, vecs, w2, a2, g2, B, S, min(128, S))

    mtile = min(MOE_TILE, S)
    x1, h2, top_i, gates, rank, cnt, tbase = _mix(attn_out, rwkv_out, x.reshape(T, D), mod, wo_a, wo_r,
                                                   jnp.stack([ln1_g, ln1_b]), w_r, b_r, B, S, mtile)

    counts = cnt[:, 0]
    padded = (counts + MOE_BLOCK - 1) // MOE_BLOCK * MOE_BLOCK
    pend = jnp.cumsum(padded)
    pstart = pend - padded
    n_blocks = T * TOP_K // MOE_BLOCK + N_EXPERTS
    blk_row = jnp.arange(n_blocks, dtype=jnp.int32) * MOE_BLOCK
    blk_e = jnp.minimum(jnp.sum((pend[None, :] <= blk_row[:, None]).astype(jnp.int32), axis=1), N_EXPERTS - 1)
    n_used = (pend[-1:] // MOE_BLOCK).astype(jnp.int32)
    tb = tbase[:, :, 0]
    tcnt = jnp.concatenate([tb[1:], counts[None]], axis=0) - tb
    lstart = jnp.cumsum(tcnt, axis=1) - tcnt
    gstart = pstart[None, :] + tb
    shift = jnp.repeat(jnp.transpose(lstart - tb), mtile, axis=1)
    experts = jnp.arange(N_EXPERTS, dtype=jnp.int32)
    lpos = rank + jnp.sum(jnp.where(top_i[None] == experts[:, None, None], shift[:, None, :], 0), axis=0)
    flat = lambda a: a.reshape(-1).astype(jnp.int32)

    xs = _dispatch(flat(tcnt), flat(lstart), flat(gstart), lpos, h2, n_blocks * MOE_BLOCK, mtile)
    ys = _experts(blk_e, n_used, xs, wgu, b_gate_up[:, None, :], wd, b_down[:, None, :])
    out = _combine(flat(tcnt), flat(lstart), flat(gstart), ys, jnp.transpose(lpos), jnp.transpose(gates), x1, mod,
                   jnp.stack([ln2_g, ln2_b]), B, S, mtile)
    return out.reshape(B, S, D)


def kernel(x, c, positions, w_ada, b_ada, w_in, shift_mu, rwkv_w0, rwkv_w2, rwkv_a0, rwkv_a2, rwkv_g2,
           rwkv_k_k, rwkv_k_a, rwkv_r_k, rwkv_ln_w, rwkv_ln_b, attn_sinks, w_out, ln1_g, ln1_b,
           w_router, b_router, w_gate_up, b_gate_up, w_down, b_down, ln2_g, ln2_b):
    for l in range(DEPTH):
        x = _layer(x, c, positions, w_ada[l], b_ada[l], w_in[l], shift_mu[l], rwkv_w0[l], rwkv_w2[l],
                   rwkv_a0[l], rwkv_a2[l], rwkv_g2[l], rwkv_k_k[l], rwkv_k_a[l], rwkv_r_k[l], rwkv_ln_w[l],
                   rwkv_ln_b[l], attn_sinks[l], w_out[l], ln1_g[l], ln1_b[l], w_router[l], b_router[l],
                   w_gate_up[l], b_gate_up[l], w_down[l], b_down[l], ln2_g[l], ln2_b[l])
    return x
```

```python
import functools
import math

import jax
import jax.numpy as jnp
from jax import lax
from jax.experimental import pallas as pl
from jax.experimental.pallas import tpu as pltpu

F32 = jnp.float32
BF16 = jnp.bfloat16

HEAD_DIM = 64
N_ATTN_HEADS = 8
N_KV_HEADS = 2
N_RWKV_HEADS = 8
ATTN_WIDTH = N_ATTN_HEADS * HEAD_DIM
KV_WIDTH = N_KV_HEADS * HEAD_DIM
RWKV_WIDTH = N_RWKV_HEADS * HEAD_DIM
ATTN_BLOCK = 128
ROT_DIM = HEAD_DIM // 4
ROPE_THETA = 500000.0
DECAY_LORA = 32
AAA_LORA = 32
GATE_LORA = 96
N_EXPERTS = 32
TOP_K = 4
SWIGLU_LIMIT = 7.0
SWIGLU_ALPHA = 1.702
LN_EPS = 1e-5
RWKV_GN_EPS = 64e-5
NEG_INF = -1e30
DEPTH = 1
DEEPNORM_ALPHA = (2 * DEPTH) ** 0.25

LANES = 128
RWKV_CHUNK = 64
MOE_BLOCK = 256
MOE_TILE = 256
ATTN_PROJ = ATTN_WIDTH + 4 * KV_WIDTH
RWKV_PROJ = 3 * RWKV_WIDTH + 3 * LANES
VMEM_LIMIT = 48 * 1024 * 1024
EXPERTS_VMEM_LIMIT = 56 * 1024 * 1024


def _dot(a, b):
    return jnp.dot(a, b, preferred_element_type=F32)


def _dot_nt(a, b):
    return lax.dot_general(a, b, (((1,), (1,)), ((), ())), preferred_element_type=F32)


def _dot_tn(a, b):
    return lax.dot_general(a, b, (((0,), (0,)), ((), ())), preferred_element_type=F32)


def _split3(x):
    h = x.astype(BF16)
    r1 = x - h.astype(F32)
    m = r1.astype(BF16)
    lo = (r1 - m.astype(F32)).astype(BF16)
    return h, m, lo


def _dot_exact_rhs(x, m_bf16):
    h, m, lo = _split3(x)
    return _dot(h, m_bf16) + _dot(m, m_bf16) + _dot(lo, m_bf16)


def _dot_exact_lhs(m_bf16, x):
    h, m, lo = _split3(x)
    return _dot(m_bf16, h) + _dot(m_bf16, m) + _dot(m_bf16, lo)


def _layer_norm(x):
    mu = jnp.mean(x, axis=-1, keepdims=True)
    xc = x - mu
    var = jnp.mean(xc * xc, axis=-1, keepdims=True)
    return xc * lax.rsqrt(var + LN_EPS)


def _sigmoid(x):
    return 1.0 / (1.0 + jnp.exp(-x))


def _mod_kernel(c_ref, w_ref, b_ref, o_ref):
    c = c_ref[...]
    s = c * _sigmoid(c)
    o_ref[...] = jnp.dot(s, w_ref[...], preferred_element_type=F32,
                         precision=lax.Precision.HIGHEST) + b_ref[...]


def _mod(c, w_ada, b_ada):
    B, D = c.shape
    n = w_ada.shape[1] // D
    return pl.pallas_call(
        _mod_kernel,
        grid=(n,),
        in_specs=[pl.BlockSpec((B, D), lambda i: (0, 0)),
                  pl.BlockSpec((D, D), lambda i: (0, i)),
                  pl.BlockSpec((1, D), lambda i: (0, i))],
        out_specs=pl.BlockSpec((B, D), lambda i: (0, i)),
        out_shape=jax.ShapeDtypeStruct((B, n * D), F32),
        compiler_params=pltpu.CompilerParams(dimension_semantics=("arbitrary",),
                                             vmem_limit_bytes=VMEM_LIMIT),
        name="mod",
    )(c, w_ada, b_ada.reshape(1, -1))


def _inproj_kernel(x_ref, pos_ref, mod_ref, wa_ref, wr_ref, mu_ref, rt_ref,
                   qkv_ref, rw_ref, carry_ref):
    j = pl.program_id(1)
    x = x_ref[0]
    tm = x.shape[0]
    mod = mod_ref[0]
    h = _layer_norm(x) * (1.0 + mod[1:2, :]) + mod[0:1, :]
    hb = h.astype(BF16)

    pa = _dot(hb, wa_ref[...])
    ang = pos_ref[0].astype(F32) * rt_ref[0:1, :]
    cs = jnp.cos(ang)
    sn = jnp.sin(ang)
    m_lo = rt_ref[1:2, :]
    m_hi = rt_ref[2:3, :]
    n_q = ATTN_WIDTH // LANES
    n_rot = (ATTN_WIDTH + 2 * KV_WIDTH) // LANES
    for ch in range(n_rot):
        t = pa[:, ch * LANES:(ch + 1) * LANES]
        if ch < n_q:
            t = t * (1.0 / math.sqrt(HEAD_DIM))
        up = pltpu.roll(t, LANES - ROT_DIM // 2, 1)
        dn = pltpu.roll(t, ROT_DIM // 2, 1)
        o = t * cs + sn * (m_lo * up + m_hi * dn)
        qkv_ref[:, ch * LANES:(ch + 1) * LANES] = o.astype(BF16)
    qkv_ref[:, n_rot * LANES:] = pa[:, n_rot * LANES:].astype(BF16)

    pr = _dot(hb, wr_ref[...])
    prev = pltpu.roll(pr, 1, 0)
    row = lax.broadcasted_iota(jnp.int32, (tm, 1), 0)
    carry = jnp.where(j == 0, 0.0, carry_ref[...])
    prev = jnp.where(row == 0, carry, prev)
    carry_ref[...] = pr[tm - 1:tm, :]
    rw_ref[...] = pr + (prev - pr) * mu_ref[...]


def _inproj(x, positions, mod, w_attn, w_rwkv, mu, rot_tab, tm):
    B, S, D = x.shape
    nt = S // tm
    return pl.pallas_call(
        _inproj_kernel,
        grid=(B, nt),
        in_specs=[pl.BlockSpec((1, tm, D), lambda b, j: (b, j, 0)),
                  pl.BlockSpec((1, tm, 1), lambda b, j: (b, j, 0)),
                  pl.BlockSpec((1, 6, D), lambda b, j: (b, 0, 0)),
                  pl.BlockSpec((D, ATTN_PROJ), lambda b, j: (0, 0)),
                  pl.BlockSpec((D, RWKV_PROJ), lambda b, j: (0, 0)),
                  pl.BlockSpec((1, RWKV_PROJ), lambda b, j: (0, 0)),
                  pl.BlockSpec((8, LANES), lambda b, j: (0, 0))],
        out_specs=[pl.BlockSpec((tm, ATTN_PROJ), lambda b, j: (b * nt + j, 0)),
                   pl.BlockSpec((tm, RWKV_PROJ), lambda b, j: (b * nt + j, 0))],
        out_shape=[jax.ShapeDtypeStruct((B * S, ATTN_PROJ), BF16),
                   jax.ShapeDtypeStruct((B * S, RWKV_PROJ), F32)],
        scratch_shapes=[pltpu.VMEM((1, RWKV_PROJ), F32)],
        compiler_params=pltpu.CompilerParams(dimension_semantics=("arbitrary", "arbitrary"),
                                             vmem_limit_bytes=VMEM_LIMIT),
        name="inproj",
    )(x, positions.reshape(B, S, 1), mod, w_attn, w_rwkv, mu, rot_tab)


def _attn_kernel(q_ref, kc_ref, kp_ref, vc_ref, vp_ref, sink_ref, o_ref):
    i = pl.program_id(1)
    blk = ATTN_BLOCK
    qi = lax.broadcasted_iota(jnp.int32, (blk, 2 * blk), 0)
    kj = lax.broadcasted_iota(jnp.int32, (blk, 2 * blk), 1)
    allowed = (kj > qi) & (kj <= qi + blk) & ((kj >= blk) | (i > 0))
    lane = lax.broadcasted_iota(jnp.int32, (1, LANES), 1)
    lo = (lane < HEAD_DIM).astype(BF16)
    hi = (lane >= HEAD_DIM).astype(BF16)
    halves = []
    for g in range(N_KV_HEADS):
        sl = slice(g * LANES, (g + 1) * LANES)
        kcat = jnp.concatenate([kp_ref[:, sl], kc_ref[:, sl]], axis=0)
        vcat = jnp.concatenate([vp_ref[:, sl], vc_ref[:, sl]], axis=0)
        halves.append(((kcat * lo, vcat * lo), (kcat * hi, vcat * hi)))
    heads = [(c, half) for c in range(ATTN_WIDTH // LANES) for half in range(2)]
    scores = [_dot_nt(q_ref[:, c * LANES:(c + 1) * LANES], halves[c // 2][half][0]) for c, half in heads]
    probs, denoms = [], []
    for (c, half), s in zip(heads, scores):
        sink = sink_ref[2 * c + half]
        s = jnp.where(allowed, s, NEG_INF)
        m = jnp.maximum(jnp.max(s, axis=-1, keepdims=True), sink)
        p = jnp.exp(s - m)
        denoms.append(jnp.sum(p, axis=-1, keepdims=True) + jnp.exp(sink - m))
        probs.append(p.astype(BF16))
    outs = [_dot(p, halves[c // 2][half][1]) / d for (c, half), p, d in zip(heads, probs, denoms)]
    for c in range(ATTN_WIDTH // LANES):
        o_ref[:, c * LANES:(c + 1) * LANES] = (outs[2 * c] + outs[2 * c + 1]).astype(BF16)


def _attention(qkv, sinks, B, S):
    nb = S // ATTN_BLOCK
    blk = ATTN_BLOCK
    kcol = ATTN_WIDTH // (2 * KV_WIDTH)
    vcol = kcol + 1
    cur = lambda col: (lambda b, i: (b * nb + i, col))
    prv = lambda col: (lambda b, i: (jnp.maximum(b * nb + i - 1, 0), col))
    return pl.pallas_call(
        _attn_kernel,
        grid=(B, nb),
        in_specs=[pl.BlockSpec((blk, ATTN_WIDTH), lambda b, i: (b * nb + i, 0)),
                  pl.BlockSpec((blk, 2 * KV_WIDTH), cur(kcol)),
                  pl.BlockSpec((blk, 2 * KV_WIDTH), prv(kcol)),
                  pl.BlockSpec((blk, 2 * KV_WIDTH), cur(vcol)),
                  pl.BlockSpec((blk, 2 * KV_WIDTH), prv(vcol)),
                  pl.BlockSpec(memory_space=pltpu.SMEM)],
        out_specs=pl.BlockSpec((blk, ATTN_WIDTH), lambda b, i: (b * nb + i, 0)),
        out_shape=jax.ShapeDtypeStruct((B * S, ATTN_WIDTH), BF16),
        compiler_params=pltpu.CompilerParams(dimension_semantics=("arbitrary", "arbitrary"),
                                             vmem_limit_bytes=VMEM_LIMIT),
        name="attn",
    )(qkv, qkv, qkv, qkv, qkv, sinks)


def _rwkv_kernel(rw_ref, vec_ref, w2_ref, a2_ref, g2_ref, o_ref, state_ref, *, n_chunk):
    j = pl.program_id(1)
    C = RWKV_CHUNK
    W = RWKV_WIDTH
    n_pair = W // LANES

    @pl.when(j == 0)
    def _():
        state_ref[...] = jnp.zeros_like(state_ref)

    w0 = vec_ref[0:1, :]
    a0 = vec_ref[1:2, :]
    k_k = vec_ref[2:3, :]
    k_a = vec_ref[3:4, :]
    r_k = vec_ref[4:5, :]
    ln_w = vec_ref[5:6, :]
    ln_b = vec_ref[6:7, :]

    r = rw_ref[:, 0:W]
    k = rw_ref[:, W:2 * W]
    v = rw_ref[:, 2 * W:3 * W]
    wl = rw_ref[:, 3 * W:3 * W + LANES]
    al = rw_ref[:, 3 * W + LANES:3 * W + 2 * LANES]
    gl = rw_ref[:, 3 * W + 2 * LANES:3 * W + 3 * LANES]

    ri = lax.broadcasted_iota(jnp.int32, (LANES, LANES), 0)
    ci = lax.broadcasted_iota(jnp.int32, (LANES, LANES), 1)
    same = (ri // HEAD_DIM) == (ci // HEAD_DIM)
    strict = same & ((ri % HEAD_DIM) > (ci % HEAD_DIM))
    incl = same & ((ri % HEAD_DIM) >= (ci % HEAD_DIM))
    ones_bd = same.astype(BF16)
    lane = lax.broadcasted_iota(jnp.int32, (1, LANES), 1)
    m0 = (lane < HEAD_DIM).astype(F32)
    m1 = 1.0 - m0
    tri = (lax.broadcasted_iota(jnp.int32, (C, C), 0) >= lax.broadcasted_iota(jnp.int32, (C, C), 1)).astype(BF16)

    def head_sum(xv):
        return jnp.concatenate(
            [_dot_exact_rhs(xv[:, p * LANES:(p + 1) * LANES], ones_bd) for p in range(n_pair)], axis=1)

    def stack2(xp):
        return jnp.concatenate([xp * m0, xp * m1], axis=0)

    z = w0 + _dot(jnp.tanh(wl).astype(BF16), w2_ref[...])
    lw = -math.exp(-0.5) * _sigmoid(z)
    a = _sigmoid(a0 + _dot(al.astype(BF16), a2_ref[...]))
    g = _dot(_sigmoid(gl).astype(BF16), g2_ref[...])
    kk = k * k_k
    kkn = kk / jnp.maximum(jnp.sqrt(head_sum(kk * kk)), 1e-12)
    k2 = k * (1.0 + (a - 1.0) * k_a)
    av = -kkn
    bv = kkn * a
    bonus = head_sum(r * k2 * r_k) * v

    eye = (ri == ci).astype(F32)
    bf = lambda t: t.astype(BF16)

    pre = []
    for c in range(n_chunk):
        rows = slice(c * C, (c + 1) * C)
        lwc = lw[rows]
        cw = _dot_exact_lhs(tri, lwc)
        cwl = cw[C - 1:C, :]
        e_in = jnp.exp(cw)
        e_neg = jnp.exp(-cw)
        e_rem = jnp.exp(cwl - cw)
        wc = jnp.exp(cwl)
        Rt = r[rows] * e_in
        At = av[rows] * jnp.exp(cw - lwc)
        Bb = bv[rows] * e_neg
        Kb = k2[rows] * e_neg
        Bh = bv[rows] * e_rem
        Kh = k2[rows] * e_rem
        vc = v[rows]
        for p in range(n_pair):
            sl = slice(p * LANES, (p + 1) * LANES)
            pre.append(dict(At=At[:, sl], Rt=Rt[:, sl], Bb=Bb[:, sl], Kb=Kb[:, sl], Bh=Bh[:, sl], Kh=Kh[:, sl],
                            v=vc[:, sl], wc=wc[:, sl]))

    for u in pre:
        u["at_bd"] = stack2(u["At"])
        lhs = bf(jnp.concatenate([u["at_bd"], stack2(u["Rt"])], axis=0))
        rhs = bf(jnp.concatenate([stack2(u["Bb"]), stack2(u["Kb"])], axis=0))
        u["G"] = _dot_nt(lhs, rhs)
    for u in pre:
        G = u.pop("G")
        u["a_ab"] = jnp.where(strict, G[0:2 * C, 0:2 * C], 0.0)
        u["a_ak"] = bf(jnp.where(strict, G[0:2 * C, 2 * C:4 * C], 0.0))
        u["a_rb"] = bf(jnp.where(incl, G[2 * C:4 * C, 0:2 * C], 0.0))
        u["a_rk"] = bf(jnp.where(incl, G[2 * C:4 * C, 2 * C:4 * C], 0.0))
        u["v_bd"] = bf(stack2(u["v"]))
    for u in pre:
        xb = bf(u["a_ab"])
        u["P"] = eye + u.pop("a_ab")
        u["X"] = _dot(xb, xb)
        u["M0"] = _dot(u["a_ak"], u["v_bd"])
    for _ in range(int(math.log2(C)) - 2):
        for u in pre:
            Wm = _dot(bf(u["X"]), bf(jnp.concatenate([u["P"], u["X"]], axis=1)))
            u["P"] = u["P"] + Wm[:, 0:LANES]
            u["X"] = Wm[:, LANES:2 * LANES]
    for u in pre:
        u["P"] = bf(u["P"] + _dot(bf(u.pop("X")), bf(u["P"])))
    for u in pre:
        u["M1"] = _dot(u["P"], bf(u.pop("M0")))
        u["Q"] = bf(_dot(u["a_rb"], u["P"]))
        u["PtB"] = _dot_tn(u["P"], bf(stack2(u["Bh"])))
    for u in pre:
        M1 = u.pop("M1")
        u["Y0"] = _dot(jnp.concatenate([u["a_rb"], u["a_rk"]], axis=1),
                       jnp.concatenate([bf(M1), u["v_bd"]], axis=0))
        u["Tm"] = bf(_dot_tn(bf(u["at_bd"]), bf(u.pop("PtB"))))
        m1_pair = M1[0:C] + M1[C:2 * C]
        cst = _dot_tn(bf(jnp.concatenate([m1_pair, u["v"]], axis=0)),
                      bf(jnp.concatenate([u["Bh"], u["Kh"]], axis=0)))
        u["cst"] = jnp.where(same, cst, 0.0)
        u["ar"] = bf(jnp.concatenate([u["At"], u["Rt"]], axis=0))

    states = [state_ref[p] for p in range(n_pair)]
    for c in range(n_chunk):
        rows = slice(c * C, (c + 1) * C)
        us = pre[c * n_pair:(c + 1) * n_pair]
        sbs = [bf(S) for S in states]
        zs = [_dot_nt(u["ar"], sb) for u, sb in zip(us, sbs)]
        new_states = [S * u["wc"] + _dot(sb, u["Tm"]) + u["cst"] for u, S, sb in zip(us, states, sbs)]
        ybds = [stack2(Z[C:2 * C]) + _dot(u["Q"], bf(stack2(Z[0:C]))) + u["Y0"] for u, Z in zip(us, zs)]
        ys = [y_bd[0:C] + y_bd[C:2 * C] for y_bd in ybds]
        states = new_states
        y = jnp.concatenate(ys, axis=1)
        mu = head_sum(y) * (1.0 / HEAD_DIM)
        yc = y - mu
        var = head_sum(yc * yc) * (1.0 / HEAD_DIM)
        yn = yc * lax.rsqrt(var + RWKV_GN_EPS) * ln_w + ln_b
        o_ref[rows, :] = ((yn + bonus[rows]) * g[rows]).astype(BF16)
    for p in range(n_pair):
        state_ref[p] = states[p]


def _rwkv(rw, vecs, w2, a2, g2, B, S, lb):
    nt = S // lb
    return pl.pallas_call(
        functools.partial(_rwkv_kernel, n_chunk=lb // RWKV_CHUNK),
        grid=(B, nt),
        in_specs=[pl.BlockSpec((lb, RWKV_PROJ), lambda b, j: (b * nt + j, 0)),
                  pl.BlockSpec((8, RWKV_WIDTH), lambda b, j: (0, 0)),
                  pl.BlockSpec((LANES, RWKV_WIDTH), lambda b, j: (0, 0)),
                  pl.BlockSpec((LANES, RWKV_WIDTH), lambda b, j: (0, 0)),
                  pl.BlockSpec((LANES, RWKV_WIDTH), lambda b, j: (0, 0))],
        out_specs=pl.BlockSpec((lb, RWKV_WIDTH), lambda b, j: (b * nt + j, 0)),
        out_shape=jax.ShapeDtypeStruct((B * S, RWKV_WIDTH), BF16),
        scratch_shapes=[pltpu.VMEM((RWKV_WIDTH // LANES, LANES, LANES), F32)],
        compiler_params=pltpu.CompilerParams(dimension_semantics=("arbitrary", "arbitrary"),
                                             vmem_limit_bytes=VMEM_LIMIT),
        name="rwkv",
    )(rw, vecs, w2, a2, g2)


def _mix_kernel(at_ref, rk_ref, x_ref, mod_ref, wo_a_ref, wo_r_ref, ln_ref, wr_ref, br_ref,
                x1_ref, h2_ref, ti_ref, gt_ref, rank_ref, cnt_ref, tb_ref, base_ref):
    first = (pl.program_id(0) == 0) & (pl.program_id(1) == 0)

    @pl.when(first)
    def _():
        base_ref[...] = jnp.zeros_like(base_ref)

    mod = mod_ref[0]
    y = _dot(at_ref[...], wo_a_ref[...]) + _dot(rk_ref[...], wo_r_ref[...])
    x = x_ref[...]
    tm = x.shape[0]
    x1 = _layer_norm(DEEPNORM_ALPHA * x + (1.0 + mod[2:3, :]) * y) * ln_ref[0:1, :] + ln_ref[1:2, :]
    h2 = _layer_norm(x1) * (1.0 + mod[4:5, :]) + mod[3:4, :]
    x1_ref[...] = x1
    h_hi = h2.astype(BF16)
    h2_ref[...] = h_hi

    h_lo = (h2 - h_hi.astype(F32)).astype(BF16)
    part = _dot(h_hi, wr_ref[...])
    logits = part[:, 0:LANES] + part[:, LANES:2 * LANES] + _dot(h_lo, wr_ref[:, 0:LANES]) + br_ref[...]
    lt = jnp.transpose(logits)[0:N_EXPERTS, :]
    erow = lax.broadcasted_iota(jnp.int32, (N_EXPERTS, tm), 0).astype(F32)
    cur = lt
    vals, idxs = [], []
    for _ in range(TOP_K):
        m = jnp.max(cur, axis=0, keepdims=True)
        idx = jnp.min(jnp.where(cur == m, erow, float(N_EXPERTS)), axis=0, keepdims=True)
        vals.append(m)
        idxs.append(idx)
        cur = jnp.where(erow == idx, -jnp.inf, cur)
    tv = jnp.concatenate(vals, axis=0)
    e = jnp.exp(tv - tv[0:1, :])
    gt_ref[...] = e / jnp.sum(e, axis=0, keepdims=True)
    ti_ref[...] = jnp.concatenate(idxs, axis=0).astype(jnp.int32)

    onehot = jnp.zeros((N_EXPERTS, tm), F32)
    for idx in idxs:
        onehot = onehot + (erow == idx).astype(F32)
    before = (lax.broadcasted_iota(jnp.int32, (tm, tm), 0)
              < lax.broadcasted_iota(jnp.int32, (tm, tm), 1)).astype(BF16)
    tot = base_ref[:, 0:1] + _dot(onehot.astype(BF16), before)
    ranks = [jnp.sum(jnp.where(erow == idx, tot, 0.0), axis=0, keepdims=True) for idx in idxs]
    rank_ref[...] = jnp.concatenate(ranks, axis=0).astype(jnp.int32)
    tb_ref[0] = base_ref[...].astype(jnp.int32)
    base_ref[...] = base_ref[...] + jnp.sum(onehot, axis=1, keepdims=True)
    cnt_ref[...] = base_ref[...].astype(jnp.int32)


def _mix(attn_out, rwkv_out, x2d, mod, wo_a, wo_r, ln1, w_router, b_router, B, S, tm):
    D = x2d.shape[1]
    nt = S // tm
    T = B * S
    tok = lambda b, j: (b * nt + j, 0)
    col = lambda b, j: (0, b * nt + j)
    fixed = lambda b, j: (0, 0)
    return pl.pallas_call(
        _mix_kernel,
        grid=(B, nt),
        in_specs=[pl.BlockSpec((tm, ATTN_WIDTH), tok),
                  pl.BlockSpec((tm, RWKV_WIDTH), tok),
                  pl.BlockSpec((tm, D), tok),
                  pl.BlockSpec((1, 6, D), lambda b, j: (b, 0, 0)),
                  pl.BlockSpec((ATTN_WIDTH, D), fixed),
                  pl.BlockSpec((RWKV_WIDTH, D), fixed),
                  pl.BlockSpec((2, D), fixed),
                  pl.BlockSpec((D, 2 * LANES), fixed),
                  pl.BlockSpec((1, LANES), fixed)],
        out_specs=[pl.BlockSpec((tm, D), tok),
                   pl.BlockSpec((tm, D), tok),
                   pl.BlockSpec((TOP_K, tm), col),
                   pl.BlockSpec((TOP_K, tm), col),
                   pl.BlockSpec((TOP_K, tm), col),
                   pl.BlockSpec((N_EXPERTS, LANES), fixed),
                   pl.BlockSpec((1, N_EXPERTS, LANES), lambda b, j: (b * nt + j, 0, 0))],
        out_shape=[jax.ShapeDtypeStruct((T, D), F32),
                   jax.ShapeDtypeStruct((T, D), BF16),
                   jax.ShapeDtypeStruct((TOP_K, T), jnp.int32),
                   jax.ShapeDtypeStruct((TOP_K, T), F32),
                   jax.ShapeDtypeStruct((TOP_K, T), jnp.int32),
                   jax.ShapeDtypeStruct((N_EXPERTS, LANES), jnp.int32),
                   jax.ShapeDtypeStruct((T // tm, N_EXPERTS, LANES), jnp.int32)],
        scratch_shapes=[pltpu.VMEM((N_EXPERTS, LANES), F32)],
        compiler_params=pltpu.CompilerParams(dimension_semantics=("arbitrary", "arbitrary"),
                                             vmem_limit_bytes=VMEM_LIMIT),
        name="mix",
    )(attn_out, rwkv_out, x2d, mod, wo_a, wo_r, ln1, w_router, b_router)


RUN_PIECES = tuple(2 ** b for b in range(int(math.log2(MOE_TILE)), -1, -1))
SUBLANES = 8


def _to_tiles(ref, x):
    n = x.shape[0]
    for c in range(SUBLANES):
        ref[pl.ds(c, n, stride=SUBLANES), :] = x[:, c * LANES:(c + 1) * LANES]


def _from_tiles(ref):
    n = ref.shape[0] // SUBLANES
    return jnp.concatenate([ref[pl.ds(c, n, stride=SUBLANES), :] for c in range(SUBLANES)], axis=1)


def _run_copies(n, local, local_start, remote, remote_start, sem, to_remote):
    off = 0
    for piece in RUN_PIECES:
        take = (n & piece) != 0

        @pl.when(take)
        def _(off=off, piece=piece):
            lo = pl.multiple_of((local_start + off) * SUBLANES, SUBLANES)
            ro = pl.multiple_of((remote_start + off) * SUBLANES, SUBLANES)
            loc = local.at[pl.ds(lo, piece * SUBLANES)]
            rem = remote.at[pl.ds(ro, piece * SUBLANES)]
            src, dst = (loc, rem) if to_remote else (rem, loc)
            pltpu.make_async_copy(src, dst, sem).start()

        off = off + (n & piece)


def _dispatch_kernel(tcnt_ref, lstart_ref, gstart_ref, pad_ref, pad_start_ref, n_used_ref, lpos_ref, h2_ref,
                     xs_ref, xbuf, zbuf, sems):
    i = pl.program_id(0)
    tm = h2_ref.shape[0]
    n_loc = TOP_K * tm
    n_blocks = xs_ref.shape[0] // (MOE_BLOCK * SUBLANES)
    sem = sems.at[0]

    @pl.when(i == 0)
    def _():
        zbuf[...] = jnp.zeros_like(zbuf)

        def zero_pad(e, carry):
            _run_copies(pad_ref[e], zbuf, 0, xs_ref, pad_start_ref[e], sems.at[1], True)
            return carry

        def zero_tail(b, carry):
            @pl.when(b >= n_used_ref[0])
            def _():
                start = pl.multiple_of(b * (MOE_BLOCK * SUBLANES), MOE_BLOCK * SUBLANES)
                pltpu.make_async_copy(zbuf, xs_ref.at[pl.ds(start, MOE_BLOCK * SUBLANES)], sems.at[1]).start()
            return carry

        lax.fori_loop(0, N_EXPERTS, zero_pad, 0)
        lax.fori_loop(n_blocks - N_EXPERTS, n_blocks, zero_tail, 0)
        n_zero = N_EXPERTS * MOE_BLOCK * SUBLANES
        pltpu.make_async_copy(xs_ref.at[pl.ds(0, n_zero)], xs_ref.at[pl.ds(0, n_zero)], sems.at[1]).wait()

    slot = lax.broadcasted_iota(jnp.int32, (n_loc, tm), 0)
    lpos = lpos_ref[...]
    perm = jnp.zeros((n_loc, tm), F32)
    for k in range(TOP_K):
        perm = perm + (slot == lpos[k:k + 1, :]).astype(F32)
    _to_tiles(xbuf, _dot(perm.astype(BF16), h2_ref[...].astype(BF16)))

    def issue(e, carry):
        idx = i * N_EXPERTS + e
        _run_copies(tcnt_ref[idx], xbuf, lstart_ref[idx], xs_ref, gstart_ref[idx], sem, True)
        return carry

    lax.fori_loop(0, N_EXPERTS, issue, 0)
    pltpu.make_async_copy(xbuf, xs_ref.at[pl.ds(0, n_loc * SUBLANES)], sem).wait()


def _dispatch(tcnt, lstart, gstart, pad, pad_start, n_used, lpos, h2, n_rows, tm):
    T, D = h2.shape
    grid_spec = pltpu.PrefetchScalarGridSpec(
        num_scalar_prefetch=6,
        grid=(T // tm,),
        in_specs=[pl.BlockSpec((TOP_K, tm), lambda i, *_: (0, i)),
                  pl.BlockSpec((tm, D), lambda i, *_: (i, 0))],
        out_specs=pl.BlockSpec(memory_space=pl.ANY),
        scratch_shapes=[pltpu.VMEM((TOP_K * tm * SUBLANES, LANES), F32),
                        pltpu.VMEM((MOE_BLOCK * SUBLANES, LANES), F32),
                        pltpu.SemaphoreType.DMA((2,))],
    )
    return pl.pallas_call(
        _dispatch_kernel,
        grid_spec=grid_spec,
        out_shape=jax.ShapeDtypeStruct((n_rows * SUBLANES, LANES), F32),
        compiler_params=pltpu.CompilerParams(dimension_semantics=("arbitrary",),
                                             vmem_limit_bytes=VMEM_LIMIT),
        name="dispatch",
    )(tcnt, lstart, gstart, pad, pad_start, n_used, lpos, h2)


def _experts_kernel(blk_e_ref, n_used_ref, xs_ref, wgu_ref, bgu_ref, wd_ref, bd_ref, ys_ref, wgu_bf, wd_bf):
    i = pl.program_id(0)
    d_ff = wd_ref.shape[1]
    used = i < n_used_ref[0]
    new_expert = (i == 0) | (blk_e_ref[i] != blk_e_ref[jnp.maximum(i - 1, 0)])

    @pl.when(used & new_expert)
    def _():
        wgu_bf[...] = wgu_ref[0].astype(BF16)
        wd_bf[...] = wd_ref[0].astype(BF16)

    @pl.when(used)
    def _():
        xb = _from_tiles(xs_ref).astype(BF16)
        gu = _dot(xb, wgu_bf[...]) + bgu_ref[0]
        gate = jnp.minimum(gu[:, :d_ff], SWIGLU_LIMIT)
        up = jnp.clip(gu[:, d_ff:], -SWIGLU_LIMIT, SWIGLU_LIMIT)
        act = (up + 1.0) * (gate * _sigmoid(SWIGLU_ALPHA * gate))
        _to_tiles(ys_ref, _dot(act.astype(BF16), wd_bf[...]) + bd_ref[0])

    @pl.when(i >= n_used_ref[0])
    def _():
        ys_ref[...] = jnp.zeros_like(ys_ref)


def _experts(blk_e, n_used, xs, wgu, bgu, wd, bd):
    d_ff, D = wd.shape[1], wd.shape[2]
    n_blocks = xs.shape[0] // (MOE_BLOCK * SUBLANES)
    blk = (MOE_BLOCK * SUBLANES, LANES)

    def last_used(i, n_used_ref):
        return jnp.minimum(i, jnp.maximum(n_used_ref[0] - 1, 0))

    def row_map(i, blk_e_ref, n_used_ref):
        return (last_used(i, n_used_ref), 0)

    def exp_map(i, blk_e_ref, n_used_ref):
        return (blk_e_ref[last_used(i, n_used_ref)], 0, 0)

    grid_spec = pltpu.PrefetchScalarGridSpec(
        num_scalar_prefetch=2,
        grid=(n_blocks,),
        in_specs=[pl.BlockSpec(blk, row_map),
                  pl.BlockSpec((1, D, 2 * d_ff), exp_map),
                  pl.BlockSpec((1, 1, 2 * d_ff), exp_map),
                  pl.BlockSpec((1, d_ff, D), exp_map),
                  pl.BlockSpec((1, 1, D), exp_map)],
        out_specs=pl.BlockSpec(blk, lambda i, blk_e_ref, n_used_ref: (i, 0)),
        scratch_shapes=[pltpu.VMEM((D, 2 * d_ff), BF16), pltpu.VMEM((d_ff, D), BF16)],
    )
    return pl.pallas_call(
        _experts_kernel,
        grid_spec=grid_spec,
        out_shape=jax.ShapeDtypeStruct(xs.shape, F32),
        compiler_params=pltpu.CompilerParams(dimension_semantics=("arbitrary",),
                                             vmem_limit_bytes=EXPERTS_VMEM_LIMIT),
        name="experts",
    )(blk_e, n_used, xs, wgu, bgu, wd, bd)


def _combine_kernel(tcnt_ref, lstart_ref, gstart_ref, ys_ref, lpos_ref, gt_ref, x1_ref, mod_ref, ln_ref,
                    o_ref, buf, sem):
    nt = pl.num_programs(1)
    i = pl.program_id(0) * nt + pl.program_id(1)
    tm = x1_ref.shape[0]
    n_loc = TOP_K * tm

    def issue(e, carry):
        idx = i * N_EXPERTS + e
        _run_copies(tcnt_ref[idx], buf, lstart_ref[idx], ys_ref, gstart_ref[idx], sem, False)
        return carry

    lax.fori_loop(0, N_EXPERTS, issue, 0)
    slot = lax.broadcasted_iota(jnp.int32, (tm, n_loc), 1)
    lpos = lpos_ref[...]
    gt = gt_ref[...]
    pick = jnp.zeros((tm, n_loc), F32)
    for k in range(TOP_K):
        pick = pick + jnp.where(slot == lpos[:, k:k + 1], gt[:, k:k + 1], 0.0)
    pltpu.make_async_copy(ys_ref.at[pl.ds(0, n_loc * SUBLANES)], buf, sem).wait()
    y = _dot(pick.astype(BF16), _from_tiles(buf).astype(BF16))
    mod = mod_ref[0]
    z = DEEPNORM_ALPHA * x1_ref[...] + (1.0 + mod[5:6, :]) * y
    o_ref[...] = _layer_norm(z) * ln_ref[0:1, :] + ln_ref[1:2, :]


def _combine(tcnt, lstart, gstart, ys, lpos_t, gates_t, x1, mod, ln2, B, S, tm):
    T, D = x1.shape
    nt = S // tm
    tok = lambda b, j, *_: (b * nt + j, 0)
    grid_spec = pltpu.PrefetchScalarGridSpec(
        num_scalar_prefetch=3,
        grid=(B, nt),
        in_specs=[pl.BlockSpec(memory_space=pl.ANY),
                  pl.BlockSpec((tm, TOP_K), tok),
                  pl.BlockSpec((tm, TOP_K), tok),
                  pl.BlockSpec((tm, D), tok),
                  pl.BlockSpec((1, 6, D), lambda b, j, *_: (b, 0, 0)),
                  pl.BlockSpec((2, D), lambda b, j, *_: (0, 0))],
        out_specs=pl.BlockSpec((tm, D), tok),
        scratch_shapes=[pltpu.VMEM((TOP_K * tm * SUBLANES, LANES), F32), pltpu.SemaphoreType.DMA(())],
    )
    return pl.pallas_call(
        _combine_kernel,
        grid_spec=grid_spec,
        out_shape=jax.ShapeDtypeStruct((T, D), F32),
        compiler_params=pltpu.CompilerParams(dimension_semantics=("arbitrary", "arbitrary"),
                                             vmem_limit_bytes=VMEM_LIMIT),
        name="combine",
    )(tcnt, lstart, gstart, ys, lpos_t, gates_t, x1, mod, ln2)


def _pad_rows(w, rows):
    return jnp.pad(w, ((0, rows - w.shape[0]), (0, 0)))


def _pad_cols(w, cols):
    return jnp.pad(w, ((0, 0), (0, cols - w.shape[1])))


def _layer(x, c, positions, w_ada, b_ada, w_in, shift_mu, rwkv_w0, rwkv_w2, rwkv_a0, rwkv_a2, rwkv_g2,
           rwkv_k_k, rwkv_k_a, rwkv_r_k, rwkv_ln_w, rwkv_ln_b, attn_sinks, w_out, ln1_g, ln1_b,
           w_router, b_router, w_gate_up, b_gate_up, w_down, b_down, ln2_g, ln2_b):
    B, S, D = x.shape
    T = B * S
    tile = min(256, S)

    q0, k0, v0 = 0, ATTN_WIDTH, ATTN_WIDTH + KV_WIDTH
    r0 = ATTN_WIDTH + 2 * KV_WIDTH
    heads = lambda base: [w_in[:, base + h * HEAD_DIM: base + (h + 1) * HEAD_DIM] for h in range(N_KV_HEADS)]
    dup = lambda hs: [w for w in hs for _ in range(2)]
    w_attn = jnp.concatenate([w_in[:, q0:q0 + ATTN_WIDTH]] + dup(heads(k0)) + dup(heads(v0)), axis=1).astype(BF16)
    lora0 = r0 + 3 * RWKV_WIDTH
    lora = (DECAY_LORA, AAA_LORA, GATE_LORA)
    pieces_w = [w_in[:, r0:lora0]]
    pieces_mu = [shift_mu[None, 0:3 * RWKV_WIDTH]]
    off = lora0
    for n in lora:
        pieces_w.append(_pad_cols(w_in[:, off:off + n], LANES))
        pieces_mu.append(_pad_cols(shift_mu[None, off - r0:off - r0 + n], LANES))
        off += n
    w_rwkv = jnp.concatenate(pieces_w, axis=1).astype(BF16)
    mu = jnp.concatenate(pieces_mu, axis=1)
    inv_freq = ROPE_THETA ** (-jnp.arange(0, ROT_DIM, 2, dtype=F32) / ROT_DIM)
    lane_p = jnp.arange(LANES) % HEAD_DIM
    rot_tab = jnp.zeros((8, LANES), F32)
    rot_tab = rot_tab.at[0].set(jnp.where(lane_p < ROT_DIM, inv_freq[lane_p % (ROT_DIM // 2)], 0.0))
    rot_tab = rot_tab.at[1].set(jnp.where(lane_p < ROT_DIM // 2, -1.0, 0.0))
    rot_tab = rot_tab.at[2].set(jnp.where((lane_p >= ROT_DIM // 2) & (lane_p < ROT_DIM), 1.0, 0.0))
    vecs = jnp.stack([rwkv_w0, rwkv_a0, rwkv_k_k, rwkv_k_a, rwkv_r_k.reshape(-1), rwkv_ln_w, rwkv_ln_b,
                      jnp.zeros_like(rwkv_w0)])
    w2 = _pad_rows(rwkv_w2, LANES).astype(BF16)
    a2 = _pad_rows(rwkv_a2, LANES).astype(BF16)
    g2 = _pad_rows(rwkv_g2, LANES).astype(BF16)
    wo_a = w_out[:ATTN_WIDTH].astype(BF16)
    wo_r = w_out[ATTN_WIDTH:].astype(BF16)
    w_r_hi = w_router.astype(BF16)
    w_r_lo = (w_router - w_r_hi.astype(F32)).astype(BF16)
    w_r = jnp.concatenate([_pad_cols(w_r_hi, LANES), _pad_cols(w_r_lo, LANES)], axis=1)
    b_r = jnp.concatenate([b_router, jnp.full((LANES - N_EXPERTS,), NEG_INF, F32)])[None, :]

    mod = _mod(c, w_ada, b_ada).reshape(B, 6, D)
    qkv, rw = _inproj(x, positions, mod, w_attn, w_rwkv, mu, rot_tab, tile)
    attn_out = _attention(qkv, attn_sinks, B, S)
    rwkv_out = _rwkv(rw, vecs, w2, a2, g2, B, S, min(128, S))

    mtile = min(MOE_TILE, S)
    x1, h2, top_i, gates, rank, cnt, tbase = _mix(attn_out, rwkv_out, x.reshape(T, D), mod, wo_a, wo_r,
                                                   jnp.stack([ln1_g, ln1_b]), w_r, b_r, B, S, mtile)

    counts = cnt[:, 0]
    padded = (counts + MOE_BLOCK - 1) // MOE_BLOCK * MOE_BLOCK
    pend = jnp.cumsum(padded)
    pstart = pend - padded
    n_blocks = T * TOP_K // MOE_BLOCK + N_EXPERTS
    blk_row = jnp.arange(n_blocks, dtype=jnp.int32) * MOE_BLOCK
    blk_e = jnp.minimum(jnp.sum((pend[None, :] <= blk_row[:, None]).astype(jnp.int32), axis=1), N_EXPERTS - 1)
    n_used = (pend[-1:] // MOE_BLOCK).astype(jnp.int32)
    tb = tbase[:, :, 0]
    tcnt = jnp.concatenate([tb[1:], counts[None]], axis=0) - tb
    lstart = jnp.cumsum(tcnt, axis=1) - tcnt
    gstart = pstart[None, :] + tb
    shift = jnp.repeat(jnp.transpose(lstart - tb), mtile, axis=1)
    experts = jnp.arange(N_EXPERTS, dtype=jnp.int32)
    lpos = rank + jnp.sum(jnp.where(top_i[None] == experts[:, None, None], shift[:, None, :], 0), axis=0)
    flat = lambda a: a.reshape(-1).astype(jnp.int32)

    xs = _dispatch(flat(tcnt), flat(lstart), flat(gstart), flat(padded - counts), flat(pstart + counts), n_used,
                   lpos, h2, n_blocks * MOE_BLOCK, mtile)
    ys = _experts(blk_e, n_used, xs, w_gate_up, b_gate_up[:, None, :], w_down, b_down[:, None, :])
    out = _combine(flat(tcnt), flat(lstart), flat(gstart), ys, jnp.transpose(lpos), jnp.transpose(gates), x1, mod,
                   jnp.stack([ln2_g, ln2_b]), B, S, mtile)
    return out.reshape(B, S, D)


def kernel(x, c, positions, w_ada, b_ada, w_in, shift_mu, rwkv_w0, rwkv_w2, rwkv_a0, rwkv_a2, rwkv_g2,
           rwkv_k_k, rwkv_k_a, rwkv_r_k, rwkv_ln_w, rwkv_ln_b, attn_sinks, w_out, ln1_g, ln1_b,
           w_router, b_router, w_gate_up, b_gate_up, w_down, b_down, ln2_g, ln2_b):
    for l in range(DEPTH):
        x = _layer(x, c, positions, w_ada[l], b_ada[l], w_in[l], shift_mu[l], rwkv_w0[l], rwkv_w2[l],
                   rwkv_a0[l], rwkv_a2[l], rwkv_g2[l], rwkv_k_k[l], rwkv_k_a[l], rwkv_r_k[l], rwkv_ln_w[l],
                   rwkv_ln_b[l], attn_sinks[l], w_out[l], ln1_g[l], ln1_b[l], w_router[l], b_router[l],
                   w_gate_up[l], b_gate_up[l], w_down[l], b_down[l], ln2_g[l], ln2_b[l])
    return x
```

```python
import functools
import math

import jax
import jax.numpy as jnp
from jax import lax
from jax.experimental import pallas as pl
from jax.experimental.pallas import tpu as pltpu

F32 = jnp.float32
BF16 = jnp.bfloat16

HEAD_DIM = 64
N_ATTN_HEADS = 8
N_KV_HEADS = 2
N_RWKV_HEADS = 8
ATTN_WIDTH = N_ATTN_HEADS * HEAD_DIM
KV_WIDTH = N_KV_HEADS * HEAD_DIM
RWKV_WIDTH = N_RWKV_HEADS * HEAD_DIM
ATTN_BLOCK = 128
ROT_DIM = HEAD_DIM // 4
ROPE_THETA = 500000.0
DECAY_LORA = 32
AAA_LORA = 32
GATE_LORA = 96
N_EXPERTS = 32
TOP_K = 4
SWIGLU_LIMIT = 7.0
SWIGLU_ALPHA = 1.702
LN_EPS = 1e-5
RWKV_GN_EPS = 64e-5
NEG_INF = -1e30
DEPTH = 1
DEEPNORM_ALPHA = (2 * DEPTH) ** 0.25

LANES = 128
RWKV_CHUNK = 64
MOE_BLOCK = 256
MOE_TILE = 256
ATTN_PROJ = ATTN_WIDTH + 4 * KV_WIDTH
RWKV_PROJ = 3 * RWKV_WIDTH + 3 * LANES
VMEM_LIMIT = 48 * 1024 * 1024


def _dot(a, b):
    return jnp.dot(a, b, preferred_element_type=F32)


def _dot_nt(a, b):
    return lax.dot_general(a, b, (((1,), (1,)), ((), ())), preferred_element_type=F32)


def _dot_tn(a, b):
    return lax.dot_general(a, b, (((0,), (0,)), ((), ())), preferred_element_type=F32)


def _split3(x):
    h = x.astype(BF16)
    r1 = x - h.astype(F32)
    m = r1.astype(BF16)
    lo = (r1 - m.astype(F32)).astype(BF16)
    return h, m, lo


def _dot_hilo_rhs(x, m_bf16):
    h = x.astype(BF16)
    lo = (x - h.astype(F32)).astype(BF16)
    return _dot(h, m_bf16) + _dot(lo, m_bf16)


def _dot_exact_lhs(m_bf16, x):
    h, m, lo = _split3(x)
    return _dot(m_bf16, h) + _dot(m_bf16, m) + _dot(m_bf16, lo)


def _layer_norm(x):
    mu = jnp.mean(x, axis=-1, keepdims=True)
    xc = x - mu
    var = jnp.mean(xc * xc, axis=-1, keepdims=True)
    return xc * lax.rsqrt(var + LN_EPS)


def _sigmoid(x):
    return 1.0 / (1.0 + jnp.exp(-x))


def _mod_kernel(c_ref, w_ref, b_ref, o_ref):
    c = c_ref[...]
    s = c * _sigmoid(c)
    o_ref[...] = jnp.dot(s, w_ref[...], preferred_element_type=F32,
                         precision=lax.Precision.HIGHEST) + b_ref[...]


def _mod(c, w_ada, b_ada):
    B, D = c.shape
    n = w_ada.shape[1] // D
    return pl.pallas_call(
        _mod_kernel,
        grid=(n,),
        in_specs=[pl.BlockSpec((B, D), lambda i: (0, 0)),
                  pl.BlockSpec((D, D), lambda i: (0, i)),
                  pl.BlockSpec((1, D), lambda i: (0, i))],
        out_specs=pl.BlockSpec((B, D), lambda i: (0, i)),
        out_shape=jax.ShapeDtypeStruct((B, n * D), F32),
        compiler_params=pltpu.CompilerParams(dimension_semantics=("arbitrary",),
                                             vmem_limit_bytes=VMEM_LIMIT),
        name="mod",
    )(c, w_ada, b_ada.reshape(1, -1))


def _inproj_kernel(x_ref, pos_ref, mod_ref, wa_ref, wr_ref, mu_ref, rt_ref,
                   qkv_ref, rw_ref, carry_ref):
    j = pl.program_id(1)
    x = x_ref[0]
    tm = x.shape[0]
    mod = mod_ref[0]
    h = _layer_norm(x) * (1.0 + mod[1:2, :]) + mod[0:1, :]
    hb = h.astype(BF16)

    pa = _dot(hb, wa_ref[...])
    ang = pos_ref[0].astype(F32) * rt_ref[0:1, :]
    cs = jnp.cos(ang)
    sn = jnp.sin(ang)
    m_lo = rt_ref[1:2, :]
    m_hi = rt_ref[2:3, :]
    n_q = ATTN_WIDTH // LANES
    n_rot = (ATTN_WIDTH + 2 * KV_WIDTH) // LANES
    for ch in range(n_rot):
        t = pa[:, ch * LANES:(ch + 1) * LANES]
        if ch < n_q:
            t = t * (1.0 / math.sqrt(HEAD_DIM))
        up = pltpu.roll(t, LANES - ROT_DIM // 2, 1)
        dn = pltpu.roll(t, ROT_DIM // 2, 1)
        o = t * cs + sn * (m_lo * up + m_hi * dn)
        qkv_ref[:, ch * LANES:(ch + 1) * LANES] = o.astype(BF16)
    qkv_ref[:, n_rot * LANES:] = pa[:, n_rot * LANES:].astype(BF16)

    pr = _dot(hb, wr_ref[...])
    prev = pltpu.roll(pr, 1, 0)
    row = lax.broadcasted_iota(jnp.int32, (tm, 1), 0)
    carry = jnp.where(j == 0, 0.0, carry_ref[...])
    prev = jnp.where(row == 0, carry, prev)
    carry_ref[...] = pr[tm - 1:tm, :]
    rw_ref[...] = pr + (prev - pr) * mu_ref[...]


def _inproj(x, positions, mod, w_attn, w_rwkv, mu, rot_tab, tm):
    B, S, D = x.shape
    nt = S // tm
    return pl.pallas_call(
        _inproj_kernel,
        grid=(B, nt),
        in_specs=[pl.BlockSpec((1, tm, D), lambda b, j: (b, j, 0)),
                  pl.BlockSpec((1, tm, 1), lambda b, j: (b, j, 0)),
                  pl.BlockSpec((1, 6, D), lambda b, j: (b, 0, 0)),
                  pl.BlockSpec((D, ATTN_PROJ), lambda b, j: (0, 0)),
                  pl.BlockSpec((D, RWKV_PROJ), lambda b, j: (0, 0)),
                  pl.BlockSpec((1, RWKV_PROJ), lambda b, j: (0, 0)),
                  pl.BlockSpec((8, LANES), lambda b, j: (0, 0))],
        out_specs=[pl.BlockSpec((tm, ATTN_PROJ), lambda b, j: (b * nt + j, 0)),
                   pl.BlockSpec((tm, RWKV_PROJ), lambda b, j: (b * nt + j, 0))],
        out_shape=[jax.ShapeDtypeStruct((B * S, ATTN_PROJ), BF16),
                   jax.ShapeDtypeStruct((B * S, RWKV_PROJ), F32)],
        scratch_shapes=[pltpu.VMEM((1, RWKV_PROJ), F32)],
        compiler_params=pltpu.CompilerParams(dimension_semantics=("arbitrary", "arbitrary"),
                                             vmem_limit_bytes=VMEM_LIMIT),
        name="inproj",
    )(x, positions.reshape(B, S, 1), mod, w_attn, w_rwkv, mu, rot_tab)


def _attn_kernel(q_ref, kc_ref, kp_ref, vc_ref, vp_ref, sink_ref, o_ref):
    i = pl.program_id(1)
    blk = ATTN_BLOCK
    qi = lax.broadcasted_iota(jnp.int32, (blk, 2 * blk), 0)
    kj = lax.broadcasted_iota(jnp.int32, (blk, 2 * blk), 1)
    allowed = (kj > qi) & (kj <= qi + blk) & ((kj >= blk) | (i > 0))
    lane = lax.broadcasted_iota(jnp.int32, (1, LANES), 1)
    lo = (lane < HEAD_DIM).astype(BF16)
    hi = (lane >= HEAD_DIM).astype(BF16)
    halves = []
    for g in range(N_KV_HEADS):
        sl = slice(g * LANES, (g + 1) * LANES)
        kcat = jnp.concatenate([kp_ref[:, sl], kc_ref[:, sl]], axis=0)
        vcat = jnp.concatenate([vp_ref[:, sl], vc_ref[:, sl]], axis=0)
        halves.append(((kcat * lo, vcat * lo), (kcat * hi, vcat * hi)))
    heads = [(c, half) for c in range(ATTN_WIDTH // LANES) for half in range(2)]
    scores = [_dot_nt(q_ref[:, c * LANES:(c + 1) * LANES], halves[c // 2][half][0]) for c, half in heads]
    probs, denoms = [], []
    for (c, half), s in zip(heads, scores):
        sink = sink_ref[2 * c + half]
        s = jnp.where(allowed, s, NEG_INF)
        m = jnp.maximum(jnp.max(s, axis=-1, keepdims=True), sink)
        p = jnp.exp(s - m)
        denoms.append(jnp.sum(p, axis=-1, keepdims=True) + jnp.exp(sink - m))
        probs.append(p.astype(BF16))
    outs = [_dot(p, halves[c // 2][half][1]) / d for (c, half), p, d in zip(heads, probs, denoms)]
    for c in range(ATTN_WIDTH // LANES):
        o_ref[:, c * LANES:(c + 1) * LANES] = (outs[2 * c] + outs[2 * c + 1]).astype(BF16)


def _attention(qkv, sinks, B, S):
    nb = S // ATTN_BLOCK
    blk = ATTN_BLOCK
    kcol = ATTN_WIDTH // (2 * KV_WIDTH)
    vcol = kcol + 1
    cur = lambda col: (lambda b, i: (b * nb + i, col))
    prv = lambda col: (lambda b, i: (jnp.maximum(b * nb + i - 1, 0), col))
    return pl.pallas_call(
        _attn_kernel,
        grid=(B, nb),
        in_specs=[pl.BlockSpec((blk, ATTN_WIDTH), lambda b, i: (b * nb + i, 0)),
                  pl.BlockSpec((blk, 2 * KV_WIDTH), cur(kcol)),
                  pl.BlockSpec((blk, 2 * KV_WIDTH), prv(kcol)),
                  pl.BlockSpec((blk, 2 * KV_WIDTH), cur(vcol)),
                  pl.BlockSpec((blk, 2 * KV_WIDTH), prv(vcol)),
                  pl.BlockSpec(memory_space=pltpu.SMEM)],
        out_specs=pl.BlockSpec((blk, ATTN_WIDTH), lambda b, i: (b * nb + i, 0)),
        out_shape=jax.ShapeDtypeStruct((B * S, ATTN_WIDTH), BF16),
        compiler_params=pltpu.CompilerParams(dimension_semantics=("arbitrary", "arbitrary"),
                                             vmem_limit_bytes=VMEM_LIMIT),
        name="attn",
    )(qkv, qkv, qkv, qkv, qkv, sinks)


def _rwkv_kernel(rw_ref, vec_ref, w2_ref, a2_ref, g2_ref, o_ref, state_ref, *, n_chunk):
    j = pl.program_id(1)
    C = RWKV_CHUNK
    W = RWKV_WIDTH
    n_pair = W // LANES

    @pl.when(j == 0)
    def _():
        state_ref[...] = jnp.zeros_like(state_ref)

    w0 = vec_ref[0:1, :]
    a0 = vec_ref[1:2, :]
    k_k = vec_ref[2:3, :]
    k_a = vec_ref[3:4, :]
    r_k = vec_ref[4:5, :]
    ln_w = vec_ref[5:6, :]
    ln_b = vec_ref[6:7, :]

    r = rw_ref[:, 0:W]
    k = rw_ref[:, W:2 * W]
    v = rw_ref[:, 2 * W:3 * W]
    wl = rw_ref[:, 3 * W:3 * W + LANES]
    al = rw_ref[:, 3 * W + LANES:3 * W + 2 * LANES]
    gl = rw_ref[:, 3 * W + 2 * LANES:3 * W + 3 * LANES]

    ri = lax.broadcasted_iota(jnp.int32, (LANES, LANES), 0)
    ci = lax.broadcasted_iota(jnp.int32, (LANES, LANES), 1)
    same = (ri // HEAD_DIM) == (ci // HEAD_DIM)
    strict = same & ((ri % HEAD_DIM) > (ci % HEAD_DIM))
    incl = same & ((ri % HEAD_DIM) >= (ci % HEAD_DIM))
    ones_bd = same.astype(BF16)
    lane = lax.broadcasted_iota(jnp.int32, (1, LANES), 1)
    m0 = (lane < HEAD_DIM).astype(F32)
    m1 = 1.0 - m0
    tri = (lax.broadcasted_iota(jnp.int32, (C, C), 0) >= lax.broadcasted_iota(jnp.int32, (C, C), 1)).astype(BF16)

    def head_sum(xv):
        return jnp.concatenate(
            [_dot_hilo_rhs(xv[:, p * LANES:(p + 1) * LANES], ones_bd) for p in range(n_pair)], axis=1)

    def stack2(xp):
        return jnp.concatenate([xp * m0, xp * m1], axis=0)

    z = w0 + _dot(jnp.tanh(wl).astype(BF16), w2_ref[...])
    lw = -math.exp(-0.5) * _sigmoid(z)
    a = _sigmoid(a0 + _dot(al.astype(BF16), a2_ref[...]))
    g = _dot(_sigmoid(gl).astype(BF16), g2_ref[...])
    kk = k * k_k
    kkn = kk / jnp.maximum(jnp.sqrt(head_sum(kk * kk)), 1e-12)
    k2 = k * (1.0 + (a - 1.0) * k_a)
    av = -kkn
    bv = kkn * a
    bonus = head_sum(r * k2 * r_k) * v

    eye = (ri == ci).astype(F32)
    bf = lambda t: t.astype(BF16)

    pre = []
    for c in range(n_chunk):
        rows = slice(c * C, (c + 1) * C)
        lwc = lw[rows]
        cw = _dot_exact_lhs(tri, lwc)
        cwl = cw[C - 1:C, :]
        e_in = jnp.exp(cw)
        e_neg = jnp.exp(-cw)
        e_rem = jnp.exp(cwl - cw)
        wc = jnp.exp(cwl)
        Rt = r[rows] * e_in
        At = av[rows] * jnp.exp(cw - lwc)
        Bb = bv[rows] * e_neg
        Kb = k2[rows] * e_neg
        Bh = bv[rows] * e_rem
        Kh = k2[rows] * e_rem
        vc = v[rows]
        for p in range(n_pair):
            sl = slice(p * LANES, (p + 1) * LANES)
            pre.append(dict(At=At[:, sl], Rt=Rt[:, sl], Bb=Bb[:, sl], Kb=Kb[:, sl], Bh=Bh[:, sl], Kh=Kh[:, sl],
                            v=vc[:, sl], wc=wc[:, sl]))

    for u in pre:
        u["at_bd"] = stack2(u["At"])
        lhs = bf(jnp.concatenate([u["at_bd"], stack2(u["Rt"])], axis=0))
        rhs = bf(jnp.concatenate([stack2(u["Bb"]), stack2(u["Kb"])], axis=0))
        u["G"] = _dot_nt(lhs, rhs)
    for u in pre:
        G = u.pop("G")
        u["a_ab"] = jnp.where(strict, G[0:2 * C, 0:2 * C], 0.0)
        u["a_ak"] = bf(jnp.where(strict, G[0:2 * C, 2 * C:4 * C], 0.0))
        u["a_rb"] = bf(jnp.where(incl, G[2 * C:4 * C, 0:2 * C], 0.0))
        u["a_rk"] = bf(jnp.where(incl, G[2 * C:4 * C, 2 * C:4 * C], 0.0))
        u["v_bd"] = bf(stack2(u["v"]))
    for u in pre:
        xb = bf(u["a_ab"])
        u["P"] = eye + u.pop("a_ab")
        u["X"] = _dot(xb, xb)
        u["M0"] = _dot(u["a_ak"], u["v_bd"])
    for _ in range(int(math.log2(C)) - 2):
        for u in pre:
            Wm = _dot(bf(u["X"]), bf(jnp.concatenate([u["P"], u["X"]], axis=1)))
            u["P"] = u["P"] + Wm[:, 0:LANES]
            u["X"] = Wm[:, LANES:2 * LANES]
    for u in pre:
        u["P"] = bf(u["P"] + _dot(bf(u.pop("X")), bf(u["P"])))
    for u in pre:
        u["M1"] = _dot(u["P"], bf(u.pop("M0")))
        u["Q"] = bf(_dot(u["a_rb"], u["P"]))
        u["PtB"] = _dot_tn(u["P"], bf(stack2(u["Bh"])))
    for u in pre:
        M1 = u.pop("M1")
        u["Y0"] = _dot(jnp.concatenate([u["a_rb"], u["a_rk"]], axis=1),
                       jnp.concatenate([bf(M1), u["v_bd"]], axis=0))
        u["Tm"] = bf(_dot_tn(bf(u["at_bd"]), bf(u.pop("PtB"))))
        m1_pair = M1[0:C] + M1[C:2 * C]
        cst = _dot_tn(bf(jnp.concatenate([m1_pair, u["v"]], axis=0)),
                      bf(jnp.concatenate([u["Bh"], u["Kh"]], axis=0)))
        u["cst"] = jnp.where(same, cst, 0.0)
        u["ar"] = bf(jnp.concatenate([u["At"], u["Rt"]], axis=0))

    states = [state_ref[p] for p in range(n_pair)]
    for c in range(n_chunk):
        rows = slice(c * C, (c + 1) * C)
        us = pre[c * n_pair:(c + 1) * n_pair]
        sbs = [bf(S) for S in states]
        zs = [_dot_nt(u["ar"], sb) for u, sb in zip(us, sbs)]
        new_states = [S * u["wc"] + _dot(sb, u["Tm"]) + u["cst"] for u, S, sb in zip(us, states, sbs)]
        ybds = [stack2(Z[C:2 * C]) + _dot(u["Q"], bf(stack2(Z[0:C]))) + u["Y0"] for u, Z in zip(us, zs)]
        ys = [y_bd[0:C] + y_bd[C:2 * C] for y_bd in ybds]
        states = new_states
        y = jnp.concatenate(ys, axis=1)
        mu = head_sum(y) * (1.0 / HEAD_DIM)
        yc = y - mu
        var = head_sum(yc * yc) * (1.0 / HEAD_DIM)
        yn = yc * lax.rsqrt(var + RWKV_GN_EPS) * ln_w + ln_b
        o_ref[rows, :] = ((yn + bonus[rows]) * g[rows]).astype(BF16)
    for p in range(n_pair):
        state_ref[p] = states[p]


def _rwkv(rw, vecs, w2, a2, g2, B, S, lb):
    nt = S // lb
    return pl.pallas_call(
        functools.partial(_rwkv_kernel, n_chunk=lb // RWKV_CHUNK),
        grid=(B, nt),
        in_specs=[pl.BlockSpec((lb, RWKV_PROJ), lambda b, j: (b * nt + j, 0)),
                  pl.BlockSpec((8, RWKV_WIDTH), lambda b, j: (0, 0)),
                  pl.BlockSpec((LANES, RWKV_WIDTH), lambda b, j: (0, 0)),
                  pl.BlockSpec((LANES, RWKV_WIDTH), lambda b, j: (0, 0)),
                  pl.BlockSpec((LANES, RWKV_WIDTH), lambda b, j: (0, 0))],
        out_specs=pl.BlockSpec((lb, RWKV_WIDTH), lambda b, j: (b * nt + j, 0)),
        out_shape=jax.ShapeDtypeStruct((B * S, RWKV_WIDTH), BF16),
        scratch_shapes=[pltpu.VMEM((RWKV_WIDTH // LANES, LANES, LANES), F32)],
        compiler_params=pltpu.CompilerParams(dimension_semantics=("arbitrary", "arbitrary"),
                                             vmem_limit_bytes=VMEM_LIMIT),
        name="rwkv",
    )(rw, vecs, w2, a2, g2)


def _mix_kernel(at_ref, rk_ref, x_ref, mod_ref, wo_a_ref, wo_r_ref, ln_ref, wr_ref, br_ref,
                x1_ref, h2_ref, ti_ref, gt_ref, rank_ref, cnt_ref, tb_ref, base_ref):
    first = (pl.program_id(0) == 0) & (pl.program_id(1) == 0)

    @pl.when(first)
    def _():
        base_ref[...] = jnp.zeros_like(base_ref)

    mod = mod_ref[0]
    y = _dot(at_ref[...], wo_a_ref[...]) + _dot(rk_ref[...], wo_r_ref[...])
    x = x_ref[...]
    tm = x.shape[0]
    x1 = _layer_norm(DEEPNORM_ALPHA * x + (1.0 + mod[2:3, :]) * y) * ln_ref[0:1, :] + ln_ref[1:2, :]
    h2 = _layer_norm(x1) * (1.0 + mod[4:5, :]) + mod[3:4, :]
    x1_ref[...] = x1
    h_hi = h2.astype(BF16)
    h2_ref[...] = h_hi

    h_lo = (h2 - h_hi.astype(F32)).astype(BF16)
    part = _dot(h_hi, wr_ref[...])
    logits = part[:, 0:LANES] + part[:, LANES:2 * LANES] + _dot(h_lo, wr_ref[:, 0:LANES]) + br_ref[...]
    lt = jnp.transpose(logits)[0:N_EXPERTS, :]
    erow = lax.broadcasted_iota(jnp.int32, (N_EXPERTS, tm), 0).astype(F32)
    cur = lt
    vals, idxs = [], []
    for _ in range(TOP_K):
        m = jnp.max(cur, axis=0, keepdims=True)
        idx = jnp.min(jnp.where(cur == m, erow, float(N_EXPERTS)), axis=0, keepdims=True)
        vals.append(m)
        idxs.append(idx)
        cur = jnp.where(erow == idx, -jnp.inf, cur)
    tv = jnp.concatenate(vals, axis=0)
    e = jnp.exp(tv - tv[0:1, :])
    gt_ref[...] = e / jnp.sum(e, axis=0, keepdims=True)
    ti_ref[...] = jnp.concatenate(idxs, axis=0).astype(jnp.int32)

    onehot = jnp.zeros((N_EXPERTS, tm), F32)
    for idx in idxs:
        onehot = onehot + (erow == idx).astype(F32)
    before = (lax.broadcasted_iota(jnp.int32, (tm, tm), 0)
              < lax.broadcasted_iota(jnp.int32, (tm, tm), 1)).astype(BF16)
    tot = base_ref[:, 0:1] + _dot(onehot.astype(BF16), before)
    ranks = [jnp.sum(jnp.where(erow == idx, tot, 0.0), axis=0, keepdims=True) for idx in idxs]
    rank_ref[...] = jnp.concatenate(ranks, axis=0).astype(jnp.int32)
    tb_ref[0] = base_ref[...].astype(jnp.int32)
    base_ref[...] = base_ref[...] + jnp.sum(onehot, axis=1, keepdims=True)
    cnt_ref[...] = base_ref[...].astype(jnp.int32)


def _mix(attn_out, rwkv_out, x2d, mod, wo_a, wo_r, ln1, w_router, b_router, B, S, tm):
    D = x2d.shape[1]
    nt = S // tm
    T = B * S
    tok = lambda b, j: (b * nt + j, 0)
    col = lambda b, j: (0, b * nt + j)
    fixed = lambda b, j: (0, 0)
    return pl.pallas_call(
        _mix_kernel,
        grid=(B, nt),
        in_specs=[pl.BlockSpec((tm, ATTN_WIDTH), tok),
                  pl.BlockSpec((tm, RWKV_WIDTH), tok),
                  pl.BlockSpec((tm, D), tok),
                  pl.BlockSpec((1, 6, D), lambda b, j: (b, 0, 0)),
                  pl.BlockSpec((ATTN_WIDTH, D), fixed),
                  pl.BlockSpec((RWKV_WIDTH, D), fixed),
                  pl.BlockSpec((2, D), fixed),
                  pl.BlockSpec((D, 2 * LANES), fixed),
                  pl.BlockSpec((1, LANES), fixed)],
        out_specs=[pl.BlockSpec((tm, D), tok),
                   pl.BlockSpec((tm, D), tok),
                   pl.BlockSpec((TOP_K, tm), col),
                   pl.BlockSpec((TOP_K, tm), col),
                   pl.BlockSpec((TOP_K, tm), col),
                   pl.BlockSpec((N_EXPERTS, LANES), fixed),
                   pl.BlockSpec((1, N_EXPERTS, LANES), lambda b, j: (b * nt + j, 0, 0))],
        out_shape=[jax.ShapeDtypeStruct((T, D), F32),
                   jax.ShapeDtypeStruct((T, D), BF16),
                   jax.ShapeDtypeStruct((TOP_K, T), jnp.int32),
                   jax.ShapeDtypeStruct((TOP_K, T), F32),
                   jax.ShapeDtypeStruct((TOP_K, T), jnp.int32),
                   jax.ShapeDtypeStruct((N_EXPERTS, LANES), jnp.int32),
                   jax.ShapeDtypeStruct((T // tm, N_EXPERTS, LANES), jnp.int32)],
        scratch_shapes=[pltpu.VMEM((N_EXPERTS, LANES), F32)],
        compiler_params=pltpu.CompilerParams(dimension_semantics=("arbitrary", "arbitrary"),
                                             vmem_limit_bytes=VMEM_LIMIT),
        name="mix",
    )(attn_out, rwkv_out, x2d, mod, wo_a, wo_r, ln1, w_router, b_router)


RUN_PIECES = tuple(2 ** b for b in range(int(math.log2(MOE_TILE)), -1, -1))
SUBLANES = 8


def _to_tiles(ref, x):
    n = x.shape[0]
    for c in range(SUBLANES):
        ref[pl.ds(c, n, stride=SUBLANES), :] = x[:, c * LANES:(c + 1) * LANES]


def _from_tiles(ref):
    n = ref.shape[0] // SUBLANES
    return jnp.concatenate([ref[pl.ds(c, n, stride=SUBLANES), :] for c in range(SUBLANES)], axis=1)


def _run_copies(n, local, local_start, remote, remote_start, sem, to_remote):
    off = 0
    for piece in RUN_PIECES:
        take = (n & piece) != 0

        @pl.when(take)
        def _(off=off, piece=piece):
            lo = pl.multiple_of((local_start + off) * SUBLANES, SUBLANES)
            ro = pl.multiple_of((remote_start + off) * SUBLANES, SUBLANES)
            loc = local.at[pl.ds(lo, piece * SUBLANES)]
            rem = remote.at[pl.ds(ro, piece * SUBLANES)]
            src, dst = (loc, rem) if to_remote else (rem, loc)
            pltpu.make_async_copy(src, dst, sem).start()

        off = off + (n & piece)


def _dispatch_kernel(tcnt_ref, lstart_ref, gstart_ref, pad_ref, pad_start_ref, n_used_ref, lpos_ref, h2_ref,
                     xs_ref, xbuf, zbuf, sems):
    i = pl.program_id(0)
    tm = h2_ref.shape[0]
    n_loc = TOP_K * tm
    n_blocks = xs_ref.shape[0] // (MOE_BLOCK * SUBLANES)
    zero_sem = sems.at[2]

    @pl.when(i == 0)
    def _():
        zbuf[...] = jnp.zeros_like(zbuf)

        def zero_pad(e, carry):
            _run_copies(pad_ref[e], zbuf, 0, xs_ref, pad_start_ref[e], zero_sem, True)
            return carry

        def zero_tail(b, carry):
            @pl.when(b >= n_used_ref[0])
            def _():
                start = pl.multiple_of(b * (MOE_BLOCK * SUBLANES), MOE_BLOCK * SUBLANES)
                pltpu.make_async_copy(zbuf, xs_ref.at[pl.ds(start, MOE_BLOCK * SUBLANES)], zero_sem).start()
            return carry

        lax.fori_loop(0, N_EXPERTS, zero_pad, 0)
        lax.fori_loop(n_blocks - N_EXPERTS, n_blocks, zero_tail, 0)
        n_zero = N_EXPERTS * MOE_BLOCK * SUBLANES
        pltpu.make_async_copy(xs_ref.at[pl.ds(0, n_zero)], xs_ref.at[pl.ds(0, n_zero)], zero_sem).wait()

    slot = lax.broadcasted_iota(jnp.int32, (n_loc, tm), 0)
    lpos = lpos_ref[...]
    perm = jnp.zeros((n_loc, tm), F32)
    for k in range(TOP_K):
        perm = perm + (slot == lpos[k:k + 1, :]).astype(F32)
    rows = _dot(perm.astype(BF16), h2_ref[...].astype(BF16))

    def wait_tile(s):
        pltpu.make_async_copy(xbuf.at[s], xs_ref.at[pl.ds(0, n_loc * SUBLANES)], sems.at[s]).wait()

    for s in range(2):
        @pl.when((i % 2 == s) & (i >= 2))
        def _(s=s):
            wait_tile(s)

    for s in range(2):
        @pl.when(i % 2 == s)
        def _(s=s):
            _to_tiles(xbuf.at[s], rows)

            def issue(e, carry):
                idx = i * N_EXPERTS + e
                _run_copies(tcnt_ref[idx], xbuf.at[s], lstart_ref[idx], xs_ref, gstart_ref[idx], sems.at[s], True)
                return carry

            lax.fori_loop(0, N_EXPERTS, issue, 0)

    @pl.when(i == pl.num_programs(0) - 1)
    def _():
        for s in range(2):
            @pl.when((i % 2 == s) | (i >= 1))
            def _(s=s):
                wait_tile(s)


def _dispatch(tcnt, lstart, gstart, pad, pad_start, n_used, lpos, h2, n_rows, tm):
    T, D = h2.shape
    grid_spec = pltpu.PrefetchScalarGridSpec(
        num_scalar_prefetch=6,
        grid=(T // tm,),
        in_specs=[pl.BlockSpec((TOP_K, tm), lambda i, *_: (0, i)),
                  pl.BlockSpec((tm, D), lambda i, *_: (i, 0))],
        out_specs=pl.BlockSpec(memory_space=pl.ANY),
        scratch_shapes=[pltpu.VMEM((2, TOP_K * tm * SUBLANES, LANES), F32),
                        pltpu.VMEM((MOE_BLOCK * SUBLANES, LANES), F32),
                        pltpu.SemaphoreType.DMA((3,))],
    )
    return pl.pallas_call(
        _dispatch_kernel,
        grid_spec=grid_spec,
        out_shape=jax.ShapeDtypeStruct((n_rows * SUBLANES, LANES), F32),
        compiler_params=pltpu.CompilerParams(dimension_semantics=("arbitrary",),
                                             vmem_limit_bytes=VMEM_LIMIT),
        name="dispatch",
    )(tcnt, lstart, gstart, pad, pad_start, n_used, lpos, h2)


def _experts_kernel(blk_e_ref, n_used_ref, next_e_ref, xs_ref, wgu_hbm, bgu_ref, wd_hbm, bd_ref, ys_ref,
                    wgu_f32, wd_f32, wgu_bf, wd_bf, sems):
    i = pl.program_id(0)
    d_ff = wd_bf.shape[0]
    used = i < n_used_ref[0]
    e = blk_e_ref[i]
    new_expert = (i == 0) | (e != blk_e_ref[jnp.maximum(i - 1, 0)])

    def weight_copies(ex):
        return (pltpu.make_async_copy(wgu_hbm.at[ex], wgu_f32, sems.at[0]),
                pltpu.make_async_copy(wd_hbm.at[ex], wd_f32, sems.at[1]))

    @pl.when(i == 0)
    def _():
        for cp in weight_copies(e):
            cp.start()

    @pl.when(used & new_expert)
    def _():
        for cp in weight_copies(e):
            cp.wait()
        wgu_bf[...] = wgu_f32[...].astype(BF16)
        wd_bf[...] = wd_f32[...].astype(BF16)
        nxt = next_e_ref[i]

        @pl.when(nxt >= 0)
        def _():
            for cp in weight_copies(nxt):
                cp.start()

    @pl.when(used)
    def _():
        xb = _from_tiles(xs_ref).astype(BF16)
        gu = _dot(xb, wgu_bf[...]) + bgu_ref[0]
        gate = jnp.minimum(gu[:, :d_ff], SWIGLU_LIMIT)
        up = jnp.clip(gu[:, d_ff:], -SWIGLU_LIMIT, SWIGLU_LIMIT)
        act = (up + 1.0) * (gate * _sigmoid(SWIGLU_ALPHA * gate))
        _to_tiles(ys_ref, _dot(act.astype(BF16), wd_bf[...]) + bd_ref[0])

    @pl.when(i >= n_used_ref[0])
    def _():
        ys_ref[...] = jnp.zeros_like(ys_ref)


def _experts(blk_e, n_used, next_e, xs, wgu, bgu, wd, bd):
    d_ff, D = wd.shape[1], wd.shape[2]
    n_blocks = xs.shape[0] // (MOE_BLOCK * SUBLANES)
    blk = (MOE_BLOCK * SUBLANES, LANES)

    def last_used(i, n_used_ref):
        return jnp.minimum(i, jnp.maximum(n_used_ref[0] - 1, 0))

    def row_map(i, blk_e_ref, n_used_ref, next_e_ref):
        return (last_used(i, n_used_ref), 0)

    def exp_map(i, blk_e_ref, n_used_ref, next_e_ref):
        return (blk_e_ref[last_used(i, n_used_ref)], 0, 0)

    grid_spec = pltpu.PrefetchScalarGridSpec(
        num_scalar_prefetch=3,
        grid=(n_blocks,),
        in_specs=[pl.BlockSpec(blk, row_map),
                  pl.BlockSpec(memory_space=pl.ANY),
                  pl.BlockSpec((1, 1, 2 * d_ff), exp_map),
                  pl.BlockSpec(memory_space=pl.ANY),
                  pl.BlockSpec((1, 1, D), exp_map)],
        out_specs=pl.BlockSpec(blk, lambda i, *_: (i, 0)),
        scratch_shapes=[pltpu.VMEM((D, 2 * d_ff), F32), pltpu.VMEM((d_ff, D), F32),
                        pltpu.VMEM((D, 2 * d_ff), BF16), pltpu.VMEM((d_ff, D), BF16),
                        pltpu.SemaphoreType.DMA((2,))],
    )
    return pl.pallas_call(
        _experts_kernel,
        grid_spec=grid_spec,
        out_shape=jax.ShapeDtypeStruct(xs.shape, F32),
        compiler_params=pltpu.CompilerParams(dimension_semantics=("arbitrary",),
                                             vmem_limit_bytes=VMEM_LIMIT),
        name="experts",
    )(blk_e, n_used, next_e, xs, wgu, bgu, wd, bd)


def _combine_kernel(tcnt_ref, lstart_ref, gstart_ref, ys_ref, lpos_ref, gt_ref, x1_ref, mod_ref, ln_ref,
                    o_ref, buf, y_ref, sems):
    nt = pl.num_programs(1)
    n_tiles = pl.num_programs(0) * nt
    i = pl.program_id(0) * nt + pl.program_id(1)
    tm = x1_ref.shape[0]
    n_loc = TOP_K * tm

    def fetch(tile, s):
        def issue(e, carry):
            idx = tile * N_EXPERTS + e
            _run_copies(tcnt_ref[idx], buf.at[s], lstart_ref[idx], ys_ref, gstart_ref[idx], sems.at[s], False)
            return carry

        lax.fori_loop(0, N_EXPERTS, issue, 0)

    @pl.when(i == 0)
    def _():
        fetch(i, 0)

    for s in range(2):
        @pl.when(((i + 1) % 2 == s) & (i + 1 < n_tiles))
        def _(s=s):
            fetch(i + 1, s)

    slot = lax.broadcasted_iota(jnp.int32, (tm, n_loc), 1)
    lpos = lpos_ref[...]
    gt = gt_ref[...]
    pick = jnp.zeros((tm, n_loc), F32)
    for k in range(TOP_K):
        pick = pick + jnp.where(slot == lpos[:, k:k + 1], gt[:, k:k + 1], 0.0)
    pick = pick.astype(BF16)
    for s in range(2):
        @pl.when(i % 2 == s)
        def _(s=s):
            pltpu.make_async_copy(ys_ref.at[pl.ds(0, n_loc * SUBLANES)], buf.at[s], sems.at[s]).wait()
            y_ref[...] = _dot(pick, _from_tiles(buf.at[s]).astype(BF16))
    y = y_ref[...]
    mod = mod_ref[0]
    z = DEEPNORM_ALPHA * x1_ref[...] + (1.0 + mod[5:6, :]) * y
    o_ref[...] = _layer_norm(z) * ln_ref[0:1, :] + ln_ref[1:2, :]


def _combine(tcnt, lstart, gstart, ys, lpos_t, gates_t, x1, mod, ln2, B, S, tm):
    T, D = x1.shape
    nt = S // tm
    tok = lambda b, j, *_: (b * nt + j, 0)
    grid_spec = pltpu.PrefetchScalarGridSpec(
        num_scalar_prefetch=3,
        grid=(B, nt),
        in_specs=[pl.BlockSpec(memory_space=pl.ANY),
                  pl.BlockSpec((tm, TOP_K), tok),
                  pl.BlockSpec((tm, TOP_K), tok),
                  pl.BlockSpec((tm, D), tok),
                  pl.BlockSpec((1, 6, D), lambda b, j, *_: (b, 0, 0)),
                  pl.BlockSpec((2, D), lambda b, j, *_: (0, 0))],
        out_specs=pl.BlockSpec((tm, D), tok),
        scratch_shapes=[pltpu.VMEM((2, TOP_K * tm * SUBLANES, LANES), F32), pltpu.VMEM((tm, D), F32),
                        pltpu.SemaphoreType.DMA((2,))],
    )
    return pl.pallas_call(
        _combine_kernel,
        grid_spec=grid_spec,
        out_shape=jax.ShapeDtypeStruct((T, D), F32),
        compiler_params=pltpu.CompilerParams(dimension_semantics=("arbitrary", "arbitrary"),
                                             vmem_limit_bytes=VMEM_LIMIT),
        name="combine",
    )(tcnt, lstart, gstart, ys, lpos_t, gates_t, x1, mod, ln2)


def _pad_rows(w, rows):
    return jnp.pad(w, ((0, rows - w.shape[0]), (0, 0)))


def _pad_cols(w, cols):
    return jnp.pad(w, ((0, 0), (0, cols - w.shape[1])))


def _layer(x, c, positions, w_ada, b_ada, w_in, shift_mu, rwkv_w0, rwkv_w2, rwkv_a0, rwkv_a2, rwkv_g2,
           rwkv_k_k, rwkv_k_a, rwkv_r_k, rwkv_ln_w, rwkv_ln_b, attn_sinks, w_out, ln1_g, ln1_b,
           w_router, b_router, w_gate_up, b_gate_up, w_down, b_down, ln2_g, ln2_b):
    B, S, D = x.shape
    T = B * S
    tile = min(256, S)

    q0, k0, v0 = 0, ATTN_WIDTH, ATTN_WIDTH + KV_WIDTH
    r0 = ATTN_WIDTH + 2 * KV_WIDTH
    heads = lambda base: [w_in[:, base + h * HEAD_DIM: base + (h + 1) * HEAD_DIM] for h in range(N_KV_HEADS)]
    dup = lambda hs: [w for w in hs for _ in range(2)]
    w_attn = jnp.concatenate([w_in[:, q0:q0 + ATTN_WIDTH]] + dup(heads(k0)) + dup(heads(v0)), axis=1).astype(BF16)
    lora0 = r0 + 3 * RWKV_WIDTH
    lora = (DECAY_LORA, AAA_LORA, GATE_LORA)
    pieces_w = [w_in[:, r0:lora0]]
    pieces_mu = [shift_mu[None, 0:3 * RWKV_WIDTH]]
    off = lora0
    for n in lora:
        pieces_w.append(_pad_cols(w_in[:, off:off + n], LANES))
        pieces_mu.append(_pad_cols(shift_mu[None, off - r0:off - r0 + n], LANES))
        off += n
    w_rwkv = jnp.concatenate(pieces_w, axis=1).astype(BF16)
    mu = jnp.concatenate(pieces_mu, axis=1)
    inv_freq = ROPE_THETA ** (-jnp.arange(0, ROT_DIM, 2, dtype=F32) / ROT_DIM)
    lane_p = jnp.arange(LANES) % HEAD_DIM
    rot_tab = jnp.zeros((8, LANES), F32)
    rot_tab = rot_tab.at[0].set(jnp.where(lane_p < ROT_DIM, inv_freq[lane_p % (ROT_DIM // 2)], 0.0))
    rot_tab = rot_tab.at[1].set(jnp.where(lane_p < ROT_DIM // 2, -1.0, 0.0))
    rot_tab = rot_tab.at[2].set(jnp.where((lane_p >= ROT_DIM // 2) & (lane_p < ROT_DIM), 1.0, 0.0))
    vecs = jnp.stack([rwkv_w0, rwkv_a0, rwkv_k_k, rwkv_k_a, rwkv_r_k.reshape(-1), rwkv_ln_w, rwkv_ln_b,
                      jnp.zeros_like(rwkv_w0)])
    w2 = _pad_rows(rwkv_w2, LANES).astype(BF16)
    a2 = _pad_rows(rwkv_a2, LANES).astype(BF16)
    g2 = _pad_rows(rwkv_g2, LANES).astype(BF16)
    wo_a = w_out[:ATTN_WIDTH].astype(BF16)
    wo_r = w_out[ATTN_WIDTH:].astype(BF16)
    w_r_hi = w_router.astype(BF16)
    w_r_lo = (w_router - w_r_hi.astype(F32)).astype(BF16)
    w_r = jnp.concatenate([_pad_cols(w_r_hi, LANES), _pad_cols(w_r_lo, LANES)], axis=1)
    b_r = jnp.concatenate([b_router, jnp.full((LANES - N_EXPERTS,), NEG_INF, F32)])[None, :]

    mod = _mod(c, w_ada, b_ada).reshape(B, 6, D)
    qkv, rw = _inproj(x, positions, mod, w_attn, w_rwkv, mu, rot_tab, tile)
    attn_out = _attention(qkv, attn_sinks, B, S)
    rwkv_out = _rwkv(rw, vecs, w2, a2, g2, B, S, min(128, S))

    mtile = min(MOE_TILE, S)
    x1, h2, top_i, gates, rank, cnt, tbase = _mix(attn_out, rwkv_out, x.reshape(T, D), mod, wo_a, wo_r,
                                                   jnp.stack([ln1_g, ln1_b]), w_r, b_r, B, S, mtile)

    counts = cnt[:, 0]
    padded = (counts + MOE_BLOCK - 1) // MOE_BLOCK * MOE_BLOCK
    pend = jnp.cumsum(padded)
    pstart = pend - padded
    n_blocks = T * TOP_K // MOE_BLOCK + N_EXPERTS
    blk_row = jnp.arange(n_blocks, dtype=jnp.int32) * MOE_BLOCK
    blk_e = jnp.minimum(jnp.sum((pend[None, :] <= blk_row[:, None]).astype(jnp.int32), axis=1), N_EXPERTS - 1)
    n_used = (pend[-1:] // MOE_BLOCK).astype(jnp.int32)
    tb = tbase[:, :, 0]
    tcnt = jnp.concatenate([tb[1:], counts[None]], axis=0) - tb
    lstart = jnp.cumsum(tcnt, axis=1) - tcnt
    gstart = pstart[None, :] + tb
    shift = jnp.repeat(jnp.transpose(lstart - tb), mtile, axis=1)
    experts = jnp.arange(N_EXPERTS, dtype=jnp.int32)
    lpos = rank + jnp.sum(jnp.where(top_i[None] == experts[:, None, None], shift[:, None, :], 0), axis=0)
    flat = lambda a: a.reshape(-1).astype(jnp.int32)

    xs = _dispatch(flat(tcnt), flat(lstart), flat(gstart), flat(padded - counts), flat(pstart + counts), n_used,
                   lpos, h2, n_blocks * MOE_BLOCK, mtile)
    later_with_rows = (experts[None, :] > experts[:, None]) & (counts[None, :] > 0)
    next_of = jnp.min(jnp.where(later_with_rows, experts[None, :], N_EXPERTS), axis=1)
    next_of = jnp.where(next_of < N_EXPERTS, next_of, -1)
    next_e = jnp.sum(jnp.where(blk_e[:, None] == experts[None, :], next_of[None, :], 0), axis=1).astype(jnp.int32)
    ys = _experts(blk_e, n_used, next_e, xs, w_gate_up, b_gate_up[:, None, :], w_down, b_down[:, None, :])
    out = _combine(flat(tcnt), flat(lstart), flat(gstart), ys, jnp.transpose(lpos), jnp.transpose(gates), x1, mod,
                   jnp.stack([ln2_g, ln2_b]), B, S, mtile)
    return out.reshape(B, S, D)


def kernel(x, c, positions, w_ada, b_ada, w_in, shift_mu, rwkv_w0, rwkv_w2, rwkv_a0, rwkv_a2, rwkv_g2,
           rwkv_k_k, rwkv_k_a, rwkv_r_k, rwkv_ln_w, rwkv_ln_b, attn_sinks, w_out, ln1_g, ln1_b,
           w_router, b_router, w_gate_up, b_gate_up, w_down, b_down, ln2_g, ln2_b):
    for l in range(DEPTH):
        x = _layer(x, c, positions, w_ada[l], b_ada[l], w_in[l], shift_mu[l], rwkv_w0[l], rwkv_w2[l],
                   rwkv_a0[l], rwkv_a2[l], rwkv_g2[l], rwkv_k_k[l], rwkv_k_a[l], rwkv_r_k[l], rwkv_ln_w[l],
                   rwkv_ln_b[l], attn_sinks[l], w_out[l], ln1_g[l], ln1_b[l], w_router[l], b_router[l],
                   w_gate_up[l], b_gate_up[l], w_down[l], b_down[l], ln2_g[l], ln2_b[l])
    return x
```

```python
import functools
import math

import jax
import jax.numpy as jnp
from jax import lax
from jax.experimental import pallas as pl
from jax.experimental.pallas import tpu as pltpu

F32 = jnp.float32
BF16 = jnp.bfloat16

HEAD_DIM = 64
N_ATTN_HEADS = 8
N_KV_HEADS = 2
N_RWKV_HEADS = 8
ATTN_WIDTH = N_ATTN_HEADS * HEAD_DIM
KV_WIDTH = N_KV_HEADS * HEAD_DIM
RWKV_WIDTH = N_RWKV_HEADS * HEAD_DIM
ATTN_BLOCK = 128
ROT_DIM = HEAD_DIM // 4
ROPE_THETA = 500000.0
DECAY_LORA = 32
AAA_LORA = 32
GATE_LORA = 96
N_EXPERTS = 32
TOP_K = 4
SWIGLU_LIMIT = 7.0
SWIGLU_ALPHA = 1.702
LN_EPS = 1e-5
RWKV_GN_EPS = 64e-5
NEG_INF = -1e30
DEPTH = 1
DEEPNORM_ALPHA = (2 * DEPTH) ** 0.25

LANES = 128
RWKV_CHUNK = 64
MOE_BLOCK = 512
MOE_TILE = 256
ATTN_PROJ = ATTN_WIDTH + 4 * KV_WIDTH
RWKV_PROJ = 3 * RWKV_WIDTH + 3 * LANES
VMEM_LIMIT = 48 * 1024 * 1024


def _dot(a, b):
    return jnp.dot(a, b, preferred_element_type=F32)


def _dot_nt(a, b):
    return lax.dot_general(a, b, (((1,), (1,)), ((), ())), preferred_element_type=F32)


def _dot_tn(a, b):
    return lax.dot_general(a, b, (((0,), (0,)), ((), ())), preferred_element_type=F32)


def _split3(x):
    h = x.astype(BF16)
    r1 = x - h.astype(F32)
    m = r1.astype(BF16)
    lo = (r1 - m.astype(F32)).astype(BF16)
    return h, m, lo


def _dot_exact_lhs(m_bf16, x):
    h, m, lo = _split3(x)
    return _dot(m_bf16, h) + _dot(m_bf16, m) + _dot(m_bf16, lo)


def _layer_norm(x):
    mu = jnp.mean(x, axis=-1, keepdims=True)
    xc = x - mu
    var = jnp.mean(xc * xc, axis=-1, keepdims=True)
    return xc * lax.rsqrt(var + LN_EPS)


def _sigmoid(x):
    return 1.0 / (1.0 + jnp.exp(-x))


def _mod_kernel(c_ref, w_ref, b_ref, o_ref):
    c = c_ref[...]
    s = c * _sigmoid(c)
    o_ref[...] = jnp.dot(s, w_ref[...], preferred_element_type=F32,
                         precision=lax.Precision.HIGHEST) + b_ref[...]


def _mod(c, w_ada, b_ada):
    B, D = c.shape
    n = w_ada.shape[1] // D
    return pl.pallas_call(
        _mod_kernel,
        grid=(n,),
        in_specs=[pl.BlockSpec((B, D), lambda i: (0, 0)),
                  pl.BlockSpec((D, D), lambda i: (0, i)),
                  pl.BlockSpec((1, D), lambda i: (0, i))],
        out_specs=pl.BlockSpec((B, D), lambda i: (0, i)),
        out_shape=jax.ShapeDtypeStruct((B, n * D), F32),
        compiler_params=pltpu.CompilerParams(dimension_semantics=("arbitrary",),
                                             vmem_limit_bytes=VMEM_LIMIT),
        name="mod",
    )(c, w_ada, b_ada.reshape(1, -1))


def _inproj_kernel(x_ref, pos_ref, mod_ref, wa_ref, wr_ref, mu_ref, rt_ref,
                   qkv_ref, rw_ref, carry_ref):
    j = pl.program_id(1)
    x = x_ref[0]
    tm = x.shape[0]
    mod = mod_ref[0]
    h = _layer_norm(x) * (1.0 + mod[1:2, :]) + mod[0:1, :]
    hb = h.astype(BF16)

    pa = _dot(hb, wa_ref[...])
    ang = pos_ref[0].astype(F32) * rt_ref[0:1, :]
    cs = jnp.cos(ang)
    sn = jnp.sin(ang)
    m_lo = rt_ref[1:2, :]
    m_hi = rt_ref[2:3, :]
    n_q = ATTN_WIDTH // LANES
    n_rot = (ATTN_WIDTH + 2 * KV_WIDTH) // LANES
    for ch in range(n_rot):
        t = pa[:, ch * LANES:(ch + 1) * LANES]
        if ch < n_q:
            t = t * (1.0 / math.sqrt(HEAD_DIM))
        up = pltpu.roll(t, LANES - ROT_DIM // 2, 1)
        dn = pltpu.roll(t, ROT_DIM // 2, 1)
        o = t * cs + sn * (m_lo * up + m_hi * dn)
        qkv_ref[:, ch * LANES:(ch + 1) * LANES] = o.astype(BF16)
    qkv_ref[:, n_rot * LANES:] = pa[:, n_rot * LANES:].astype(BF16)

    pr = _dot(hb, wr_ref[...])
    prev = pltpu.roll(pr, 1, 0)
    row = lax.broadcasted_iota(jnp.int32, (tm, 1), 0)
    carry = jnp.where(j == 0, 0.0, carry_ref[...])
    prev = jnp.where(row == 0, carry, prev)
    carry_ref[...] = pr[tm - 1:tm, :]
    rw_ref[...] = pr + (prev - pr) * mu_ref[...]


def _inproj(x, positions, mod, w_attn, w_rwkv, mu, rot_tab, tm):
    B, S, D = x.shape
    nt = S // tm
    return pl.pallas_call(
        _inproj_kernel,
        grid=(B, nt),
        in_specs=[pl.BlockSpec((1, tm, D), lambda b, j: (b, j, 0)),
                  pl.BlockSpec((1, tm, 1), lambda b, j: (b, j, 0)),
                  pl.BlockSpec((1, 6, D), lambda b, j: (b, 0, 0)),
                  pl.BlockSpec((D, ATTN_PROJ), lambda b, j: (0, 0)),
                  pl.BlockSpec((D, RWKV_PROJ), lambda b, j: (0, 0)),
                  pl.BlockSpec((1, RWKV_PROJ), lambda b, j: (0, 0)),
                  pl.BlockSpec((8, LANES), lambda b, j: (0, 0))],
        out_specs=[pl.BlockSpec((tm, ATTN_PROJ), lambda b, j: (b * nt + j, 0)),
                   pl.BlockSpec((tm, RWKV_PROJ), lambda b, j: (b * nt + j, 0))],
        out_shape=[jax.ShapeDtypeStruct((B * S, ATTN_PROJ), BF16),
                   jax.ShapeDtypeStruct((B * S, RWKV_PROJ), F32)],
        scratch_shapes=[pltpu.VMEM((1, RWKV_PROJ), F32)],
        compiler_params=pltpu.CompilerParams(dimension_semantics=("arbitrary", "arbitrary"),
                                             vmem_limit_bytes=VMEM_LIMIT),
        name="inproj",
    )(x, positions.reshape(B, S, 1), mod, w_attn, w_rwkv, mu, rot_tab)


def _attn_kernel(q_ref, kc_ref, kp_ref, vc_ref, vp_ref, sink_ref, o_ref):
    i = pl.program_id(1)
    blk = ATTN_BLOCK
    qi = lax.broadcasted_iota(jnp.int32, (blk, 2 * blk), 0)
    kj = lax.broadcasted_iota(jnp.int32, (blk, 2 * blk), 1)
    allowed = (kj > qi) & (kj <= qi + blk) & ((kj >= blk) | (i > 0))
    lane = lax.broadcasted_iota(jnp.int32, (1, LANES), 1)
    lo = (lane < HEAD_DIM).astype(BF16)
    hi = (lane >= HEAD_DIM).astype(BF16)
    halves = []
    for g in range(N_KV_HEADS):
        sl = slice(g * LANES, (g + 1) * LANES)
        kcat = jnp.concatenate([kp_ref[:, sl], kc_ref[:, sl]], axis=0)
        vcat = jnp.concatenate([vp_ref[:, sl], vc_ref[:, sl]], axis=0)
        halves.append(((kcat * lo, vcat * lo), (kcat * hi, vcat * hi)))
    heads = [(c, half) for c in range(ATTN_WIDTH // LANES) for half in range(2)]
    scores = [_dot_nt(q_ref[:, c * LANES:(c + 1) * LANES], halves[c // 2][half][0]) for c, half in heads]
    probs, denoms = [], []
    for (c, half), s in zip(heads, scores):
        sink = sink_ref[2 * c + half]
        s = jnp.where(allowed, s, NEG_INF)
        m = jnp.maximum(jnp.max(s, axis=-1, keepdims=True), sink)
        p = jnp.exp(s - m)
        denoms.append(jnp.sum(p, axis=-1, keepdims=True) + jnp.exp(sink - m))
        probs.append(p.astype(BF16))
    outs = [_dot(p, halves[c // 2][half][1]) / d for (c, half), p, d in zip(heads, probs, denoms)]
    for c in range(ATTN_WIDTH // LANES):
        o_ref[:, c * LANES:(c + 1) * LANES] = (outs[2 * c] + outs[2 * c + 1]).astype(BF16)


def _attention(qkv, sinks, B, S):
    nb = S // ATTN_BLOCK
    blk = ATTN_BLOCK
    kcol = ATTN_WIDTH // (2 * KV_WIDTH)
    vcol = kcol + 1
    cur = lambda col: (lambda b, i: (b * nb + i, col))
    prv = lambda col: (lambda b, i: (jnp.maximum(b * nb + i - 1, 0), col))
    return pl.pallas_call(
        _attn_kernel,
        grid=(B, nb),
        in_specs=[pl.BlockSpec((blk, ATTN_WIDTH), lambda b, i: (b * nb + i, 0)),
                  pl.BlockSpec((blk, 2 * KV_WIDTH), cur(kcol)),
                  pl.BlockSpec((blk, 2 * KV_WIDTH), prv(kcol)),
                  pl.BlockSpec((blk, 2 * KV_WIDTH), cur(vcol)),
                  pl.BlockSpec((blk, 2 * KV_WIDTH), prv(vcol)),
                  pl.BlockSpec(memory_space=pltpu.SMEM)],
        out_specs=pl.BlockSpec((blk, ATTN_WIDTH), lambda b, i: (b * nb + i, 0)),
        out_shape=jax.ShapeDtypeStruct((B * S, ATTN_WIDTH), BF16),
        compiler_params=pltpu.CompilerParams(dimension_semantics=("arbitrary", "arbitrary"),
                                             vmem_limit_bytes=VMEM_LIMIT),
        name="attn",
    )(qkv, qkv, qkv, qkv, qkv, sinks)


def _rwkv_kernel(rw_ref, vec_ref, w2_ref, a2_ref, g2_ref, o_ref, state_ref, *, n_chunk):
    j = pl.program_id(1)
    C = RWKV_CHUNK
    W = RWKV_WIDTH
    n_pair = W // LANES

    @pl.when(j == 0)
    def _():
        state_ref[...] = jnp.zeros_like(state_ref)

    w0 = vec_ref[0:1, :]
    a0 = vec_ref[1:2, :]
    k_k = vec_ref[2:3, :]
    k_a = vec_ref[3:4, :]
    r_k = vec_ref[4:5, :]
    ln_w = vec_ref[5:6, :]
    ln_b = vec_ref[6:7, :]

    r = rw_ref[:, 0:W]
    k = rw_ref[:, W:2 * W]
    v = rw_ref[:, 2 * W:3 * W]
    wl = rw_ref[:, 3 * W:3 * W + LANES]
    al = rw_ref[:, 3 * W + LANES:3 * W + 2 * LANES]
    gl = rw_ref[:, 3 * W + 2 * LANES:3 * W + 3 * LANES]

    ri = lax.broadcasted_iota(jnp.int32, (LANES, LANES), 0)
    ci = lax.broadcasted_iota(jnp.int32, (LANES, LANES), 1)
    same = (ri // HEAD_DIM) == (ci // HEAD_DIM)
    strict = same & ((ri % HEAD_DIM) > (ci % HEAD_DIM))
    incl = same & ((ri % HEAD_DIM) >= (ci % HEAD_DIM))
    lane = lax.broadcasted_iota(jnp.int32, (1, LANES), 1)
    m0 = (lane < HEAD_DIM).astype(F32)
    m1 = 1.0 - m0
    tri = (lax.broadcasted_iota(jnp.int32, (C, C), 0) >= lax.broadcasted_iota(jnp.int32, (C, C), 1)).astype(BF16)

    def head_sum(xv):
        outs = []
        for p in range(n_pair):
            xp = xv[:, p * LANES:(p + 1) * LANES]
            s0 = jnp.sum(xp * m0, axis=1, keepdims=True)
            s1 = jnp.sum(xp * m1, axis=1, keepdims=True)
            outs.append(s0 * m0 + s1 * m1)
        return jnp.concatenate(outs, axis=1)

    def stack2(xp):
        return jnp.concatenate([xp * m0, xp * m1], axis=0)

    z = w0 + _dot(jnp.tanh(wl).astype(BF16), w2_ref[...])
    lw = -math.exp(-0.5) * _sigmoid(z)
    a = _sigmoid(a0 + _dot(al.astype(BF16), a2_ref[...]))
    g = _dot(_sigmoid(gl).astype(BF16), g2_ref[...])
    kk = k * k_k
    kkn = kk / jnp.maximum(jnp.sqrt(head_sum(kk * kk)), 1e-12)
    k2 = k * (1.0 + (a - 1.0) * k_a)
    av = -kkn
    bv = kkn * a
    bonus = head_sum(r * k2 * r_k) * v

    eye = (ri == ci).astype(F32)
    bf = lambda t: t.astype(BF16)

    pre = []
    for c in range(n_chunk):
        rows = slice(c * C, (c + 1) * C)
        lwc = lw[rows]
        cw = _dot_exact_lhs(tri, lwc)
        cwl = cw[C - 1:C, :]
        e_in = jnp.exp(cw)
        e_neg = jnp.exp(-cw)
        e_rem = jnp.exp(cwl - cw)
        wc = jnp.exp(cwl)
        Rt = r[rows] * e_in
        At = av[rows] * jnp.exp(cw - lwc)
        Bb = bv[rows] * e_neg
        Kb = k2[rows] * e_neg
        Bh = bv[rows] * e_rem
        Kh = k2[rows] * e_rem
        vc = v[rows]
        for p in range(n_pair):
            sl = slice(p * LANES, (p + 1) * LANES)
            pre.append(dict(At=At[:, sl], Rt=Rt[:, sl], Bb=Bb[:, sl], Kb=Kb[:, sl], Bh=Bh[:, sl], Kh=Kh[:, sl],
                            v=vc[:, sl], wc=wc[:, sl]))

    for u in pre:
        u["at_bd"] = stack2(u["At"])
        lhs = bf(jnp.concatenate([u["at_bd"], stack2(u["Rt"])], axis=0))
        rhs = bf(jnp.concatenate([stack2(u["Bb"]), stack2(u["Kb"])], axis=0))
        u["G"] = _dot_nt(lhs, rhs)
    for u in pre:
        G = u.pop("G")
        u["a_ab"] = jnp.where(strict, G[0:2 * C, 0:2 * C], 0.0)
        u["a_ak"] = bf(jnp.where(strict, G[0:2 * C, 2 * C:4 * C], 0.0))
        u["a_rb"] = bf(jnp.where(incl, G[2 * C:4 * C, 0:2 * C], 0.0))
        u["a_rk"] = bf(jnp.where(incl, G[2 * C:4 * C, 2 * C:4 * C], 0.0))
        u["v_bd"] = bf(stack2(u["v"]))
    for u in pre:
        xb = bf(u["a_ab"])
        u["P"] = eye + u.pop("a_ab")
        u["X"] = _dot(xb, xb)
        u["M0"] = _dot(u["a_ak"], u["v_bd"])
    for _ in range(int(math.log2(C)) - 2):
        for u in pre:
            Wm = _dot(bf(u["X"]), bf(jnp.concatenate([u["P"], u["X"]], axis=1)))
            u["P"] = u["P"] + Wm[:, 0:LANES]
            u["X"] = Wm[:, LANES:2 * LANES]
    for u in pre:
        u["P"] = bf(u["P"] + _dot(bf(u.pop("X")), bf(u["P"])))
    for u in pre:
        u["M1"] = _dot(u["P"], bf(u.pop("M0")))
        u["Q"] = bf(_dot(u["a_rb"], u["P"]))
        u["PtB"] = _dot_tn(u["P"], bf(stack2(u["Bh"])))
    for u in pre:
        M1 = u.pop("M1")
        u["Y0"] = _dot(jnp.concatenate([u["a_rb"], u["a_rk"]], axis=1),
                       jnp.concatenate([bf(M1), u["v_bd"]], axis=0))
        u["Tm"] = bf(_dot_tn(bf(u["at_bd"]), bf(u.pop("PtB"))))
        m1_pair = M1[0:C] + M1[C:2 * C]
        cst = _dot_tn(bf(jnp.concatenate([m1_pair, u["v"]], axis=0)),
                      bf(jnp.concatenate([u["Bh"], u["Kh"]], axis=0)))
        u["cst"] = jnp.where(same, cst, 0.0)
        u["ar"] = bf(jnp.concatenate([u["At"], u["Rt"]], axis=0))

    states = [state_ref[p] for p in range(n_pair)]
    for c in range(n_chunk):
        rows = slice(c * C, (c + 1) * C)
        us = pre[c * n_pair:(c + 1) * n_pair]
        sbs = [bf(S) for S in states]
        zs = [_dot_nt(u["ar"], sb) for u, sb in zip(us, sbs)]
        new_states = [S * u["wc"] + _dot(sb, u["Tm"]) + u["cst"] for u, S, sb in zip(us, states, sbs)]
        ybds = [stack2(Z[C:2 * C]) + _dot(u["Q"], bf(stack2(Z[0:C]))) + u["Y0"] for u, Z in zip(us, zs)]
        ys = [y_bd[0:C] + y_bd[C:2 * C] for y_bd in ybds]
        states = new_states
        y = jnp.concatenate(ys, axis=1)
        mu = head_sum(y) * (1.0 / HEAD_DIM)
        yc = y - mu
        var = head_sum(yc * yc) * (1.0 / HEAD_DIM)
        yn = yc * lax.rsqrt(var + RWKV_GN_EPS) * ln_w + ln_b
        o_ref[rows, :] = ((yn + bonus[rows]) * g[rows]).astype(BF16)
    for p in range(n_pair):
        state_ref[p] = states[p]


def _rwkv(rw, vecs, w2, a2, g2, B, S, lb):
    nt = S // lb
    return pl.pallas_call(
        functools.partial(_rwkv_kernel, n_chunk=lb // RWKV_CHUNK),
        grid=(B, nt),
        in_specs=[pl.BlockSpec((lb, RWKV_PROJ), lambda b, j: (b * nt + j, 0)),
                  pl.BlockSpec((8, RWKV_WIDTH), lambda b, j: (0, 0)),
                  pl.BlockSpec((LANES, RWKV_WIDTH), lambda b, j: (0, 0)),
                  pl.BlockSpec((LANES, RWKV_WIDTH), lambda b, j: (0, 0)),
                  pl.BlockSpec((LANES, RWKV_WIDTH), lambda b, j: (0, 0))],
        out_specs=pl.BlockSpec((lb, RWKV_WIDTH), lambda b, j: (b * nt + j, 0)),
        out_shape=jax.ShapeDtypeStruct((B * S, RWKV_WIDTH), BF16),
        scratch_shapes=[pltpu.VMEM((RWKV_WIDTH // LANES, LANES, LANES), F32)],
        compiler_params=pltpu.CompilerParams(dimension_semantics=("arbitrary", "arbitrary"),
                                             vmem_limit_bytes=VMEM_LIMIT),
        name="rwkv",
    )(rw, vecs, w2, a2, g2)


def _mix_kernel(at_ref, rk_ref, x_ref, mod_ref, wo_a_ref, wo_r_ref, ln_ref, wr_ref, br_ref,
                x1_ref, h2_ref, ti_ref, gt_ref, rank_ref, cnt_ref, tb_ref, base_ref):
    first = (pl.program_id(0) == 0) & (pl.program_id(1) == 0)

    @pl.when(first)
    def _():
        base_ref[...] = jnp.zeros_like(base_ref)

    mod = mod_ref[0]
    y = _dot(at_ref[...], wo_a_ref[...]) + _dot(rk_ref[...], wo_r_ref[...])
    x = x_ref[...]
    tm = x.shape[0]
    x1 = _layer_norm(DEEPNORM_ALPHA * x + (1.0 + mod[2:3, :]) * y) * ln_ref[0:1, :] + ln_ref[1:2, :]
    h2 = _layer_norm(x1) * (1.0 + mod[4:5, :]) + mod[3:4, :]
    x1_ref[...] = x1
    h_hi = h2.astype(BF16)
    h2_ref[...] = h_hi

    h_lo = (h2 - h_hi.astype(F32)).astype(BF16)
    part = _dot(h_hi, wr_ref[...])
    logits = part[:, 0:LANES] + part[:, LANES:2 * LANES] + _dot(h_lo, wr_ref[:, 0:LANES]) + br_ref[...]
    lt = jnp.transpose(logits)[0:N_EXPERTS, :]
    erow = lax.broadcasted_iota(jnp.int32, (N_EXPERTS, tm), 0).astype(F32)
    cur = lt
    vals, idxs = [], []
    for _ in range(TOP_K):
        m = jnp.max(cur, axis=0, keepdims=True)
        idx = jnp.min(jnp.where(cur == m, erow, float(N_EXPERTS)), axis=0, keepdims=True)
        vals.append(m)
        idxs.append(idx)
        cur = jnp.where(erow == idx, -jnp.inf, cur)
    tv = jnp.concatenate(vals, axis=0)
    e = jnp.exp(tv - tv[0:1, :])
    gt_ref[...] = e / jnp.sum(e, axis=0, keepdims=True)
    ti_ref[...] = jnp.concatenate(idxs, axis=0).astype(jnp.int32)

    onehot = jnp.zeros((N_EXPERTS, tm), F32)
    for idx in idxs:
        onehot = onehot + (erow == idx).astype(F32)
    before = (lax.broadcasted_iota(jnp.int32, (tm, tm), 0)
              < lax.broadcasted_iota(jnp.int32, (tm, tm), 1)).astype(BF16)
    tot = base_ref[:, 0:1] + _dot(onehot.astype(BF16), before)
    ranks = [jnp.sum(jnp.where(erow == idx, tot, 0.0), axis=0, keepdims=True) for idx in idxs]
    rank_ref[...] = jnp.concatenate(ranks, axis=0).astype(jnp.int32)
    tb_ref[0] = base_ref[...].astype(jnp.int32)
    base_ref[...] = base_ref[...] + jnp.sum(onehot, axis=1, keepdims=True)
    cnt_ref[...] = base_ref[...].astype(jnp.int32)


def _mix(attn_out, rwkv_out, x2d, mod, wo_a, wo_r, ln1, w_router, b_router, B, S, tm):
    D = x2d.shape[1]
    nt = S // tm
    T = B * S
    tok = lambda b, j: (b * nt + j, 0)
    col = lambda b, j: (0, b * nt + j)
    fixed = lambda b, j: (0, 0)
    return pl.pallas_call(
        _mix_kernel,
        grid=(B, nt),
        in_specs=[pl.BlockSpec((tm, ATTN_WIDTH), tok),
                  pl.BlockSpec((tm, RWKV_WIDTH), tok),
                  pl.BlockSpec((tm, D), tok),
                  pl.BlockSpec((1, 6, D), lambda b, j: (b, 0, 0)),
                  pl.BlockSpec((ATTN_WIDTH, D), fixed),
                  pl.BlockSpec((RWKV_WIDTH, D), fixed),
                  pl.BlockSpec((2, D), fixed),
                  pl.BlockSpec((D, 2 * LANES), fixed),
                  pl.BlockSpec((1, LANES), fixed)],
        out_specs=[pl.BlockSpec((tm, D), tok),
                   pl.BlockSpec((tm, D), tok),
                   pl.BlockSpec((TOP_K, tm), col),
                   pl.BlockSpec((TOP_K, tm), col),
                   pl.BlockSpec((TOP_K, tm), col),
                   pl.BlockSpec((N_EXPERTS, LANES), fixed),
                   pl.BlockSpec((1, N_EXPERTS, LANES), lambda b, j: (b * nt + j, 0, 0))],
        out_shape=[jax.ShapeDtypeStruct((T, D), F32),
                   jax.ShapeDtypeStruct((T, D), BF16),
                   jax.ShapeDtypeStruct((TOP_K, T), jnp.int32),
                   jax.ShapeDtypeStruct((TOP_K, T), F32),
                   jax.ShapeDtypeStruct((TOP_K, T), jnp.int32),
                   jax.ShapeDtypeStruct((N_EXPERTS, LANES), jnp.int32),
                   jax.ShapeDtypeStruct((T // tm, N_EXPERTS, LANES), jnp.int32)],
        scratch_shapes=[pltpu.VMEM((N_EXPERTS, LANES), F32)],
        compiler_params=pltpu.CompilerParams(dimension_semantics=("arbitrary", "arbitrary"),
                                             vmem_limit_bytes=VMEM_LIMIT),
        name="mix",
    )(attn_out, rwkv_out, x2d, mod, wo_a, wo_r, ln1, w_router, b_router)


RUN_PIECES = tuple(2 ** b for b in range(int(math.log2(MOE_TILE)), -1, -1))
SUBLANES = 8


def _to_tiles(ref, x):
    n = x.shape[0]
    for c in range(SUBLANES):
        ref[pl.ds(c, n, stride=SUBLANES), :] = x[:, c * LANES:(c + 1) * LANES]


def _from_tiles(ref):
    n = ref.shape[0] // SUBLANES
    return jnp.concatenate([ref[pl.ds(c, n, stride=SUBLANES), :] for c in range(SUBLANES)], axis=1)


def _run_copies(n, local, local_start, remote, remote_start, sem, to_remote):
    off = 0
    for piece in RUN_PIECES:
        take = (n & piece) != 0

        @pl.when(take)
        def _(off=off, piece=piece):
            lo = pl.multiple_of((local_start + off) * SUBLANES, SUBLANES)
            ro = pl.multiple_of((remote_start + off) * SUBLANES, SUBLANES)
            loc = local.at[pl.ds(lo, piece * SUBLANES)]
            rem = remote.at[pl.ds(ro, piece * SUBLANES)]
            src, dst = (loc, rem) if to_remote else (rem, loc)
            pltpu.make_async_copy(src, dst, sem).start()

        off = off + (n & piece)


def _dispatch_kernel(tcnt_ref, lstart_ref, gstart_ref, pad_ref, pad_start_ref, n_used_ref, lpos_ref, h2_ref,
                     xs_ref, xbuf, zbuf, sems):
    i = pl.program_id(0)
    tm = h2_ref.shape[0]
    n_loc = TOP_K * tm
    n_blocks = xs_ref.shape[0] // (MOE_BLOCK * SUBLANES)
    zero_sem = sems.at[2]

    @pl.when(i == 0)
    def _():
        zbuf[...] = jnp.zeros_like(zbuf)

        def zero_pad(e, carry):
            _run_copies(pad_ref[e], zbuf, 0, xs_ref, pad_start_ref[e], zero_sem, True)
            return carry

        def zero_tail(b, carry):
            @pl.when(b >= n_used_ref[0])
            def _():
                start = pl.multiple_of(b * (MOE_BLOCK * SUBLANES), MOE_BLOCK * SUBLANES)
                pltpu.make_async_copy(zbuf, xs_ref.at[pl.ds(start, MOE_BLOCK * SUBLANES)], zero_sem).start()
            return carry

        lax.fori_loop(0, N_EXPERTS, zero_pad, 0)
        lax.fori_loop(n_blocks - N_EXPERTS, n_blocks, zero_tail, 0)
        n_zero = N_EXPERTS * MOE_BLOCK * SUBLANES
        pltpu.make_async_copy(xs_ref.at[pl.ds(0, n_zero)], xs_ref.at[pl.ds(0, n_zero)], zero_sem).wait()

    slot = lax.broadcasted_iota(jnp.int32, (n_loc, tm), 0)
    lpos = lpos_ref[...]
    perm = jnp.zeros((n_loc, tm), F32)
    for k in range(TOP_K):
        perm = perm + (slot == lpos[k:k + 1, :]).astype(F32)
    rows = _dot(perm.astype(BF16), h2_ref[...].astype(BF16))

    def wait_tile(s):
        pltpu.make_async_copy(xbuf.at[s], xs_ref.at[pl.ds(0, n_loc * SUBLANES)], sems.at[s]).wait()

    for s in range(2):
        @pl.when((i % 2 == s) & (i >= 2))
        def _(s=s):
            wait_tile(s)

    for s in range(2):
        @pl.when(i % 2 == s)
        def _(s=s):
            _to_tiles(xbuf.at[s], rows)

            def issue(e, carry):
                idx = i * N_EXPERTS + e
                _run_copies(tcnt_ref[idx], xbuf.at[s], lstart_ref[idx], xs_ref, gstart_ref[idx], sems.at[s], True)
                return carry

            lax.fori_loop(0, N_EXPERTS, issue, 0)

    @pl.when(i == pl.num_programs(0) - 1)
    def _():
        for s in range(2):
            @pl.when((i % 2 == s) | (i >= 1))
            def _(s=s):
                wait_tile(s)


def _dispatch(tcnt, lstart, gstart, pad, pad_start, n_used, lpos, h2, n_rows, tm):
    T, D = h2.shape
    grid_spec = pltpu.PrefetchScalarGridSpec(
        num_scalar_prefetch=6,
        grid=(T // tm,),
        in_specs=[pl.BlockSpec((TOP_K, tm), lambda i, *_: (0, i)),
                  pl.BlockSpec((tm, D), lambda i, *_: (i, 0))],
        out_specs=pl.BlockSpec(memory_space=pl.ANY),
        scratch_shapes=[pltpu.VMEM((2, TOP_K * tm * SUBLANES, LANES), F32),
                        pltpu.VMEM((MOE_BLOCK * SUBLANES, LANES), F32),
                        pltpu.SemaphoreType.DMA((3,))],
    )
    return pl.pallas_call(
        _dispatch_kernel,
        grid_spec=grid_spec,
        out_shape=jax.ShapeDtypeStruct((n_rows * SUBLANES, LANES), F32),
        compiler_params=pltpu.CompilerParams(dimension_semantics=("arbitrary",),
                                             vmem_limit_bytes=VMEM_LIMIT),
        name="dispatch",
    )(tcnt, lstart, gstart, pad, pad_start, n_used, lpos, h2)


def _experts_kernel(blk_e_ref, n_used_ref, next_e_ref, xs_ref, wgu_hbm, bgu_ref, wd_hbm, bd_ref, ys_ref,
                    wgu_f32, wd_f32, wgu_bf, wd_bf, sems):
    i = pl.program_id(0)
    d_ff = wd_bf.shape[0]
    used = i < n_used_ref[0]
    e = blk_e_ref[i]
    new_expert = (i == 0) | (e != blk_e_ref[jnp.maximum(i - 1, 0)])

    def weight_copies(ex):
        return (pltpu.make_async_copy(wgu_hbm.at[ex], wgu_f32, sems.at[0]),
                pltpu.make_async_copy(wd_hbm.at[ex], wd_f32, sems.at[1]))

    @pl.when(i == 0)
    def _():
        for cp in weight_copies(e):
            cp.start()

    @pl.when(used & new_expert)
    def _():
        for cp in weight_copies(e):
            cp.wait()
        wgu_bf[...] = wgu_f32[...].astype(BF16)
        wd_bf[...] = wd_f32[...].astype(BF16)
        nxt = next_e_ref[i]

        @pl.when(nxt >= 0)
        def _():
            for cp in weight_copies(nxt):
                cp.start()

    @pl.when(used)
    def _():
        xb = _from_tiles(xs_ref).astype(BF16)
        gu = _dot(xb, wgu_bf[...]) + bgu_ref[0]
        gate = jnp.minimum(gu[:, :d_ff], SWIGLU_LIMIT)
        up = jnp.clip(gu[:, d_ff:], -SWIGLU_LIMIT, SWIGLU_LIMIT)
        act = (up + 1.0) * (gate * _sigmoid(SWIGLU_ALPHA * gate))
        _to_tiles(ys_ref, _dot(act.astype(BF16), wd_bf[...]) + bd_ref[0])

    @pl.when(i >= n_used_ref[0])
    def _():
        ys_ref[...] = jnp.zeros_like(ys_ref)


def _experts(blk_e, n_used, next_e, xs, wgu, bgu, wd, bd):
    d_ff, D = wd.shape[1], wd.shape[2]
    n_blocks = xs.shape[0] // (MOE_BLOCK * SUBLANES)
    blk = (MOE_BLOCK * SUBLANES, LANES)

    def last_used(i, n_used_ref):
        return jnp.minimum(i, jnp.maximum(n_used_ref[0] - 1, 0))

    def row_map(i, blk_e_ref, n_used_ref, next_e_ref):
        return (last_used(i, n_used_ref), 0)

    def exp_map(i, blk_e_ref, n_used_ref, next_e_ref):
        return (blk_e_ref[last_used(i, n_used_ref)], 0, 0)

    grid_spec = pltpu.PrefetchScalarGridSpec(
        num_scalar_prefetch=3,
        grid=(n_blocks,),
        in_specs=[pl.BlockSpec(blk, row_map),
                  pl.BlockSpec(memory_space=pl.ANY),
                  pl.BlockSpec((1, 1, 2 * d_ff), exp_map),
                  pl.BlockSpec(memory_space=pl.ANY),
                  pl.BlockSpec((1, 1, D), exp_map)],
        out_specs=pl.BlockSpec(blk, lambda i, *_: (i, 0)),
        scratch_shapes=[pltpu.VMEM((D, 2 * d_ff), F32), pltpu.VMEM((d_ff, D), F32),
                        pltpu.VMEM((D, 2 * d_ff), BF16), pltpu.VMEM((d_ff, D), BF16),
                        pltpu.SemaphoreType.DMA((2,))],
    )
    return pl.pallas_call(
        _experts_kernel,
        grid_spec=grid_spec,
        out_shape=jax.ShapeDtypeStruct(xs.shape, F32),
        compiler_params=pltpu.CompilerParams(dimension_semantics=("arbitrary",),
                                             vmem_limit_bytes=VMEM_LIMIT),
        name="experts",
    )(blk_e, n_used, next_e, xs, wgu, bgu, wd, bd)


def _combine_kernel(tcnt_ref, lstart_ref, gstart_ref, ys_ref, lpos_ref, gt_ref, x1_ref, mod_ref, ln_ref,
                    o_ref, buf, y_ref, sems):
    nt = pl.num_programs(1)
    n_tiles = pl.num_programs(0) * nt
    i = pl.program_id(0) * nt + pl.program_id(1)
    tm = x1_ref.shape[0]
    n_loc = TOP_K * tm

    def fetch(tile, s):
        def issue(e, carry):
            idx = tile * N_EXPERTS + e
            _run_copies(tcnt_ref[idx], buf.at[s], lstart_ref[idx], ys_ref, gstart_ref[idx], sems.at[s], False)
            return carry

        lax.fori_loop(0, N_EXPERTS, issue, 0)

    @pl.when(i == 0)
    def _():
        fetch(i, 0)

    for s in range(2):
        @pl.when(((i + 1) % 2 == s) & (i + 1 < n_tiles))
        def _(s=s):
            fetch(i + 1, s)

    slot = lax.broadcasted_iota(jnp.int32, (tm, n_loc), 1)
    lpos = lpos_ref[...]
    gt = gt_ref[...]
    pick = jnp.zeros((tm, n_loc), F32)
    for k in range(TOP_K):
        pick = pick + jnp.where(slot == lpos[:, k:k + 1], gt[:, k:k + 1], 0.0)
    pick = pick.astype(BF16)
    for s in range(2):
        @pl.when(i % 2 == s)
        def _(s=s):
            pltpu.make_async_copy(ys_ref.at[pl.ds(0, n_loc * SUBLANES)], buf.at[s], sems.at[s]).wait()
            y_ref[...] = _dot(pick, _from_tiles(buf.at[s]).astype(BF16))
    y = y_ref[...]
    mod = mod_ref[0]
    z = DEEPNORM_ALPHA * x1_ref[...] + (1.0 + mod[5:6, :]) * y
    o_ref[...] = _layer_norm(z) * ln_ref[0:1, :] + ln_ref[1:2, :]


def _combine(tcnt, lstart, gstart, ys, lpos_t, gates_t, x1, mod, ln2, B, S, tm):
    T, D = x1.shape
    nt = S // tm
    tok = lambda b, j, *_: (b * nt + j, 0)
    grid_spec = pltpu.PrefetchScalarGridSpec(
        num_scalar_prefetch=3,
        grid=(B, nt),
        in_specs=[pl.BlockSpec(memory_space=pl.ANY),
                  pl.BlockSpec((tm, TOP_K), tok),
                  pl.BlockSpec((tm, TOP_K), tok),
                  pl.BlockSpec((tm, D), tok),
                  pl.BlockSpec((1, 6, D), lambda b, j, *_: (b, 0, 0)),
                  pl.BlockSpec((2, D), lambda b, j, *_: (0, 0))],
        out_specs=pl.BlockSpec((tm, D), tok),
        scratch_shapes=[pltpu.VMEM((2, TOP_K * tm * SUBLANES, LANES), F32), pltpu.VMEM((tm, D), F32),
                        pltpu.SemaphoreType.DMA((2,))],
    )
    return pl.pallas_call(
        _combine_kernel,
        grid_spec=grid_spec,
        out_shape=jax.ShapeDtypeStruct((T, D), F32),
        compiler_params=pltpu.CompilerParams(dimension_semantics=("arbitrary", "arbitrary"),
                                             vmem_limit_bytes=VMEM_LIMIT),
        name="combine",
    )(tcnt, lstart, gstart, ys, lpos_t, gates_t, x1, mod, ln2)


def _pad_rows(w, rows):
    return jnp.pad(w, ((0, rows - w.shape[0]), (0, 0)))


def _pad_cols(w, cols):
    return jnp.pad(w, ((0, 0), (0, cols - w.shape[1])))


def _layer(x, c, positions, w_ada, b_ada, w_in, shift_mu, rwkv_w0, rwkv_w2, rwkv_a0, rwkv_a2, rwkv_g2,
           rwkv_k_k, rwkv_k_a, rwkv_r_k, rwkv_ln_w, rwkv_ln_b, attn_sinks, w_out, ln1_g, ln1_b,
           w_router, b_router, w_gate_up, b_gate_up, w_down, b_down, ln2_g, ln2_b):
    B, S, D = x.shape
    T = B * S
    tile = min(256, S)

    q0, k0, v0 = 0, ATTN_WIDTH, ATTN_WIDTH + KV_WIDTH
    r0 = ATTN_WIDTH + 2 * KV_WIDTH
    heads = lambda base: [w_in[:, base + h * HEAD_DIM: base + (h + 1) * HEAD_DIM] for h in range(N_KV_HEADS)]
    dup = lambda hs: [w for w in hs for _ in range(2)]
    w_attn = jnp.concatenate([w_in[:, q0:q0 + ATTN_WIDTH]] + dup(heads(k0)) + dup(heads(v0)), axis=1).astype(BF16)
    lora0 = r0 + 3 * RWKV_WIDTH
    lora = (DECAY_LORA, AAA_LORA, GATE_LORA)
    pieces_w = [w_in[:, r0:lora0]]
    pieces_mu = [shift_mu[None, 0:3 * RWKV_WIDTH]]
    off = lora0
    for n in lora:
        pieces_w.append(_pad_cols(w_in[:, off:off + n], LANES))
        pieces_mu.append(_pad_cols(shift_mu[None, off - r0:off - r0 + n], LANES))
        off += n
    w_rwkv = jnp.concatenate(pieces_w, axis=1).astype(BF16)
    mu = jnp.concatenate(pieces_mu, axis=1)
    inv_freq = ROPE_THETA ** (-jnp.arange(0, ROT_DIM, 2, dtype=F32) / ROT_DIM)
    lane_p = jnp.arange(LANES) % HEAD_DIM
    rot_tab = jnp.zeros((8, LANES), F32)
    rot_tab = rot_tab.at[0].set(jnp.where(lane_p < ROT_DIM, inv_freq[lane_p % (ROT_DIM // 2)], 0.0))
    rot_tab = rot_tab.at[1].set(jnp.where(lane_p < ROT_DIM // 2, -1.0, 0.0))
    rot_tab = rot_tab.at[2].set(jnp.where((lane_p >= ROT_DIM // 2) & (lane_p < ROT_DIM), 1.0, 0.0))
    vecs = jnp.stack([rwkv_w0, rwkv_a0, rwkv_k_k, rwkv_k_a, rwkv_r_k.reshape(-1), rwkv_ln_w, rwkv_ln_b,
                      jnp.zeros_like(rwkv_w0)])
    w2 = _pad_rows(rwkv_w2, LANES).astype(BF16)
    a2 = _pad_rows(rwkv_a2, LANES).astype(BF16)
    g2 = _pad_rows(rwkv_g2, LANES).astype(BF16)
    wo_a = w_out[:ATTN_WIDTH].astype(BF16)
    wo_r = w_out[ATTN_WIDTH:].astype(BF16)
    w_r_hi = w_router.astype(BF16)
    w_r_lo = (w_router - w_r_hi.astype(F32)).astype(BF16)
    w_r = jnp.concatenate([_pad_cols(w_r_hi, LANES), _pad_cols(w_r_lo, LANES)], axis=1)
    b_r = jnp.concatenate([b_router, jnp.full((LANES - N_EXPERTS,), NEG_INF, F32)])[None, :]

    mod = _mod(c, w_ada, b_ada).reshape(B, 6, D)
    qkv, rw = _inproj(x, positions, mod, w_attn, w_rwkv, mu, rot_tab, tile)
    attn_out = _attention(qkv, attn_sinks, B, S)
    rwkv_out = _rwkv(rw, vecs, w2, a2, g2, B, S, min(256, S))

    mtile = min(MOE_TILE, S)
    x1, h2, top_i, gates, rank, cnt, tbase = _mix(attn_out, rwkv_out, x.reshape(T, D), mod, wo_a, wo_r,
                                                   jnp.stack([ln1_g, ln1_b]), w_r, b_r, B, S, mtile)

    counts = cnt[:, 0]
    padded = (counts + MOE_BLOCK - 1) // MOE_BLOCK * MOE_BLOCK
    pend = jnp.cumsum(padded)
    pstart = pend - padded
    n_blocks = T * TOP_K // MOE_BLOCK + N_EXPERTS
    blk_row = jnp.arange(n_blocks, dtype=jnp.int32) * MOE_BLOCK
    blk_e = jnp.minimum(jnp.sum((pend[None, :] <= blk_row[:, None]).astype(jnp.int32), axis=1), N_EXPERTS - 1)
    n_used = (pend[-1:] // MOE_BLOCK).astype(jnp.int32)
    tb = tbase[:, :, 0]
    tcnt = jnp.concatenate([tb[1:], counts[None]], axis=0) - tb
    lstart = jnp.cumsum(tcnt, axis=1) - tcnt
    gstart = pstart[None, :] + tb
    shift = jnp.repeat(jnp.transpose(lstart - tb), mtile, axis=1)
    experts = jnp.arange(N_EXPERTS, dtype=jnp.int32)
    lpos = rank + jnp.sum(jnp.where(top_i[None] == experts[:, None, None], shift[:, None, :], 0), axis=0)
    flat = lambda a: a.reshape(-1).astype(jnp.int32)

    xs = _dispatch(flat(tcnt), flat(lstart), flat(gstart), flat(padded - counts), flat(pstart + counts), n_used,
                   lpos, h2, n_blocks * MOE_BLOCK, mtile)
    later_with_rows = (experts[None, :] > experts[:, None]) & (counts[None, :] > 0)
    next_of = jnp.min(jnp.where(later_with_rows, experts[None, :], N_EXPERTS), axis=1)
    next_of = jnp.where(next_of < N_EXPERTS, next_of, -1)
    next_e = jnp.sum(jnp.where(blk_e[:, None] == experts[None, :], next_of[None, :], 0), axis=1).astype(jnp.int32)
    ys = _experts(blk_e, n_used, next_e, xs, w_gate_up, b_gate_up[:, None, :], w_down, b_down[:, None, :])
    out = _combine(flat(tcnt), flat(lstart), flat(gstart), ys, jnp.transpose(lpos), jnp.transpose(gates), x1, mod,
                   jnp.stack([ln2_g, ln2_b]), B, S, mtile)
    return out.reshape(B, S, D)


def kernel(x, c, positions, w_ada, b_ada, w_in, shift_mu, rwkv_w0, rwkv_w2, rwkv_a0, rwkv_a2, rwkv_g2,
           rwkv_k_k, rwkv_k_a, rwkv_r_k, rwkv_ln_w, rwkv_ln_b, attn_sinks, w_out, ln1_g, ln1_b,
           w_router, b_router, w_gate_up, b_gate_up, w_down, b_down, ln2_g, ln2_b):
    for l in range(DEPTH):
        x = _layer(x, c, positions, w_ada[l], b_ada[l], w_in[l], shift_mu[l], rwkv_w0[l], rwkv_w2[l],
                   rwkv_a0[l], rwkv_a2[l], rwkv_g2[l], rwkv_k_k[l], rwkv_k_a[l], rwkv_r_k[l], rwkv_ln_w[l],
                   rwkv_ln_b[l], attn_sinks[l], w_out[l], ln1_g[l], ln1_b[l], w_router[l], b_router[l],
                   w_gate_up[l], b_gate_up[l], w_down[l], b_down[l], ln2_g[l], ln2_b[l])
    return x
```

```python
import functools
import math

import jax
import jax.numpy as jnp
from jax import lax
from jax.experimental import pallas as pl
from jax.experimental.pallas import tpu as pltpu

F32 = jnp.float32
BF16 = jnp.bfloat16

HEAD_DIM = 64
N_ATTN_HEADS = 8
N_KV_HEADS = 2
N_RWKV_HEADS = 8
ATTN_WIDTH = N_ATTN_HEADS * HEAD_DIM
KV_WIDTH = N_KV_HEADS * HEAD_DIM
RWKV_WIDTH = N_RWKV_HEADS * HEAD_DIM
ATTN_BLOCK = 128
ROT_DIM = HEAD_DIM // 4
ROPE_THETA = 500000.0
DECAY_LORA = 32
AAA_LORA = 32
GATE_LORA = 96
N_EXPERTS = 32
TOP_K = 4
SWIGLU_LIMIT = 7.0
SWIGLU_ALPHA = 1.702
LN_EPS = 1e-5
RWKV_GN_EPS = 64e-5
NEG_INF = -1e30
DEPTH = 1
DEEPNORM_ALPHA = (2 * DEPTH) ** 0.25

LANES = 128
RWKV_CHUNK = 64
RWKV_STEP = 256
INPROJ_TILE = 512
ATTN_STEP_BLOCKS = 2
MOE_BLOCK = 512
MOE_TILE = 256
ATTN_PROJ = ATTN_WIDTH + 4 * KV_WIDTH
RWKV_PROJ = 3 * RWKV_WIDTH + 3 * LANES
VMEM_LIMIT = 48 * 1024 * 1024


def _dot(a, b):
    return jnp.dot(a, b, preferred_element_type=F32)


def _dot_nt(a, b):
    return lax.dot_general(a, b, (((1,), (1,)), ((), ())), preferred_element_type=F32)


def _dot_tn(a, b):
    return lax.dot_general(a, b, (((0,), (0,)), ((), ())), preferred_element_type=F32)


def _split3(x):
    h = x.astype(BF16)
    r1 = x - h.astype(F32)
    m = r1.astype(BF16)
    lo = (r1 - m.astype(F32)).astype(BF16)
    return h, m, lo


def _dot_exact_lhs(m_bf16, x):
    h, m, lo = _split3(x)
    return _dot(m_bf16, h) + _dot(m_bf16, m) + _dot(m_bf16, lo)


def _layer_norm(x):
    mu = jnp.mean(x, axis=-1, keepdims=True)
    xc = x - mu
    var = jnp.mean(xc * xc, axis=-1, keepdims=True)
    return xc * lax.rsqrt(var + LN_EPS)


def _sigmoid(x):
    return 1.0 / (1.0 + jnp.exp(-x))


def _mod_kernel(c_ref, w_ref, b_ref, o_ref):
    c = c_ref[...]
    s = c * _sigmoid(c)
    o_ref[...] = jnp.dot(s, w_ref[...], preferred_element_type=F32,
                         precision=lax.Precision.HIGHEST) + b_ref[...]


def _mod(c, w_ada, b_ada):
    B, D = c.shape
    n = w_ada.shape[1] // D
    return pl.pallas_call(
        _mod_kernel,
        grid=(n,),
        in_specs=[pl.BlockSpec((B, D), lambda i: (0, 0)),
                  pl.BlockSpec((D, D), lambda i: (0, i)),
                  pl.BlockSpec((1, D), lambda i: (0, i))],
        out_specs=pl.BlockSpec((B, D), lambda i: (0, i)),
        out_shape=jax.ShapeDtypeStruct((B, n * D), F32),
        compiler_params=pltpu.CompilerParams(dimension_semantics=("arbitrary",),
                                             vmem_limit_bytes=VMEM_LIMIT),
        name="mod",
    )(c, w_ada, b_ada.reshape(1, -1))


def _inproj_kernel(x_ref, pos_ref, mod_ref, wa_ref, wr_ref, mu_ref, rt_ref,
                   qkv_ref, rw_ref, carry_ref):
    j = pl.program_id(1)
    x = x_ref[0]
    tm = x.shape[0]
    mod = mod_ref[0]
    h = _layer_norm(x) * (1.0 + mod[1:2, :]) + mod[0:1, :]
    hb = h.astype(BF16)

    pa = _dot(hb, wa_ref[...])
    ang = pos_ref[0].astype(F32) * rt_ref[0:1, :]
    cs = jnp.cos(ang)
    sn = jnp.sin(ang)
    m_lo = rt_ref[1:2, :]
    m_hi = rt_ref[2:3, :]
    n_q = ATTN_WIDTH // LANES
    n_rot = (ATTN_WIDTH + 2 * KV_WIDTH) // LANES
    for ch in range(n_rot):
        t = pa[:, ch * LANES:(ch + 1) * LANES]
        if ch < n_q:
            t = t * (1.0 / math.sqrt(HEAD_DIM))
        up = pltpu.roll(t, LANES - ROT_DIM // 2, 1)
        dn = pltpu.roll(t, ROT_DIM // 2, 1)
        o = t * cs + sn * (m_lo * up + m_hi * dn)
        qkv_ref[:, ch * LANES:(ch + 1) * LANES] = o.astype(BF16)
    qkv_ref[:, n_rot * LANES:] = pa[:, n_rot * LANES:].astype(BF16)

    pr = _dot(hb, wr_ref[...])
    prev = pltpu.roll(pr, 1, 0)
    row = lax.broadcasted_iota(jnp.int32, (tm, 1), 0)
    carry = jnp.where(j == 0, 0.0, carry_ref[...])
    prev = jnp.where(row == 0, carry, prev)
    carry_ref[...] = pr[tm - 1:tm, :]
    rw_ref[...] = pr + (prev - pr) * mu_ref[...]


def _inproj(x, positions, mod, w_attn, w_rwkv, mu, rot_tab, tm):
    B, S, D = x.shape
    nt = S // tm
    return pl.pallas_call(
        _inproj_kernel,
        grid=(B, nt),
        in_specs=[pl.BlockSpec((1, tm, D), lambda b, j: (b, j, 0)),
                  pl.BlockSpec((1, tm, 1), lambda b, j: (b, j, 0)),
                  pl.BlockSpec((1, 6, D), lambda b, j: (b, 0, 0)),
                  pl.BlockSpec((D, ATTN_PROJ), lambda b, j: (0, 0)),
                  pl.BlockSpec((D, RWKV_PROJ), lambda b, j: (0, 0)),
                  pl.BlockSpec((1, RWKV_PROJ), lambda b, j: (0, 0)),
                  pl.BlockSpec((8, LANES), lambda b, j: (0, 0))],
        out_specs=[pl.BlockSpec((tm, ATTN_PROJ), lambda b, j: (b * nt + j, 0)),
                   pl.BlockSpec((tm, RWKV_PROJ), lambda b, j: (b * nt + j, 0))],
        out_shape=[jax.ShapeDtypeStruct((B * S, ATTN_PROJ), BF16),
                   jax.ShapeDtypeStruct((B * S, RWKV_PROJ), F32)],
        scratch_shapes=[pltpu.VMEM((1, RWKV_PROJ), F32)],
        compiler_params=pltpu.CompilerParams(dimension_semantics=("arbitrary", "arbitrary"),
                                             vmem_limit_bytes=VMEM_LIMIT),
        name="inproj",
    )(x, positions.reshape(B, S, 1), mod, w_attn, w_rwkv, mu, rot_tab)


def _attn_kernel(q_ref, kc_ref, kp_ref, vc_ref, vp_ref, sink_ref, o_ref):
    i = pl.program_id(1)
    blk = ATTN_BLOCK
    n_sub = q_ref.shape[0] // blk
    qi = lax.broadcasted_iota(jnp.int32, (blk, 2 * blk), 0)
    kj = lax.broadcasted_iota(jnp.int32, (blk, 2 * blk), 1)
    band = (kj > qi) & (kj <= qi + blk)
    first = band & ((kj >= blk) | (i > 0))
    lane = lax.broadcasted_iota(jnp.int32, (1, LANES), 1)
    lo = (lane < HEAD_DIM).astype(BF16)
    hi = (lane >= HEAD_DIM).astype(BF16)
    halves = {}
    for u in range(n_sub):
        for g in range(N_KV_HEADS):
            sl = slice(g * LANES, (g + 1) * LANES)
            prev_k = kp_ref[:, sl] if u == 0 else kc_ref[(u - 1) * blk:u * blk, sl]
            prev_v = vp_ref[:, sl] if u == 0 else vc_ref[(u - 1) * blk:u * blk, sl]
            kcat = jnp.concatenate([prev_k, kc_ref[u * blk:(u + 1) * blk, sl]], axis=0)
            vcat = jnp.concatenate([prev_v, vc_ref[u * blk:(u + 1) * blk, sl]], axis=0)
            halves[u, g] = ((kcat * lo, vcat * lo), (kcat * hi, vcat * hi))
    units = [(u, c, half) for u in range(n_sub) for c in range(ATTN_WIDTH // LANES) for half in range(2)]
    scores = [_dot_nt(q_ref[u * blk:(u + 1) * blk, c * LANES:(c + 1) * LANES], halves[u, c // 2][half][0])
              for u, c, half in units]
    probs, denoms = [], []
    for (u, c, half), s in zip(units, scores):
        sink = sink_ref[2 * c + half]
        s = jnp.where(first if u == 0 else band, s, NEG_INF)
        m = jnp.maximum(jnp.max(s, axis=-1, keepdims=True), sink)
        p = jnp.exp(s - m)
        denoms.append(jnp.sum(p, axis=-1, keepdims=True) + jnp.exp(sink - m))
        probs.append(p.astype(BF16))
    outs = [_dot(p, halves[u, c // 2][half][1]) / d for (u, c, half), p, d in zip(units, probs, denoms)]
    for n, (u, c, half) in enumerate(units):
        if half == 0:
            o_ref[u * blk:(u + 1) * blk, c * LANES:(c + 1) * LANES] = (outs[n] + outs[n + 1]).astype(BF16)


def _attention(qkv, sinks, B, S, n_sub):
    blk = ATTN_BLOCK
    step = n_sub * blk
    ns = S // step
    kcol = ATTN_WIDTH // (2 * KV_WIDTH)
    vcol = kcol + 1
    cur = lambda col: (lambda b, i: (b * ns + i, col))
    prv = lambda col: (lambda b, i: (jnp.maximum((b * ns + i) * n_sub - 1, 0), col))
    return pl.pallas_call(
        _attn_kernel,
        grid=(B, ns),
        in_specs=[pl.BlockSpec((step, ATTN_WIDTH), lambda b, i: (b * ns + i, 0)),
                  pl.BlockSpec((step, 2 * KV_WIDTH), cur(kcol)),
                  pl.BlockSpec((blk, 2 * KV_WIDTH), prv(kcol)),
                  pl.BlockSpec((step, 2 * KV_WIDTH), cur(vcol)),
                  pl.BlockSpec((blk, 2 * KV_WIDTH), prv(vcol)),
                  pl.BlockSpec(memory_space=pltpu.SMEM)],
        out_specs=pl.BlockSpec((step, ATTN_WIDTH), lambda b, i: (b * ns + i, 0)),
        out_shape=jax.ShapeDtypeStruct((B * S, ATTN_WIDTH), BF16),
        compiler_params=pltpu.CompilerParams(dimension_semantics=("arbitrary", "arbitrary"),
                                             vmem_limit_bytes=VMEM_LIMIT),
        name="attn",
    )(qkv, qkv, qkv, qkv, qkv, sinks)


def _rwkv_kernel(rw_ref, vec_ref, w2_ref, a2_ref, g2_ref, o_ref, state_ref, *, n_chunk):
    j = pl.program_id(1)
    C = RWKV_CHUNK
    W = RWKV_WIDTH
    n_pair = W // LANES

    @pl.when(j == 0)
    def _():
        state_ref[...] = jnp.zeros_like(state_ref)

    w0 = vec_ref[0:1, :]
    a0 = vec_ref[1:2, :]
    k_k = vec_ref[2:3, :]
    k_a = vec_ref[3:4, :]
    r_k = vec_ref[4:5, :]
    ln_w = vec_ref[5:6, :]
    ln_b = vec_ref[6:7, :]

    r = rw_ref[:, 0:W]
    k = rw_ref[:, W:2 * W]
    v = rw_ref[:, 2 * W:3 * W]
    wl = rw_ref[:, 3 * W:3 * W + LANES]
    al = rw_ref[:, 3 * W + LANES:3 * W + 2 * LANES]
    gl = rw_ref[:, 3 * W + 2 * LANES:3 * W + 3 * LANES]

    ri = lax.broadcasted_iota(jnp.int32, (LANES, LANES), 0)
    ci = lax.broadcasted_iota(jnp.int32, (LANES, LANES), 1)
    same = (ri // HEAD_DIM) == (ci // HEAD_DIM)
    strict = same & ((ri % HEAD_DIM) > (ci % HEAD_DIM))
    incl = same & ((ri % HEAD_DIM) >= (ci % HEAD_DIM))
    lane = lax.broadcasted_iota(jnp.int32, (1, LANES), 1)
    m0 = (lane < HEAD_DIM).astype(F32)
    m1 = 1.0 - m0
    tri = (lax.broadcasted_iota(jnp.int32, (C, C), 0) >= lax.broadcasted_iota(jnp.int32, (C, C), 1)).astype(BF16)

    def head_sum(xv):
        outs = []
        for p in range(n_pair):
            xp = xv[:, p * LANES:(p + 1) * LANES]
            s0 = jnp.sum(xp * m0, axis=1, keepdims=True)
            s1 = jnp.sum(xp * m1, axis=1, keepdims=True)
            outs.append(s0 * m0 + s1 * m1)
        return jnp.concatenate(outs, axis=1)

    def stack2(xp):
        return jnp.concatenate([xp * m0, xp * m1], axis=0)

    z = w0 + _dot(jnp.tanh(wl).astype(BF16), w2_ref[...])
    lw = -math.exp(-0.5) * _sigmoid(z)
    a = _sigmoid(a0 + _dot(al.astype(BF16), a2_ref[...]))
    g = _dot(_sigmoid(gl).astype(BF16), g2_ref[...])
    kk = k * k_k
    kkn = kk / jnp.maximum(jnp.sqrt(head_sum(kk * kk)), 1e-12)
    k2 = k * (1.0 + (a - 1.0) * k_a)
    av = -kkn
    bv = kkn * a
    bonus = head_sum(r * k2 * r_k) * v

    eye = (ri == ci).astype(F32)
    bf = lambda t: t.astype(BF16)

    pre = []
    for c in range(n_chunk):
        rows = slice(c * C, (c + 1) * C)
        lwc = lw[rows]
        cw = _dot_exact_lhs(tri, lwc)
        cwl = cw[C - 1:C, :]
        e_in = jnp.exp(cw)
        e_neg = jnp.exp(-cw)
        e_rem = jnp.exp(cwl - cw)
        wc = jnp.exp(cwl)
        Rt = r[rows] * e_in
        At = av[rows] * jnp.exp(cw - lwc)
        Bb = bv[rows] * e_neg
        Kb = k2[rows] * e_neg
        Bh = bv[rows] * e_rem
        Kh = k2[rows] * e_rem
        vc = v[rows]
        for p in range(n_pair):
            sl = slice(p * LANES, (p + 1) * LANES)
            pre.append(dict(At=At[:, sl], Rt=Rt[:, sl], Bb=Bb[:, sl], Kb=Kb[:, sl], Bh=Bh[:, sl], Kh=Kh[:, sl],
                            v=vc[:, sl], wc=wc[:, sl]))

    for u in pre:
        u["at_bd"] = stack2(u["At"])
        lhs = bf(jnp.concatenate([u["at_bd"], stack2(u["Rt"])], axis=0))
        rhs = bf(jnp.concatenate([stack2(u["Bb"]), stack2(u["Kb"])], axis=0))
        u["G"] = _dot_nt(lhs, rhs)
    for u in pre:
        G = u.pop("G")
        u["a_ab"] = jnp.where(strict, G[0:2 * C, 0:2 * C], 0.0)
        u["a_ak"] = bf(jnp.where(strict, G[0:2 * C, 2 * C:4 * C], 0.0))
        u["a_rb"] = bf(jnp.where(incl, G[2 * C:4 * C, 0:2 * C], 0.0))
        u["a_rk"] = bf(jnp.where(incl, G[2 * C:4 * C, 2 * C:4 * C], 0.0))
        u["v_bd"] = bf(stack2(u["v"]))
    for u in pre:
        xb = bf(u["a_ab"])
        u["P"] = eye + u.pop("a_ab")
        u["X"] = _dot(xb, xb)
        u["M0"] = _dot(u["a_ak"], u["v_bd"])
    for _ in range(int(math.log2(C)) - 2):
        for u in pre:
            Wm = _dot(bf(u["X"]), bf(jnp.concatenate([u["P"], u["X"]], axis=1)))
            u["P"] = u["P"] + Wm[:, 0:LANES]
            u["X"] = Wm[:, LANES:2 * LANES]
    for u in pre:
        u["P"] = bf(u["P"] + _dot(bf(u.pop("X")), bf(u["P"])))
    for u in pre:
        u["M1"] = _dot(u["P"], bf(u.pop("M0")))
        u["Q"] = bf(_dot(u["a_rb"], u["P"]))
        u["PtB"] = _dot_tn(u["P"], bf(stack2(u["Bh"])))
    for u in pre:
        M1 = u.pop("M1")
        u["Y0"] = _dot(jnp.concatenate([u["a_rb"], u["a_rk"]], axis=1),
                       jnp.concatenate([bf(M1), u["v_bd"]], axis=0))
        u["Tm"] = bf(_dot_tn(bf(u["at_bd"]), bf(u.pop("PtB"))))
        m1_pair = M1[0:C] + M1[C:2 * C]
        cst = _dot_tn(bf(jnp.concatenate([m1_pair, u["v"]], axis=0)),
                      bf(jnp.concatenate([u["Bh"], u["Kh"]], axis=0)))
        u["cst"] = jnp.where(same, cst, 0.0)
        u["ar"] = bf(jnp.concatenate([u["At"], u["Rt"]], axis=0))

    states = [state_ref[p] for p in range(n_pair)]
    for c in range(n_chunk):
        rows = slice(c * C, (c + 1) * C)
        us = pre[c * n_pair:(c + 1) * n_pair]
        sbs = [bf(S) for S in states]
        zs = [_dot_nt(u["ar"], sb) for u, sb in zip(us, sbs)]
        new_states = [S * u["wc"] + _dot(sb, u["Tm"]) + u["cst"] for u, S, sb in zip(us, states, sbs)]
        ybds = [stack2(Z[C:2 * C]) + _dot(u["Q"], bf(stack2(Z[0:C]))) + u["Y0"] for u, Z in zip(us, zs)]
        ys = [y_bd[0:C] + y_bd[C:2 * C] for y_bd in ybds]
        states = new_states
        y = jnp.concatenate(ys, axis=1)
        mu = head_sum(y) * (1.0 / HEAD_DIM)
        yc = y - mu
        var = head_sum(yc * yc) * (1.0 / HEAD_DIM)
        yn = yc * lax.rsqrt(var + RWKV_GN_EPS) * ln_w + ln_b
        o_ref[rows, :] = ((yn + bonus[rows]) * g[rows]).astype(BF16)
    for p in range(n_pair):
        state_ref[p] = states[p]


def _rwkv(rw, vecs, w2, a2, g2, B, S, lb):
    nt = S // lb
    return pl.pallas_call(
        functools.partial(_rwkv_kernel, n_chunk=lb // RWKV_CHUNK),
        grid=(B, nt),
        in_specs=[pl.BlockSpec((lb, RWKV_PROJ), lambda b, j: (b * nt + j, 0)),
                  pl.BlockSpec((8, RWKV_WIDTH), lambda b, j: (0, 0)),
                  pl.BlockSpec((LANES, RWKV_WIDTH), lambda b, j: (0, 0)),
                  pl.BlockSpec((LANES, RWKV_WIDTH), lambda b, j: (0, 0)),
                  pl.BlockSpec((LANES, RWKV_WIDTH), lambda b, j: (0, 0))],
        out_specs=pl.BlockSpec((lb, RWKV_WIDTH), lambda b, j: (b * nt + j, 0)),
        out_shape=jax.ShapeDtypeStruct((B * S, RWKV_WIDTH), BF16),
        scratch_shapes=[pltpu.VMEM((RWKV_WIDTH // LANES, LANES, LANES), F32)],
        compiler_params=pltpu.CompilerParams(dimension_semantics=("arbitrary", "arbitrary"),
                                             vmem_limit_bytes=VMEM_LIMIT),
        name="rwkv",
    )(rw, vecs, w2, a2, g2)


def _mix_kernel(at_ref, rk_ref, x_ref, mod_ref, wo_a_ref, wo_r_ref, ln_ref, wr_ref, br_ref,
                x1_ref, h2_ref, ti_ref, gt_ref, rank_ref, cnt_ref, tb_ref, base_ref):
    first = (pl.program_id(0) == 0) & (pl.program_id(1) == 0)

    @pl.when(first)
    def _():
        base_ref[...] = jnp.zeros_like(base_ref)

    mod = mod_ref[0]
    y = _dot(at_ref[...], wo_a_ref[...]) + _dot(rk_ref[...], wo_r_ref[...])
    x = x_ref[...]
    tm = x.shape[0]
    x1 = _layer_norm(DEEPNORM_ALPHA * x + (1.0 + mod[2:3, :]) * y) * ln_ref[0:1, :] + ln_ref[1:2, :]
    h2 = _layer_norm(x1) * (1.0 + mod[4:5, :]) + mod[3:4, :]
    x1_ref[...] = x1
    h_hi = h2.astype(BF16)
    h2_ref[...] = h_hi

    h_lo = (h2 - h_hi.astype(F32)).astype(BF16)
    part = _dot(h_hi, wr_ref[...])
    logits = part[:, 0:LANES] + part[:, LANES:2 * LANES] + _dot(h_lo, wr_ref[:, 0:LANES]) + br_ref[...]
    lt = jnp.transpose(logits)[0:N_EXPERTS, :]
    erow = lax.broadcasted_iota(jnp.int32, (N_EXPERTS, tm), 0).astype(F32)
    cur = lt
    vals, idxs = [], []
    for _ in range(TOP_K):
        m = jnp.max(cur, axis=0, keepdims=True)
        idx = jnp.min(jnp.where(cur == m, erow, float(N_EXPERTS)), axis=0, keepdims=True)
        vals.append(m)
        idxs.append(idx)
        cur = jnp.where(erow == idx, -jnp.inf, cur)
    tv = jnp.concatenate(vals, axis=0)
    e = jnp.exp(tv - tv[0:1, :])
    gt_ref[...] = e / jnp.sum(e, axis=0, keepdims=True)
    ti_ref[...] = jnp.concatenate(idxs, axis=0).astype(jnp.int32)

    onehot = jnp.zeros((N_EXPERTS, tm), F32)
    for idx in idxs:
        onehot = onehot + (erow == idx).astype(F32)
    before = (lax.broadcasted_iota(jnp.int32, (tm, tm), 0)
              < lax.broadcasted_iota(jnp.int32, (tm, tm), 1)).astype(BF16)
    tot = base_ref[:, 0:1] + _dot(onehot.astype(BF16), before)
    ranks = [jnp.sum(jnp.where(erow == idx, tot, 0.0), axis=0, keepdims=True) for idx in idxs]
    rank_ref[...] = jnp.concatenate(ranks, axis=0).astype(jnp.int32)
    tb_ref[0] = base_ref[...].astype(jnp.int32)
    base_ref[...] = base_ref[...] + jnp.sum(onehot, axis=1, keepdims=True)
    cnt_ref[...] = base_ref[...].astype(jnp.int32)


def _mix(attn_out, rwkv_out, x2d, mod, wo_a, wo_r, ln1, w_router, b_router, B, S, tm):
    D = x2d.shape[1]
    nt = S // tm
    T = B * S
    tok = lambda b, j: (b * nt + j, 0)
    col = lambda b, j: (0, b * nt + j)
    fixed = lambda b, j: (0, 0)
    return pl.pallas_call(
        _mix_kernel,
        grid=(B, nt),
        in_specs=[pl.BlockSpec((tm, ATTN_WIDTH), tok),
                  pl.BlockSpec((tm, RWKV_WIDTH), tok),
                  pl.BlockSpec((tm, D), tok),
                  pl.BlockSpec((1, 6, D), lambda b, j: (b, 0, 0)),
                  pl.BlockSpec((ATTN_WIDTH, D), fixed),
                  pl.BlockSpec((RWKV_WIDTH, D), fixed),
                  pl.BlockSpec((2, D), fixed),
                  pl.BlockSpec((D, 2 * LANES), fixed),
                  pl.BlockSpec((1, LANES), fixed)],
        out_specs=[pl.BlockSpec((tm, D), tok),
                   pl.BlockSpec((tm, D), tok),
                   pl.BlockSpec((TOP_K, tm), col),
                   pl.BlockSpec((TOP_K, tm), col),
                   pl.BlockSpec((TOP_K, tm), col),
                   pl.BlockSpec((N_EXPERTS, LANES), fixed),
                   pl.BlockSpec((1, N_EXPERTS, LANES), lambda b, j: (b * nt + j, 0, 0))],
        out_shape=[jax.ShapeDtypeStruct((T, D), F32),
                   jax.ShapeDtypeStruct((T, D), BF16),
                   jax.ShapeDtypeStruct((TOP_K, T), jnp.int32),
                   jax.ShapeDtypeStruct((TOP_K, T), F32),
                   jax.ShapeDtypeStruct((TOP_K, T), jnp.int32),
                   jax.ShapeDtypeStruct((N_EXPERTS, LANES), jnp.int32),
                   jax.ShapeDtypeStruct((T // tm, N_EXPERTS, LANES), jnp.int32)],
        scratch_shapes=[pltpu.VMEM((N_EXPERTS, LANES), F32)],
        compiler_params=pltpu.CompilerParams(dimension_semantics=("arbitrary", "arbitrary"),
                                             vmem_limit_bytes=VMEM_LIMIT),
        name="mix",
    )(attn_out, rwkv_out, x2d, mod, wo_a, wo_r, ln1, w_router, b_router)


RUN_PIECES = tuple(2 ** b for b in range(int(math.log2(MOE_TILE)), -1, -1))
SUBLANES = 8


def _to_tiles(ref, x):
    n = x.shape[0]
    for c in range(SUBLANES):
        ref[pl.ds(c, n, stride=SUBLANES), :] = x[:, c * LANES:(c + 1) * LANES]


def _from_tiles(ref):
    n = ref.shape[0] // SUBLANES
    return jnp.concatenate([ref[pl.ds(c, n, stride=SUBLANES), :] for c in range(SUBLANES)], axis=1)


def _run_copies(n, local, local_start, remote, remote_start, sem, to_remote):
    off = 0
    for piece in RUN_PIECES:
        take = (n & piece) != 0

        @pl.when(take)
        def _(off=off, piece=piece):
            lo = pl.multiple_of((local_start + off) * SUBLANES, SUBLANES)
            ro = pl.multiple_of((remote_start + off) * SUBLANES, SUBLANES)
            loc = local.at[pl.ds(lo, piece * SUBLANES)]
            rem = remote.at[pl.ds(ro, piece * SUBLANES)]
            src, dst = (loc, rem) if to_remote else (rem, loc)
            pltpu.make_async_copy(src, dst, sem).start()

        off = off + (n & piece)


def _dispatch_kernel(tcnt_ref, lstart_ref, gstart_ref, pad_ref, pad_start_ref, n_used_ref, lpos_ref, h2_ref,
                     xs_ref, xbuf, zbuf, sems):
    i = pl.program_id(0)
    tm = h2_ref.shape[0]
    n_loc = TOP_K * tm
    n_blocks = xs_ref.shape[0] // (MOE_BLOCK * SUBLANES)
    zero_sem = sems.at[2]

    @pl.when(i == 0)
    def _():
        zbuf[...] = jnp.zeros_like(zbuf)

        def zero_pad(e, carry):
            _run_copies(pad_ref[e], zbuf, 0, xs_ref, pad_start_ref[e], zero_sem, True)
            return carry

        def zero_tail(b, carry):
            @pl.when(b >= n_used_ref[0])
            def _():
                start = pl.multiple_of(b * (MOE_BLOCK * SUBLANES), MOE_BLOCK * SUBLANES)
                pltpu.make_async_copy(zbuf, xs_ref.at[pl.ds(start, MOE_BLOCK * SUBLANES)], zero_sem).start()
            return carry

        lax.fori_loop(0, N_EXPERTS, zero_pad, 0)
        lax.fori_loop(n_blocks - N_EXPERTS, n_blocks, zero_tail, 0)

    slot = lax.broadcasted_iota(jnp.int32, (n_loc, tm), 0)
    lpos = lpos_ref[...]
    perm = jnp.zeros((n_loc, tm), F32)
    for k in range(TOP_K):
        perm = perm + (slot == lpos[k:k + 1, :]).astype(F32)
    rows = _dot(perm.astype(BF16), h2_ref[...])

    def wait_tile(s):
        pltpu.make_async_copy(xbuf.at[s], xs_ref.at[pl.ds(0, n_loc * SUBLANES)], sems.at[s]).wait()

    for s in range(2):
        @pl.when((i % 2 == s) & (i >= 2))
        def _(s=s):
            wait_tile(s)

    for s in range(2):
        @pl.when(i % 2 == s)
        def _(s=s):
            _to_tiles(xbuf.at[s], rows)

            def issue(e, carry):
                idx = i * N_EXPERTS + e
                _run_copies(tcnt_ref[idx], xbuf.at[s], lstart_ref[idx], xs_ref, gstart_ref[idx], sems.at[s], True)
                return carry

            lax.fori_loop(0, N_EXPERTS, issue, 0)

    @pl.when(i == pl.num_programs(0) - 1)
    def _():
        for s in range(2):
            @pl.when((i % 2 == s) | (i >= 1))
            def _(s=s):
                wait_tile(s)
        n_zero = N_EXPERTS * MOE_BLOCK * SUBLANES
        pltpu.make_async_copy(xs_ref.at[pl.ds(0, n_zero)], xs_ref.at[pl.ds(0, n_zero)], zero_sem).wait()


def _dispatch(tcnt, lstart, gstart, pad, pad_start, n_used, lpos, h2, n_rows, tm):
    T, D = h2.shape
    grid_spec = pltpu.PrefetchScalarGridSpec(
        num_scalar_prefetch=6,
        grid=(T // tm,),
        in_specs=[pl.BlockSpec((TOP_K, tm), lambda i, *_: (0, i)),
                  pl.BlockSpec((tm, D), lambda i, *_: (i, 0))],
        out_specs=pl.BlockSpec(memory_space=pl.ANY),
        scratch_shapes=[pltpu.VMEM((2, TOP_K * tm * SUBLANES, LANES), F32),
                        pltpu.VMEM((MOE_BLOCK * SUBLANES, LANES), F32),
                        pltpu.SemaphoreType.DMA((3,))],
    )
    return pl.pallas_call(
        _dispatch_kernel,
        grid_spec=grid_spec,
        out_shape=jax.ShapeDtypeStruct((n_rows * SUBLANES, LANES), F32),
        compiler_params=pltpu.CompilerParams(dimension_semantics=("arbitrary",),
                                             vmem_limit_bytes=VMEM_LIMIT),
        name="dispatch",
    )(tcnt, lstart, gstart, pad, pad_start, n_used, lpos, h2)


def _experts_kernel(blk_e_ref, n_used_ref, next_e_ref, xs_ref, wgu_hbm, bgu_ref, wd_hbm, bd_ref, ys_ref,
                    wgu_f32, wd_f32, wgu_bf, wd_bf, sems):
    i = pl.program_id(0)
    d_ff = wd_bf.shape[0]
    used = i < n_used_ref[0]
    e = blk_e_ref[i]
    new_expert = (i == 0) | (e != blk_e_ref[jnp.maximum(i - 1, 0)])

    def weight_copies(ex):
        return (pltpu.make_async_copy(wgu_hbm.at[ex], wgu_f32, sems.at[0]),
                pltpu.make_async_copy(wd_hbm.at[ex], wd_f32, sems.at[1]))

    @pl.when(i == 0)
    def _():
        for cp in weight_copies(e):
            cp.start()

    @pl.when(used & new_expert)
    def _():
        for cp in weight_copies(e):
            cp.wait()
        wgu_bf[...] = wgu_f32[...].astype(BF16)
        wd_bf[...] = wd_f32[...].astype(BF16)
        nxt = next_e_ref[i]

        @pl.when(nxt >= 0)
        def _():
            for cp in weight_copies(nxt):
                cp.start()

    @pl.when(used)
    def _():
        xb = _from_tiles(xs_ref).astype(BF16)
        gu = _dot(xb, wgu_bf[...]) + bgu_ref[0]
        gate = jnp.minimum(gu[:, :d_ff], SWIGLU_LIMIT)
        up = jnp.clip(gu[:, d_ff:], -SWIGLU_LIMIT, SWIGLU_LIMIT)
        act = (up + 1.0) * (gate * _sigmoid(SWIGLU_ALPHA * gate))
        _to_tiles(ys_ref, _dot(act.astype(BF16), wd_bf[...]) + bd_ref[0])

    @pl.when(i >= n_used_ref[0])
    def _():
        ys_ref[...] = jnp.zeros_like(ys_ref)


def _experts(blk_e, n_used, next_e, xs, wgu, bgu, wd, bd):
    d_ff, D = wd.shape[1], wd.shape[2]
    n_blocks = xs.shape[0] // (MOE_BLOCK * SUBLANES)
    blk = (MOE_BLOCK * SUBLANES, LANES)

    def last_used(i, n_used_ref):
        return jnp.minimum(i, jnp.maximum(n_used_ref[0] - 1, 0))

    def row_map(i, blk_e_ref, n_used_ref, next_e_ref):
        return (last_used(i, n_used_ref), 0)

    def exp_map(i, blk_e_ref, n_used_ref, next_e_ref):
        return (blk_e_ref[last_used(i, n_used_ref)], 0, 0)

    grid_spec = pltpu.PrefetchScalarGridSpec(
        num_scalar_prefetch=3,
        grid=(n_blocks,),
        in_specs=[pl.BlockSpec(blk, row_map),
                  pl.BlockSpec(memory_space=pl.ANY),
                  pl.BlockSpec((1, 1, 2 * d_ff), exp_map),
                  pl.BlockSpec(memory_space=pl.ANY),
                  pl.BlockSpec((1, 1, D), exp_map)],
        out_specs=pl.BlockSpec(blk, lambda i, *_: (i, 0)),
        scratch_shapes=[pltpu.VMEM((D, 2 * d_ff), F32), pltpu.VMEM((d_ff, D), F32),
                        pltpu.VMEM((D, 2 * d_ff), BF16), pltpu.VMEM((d_ff, D), BF16),
                        pltpu.SemaphoreType.DMA((2,))],
    )
    return pl.pallas_call(
        _experts_kernel,
        grid_spec=grid_spec,
        out_shape=jax.ShapeDtypeStruct(xs.shape, F32),
        compiler_params=pltpu.CompilerParams(dimension_semantics=("arbitrary",),
                                             vmem_limit_bytes=VMEM_LIMIT),
        name="experts",
    )(blk_e, n_used, next_e, xs, wgu, bgu, wd, bd)


def _combine_kernel(tcnt_ref, lstart_ref, gstart_ref, ys_ref, lpos_ref, gt_ref, x1_ref, mod_ref, ln_ref,
                    o_ref, buf, y_ref, sems):
    nt = pl.num_programs(1)
    n_tiles = pl.num_programs(0) * nt
    i = pl.program_id(0) * nt + pl.program_id(1)
    tm = x1_ref.shape[0]
    n_loc = TOP_K * tm

    def fetch(tile, s):
        def issue(e, carry):
            idx = tile * N_EXPERTS + e
            _run_copies(tcnt_ref[idx], buf.at[s], lstart_ref[idx], ys_ref, gstart_ref[idx], sems.at[s], False)
            return carry

        lax.fori_loop(0, N_EXPERTS, issue, 0)

    @pl.when(i == 0)
    def _():
        fetch(i, 0)

    for s in range(2):
        @pl.when(((i + 1) % 2 == s) & (i + 1 < n_tiles))
        def _(s=s):
            fetch(i + 1, s)

    slot = lax.broadcasted_iota(jnp.int32, (tm, n_loc), 1)
    lpos = lpos_ref[...]
    gt = gt_ref[...]
    pick = jnp.zeros((tm, n_loc), F32)
    for k in range(TOP_K):
        pick = pick + jnp.where(slot == lpos[:, k:k + 1], gt[:, k:k + 1], 0.0)
    pick = pick.astype(BF16)
    for s in range(2):
        @pl.when(i % 2 == s)
        def _(s=s):
            pltpu.make_async_copy(ys_ref.at[pl.ds(0, n_loc * SUBLANES)], buf.at[s], sems.at[s]).wait()
            y_ref[...] = _dot(pick, _from_tiles(buf.at[s]).astype(BF16))
    y = y_ref[...]
    mod = mod_ref[0]
    z = DEEPNORM_ALPHA * x1_ref[...] + (1.0 + mod[5:6, :]) * y
    o_ref[...] = _layer_norm(z) * ln_ref[0:1, :] + ln_ref[1:2, :]


def _combine(tcnt, lstart, gstart, ys, lpos_t, gates_t, x1, mod, ln2, B, S, tm):
    T, D = x1.shape
    nt = S // tm
    tok = lambda b, j, *_: (b * nt + j, 0)
    grid_spec = pltpu.PrefetchScalarGridSpec(
        num_scalar_prefetch=3,
        grid=(B, nt),
        in_specs=[pl.BlockSpec(memory_space=pl.ANY),
                  pl.BlockSpec((tm, TOP_K), tok),
                  pl.BlockSpec((tm, TOP_K), tok),
                  pl.BlockSpec((tm, D), tok),
                  pl.BlockSpec((1, 6, D), lambda b, j, *_: (b, 0, 0)),
                  pl.BlockSpec((2, D), lambda b, j, *_: (0, 0))],
        out_specs=pl.BlockSpec((tm, D), tok),
        scratch_shapes=[pltpu.VMEM((2, TOP_K * tm * SUBLANES, LANES), F32), pltpu.VMEM((tm, D), F32),
                        pltpu.SemaphoreType.DMA((2,))],
    )
    return pl.pallas_call(
        _combine_kernel,
        grid_spec=grid_spec,
        out_shape=jax.ShapeDtypeStruct((T, D), F32),
        compiler_params=pltpu.CompilerParams(dimension_semantics=("arbitrary", "arbitrary"),
                                             vmem_limit_bytes=VMEM_LIMIT),
        name="combine",
    )(tcnt, lstart, gstart, ys, lpos_t, gates_t, x1, mod, ln2)


def _pad_rows(w, rows):
    return jnp.pad(w, ((0, rows - w.shape[0]), (0, 0)))


def _pad_cols(w, cols):
    return jnp.pad(w, ((0, 0), (0, cols - w.shape[1])))


def _layer(x, c, positions, w_ada, b_ada, w_in, shift_mu, rwkv_w0, rwkv_w2, rwkv_a0, rwkv_a2, rwkv_g2,
           rwkv_k_k, rwkv_k_a, rwkv_r_k, rwkv_ln_w, rwkv_ln_b, attn_sinks, w_out, ln1_g, ln1_b,
           w_router, b_router, w_gate_up, b_gate_up, w_down, b_down, ln2_g, ln2_b):
    B, S, D = x.shape
    T = B * S

    q0, k0, v0 = 0, ATTN_WIDTH, ATTN_WIDTH + KV_WIDTH
    r0 = ATTN_WIDTH + 2 * KV_WIDTH
    heads = lambda base: [w_in[:, base + h * HEAD_DIM: base + (h + 1) * HEAD_DIM] for h in range(N_KV_HEADS)]
    dup = lambda hs: [w for w in hs for _ in range(2)]
    w_attn = jnp.concatenate([w_in[:, q0:q0 + ATTN_WIDTH]] + dup(heads(k0)) + dup(heads(v0)), axis=1).astype(BF16)
    lora0 = r0 + 3 * RWKV_WIDTH
    lora = (DECAY_LORA, AAA_LORA, GATE_LORA)
    pieces_w = [w_in[:, r0:lora0]]
    pieces_mu = [shift_mu[None, 0:3 * RWKV_WIDTH]]
    off = lora0
    for n in lora:
        pieces_w.append(_pad_cols(w_in[:, off:off + n], LANES))
        pieces_mu.append(_pad_cols(shift_mu[None, off - r0:off - r0 + n], LANES))
        off += n
    w_rwkv = jnp.concatenate(pieces_w, axis=1).astype(BF16)
    mu = jnp.concatenate(pieces_mu, axis=1)
    inv_freq = ROPE_THETA ** (-jnp.arange(0, ROT_DIM, 2, dtype=F32) / ROT_DIM)
    lane_p = jnp.arange(LANES) % HEAD_DIM
    rot_tab = jnp.zeros((8, LANES), F32)
    rot_tab = rot_tab.at[0].set(jnp.where(lane_p < ROT_DIM, inv_freq[lane_p % (ROT_DIM // 2)], 0.0))
    rot_tab = rot_tab.at[1].set(jnp.where(lane_p < ROT_DIM // 2, -1.0, 0.0))
    rot_tab = rot_tab.at[2].set(jnp.where((lane_p >= ROT_DIM // 2) & (lane_p < ROT_DIM), 1.0, 0.0))
    vecs = jnp.stack([rwkv_w0, rwkv_a0, rwkv_k_k, rwkv_k_a, rwkv_r_k.reshape(-1), rwkv_ln_w, rwkv_ln_b,
                      jnp.zeros_like(rwkv_w0)])
    w2 = _pad_rows(rwkv_w2, LANES).astype(BF16)
    a2 = _pad_rows(rwkv_a2, LANES).astype(BF16)
    g2 = _pad_rows(rwkv_g2, LANES).astype(BF16)
    wo_a = w_out[:ATTN_WIDTH].astype(BF16)
    wo_r = w_out[ATTN_WIDTH:].astype(BF16)
    w_r_hi = w_router.astype(BF16)
    w_r_lo = (w_router - w_r_hi.astype(F32)).astype(BF16)
    w_r = jnp.concatenate([_pad_cols(w_r_hi, LANES), _pad_cols(w_r_lo, LANES)], axis=1)
    b_r = jnp.concatenate([b_router, jnp.full((LANES - N_EXPERTS,), NEG_INF, F32)])[None, :]

    mod = _mod(c, w_ada, b_ada).reshape(B, 6, D)
    qkv, rw = _inproj(x, positions, mod, w_attn, w_rwkv, mu, rot_tab, min(INPROJ_TILE, S))
    attn_out = _attention(qkv, attn_sinks, B, S, min(ATTN_STEP_BLOCKS, S // ATTN_BLOCK))
    rwkv_out = _rwkv(rw, vecs, w2, a2, g2, B, S, min(RWKV_STEP, S))

    mtile = min(MOE_TILE, S)
    x1, h2, top_i, gates, rank, cnt, tbase = _mix(attn_out, rwkv_out, x.reshape(T, D), mod, wo_a, wo_r,
                                                   jnp.stack([ln1_g, ln1_b]), w_r, b_r, B, S, mtile)

    counts = cnt[:, 0]
    padded = (counts + MOE_BLOCK - 1) // MOE_BLOCK * MOE_BLOCK
    pend = jnp.cumsum(padded)
    pstart = pend - padded
    n_blocks = T * TOP_K // MOE_BLOCK + N_EXPERTS
    blk_row = jnp.arange(n_blocks, dtype=jnp.int32) * MOE_BLOCK
    blk_e = jnp.minimum(jnp.sum((pend[None, :] <= blk_row[:, None]).astype(jnp.int32), axis=1), N_EXPERTS - 1)
    n_used = (pend[-1:] // MOE_BLOCK).astype(jnp.int32)
    tb = tbase[:, :, 0]
    tcnt = jnp.concatenate([tb[1:], counts[None]], axis=0) - tb
    lstart = jnp.cumsum(tcnt, axis=1) - tcnt
    gstart = pstart[None, :] + tb
    shift = jnp.repeat(jnp.transpose(lstart - tb), mtile, axis=1)
    experts = jnp.arange(N_EXPERTS, dtype=jnp.int32)
    lpos = rank + jnp.sum(jnp.where(top_i[None] == experts[:, None, None], shift[:, None, :], 0), axis=0)
    flat = lambda a: a.reshape(-1).astype(jnp.int32)

    xs = _dispatch(flat(tcnt), flat(lstart), flat(gstart), flat(padded - counts), flat(pstart + counts), n_used,
                   lpos, h2, n_blocks * MOE_BLOCK, mtile)
    later_with_rows = (experts[None, :] > experts[:, None]) & (counts[None, :] > 0)
    next_of = jnp.min(jnp.where(later_with_rows, experts[None, :], N_EXPERTS), axis=1)
    next_of = jnp.where(next_of < N_EXPERTS, next_of, -1)
    next_e = jnp.sum(jnp.where(blk_e[:, None] == experts[None, :], next_of[None, :], 0), axis=1).astype(jnp.int32)
    ys = _experts(blk_e, n_used, next_e, xs, w_gate_up, b_gate_up[:, None, :], w_down, b_down[:, None, :])
    out = _combine(flat(tcnt), flat(lstart), flat(gstart), ys, jnp.transpose(lpos), jnp.transpose(gates), x1, mod,
                   jnp.stack([ln2_g, ln2_b]), B, S, mtile)
    return out.reshape(B, S, D)


def kernel(x, c, positions, w_ada, b_ada, w_in, shift_mu, rwkv_w0, rwkv_w2, rwkv_a0, rwkv_a2, rwkv_g2,
           rwkv_k_k, rwkv_k_a, rwkv_r_k, rwkv_ln_w, rwkv_ln_b, attn_sinks, w_out, ln1_g, ln1_b,
           w_router, b_router, w_gate_up, b_gate_up, w_down, b_down, ln2_g, ln2_b):
    for l in range(DEPTH):
        x = _layer(x, c, positions, w_ada[l], b_ada[l], w_in[l], shift_mu[l], rwkv_w0[l], rwkv_w2[l],
                   rwkv_a0[l], rwkv_a2[l], rwkv_g2[l], rwkv_k_k[l], rwkv_k_a[l], rwkv_r_k[l], rwkv_ln_w[l],
                   rwkv_ln_b[l], attn_sinks[l], w_out[l], ln1_g[l], ln1_b[l], w_router[l], b_router[l],
                   w_gate_up[l], b_gate_up[l], w_down[l], b_down[l], ln2_g[l], ln2_b[l])
    return x
```

```python
import functools
import math

import jax
import jax.numpy as jnp
from jax import lax
from jax.experimental import pallas as pl
from jax.experimental.pallas import tpu as pltpu

F32 = jnp.float32
BF16 = jnp.bfloat16

HEAD_DIM = 64
N_ATTN_HEADS = 8
N_KV_HEADS = 2
N_RWKV_HEADS = 8
ATTN_WIDTH = N_ATTN_HEADS * HEAD_DIM
KV_WIDTH = N_KV_HEADS * HEAD_DIM
RWKV_WIDTH = N_RWKV_HEADS * HEAD_DIM
ATTN_BLOCK = 128
ROT_DIM = HEAD_DIM // 4
ROPE_THETA = 500000.0
DECAY_LORA = 32
AAA_LORA = 32
GATE_LORA = 96
N_EXPERTS = 32
TOP_K = 4
SWIGLU_LIMIT = 7.0
SWIGLU_ALPHA = 1.702
LN_EPS = 1e-5
RWKV_GN_EPS = 64e-5
NEG_INF = -1e30
DEPTH = 1
DEEPNORM_ALPHA = (2 * DEPTH) ** 0.25

LANES = 128
RWKV_CHUNK = 64
RWKV_STEP = 512
INPROJ_TILE = 512
ATTN_STEP_BLOCKS = 2
MOE_BLOCK = 512
MOE_TILE = 256
ATTN_PROJ = ATTN_WIDTH + 4 * KV_WIDTH
RWKV_PROJ = 3 * RWKV_WIDTH + 3 * LANES
VMEM_LIMIT = 48 * 1024 * 1024


def _dot(a, b):
    return jnp.dot(a, b, preferred_element_type=F32)


def _dot_nt(a, b):
    return lax.dot_general(a, b, (((1,), (1,)), ((), ())), preferred_element_type=F32)


def _dot_tn(a, b):
    return lax.dot_general(a, b, (((0,), (0,)), ((), ())), preferred_element_type=F32)


def _split3(x):
    h = x.astype(BF16)
    r1 = x - h.astype(F32)
    m = r1.astype(BF16)
    lo = (r1 - m.astype(F32)).astype(BF16)
    return h, m, lo


def _dot_exact_lhs(m_bf16, x):
    h, m, lo = _split3(x)
    return _dot(m_bf16, h) + _dot(m_bf16, m) + _dot(m_bf16, lo)


def _layer_norm(x):
    mu = jnp.mean(x, axis=-1, keepdims=True)
    xc = x - mu
    var = jnp.mean(xc * xc, axis=-1, keepdims=True)
    return xc * lax.rsqrt(var + LN_EPS)


def _sigmoid(x):
    return 1.0 / (1.0 + jnp.exp(-x))


def _mod_kernel(c_ref, w_ref, b_ref, o_ref):
    c = c_ref[...]
    s = c * _sigmoid(c)
    o_ref[...] = jnp.dot(s, w_ref[...], preferred_element_type=F32,
                         precision=lax.Precision.HIGHEST) + b_ref[...]


def _mod(c, w_ada, b_ada):
    B, D = c.shape
    n = w_ada.shape[1] // D
    return pl.pallas_call(
        _mod_kernel,
        grid=(n,),
        in_specs=[pl.BlockSpec((B, D), lambda i: (0, 0)),
                  pl.BlockSpec((D, D), lambda i: (0, i)),
                  pl.BlockSpec((1, D), lambda i: (0, i))],
        out_specs=pl.BlockSpec((B, D), lambda i: (0, i)),
        out_shape=jax.ShapeDtypeStruct((B, n * D), F32),
        compiler_params=pltpu.CompilerParams(dimension_semantics=("arbitrary",),
                                             vmem_limit_bytes=VMEM_LIMIT),
        name="mod",
    )(c, w_ada, b_ada.reshape(1, -1))


def _inproj_kernel(x_ref, pos_ref, mod_ref, wa_ref, wr_ref, mu_ref, rt_ref, fq_ref, ex_ref,
                   qkv_ref, rw_ref, carry_ref):
    j = pl.program_id(1)
    x = x_ref[0]
    tm = x.shape[0]
    mod = mod_ref[0]
    h = _layer_norm(x) * (1.0 + mod[1:2, :]) + mod[0:1, :]
    hb = h.astype(BF16)

    pa = _dot(hb, wa_ref[...])
    ang = pos_ref[0].astype(F32) * fq_ref[:, 0:1]
    pieces = _split3(jnp.concatenate([jnp.cos(ang), jnp.sin(ang)], axis=0))
    trig = _dot_tn(pieces[0], ex_ref[...]) + _dot_tn(pieces[1], ex_ref[...]) + _dot_tn(pieces[2], ex_ref[...])
    cs = trig[:, 0:LANES] + rt_ref[0:1, :]
    sn = trig[:, LANES:2 * LANES]
    m_lo = rt_ref[1:2, :]
    m_hi = rt_ref[2:3, :]
    n_q = ATTN_WIDTH // LANES
    n_rot = (ATTN_WIDTH + 2 * KV_WIDTH) // LANES
    for ch in range(n_rot):
        t = pa[:, ch * LANES:(ch + 1) * LANES]
        if ch < n_q:
            t = t * (1.0 / math.sqrt(HEAD_DIM))
        up = pltpu.roll(t, LANES - ROT_DIM // 2, 1)
        dn = pltpu.roll(t, ROT_DIM // 2, 1)
        o = t * cs + sn * (m_lo * up + m_hi * dn)
        qkv_ref[:, ch * LANES:(ch + 1) * LANES] = o.astype(BF16)
    qkv_ref[:, n_rot * LANES:] = pa[:, n_rot * LANES:].astype(BF16)

    pr = _dot(hb, wr_ref[...])
    prev = pltpu.roll(pr, 1, 0)
    row = lax.broadcasted_iota(jnp.int32, (tm, 1), 0)
    carry = jnp.where(j == 0, 0.0, carry_ref[...])
    prev = jnp.where(row == 0, carry, prev)
    carry_ref[...] = pr[tm - 1:tm, :]
    rw_ref[...] = pr + (prev - pr) * mu_ref[...]


def _inproj(x, positions, mod, w_attn, w_rwkv, mu, rot_tab, freq_tab, expand, tm):
    B, S, D = x.shape
    nt = S // tm
    return pl.pallas_call(
        _inproj_kernel,
        grid=(B, nt),
        in_specs=[pl.BlockSpec((1, tm, D), lambda b, j: (b, j, 0)),
                  pl.BlockSpec((1, 1, tm), lambda b, j: (b, 0, j)),
                  pl.BlockSpec((1, 6, D), lambda b, j: (b, 0, 0)),
                  pl.BlockSpec((D, ATTN_PROJ), lambda b, j: (0, 0)),
                  pl.BlockSpec((D, RWKV_PROJ), lambda b, j: (0, 0)),
                  pl.BlockSpec((1, RWKV_PROJ), lambda b, j: (0, 0)),
                  pl.BlockSpec((8, LANES), lambda b, j: (0, 0)),
                  pl.BlockSpec(freq_tab.shape, lambda b, j: (0, 0)),
                  pl.BlockSpec(expand.shape, lambda b, j: (0, 0))],
        out_specs=[pl.BlockSpec((tm, ATTN_PROJ), lambda b, j: (b * nt + j, 0)),
                   pl.BlockSpec((tm, RWKV_PROJ), lambda b, j: (b * nt + j, 0))],
        out_shape=[jax.ShapeDtypeStruct((B * S, ATTN_PROJ), BF16),
                   jax.ShapeDtypeStruct((B * S, RWKV_PROJ), F32)],
        scratch_shapes=[pltpu.VMEM((1, RWKV_PROJ), F32)],
        compiler_params=pltpu.CompilerParams(dimension_semantics=("arbitrary", "arbitrary"),
                                             vmem_limit_bytes=VMEM_LIMIT),
        name="inproj",
    )(x, positions.reshape(B, 1, S), mod, w_attn, w_rwkv, mu, rot_tab, freq_tab, expand)


def _attn_kernel(q_ref, kc_ref, kp_ref, vc_ref, vp_ref, sink_ref, o_ref):
    i = pl.program_id(1)
    blk = ATTN_BLOCK
    n_sub = q_ref.shape[0] // blk
    qi = lax.broadcasted_iota(jnp.int32, (blk, 2 * blk), 0)
    kj = lax.broadcasted_iota(jnp.int32, (blk, 2 * blk), 1)
    band = (kj > qi) & (kj <= qi + blk)
    first = band & ((kj >= blk) | (i > 0))
    lane = lax.broadcasted_iota(jnp.int32, (1, LANES), 1)
    lo = (lane < HEAD_DIM).astype(BF16)
    hi = (lane >= HEAD_DIM).astype(BF16)
    halves = {}
    for u in range(n_sub):
        for g in range(N_KV_HEADS):
            sl = slice(g * LANES, (g + 1) * LANES)
            prev_k = kp_ref[:, sl] if u == 0 else kc_ref[(u - 1) * blk:u * blk, sl]
            prev_v = vp_ref[:, sl] if u == 0 else vc_ref[(u - 1) * blk:u * blk, sl]
            kcat = jnp.concatenate([prev_k, kc_ref[u * blk:(u + 1) * blk, sl]], axis=0)
            vcat = jnp.concatenate([prev_v, vc_ref[u * blk:(u + 1) * blk, sl]], axis=0)
            halves[u, g] = ((kcat * lo, vcat * lo), (kcat * hi, vcat * hi))
    units = [(u, c, half) for u in range(n_sub) for c in range(ATTN_WIDTH // LANES) for half in range(2)]
    scores = [_dot_nt(q_ref[u * blk:(u + 1) * blk, c * LANES:(c + 1) * LANES], halves[u, c // 2][half][0])
              for u, c, half in units]
    probs, denoms = [], []
    for (u, c, half), s in zip(units, scores):
        sink = sink_ref[2 * c + half]
        s = jnp.where(first if u == 0 else band, s, NEG_INF)
        m = jnp.maximum(jnp.max(s, axis=-1, keepdims=True), sink)
        p = jnp.exp(s - m)
        denoms.append(jnp.sum(p, axis=-1, keepdims=True) + jnp.exp(sink - m))
        probs.append(p.astype(BF16))
    outs = [_dot(p, halves[u, c // 2][half][1]) / d for (u, c, half), p, d in zip(units, probs, denoms)]
    for n, (u, c, half) in enumerate(units):
        if half == 0:
            o_ref[u * blk:(u + 1) * blk, c * LANES:(c + 1) * LANES] = (outs[n] + outs[n + 1]).astype(BF16)


def _attention(qkv, sinks, B, S, n_sub):
    blk = ATTN_BLOCK
    step = n_sub * blk
    ns = S // step
    kcol = ATTN_WIDTH // (2 * KV_WIDTH)
    vcol = kcol + 1
    cur = lambda col: (lambda b, i: (b * ns + i, col))
    prv = lambda col: (lambda b, i: (jnp.maximum((b * ns + i) * n_sub - 1, 0), col))
    return pl.pallas_call(
        _attn_kernel,
        grid=(B, ns),
        in_specs=[pl.BlockSpec((step, ATTN_WIDTH), lambda b, i: (b * ns + i, 0)),
                  pl.BlockSpec((step, 2 * KV_WIDTH), cur(kcol)),
                  pl.BlockSpec((blk, 2 * KV_WIDTH), prv(kcol)),
                  pl.BlockSpec((step, 2 * KV_WIDTH), cur(vcol)),
                  pl.BlockSpec((blk, 2 * KV_WIDTH), prv(vcol)),
                  pl.BlockSpec(memory_space=pltpu.SMEM)],
        out_specs=pl.BlockSpec((step, ATTN_WIDTH), lambda b, i: (b * ns + i, 0)),
        out_shape=jax.ShapeDtypeStruct((B * S, ATTN_WIDTH), BF16),
        compiler_params=pltpu.CompilerParams(dimension_semantics=("arbitrary", "arbitrary"),
                                             vmem_limit_bytes=VMEM_LIMIT),
        name="attn",
    )(qkv, qkv, qkv, qkv, qkv, sinks)


def _rwkv_kernel(rw_ref, vec_ref, w2_ref, a2_ref, g2_ref, o_ref, state_ref, *, n_chunk):
    j = pl.program_id(1)
    C = RWKV_CHUNK
    W = RWKV_WIDTH
    n_pair = W // LANES

    @pl.when(j == 0)
    def _():
        state_ref[...] = jnp.zeros_like(state_ref)

    w0 = vec_ref[0:1, :]
    a0 = vec_ref[1:2, :]
    k_k = vec_ref[2:3, :]
    k_a = vec_ref[3:4, :]
    r_k = vec_ref[4:5, :]
    ln_w = vec_ref[5:6, :]
    ln_b = vec_ref[6:7, :]

    r = rw_ref[:, 0:W]
    k = rw_ref[:, W:2 * W]
    v = rw_ref[:, 2 * W:3 * W]
    wl = rw_ref[:, 3 * W:3 * W + LANES]
    al = rw_ref[:, 3 * W + LANES:3 * W + 2 * LANES]
    gl = rw_ref[:, 3 * W + 2 * LANES:3 * W + 3 * LANES]

    ri = lax.broadcasted_iota(jnp.int32, (LANES, LANES), 0)
    ci = lax.broadcasted_iota(jnp.int32, (LANES, LANES), 1)
    same = (ri // HEAD_DIM) == (ci // HEAD_DIM)
    strict = same & ((ri % HEAD_DIM) > (ci % HEAD_DIM))
    incl = same & ((ri % HEAD_DIM) >= (ci % HEAD_DIM))
    lane = lax.broadcasted_iota(jnp.int32, (1, LANES), 1)
    m0 = (lane < HEAD_DIM).astype(F32)
    m1 = 1.0 - m0
    tri = (lax.broadcasted_iota(jnp.int32, (C, C), 0) >= lax.broadcasted_iota(jnp.int32, (C, C), 1)).astype(BF16)

    def head_sum(xv):
        outs = []
        for p in range(n_pair):
            xp = xv[:, p * LANES:(p + 1) * LANES]
            s0 = jnp.sum(xp * m0, axis=1, keepdims=True)
            s1 = jnp.sum(xp * m1, axis=1, keepdims=True)
            outs.append(s0 * m0 + s1 * m1)
        return jnp.concatenate(outs, axis=1)

    def stack2(xp):
        return jnp.concatenate([xp * m0, xp * m1], axis=0)

    z = w0 + _dot(jnp.tanh(wl).astype(BF16), w2_ref[...])
    lw = -math.exp(-0.5) * _sigmoid(z)
    a = _sigmoid(a0 + _dot(al.astype(BF16), a2_ref[...]))
    g = _dot(_sigmoid(gl).astype(BF16), g2_ref[...])
    kk = k * k_k
    kkn = kk / jnp.maximum(jnp.sqrt(head_sum(kk * kk)), 1e-12)
    k2 = k * (1.0 + (a - 1.0) * k_a)
    av = -kkn
    bv = kkn * a
    bonus = head_sum(r * k2 * r_k) * v

    eye = (ri == ci).astype(F32)
    bf = lambda t: t.astype(BF16)

    pre = []
    for c in range(n_chunk):
        rows = slice(c * C, (c + 1) * C)
        lwc = lw[rows]
        cw = _dot_exact_lhs(tri, lwc)
        cwl = cw[C - 1:C, :]
        e_in = jnp.exp(cw)
        e_neg = jnp.exp(-cw)
        e_rem = jnp.exp(cwl - cw)
        wc = jnp.exp(cwl)
        Rt = r[rows] * e_in
        At = av[rows] * jnp.exp(cw - lwc)
        Bb = bv[rows] * e_neg
        Kb = k2[rows] * e_neg
        Bh = bv[rows] * e_rem
        Kh = k2[rows] * e_rem
        vc = v[rows]
        for p in range(n_pair):
            sl = slice(p * LANES, (p + 1) * LANES)
            pre.append(dict(At=At[:, sl], Rt=Rt[:, sl], Bb=Bb[:, sl], Kb=Kb[:, sl], Bh=Bh[:, sl], Kh=Kh[:, sl],
                            v=vc[:, sl], wc=wc[:, sl]))

    for u in pre:
        u["at_bd"] = stack2(u["At"])
        lhs = bf(jnp.concatenate([u["at_bd"], stack2(u["Rt"])], axis=0))
        rhs = bf(jnp.concatenate([stack2(u["Bb"]), stack2(u["Kb"])], axis=0))
        u["G"] = _dot_nt(lhs, rhs)
    for u in pre:
        G = u.pop("G")
        u["a_ab"] = jnp.where(strict, G[0:2 * C, 0:2 * C], 0.0)
        u["a_ak"] = bf(jnp.where(strict, G[0:2 * C, 2 * C:4 * C], 0.0))
        u["a_rb"] = bf(jnp.where(incl, G[2 * C:4 * C, 0:2 * C], 0.0))
        u["a_rk"] = bf(jnp.where(incl, G[2 * C:4 * C, 2 * C:4 * C], 0.0))
        u["v_bd"] = bf(stack2(u["v"]))
    for u in pre:
        xb = bf(u["a_ab"])
        u["P"] = eye + u.pop("a_ab")
        u["X"] = _dot(xb, xb)
        u["M0"] = _dot(u["a_ak"], u["v_bd"])
    for _ in range(int(math.log2(C)) - 2):
        for u in pre:
            Wm = _dot(bf(u["X"]), bf(jnp.concatenate([u["P"], u["X"]], axis=1)))
            u["P"] = u["P"] + Wm[:, 0:LANES]
            u["X"] = Wm[:, LANES:2 * LANES]
    for u in pre:
        u["P"] = bf(u["P"] + _dot(bf(u.pop("X")), bf(u["P"])))
    for u in pre:
        u["M1"] = _dot(u["P"], bf(u.pop("M0")))
        u["Q"] = bf(_dot(u["a_rb"], u["P"]))
        u["PtB"] = _dot_tn(u["P"], bf(stack2(u["Bh"])))
    for u in pre:
        M1 = u.pop("M1")
        u["Y0"] = _dot(jnp.concatenate([u["a_rb"], u["a_rk"]], axis=1),
                       jnp.concatenate([bf(M1), u["v_bd"]], axis=0))
        u["Tm"] = bf(_dot_tn(bf(u["at_bd"]), bf(u.pop("PtB"))))
        m1_pair = M1[0:C] + M1[C:2 * C]
        cst = _dot_tn(bf(jnp.concatenate([m1_pair, u["v"]], axis=0)),
                      bf(jnp.concatenate([u["Bh"], u["Kh"]], axis=0)))
        u["cst"] = jnp.where(same, cst, 0.0)
        u["ar"] = bf(jnp.concatenate([u["At"], u["Rt"]], axis=0))

    states = [state_ref[p] for p in range(n_pair)]
    for c in range(n_chunk):
        rows = slice(c * C, (c + 1) * C)
        us = pre[c * n_pair:(c + 1) * n_pair]
        sbs = [bf(S) for S in states]
        zs = [_dot_nt(u["ar"], sb) for u, sb in zip(us, sbs)]
        new_states = [S * u["wc"] + _dot(sb, u["Tm"]) + u["cst"] for u, S, sb in zip(us, states, sbs)]
        ybds = [stack2(Z[C:2 * C]) + _dot(u["Q"], bf(stack2(Z[0:C]))) + u["Y0"] for u, Z in zip(us, zs)]
        ys = [y_bd[0:C] + y_bd[C:2 * C] for y_bd in ybds]
        states = new_states
        y = jnp.concatenate(ys, axis=1)
        mu = head_sum(y) * (1.0 / HEAD_DIM)
        yc = y - mu
        var = head_sum(yc * yc) * (1.0 / HEAD_DIM)
        yn = yc * lax.rsqrt(var + RWKV_GN_EPS) * ln_w + ln_b
        o_ref[rows, :] = ((yn + bonus[rows]) * g[rows]).astype(BF16)
    for p in range(n_pair):
        state_ref[p] = states[p]


def _rwkv(rw, vecs, w2, a2, g2, B, S, lb):
    nt = S // lb
    return pl.pallas_call(
        functools.partial(_rwkv_kernel, n_chunk=lb // RWKV_CHUNK),
        grid=(B, nt),
        in_specs=[pl.BlockSpec((lb, RWKV_PROJ), lambda b, j: (b * nt + j, 0)),
                  pl.BlockSpec((8, RWKV_WIDTH), lambda b, j: (0, 0)),
                  pl.BlockSpec((LANES, RWKV_WIDTH), lambda b, j: (0, 0)),
                  pl.BlockSpec((LANES, RWKV_WIDTH), lambda b, j: (0, 0)),
                  pl.BlockSpec((LANES, RWKV_WIDTH), lambda b, j: (0, 0))],
        out_specs=pl.BlockSpec((lb, RWKV_WIDTH), lambda b, j: (b * nt + j, 0)),
        out_shape=jax.ShapeDtypeStruct((B * S, RWKV_WIDTH), BF16),
        scratch_shapes=[pltpu.VMEM((RWKV_WIDTH // LANES, LANES, LANES), F32)],
        compiler_params=pltpu.CompilerParams(dimension_semantics=("arbitrary", "arbitrary"),
                                             vmem_limit_bytes=VMEM_LIMIT),
        name="rwkv",
    )(rw, vecs, w2, a2, g2)


def _mix_kernel(at_ref, rk_ref, x_ref, mod_ref, wo_a_ref, wo_r_ref, ln_ref, wr_ref, br_ref,
                x1_ref, h2_ref, ti_ref, gt_ref, rank_ref, cnt_ref, tb_ref, base_ref):
    first = (pl.program_id(0) == 0) & (pl.program_id(1) == 0)

    @pl.when(first)
    def _():
        base_ref[...] = jnp.zeros_like(base_ref)

    mod = mod_ref[0]
    y = _dot(at_ref[...], wo_a_ref[...]) + _dot(rk_ref[...], wo_r_ref[...])
    x = x_ref[...]
    tm = x.shape[0]
    x1 = _layer_norm(DEEPNORM_ALPHA * x + (1.0 + mod[2:3, :]) * y) * ln_ref[0:1, :] + ln_ref[1:2, :]
    h2 = _layer_norm(x1) * (1.0 + mod[4:5, :]) + mod[3:4, :]
    x1_ref[...] = x1
    h_hi = h2.astype(BF16)
    h2_ref[...] = h_hi

    h_lo = (h2 - h_hi.astype(F32)).astype(BF16)
    part = _dot(h_hi, wr_ref[...])
    logits = part[:, 0:LANES] + part[:, LANES:2 * LANES] + _dot(h_lo, wr_ref[:, 0:LANES]) + br_ref[...]
    lt = jnp.transpose(logits)[0:N_EXPERTS, :]
    erow = lax.broadcasted_iota(jnp.int32, (N_EXPERTS, tm), 0).astype(F32)
    cur = lt
    vals, idxs = [], []
    for _ in range(TOP_K):
        m = jnp.max(cur, axis=0, keepdims=True)
        idx = jnp.min(jnp.where(cur == m, erow, float(N_EXPERTS)), axis=0, keepdims=True)
        vals.append(m)
        idxs.append(idx)
        cur = jnp.where(erow == idx, -jnp.inf, cur)
    tv = jnp.concatenate(vals, axis=0)
    e = jnp.exp(tv - tv[0:1, :])
    gt_ref[...] = e / jnp.sum(e, axis=0, keepdims=True)
    ti_ref[...] = jnp.concatenate(idxs, axis=0).astype(jnp.int32)

    onehot = jnp.zeros((N_EXPERTS, tm), F32)
    for idx in idxs:
        onehot = onehot + (erow == idx).astype(F32)
    before = (lax.broadcasted_iota(jnp.int32, (tm, tm), 0)
              < lax.broadcasted_iota(jnp.int32, (tm, tm), 1)).astype(BF16)
    tot = base_ref[:, 0:1] + _dot(onehot.astype(BF16), before)
    ranks = [jnp.sum(jnp.where(erow == idx, tot, 0.0), axis=0, keepdims=True) for idx in idxs]
    rank_ref[...] = jnp.concatenate(ranks, axis=0).astype(jnp.int32)
    tb_ref[0] = base_ref[...].astype(jnp.int32)
    base_ref[...] = base_ref[...] + jnp.sum(onehot, axis=1, keepdims=True)
    cnt_ref[...] = base_ref[...].astype(jnp.int32)


def _mix(attn_out, rwkv_out, x2d, mod, wo_a, wo_r, ln1, w_router, b_router, B, S, tm):
    D = x2d.shape[1]
    nt = S // tm
    T = B * S
    tok = lambda b, j: (b * nt + j, 0)
    col = lambda b, j: (0, b * nt + j)
    fixed = lambda b, j: (0, 0)
    return pl.pallas_call(
        _mix_kernel,
        grid=(B, nt),
        in_specs=[pl.BlockSpec((tm, ATTN_WIDTH), tok),
                  pl.BlockSpec((tm, RWKV_WIDTH), tok),
                  pl.BlockSpec((tm, D), tok),
                  pl.BlockSpec((1, 6, D), lambda b, j: (b, 0, 0)),
                  pl.BlockSpec((ATTN_WIDTH, D), fixed),
                  pl.BlockSpec((RWKV_WIDTH, D), fixed),
                  pl.BlockSpec((2, D), fixed),
                  pl.BlockSpec((D, 2 * LANES), fixed),
                  pl.BlockSpec((1, LANES), fixed)],
        out_specs=[pl.BlockSpec((tm, D), tok),
                   pl.BlockSpec((tm, D), tok),
                   pl.BlockSpec((TOP_K, tm), col),
                   pl.BlockSpec((TOP_K, tm), col),
                   pl.BlockSpec((TOP_K, tm), col),
                   pl.BlockSpec((N_EXPERTS, LANES), fixed),
                   pl.BlockSpec((1, N_EXPERTS, LANES), lambda b, j: (b * nt + j, 0, 0))],
        out_shape=[jax.ShapeDtypeStruct((T, D), F32),
                   jax.ShapeDtypeStruct((T, D), BF16),
                   jax.ShapeDtypeStruct((TOP_K, T), jnp.int32),
                   jax.ShapeDtypeStruct((TOP_K, T), F32),
                   jax.ShapeDtypeStruct((TOP_K, T), jnp.int32),
                   jax.ShapeDtypeStruct((N_EXPERTS, LANES), jnp.int32),
                   jax.ShapeDtypeStruct((T // tm, N_EXPERTS, LANES), jnp.int32)],
        scratch_shapes=[pltpu.VMEM((N_EXPERTS, LANES), F32)],
        compiler_params=pltpu.CompilerParams(dimension_semantics=("arbitrary", "arbitrary"),
                                             vmem_limit_bytes=VMEM_LIMIT),
        name="mix",
    )(attn_out, rwkv_out, x2d, mod, wo_a, wo_r, ln1, w_router, b_router)


RUN_PIECES = tuple(2 ** b for b in range(int(math.log2(MOE_TILE)), -1, -1))
RUN_LARGE = 64
SUBLANES = 8


def _to_tiles(ref, x):
    n = x.shape[0]
    for c in range(SUBLANES):
        ref[pl.ds(c, n, stride=SUBLANES), :] = x[:, c * LANES:(c + 1) * LANES]


def _from_tiles(ref):
    n = ref.shape[0] // SUBLANES
    return jnp.concatenate([ref[pl.ds(c, n, stride=SUBLANES), :] for c in range(SUBLANES)], axis=1)


def _run_copies(n, local, local_start, remote, remote_start, sem, to_remote):
    def copy_piece(piece):
        @pl.when((n & piece) != 0)
        def _():
            off = n & ~(2 * piece - 1)
            lo = pl.multiple_of((local_start + off) * SUBLANES, SUBLANES)
            ro = pl.multiple_of((remote_start + off) * SUBLANES, SUBLANES)
            loc = local.at[pl.ds(lo, piece * SUBLANES)]
            rem = remote.at[pl.ds(ro, piece * SUBLANES)]
            src, dst = (loc, rem) if to_remote else (rem, loc)
            pltpu.make_async_copy(src, dst, sem).start()

    @pl.when(n >= RUN_LARGE)
    def _():
        for piece in RUN_PIECES:
            if piece >= RUN_LARGE:
                copy_piece(piece)

    for piece in RUN_PIECES:
        if piece < RUN_LARGE:
            copy_piece(piece)


def _dispatch_kernel(tcnt_ref, lstart_ref, gstart_ref, pad_ref, pad_start_ref, n_used_ref, lpos_ref, h2_ref,
                     xs_ref, xbuf, zbuf, sems):
    i = pl.program_id(0)
    tm = h2_ref.shape[0]
    n_loc = TOP_K * tm
    n_blocks = xs_ref.shape[0] // (MOE_BLOCK * SUBLANES)
    zero_sem = sems.at[2]

    @pl.when(i == 0)
    def _():
        zbuf[...] = jnp.zeros_like(zbuf)

        def zero_pad(e, carry):
            _run_copies(pad_ref[e], zbuf, 0, xs_ref, pad_start_ref[e], zero_sem, True)
            return carry

        def zero_tail(b, carry):
            @pl.when(b >= n_used_ref[0])
            def _():
                start = pl.multiple_of(b * (MOE_BLOCK * SUBLANES), MOE_BLOCK * SUBLANES)
                pltpu.make_async_copy(zbuf, xs_ref.at[pl.ds(start, MOE_BLOCK * SUBLANES)], zero_sem).start()
            return carry

        lax.fori_loop(0, N_EXPERTS, zero_pad, 0)
        lax.fori_loop(n_blocks - N_EXPERTS, n_blocks, zero_tail, 0)

    slot = lax.broadcasted_iota(jnp.int32, (n_loc, tm), 0)
    lpos = lpos_ref[...]
    perm = jnp.zeros((n_loc, tm), F32)
    for k in range(TOP_K):
        perm = perm + (slot == lpos[k:k + 1, :]).astype(F32)
    rows = _dot(perm.astype(BF16), h2_ref[...])

    def wait_tile(s):
        pltpu.make_async_copy(xbuf.at[s], xs_ref.at[pl.ds(0, n_loc * SUBLANES)], sems.at[s]).wait()

    for s in range(2):
        @pl.when((i % 2 == s) & (i >= 2))
        def _(s=s):
            wait_tile(s)

    for s in range(2):
        @pl.when(i % 2 == s)
        def _(s=s):
            _to_tiles(xbuf.at[s], rows)

            def issue(e, carry):
                idx = i * N_EXPERTS + e
                _run_copies(tcnt_ref[idx], xbuf.at[s], lstart_ref[idx], xs_ref, gstart_ref[idx], sems.at[s], True)
                return carry

            lax.fori_loop(0, N_EXPERTS, issue, 0)

    @pl.when(i == pl.num_programs(0) - 1)
    def _():
        for s in range(2):
            @pl.when((i % 2 == s) | (i >= 1))
            def _(s=s):
                wait_tile(s)
        n_zero = N_EXPERTS * MOE_BLOCK * SUBLANES
        pltpu.make_async_copy(xs_ref.at[pl.ds(0, n_zero)], xs_ref.at[pl.ds(0, n_zero)], zero_sem).wait()


def _dispatch(tcnt, lstart, gstart, pad, pad_start, n_used, lpos, h2, n_rows, tm):
    T, D = h2.shape
    grid_spec = pltpu.PrefetchScalarGridSpec(
        num_scalar_prefetch=6,
        grid=(T // tm,),
        in_specs=[pl.BlockSpec((TOP_K, tm), lambda i, *_: (0, i)),
                  pl.BlockSpec((tm, D), lambda i, *_: (i, 0))],
        out_specs=pl.BlockSpec(memory_space=pl.ANY),
        scratch_shapes=[pltpu.VMEM((2, TOP_K * tm * SUBLANES, LANES), F32),
                        pltpu.VMEM((MOE_BLOCK * SUBLANES, LANES), F32),
                        pltpu.SemaphoreType.DMA((3,))],
    )
    return pl.pallas_call(
        _dispatch_kernel,
        grid_spec=grid_spec,
        out_shape=jax.ShapeDtypeStruct((n_rows * SUBLANES, LANES), F32),
        compiler_params=pltpu.CompilerParams(dimension_semantics=("arbitrary",),
                                             vmem_limit_bytes=VMEM_LIMIT),
        name="dispatch",
    )(tcnt, lstart, gstart, pad, pad_start, n_used, lpos, h2)


def _experts_kernel(blk_e_ref, n_used_ref, next_e_ref, xs_ref, wgu_hbm, bgu_ref, wd_hbm, bd_ref, ys_ref,
                    wgu_f32, wd_f32, wgu_bf, wd_bf, sems):
    i = pl.program_id(0)
    d_ff = wd_bf.shape[0]
    used = i < n_used_ref[0]
    e = blk_e_ref[i]
    new_expert = (i == 0) | (e != blk_e_ref[jnp.maximum(i - 1, 0)])

    def weight_copies(ex):
        return (pltpu.make_async_copy(wgu_hbm.at[ex], wgu_f32, sems.at[0]),
                pltpu.make_async_copy(wd_hbm.at[ex], wd_f32, sems.at[1]))

    @pl.when(i == 0)
    def _():
        for cp in weight_copies(e):
            cp.start()

    @pl.when(used & new_expert)
    def _():
        for cp in weight_copies(e):
            cp.wait()
        wgu_bf[...] = wgu_f32[...].astype(BF16)
        wd_bf[...] = wd_f32[...].astype(BF16)
        nxt = next_e_ref[i]

        @pl.when(nxt >= 0)
        def _():
            for cp in weight_copies(nxt):
                cp.start()

    @pl.when(used)
    def _():
        xb = _from_tiles(xs_ref).astype(BF16)
        gu = _dot(xb, wgu_bf[...]) + bgu_ref[0]
        gate = jnp.minimum(gu[:, :d_ff], SWIGLU_LIMIT)
        up = jnp.clip(gu[:, d_ff:], -SWIGLU_LIMIT, SWIGLU_LIMIT)
        act = (up + 1.0) * (gate * _sigmoid(SWIGLU_ALPHA * gate))
        _to_tiles(ys_ref, _dot(act.astype(BF16), wd_bf[...]) + bd_ref[0])

    @pl.when(i >= n_used_ref[0])
    def _():
        ys_ref[...] = jnp.zeros_like(ys_ref)


def _experts(blk_e, n_used, next_e, xs, wgu, bgu, wd, bd):
    d_ff, D = wd.shape[1], wd.shape[2]
    n_blocks = xs.shape[0] // (MOE_BLOCK * SUBLANES)
    blk = (MOE_BLOCK * SUBLANES, LANES)

    def last_used(i, n_used_ref):
        return jnp.minimum(i, jnp.maximum(n_used_ref[0] - 1, 0))

    def row_map(i, blk_e_ref, n_used_ref, next_e_ref):
        return (last_used(i, n_used_ref), 0)

    def exp_map(i, blk_e_ref, n_used_ref, next_e_ref):
        return (blk_e_ref[last_used(i, n_used_ref)], 0, 0)

    grid_spec = pltpu.PrefetchScalarGridSpec(
        num_scalar_prefetch=3,
        grid=(n_blocks,),
        in_specs=[pl.BlockSpec(blk, row_map),
                  pl.BlockSpec(memory_space=pl.ANY),
                  pl.BlockSpec((1, 1, 2 * d_ff), exp_map),
                  pl.BlockSpec(memory_space=pl.ANY),
                  pl.BlockSpec((1, 1, D), exp_map)],
        out_specs=pl.BlockSpec(blk, lambda i, *_: (i, 0)),
        scratch_shapes=[pltpu.VMEM((D, 2 * d_ff), F32), pltpu.VMEM((d_ff, D), F32),
                        pltpu.VMEM((D, 2 * d_ff), BF16), pltpu.VMEM((d_ff, D), BF16),
                        pltpu.SemaphoreType.DMA((2,))],
    )
    return pl.pallas_call(
        _experts_kernel,
        grid_spec=grid_spec,
        out_shape=jax.ShapeDtypeStruct(xs.shape, F32),
        compiler_params=pltpu.CompilerParams(dimension_semantics=("arbitrary",),
                                             vmem_limit_bytes=VMEM_LIMIT),
        name="experts",
    )(blk_e, n_used, next_e, xs, wgu, bgu, wd, bd)


def _combine_kernel(tcnt_ref, lstart_ref, gstart_ref, ys_ref, lpos_ref, gt_ref, x1_ref, mod_ref, ln_ref,
                    o_ref, buf, y_ref, sems):
    nt = pl.num_programs(1)
    n_tiles = pl.num_programs(0) * nt
    i = pl.program_id(0) * nt + pl.program_id(1)
    tm = x1_ref.shape[0]
    n_loc = TOP_K * tm

    def fetch(tile, s):
        def issue(e, carry):
            idx = tile * N_EXPERTS + e
            _run_copies(tcnt_ref[idx], buf.at[s], lstart_ref[idx], ys_ref, gstart_ref[idx], sems.at[s], False)
            return carry

        lax.fori_loop(0, N_EXPERTS, issue, 0)

    @pl.when(i == 0)
    def _():
        fetch(i, 0)

    for s in range(2):
        @pl.when(((i + 1) % 2 == s) & (i + 1 < n_tiles))
        def _(s=s):
            fetch(i + 1, s)

    slot = lax.broadcasted_iota(jnp.int32, (tm, n_loc), 1)
    lpos = lpos_ref[...]
    gt = gt_ref[...]
    pick = jnp.zeros((tm, n_loc), F32)
    for k in range(TOP_K):
        pick = pick + jnp.where(slot == lpos[:, k:k + 1], gt[:, k:k + 1], 0.0)
    pick = pick.astype(BF16)
    for s in range(2):
        @pl.when(i % 2 == s)
        def _(s=s):
            pltpu.make_async_copy(ys_ref.at[pl.ds(0, n_loc * SUBLANES)], buf.at[s], sems.at[s]).wait()
            y_ref[...] = _dot(pick, _from_tiles(buf.at[s]).astype(BF16))
    y = y_ref[...]
    mod = mod_ref[0]
    z = DEEPNORM_ALPHA * x1_ref[...] + (1.0 + mod[5:6, :]) * y
    o_ref[...] = _layer_norm(z) * ln_ref[0:1, :] + ln_ref[1:2, :]


def _combine(tcnt, lstart, gstart, ys, lpos_t, gates_t, x1, mod, ln2, B, S, tm):
    T, D = x1.shape
    nt = S // tm
    tok = lambda b, j, *_: (b * nt + j, 0)
    grid_spec = pltpu.PrefetchScalarGridSpec(
        num_scalar_prefetch=3,
        grid=(B, nt),
        in_specs=[pl.BlockSpec(memory_space=pl.ANY),
                  pl.BlockSpec((tm, TOP_K), tok),
                  pl.BlockSpec((tm, TOP_K), tok),
                  pl.BlockSpec((tm, D), tok),
                  pl.BlockSpec((1, 6, D), lambda b, j, *_: (b, 0, 0)),
                  pl.BlockSpec((2, D), lambda b, j, *_: (0, 0))],
        out_specs=pl.BlockSpec((tm, D), tok),
        scratch_shapes=[pltpu.VMEM((2, TOP_K * tm * SUBLANES, LANES), F32), pltpu.VMEM((tm, D), F32),
                        pltpu.SemaphoreType.DMA((2,))],
    )
    return pl.pallas_call(
        _combine_kernel,
        grid_spec=grid_spec,
        out_shape=jax.ShapeDtypeStruct((T, D), F32),
        compiler_params=pltpu.CompilerParams(dimension_semantics=("arbitrary", "arbitrary"),
                                             vmem_limit_bytes=VMEM_LIMIT),
        name="combine",
    )(tcnt, lstart, gstart, ys, lpos_t, gates_t, x1, mod, ln2)


def _pad_rows(w, rows):
    return jnp.pad(w, ((0, rows - w.shape[0]), (0, 0)))


def _pad_cols(w, cols):
    return jnp.pad(w, ((0, 0), (0, cols - w.shape[1])))


def _layer(x, c, positions, w_ada, b_ada, w_in, shift_mu, rwkv_w0, rwkv_w2, rwkv_a0, rwkv_a2, rwkv_g2,
           rwkv_k_k, rwkv_k_a, rwkv_r_k, rwkv_ln_w, rwkv_ln_b, attn_sinks, w_out, ln1_g, ln1_b,
           w_router, b_router, w_gate_up, b_gate_up, w_down, b_down, ln2_g, ln2_b):
    B, S, D = x.shape
    T = B * S

    q0, k0, v0 = 0, ATTN_WIDTH, ATTN_WIDTH + KV_WIDTH
    r0 = ATTN_WIDTH + 2 * KV_WIDTH
    heads = lambda base: [w_in[:, base + h * HEAD_DIM: base + (h + 1) * HEAD_DIM] for h in range(N_KV_HEADS)]
    dup = lambda hs: [w for w in hs for _ in range(2)]
    w_attn = jnp.concatenate([w_in[:, q0:q0 + ATTN_WIDTH]] + dup(heads(k0)) + dup(heads(v0)), axis=1).astype(BF16)
    lora0 = r0 + 3 * RWKV_WIDTH
    lora = (DECAY_LORA, AAA_LORA, GATE_LORA)
    pieces_w = [w_in[:, r0:lora0]]
    pieces_mu = [shift_mu[None, 0:3 * RWKV_WIDTH]]
    off = lora0
    for n in lora:
        pieces_w.append(_pad_cols(w_in[:, off:off + n], LANES))
        pieces_mu.append(_pad_cols(shift_mu[None, off - r0:off - r0 + n], LANES))
        off += n
    w_rwkv = jnp.concatenate(pieces_w, axis=1).astype(BF16)
    mu = jnp.concatenate(pieces_mu, axis=1)
    inv_freq = ROPE_THETA ** (-jnp.arange(0, ROT_DIM, 2, dtype=F32) / ROT_DIM)
    lane_p = jnp.arange(LANES) % HEAD_DIM
    n_freq = ROT_DIM // 2
    rot_tab = jnp.zeros((8, LANES), F32)
    rot_tab = rot_tab.at[0].set(jnp.where(lane_p < ROT_DIM, 0.0, 1.0))
    rot_tab = rot_tab.at[1].set(jnp.where(lane_p < n_freq, -1.0, 0.0))
    rot_tab = rot_tab.at[2].set(jnp.where((lane_p >= n_freq) & (lane_p < ROT_DIM), 1.0, 0.0))
    freq_tab = jnp.broadcast_to(inv_freq[:, None], (n_freq, LANES))
    lane_freq = (jnp.arange(n_freq)[:, None] == (lane_p % n_freq)[None, :]) & (lane_p < ROT_DIM)[None, :]
    zeros = jnp.zeros_like(lane_freq)
    expand = jnp.concatenate([jnp.concatenate([lane_freq, zeros], axis=1),
                              jnp.concatenate([zeros, lane_freq], axis=1)], axis=0).astype(BF16)
    vecs = jnp.stack([rwkv_w0, rwkv_a0, rwkv_k_k, rwkv_k_a, rwkv_r_k.reshape(-1), rwkv_ln_w, rwkv_ln_b,
                      jnp.zeros_like(rwkv_w0)])
    w2 = _pad_rows(rwkv_w2, LANES).astype(BF16)
    a2 = _pad_rows(rwkv_a2, LANES).astype(BF16)
    g2 = _pad_rows(rwkv_g2, LANES).astype(BF16)
    wo_a = w_out[:ATTN_WIDTH].astype(BF16)
    wo_r = w_out[ATTN_WIDTH:].astype(BF16)
    w_r_hi = w_router.astype(BF16)
    w_r_lo = (w_router - w_r_hi.astype(F32)).astype(BF16)
    w_r = jnp.concatenate([_pad_cols(w_r_hi, LANES), _pad_cols(w_r_lo, LANES)], axis=1)
    b_r = jnp.concatenate([b_router, jnp.full((LANES - N_EXPERTS,), NEG_INF, F32)])[None, :]

    mod = _mod(c, w_ada, b_ada).reshape(B, 6, D)
    qkv, rw = _inproj(x, positions, mod, w_attn, w_rwkv, mu, rot_tab, freq_tab, expand, min(INPROJ_TILE, S))
    attn_out = _attention(qkv, attn_sinks, B, S, min(ATTN_STEP_BLOCKS, S // ATTN_BLOCK))
    rwkv_out = _rwkv(rw, vecs, w2, a2, g2, B, S, min(RWKV_STEP, S))

    mtile = min(MOE_TILE, S)
    x1, h2, top_i, gates, rank, cnt, tbase = _mix(attn_out, rwkv_out, x.reshape(T, D), mod, wo_a, wo_r,
                                                   jnp.stack([ln1_g, ln1_b]), w_r, b_r, B, S, mtile)

    counts = cnt[:, 0]
    padded = (counts + MOE_BLOCK - 1) // MOE_BLOCK * MOE_BLOCK
    pend = jnp.cumsum(padded)
    pstart = pend - padded
    n_blocks = T * TOP_K // MOE_BLOCK + N_EXPERTS
    blk_row = jnp.arange(n_blocks, dtype=jnp.int32) * MOE_BLOCK
    blk_e = jnp.minimum(jnp.sum((pend[None, :] <= blk_row[:, None]).astype(jnp.int32), axis=1), N_EXPERTS - 1)
    n_used = (pend[-1:] // MOE_BLOCK).astype(jnp.int32)
    tb = tbase[:, :, 0]
    tcnt = jnp.concatenate([tb[1:], counts[None]], axis=0) - tb
    lstart = jnp.cumsum(tcnt, axis=1) - tcnt
    gstart = pstart[None, :] + tb
    shift = jnp.repeat(jnp.transpose(lstart - tb), mtile, axis=1)
    experts = jnp.arange(N_EXPERTS, dtype=jnp.int32)
    lpos = rank + jnp.sum(jnp.where(top_i[None] == experts[:, None, None], shift[:, None, :], 0), axis=0)
    flat = lambda a: a.reshape(-1).astype(jnp.int32)

    xs = _dispatch(flat(tcnt), flat(lstart), flat(gstart), flat(padded - counts), flat(pstart + counts), n_used,
                   lpos, h2, n_blocks * MOE_BLOCK, mtile)
    later_with_rows = (experts[None, :] > experts[:, None]) & (counts[None, :] > 0)
    next_of = jnp.min(jnp.where(later_with_rows, experts[None, :], N_EXPERTS), axis=1)
    next_of = jnp.where(next_of < N_EXPERTS, next_of, -1)
    next_e = jnp.sum(jnp.where(blk_e[:, None] == experts[None, :], next_of[None, :], 0), axis=1).astype(jnp.int32)
    ys = _experts(blk_e, n_used, next_e, xs, w_gate_up, b_gate_up[:, None, :], w_down, b_down[:, None, :])
    out = _combine(flat(tcnt), flat(lstart), flat(gstart), ys, jnp.transpose(lpos), jnp.transpose(gates), x1, mod,
                   jnp.stack([ln2_g, ln2_b]), B, S, mtile)
    return out.reshape(B, S, D)


def kernel(x, c, positions, w_ada, b_ada, w_in, shift_mu, rwkv_w0, rwkv_w2, rwkv_a0, rwkv_a2, rwkv_g2,
           rwkv_k_k, rwkv_k_a, rwkv_r_k, rwkv_ln_w, rwkv_ln_b, attn_sinks, w_out, ln1_g, ln1_b,
           w_router, b_router, w_gate_up, b_gate_up, w_down, b_down, ln2_g, ln2_b):
    for l in range(DEPTH):
        x = _layer(x, c, positions, w_ada[l], b_ada[l], w_in[l], shift_mu[l], rwkv_w0[l], rwkv_w2[l],
                   rwkv_a0[l], rwkv_a2[l], rwkv_g2[l], rwkv_k_k[l], rwkv_k_a[l], rwkv_r_k[l], rwkv_ln_w[l],
                   rwkv_ln_b[l], attn_sinks[l], w_out[l], ln1_g[l], ln1_b[l], w_router[l], b_router[l],
                   w_gate_up[l], b_gate_up[l], w_down[l], b_down[l], ln2_g[l], ln2_b[l])
    return x
```

```python
import functools
import math

import jax
import jax.numpy as jnp
from jax import lax
from jax.experimental import pallas as pl
from jax.experimental.pallas import tpu as pltpu

F32 = jnp.float32
BF16 = jnp.bfloat16

HEAD_DIM = 64
N_ATTN_HEADS = 8
N_KV_HEADS = 2
N_RWKV_HEADS = 8
ATTN_WIDTH = N_ATTN_HEADS * HEAD_DIM
KV_WIDTH = N_KV_HEADS * HEAD_DIM
RWKV_WIDTH = N_RWKV_HEADS * HEAD_DIM
ATTN_BLOCK = 128
ROT_DIM = HEAD_DIM // 4
ROPE_THETA = 500000.0
DECAY_LORA = 32
AAA_LORA = 32
GATE_LORA = 96
N_EXPERTS = 32
TOP_K = 4
SWIGLU_LIMIT = 7.0
SWIGLU_ALPHA = 1.702
LN_EPS = 1e-5
RWKV_GN_EPS = 64e-5
NEG_INF = -1e30
DEPTH = 1
DEEPNORM_ALPHA = (2 * DEPTH) ** 0.25

LANES = 128
RWKV_CHUNK = 64
RWKV_STEP = 512
INPROJ_TILE = 512
ATTN_STEP_BLOCKS = 2
MOE_BLOCK = 512
MOE_TILE = 256
ATTN_PROJ = ATTN_WIDTH + 4 * KV_WIDTH
RWKV_PROJ = 3 * RWKV_WIDTH + 3 * LANES
VMEM_LIMIT = 48 * 1024 * 1024


def _dot(a, b):
    return jnp.dot(a, b, preferred_element_type=F32)


def _dot_nt(a, b):
    return lax.dot_general(a, b, (((1,), (1,)), ((), ())), preferred_element_type=F32)


def _dot_tn(a, b):
    return lax.dot_general(a, b, (((0,), (0,)), ((), ())), preferred_element_type=F32)


def _split3(x):
    h = x.astype(BF16)
    r1 = x - h.astype(F32)
    m = r1.astype(BF16)
    lo = (r1 - m.astype(F32)).astype(BF16)
    return h, m, lo


def _dot_exact_lhs(m_bf16, x):
    h, m, lo = _split3(x)
    return _dot(m_bf16, h) + _dot(m_bf16, m) + _dot(m_bf16, lo)


def _layer_norm(x):
    mu = jnp.mean(x, axis=-1, keepdims=True)
    xc = x - mu
    var = jnp.mean(xc * xc, axis=-1, keepdims=True)
    return xc * lax.rsqrt(var + LN_EPS)


def _sigmoid(x):
    return 1.0 / (1.0 + jnp.exp(-x))


def _mod_kernel(c_ref, w_ref, b_ref, o_ref):
    c = c_ref[...]
    s = c * _sigmoid(c)
    o_ref[...] = jnp.dot(s, w_ref[...], preferred_element_type=F32,
                         precision=lax.Precision.HIGHEST) + b_ref[...]


def _mod(c, w_ada, b_ada):
    B, D = c.shape
    n = w_ada.shape[1] // D
    return pl.pallas_call(
        _mod_kernel,
        grid=(n,),
        in_specs=[pl.BlockSpec((B, D), lambda i: (0, 0)),
                  pl.BlockSpec((D, D), lambda i: (0, i)),
                  pl.BlockSpec((1, D), lambda i: (0, i))],
        out_specs=pl.BlockSpec((B, D), lambda i: (0, i)),
        out_shape=jax.ShapeDtypeStruct((B, n * D), F32),
        compiler_params=pltpu.CompilerParams(dimension_semantics=("arbitrary",),
                                             vmem_limit_bytes=VMEM_LIMIT),
        name="mod",
    )(c, w_ada, b_ada.reshape(1, -1))


def _inproj_kernel(x_ref, pos_ref, mod_ref, wa_ref, wr_ref, mu_ref, rt_ref, fq_ref, ex_ref,
                   qkv_ref, rw_ref, carry_ref):
    j = pl.program_id(1)
    x = x_ref[0]
    tm = x.shape[0]
    mod = mod_ref[0]
    h = _layer_norm(x) * (1.0 + mod[1:2, :]) + mod[0:1, :]
    hb = h.astype(BF16)

    pa = _dot(hb, wa_ref[...])
    ang = pos_ref[0].astype(F32) * fq_ref[:, 0:1]
    pieces = _split3(jnp.concatenate([jnp.cos(ang), jnp.sin(ang)], axis=0))
    trig = _dot_tn(pieces[0], ex_ref[...]) + _dot_tn(pieces[1], ex_ref[...]) + _dot_tn(pieces[2], ex_ref[...])
    cs = trig[:, 0:LANES] + rt_ref[0:1, :]
    sn = trig[:, LANES:2 * LANES]
    m_lo = rt_ref[1:2, :]
    m_hi = rt_ref[2:3, :]
    n_q = ATTN_WIDTH // LANES
    n_rot = (ATTN_WIDTH + 2 * KV_WIDTH) // LANES
    for ch in range(n_rot):
        t = pa[:, ch * LANES:(ch + 1) * LANES]
        if ch < n_q:
            t = t * (1.0 / math.sqrt(HEAD_DIM))
        up = pltpu.roll(t, LANES - ROT_DIM // 2, 1)
        dn = pltpu.roll(t, ROT_DIM // 2, 1)
        o = t * cs + sn * (m_lo * up + m_hi * dn)
        qkv_ref[:, ch * LANES:(ch + 1) * LANES] = o.astype(BF16)
    qkv_ref[:, n_rot * LANES:] = pa[:, n_rot * LANES:].astype(BF16)

    pr = _dot(hb, wr_ref[...])
    prev = pltpu.roll(pr, 1, 0)
    row = lax.broadcasted_iota(jnp.int32, (tm, 1), 0)
    carry = jnp.where(j == 0, 0.0, carry_ref[...])
    prev = jnp.where(row == 0, carry, prev)
    carry_ref[...] = pr[tm - 1:tm, :]
    rw_ref[...] = pr + (prev - pr) * mu_ref[...]


def _inproj(x, positions, mod, w_attn, w_rwkv, mu, rot_tab, freq_tab, expand, tm):
    B, S, D = x.shape
    nt = S // tm
    return pl.pallas_call(
        _inproj_kernel,
        grid=(B, nt),
        in_specs=[pl.BlockSpec((1, tm, D), lambda b, j: (b, j, 0)),
                  pl.BlockSpec((1, 1, tm), lambda b, j: (b, 0, j)),
                  pl.BlockSpec((1, 6, D), lambda b, j: (b, 0, 0)),
                  pl.BlockSpec((D, ATTN_PROJ), lambda b, j: (0, 0)),
                  pl.BlockSpec((D, RWKV_PROJ), lambda b, j: (0, 0)),
                  pl.BlockSpec((1, RWKV_PROJ), lambda b, j: (0, 0)),
                  pl.BlockSpec((8, LANES), lambda b, j: (0, 0)),
                  pl.BlockSpec(freq_tab.shape, lambda b, j: (0, 0)),
                  pl.BlockSpec(expand.shape, lambda b, j: (0, 0))],
        out_specs=[pl.BlockSpec((tm, ATTN_PROJ), lambda b, j: (b * nt + j, 0)),
                   pl.BlockSpec((tm, RWKV_PROJ), lambda b, j: (b * nt + j, 0))],
        out_shape=[jax.ShapeDtypeStruct((B * S, ATTN_PROJ), BF16),
                   jax.ShapeDtypeStruct((B * S, RWKV_PROJ), F32)],
        scratch_shapes=[pltpu.VMEM((1, RWKV_PROJ), F32)],
        compiler_params=pltpu.CompilerParams(dimension_semantics=("arbitrary", "arbitrary"),
                                             vmem_limit_bytes=VMEM_LIMIT),
        name="inproj",
    )(x, positions.reshape(B, 1, S), mod, w_attn, w_rwkv, mu, rot_tab, freq_tab, expand)


def _attn_kernel(q_ref, kc_ref, kp_ref, vc_ref, vp_ref, sink_ref, o_ref):
    i = pl.program_id(1)
    blk = ATTN_BLOCK
    n_sub = q_ref.shape[0] // blk
    qi = lax.broadcasted_iota(jnp.int32, (blk, 2 * blk), 0)
    kj = lax.broadcasted_iota(jnp.int32, (blk, 2 * blk), 1)
    band = (kj > qi) & (kj <= qi + blk)
    first = band & ((kj >= blk) | (i > 0))
    lane = lax.broadcasted_iota(jnp.int32, (1, LANES), 1)
    lo = (lane < HEAD_DIM).astype(BF16)
    hi = (lane >= HEAD_DIM).astype(BF16)
    halves = {}
    for u in range(n_sub):
        for g in range(N_KV_HEADS):
            sl = slice(g * LANES, (g + 1) * LANES)
            prev_k = kp_ref[:, sl] if u == 0 else kc_ref[(u - 1) * blk:u * blk, sl]
            prev_v = vp_ref[:, sl] if u == 0 else vc_ref[(u - 1) * blk:u * blk, sl]
            kcat = jnp.concatenate([prev_k, kc_ref[u * blk:(u + 1) * blk, sl]], axis=0)
            vcat = jnp.concatenate([prev_v, vc_ref[u * blk:(u + 1) * blk, sl]], axis=0)
            halves[u, g] = ((kcat * lo, vcat * lo), (kcat * hi, vcat * hi))
    units = [(u, c, half) for u in range(n_sub) for c in range(ATTN_WIDTH // LANES) for half in range(2)]
    scores = [_dot_nt(q_ref[u * blk:(u + 1) * blk, c * LANES:(c + 1) * LANES], halves[u, c // 2][half][0])
              for u, c, half in units]
    probs, denoms = [], []
    for (u, c, half), s in zip(units, scores):
        sink = sink_ref[2 * c + half]
        s = jnp.where(first if u == 0 else band, s, NEG_INF)
        m = jnp.maximum(jnp.max(s, axis=-1, keepdims=True), sink)
        p = jnp.exp(s - m)
        denoms.append(jnp.sum(p, axis=-1, keepdims=True) + jnp.exp(sink - m))
        probs.append(p.astype(BF16))
    outs = [_dot(p, halves[u, c // 2][half][1]) / d for (u, c, half), p, d in zip(units, probs, denoms)]
    for n, (u, c, half) in enumerate(units):
        if half == 0:
            o_ref[u * blk:(u + 1) * blk, c * LANES:(c + 1) * LANES] = (outs[n] + outs[n + 1]).astype(BF16)


def _attention(qkv, sinks, B, S, n_sub):
    blk = ATTN_BLOCK
    step = n_sub * blk
    ns = S // step
    kcol = ATTN_WIDTH // (2 * KV_WIDTH)
    vcol = kcol + 1
    cur = lambda col: (lambda b, i: (b * ns + i, col))
    prv = lambda col: (lambda b, i: (jnp.maximum((b * ns + i) * n_sub - 1, 0), col))
    return pl.pallas_call(
        _attn_kernel,
        grid=(B, ns),
        in_specs=[pl.BlockSpec((step, ATTN_WIDTH), lambda b, i: (b * ns + i, 0)),
                  pl.BlockSpec((step, 2 * KV_WIDTH), cur(kcol)),
                  pl.BlockSpec((blk, 2 * KV_WIDTH), prv(kcol)),
                  pl.BlockSpec((step, 2 * KV_WIDTH), cur(vcol)),
                  pl.BlockSpec((blk, 2 * KV_WIDTH), prv(vcol)),
                  pl.BlockSpec(memory_space=pltpu.SMEM)],
        out_specs=pl.BlockSpec((step, ATTN_WIDTH), lambda b, i: (b * ns + i, 0)),
        out_shape=jax.ShapeDtypeStruct((B * S, ATTN_WIDTH), BF16),
        compiler_params=pltpu.CompilerParams(dimension_semantics=("arbitrary", "arbitrary"),
                                             vmem_limit_bytes=VMEM_LIMIT),
        name="attn",
    )(qkv, qkv, qkv, qkv, qkv, sinks)


def _rwkv_kernel(rw_ref, vec_ref, w2_ref, a2_ref, g2_ref, o_ref, state_ref, *, n_chunk):
    j = pl.program_id(1)
    C = RWKV_CHUNK
    W = RWKV_WIDTH
    n_pair = W // LANES

    @pl.when(j == 0)
    def _():
        state_ref[...] = jnp.zeros_like(state_ref)

    w0 = vec_ref[0:1, :]
    a0 = vec_ref[1:2, :]
    k_k = vec_ref[2:3, :]
    k_a = vec_ref[3:4, :]
    r_k = vec_ref[4:5, :]
    ln_w = vec_ref[5:6, :]
    ln_b = vec_ref[6:7, :]

    r = rw_ref[:, 0:W]
    k = rw_ref[:, W:2 * W]
    v = rw_ref[:, 2 * W:3 * W]
    wl = rw_ref[:, 3 * W:3 * W + LANES]
    al = rw_ref[:, 3 * W + LANES:3 * W + 2 * LANES]
    gl = rw_ref[:, 3 * W + 2 * LANES:3 * W + 3 * LANES]

    ri = lax.broadcasted_iota(jnp.int32, (LANES, LANES), 0)
    ci = lax.broadcasted_iota(jnp.int32, (LANES, LANES), 1)
    same = (ri // HEAD_DIM) == (ci // HEAD_DIM)
    strict = same & ((ri % HEAD_DIM) > (ci % HEAD_DIM))
    incl = same & ((ri % HEAD_DIM) >= (ci % HEAD_DIM))
    lane = lax.broadcasted_iota(jnp.int32, (1, LANES), 1)
    m0 = (lane < HEAD_DIM).astype(F32)
    m1 = 1.0 - m0
    tri = (lax.broadcasted_iota(jnp.int32, (C, C), 0) >= lax.broadcasted_iota(jnp.int32, (C, C), 1)).astype(BF16)

    def head_sum(xv):
        outs = []
        for p in range(n_pair):
            xp = xv[:, p * LANES:(p + 1) * LANES]
            s0 = jnp.sum(xp * m0, axis=1, keepdims=True)
            s1 = jnp.sum(xp * m1, axis=1, keepdims=True)
            outs.append(s0 * m0 + s1 * m1)
        return jnp.concatenate(outs, axis=1)

    def stack2(xp):
        return jnp.concatenate([xp * m0, xp * m1], axis=0)

    z = w0 + _dot(jnp.tanh(wl).astype(BF16), w2_ref[...])
    lw = -math.exp(-0.5) * _sigmoid(z)
    a = _sigmoid(a0 + _dot(al.astype(BF16), a2_ref[...]))
    g = _dot(_sigmoid(gl).astype(BF16), g2_ref[...])
    kk = k * k_k
    kkn = kk / jnp.maximum(jnp.sqrt(head_sum(kk * kk)), 1e-12)
    k2 = k * (1.0 + (a - 1.0) * k_a)
    av = -kkn
    bv = kkn * a
    bonus = head_sum(r * k2 * r_k) * v

    eye = (ri == ci).astype(F32)
    bf = lambda t: t.astype(BF16)

    pre = []
    for c in range(n_chunk):
        rows = slice(c * C, (c + 1) * C)
        lwc = lw[rows]
        cw = _dot_exact_lhs(tri, lwc)
        cwl = cw[C - 1:C, :]
        e_in = jnp.exp(cw)
        e_neg = jnp.exp(-cw)
        e_rem = jnp.exp(cwl - cw)
        wc = jnp.exp(cwl)
        Rt = r[rows] * e_in
        At = av[rows] * jnp.exp(cw - lwc)
        Bb = bv[rows] * e_neg
        Kb = k2[rows] * e_neg
        Bh = bv[rows] * e_rem
        Kh = k2[rows] * e_rem
        vc = v[rows]
        for p in range(n_pair):
            sl = slice(p * LANES, (p + 1) * LANES)
            pre.append(dict(At=At[:, sl], Rt=Rt[:, sl], Bb=Bb[:, sl], Kb=Kb[:, sl], Bh=Bh[:, sl], Kh=Kh[:, sl],
                            v=vc[:, sl], wc=wc[:, sl]))

    for u in pre:
        u["at_bd"] = stack2(u["At"])
        lhs = bf(jnp.concatenate([u["at_bd"], stack2(u["Rt"])], axis=0))
        rhs = bf(jnp.concatenate([stack2(u["Bb"]), stack2(u["Kb"])], axis=0))
        u["G"] = _dot_nt(lhs, rhs)
    for u in pre:
        G = u.pop("G")
        u["a_ab"] = jnp.where(strict, G[0:2 * C, 0:2 * C], 0.0)
        u["a_ak"] = bf(jnp.where(strict, G[0:2 * C, 2 * C:4 * C], 0.0))
        u["a_rb"] = bf(jnp.where(incl, G[2 * C:4 * C, 0:2 * C], 0.0))
        u["a_rk"] = bf(jnp.where(incl, G[2 * C:4 * C, 2 * C:4 * C], 0.0))
        u["v_bd"] = bf(stack2(u["v"]))
    for u in pre:
        xb = bf(u["a_ab"])
        u["P"] = eye + u.pop("a_ab")
        u["X"] = _dot(xb, xb)
        u["M0"] = _dot(u["a_ak"], u["v_bd"])
    for _ in range(int(math.log2(C)) - 2):
        for u in pre:
            Wm = _dot(bf(u["X"]), bf(jnp.concatenate([u["P"], u["X"]], axis=1)))
            u["P"] = u["P"] + Wm[:, 0:LANES]
            u["X"] = Wm[:, LANES:2 * LANES]
    for u in pre:
        u["P"] = bf(u["P"] + _dot(bf(u.pop("X")), bf(u["P"])))
    for u in pre:
        u["M1"] = _dot(u["P"], bf(u.pop("M0")))
        u["Q"] = bf(_dot(u["a_rb"], u["P"]))
        u["PtB"] = _dot_tn(u["P"], bf(stack2(u["Bh"])))
    for u in pre:
        M1 = u.pop("M1")
        u["Y0"] = _dot(jnp.concatenate([u["a_rb"], u["a_rk"]], axis=1),
                       jnp.concatenate([bf(M1), u["v_bd"]], axis=0))
        u["Tm"] = bf(_dot_tn(bf(u["at_bd"]), bf(u.pop("PtB"))))
        m1_pair = M1[0:C] + M1[C:2 * C]
        cst = _dot_tn(bf(jnp.concatenate([m1_pair, u["v"]], axis=0)),
                      bf(jnp.concatenate([u["Bh"], u["Kh"]], axis=0)))
        u["cst"] = jnp.where(same, cst, 0.0)
        u["ar"] = bf(jnp.concatenate([u["At"], u["Rt"]], axis=0))

    states = [state_ref[p] for p in range(n_pair)]
    for c in range(n_chunk):
        rows = slice(c * C, (c + 1) * C)
        us = pre[c * n_pair:(c + 1) * n_pair]
        sbs = [bf(S) for S in states]
        zs = [_dot_nt(u["ar"], sb) for u, sb in zip(us, sbs)]
        new_states = [S * u["wc"] + _dot(sb, u["Tm"]) + u["cst"] for u, S, sb in zip(us, states, sbs)]
        ybds = [stack2(Z[C:2 * C]) + _dot(u["Q"], bf(stack2(Z[0:C]))) + u["Y0"] for u, Z in zip(us, zs)]
        ys = [y_bd[0:C] + y_bd[C:2 * C] for y_bd in ybds]
        states = new_states
        y = jnp.concatenate(ys, axis=1)
        mu = head_sum(y) * (1.0 / HEAD_DIM)
        yc = y - mu
        var = head_sum(yc * yc) * (1.0 / HEAD_DIM)
        yn = yc * lax.rsqrt(var + RWKV_GN_EPS) * ln_w + ln_b
        o_ref[rows, :] = ((yn + bonus[rows]) * g[rows]).astype(BF16)
    for p in range(n_pair):
        state_ref[p] = states[p]


def _rwkv(rw, vecs, w2, a2, g2, B, S, lb):
    nt = S // lb
    return pl.pallas_call(
        functools.partial(_rwkv_kernel, n_chunk=lb // RWKV_CHUNK),
        grid=(B, nt),
        in_specs=[pl.BlockSpec((lb, RWKV_PROJ), lambda b, j: (b * nt + j, 0)),
                  pl.BlockSpec((8, RWKV_WIDTH), lambda b, j: (0, 0)),
                  pl.BlockSpec((LANES, RWKV_WIDTH), lambda b, j: (0, 0)),
                  pl.BlockSpec((LANES, RWKV_WIDTH), lambda b, j: (0, 0)),
                  pl.BlockSpec((LANES, RWKV_WIDTH), lambda b, j: (0, 0))],
        out_specs=pl.BlockSpec((lb, RWKV_WIDTH), lambda b, j: (b * nt + j, 0)),
        out_shape=jax.ShapeDtypeStruct((B * S, RWKV_WIDTH), BF16),
        scratch_shapes=[pltpu.VMEM((RWKV_WIDTH // LANES, LANES, LANES), F32)],
        compiler_params=pltpu.CompilerParams(dimension_semantics=("arbitrary", "arbitrary"),
                                             vmem_limit_bytes=VMEM_LIMIT),
        name="rwkv",
    )(rw, vecs, w2, a2, g2)


def _mix_kernel(at_ref, rk_ref, x_ref, mod_ref, wo_a_ref, wo_r_ref, ln_ref, wr_ref, br_ref,
                x1_ref, h2_ref, ti_ref, gt_ref, rank_ref, cnt_ref, tb_ref, base_ref):
    first = (pl.program_id(0) == 0) & (pl.program_id(1) == 0)

    @pl.when(first)
    def _():
        base_ref[...] = jnp.zeros_like(base_ref)

    mod = mod_ref[0]
    y = _dot(at_ref[...], wo_a_ref[...]) + _dot(rk_ref[...], wo_r_ref[...])
    x = x_ref[...]
    tm = x.shape[0]
    x1 = _layer_norm(DEEPNORM_ALPHA * x + (1.0 + mod[2:3, :]) * y) * ln_ref[0:1, :] + ln_ref[1:2, :]
    h2 = _layer_norm(x1) * (1.0 + mod[4:5, :]) + mod[3:4, :]
    x1_ref[...] = x1
    h_hi = h2.astype(BF16)
    h2_ref[...] = h_hi

    h_lo = (h2 - h_hi.astype(F32)).astype(BF16)
    part = _dot(h_hi, wr_ref[...])
    logits = part[:, 0:LANES] + part[:, LANES:2 * LANES] + _dot(h_lo, wr_ref[:, 0:LANES]) + br_ref[...]
    lt = jnp.transpose(logits)[0:N_EXPERTS, :]
    erow = lax.broadcasted_iota(jnp.int32, (N_EXPERTS, tm), 0).astype(F32)
    cur = lt
    vals, idxs = [], []
    for _ in range(TOP_K):
        m = jnp.max(cur, axis=0, keepdims=True)
        idx = jnp.min(jnp.where(cur == m, erow, float(N_EXPERTS)), axis=0, keepdims=True)
        vals.append(m)
        idxs.append(idx)
        cur = jnp.where(erow == idx, -jnp.inf, cur)
    tv = jnp.concatenate(vals, axis=0)
    e = jnp.exp(tv - tv[0:1, :])
    gt_ref[...] = e / jnp.sum(e, axis=0, keepdims=True)
    ti_ref[...] = jnp.concatenate(idxs, axis=0).astype(jnp.int32)

    onehot = jnp.zeros((N_EXPERTS, tm), F32)
    for idx in idxs:
        onehot = onehot + (erow == idx).astype(F32)
    before = (lax.broadcasted_iota(jnp.int32, (tm, tm), 0)
              < lax.broadcasted_iota(jnp.int32, (tm, tm), 1)).astype(BF16)
    tot = base_ref[:, 0:1] + _dot(onehot.astype(BF16), before)
    ranks = [jnp.sum(jnp.where(erow == idx, tot, 0.0), axis=0, keepdims=True) for idx in idxs]
    rank_ref[...] = jnp.concatenate(ranks, axis=0).astype(jnp.int32)
    tb_ref[0] = base_ref[...].astype(jnp.int32)
    base_ref[...] = base_ref[...] + jnp.sum(onehot, axis=1, keepdims=True)
    cnt_ref[...] = base_ref[...].astype(jnp.int32)


def _mix(attn_out, rwkv_out, x2d, mod, wo_a, wo_r, ln1, w_router, b_router, B, S, tm):
    D = x2d.shape[1]
    nt = S // tm
    T = B * S
    tok = lambda b, j: (b * nt + j, 0)
    col = lambda b, j: (0, b * nt + j)
    fixed = lambda b, j: (0, 0)
    return pl.pallas_call(
        _mix_kernel,
        grid=(B, nt),
        in_specs=[pl.BlockSpec((tm, ATTN_WIDTH), tok),
                  pl.BlockSpec((tm, RWKV_WIDTH), tok),
                  pl.BlockSpec((tm, D), tok),
                  pl.BlockSpec((1, 6, D), lambda b, j: (b, 0, 0)),
                  pl.BlockSpec((ATTN_WIDTH, D), fixed),
                  pl.BlockSpec((RWKV_WIDTH, D), fixed),
                  pl.BlockSpec((2, D), fixed),
                  pl.BlockSpec((D, 2 * LANES), fixed),
                  pl.BlockSpec((1, LANES), fixed)],
        out_specs=[pl.BlockSpec((tm, D), tok),
                   pl.BlockSpec((tm, D), tok),
                   pl.BlockSpec((TOP_K, tm), col),
                   pl.BlockSpec((TOP_K, tm), col),
                   pl.BlockSpec((TOP_K, tm), col),
                   pl.BlockSpec((N_EXPERTS, LANES), fixed),
                   pl.BlockSpec((1, N_EXPERTS, LANES), lambda b, j: (b * nt + j, 0, 0))],
        out_shape=[jax.ShapeDtypeStruct((T, D), F32),
                   jax.ShapeDtypeStruct((T, D), BF16),
                   jax.ShapeDtypeStruct((TOP_K, T), jnp.int32),
                   jax.ShapeDtypeStruct((TOP_K, T), F32),
                   jax.ShapeDtypeStruct((TOP_K, T), jnp.int32),
                   jax.ShapeDtypeStruct((N_EXPERTS, LANES), jnp.int32),
                   jax.ShapeDtypeStruct((T // tm, N_EXPERTS, LANES), jnp.int32)],
        scratch_shapes=[pltpu.VMEM((N_EXPERTS, LANES), F32)],
        compiler_params=pltpu.CompilerParams(dimension_semantics=("arbitrary", "arbitrary"),
                                             vmem_limit_bytes=VMEM_LIMIT),
        name="mix",
    )(attn_out, rwkv_out, x2d, mod, wo_a, wo_r, ln1, w_router, b_router)


RUN_PIECES = tuple(2 ** b for b in range(int(math.log2(MOE_TILE)), -1, -1))
SUBLANES = 8


def _to_tiles(ref, x):
    n = x.shape[0]
    for c in range(SUBLANES):
        ref[pl.ds(c, n, stride=SUBLANES), :] = x[:, c * LANES:(c + 1) * LANES]


def _from_tiles(ref):
    n = ref.shape[0] // SUBLANES
    return jnp.concatenate([ref[pl.ds(c, n, stride=SUBLANES), :] for c in range(SUBLANES)], axis=1)


def _run_copies(n, local, local_start, remote, remote_start, sem, to_remote):
    off = 0
    for piece in RUN_PIECES:
        take = (n & piece) != 0

        @pl.when(take)
        def _(off=off, piece=piece):
            lo = pl.multiple_of((local_start + off) * SUBLANES, SUBLANES)
            ro = pl.multiple_of((remote_start + off) * SUBLANES, SUBLANES)
            loc = local.at[pl.ds(lo, piece * SUBLANES)]
            rem = remote.at[pl.ds(ro, piece * SUBLANES)]
            src, dst = (loc, rem) if to_remote else (rem, loc)
            pltpu.make_async_copy(src, dst, sem).start()

        off = off + (n & piece)


def _dispatch_kernel(tcnt_ref, lstart_ref, gstart_ref, pad_ref, pad_start_ref, n_used_ref, lpos_ref, h2_ref,
                     xs_ref, xbuf, zbuf, sems):
    i = pl.program_id(0)
    tm = h2_ref.shape[0]
    n_loc = TOP_K * tm
    n_blocks = xs_ref.shape[0] // (MOE_BLOCK * SUBLANES)
    zero_sem = sems.at[2]

    @pl.when(i == 0)
    def _():
        zbuf[...] = jnp.zeros_like(zbuf)

        def zero_pad(e, carry):
            _run_copies(pad_ref[e], zbuf, 0, xs_ref, pad_start_ref[e], zero_sem, True)
            return carry

        def zero_tail(b, carry):
            @pl.when(b >= n_used_ref[0])
            def _():
                start = pl.multiple_of(b * (MOE_BLOCK * SUBLANES), MOE_BLOCK * SUBLANES)
                pltpu.make_async_copy(zbuf, xs_ref.at[pl.ds(start, MOE_BLOCK * SUBLANES)], zero_sem).start()
            return carry

        lax.fori_loop(0, N_EXPERTS, zero_pad, 0)
        lax.fori_loop(n_blocks - N_EXPERTS, n_blocks, zero_tail, 0)

    slot = lax.broadcasted_iota(jnp.int32, (n_loc, tm), 0)
    lpos = lpos_ref[...]
    perm = jnp.zeros((n_loc, tm), F32)
    for k in range(TOP_K):
        perm = perm + (slot == lpos[k:k + 1, :]).astype(F32)
    rows = _dot(perm.astype(BF16), h2_ref[...])

    def wait_tile(s):
        pltpu.make_async_copy(xbuf.at[s], xs_ref.at[pl.ds(0, n_loc * SUBLANES)], sems.at[s]).wait()

    for s in range(2):
        @pl.when((i % 2 == s) & (i >= 2))
        def _(s=s):
            wait_tile(s)

    for s in range(2):
        @pl.when(i % 2 == s)
        def _(s=s):
            _to_tiles(xbuf.at[s], rows)

            def issue(e, carry):
                idx = i * N_EXPERTS + e
                _run_copies(tcnt_ref[idx], xbuf.at[s], lstart_ref[idx], xs_ref, gstart_ref[idx], sems.at[s], True)
                return carry

            lax.fori_loop(0, N_EXPERTS, issue, 0)

    @pl.when(i == pl.num_programs(0) - 1)
    def _():
        for s in range(2):
            @pl.when((i % 2 == s) | (i >= 1))
            def _(s=s):
                wait_tile(s)
        n_zero = N_EXPERTS * MOE_BLOCK * SUBLANES
        pltpu.make_async_copy(xs_ref.at[pl.ds(0, n_zero)], xs_ref.at[pl.ds(0, n_zero)], zero_sem).wait()


def _dispatch(tcnt, lstart, gstart, pad, pad_start, n_used, lpos, h2, n_rows, tm):
    T, D = h2.shape
    grid_spec = pltpu.PrefetchScalarGridSpec(
        num_scalar_prefetch=6,
        grid=(T // tm,),
        in_specs=[pl.BlockSpec((TOP_K, tm), lambda i, *_: (0, i)),
                  pl.BlockSpec((tm, D), lambda i, *_: (i, 0))],
        out_specs=pl.BlockSpec(memory_space=pl.ANY),
        scratch_shapes=[pltpu.VMEM((2, TOP_K * tm * SUBLANES, LANES), F32),
                        pltpu.VMEM((MOE_BLOCK * SUBLANES, LANES), F32),
                        pltpu.SemaphoreType.DMA((3,))],
    )
    return pl.pallas_call(
        _dispatch_kernel,
        grid_spec=grid_spec,
        out_shape=jax.ShapeDtypeStruct((n_rows * SUBLANES, LANES), F32),
        compiler_params=pltpu.CompilerParams(dimension_semantics=("arbitrary",),
                                             vmem_limit_bytes=VMEM_LIMIT),
        name="dispatch",
    )(tcnt, lstart, gstart, pad, pad_start, n_used, lpos, h2)


def _experts_kernel(blk_e_ref, n_used_ref, next_e_ref, xs_ref, wgu_hbm, bgu_ref, wd_hbm, bd_ref, ys_ref,
                    wgu_f32, wd_f32, wgu_bf, wd_bf, sems):
    i = pl.program_id(0)
    d_ff = wd_bf.shape[0]
    used = i < n_used_ref[0]
    e = blk_e_ref[i]
    new_expert = (i == 0) | (e != blk_e_ref[jnp.maximum(i - 1, 0)])

    def weight_copies(ex):
        return (pltpu.make_async_copy(wgu_hbm.at[ex], wgu_f32, sems.at[0]),
                pltpu.make_async_copy(wd_hbm.at[ex], wd_f32, sems.at[1]))

    @pl.when(i == 0)
    def _():
        for cp in weight_copies(e):
            cp.start()

    @pl.when(used & new_expert)
    def _():
        for cp in weight_copies(e):
            cp.wait()
        wgu_bf[...] = wgu_f32[...].astype(BF16)
        wd_bf[...] = wd_f32[...].astype(BF16)
        nxt = next_e_ref[i]

        @pl.when(nxt >= 0)
        def _():
            for cp in weight_copies(nxt):
                cp.start()

    @pl.when(used)
    def _():
        xb = _from_tiles(xs_ref).astype(BF16)
        gu = _dot(xb, wgu_bf[...]) + bgu_ref[0]
        gate = jnp.minimum(gu[:, :d_ff], SWIGLU_LIMIT)
        up = jnp.clip(gu[:, d_ff:], -SWIGLU_LIMIT, SWIGLU_LIMIT)
        act = (up + 1.0) * (gate * _sigmoid(SWIGLU_ALPHA * gate))
        _to_tiles(ys_ref, _dot(act.astype(BF16), wd_bf[...]) + bd_ref[0])

    @pl.when(i >= n_used_ref[0])
    def _():
        ys_ref[...] = jnp.zeros_like(ys_ref)


def _experts(blk_e, n_used, next_e, xs, wgu, bgu, wd, bd):
    d_ff, D = wd.shape[1], wd.shape[2]
    n_blocks = xs.shape[0] // (MOE_BLOCK * SUBLANES)
    blk = (MOE_BLOCK * SUBLANES, LANES)

    def last_used(i, n_used_ref):
        return jnp.minimum(i, jnp.maximum(n_used_ref[0] - 1, 0))

    def row_map(i, blk_e_ref, n_used_ref, next_e_ref):
        return (last_used(i, n_used_ref), 0)

    def exp_map(i, blk_e_ref, n_used_ref, next_e_ref):
        return (blk_e_ref[last_used(i, n_used_ref)], 0, 0)

    grid_spec = pltpu.PrefetchScalarGridSpec(
        num_scalar_prefetch=3,
        grid=(n_blocks,),
        in_specs=[pl.BlockSpec(blk, row_map),
                  pl.BlockSpec(memory_space=pl.ANY),
                  pl.BlockSpec((1, 1, 2 * d_ff), exp_map),
                  pl.BlockSpec(memory_space=pl.ANY),
                  pl.BlockSpec((1, 1, D), exp_map)],
        out_specs=pl.BlockSpec(blk, lambda i, *_: (i, 0)),
        scratch_shapes=[pltpu.VMEM((D, 2 * d_ff), F32), pltpu.VMEM((d_ff, D), F32),
                        pltpu.VMEM((D, 2 * d_ff), BF16), pltpu.VMEM((d_ff, D), BF16),
                        pltpu.SemaphoreType.DMA((2,))],
    )
    return pl.pallas_call(
        _experts_kernel,
        grid_spec=grid_spec,
        out_shape=jax.ShapeDtypeStruct(xs.shape, F32),
        compiler_params=pltpu.CompilerParams(dimension_semantics=("arbitrary",),
                                             vmem_limit_bytes=VMEM_LIMIT),
        name="experts",
    )(blk_e, n_used, next_e, xs, wgu, bgu, wd, bd)


def _combine_kernel(tcnt_ref, lstart_ref, gstart_ref, ys_ref, lpos_ref, gt_ref, x1_ref, mod_ref, ln_ref,
                    o_ref, buf, y_ref, sems):
    nt = pl.num_programs(1)
    n_tiles = pl.num_programs(0) * nt
    i = pl.program_id(0) * nt + pl.program_id(1)
    tm = x1_ref.shape[0]
    n_loc = TOP_K * tm

    def fetch(tile, s):
        def issue(e, carry):
            idx = tile * N_EXPERTS + e
            _run_copies(tcnt_ref[idx], buf.at[s], lstart_ref[idx], ys_ref, gstart_ref[idx], sems.at[s], False)
            return carry

        lax.fori_loop(0, N_EXPERTS, issue, 0)

    @pl.when(i == 0)
    def _():
        fetch(i, 0)

    for s in range(2):
        @pl.when(((i + 1) % 2 == s) & (i + 1 < n_tiles))
        def _(s=s):
            fetch(i + 1, s)

    slot = lax.broadcasted_iota(jnp.int32, (tm, n_loc), 1)
    lpos = lpos_ref[...]
    gt = gt_ref[...]
    pick = jnp.zeros((tm, n_loc), F32)
    for k in range(TOP_K):
        pick = pick + jnp.where(slot == lpos[:, k:k + 1], gt[:, k:k + 1], 0.0)
    pick = pick.astype(BF16)
    for s in range(2):
        @pl.when(i % 2 == s)
        def _(s=s):
            pltpu.make_async_copy(ys_ref.at[pl.ds(0, n_loc * SUBLANES)], buf.at[s], sems.at[s]).wait()
            y_ref[...] = _dot(pick, _from_tiles(buf.at[s]).astype(BF16))
    y = y_ref[...]
    mod = mod_ref[0]
    z = DEEPNORM_ALPHA * x1_ref[...] + (1.0 + mod[5:6, :]) * y
    o_ref[...] = _layer_norm(z) * ln_ref[0:1, :] + ln_ref[1:2, :]


def _combine(tcnt, lstart, gstart, ys, lpos_t, gates_t, x1, mod, ln2, B, S, tm):
    T, D = x1.shape
    nt = S // tm
    tok = lambda b, j, *_: (b * nt + j, 0)
    grid_spec = pltpu.PrefetchScalarGridSpec(
        num_scalar_prefetch=3,
        grid=(B, nt),
        in_specs=[pl.BlockSpec(memory_space=pl.ANY),
                  pl.BlockSpec((tm, TOP_K), tok),
                  pl.BlockSpec((tm, TOP_K), tok),
                  pl.BlockSpec((tm, D), tok),
                  pl.BlockSpec((1, 6, D), lambda b, j, *_: (b, 0, 0)),
                  pl.BlockSpec((2, D), lambda b, j, *_: (0, 0))],
        out_specs=pl.BlockSpec((tm, D), tok),
        scratch_shapes=[pltpu.VMEM((2, TOP_K * tm * SUBLANES, LANES), F32), pltpu.VMEM((tm, D), F32),
                        pltpu.SemaphoreType.DMA((2,))],
    )
    return pl.pallas_call(
        _combine_kernel,
        grid_spec=grid_spec,
        out_shape=jax.ShapeDtypeStruct((T, D), F32),
        compiler_params=pltpu.CompilerParams(dimension_semantics=("arbitrary", "arbitrary"),
                                             vmem_limit_bytes=VMEM_LIMIT),
        name="combine",
    )(tcnt, lstart, gstart, ys, lpos_t, gates_t, x1, mod, ln2)


def _pad_rows(w, rows):
    return jnp.pad(w, ((0, rows - w.shape[0]), (0, 0)))


def _pad_cols(w, cols):
    return jnp.pad(w, ((0, 0), (0, cols - w.shape[1])))


def _layer(x, c, positions, w_ada, b_ada, w_in, shift_mu, rwkv_w0, rwkv_w2, rwkv_a0, rwkv_a2, rwkv_g2,
           rwkv_k_k, rwkv_k_a, rwkv_r_k, rwkv_ln_w, rwkv_ln_b, attn_sinks, w_out, ln1_g, ln1_b,
           w_router, b_router, w_gate_up, b_gate_up, w_down, b_down, ln2_g, ln2_b):
    B, S, D = x.shape
    T = B * S

    q0, k0, v0 = 0, ATTN_WIDTH, ATTN_WIDTH + KV_WIDTH
    r0 = ATTN_WIDTH + 2 * KV_WIDTH
    heads = lambda base: [w_in[:, base + h * HEAD_DIM: base + (h + 1) * HEAD_DIM] for h in range(N_KV_HEADS)]
    dup = lambda hs: [w for w in hs for _ in range(2)]
    w_attn = jnp.concatenate([w_in[:, q0:q0 + ATTN_WIDTH]] + dup(heads(k0)) + dup(heads(v0)), axis=1).astype(BF16)
    lora0 = r0 + 3 * RWKV_WIDTH
    lora = (DECAY_LORA, AAA_LORA, GATE_LORA)
    pieces_w = [w_in[:, r0:lora0]]
    pieces_mu = [shift_mu[None, 0:3 * RWKV_WIDTH]]
    off = lora0
    for n in lora:
        pieces_w.append(_pad_cols(w_in[:, off:off + n], LANES))
        pieces_mu.append(_pad_cols(shift_mu[None, off - r0:off - r0 + n], LANES))
        off += n
    w_rwkv = jnp.concatenate(pieces_w, axis=1).astype(BF16)
    mu = jnp.concatenate(pieces_mu, axis=1)
    inv_freq = ROPE_THETA ** (-jnp.arange(0, ROT_DIM, 2, dtype=F32) / ROT_DIM)
    lane_p = jnp.arange(LANES) % HEAD_DIM
    n_freq = ROT_DIM // 2
    rot_tab = jnp.zeros((8, LANES), F32)
    rot_tab = rot_tab.at[0].set(jnp.where(lane_p < ROT_DIM, 0.0, 1.0))
    rot_tab = rot_tab.at[1].set(jnp.where(lane_p < n_freq, -1.0, 0.0))
    rot_tab = rot_tab.at[2].set(jnp.where((lane_p >= n_freq) & (lane_p < ROT_DIM), 1.0, 0.0))
    freq_tab = jnp.broadcast_to(inv_freq[:, None], (n_freq, LANES))
    lane_freq = (jnp.arange(n_freq)[:, None] == (lane_p % n_freq)[None, :]) & (lane_p < ROT_DIM)[None, :]
    zeros = jnp.zeros_like(lane_freq)
    expand = jnp.concatenate([jnp.concatenate([lane_freq, zeros], axis=1),
                              jnp.concatenate([zeros, lane_freq], axis=1)], axis=0).astype(BF16)
    vecs = jnp.stack([rwkv_w0, rwkv_a0, rwkv_k_k, rwkv_k_a, rwkv_r_k.reshape(-1), rwkv_ln_w, rwkv_ln_b,
                      jnp.zeros_like(rwkv_w0)])
    w2 = _pad_rows(rwkv_w2, LANES).astype(BF16)
    a2 = _pad_rows(rwkv_a2, LANES).astype(BF16)
    g2 = _pad_rows(rwkv_g2, LANES).astype(BF16)
    wo_a = w_out[:ATTN_WIDTH].astype(BF16)
    wo_r = w_out[ATTN_WIDTH:].astype(BF16)
    w_r_hi = w_router.astype(BF16)
    w_r_lo = (w_router - w_r_hi.astype(F32)).astype(BF16)
    w_r = jnp.concatenate([_pad_cols(w_r_hi, LANES), _pad_cols(w_r_lo, LANES)], axis=1)
    b_r = jnp.concatenate([b_router, jnp.full((LANES - N_EXPERTS,), NEG_INF, F32)])[None, :]

    mod = _mod(c, w_ada, b_ada).reshape(B, 6, D)
    qkv, rw = _inproj(x, positions, mod, w_attn, w_rwkv, mu, rot_tab, freq_tab, expand, min(INPROJ_TILE, S))
    attn_out = _attention(qkv, attn_sinks, B, S, min(ATTN_STEP_BLOCKS, S // ATTN_BLOCK))
    rwkv_out = _rwkv(rw, vecs, w2, a2, g2, B, S, min(RWKV_STEP, S))

    mtile = min(MOE_TILE, S)
    x1, h2, top_i, gates, rank, cnt, tbase = _mix(attn_out, rwkv_out, x.reshape(T, D), mod, wo_a, wo_r,
                                                   jnp.stack([ln1_g, ln1_b]), w_r, b_r, B, S, mtile)

    counts = cnt[:, 0]
    padded = (counts + MOE_BLOCK - 1) // MOE_BLOCK * MOE_BLOCK
    pend = jnp.cumsum(padded)
    pstart = pend - padded
    n_blocks = T * TOP_K // MOE_BLOCK + N_EXPERTS
    blk_row = jnp.arange(n_blocks, dtype=jnp.int32) * MOE_BLOCK
    blk_e = jnp.minimum(jnp.sum((pend[None, :] <= blk_row[:, None]).astype(jnp.int32), axis=1), N_EXPERTS - 1)
    n_used = (pend[-1:] // MOE_BLOCK).astype(jnp.int32)
    tb = tbase[:, :, 0]
    tcnt = jnp.concatenate([tb[1:], counts[None]], axis=0) - tb
    lstart = jnp.cumsum(tcnt, axis=1) - tcnt
    gstart = pstart[None, :] + tb
    shift = jnp.repeat(jnp.transpose(lstart - tb), mtile, axis=1)
    experts = jnp.arange(N_EXPERTS, dtype=jnp.int32)
    lpos = rank + jnp.sum(jnp.where(top_i[None] == experts[:, None, None], shift[:, None, :], 0), axis=0)
    flat = lambda a: a.reshape(-1).astype(jnp.int32)

    xs = _dispatch(flat(tcnt), flat(lstart), flat(gstart), flat(padded - counts), flat(pstart + counts), n_used,
                   lpos, h2, n_blocks * MOE_BLOCK, mtile)
    later_with_rows = (experts[None, :] > experts[:, None]) & (counts[None, :] > 0)
    next_of = jnp.min(jnp.where(later_with_rows, experts[None, :], N_EXPERTS), axis=1)
    next_of = jnp.where(next_of < N_EXPERTS, next_of, -1)
    next_e = jnp.sum(jnp.where(blk_e[:, None] == experts[None, :], next_of[None, :], 0), axis=1).astype(jnp.int32)
    ys = _experts(blk_e, n_used, next_e, xs, w_gate_up, b_gate_up[:, None, :], w_down, b_down[:, None, :])
    out = _combine(flat(tcnt), flat(lstart), flat(gstart), ys, jnp.transpose(lpos), jnp.transpose(gates), x1, mod,
                   jnp.stack([ln2_g, ln2_b]), B, S, mtile)
    return out.reshape(B, S, D)


def kernel(x, c, positions, w_ada, b_ada, w_in, shift_mu, rwkv_w0, rwkv_w2, rwkv_a0, rwkv_a2, rwkv_g2,
           rwkv_k_k, rwkv_k_a, rwkv_r_k, rwkv_ln_w, rwkv_ln_b, attn_sinks, w_out, ln1_g, ln1_b,
           w_router, b_router, w_gate_up, b_gate_up, w_down, b_down, ln2_g, ln2_b):
    for l in range(DEPTH):
        x = _layer(x, c, positions, w_ada[l], b_ada[l], w_in[l], shift_mu[l], rwkv_w0[l], rwkv_w2[l],
                   rwkv_a0[l], rwkv_a2[l], rwkv_g2[l], rwkv_k_k[l], rwkv_k_a[l], rwkv_r_k[l], rwkv_ln_w[l],
                   rwkv_ln_b[l], attn_sinks[l], w_out[l], ln1_g[l], ln1_b[l], w_router[l], b_router[l],
                   w_gate_up[l], b_gate_up[l], w_down[l], b_down[l], ln2_g[l], ln2_b[l])
    return x
```

```python
import functools
import math

import jax
import jax.numpy as jnp
from jax import lax
from jax.experimental import pallas as pl
from jax.experimental.pallas import tpu as pltpu

F32 = jnp.float32
BF16 = jnp.bfloat16

HEAD_DIM = 64
N_ATTN_HEADS = 8
N_KV_HEADS = 2
N_RWKV_HEADS = 8
ATTN_WIDTH = N_ATTN_HEADS * HEAD_DIM
KV_WIDTH = N_KV_HEADS * HEAD_DIM
RWKV_WIDTH = N_RWKV_HEADS * HEAD_DIM
ATTN_BLOCK = 128
ROT_DIM = HEAD_DIM // 4
ROPE_THETA = 500000.0
DECAY_LORA = 32
AAA_LORA = 32
GATE_LORA = 96
N_EXPERTS = 32
TOP_K = 4
SWIGLU_LIMIT = 7.0
SWIGLU_ALPHA = 1.702
LN_EPS = 1e-5
RWKV_GN_EPS = 64e-5
NEG_INF = -1e30
DEPTH = 1
DEEPNORM_ALPHA = (2 * DEPTH) ** 0.25

LANES = 128
RWKV_CHUNK = 64
RWKV_STEP = 512
INPROJ_TILE = 512
ATTN_STEP_BLOCKS = 2
MOE_BLOCK = 512
MOE_TILE = 256
ATTN_PROJ = ATTN_WIDTH + 4 * KV_WIDTH
RWKV_PROJ = 3 * RWKV_WIDTH + 3 * LANES
VMEM_LIMIT = 48 * 1024 * 1024


def _dot(a, b):
    return jnp.dot(a, b, preferred_element_type=F32)


def _dot_nt(a, b):
    return lax.dot_general(a, b, (((1,), (1,)), ((), ())), preferred_element_type=F32)


def _dot_tn(a, b):
    return lax.dot_general(a, b, (((0,), (0,)), ((), ())), preferred_element_type=F32)


def _split3(x):
    h = x.astype(BF16)
    r1 = x - h.astype(F32)
    m = r1.astype(BF16)
    lo = (r1 - m.astype(F32)).astype(BF16)
    return h, m, lo


def _dot_exact_lhs(m_bf16, x):
    h, m, lo = _split3(x)
    return _dot(m_bf16, h) + _dot(m_bf16, m) + _dot(m_bf16, lo)


def _layer_norm(x):
    mu = jnp.mean(x, axis=-1, keepdims=True)
    xc = x - mu
    var = jnp.mean(xc * xc, axis=-1, keepdims=True)
    return xc * lax.rsqrt(var + LN_EPS)


def _sigmoid(x):
    return 1.0 / (1.0 + jnp.exp(-x))


def _mod_kernel(c_ref, w_ref, b_ref, o_ref):
    c = c_ref[...]
    s = c * _sigmoid(c)
    o_ref[...] = jnp.dot(s, w_ref[...], preferred_element_type=F32,
                         precision=lax.Precision.HIGHEST) + b_ref[...]


def _mod(c, w_ada, b_ada):
    B, D = c.shape
    n = w_ada.shape[1] // D
    return pl.pallas_call(
        _mod_kernel,
        grid=(n,),
        in_specs=[pl.BlockSpec((B, D), lambda i: (0, 0)),
                  pl.BlockSpec((D, D), lambda i: (0, i)),
                  pl.BlockSpec((1, D), lambda i: (0, i))],
        out_specs=pl.BlockSpec((B, D), lambda i: (0, i)),
        out_shape=jax.ShapeDtypeStruct((B, n * D), F32),
        compiler_params=pltpu.CompilerParams(dimension_semantics=("arbitrary",),
                                             vmem_limit_bytes=VMEM_LIMIT),
        name="mod",
    )(c, w_ada, b_ada.reshape(1, -1))


def _inproj_kernel(x_ref, pos_ref, mod_ref, wa_ref, wr_ref, mu_ref, rt_ref, fq_ref, ex_ref,
                   qkv_ref, rw_ref, carry_ref):
    j = pl.program_id(1)
    x = x_ref[0]
    tm = x.shape[0]
    mod = mod_ref[0]
    h = _layer_norm(x) * (1.0 + mod[1:2, :]) + mod[0:1, :]
    hb = h.astype(BF16)

    pa = _dot(hb, wa_ref[...])
    ang = pos_ref[0].astype(F32) * fq_ref[:, 0:1]
    pieces = _split3(jnp.concatenate([jnp.cos(ang), jnp.sin(ang)], axis=0))
    trig = _dot_tn(pieces[0], ex_ref[...]) + _dot_tn(pieces[1], ex_ref[...]) + _dot_tn(pieces[2], ex_ref[...])
    cs = trig[:, 0:LANES] + rt_ref[0:1, :]
    sn = trig[:, LANES:2 * LANES]
    m_lo = rt_ref[1:2, :]
    m_hi = rt_ref[2:3, :]
    n_q = ATTN_WIDTH // LANES
    n_rot = (ATTN_WIDTH + 2 * KV_WIDTH) // LANES
    for ch in range(n_rot):
        t = pa[:, ch * LANES:(ch + 1) * LANES]
        if ch < n_q:
            t = t * (1.0 / math.sqrt(HEAD_DIM))
        up = pltpu.roll(t, LANES - ROT_DIM // 2, 1)
        dn = pltpu.roll(t, ROT_DIM // 2, 1)
        o = t * cs + sn * (m_lo * up + m_hi * dn)
        qkv_ref[:, ch * LANES:(ch + 1) * LANES] = o.astype(BF16)
    qkv_ref[:, n_rot * LANES:] = pa[:, n_rot * LANES:].astype(BF16)

    pr = _dot(hb, wr_ref[...])
    prev = pltpu.roll(pr, 1, 0)
    row = lax.broadcasted_iota(jnp.int32, (tm, 1), 0)
    carry = jnp.where(j == 0, 0.0, carry_ref[...])
    prev = jnp.where(row == 0, carry, prev)
    carry_ref[...] = pr[tm - 1:tm, :]
    rw_ref[...] = pr + (prev - pr) * mu_ref[...]


def _inproj(x, positions, mod, w_attn, w_rwkv, mu, rot_tab, freq_tab, expand, tm):
    B, S, D = x.shape
    nt = S // tm
    return pl.pallas_call(
        _inproj_kernel,
        grid=(B, nt),
        in_specs=[pl.BlockSpec((1, tm, D), lambda b, j: (b, j, 0)),
                  pl.BlockSpec((1, 1, tm), lambda b, j: (b, 0, j)),
                  pl.BlockSpec((1, 6, D), lambda b, j: (b, 0, 0)),
                  pl.BlockSpec((D, ATTN_PROJ), lambda b, j: (0, 0)),
                  pl.BlockSpec((D, RWKV_PROJ), lambda b, j: (0, 0)),
                  pl.BlockSpec((1, RWKV_PROJ), lambda b, j: (0, 0)),
                  pl.BlockSpec((8, LANES), lambda b, j: (0, 0)),
                  pl.BlockSpec(freq_tab.shape, lambda b, j: (0, 0)),
                  pl.BlockSpec(expand.shape, lambda b, j: (0, 0))],
        out_specs=[pl.BlockSpec((tm, ATTN_PROJ), lambda b, j: (b * nt + j, 0)),
                   pl.BlockSpec((tm, RWKV_PROJ), lambda b, j: (b * nt + j, 0))],
        out_shape=[jax.ShapeDtypeStruct((B * S, ATTN_PROJ), BF16),
                   jax.ShapeDtypeStruct((B * S, RWKV_PROJ), F32)],
        scratch_shapes=[pltpu.VMEM((1, RWKV_PROJ), F32)],
        compiler_params=pltpu.CompilerParams(dimension_semantics=("arbitrary", "arbitrary"),
                                             vmem_limit_bytes=VMEM_LIMIT),
        name="inproj",
    )(x, positions.reshape(B, 1, S), mod, w_attn, w_rwkv, mu, rot_tab, freq_tab, expand)


def _attn_kernel(q_ref, kc_ref, kp_ref, vc_ref, vp_ref, sink_ref, o_ref):
    i = pl.program_id(1)
    blk = ATTN_BLOCK
    n_sub = q_ref.shape[0] // blk
    qi = lax.broadcasted_iota(jnp.int32, (blk, 2 * blk), 0)
    kj = lax.broadcasted_iota(jnp.int32, (blk, 2 * blk), 1)
    band = (kj > qi) & (kj <= qi + blk)
    first = band & ((kj >= blk) | (i > 0))
    lane = lax.broadcasted_iota(jnp.int32, (1, LANES), 1)
    lo = (lane < HEAD_DIM).astype(BF16)
    hi = (lane >= HEAD_DIM).astype(BF16)
    halves = {}
    for u in range(n_sub):
        for g in range(N_KV_HEADS):
            sl = slice(g * LANES, (g + 1) * LANES)
            prev_k = kp_ref[:, sl] if u == 0 else kc_ref[(u - 1) * blk:u * blk, sl]
            prev_v = vp_ref[:, sl] if u == 0 else vc_ref[(u - 1) * blk:u * blk, sl]
            kcat = jnp.concatenate([prev_k, kc_ref[u * blk:(u + 1) * blk, sl]], axis=0)
            vcat = jnp.concatenate([prev_v, vc_ref[u * blk:(u + 1) * blk, sl]], axis=0)
            halves[u, g] = ((kcat * lo, vcat * lo), (kcat * hi, vcat * hi))
    units = [(u, c, half) for u in range(n_sub) for c in range(ATTN_WIDTH // LANES) for half in range(2)]
    scores = [_dot_nt(q_ref[u * blk:(u + 1) * blk, c * LANES:(c + 1) * LANES], halves[u, c // 2][half][0])
              for u, c, half in units]
    probs, denoms = [], []
    for (u, c, half), s in zip(units, scores):
        sink = sink_ref[2 * c + half]
        s = jnp.where(first if u == 0 else band, s, NEG_INF)
        m = jnp.maximum(jnp.max(s, axis=-1, keepdims=True), sink)
        p = jnp.exp(s - m)
        denoms.append(jnp.sum(p, axis=-1, keepdims=True) + jnp.exp(sink - m))
        probs.append(p.astype(BF16))
    outs = [_dot(p, halves[u, c // 2][half][1]) / d for (u, c, half), p, d in zip(units, probs, denoms)]
    for n, (u, c, half) in enumerate(units):
        if half == 0:
            o_ref[u * blk:(u + 1) * blk, c * LANES:(c + 1) * LANES] = (outs[n] + outs[n + 1]).astype(BF16)


def _attention(qkv, sinks, B, S, n_sub):
    blk = ATTN_BLOCK
    step = n_sub * blk
    ns = S // step
    kcol = ATTN_WIDTH // (2 * KV_WIDTH)
    vcol = kcol + 1
    cur = lambda col: (lambda b, i: (b * ns + i, col))
    prv = lambda col: (lambda b, i: (jnp.maximum((b * ns + i) * n_sub - 1, 0), col))
    return pl.pallas_call(
        _attn_kernel,
        grid=(B, ns),
        in_specs=[pl.BlockSpec((step, ATTN_WIDTH), lambda b, i: (b * ns + i, 0)),
                  pl.BlockSpec((step, 2 * KV_WIDTH), cur(kcol)),
                  pl.BlockSpec((blk, 2 * KV_WIDTH), prv(kcol)),
                  pl.BlockSpec((step, 2 * KV_WIDTH), cur(vcol)),
                  pl.BlockSpec((blk, 2 * KV_WIDTH), prv(vcol)),
                  pl.BlockSpec(memory_space=pltpu.SMEM)],
        out_specs=pl.BlockSpec((step, ATTN_WIDTH), lambda b, i: (b * ns + i, 0)),
        out_shape=jax.ShapeDtypeStruct((B * S, ATTN_WIDTH), BF16),
        compiler_params=pltpu.CompilerParams(dimension_semantics=("arbitrary", "arbitrary"),
                                             vmem_limit_bytes=VMEM_LIMIT),
        name="attn",
    )(qkv, qkv, qkv, qkv, qkv, sinks)


def _rwkv_kernel(rw_ref, vec_ref, w2_ref, a2_ref, g2_ref, o_ref, state_ref, *, n_chunk):
    j = pl.program_id(1)
    C = RWKV_CHUNK
    W = RWKV_WIDTH
    n_pair = W // LANES

    @pl.when(j == 0)
    def _():
        state_ref[...] = jnp.zeros_like(state_ref)

    w0 = vec_ref[0:1, :]
    a0 = vec_ref[1:2, :]
    k_k = vec_ref[2:3, :]
    k_a = vec_ref[3:4, :]
    r_k = vec_ref[4:5, :]
    ln_w = vec_ref[5:6, :]
    ln_b = vec_ref[6:7, :]

    r = rw_ref[:, 0:W]
    k = rw_ref[:, W:2 * W]
    v = rw_ref[:, 2 * W:3 * W]
    wl = rw_ref[:, 3 * W:3 * W + LANES]
    al = rw_ref[:, 3 * W + LANES:3 * W + 2 * LANES]
    gl = rw_ref[:, 3 * W + 2 * LANES:3 * W + 3 * LANES]

    ri = lax.broadcasted_iota(jnp.int32, (LANES, LANES), 0)
    ci = lax.broadcasted_iota(jnp.int32, (LANES, LANES), 1)
    same = (ri // HEAD_DIM) == (ci // HEAD_DIM)
    strict = same & ((ri % HEAD_DIM) > (ci % HEAD_DIM))
    incl = same & ((ri % HEAD_DIM) >= (ci % HEAD_DIM))
    lane = lax.broadcasted_iota(jnp.int32, (1, LANES), 1)
    m0 = (lane < HEAD_DIM).astype(F32)
    m1 = 1.0 - m0
    tri = (lax.broadcasted_iota(jnp.int32, (C, C), 0) >= lax.broadcasted_iota(jnp.int32, (C, C), 1)).astype(BF16)

    def head_sum(xv):
        outs = []
        for p in range(n_pair):
            xp = xv[:, p * LANES:(p + 1) * LANES]
            s0 = jnp.sum(xp * m0, axis=1, keepdims=True)
            s1 = jnp.sum(xp * m1, axis=1, keepdims=True)
            outs.append(s0 * m0 + s1 * m1)
        return jnp.concatenate(outs, axis=1)

    def stack2(xp):
        return jnp.concatenate([xp * m0, xp * m1], axis=0)

    z = w0 + _dot(jnp.tanh(wl).astype(BF16), w2_ref[...])
    lw = -math.exp(-0.5) * _sigmoid(z)
    a = _sigmoid(a0 + _dot(al.astype(BF16), a2_ref[...]))
    g = _dot(_sigmoid(gl).astype(BF16), g2_ref[...])
    kk = k * k_k
    kkn = kk / jnp.maximum(jnp.sqrt(head_sum(kk * kk)), 1e-12)
    k2 = k * (1.0 + (a - 1.0) * k_a)
    av = -kkn
    bv = kkn * a
    bonus = head_sum(r * k2 * r_k) * v

    eye = (ri == ci).astype(F32)
    bf = lambda t: t.astype(BF16)

    pre = []
    for c in range(n_chunk):
        rows = slice(c * C, (c + 1) * C)
        lwc = lw[rows]
        cw = _dot_exact_lhs(tri, lwc)
        cwl = cw[C - 1:C, :]
        e_in = jnp.exp(cw)
        e_neg = jnp.exp(-cw)
        e_rem = jnp.exp(cwl - cw)
        wc = jnp.exp(cwl)
        Rt = r[rows] * e_in
        At = av[rows] * jnp.exp(cw - lwc)
        Bb = bv[rows] * e_neg
        Kb = k2[rows] * e_neg
        Bh = bv[rows] * e_rem
        Kh = k2[rows] * e_rem
        vc = v[rows]
        for p in range(n_pair):
            sl = slice(p * LANES, (p + 1) * LANES)
            pre.append(dict(At=At[:, sl], Rt=Rt[:, sl], Bb=Bb[:, sl], Kb=Kb[:, sl], Bh=Bh[:, sl], Kh=Kh[:, sl],
                            v=vc[:, sl], wc=wc[:, sl]))

    for u in pre:
        u["at_bd"] = stack2(u["At"])
        lhs = bf(jnp.concatenate([u["at_bd"], stack2(u["Rt"])], axis=0))
        rhs = bf(jnp.concatenate([stack2(u["Bb"]), stack2(u["Kb"])], axis=0))
        u["G"] = _dot_nt(lhs, rhs)
    for u in pre:
        G = u.pop("G")
        u["a_ab"] = jnp.where(strict, G[0:2 * C, 0:2 * C], 0.0)
        u["a_ak"] = bf(jnp.where(strict, G[0:2 * C, 2 * C:4 * C], 0.0))
        u["a_rb"] = bf(jnp.where(incl, G[2 * C:4 * C, 0:2 * C], 0.0))
        u["a_rk"] = bf(jnp.where(incl, G[2 * C:4 * C, 2 * C:4 * C], 0.0))
        u["v_bd"] = bf(stack2(u["v"]))
    for u in pre:
        xb = bf(u["a_ab"])
        u["P"] = eye + u.pop("a_ab")
        u["X"] = _dot(xb, xb)
        u["M0"] = _dot(u["a_ak"], u["v_bd"])
    for _ in range(int(math.log2(C)) - 2):
        for u in pre:
            Wm = _dot(bf(u["X"]), bf(jnp.concatenate([u["P"], u["X"]], axis=1)))
            u["P"] = u["P"] + Wm[:, 0:LANES]
            u["X"] = Wm[:, LANES:2 * LANES]
    for u in pre:
        u["P"] = bf(u["P"] + _dot(bf(u.pop("X")), bf(u["P"])))
    for u in pre:
        u["M1"] = _dot(u["P"], bf(u.pop("M0")))
        u["Q"] = bf(_dot(u["a_rb"], u["P"]))
        u["PtB"] = _dot_tn(u["P"], bf(stack2(u["Bh"])))
    for u in pre:
        M1 = u.pop("M1")
        u["Y0"] = _dot(jnp.concatenate([u["a_rb"], u["a_rk"]], axis=1),
                       jnp.concatenate([bf(M1), u["v_bd"]], axis=0))
        u["Tm"] = bf(_dot_tn(bf(u["at_bd"]), bf(u.pop("PtB"))))
        m1_pair = M1[0:C] + M1[C:2 * C]
        cst = _dot_tn(bf(jnp.concatenate([m1_pair, u["v"]], axis=0)),
                      bf(jnp.concatenate([u["Bh"], u["Kh"]], axis=0)))
        u["cst"] = jnp.where(same, cst, 0.0)
        u["ar"] = bf(jnp.concatenate([u["At"], u["Rt"]], axis=0))

    states = [state_ref[p] for p in range(n_pair)]
    for c in range(n_chunk):
        rows = slice(c * C, (c + 1) * C)
        us = pre[c * n_pair:(c + 1) * n_pair]
        sbs = [bf(S) for S in states]
        zs = [_dot_nt(u["ar"], sb) for u, sb in zip(us, sbs)]
        new_states = [S * u["wc"] + _dot(sb, u["Tm"]) + u["cst"] for u, S, sb in zip(us, states, sbs)]
        ybds = [stack2(Z[C:2 * C]) + _dot(u["Q"], bf(stack2(Z[0:C]))) + u["Y0"] for u, Z in zip(us, zs)]
        ys = [y_bd[0:C] + y_bd[C:2 * C] for y_bd in ybds]
        states = new_states
        y = jnp.concatenate(ys, axis=1)
        mu = head_sum(y) * (1.0 / HEAD_DIM)
        yc = y - mu
        var = head_sum(yc * yc) * (1.0 / HEAD_DIM)
        yn = yc * lax.rsqrt(var + RWKV_GN_EPS) * ln_w + ln_b
        o_ref[rows, :] = ((yn + bonus[rows]) * g[rows]).astype(BF16)
    for p in range(n_pair):
        state_ref[p] = states[p]


def _rwkv(rw, vecs, w2, a2, g2, B, S, lb):
    nt = S // lb
    return pl.pallas_call(
        functools.partial(_rwkv_kernel, n_chunk=lb // RWKV_CHUNK),
        grid=(B, nt),
        in_specs=[pl.BlockSpec((lb, RWKV_PROJ), lambda b, j: (b * nt + j, 0)),
                  pl.BlockSpec((8, RWKV_WIDTH), lambda b, j: (0, 0)),
                  pl.BlockSpec((LANES, RWKV_WIDTH), lambda b, j: (0, 0)),
                  pl.BlockSpec((LANES, RWKV_WIDTH), lambda b, j: (0, 0)),
                  pl.BlockSpec((LANES, RWKV_WIDTH), lambda b, j: (0, 0))],
        out_specs=pl.BlockSpec((lb, RWKV_WIDTH), lambda b, j: (b * nt + j, 0)),
        out_shape=jax.ShapeDtypeStruct((B * S, RWKV_WIDTH), BF16),
        scratch_shapes=[pltpu.VMEM((RWKV_WIDTH // LANES, LANES, LANES), F32)],
        compiler_params=pltpu.CompilerParams(dimension_semantics=("arbitrary", "arbitrary"),
                                             vmem_limit_bytes=VMEM_LIMIT),
        name="rwkv",
    )(rw, vecs, w2, a2, g2)


def _mix_kernel(at_ref, rk_ref, x_ref, mod_ref, wo_a_ref, wo_r_ref, ln_ref, wr_ref, br_ref,
                x1_ref, h2_ref, ti_ref, gt_ref, rank_ref, cnt_ref, tb_ref, base_ref):
    first = (pl.program_id(0) == 0) & (pl.program_id(1) == 0)

    @pl.when(first)
    def _():
        base_ref[...] = jnp.zeros_like(base_ref)

    mod = mod_ref[0]
    y = _dot(at_ref[...], wo_a_ref[...]) + _dot(rk_ref[...], wo_r_ref[...])
    x = x_ref[...]
    tm = x.shape[0]
    x1 = _layer_norm(DEEPNORM_ALPHA * x + (1.0 + mod[2:3, :]) * y) * ln_ref[0:1, :] + ln_ref[1:2, :]
    h2 = _layer_norm(x1) * (1.0 + mod[4:5, :]) + mod[3:4, :]
    x1_ref[...] = x1
    h_hi = h2.astype(BF16)
    h2_ref[...] = h_hi

    h_lo = (h2 - h_hi.astype(F32)).astype(BF16)
    part = _dot(h_hi, wr_ref[...])
    logits = part[:, 0:LANES] + part[:, LANES:2 * LANES] + _dot(h_lo, wr_ref[:, 0:LANES]) + br_ref[...]
    lt = jnp.transpose(logits)[0:N_EXPERTS, :]
    erow = lax.broadcasted_iota(jnp.int32, (N_EXPERTS, tm), 0).astype(F32)
    cur = lt
    vals, idxs = [], []
    for _ in range(TOP_K):
        m = jnp.max(cur, axis=0, keepdims=True)
        idx = jnp.min(jnp.where(cur == m, erow, float(N_EXPERTS)), axis=0, keepdims=True)
        vals.append(m)
        idxs.append(idx)
        cur = jnp.where(erow == idx, -jnp.inf, cur)
    tv = jnp.concatenate(vals, axis=0)
    e = jnp.exp(tv - tv[0:1, :])
    gt_ref[...] = e / jnp.sum(e, axis=0, keepdims=True)
    ti_ref[...] = jnp.concatenate(idxs, axis=0).astype(jnp.int32)

    onehot = jnp.zeros((N_EXPERTS, tm), F32)
    for idx in idxs:
        onehot = onehot + (erow == idx).astype(F32)
    before = (lax.broadcasted_iota(jnp.int32, (tm, tm), 0)
              < lax.broadcasted_iota(jnp.int32, (tm, tm), 1)).astype(BF16)
    tot = base_ref[:, 0:1] + _dot(onehot.astype(BF16), before)
    ranks = [jnp.sum(jnp.where(erow == idx, tot, 0.0), axis=0, keepdims=True) for idx in idxs]
    rank_ref[...] = jnp.concatenate(ranks, axis=0).astype(jnp.int32)
    tb_ref[0] = base_ref[...].astype(jnp.int32)
    base_ref[...] = base_ref[...] + jnp.sum(onehot, axis=1, keepdims=True)
    cnt_ref[...] = base_ref[...].astype(jnp.int32)


def _mix(attn_out, rwkv_out, x2d, mod, wo_a, wo_r, ln1, w_router, b_router, B, S, tm):
    D = x2d.shape[1]
    nt = S // tm
    T = B * S
    tok = lambda b, j: (b * nt + j, 0)
    col = lambda b, j: (0, b * nt + j)
    fixed = lambda b, j: (0, 0)
    return pl.pallas_call(
        _mix_kernel,
        grid=(B, nt),
        in_specs=[pl.BlockSpec((tm, ATTN_WIDTH), tok),
                  pl.BlockSpec((tm, RWKV_WIDTH), tok),
                  pl.BlockSpec((tm, D), tok),
                  pl.BlockSpec((1, 6, D), lambda b, j: (b, 0, 0)),
                  pl.BlockSpec((ATTN_WIDTH, D), fixed),
                  pl.BlockSpec((RWKV_WIDTH, D), fixed),
                  pl.BlockSpec((2, D), fixed),
                  pl.BlockSpec((D, 2 * LANES), fixed),
                  pl.BlockSpec((1, LANES), fixed)],
        out_specs=[pl.BlockSpec((tm, D), tok),
                   pl.BlockSpec((tm, D), tok),
                   pl.BlockSpec((TOP_K, tm), col),
                   pl.BlockSpec((TOP_K, tm), col),
                   pl.BlockSpec((TOP_K, tm), col),
                   pl.BlockSpec((N_EXPERTS, LANES), fixed),
                   pl.BlockSpec((1, N_EXPERTS, LANES), lambda b, j: (b * nt + j, 0, 0))],
        out_shape=[jax.ShapeDtypeStruct((T, D), F32),
                   jax.ShapeDtypeStruct((T, D), BF16),
                   jax.ShapeDtypeStruct((TOP_K, T), jnp.int32),
                   jax.ShapeDtypeStruct((TOP_K, T), F32),
                   jax.ShapeDtypeStruct((TOP_K, T), jnp.int32),
                   jax.ShapeDtypeStruct((N_EXPERTS, LANES), jnp.int32),
                   jax.ShapeDtypeStruct((T // tm, N_EXPERTS, LANES), jnp.int32)],
        scratch_shapes=[pltpu.VMEM((N_EXPERTS, LANES), F32)],
        compiler_params=pltpu.CompilerParams(dimension_semantics=("arbitrary", "arbitrary"),
                                             vmem_limit_bytes=VMEM_LIMIT),
        name="mix",
    )(attn_out, rwkv_out, x2d, mod, wo_a, wo_r, ln1, w_router, b_router)


RUN_PIECES = tuple(2 ** b for b in range(int(math.log2(MOE_TILE)), -1, -1))
SUBLANES = 8


def _to_tiles(ref, x):
    n = x.shape[0]
    for c in range(SUBLANES):
        ref[pl.ds(c, n, stride=SUBLANES), :] = x[:, c * LANES:(c + 1) * LANES]


def _from_tiles(ref):
    n = ref.shape[0] // SUBLANES
    return jnp.concatenate([ref[pl.ds(c, n, stride=SUBLANES), :] for c in range(SUBLANES)], axis=1)


def _run_copies(n, local, local_start, remote, remote_start, sem, to_remote):
    off = 0
    for piece in RUN_PIECES:
        take = (n & piece) != 0

        @pl.when(take)
        def _(off=off, piece=piece):
            lo = pl.multiple_of((local_start + off) * SUBLANES, SUBLANES)
            ro = pl.multiple_of((remote_start + off) * SUBLANES, SUBLANES)
            loc = local.at[pl.ds(lo, piece * SUBLANES)]
            rem = remote.at[pl.ds(ro, piece * SUBLANES)]
            src, dst = (loc, rem) if to_remote else (rem, loc)
            pltpu.make_async_copy(src, dst, sem).start()

        off = off + (n & piece)


def _dispatch_kernel(tcnt_ref, lstart_ref, gstart_ref, pad_ref, pad_start_ref, n_used_ref, lpos_ref, h2_ref,
                     xs_ref, xbuf, zbuf, sems):
    i = pl.program_id(0)
    tm = h2_ref.shape[0]
    n_loc = TOP_K * tm
    n_blocks = xs_ref.shape[0] // (MOE_BLOCK * SUBLANES)
    zero_sem = sems.at[2]

    @pl.when(i == 0)
    def _():
        zbuf[...] = jnp.zeros_like(zbuf)

        def zero_pad(e, carry):
            _run_copies(pad_ref[e], zbuf, 0, xs_ref, pad_start_ref[e], zero_sem, True)
            return carry

        def zero_tail(b, carry):
            @pl.when(b >= n_used_ref[0])
            def _():
                start = pl.multiple_of(b * (MOE_BLOCK * SUBLANES), MOE_BLOCK * SUBLANES)
                pltpu.make_async_copy(zbuf, xs_ref.at[pl.ds(start, MOE_BLOCK * SUBLANES)], zero_sem).start()
            return carry

        lax.fori_loop(0, N_EXPERTS, zero_pad, 0)
        lax.fori_loop(n_blocks - N_EXPERTS, n_blocks, zero_tail, 0)

    slot = lax.broadcasted_iota(jnp.int32, (n_loc, tm), 0)
    lpos = lpos_ref[...]
    perm = jnp.zeros((n_loc, tm), F32)
    for k in range(TOP_K):
        perm = perm + (slot == lpos[k:k + 1, :]).astype(F32)
    perm = perm.astype(BF16)

    def wait_tile(s):
        pltpu.make_async_copy(xbuf.at[s], xs_ref.at[pl.ds(0, n_loc * SUBLANES)], sems.at[s]).wait()

    for s in range(2):
        @pl.when((i % 2 == s) & (i >= 2))
        def _(s=s):
            wait_tile(s)

    for s in range(2):
        @pl.when(i % 2 == s)
        def _(s=s):
            _to_tiles(xbuf.at[s], _dot(perm, h2_ref[...]))

            def issue(e, carry):
                idx = i * N_EXPERTS + e
                _run_copies(tcnt_ref[idx], xbuf.at[s], lstart_ref[idx], xs_ref, gstart_ref[idx], sems.at[s], True)
                return carry

            lax.fori_loop(0, N_EXPERTS, issue, 0)

    @pl.when(i == pl.num_programs(0) - 1)
    def _():
        for s in range(2):
            @pl.when((i % 2 == s) | (i >= 1))
            def _(s=s):
                wait_tile(s)
        n_zero = N_EXPERTS * MOE_BLOCK * SUBLANES
        pltpu.make_async_copy(xs_ref.at[pl.ds(0, n_zero)], xs_ref.at[pl.ds(0, n_zero)], zero_sem).wait()


def _dispatch(tcnt, lstart, gstart, pad, pad_start, n_used, lpos, h2, n_rows, tm):
    T, D = h2.shape
    grid_spec = pltpu.PrefetchScalarGridSpec(
        num_scalar_prefetch=6,
        grid=(T // tm,),
        in_specs=[pl.BlockSpec((TOP_K, tm), lambda i, *_: (0, i)),
                  pl.BlockSpec((tm, D), lambda i, *_: (i, 0))],
        out_specs=pl.BlockSpec(memory_space=pl.ANY),
        scratch_shapes=[pltpu.VMEM((2, TOP_K * tm * SUBLANES, LANES), F32),
                        pltpu.VMEM((MOE_BLOCK * SUBLANES, LANES), F32),
                        pltpu.SemaphoreType.DMA((3,))],
    )
    return pl.pallas_call(
        _dispatch_kernel,
        grid_spec=grid_spec,
        out_shape=jax.ShapeDtypeStruct((n_rows * SUBLANES, LANES), F32),
        compiler_params=pltpu.CompilerParams(dimension_semantics=("arbitrary",),
                                             vmem_limit_bytes=VMEM_LIMIT),
        name="dispatch",
    )(tcnt, lstart, gstart, pad, pad_start, n_used, lpos, h2)


def _experts_kernel(blk_e_ref, n_used_ref, next_e_ref, xs_ref, wgu_hbm, bgu_ref, wd_hbm, bd_ref, ys_ref,
                    wgu_f32, wd_f32, wgu_bf, wd_bf, sems):
    i = pl.program_id(0)
    d_ff = wd_bf.shape[0]
    used = i < n_used_ref[0]
    e = blk_e_ref[i]
    new_expert = (i == 0) | (e != blk_e_ref[jnp.maximum(i - 1, 0)])

    def weight_copies(ex):
        return (pltpu.make_async_copy(wgu_hbm.at[ex], wgu_f32, sems.at[0]),
                pltpu.make_async_copy(wd_hbm.at[ex], wd_f32, sems.at[1]))

    @pl.when(i == 0)
    def _():
        for cp in weight_copies(e):
            cp.start()

    @pl.when(used & new_expert)
    def _():
        for cp in weight_copies(e):
            cp.wait()
        wgu_bf[...] = wgu_f32[...].astype(BF16)
        wd_bf[...] = wd_f32[...].astype(BF16)
        nxt = next_e_ref[i]

        @pl.when(nxt >= 0)
        def _():
            for cp in weight_copies(nxt):
                cp.start()

    @pl.when(used)
    def _():
        xb = _from_tiles(xs_ref).astype(BF16)
        gu = _dot(xb, wgu_bf[...]) + bgu_ref[0]
        gate = jnp.minimum(gu[:, :d_ff], SWIGLU_LIMIT)
        up = jnp.clip(gu[:, d_ff:], -SWIGLU_LIMIT, SWIGLU_LIMIT)
        act = (up + 1.0) * (gate * _sigmoid(SWIGLU_ALPHA * gate))
        _to_tiles(ys_ref, _dot(act.astype(BF16), wd_bf[...]) + bd_ref[0])

    @pl.when(i >= n_used_ref[0])
    def _():
        ys_ref[...] = jnp.zeros_like(ys_ref)


def _experts(blk_e, n_used, next_e, xs, wgu, bgu, wd, bd):
    d_ff, D = wd.shape[1], wd.shape[2]
    n_blocks = xs.shape[0] // (MOE_BLOCK * SUBLANES)
    blk = (MOE_BLOCK * SUBLANES, LANES)

    def last_used(i, n_used_ref):
        return jnp.minimum(i, jnp.maximum(n_used_ref[0] - 1, 0))

    def row_map(i, blk_e_ref, n_used_ref, next_e_ref):
        return (last_used(i, n_used_ref), 0)

    def exp_map(i, blk_e_ref, n_used_ref, next_e_ref):
        return (blk_e_ref[last_used(i, n_used_ref)], 0, 0)

    grid_spec = pltpu.PrefetchScalarGridSpec(
        num_scalar_prefetch=3,
        grid=(n_blocks,),
        in_specs=[pl.BlockSpec(blk, row_map),
                  pl.BlockSpec(memory_space=pl.ANY),
                  pl.BlockSpec((1, 1, 2 * d_ff), exp_map),
                  pl.BlockSpec(memory_space=pl.ANY),
                  pl.BlockSpec((1, 1, D), exp_map)],
        out_specs=pl.BlockSpec(blk, lambda i, *_: (i, 0)),
        scratch_shapes=[pltpu.VMEM((D, 2 * d_ff), F32), pltpu.VMEM((d_ff, D), F32),
                        pltpu.VMEM((D, 2 * d_ff), BF16), pltpu.VMEM((d_ff, D), BF16),
                        pltpu.SemaphoreType.DMA((2,))],
    )
    return pl.pallas_call(
        _experts_kernel,
        grid_spec=grid_spec,
        out_shape=jax.ShapeDtypeStruct(xs.shape, F32),
        compiler_params=pltpu.CompilerParams(dimension_semantics=("arbitrary",),
                                             vmem_limit_bytes=VMEM_LIMIT),
        name="experts",
    )(blk_e, n_used, next_e, xs, wgu, bgu, wd, bd)


def _combine_kernel(tcnt_ref, lstart_ref, gstart_ref, ys_ref, lpos_ref, gt_ref, x1_ref, mod_ref, ln_ref,
                    o_ref, buf, y_ref, sems):
    nt = pl.num_programs(1)
    n_tiles = pl.num_programs(0) * nt
    i = pl.program_id(0) * nt + pl.program_id(1)
    tm = x1_ref.shape[0]
    n_loc = TOP_K * tm

    def fetch(tile, s):
        def issue(e, carry):
            idx = tile * N_EXPERTS + e
            _run_copies(tcnt_ref[idx], buf.at[s], lstart_ref[idx], ys_ref, gstart_ref[idx], sems.at[s], False)
            return carry

        lax.fori_loop(0, N_EXPERTS, issue, 0)

    @pl.when(i == 0)
    def _():
        fetch(i, 0)

    for s in range(2):
        @pl.when(((i + 1) % 2 == s) & (i + 1 < n_tiles))
        def _(s=s):
            fetch(i + 1, s)

    slot = lax.broadcasted_iota(jnp.int32, (tm, n_loc), 1)
    lpos = lpos_ref[...]
    gt = gt_ref[...]
    pick = jnp.zeros((tm, n_loc), F32)
    for k in range(TOP_K):
        pick = pick + jnp.where(slot == lpos[:, k:k + 1], gt[:, k:k + 1], 0.0)
    pick = pick.astype(BF16)
    for s in range(2):
        @pl.when(i % 2 == s)
        def _(s=s):
            pltpu.make_async_copy(ys_ref.at[pl.ds(0, n_loc * SUBLANES)], buf.at[s], sems.at[s]).wait()
            y_ref[...] = _dot(pick, _from_tiles(buf.at[s]).astype(BF16))
    y = y_ref[...]
    mod = mod_ref[0]
    z = DEEPNORM_ALPHA * x1_ref[...] + (1.0 + mod[5:6, :]) * y
    o_ref[...] = _layer_norm(z) * ln_ref[0:1, :] + ln_ref[1:2, :]


def _combine(tcnt, lstart, gstart, ys, lpos_t, gates_t, x1, mod, ln2, B, S, tm):
    T, D = x1.shape
    nt = S // tm
    tok = lambda b, j, *_: (b * nt + j, 0)
    grid_spec = pltpu.PrefetchScalarGridSpec(
        num_scalar_prefetch=3,
        grid=(B, nt),
        in_specs=[pl.BlockSpec(memory_space=pl.ANY),
                  pl.BlockSpec((tm, TOP_K), tok),
                  pl.BlockSpec((tm, TOP_K), tok),
                  pl.BlockSpec((tm, D), tok),
                  pl.BlockSpec((1, 6, D), lambda b, j, *_: (b, 0, 0)),
                  pl.BlockSpec((2, D), lambda b, j, *_: (0, 0))],
        out_specs=pl.BlockSpec((tm, D), tok),
        scratch_shapes=[pltpu.VMEM((2, TOP_K * tm * SUBLANES, LANES), F32), pltpu.VMEM((tm, D), F32),
                        pltpu.SemaphoreType.DMA((2,))],
    )
    return pl.pallas_call(
        _combine_kernel,
        grid_spec=grid_spec,
        out_shape=jax.ShapeDtypeStruct((T, D), F32),
        compiler_params=pltpu.CompilerParams(dimension_semantics=("arbitrary", "arbitrary"),
                                             vmem_limit_bytes=VMEM_LIMIT),
        name="combine",
    )(tcnt, lstart, gstart, ys, lpos_t, gates_t, x1, mod, ln2)


def _pad_rows(w, rows):
    return jnp.pad(w, ((0, rows - w.shape[0]), (0, 0)))


def _pad_cols(w, cols):
    return jnp.pad(w, ((0, 0), (0, cols - w.shape[1])))


def _layer(x, c, positions, w_ada, b_ada, w_in, shift_mu, rwkv_w0, rwkv_w2, rwkv_a0, rwkv_a2, rwkv_g2,
           rwkv_k_k, rwkv_k_a, rwkv_r_k, rwkv_ln_w, rwkv_ln_b, attn_sinks, w_out, ln1_g, ln1_b,
           w_router, b_router, w_gate_up, b_gate_up, w_down, b_down, ln2_g, ln2_b):
    B, S, D = x.shape
    T = B * S

    q0, k0, v0 = 0, ATTN_WIDTH, ATTN_WIDTH + KV_WIDTH
    r0 = ATTN_WIDTH + 2 * KV_WIDTH
    heads = lambda base: [w_in[:, base + h * HEAD_DIM: base + (h + 1) * HEAD_DIM] for h in range(N_KV_HEADS)]
    dup = lambda hs: [w for w in hs for _ in range(2)]
    w_attn = jnp.concatenate([w_in[:, q0:q0 + ATTN_WIDTH]] + dup(heads(k0)) + dup(heads(v0)), axis=1).astype(BF16)
    lora0 = r0 + 3 * RWKV_WIDTH
    lora = (DECAY_LORA, AAA_LORA, GATE_LORA)
    pieces_w = [w_in[:, r0:lora0]]
    pieces_mu = [shift_mu[None, 0:3 * RWKV_WIDTH]]
    off = lora0
    for n in lora:
        pieces_w.append(_pad_cols(w_in[:, off:off + n], LANES))
        pieces_mu.append(_pad_cols(shift_mu[None, off - r0:off - r0 + n], LANES))
        off += n
    w_rwkv = jnp.concatenate(pieces_w, axis=1).astype(BF16)
    mu = jnp.concatenate(pieces_mu, axis=1)
    inv_freq = ROPE_THETA ** (-jnp.arange(0, ROT_DIM, 2, dtype=F32) / ROT_DIM)
    lane_p = jnp.arange(LANES) % HEAD_DIM
    n_freq = ROT_DIM // 2
    rot_tab = jnp.zeros((8, LANES), F32)
    rot_tab = rot_tab.at[0].set(jnp.where(lane_p < ROT_DIM, 0.0, 1.0))
    rot_tab = rot_tab.at[1].set(jnp.where(lane_p < n_freq, -1.0, 0.0))
    rot_tab = rot_tab.at[2].set(jnp.where((lane_p >= n_freq) & (lane_p < ROT_DIM), 1.0, 0.0))
    freq_tab = jnp.broadcast_to(inv_freq[:, None], (n_freq, LANES))
    lane_freq = (jnp.arange(n_freq)[:, None] == (lane_p % n_freq)[None, :]) & (lane_p < ROT_DIM)[None, :]
    zeros = jnp.zeros_like(lane_freq)
    expand = jnp.concatenate([jnp.concatenate([lane_freq, zeros], axis=1),
                              jnp.concatenate([zeros, lane_freq], axis=1)], axis=0).astype(BF16)
    vecs = jnp.stack([rwkv_w0, rwkv_a0, rwkv_k_k, rwkv_k_a, rwkv_r_k.reshape(-1), rwkv_ln_w, rwkv_ln_b,
                      jnp.zeros_like(rwkv_w0)])
    w2 = _pad_rows(rwkv_w2, LANES).astype(BF16)
    a2 = _pad_rows(rwkv_a2, LANES).astype(BF16)
    g2 = _pad_rows(rwkv_g2, LANES).astype(BF16)
    wo_a = w_out[:ATTN_WIDTH].astype(BF16)
    wo_r = w_out[ATTN_WIDTH:].astype(BF16)
    w_r_hi = w_router.astype(BF16)
    w_r_lo = (w_router - w_r_hi.astype(F32)).astype(BF16)
    w_r = jnp.concatenate([_pad_cols(w_r_hi, LANES), _pad_cols(w_r_lo, LANES)], axis=1)
    b_r = jnp.concatenate([b_router, jnp.full((LANES - N_EXPERTS,), NEG_INF, F32)])[None, :]

    mod = _mod(c, w_ada, b_ada).reshape(B, 6, D)
    qkv, rw = _inproj(x, positions, mod, w_attn, w_rwkv, mu, rot_tab, freq_tab, expand, min(INPROJ_TILE, S))
    attn_out = _attention(qkv, attn_sinks, B, S, min(ATTN_STEP_BLOCKS, S // ATTN_BLOCK))
    rwkv_out = _rwkv(rw, vecs, w2, a2, g2, B, S, min(RWKV_STEP, S))

    mtile = min(MOE_TILE, S)
    x1, h2, top_i, gates, rank, cnt, tbase = _mix(attn_out, rwkv_out, x.reshape(T, D), mod, wo_a, wo_r,
                                                   jnp.stack([ln1_g, ln1_b]), w_r, b_r, B, S, mtile)

    counts = cnt[:, 0]
    padded = (counts + MOE_BLOCK - 1) // MOE_BLOCK * MOE_BLOCK
    pend = jnp.cumsum(padded)
    pstart = pend - padded
    n_blocks = T * TOP_K // MOE_BLOCK + N_EXPERTS
    blk_row = jnp.arange(n_blocks, dtype=jnp.int32) * MOE_BLOCK
    blk_e = jnp.minimum(jnp.sum((pend[None, :] <= blk_row[:, None]).astype(jnp.int32), axis=1), N_EXPERTS - 1)
    n_used = (pend[-1:] // MOE_BLOCK).astype(jnp.int32)
    tb = tbase[:, :, 0]
    tcnt = jnp.concatenate([tb[1:], counts[None]], axis=0) - tb
    lstart = jnp.cumsum(tcnt, axis=1) - tcnt
    gstart = pstart[None, :] + tb
    shift = jnp.repeat(jnp.transpose(lstart - tb), mtile, axis=1)
    experts = jnp.arange(N_EXPERTS, dtype=jnp.int32)
    lpos = rank + jnp.sum(jnp.where(top_i[None] == experts[:, None, None], shift[:, None, :], 0), axis=0)
    flat = lambda a: a.reshape(-1).astype(jnp.int32)

    xs = _dispatch(flat(tcnt), flat(lstart), flat(gstart), flat(padded - counts), flat(pstart + counts), n_used,
                   lpos, h2, n_blocks * MOE_BLOCK, mtile)
    later_with_rows = (experts[None, :] > experts[:, None]) & (counts[None, :] > 0)
    next_of = jnp.min(jnp.where(later_with_rows, experts[None, :], N_EXPERTS), axis=1)
    next_of = jnp.where(next_of < N_EXPERTS, next_of, -1)
    next_e = jnp.sum(jnp.where(blk_e[:, None] == experts[None, :], next_of[None, :], 0), axis=1).astype(jnp.int32)
    ys = _experts(blk_e, n_used, next_e, xs, w_gate_up, b_gate_up[:, None, :], w_down, b_down[:, None, :])
    out = _combine(flat(tcnt), flat(lstart), flat(gstart), ys, jnp.transpose(lpos), jnp.transpose(gates), x1, mod,
                   jnp.stack([ln2_g, ln2_b]), B, S, mtile)
    return out.reshape(B, S, D)


def kernel(x, c, positions, w_ada, b_ada, w_in, shift_mu, rwkv_w0, rwkv_w2, rwkv_a0, rwkv_a2, rwkv_g2,
           rwkv_k_k, rwkv_k_a, rwkv_r_k, rwkv_ln_w, rwkv_ln_b, attn_sinks, w_out, ln1_g, ln1_b,
           w_router, b_router, w_gate_up, b_gate_up, w_down, b_down, ln2_g, ln2_b):
    for l in range(DEPTH):
        x = _layer(x, c, positions, w_ada[l], b_ada[l], w_in[l], shift_mu[l], rwkv_w0[l], rwkv_w2[l],
                   rwkv_a0[l], rwkv_a2[l], rwkv_g2[l], rwkv_k_k[l], rwkv_k_a[l], rwkv_r_k[l], rwkv_ln_w[l],
                   rwkv_ln_b[l], attn_sinks[l], w_out[l], ln1_g[l], ln1_b[l], w_router[l], b_router[l],
                   w_gate_up[l], b_gate_up[l], w_down[l], b_down[l], ln2_g[l], ln2_b[l])
    return x
```

```python
import functools
import math

import jax
import jax.numpy as jnp
from jax import lax
from jax.experimental import pallas as pl
from jax.experimental.pallas import tpu as pltpu

F32 = jnp.float32
BF16 = jnp.bfloat16

HEAD_DIM = 64
N_ATTN_HEADS = 8
N_KV_HEADS = 2
N_RWKV_HEADS = 8
ATTN_WIDTH = N_ATTN_HEADS * HEAD_DIM
KV_WIDTH = N_KV_HEADS * HEAD_DIM
RWKV_WIDTH = N_RWKV_HEADS * HEAD_DIM
ATTN_BLOCK = 128
ROT_DIM = HEAD_DIM // 4
ROPE_THETA = 500000.0
DECAY_LORA = 32
AAA_LORA = 32
GATE_LORA = 96
N_EXPERTS = 32
TOP_K = 4
SWIGLU_LIMIT = 7.0
SWIGLU_ALPHA = 1.702
LN_EPS = 1e-5
RWKV_GN_EPS = 64e-5
NEG_INF = -1e30
DEPTH = 1
DEEPNORM_ALPHA = (2 * DEPTH) ** 0.25

LANES = 128
RWKV_CHUNK = 64
RWKV_STEP = 512
INPROJ_TILE = 512
ATTN_STEP_BLOCKS = 2
MOE_BLOCK = 512
MOE_TILE = 256
ATTN_PROJ = ATTN_WIDTH + 4 * KV_WIDTH
RWKV_PROJ = 3 * RWKV_WIDTH + 3 * LANES
VMEM_LIMIT = 48 * 1024 * 1024


def _dot(a, b):
    return jnp.dot(a, b, preferred_element_type=F32)


def _dot_nt(a, b):
    return lax.dot_general(a, b, (((1,), (1,)), ((), ())), preferred_element_type=F32)


def _dot_tn(a, b):
    return lax.dot_general(a, b, (((0,), (0,)), ((), ())), preferred_element_type=F32)


def _split3(x):
    h = x.astype(BF16)
    r1 = x - h.astype(F32)
    m = r1.astype(BF16)
    lo = (r1 - m.astype(F32)).astype(BF16)
    return h, m, lo


def _dot_exact_lhs(m_bf16, x):
    h, m, lo = _split3(x)
    return _dot(m_bf16, h) + _dot(m_bf16, m) + _dot(m_bf16, lo)


def _layer_norm(x):
    mu = jnp.mean(x, axis=-1, keepdims=True)
    xc = x - mu
    var = jnp.mean(xc * xc, axis=-1, keepdims=True)
    return xc * lax.rsqrt(var + LN_EPS)


def _sigmoid(x):
    return 1.0 / (1.0 + jnp.exp(-x))


def _mod_kernel(c_ref, w_ref, b_ref, o_ref):
    c = c_ref[...]
    s = c * _sigmoid(c)
    o_ref[...] = jnp.dot(s, w_ref[...], preferred_element_type=F32,
                         precision=lax.Precision.HIGHEST) + b_ref[...]


def _mod(c, w_ada, b_ada):
    B, D = c.shape
    n = w_ada.shape[1] // D
    return pl.pallas_call(
        _mod_kernel,
        grid=(n,),
        in_specs=[pl.BlockSpec((B, D), lambda i: (0, 0)),
                  pl.BlockSpec((D, D), lambda i: (0, i)),
                  pl.BlockSpec((1, D), lambda i: (0, i))],
        out_specs=pl.BlockSpec((B, D), lambda i: (0, i)),
        out_shape=jax.ShapeDtypeStruct((B, n * D), F32),
        compiler_params=pltpu.CompilerParams(dimension_semantics=("arbitrary",),
                                             vmem_limit_bytes=VMEM_LIMIT),
        name="mod",
    )(c, w_ada, b_ada.reshape(1, -1))


def _inproj_kernel(x_ref, pos_ref, mod_ref, wa_ref, wr_ref, mu_ref, rt_ref, fq_ref, ex_ref,
                   qkv_ref, rw_ref, carry_ref):
    j = pl.program_id(1)
    x = x_ref[0]
    tm = x.shape[0]
    mod = mod_ref[0]
    h = _layer_norm(x) * (1.0 + mod[1:2, :]) + mod[0:1, :]
    hb = h.astype(BF16)

    pa = _dot(hb, wa_ref[...])
    ang = pos_ref[0].astype(F32) * fq_ref[:, 0:1]
    pieces = _split3(jnp.concatenate([jnp.cos(ang), jnp.sin(ang)], axis=0))
    trig = _dot_tn(pieces[0], ex_ref[...]) + _dot_tn(pieces[1], ex_ref[...]) + _dot_tn(pieces[2], ex_ref[...])
    cs = trig[:, 0:LANES] + rt_ref[0:1, :]
    sn = trig[:, LANES:2 * LANES]
    m_lo = rt_ref[1:2, :]
    m_hi = rt_ref[2:3, :]
    n_q = ATTN_WIDTH // LANES
    n_rot = (ATTN_WIDTH + 2 * KV_WIDTH) // LANES
    for ch in range(n_rot):
        t = pa[:, ch * LANES:(ch + 1) * LANES]
        if ch < n_q:
            t = t * (1.0 / math.sqrt(HEAD_DIM))
        up = pltpu.roll(t, LANES - ROT_DIM // 2, 1)
        dn = pltpu.roll(t, ROT_DIM // 2, 1)
        o = t * cs + sn * (m_lo * up + m_hi * dn)
        qkv_ref[:, ch * LANES:(ch + 1) * LANES] = o.astype(BF16)
    qkv_ref[:, n_rot * LANES:] = pa[:, n_rot * LANES:].astype(BF16)

    pr = _dot(hb, wr_ref[...])
    prev = pltpu.roll(pr, 1, 0)
    row = lax.broadcasted_iota(jnp.int32, (tm, 1), 0)
    carry = jnp.where(j == 0, 0.0, carry_ref[...])
    prev = jnp.where(row == 0, carry, prev)
    carry_ref[...] = pr[tm - 1:tm, :]
    rw_ref[...] = pr + (prev - pr) * mu_ref[...]


def _inproj(x, positions, mod, w_attn, w_rwkv, mu, rot_tab, freq_tab, expand, tm):
    B, S, D = x.shape
    nt = S // tm
    return pl.pallas_call(
        _inproj_kernel,
        grid=(B, nt),
        in_specs=[pl.BlockSpec((1, tm, D), lambda b, j: (b, j, 0)),
                  pl.BlockSpec((1, 1, tm), lambda b, j: (b, 0, j)),
                  pl.BlockSpec((1, 6, D), lambda b, j: (b, 0, 0)),
                  pl.BlockSpec((D, ATTN_PROJ), lambda b, j: (0, 0)),
                  pl.BlockSpec((D, RWKV_PROJ), lambda b, j: (0, 0)),
                  pl.BlockSpec((1, RWKV_PROJ), lambda b, j: (0, 0)),
                  pl.BlockSpec((8, LANES), lambda b, j: (0, 0)),
                  pl.BlockSpec(freq_tab.shape, lambda b, j: (0, 0)),
                  pl.BlockSpec(expand.shape, lambda b, j: (0, 0))],
        out_specs=[pl.BlockSpec((tm, ATTN_PROJ), lambda b, j: (b * nt + j, 0)),
                   pl.BlockSpec((tm, RWKV_PROJ), lambda b, j: (b * nt + j, 0))],
        out_shape=[jax.ShapeDtypeStruct((B * S, ATTN_PROJ), BF16),
                   jax.ShapeDtypeStruct((B * S, RWKV_PROJ), F32)],
        scratch_shapes=[pltpu.VMEM((1, RWKV_PROJ), F32)],
        compiler_params=pltpu.CompilerParams(dimension_semantics=("arbitrary", "arbitrary"),
                                             vmem_limit_bytes=VMEM_LIMIT),
        name="inproj",
    )(x, positions.reshape(B, 1, S), mod, w_attn, w_rwkv, mu, rot_tab, freq_tab, expand)


def _attn_kernel(q_ref, kc_ref, kp_ref, vc_ref, vp_ref, sink_ref, o_ref):
    i = pl.program_id(1)
    blk = ATTN_BLOCK
    n_sub = q_ref.shape[0] // blk
    qi = lax.broadcasted_iota(jnp.int32, (blk, 2 * blk), 0)
    kj = lax.broadcasted_iota(jnp.int32, (blk, 2 * blk), 1)
    band = (kj > qi) & (kj <= qi + blk)
    first = band & ((kj >= blk) | (i > 0))
    lane = lax.broadcasted_iota(jnp.int32, (1, LANES), 1)
    lo = (lane < HEAD_DIM).astype(BF16)
    hi = (lane >= HEAD_DIM).astype(BF16)
    halves = {}
    for u in range(n_sub):
        for g in range(N_KV_HEADS):
            sl = slice(g * LANES, (g + 1) * LANES)
            prev_k = kp_ref[:, sl] if u == 0 else kc_ref[(u - 1) * blk:u * blk, sl]
            prev_v = vp_ref[:, sl] if u == 0 else vc_ref[(u - 1) * blk:u * blk, sl]
            kcat = jnp.concatenate([prev_k, kc_ref[u * blk:(u + 1) * blk, sl]], axis=0)
            vcat = jnp.concatenate([prev_v, vc_ref[u * blk:(u + 1) * blk, sl]], axis=0)
            halves[u, g] = ((kcat * lo, vcat * lo), (kcat * hi, vcat * hi))
    units = [(u, c, half) for u in range(n_sub) for c in range(ATTN_WIDTH // LANES) for half in range(2)]
    scores = [_dot_nt(q_ref[u * blk:(u + 1) * blk, c * LANES:(c + 1) * LANES], halves[u, c // 2][half][0])
              for u, c, half in units]
    probs, denoms = [], []
    for (u, c, half), s in zip(units, scores):
        sink = sink_ref[2 * c + half]
        s = jnp.where(first if u == 0 else band, s, NEG_INF)
        m = jnp.maximum(jnp.max(s, axis=-1, keepdims=True), sink)
        p = jnp.exp(s - m)
        denoms.append(jnp.sum(p, axis=-1, keepdims=True) + jnp.exp(sink - m))
        probs.append(p.astype(BF16))
    outs = [_dot(p, halves[u, c // 2][half][1]) / d for (u, c, half), p, d in zip(units, probs, denoms)]
    for n, (u, c, half) in enumerate(units):
        if half == 0:
            o_ref[u * blk:(u + 1) * blk, c * LANES:(c + 1) * LANES] = (outs[n] + outs[n + 1]).astype(BF16)


def _attention(qkv, sinks, B, S, n_sub):
    blk = ATTN_BLOCK
    step = n_sub * blk
    ns = S // step
    kcol = ATTN_WIDTH // (2 * KV_WIDTH)
    vcol = kcol + 1
    cur = lambda col: (lambda b, i: (b * ns + i, col))
    prv = lambda col: (lambda b, i: (jnp.maximum((b * ns + i) * n_sub - 1, 0), col))
    return pl.pallas_call(
        _attn_kernel,
        grid=(B, ns),
        in_specs=[pl.BlockSpec((step, ATTN_WIDTH), lambda b, i: (b * ns + i, 0)),
                  pl.BlockSpec((step, 2 * KV_WIDTH), cur(kcol)),
                  pl.BlockSpec((blk, 2 * KV_WIDTH), prv(kcol)),
                  pl.BlockSpec((step, 2 * KV_WIDTH), cur(vcol)),
                  pl.BlockSpec((blk, 2 * KV_WIDTH), prv(vcol)),
                  pl.BlockSpec(memory_space=pltpu.SMEM)],
        out_specs=pl.BlockSpec((step, ATTN_WIDTH), lambda b, i: (b * ns + i, 0)),
        out_shape=jax.ShapeDtypeStruct((B * S, ATTN_WIDTH), BF16),
        compiler_params=pltpu.CompilerParams(dimension_semantics=("arbitrary", "arbitrary"),
                                             vmem_limit_bytes=VMEM_LIMIT),
        name="attn",
    )(qkv, qkv, qkv, qkv, qkv, sinks)


def _rwkv_kernel(rw_ref, vec_ref, w2_ref, a2_ref, g2_ref, o_ref, state_ref, *, n_chunk):
    j = pl.program_id(1)
    C = RWKV_CHUNK
    W = RWKV_WIDTH
    n_pair = W // LANES

    @pl.when(j == 0)
    def _():
        state_ref[...] = jnp.zeros_like(state_ref)

    w0 = vec_ref[0:1, :]
    a0 = vec_ref[1:2, :]
    k_k = vec_ref[2:3, :]
    k_a = vec_ref[3:4, :]
    r_k = vec_ref[4:5, :]
    ln_w = vec_ref[5:6, :]
    ln_b = vec_ref[6:7, :]

    r = rw_ref[:, 0:W]
    k = rw_ref[:, W:2 * W]
    v = rw_ref[:, 2 * W:3 * W]
    wl = rw_ref[:, 3 * W:3 * W + LANES]
    al = rw_ref[:, 3 * W + LANES:3 * W + 2 * LANES]
    gl = rw_ref[:, 3 * W + 2 * LANES:3 * W + 3 * LANES]

    ri = lax.broadcasted_iota(jnp.int32, (LANES, LANES), 0)
    ci = lax.broadcasted_iota(jnp.int32, (LANES, LANES), 1)
    same = (ri // HEAD_DIM) == (ci // HEAD_DIM)
    strict = same & ((ri % HEAD_DIM) > (ci % HEAD_DIM))
    incl = same & ((ri % HEAD_DIM) >= (ci % HEAD_DIM))
    lane = lax.broadcasted_iota(jnp.int32, (1, LANES), 1)
    m0 = (lane < HEAD_DIM).astype(F32)
    m1 = 1.0 - m0
    tri = (lax.broadcasted_iota(jnp.int32, (C, C), 0) >= lax.broadcasted_iota(jnp.int32, (C, C), 1)).astype(BF16)

    def head_sum(xv):
        outs = []
        for p in range(n_pair):
            xp = xv[:, p * LANES:(p + 1) * LANES]
            s0 = jnp.sum(xp * m0, axis=1, keepdims=True)
            s1 = jnp.sum(xp * m1, axis=1, keepdims=True)
            outs.append(s0 * m0 + s1 * m1)
        return jnp.concatenate(outs, axis=1)

    def stack2(xp):
        return jnp.concatenate([xp * m0, xp * m1], axis=0)

    z = w0 + _dot(jnp.tanh(wl).astype(BF16), w2_ref[...])
    lw = -math.exp(-0.5) * _sigmoid(z)
    a = _sigmoid(a0 + _dot(al.astype(BF16), a2_ref[...]))
    g = _dot(_sigmoid(gl).astype(BF16), g2_ref[...])
    kk = k * k_k
    kkn = kk / jnp.maximum(jnp.sqrt(head_sum(kk * kk)), 1e-12)
    k2 = k * (1.0 + (a - 1.0) * k_a)
    av = -kkn
    bv = kkn * a
    bonus = head_sum(r * k2 * r_k) * v

    eye = (ri == ci).astype(F32)
    bf = lambda t: t.astype(BF16)

    pre = []
    for c in range(n_chunk):
        rows = slice(c * C, (c + 1) * C)
        lwc = lw[rows]
        cw = _dot_exact_lhs(tri, lwc)
        cwl = cw[C - 1:C, :]
        e_in = jnp.exp(cw)
        e_neg = jnp.exp(-cw)
        e_rem = jnp.exp(cwl - cw)
        wc = jnp.exp(cwl)
        Rt = r[rows] * e_in
        At = av[rows] * jnp.exp(cw - lwc)
        Bb = bv[rows] * e_neg
        Kb = k2[rows] * e_neg
        Bh = bv[rows] * e_rem
        Kh = k2[rows] * e_rem
        vc = v[rows]
        for p in range(n_pair):
            sl = slice(p * LANES, (p + 1) * LANES)
            pre.append(dict(At=At[:, sl], Rt=Rt[:, sl], Bb=Bb[:, sl], Kb=Kb[:, sl], Bh=Bh[:, sl], Kh=Kh[:, sl],
                            v=vc[:, sl], wc=wc[:, sl]))

    for u in pre:
        u["at_bd"] = stack2(u["At"])
        lhs = bf(jnp.concatenate([u["at_bd"], stack2(u["Rt"])], axis=0))
        rhs = bf(jnp.concatenate([stack2(u["Bb"]), stack2(u["Kb"])], axis=0))
        u["G"] = _dot_nt(lhs, rhs)
    for u in pre:
        G = u.pop("G")
        u["a_ab"] = jnp.where(strict, G[0:2 * C, 0:2 * C], 0.0)
        u["a_ak"] = bf(jnp.where(strict, G[0:2 * C, 2 * C:4 * C], 0.0))
        u["a_rb"] = bf(jnp.where(incl, G[2 * C:4 * C, 0:2 * C], 0.0))
        u["a_rk"] = bf(jnp.where(incl, G[2 * C:4 * C, 2 * C:4 * C], 0.0))
        u["v_bd"] = bf(stack2(u["v"]))
    for u in pre:
        xb = bf(u["a_ab"])
        u["P"] = eye + u.pop("a_ab")
        u["X"] = _dot(xb, xb)
        u["M0"] = _dot(u["a_ak"], u["v_bd"])
    for _ in range(int(math.log2(C)) - 2):
        for u in pre:
            Wm = _dot(bf(u["X"]), bf(jnp.concatenate([u["P"], u["X"]], axis=1)))
            u["P"] = u["P"] + Wm[:, 0:LANES]
            u["X"] = Wm[:, LANES:2 * LANES]
    for u in pre:
        u["P"] = bf(u["P"] + _dot(bf(u.pop("X")), bf(u["P"])))
    for u in pre:
        u["M1"] = _dot(u["P"], bf(u.pop("M0")))
        u["Q"] = bf(_dot(u["a_rb"], u["P"]))
        u["PtB"] = _dot_tn(u["P"], bf(stack2(u["Bh"])))
    for u in pre:
        M1 = u.pop("M1")
        u["Y0"] = _dot(jnp.concatenate([u["a_rb"], u["a_rk"]], axis=1),
                       jnp.concatenate([bf(M1), u["v_bd"]], axis=0))
        u["Tm"] = bf(_dot_tn(bf(u["at_bd"]), bf(u.pop("PtB"))))
        m1_pair = M1[0:C] + M1[C:2 * C]
        cst = _dot_tn(bf(jnp.concatenate([m1_pair, u["v"]], axis=0)),
                      bf(jnp.concatenate([u["Bh"], u["Kh"]], axis=0)))
        u["cst"] = jnp.where(same, cst, 0.0)
        u["ar"] = bf(jnp.concatenate([u["At"], u["Rt"]], axis=0))

    states = [state_ref[p] for p in range(n_pair)]
    for c in range(n_chunk):
        rows = slice(c * C, (c + 1) * C)
        us = pre[c * n_pair:(c + 1) * n_pair]
        sbs = [bf(S) for S in states]
        zs = [_dot_nt(u["ar"], sb) for u, sb in zip(us, sbs)]
        new_states = [S * u["wc"] + _dot(sb, u["Tm"]) + u["cst"] for u, S, sb in zip(us, states, sbs)]
        ybds = [stack2(Z[C:2 * C]) + _dot(u["Q"], bf(stack2(Z[0:C]))) + u["Y0"] for u, Z in zip(us, zs)]
        ys = [y_bd[0:C] + y_bd[C:2 * C] for y_bd in ybds]
        states = new_states
        y = jnp.concatenate(ys, axis=1)
        mu = head_sum(y) * (1.0 / HEAD_DIM)
        yc = y - mu
        var = head_sum(yc * yc) * (1.0 / HEAD_DIM)
        yn = yc * lax.rsqrt(var + RWKV_GN_EPS) * ln_w + ln_b
        o_ref[rows, :] = ((yn + bonus[rows]) * g[rows]).astype(BF16)
    for p in range(n_pair):
        state_ref[p] = states[p]


def _rwkv(rw, vecs, w2, a2, g2, B, S, lb):
    nt = S // lb
    return pl.pallas_call(
        functools.partial(_rwkv_kernel, n_chunk=lb // RWKV_CHUNK),
        grid=(B, nt),
        in_specs=[pl.BlockSpec((lb, RWKV_PROJ), lambda b, j: (b * nt + j, 0)),
                  pl.BlockSpec((8, RWKV_WIDTH), lambda b, j: (0, 0)),
                  pl.BlockSpec((LANES, RWKV_WIDTH), lambda b, j: (0, 0)),
                  pl.BlockSpec((LANES, RWKV_WIDTH), lambda b, j: (0, 0)),
                  pl.BlockSpec((LANES, RWKV_WIDTH), lambda b, j: (0, 0))],
        out_specs=pl.BlockSpec((lb, RWKV_WIDTH), lambda b, j: (b * nt + j, 0)),
        out_shape=jax.ShapeDtypeStruct((B * S, RWKV_WIDTH), BF16),
        scratch_shapes=[pltpu.VMEM((RWKV_WIDTH // LANES, LANES, LANES), F32)],
        compiler_params=pltpu.CompilerParams(dimension_semantics=("arbitrary", "arbitrary"),
                                             vmem_limit_bytes=VMEM_LIMIT),
        name="rwkv",
    )(rw, vecs, w2, a2, g2)


def _mix_kernel(at_ref, rk_ref, x_ref, mod_ref, wo_a_ref, wo_r_ref, ln_ref, wr_ref, br_ref,
                x1_ref, h2_ref, ti_ref, gt_ref, rank_ref, cnt_ref, tb_ref, base_ref):
    first = (pl.program_id(0) == 0) & (pl.program_id(1) == 0)

    @pl.when(first)
    def _():
        base_ref[...] = jnp.zeros_like(base_ref)

    mod = mod_ref[0]
    y = _dot(at_ref[...], wo_a_ref[...]) + _dot(rk_ref[...], wo_r_ref[...])
    x = x_ref[...]
    tm = x.shape[0]
    x1 = _layer_norm(DEEPNORM_ALPHA * x + (1.0 + mod[2:3, :]) * y) * ln_ref[0:1, :] + ln_ref[1:2, :]
    h2 = _layer_norm(x1) * (1.0 + mod[4:5, :]) + mod[3:4, :]
    x1_ref[...] = x1
    h_hi = h2.astype(BF16)
    h2_ref[...] = h_hi

    h_lo = (h2 - h_hi.astype(F32)).astype(BF16)
    part = _dot(h_hi, wr_ref[...])
    logits = part[:, 0:LANES] + part[:, LANES:2 * LANES] + _dot(h_lo, wr_ref[:, 0:LANES]) + br_ref[...]
    lt = jnp.transpose(logits)[0:N_EXPERTS, :]
    erow = lax.broadcasted_iota(jnp.int32, (N_EXPERTS, tm), 0).astype(F32)
    cur = lt
    vals, idxs = [], []
    for _ in range(TOP_K):
        m = jnp.max(cur, axis=0, keepdims=True)
        idx = jnp.min(jnp.where(cur == m, erow, float(N_EXPERTS)), axis=0, keepdims=True)
        vals.append(m)
        idxs.append(idx)
        cur = jnp.where(erow == idx, -jnp.inf, cur)
    tv = jnp.concatenate(vals, axis=0)
    e = jnp.exp(tv - tv[0:1, :])
    gt_ref[...] = e / jnp.sum(e, axis=0, keepdims=True)
    ti_ref[...] = jnp.concatenate(idxs, axis=0).astype(jnp.int32)

    onehot = jnp.zeros((N_EXPERTS, tm), F32)
    for idx in idxs:
        onehot = onehot + (erow == idx).astype(F32)
    before = (lax.broadcasted_iota(jnp.int32, (tm, tm), 0)
              < lax.broadcasted_iota(jnp.int32, (tm, tm), 1)).astype(BF16)
    tot = base_ref[:, 0:1] + _dot(onehot.astype(BF16), before)
    ranks = [jnp.sum(jnp.where(erow == idx, tot, 0.0), axis=0, keepdims=True) for idx in idxs]
    rank_ref[...] = jnp.concatenate(ranks, axis=0).astype(jnp.int32)
    tb_ref[0] = base_ref[...].astype(jnp.int32)
    base_ref[...] = base_ref[...] + jnp.sum(onehot, axis=1, keepdims=True)
    cnt_ref[...] = base_ref[...].astype(jnp.int32)


def _mix(attn_out, rwkv_out, x2d, mod, wo_a, wo_r, ln1, w_router, b_router, B, S, tm):
    D = x2d.shape[1]
    nt = S // tm
    T = B * S
    tok = lambda b, j: (b * nt + j, 0)
    col = lambda b, j: (0, b * nt + j)
    fixed = lambda b, j: (0, 0)
    return pl.pallas_call(
        _mix_kernel,
        grid=(B, nt),
        in_specs=[pl.BlockSpec((tm, ATTN_WIDTH), tok),
                  pl.BlockSpec((tm, RWKV_WIDTH), tok),
                  pl.BlockSpec((tm, D), tok),
                  pl.BlockSpec((1, 6, D), lambda b, j: (b, 0, 0)),
                  pl.BlockSpec((ATTN_WIDTH, D), fixed),
                  pl.BlockSpec((RWKV_WIDTH, D), fixed),
                  pl.BlockSpec((2, D), fixed),
                  pl.BlockSpec((D, 2 * LANES), fixed),
                  pl.BlockSpec((1, LANES), fixed)],
        out_specs=[pl.BlockSpec((tm, D), tok),
                   pl.BlockSpec((tm, D), tok),
                   pl.BlockSpec((TOP_K, tm), col),
                   pl.BlockSpec((TOP_K, tm), col),
                   pl.BlockSpec((TOP_K, tm), col),
                   pl.BlockSpec((N_EXPERTS, LANES), fixed),
                   pl.BlockSpec((1, N_EXPERTS, LANES), lambda b, j: (b * nt + j, 0, 0))],
        out_shape=[jax.ShapeDtypeStruct((T, D), F32),
                   jax.ShapeDtypeStruct((T, D), BF16),
                   jax.ShapeDtypeStruct((TOP_K, T), jnp.int32),
                   jax.ShapeDtypeStruct((TOP_K, T), F32),
                   jax.ShapeDtypeStruct((TOP_K, T), jnp.int32),
                   jax.ShapeDtypeStruct((N_EXPERTS, LANES), jnp.int32),
                   jax.ShapeDtypeStruct((T // tm, N_EXPERTS, LANES), jnp.int32)],
        scratch_shapes=[pltpu.VMEM((N_EXPERTS, LANES), F32)],
        compiler_params=pltpu.CompilerParams(dimension_semantics=("arbitrary", "arbitrary"),
                                             vmem_limit_bytes=VMEM_LIMIT),
        name="mix",
    )(attn_out, rwkv_out, x2d, mod, wo_a, wo_r, ln1, w_router, b_router)


RUN_PIECES = tuple(2 ** b for b in range(int(math.log2(MOE_TILE)), -1, -1))
SUBLANES = 8


def _to_tiles(ref, x):
    n = x.shape[0]
    for c in range(SUBLANES):
        ref[pl.ds(c, n, stride=SUBLANES), :] = x[:, c * LANES:(c + 1) * LANES]


def _from_tiles(ref):
    n = ref.shape[0] // SUBLANES
    return jnp.concatenate([ref[pl.ds(c, n, stride=SUBLANES), :] for c in range(SUBLANES)], axis=1)


def _run_copies(n, local, local_start, remote, remote_start, sem, to_remote):
    off = 0
    for piece in RUN_PIECES:
        take = (n & piece) != 0

        @pl.when(take)
        def _(off=off, piece=piece):
            lo = pl.multiple_of((local_start + off) * SUBLANES, SUBLANES)
            ro = pl.multiple_of((remote_start + off) * SUBLANES, SUBLANES)
            loc = local.at[pl.ds(lo, piece * SUBLANES)]
            rem = remote.at[pl.ds(ro, piece * SUBLANES)]
            src, dst = (loc, rem) if to_remote else (rem, loc)
            pltpu.make_async_copy(src, dst, sem).start()

        off = off + (n & piece)


def _dispatch_kernel(tcnt_ref, lstart_ref, gstart_ref, pad_ref, pad_start_ref, n_used_ref, lpos_ref, h2_ref,
                     xs_ref, xbuf, zbuf, sems):
    i = pl.program_id(0)
    tm = h2_ref.shape[0]
    n_loc = TOP_K * tm
    n_blocks = xs_ref.shape[0] // (MOE_BLOCK * SUBLANES)
    zero_sem = sems.at[2]

    @pl.when(i == 0)
    def _():
        zbuf[...] = jnp.zeros_like(zbuf)

        def zero_pad(e, carry):
            _run_copies(pad_ref[e], zbuf, 0, xs_ref, pad_start_ref[e], zero_sem, True)
            return carry

        def zero_tail(b, carry):
            @pl.when(b >= n_used_ref[0])
            def _():
                start = pl.multiple_of(b * (MOE_BLOCK * SUBLANES), MOE_BLOCK * SUBLANES)
                pltpu.make_async_copy(zbuf, xs_ref.at[pl.ds(start, MOE_BLOCK * SUBLANES)], zero_sem).start()
            return carry

        lax.fori_loop(0, N_EXPERTS, zero_pad, 0)
        lax.fori_loop(n_blocks - N_EXPERTS, n_blocks, zero_tail, 0)

    slot = lax.broadcasted_iota(jnp.int32, (n_loc, tm), 0)
    lpos = lpos_ref[...]
    perm = jnp.zeros((n_loc, tm), F32)
    for k in range(TOP_K):
        perm = perm + (slot == lpos[k:k + 1, :]).astype(F32)
    perm = perm.astype(BF16)

    def wait_tile(s):
        pltpu.make_async_copy(xbuf.at[s], xs_ref.at[pl.ds(0, n_loc * SUBLANES)], sems.at[s]).wait()

    for s in range(2):
        @pl.when((i % 2 == s) & (i >= 2))
        def _(s=s):
            wait_tile(s)

    for s in range(2):
        @pl.when(i % 2 == s)
        def _(s=s):
            _to_tiles(xbuf.at[s], _dot(perm, h2_ref[...]))

            def issue(e, carry):
                idx = i * N_EXPERTS + e
                _run_copies(tcnt_ref[idx], xbuf.at[s], lstart_ref[idx], xs_ref, gstart_ref[idx], sems.at[s], True)
                return carry

            lax.fori_loop(0, N_EXPERTS, issue, 0)

    @pl.when(i == pl.num_programs(0) - 1)
    def _():
        for s in range(2):
            @pl.when((i % 2 == s) | (i >= 1))
            def _(s=s):
                wait_tile(s)
        n_zero = N_EXPERTS * MOE_BLOCK * SUBLANES
        pltpu.make_async_copy(xs_ref.at[pl.ds(0, n_zero)], xs_ref.at[pl.ds(0, n_zero)], zero_sem).wait()


def _dispatch(tcnt, lstart, gstart, pad, pad_start, n_used, lpos, h2, n_rows, tm):
    T, D = h2.shape
    grid_spec = pltpu.PrefetchScalarGridSpec(
        num_scalar_prefetch=6,
        grid=(T // tm,),
        in_specs=[pl.BlockSpec((TOP_K, tm), lambda i, *_: (0, i)),
                  pl.BlockSpec((tm, D), lambda i, *_: (i, 0))],
        out_specs=pl.BlockSpec(memory_space=pl.ANY),
        scratch_shapes=[pltpu.VMEM((2, TOP_K * tm * SUBLANES, LANES), F32),
                        pltpu.VMEM((MOE_BLOCK * SUBLANES, LANES), F32),
                        pltpu.SemaphoreType.DMA((3,))],
    )
    return pl.pallas_call(
        _dispatch_kernel,
        grid_spec=grid_spec,
        out_shape=jax.ShapeDtypeStruct((n_rows * SUBLANES, LANES), F32),
        compiler_params=pltpu.CompilerParams(dimension_semantics=("arbitrary",),
                                             vmem_limit_bytes=VMEM_LIMIT),
        name="dispatch",
    )(tcnt, lstart, gstart, pad, pad_start, n_used, lpos, h2)


def _experts_kernel(blk_e_ref, n_used_ref, next_e_ref, xs_ref, wgu_hbm, bgu_ref, wd_hbm, bd_ref, ys_ref,
                    wgu_f32, wd_f32, wgu_bf, wd_bf, sems):
    i = pl.program_id(0)
    d_ff = wd_bf.shape[0]
    used = i < n_used_ref[0]
    e = blk_e_ref[i]
    new_expert = (i == 0) | (e != blk_e_ref[jnp.maximum(i - 1, 0)])

    def weight_copies(ex):
        return (pltpu.make_async_copy(wgu_hbm.at[ex], wgu_f32, sems.at[0]),
                pltpu.make_async_copy(wd_hbm.at[ex], wd_f32, sems.at[1]))

    @pl.when(i == 0)
    def _():
        for cp in weight_copies(e):
            cp.start()

    @pl.when(used & new_expert)
    def _():
        for cp in weight_copies(e):
            cp.wait()
        wgu_bf[...] = wgu_f32[...].astype(BF16)
        wd_bf[...] = wd_f32[...].astype(BF16)
        nxt = next_e_ref[i]

        @pl.when(nxt >= 0)
        def _():
            for cp in weight_copies(nxt):
                cp.start()

    @pl.when(used)
    def _():
        xb = _from_tiles(xs_ref).astype(BF16)
        gu = _dot(xb, wgu_bf[...]) + bgu_ref[0]
        gate = jnp.minimum(gu[:, :d_ff], SWIGLU_LIMIT)
        up = jnp.clip(gu[:, d_ff:], -SWIGLU_LIMIT, SWIGLU_LIMIT)
        act = (up + 1.0) * (gate * _sigmoid(SWIGLU_ALPHA * gate))
        _to_tiles(ys_ref, _dot(act.astype(BF16), wd_bf[...]) + bd_ref[0])

    @pl.when(i >= n_used_ref[0])
    def _():
        ys_ref[...] = jnp.zeros_like(ys_ref)


def _experts(blk_e, n_used, next_e, xs, wgu, bgu, wd, bd):
    d_ff, D = wd.shape[1], wd.shape[2]
    n_blocks = xs.shape[0] // (MOE_BLOCK * SUBLANES)
    blk = (MOE_BLOCK * SUBLANES, LANES)

    def last_used(i, n_used_ref):
        return jnp.minimum(i, jnp.maximum(n_used_ref[0] - 1, 0))

    def row_map(i, blk_e_ref, n_used_ref, next_e_ref):
        return (last_used(i, n_used_ref), 0)

    def exp_map(i, blk_e_ref, n_used_ref, next_e_ref):
        return (blk_e_ref[last_used(i, n_used_ref)], 0, 0)

    grid_spec = pltpu.PrefetchScalarGridSpec(
        num_scalar_prefetch=3,
        grid=(n_blocks,),
        in_specs=[pl.BlockSpec(blk, row_map),
                  pl.BlockSpec(memory_space=pl.ANY),
                  pl.BlockSpec((1, 1, 2 * d_ff), exp_map),
                  pl.BlockSpec(memory_space=pl.ANY),
                  pl.BlockSpec((1, 1, D), exp_map)],
        out_specs=pl.BlockSpec(blk, lambda i, *_: (i, 0)),
        scratch_shapes=[pltpu.VMEM((D, 2 * d_ff), F32), pltpu.VMEM((d_ff, D), F32),
                        pltpu.VMEM((D, 2 * d_ff), BF16), pltpu.VMEM((d_ff, D), BF16),
                        pltpu.SemaphoreType.DMA((2,))],
    )
    return pl.pallas_call(
        _experts_kernel,
        grid_spec=grid_spec,
        out_shape=jax.ShapeDtypeStruct(xs.shape, F32),
        compiler_params=pltpu.CompilerParams(dimension_semantics=("arbitrary",),
                                             vmem_limit_bytes=VMEM_LIMIT),
        name="experts",
    )(blk_e, n_used, next_e, xs, wgu, bgu, wd, bd)


def _combine_kernel(tcnt_ref, lstart_ref, gstart_ref, ys_ref, lpos_ref, gt_ref, x1_ref, mod_ref, ln_ref,
                    o_ref, buf, sems):
    nt = pl.num_programs(1)
    n_tiles = pl.num_programs(0) * nt
    i = pl.program_id(0) * nt + pl.program_id(1)
    tm = x1_ref.shape[0]
    n_loc = TOP_K * tm

    def fetch(tile, s):
        def issue(e, carry):
            idx = tile * N_EXPERTS + e
            _run_copies(tcnt_ref[idx], buf.at[s], lstart_ref[idx], ys_ref, gstart_ref[idx], sems.at[s], False)
            return carry

        lax.fori_loop(0, N_EXPERTS, issue, 0)

    @pl.when(i == 0)
    def _():
        fetch(i, 0)

    for s in range(2):
        @pl.when(((i + 1) % 2 == s) & (i + 1 < n_tiles))
        def _(s=s):
            fetch(i + 1, s)

    slot = lax.broadcasted_iota(jnp.int32, (tm, n_loc), 1)
    lpos = lpos_ref[...]
    gt = gt_ref[...]
    pick = jnp.zeros((tm, n_loc), F32)
    for k in range(TOP_K):
        pick = pick + jnp.where(slot == lpos[:, k:k + 1], gt[:, k:k + 1], 0.0)
    pick = pick.astype(BF16)
    for s in range(2):
        @pl.when(i % 2 == s)
        def _(s=s):
            pltpu.make_async_copy(ys_ref.at[pl.ds(0, n_loc * SUBLANES)], buf.at[s], sems.at[s]).wait()
            y = _dot(pick, _from_tiles(buf.at[s]).astype(BF16))
            z = DEEPNORM_ALPHA * x1_ref[...] + (1.0 + mod_ref[0][5:6, :]) * y
            o_ref[...] = _layer_norm(z) * ln_ref[0:1, :] + ln_ref[1:2, :]


def _combine(tcnt, lstart, gstart, ys, lpos_t, gates_t, x1, mod, ln2, B, S, tm):
    T, D = x1.shape
    nt = S // tm
    tok = lambda b, j, *_: (b * nt + j, 0)
    grid_spec = pltpu.PrefetchScalarGridSpec(
        num_scalar_prefetch=3,
        grid=(B, nt),
        in_specs=[pl.BlockSpec(memory_space=pl.ANY),
                  pl.BlockSpec((tm, TOP_K), tok),
                  pl.BlockSpec((tm, TOP_K), tok),
                  pl.BlockSpec((tm, D), tok),
                  pl.BlockSpec((1, 6, D), lambda b, j, *_: (b, 0, 0)),
                  pl.BlockSpec((2, D), lambda b, j, *_: (0, 0))],
        out_specs=pl.BlockSpec((tm, D), tok),
        scratch_shapes=[pltpu.VMEM((2, TOP_K * tm * SUBLANES, LANES), F32),
                        pltpu.SemaphoreType.DMA((2,))],
    )
    return pl.pallas_call(
        _combine_kernel,
        grid_spec=grid_spec,
        out_shape=jax.ShapeDtypeStruct((T, D), F32),
        compiler_params=pltpu.CompilerParams(dimension_semantics=("arbitrary", "arbitrary"),
                                             vmem_limit_bytes=VMEM_LIMIT),
        name="combine",
    )(tcnt, lstart, gstart, ys, lpos_t, gates_t, x1, mod, ln2)


def _pad_rows(w, rows):
    return jnp.pad(w, ((0, rows - w.shape[0]), (0, 0)))


def _pad_cols(w, cols):
    return jnp.pad(w, ((0, 0), (0, cols - w.shape[1])))


def _layer(x, c, positions, w_ada, b_ada, w_in, shift_mu, rwkv_w0, rwkv_w2, rwkv_a0, rwkv_a2, rwkv_g2,
           rwkv_k_k, rwkv_k_a, rwkv_r_k, rwkv_ln_w, rwkv_ln_b, attn_sinks, w_out, ln1_g, ln1_b,
           w_router, b_router, w_gate_up, b_gate_up, w_down, b_down, ln2_g, ln2_b):
    B, S, D = x.shape
    T = B * S

    q0, k0, v0 = 0, ATTN_WIDTH, ATTN_WIDTH + KV_WIDTH
    r0 = ATTN_WIDTH + 2 * KV_WIDTH
    heads = lambda base: [w_in[:, base + h * HEAD_DIM: base + (h + 1) * HEAD_DIM] for h in range(N_KV_HEADS)]
    dup = lambda hs: [w for w in hs for _ in range(2)]
    w_attn = jnp.concatenate([w_in[:, q0:q0 + ATTN_WIDTH]] + dup(heads(k0)) + dup(heads(v0)), axis=1).astype(BF16)
    lora0 = r0 + 3 * RWKV_WIDTH
    lora = (DECAY_LORA, AAA_LORA, GATE_LORA)
    pieces_w = [w_in[:, r0:lora0]]
    pieces_mu = [shift_mu[None, 0:3 * RWKV_WIDTH]]
    off = lora0
    for n in lora:
        pieces_w.append(_pad_cols(w_in[:, off:off + n], LANES))
        pieces_mu.append(_pad_cols(shift_mu[None, off - r0:off - r0 + n], LANES))
        off += n
    w_rwkv = jnp.concatenate(pieces_w, axis=1).astype(BF16)
    mu = jnp.concatenate(pieces_mu, axis=1)
    inv_freq = ROPE_THETA ** (-jnp.arange(0, ROT_DIM, 2, dtype=F32) / ROT_DIM)
    lane_p = jnp.arange(LANES) % HEAD_DIM
    n_freq = ROT_DIM // 2
    rot_tab = jnp.zeros((8, LANES), F32)
    rot_tab = rot_tab.at[0].set(jnp.where(lane_p < ROT_DIM, 0.0, 1.0))
    rot_tab = rot_tab.at[1].set(jnp.where(lane_p < n_freq, -1.0, 0.0))
    rot_tab = rot_tab.at[2].set(jnp.where((lane_p >= n_freq) & (lane_p < ROT_DIM), 1.0, 0.0))
    freq_tab = jnp.broadcast_to(inv_freq[:, None], (n_freq, LANES))
    lane_freq = (jnp.arange(n_freq)[:, None] == (lane_p % n_freq)[None, :]) & (lane_p < ROT_DIM)[None, :]
    zeros = jnp.zeros_like(lane_freq)
    expand = jnp.concatenate([jnp.concatenate([lane_freq, zeros], axis=1),
                              jnp.concatenate([zeros, lane_freq], axis=1)], axis=0).astype(BF16)
    vecs = jnp.stack([rwkv_w0, rwkv_a0, rwkv_k_k, rwkv_k_a, rwkv_r_k.reshape(-1), rwkv_ln_w, rwkv_ln_b,
                      jnp.zeros_like(rwkv_w0)])
    w2 = _pad_rows(rwkv_w2, LANES).astype(BF16)
    a2 = _pad_rows(rwkv_a2, LANES).astype(BF16)
    g2 = _pad_rows(rwkv_g2, LANES).astype(BF16)
    wo_a = w_out[:ATTN_WIDTH].astype(BF16)
    wo_r = w_out[ATTN_WIDTH:].astype(BF16)
    w_r_hi = w_router.astype(BF16)
    w_r_lo = (w_router - w_r_hi.astype(F32)).astype(BF16)
    w_r = jnp.concatenate([_pad_cols(w_r_hi, LANES), _pad_cols(w_r_lo, LANES)], axis=1)
    b_r = jnp.concatenate([b_router, jnp.full((LANES - N_EXPERTS,), NEG_INF, F32)])[None, :]

    mod = _mod(c, w_ada, b_ada).reshape(B, 6, D)
    qkv, rw = _inproj(x, positions, mod, w_attn, w_rwkv, mu, rot_tab, freq_tab, expand, min(INPROJ_TILE, S))
    attn_out = _attention(qkv, attn_sinks, B, S, min(ATTN_STEP_BLOCKS, S // ATTN_BLOCK))
    rwkv_out = _rwkv(rw, vecs, w2, a2, g2, B, S, min(RWKV_STEP, S))

    mtile = min(MOE_TILE, S)
    x1, h2, top_i, gates, rank, cnt, tbase = _mix(attn_out, rwkv_out, x.reshape(T, D), mod, wo_a, wo_r,
                                                   jnp.stack([ln1_g, ln1_b]), w_r, b_r, B, S, mtile)

    counts = cnt[:, 0]
    padded = (counts + MOE_BLOCK - 1) // MOE_BLOCK * MOE_BLOCK
    pend = jnp.cumsum(padded)
    pstart = pend - padded
    n_blocks = T * TOP_K // MOE_BLOCK + N_EXPERTS
    blk_row = jnp.arange(n_blocks, dtype=jnp.int32) * MOE_BLOCK
    blk_e = jnp.minimum(jnp.sum((pend[None, :] <= blk_row[:, None]).astype(jnp.int32), axis=1), N_EXPERTS - 1)
    n_used = (pend[-1:] // MOE_BLOCK).astype(jnp.int32)
    tb = tbase[:, :, 0]
    tcnt = jnp.concatenate([tb[1:], counts[None]], axis=0) - tb
    lstart = jnp.cumsum(tcnt, axis=1) - tcnt
    gstart = pstart[None, :] + tb
    shift = jnp.repeat(jnp.transpose(lstart - tb), mtile, axis=1)
    experts = jnp.arange(N_EXPERTS, dtype=jnp.int32)
    lpos = rank + jnp.sum(jnp.where(top_i[None] == experts[:, None, None], shift[:, None, :], 0), axis=0)
    flat = lambda a: a.reshape(-1).astype(jnp.int32)

    xs = _dispatch(flat(tcnt), flat(lstart), flat(gstart), flat(padded - counts), flat(pstart + counts), n_used,
                   lpos, h2, n_blocks * MOE_BLOCK, mtile)
    later_with_rows = (experts[None, :] > experts[:, None]) & (counts[None, :] > 0)
    next_of = jnp.min(jnp.where(later_with_rows, experts[None, :], N_EXPERTS), axis=1)
    next_of = jnp.where(next_of < N_EXPERTS, next_of, -1)
    next_e = jnp.sum(jnp.where(blk_e[:, None] == experts[None, :], next_of[None, :], 0), axis=1).astype(jnp.int32)
    ys = _experts(blk_e, n_used, next_e, xs, w_gate_up, b_gate_up[:, None, :], w_down, b_down[:, None, :])
    out = _combine(flat(tcnt), flat(lstart), flat(gstart), ys, jnp.transpose(lpos), jnp.transpose(gates), x1, mod,
                   jnp.stack([ln2_g, ln2_b]), B, S, mtile)
    return out.reshape(B, S, D)


def kernel(x, c, positions, w_ada, b_ada, w_in, shift_mu, rwkv_w0, rwkv_w2, rwkv_a0, rwkv_a2, rwkv_g2,
           rwkv_k_k, rwkv_k_a, rwkv_r_k, rwkv_ln_w, rwkv_ln_b, attn_sinks, w_out, ln1_g, ln1_b,
           w_router, b_router, w_gate_up, b_gate_up, w_down, b_down, ln2_g, ln2_b):
    for l in range(DEPTH):
        x = _layer(x, c, positions, w_ada[l], b_ada[l], w_in[l], shift_mu[l], rwkv_w0[l], rwkv_w2[l],
                   rwkv_a0[l], rwkv_a2[l], rwkv_g2[l], rwkv_k_k[l], rwkv_k_a[l], rwkv_r_k[l], rwkv_ln_w[l],
                   rwkv_ln_b[l], attn_sinks[l], w_out[l], ln1_g[l], ln1_b[l], w_router[l], b_router[l],
                   w_gate_up[l], b_gate_up[l], w_down[l], b_down[l], ln2_g[l], ln2_b[l])
    return x
```

```python
import functools
import math

import jax
import jax.numpy as jnp
from jax import lax
from jax.experimental import pallas as pl
from jax.experimental.pallas import tpu as pltpu

F32 = jnp.float32
BF16 = jnp.bfloat16

HEAD_DIM = 64
N_ATTN_HEADS = 8
N_KV_HEADS = 2
N_RWKV_HEADS = 8
ATTN_WIDTH = N_ATTN_HEADS * HEAD_DIM
KV_WIDTH = N_KV_HEADS * HEAD_DIM
RWKV_WIDTH = N_RWKV_HEADS * HEAD_DIM
ATTN_BLOCK = 128
ROT_DIM = HEAD_DIM // 4
ROPE_THETA = 500000.0
DECAY_LORA = 32
AAA_LORA = 32
GATE_LORA = 96
N_EXPERTS = 32
TOP_K = 4
SWIGLU_LIMIT = 7.0
SWIGLU_ALPHA = 1.702
LN_EPS = 1e-5
RWKV_GN_EPS = 64e-5
NEG_INF = -1e30
DEPTH = 1
DEEPNORM_ALPHA = (2 * DEPTH) ** 0.25

LANES = 128
RWKV_CHUNK = 64
RWKV_STEP = 512
INPROJ_TILE = 512
ATTN_STEP_BLOCKS = 2
MOE_BLOCK = 512
MOE_TILE = 256
ATTN_PROJ = ATTN_WIDTH + 4 * KV_WIDTH
RWKV_PROJ = 3 * RWKV_WIDTH + 3 * LANES
VMEM_LIMIT = 48 * 1024 * 1024


def _dot(a, b):
    return jnp.dot(a, b, preferred_element_type=F32)


def _dot_nt(a, b):
    return lax.dot_general(a, b, (((1,), (1,)), ((), ())), preferred_element_type=F32)


def _dot_tn(a, b):
    return lax.dot_general(a, b, (((0,), (0,)), ((), ())), preferred_element_type=F32)


def _split3(x):
    h = x.astype(BF16)
    r1 = x - h.astype(F32)
    m = r1.astype(BF16)
    lo = (r1 - m.astype(F32)).astype(BF16)
    return h, m, lo


def _dot_exact_lhs(m_bf16, x):
    h, m, lo = _split3(x)
    return _dot(m_bf16, h) + _dot(m_bf16, m) + _dot(m_bf16, lo)


def _layer_norm(x):
    mu = jnp.mean(x, axis=-1, keepdims=True)
    xc = x - mu
    var = jnp.mean(xc * xc, axis=-1, keepdims=True)
    return xc * lax.rsqrt(var + LN_EPS)


def _sigmoid(x):
    return 1.0 / (1.0 + jnp.exp(-x))


def _mod_kernel(c_ref, w_ref, b_ref, o_ref):
    c = c_ref[...]
    s = c * _sigmoid(c)
    o_ref[...] = jnp.dot(s, w_ref[...], preferred_element_type=F32,
                         precision=lax.Precision.HIGHEST) + b_ref[...]


def _mod(c, w_ada, b_ada):
    B, D = c.shape
    n = w_ada.shape[1] // D
    return pl.pallas_call(
        _mod_kernel,
        grid=(n,),
        in_specs=[pl.BlockSpec((B, D), lambda i: (0, 0)),
                  pl.BlockSpec((D, D), lambda i: (0, i)),
                  pl.BlockSpec((1, D), lambda i: (0, i))],
        out_specs=pl.BlockSpec((B, D), lambda i: (0, i)),
        out_shape=jax.ShapeDtypeStruct((B, n * D), F32),
        compiler_params=pltpu.CompilerParams(dimension_semantics=("arbitrary",),
                                             vmem_limit_bytes=VMEM_LIMIT),
        name="mod",
    )(c, w_ada, b_ada.reshape(1, -1))


def _inproj_kernel(x_ref, pos_ref, mod_ref, wa_ref, wr_ref, mu_ref, rt_ref, fq_ref, ex_ref,
                   qkv_ref, rw_ref, carry_ref):
    j = pl.program_id(1)
    x = x_ref[0]
    tm = x.shape[0]
    mod = mod_ref[0]
    h = _layer_norm(x) * (1.0 + mod[1:2, :]) + mod[0:1, :]
    hb = h.astype(BF16)

    pa = _dot(hb, wa_ref[...])
    ang = pos_ref[0].astype(F32) * fq_ref[:, 0:1]
    pieces = _split3(jnp.concatenate([jnp.cos(ang), jnp.sin(ang)], axis=0))
    trig = _dot_tn(pieces[0], ex_ref[...]) + _dot_tn(pieces[1], ex_ref[...]) + _dot_tn(pieces[2], ex_ref[...])
    cs = trig[:, 0:LANES] + rt_ref[0:1, :]
    sn = trig[:, LANES:2 * LANES]
    m_lo = rt_ref[1:2, :]
    m_hi = rt_ref[2:3, :]
    n_q = ATTN_WIDTH // LANES
    n_rot = (ATTN_WIDTH + 2 * KV_WIDTH) // LANES
    for ch in range(n_rot):
        t = pa[:, ch * LANES:(ch + 1) * LANES]
        if ch < n_q:
            t = t * (1.0 / math.sqrt(HEAD_DIM))
        up = pltpu.roll(t, LANES - ROT_DIM // 2, 1)
        dn = pltpu.roll(t, ROT_DIM // 2, 1)
        o = t * cs + sn * (m_lo * up + m_hi * dn)
        qkv_ref[:, ch * LANES:(ch + 1) * LANES] = o.astype(BF16)
    qkv_ref[:, n_rot * LANES:] = pa[:, n_rot * LANES:].astype(BF16)

    pr = _dot(hb, wr_ref[...])
    prev = pltpu.roll(pr, 1, 0)
    row = lax.broadcasted_iota(jnp.int32, (tm, 1), 0)
    carry = jnp.where(j == 0, 0.0, carry_ref[...])
    prev = jnp.where(row == 0, carry, prev)
    carry_ref[...] = pr[tm - 1:tm, :]
    rw_ref[...] = pr + (prev - pr) * mu_ref[...]


def _inproj(x, positions, mod, w_attn, w_rwkv, mu, rot_tab, freq_tab, expand, tm):
    B, S, D = x.shape
    nt = S // tm
    return pl.pallas_call(
        _inproj_kernel,
        grid=(B, nt),
        in_specs=[pl.BlockSpec((1, tm, D), lambda b, j: (b, j, 0)),
                  pl.BlockSpec((1, 1, tm), lambda b, j: (b, 0, j)),
                  pl.BlockSpec((1, 6, D), lambda b, j: (b, 0, 0)),
                  pl.BlockSpec((D, ATTN_PROJ), lambda b, j: (0, 0)),
                  pl.BlockSpec((D, RWKV_PROJ), lambda b, j: (0, 0)),
                  pl.BlockSpec((1, RWKV_PROJ), lambda b, j: (0, 0)),
                  pl.BlockSpec((8, LANES), lambda b, j: (0, 0)),
                  pl.BlockSpec(freq_tab.shape, lambda b, j: (0, 0)),
                  pl.BlockSpec(expand.shape, lambda b, j: (0, 0))],
        out_specs=[pl.BlockSpec((tm, ATTN_PROJ), lambda b, j: (b * nt + j, 0)),
                   pl.BlockSpec((tm, RWKV_PROJ), lambda b, j: (b * nt + j, 0))],
        out_shape=[jax.ShapeDtypeStruct((B * S, ATTN_PROJ), BF16),
                   jax.ShapeDtypeStruct((B * S, RWKV_PROJ), F32)],
        scratch_shapes=[pltpu.VMEM((1, RWKV_PROJ), F32)],
        compiler_params=pltpu.CompilerParams(dimension_semantics=("arbitrary", "arbitrary"),
                                             vmem_limit_bytes=VMEM_LIMIT),
        name="inproj",
    )(x, positions.reshape(B, 1, S), mod, w_attn, w_rwkv, mu, rot_tab, freq_tab, expand)


def _attn_kernel(q_ref, kc_ref, kp_ref, vc_ref, vp_ref, sink_ref, o_ref):
    i = pl.program_id(1)
    blk = ATTN_BLOCK
    n_sub = q_ref.shape[0] // blk
    qi = lax.broadcasted_iota(jnp.int32, (blk, 2 * blk), 0)
    kj = lax.broadcasted_iota(jnp.int32, (blk, 2 * blk), 1)
    band = (kj > qi) & (kj <= qi + blk)
    first = band & ((kj >= blk) | (i > 0))
    lane = lax.broadcasted_iota(jnp.int32, (1, LANES), 1)
    lo = (lane < HEAD_DIM).astype(BF16)
    hi = (lane >= HEAD_DIM).astype(BF16)
    halves = {}
    for u in range(n_sub):
        for g in range(N_KV_HEADS):
            sl = slice(g * LANES, (g + 1) * LANES)
            prev_k = kp_ref[:, sl] if u == 0 else kc_ref[(u - 1) * blk:u * blk, sl]
            prev_v = vp_ref[:, sl] if u == 0 else vc_ref[(u - 1) * blk:u * blk, sl]
            kcat = jnp.concatenate([prev_k, kc_ref[u * blk:(u + 1) * blk, sl]], axis=0)
            vcat = jnp.concatenate([prev_v, vc_ref[u * blk:(u + 1) * blk, sl]], axis=0)
            halves[u, g] = ((kcat * lo, vcat * lo), (kcat * hi, vcat * hi))
    units = [(u, c, half) for u in range(n_sub) for c in range(ATTN_WIDTH // LANES) for half in range(2)]
    scores = [_dot_nt(q_ref[u * blk:(u + 1) * blk, c * LANES:(c + 1) * LANES], halves[u, c // 2][half][0])
              for u, c, half in units]
    probs, denoms = [], []
    for (u, c, half), s in zip(units, scores):
        sink = sink_ref[2 * c + half]
        s = jnp.where(first if u == 0 else band, s, NEG_INF)
        m = jnp.maximum(jnp.max(s, axis=-1, keepdims=True), sink)
        p = jnp.exp(s - m)
        denoms.append(jnp.sum(p, axis=-1, keepdims=True) + jnp.exp(sink - m))
        probs.append(p.astype(BF16))
    outs = [_dot(p, halves[u, c // 2][half][1]) / d for (u, c, half), p, d in zip(units, probs, denoms)]
    for n, (u, c, half) in enumerate(units):
        if half == 0:
            o_ref[u * blk:(u + 1) * blk, c * LANES:(c + 1) * LANES] = (outs[n] + outs[n + 1]).astype(BF16)


def _attention(qkv, sinks, B, S, n_sub):
    blk = ATTN_BLOCK
    step = n_sub * blk
    ns = S // step
    kcol = ATTN_WIDTH // (2 * KV_WIDTH)
    vcol = kcol + 1
    cur = lambda col: (lambda b, i: (b * ns + i, col))
    prv = lambda col: (lambda b, i: (jnp.maximum((b * ns + i) * n_sub - 1, 0), col))
    return pl.pallas_call(
        _attn_kernel,
        grid=(B, ns),
        in_specs=[pl.BlockSpec((step, ATTN_WIDTH), lambda b, i: (b * ns + i, 0)),
                  pl.BlockSpec((step, 2 * KV_WIDTH), cur(kcol)),
                  pl.BlockSpec((blk, 2 * KV_WIDTH), prv(kcol)),
                  pl.BlockSpec((step, 2 * KV_WIDTH), cur(vcol)),
                  pl.BlockSpec((blk, 2 * KV_WIDTH), prv(vcol)),
                  pl.BlockSpec(memory_space=pltpu.SMEM)],
        out_specs=pl.BlockSpec((step, ATTN_WIDTH), lambda b, i: (b * ns + i, 0)),
        out_shape=jax.ShapeDtypeStruct((B * S, ATTN_WIDTH), BF16),
        compiler_params=pltpu.CompilerParams(dimension_semantics=("arbitrary", "arbitrary"),
                                             vmem_limit_bytes=VMEM_LIMIT),
        name="attn",
    )(qkv, qkv, qkv, qkv, qkv, sinks)


def _rwkv_kernel(rw_ref, vec_ref, w2_ref, a2_ref, g2_ref, o_ref, state_ref, *, n_chunk):
    j = pl.program_id(1)
    C = RWKV_CHUNK
    W = RWKV_WIDTH
    n_pair = W // LANES

    @pl.when(j == 0)
    def _():
        state_ref[...] = jnp.zeros_like(state_ref)

    w0 = vec_ref[0:1, :]
    a0 = vec_ref[1:2, :]
    k_k = vec_ref[2:3, :]
    k_a = vec_ref[3:4, :]
    r_k = vec_ref[4:5, :]
    ln_w = vec_ref[5:6, :]
    ln_b = vec_ref[6:7, :]

    r = rw_ref[:, 0:W]
    k = rw_ref[:, W:2 * W]
    v = rw_ref[:, 2 * W:3 * W]
    wl = rw_ref[:, 3 * W:3 * W + LANES]
    al = rw_ref[:, 3 * W + LANES:3 * W + 2 * LANES]
    gl = rw_ref[:, 3 * W + 2 * LANES:3 * W + 3 * LANES]

    ri = lax.broadcasted_iota(jnp.int32, (LANES, LANES), 0)
    ci = lax.broadcasted_iota(jnp.int32, (LANES, LANES), 1)
    same = (ri // HEAD_DIM) == (ci // HEAD_DIM)
    strict = same & ((ri % HEAD_DIM) > (ci % HEAD_DIM))
    incl = same & ((ri % HEAD_DIM) >= (ci % HEAD_DIM))
    lane = lax.broadcasted_iota(jnp.int32, (1, LANES), 1)
    m0 = (lane < HEAD_DIM).astype(F32)
    m1 = 1.0 - m0
    tri = (lax.broadcasted_iota(jnp.int32, (C, C), 0) >= lax.broadcasted_iota(jnp.int32, (C, C), 1)).astype(BF16)

    def head_sum(xv):
        outs = []
        for p in range(n_pair):
            xp = xv[:, p * LANES:(p + 1) * LANES]
            s0 = jnp.sum(xp * m0, axis=1, keepdims=True)
            s1 = jnp.sum(xp * m1, axis=1, keepdims=True)
            outs.append(s0 * m0 + s1 * m1)
        return jnp.concatenate(outs, axis=1)

    def stack2(xp):
        return jnp.concatenate([xp * m0, xp * m1], axis=0)

    z = w0 + _dot(jnp.tanh(wl).astype(BF16), w2_ref[...])
    lw = -math.exp(-0.5) * _sigmoid(z)
    a = _sigmoid(a0 + _dot(al.astype(BF16), a2_ref[...]))
    g = _dot(_sigmoid(gl).astype(BF16), g2_ref[...])
    kk = k * k_k
    kkn = kk / jnp.maximum(jnp.sqrt(head_sum(kk * kk)), 1e-12)
    k2 = k * (1.0 + (a - 1.0) * k_a)
    av = -kkn
    bv = kkn * a
    bonus = head_sum(r * k2 * r_k) * v

    eye = (ri == ci).astype(F32)
    bf = lambda t: t.astype(BF16)

    pre = []
    for c in range(n_chunk):
        rows = slice(c * C, (c + 1) * C)
        lwc = lw[rows]
        cw = _dot_exact_lhs(tri, lwc)
        cwl = cw[C - 1:C, :]
        e_in = jnp.exp(cw)
        e_neg = jnp.exp(-cw)
        e_rem = jnp.exp(cwl - cw)
        wc = jnp.exp(cwl)
        Rt = r[rows] * e_in
        At = av[rows] * jnp.exp(cw - lwc)
        Bb = bv[rows] * e_neg
        Kb = k2[rows] * e_neg
        Bh = bv[rows] * e_rem
        Kh = k2[rows] * e_rem
        vc = v[rows]
        for p in range(n_pair):
            sl = slice(p * LANES, (p + 1) * LANES)
            pre.append(dict(At=At[:, sl], Rt=Rt[:, sl], Bb=Bb[:, sl], Kb=Kb[:, sl], Bh=Bh[:, sl], Kh=Kh[:, sl],
                            v=vc[:, sl], wc=wc[:, sl]))

    for u in pre:
        u["at_bd"] = stack2(u["At"])
        lhs = bf(jnp.concatenate([u["at_bd"], stack2(u["Rt"])], axis=0))
        rhs = bf(jnp.concatenate([stack2(u["Bb"]), stack2(u["Kb"])], axis=0))
        u["G"] = _dot_nt(lhs, rhs)
    for u in pre:
        G = u.pop("G")
        u["a_ab"] = jnp.where(strict, G[0:2 * C, 0:2 * C], 0.0)
        u["a_ak"] = bf(jnp.where(strict, G[0:2 * C, 2 * C:4 * C], 0.0))
        u["a_rb"] = bf(jnp.where(incl, G[2 * C:4 * C, 0:2 * C], 0.0))
        u["a_rk"] = bf(jnp.where(incl, G[2 * C:4 * C, 2 * C:4 * C], 0.0))
        u["v_bd"] = bf(stack2(u["v"]))
    for u in pre:
        xb = bf(u["a_ab"])
        u["P"] = eye + u.pop("a_ab")
        u["X"] = _dot(xb, xb)
        u["M0"] = _dot(u["a_ak"], u["v_bd"])
    for _ in range(int(math.log2(C)) - 2):
        for u in pre:
            Wm = _dot(bf(u["X"]), bf(jnp.concatenate([u["P"], u["X"]], axis=1)))
            u["P"] = u["P"] + Wm[:, 0:LANES]
            u["X"] = Wm[:, LANES:2 * LANES]
    for u in pre:
        u["P"] = bf(u["P"] + _dot(bf(u.pop("X")), bf(u["P"])))
    for u in pre:
        u["M1"] = _dot(u["P"], bf(u.pop("M0")))
        u["Q"] = bf(_dot(u["a_rb"], u["P"]))
        u["PtB"] = _dot_tn(u["P"], bf(stack2(u["Bh"])))
    for u in pre:
        M1 = u.pop("M1")
        u["Y0"] = _dot(jnp.concatenate([u["a_rb"], u["a_rk"]], axis=1),
                       jnp.concatenate([bf(M1), u["v_bd"]], axis=0))
        u["Tm"] = bf(_dot_tn(bf(u["at_bd"]), bf(u.pop("PtB"))))
        m1_pair = M1[0:C] + M1[C:2 * C]
        cst = _dot_tn(bf(jnp.concatenate([m1_pair, u["v"]], axis=0)),
                      bf(jnp.concatenate([u["Bh"], u["Kh"]], axis=0)))
        u["cst"] = jnp.where(same, cst, 0.0)
        u["ar"] = bf(jnp.concatenate([u["At"], u["Rt"]], axis=0))

    states = [state_ref[p] for p in range(n_pair)]
    for c in range(n_chunk):
        rows = slice(c * C, (c + 1) * C)
        us = pre[c * n_pair:(c + 1) * n_pair]
        sbs = [bf(S) for S in states]
        zs = [_dot_nt(u["ar"], sb) for u, sb in zip(us, sbs)]
        new_states = [S * u["wc"] + _dot(sb, u["Tm"]) + u["cst"] for u, S, sb in zip(us, states, sbs)]
        ybds = [stack2(Z[C:2 * C]) + _dot(u["Q"], bf(stack2(Z[0:C]))) + u["Y0"] for u, Z in zip(us, zs)]
        ys = [y_bd[0:C] + y_bd[C:2 * C] for y_bd in ybds]
        states = new_states
        y = jnp.concatenate(ys, axis=1)
        mu = head_sum(y) * (1.0 / HEAD_DIM)
        yc = y - mu
        var = head_sum(yc * yc) * (1.0 / HEAD_DIM)
        yn = yc * lax.rsqrt(var + RWKV_GN_EPS) * ln_w + ln_b
        o_ref[rows, :] = ((yn + bonus[rows]) * g[rows]).astype(BF16)
    for p in range(n_pair):
        state_ref[p] = states[p]


def _rwkv(rw, vecs, w2, a2, g2, B, S, lb):
    nt = S // lb
    return pl.pallas_call(
        functools.partial(_rwkv_kernel, n_chunk=lb // RWKV_CHUNK),
        grid=(B, nt),
        in_specs=[pl.BlockSpec((lb, RWKV_PROJ), lambda b, j: (b * nt + j, 0)),
                  pl.BlockSpec((8, RWKV_WIDTH), lambda b, j: (0, 0)),
                  pl.BlockSpec((LANES, RWKV_WIDTH), lambda b, j: (0, 0)),
                  pl.BlockSpec((LANES, RWKV_WIDTH), lambda b, j: (0, 0)),
                  pl.BlockSpec((LANES, RWKV_WIDTH), lambda b, j: (0, 0))],
        out_specs=pl.BlockSpec((lb, RWKV_WIDTH), lambda b, j: (b * nt + j, 0)),
        out_shape=jax.ShapeDtypeStruct((B * S, RWKV_WIDTH), BF16),
        scratch_shapes=[pltpu.VMEM((RWKV_WIDTH // LANES, LANES, LANES), F32)],
        compiler_params=pltpu.CompilerParams(dimension_semantics=("arbitrary", "arbitrary"),
                                             vmem_limit_bytes=VMEM_LIMIT),
        name="rwkv",
    )(rw, vecs, w2, a2, g2)


def _mix_kernel(at_ref, rk_ref, x_ref, mod_ref, wo_a_ref, wo_r_ref, ln_ref, wr_ref, br_ref,
                x1_ref, h2_ref, ti_ref, gt_ref, rank_ref, cnt_ref, tb_ref, base_ref):
    first = (pl.program_id(0) == 0) & (pl.program_id(1) == 0)

    @pl.when(first)
    def _():
        base_ref[...] = jnp.zeros_like(base_ref)

    mod = mod_ref[0]
    y = _dot(at_ref[...], wo_a_ref[...]) + _dot(rk_ref[...], wo_r_ref[...])
    x = x_ref[...]
    tm = x.shape[0]
    x1 = _layer_norm(DEEPNORM_ALPHA * x + (1.0 + mod[2:3, :]) * y) * ln_ref[0:1, :] + ln_ref[1:2, :]
    h2 = _layer_norm(x1) * (1.0 + mod[4:5, :]) + mod[3:4, :]
    x1_ref[...] = x1
    h_hi = h2.astype(BF16)
    h2_ref[...] = h_hi

    h_lo = (h2 - h_hi.astype(F32)).astype(BF16)
    part = _dot(h_hi, wr_ref[...])
    logits = part[:, 0:LANES] + part[:, LANES:2 * LANES] + _dot(h_lo, wr_ref[:, 0:LANES]) + br_ref[...]
    lt = jnp.transpose(logits)[0:N_EXPERTS, :]
    erow = lax.broadcasted_iota(jnp.int32, (N_EXPERTS, tm), 0).astype(F32)
    cur = lt
    vals, idxs = [], []
    for _ in range(TOP_K):
        m = jnp.max(cur, axis=0, keepdims=True)
        idx = jnp.min(jnp.where(cur == m, erow, float(N_EXPERTS)), axis=0, keepdims=True)
        vals.append(m)
        idxs.append(idx)
        cur = jnp.where(erow == idx, -jnp.inf, cur)
    tv = jnp.concatenate(vals, axis=0)
    e = jnp.exp(tv - tv[0:1, :])
    gt_ref[...] = e / jnp.sum(e, axis=0, keepdims=True)
    ti_ref[...] = jnp.concatenate(idxs, axis=0).astype(jnp.int32)

    onehot = jnp.zeros((N_EXPERTS, tm), F32)
    for idx in idxs:
        onehot = onehot + (erow == idx).astype(F32)
    before = (lax.broadcasted_iota(jnp.int32, (tm, tm), 0)
              < lax.broadcasted_iota(jnp.int32, (tm, tm), 1)).astype(BF16)
    tot = base_ref[:, 0:1] + _dot(onehot.astype(BF16), before)
    ranks = [jnp.sum(jnp.where(erow == idx, tot, 0.0), axis=0, keepdims=True) for idx in idxs]
    rank_ref[...] = jnp.concatenate(ranks, axis=0).astype(jnp.int32)
    tb_ref[0] = base_ref[...].astype(jnp.int32)
    base_ref[...] = base_ref[...] + jnp.sum(onehot, axis=1, keepdims=True)
    cnt_ref[...] = base_ref[...].astype(jnp.int32)


def _mix(attn_out, rwkv_out, x2d, mod, wo_a, wo_r, ln1, w_router, b_router, B, S, tm):
    D = x2d.shape[1]
    nt = S // tm
    T = B * S
    tok = lambda b, j: (b * nt + j, 0)
    col = lambda b, j: (0, b * nt + j)
    fixed = lambda b, j: (0, 0)
    return pl.pallas_call(
        _mix_kernel,
        grid=(B, nt),
        in_specs=[pl.BlockSpec((tm, ATTN_WIDTH), tok),
                  pl.BlockSpec((tm, RWKV_WIDTH), tok),
                  pl.BlockSpec((tm, D), tok),
                  pl.BlockSpec((1, 6, D), lambda b, j: (b, 0, 0)),
                  pl.BlockSpec((ATTN_WIDTH, D), fixed),
                  pl.BlockSpec((RWKV_WIDTH, D), fixed),
                  pl.BlockSpec((2, D), fixed),
                  pl.BlockSpec((D, 2 * LANES), fixed),
                  pl.BlockSpec((1, LANES), fixed)],
        out_specs=[pl.BlockSpec((tm, D), tok),
                   pl.BlockSpec((tm, D), tok),
                   pl.BlockSpec((TOP_K, tm), col),
                   pl.BlockSpec((TOP_K, tm), col),
                   pl.BlockSpec((TOP_K, tm), col),
                   pl.BlockSpec((N_EXPERTS, LANES), fixed),
                   pl.BlockSpec((1, N_EXPERTS, LANES), lambda b, j: (b * nt + j, 0, 0))],
        out_shape=[jax.ShapeDtypeStruct((T, D), F32),
                   jax.ShapeDtypeStruct((T, D), BF16),
                   jax.ShapeDtypeStruct((TOP_K, T), jnp.int32),
                   jax.ShapeDtypeStruct((TOP_K, T), F32),
                   jax.ShapeDtypeStruct((TOP_K, T), jnp.int32),
                   jax.ShapeDtypeStruct((N_EXPERTS, LANES), jnp.int32),
                   jax.ShapeDtypeStruct((T // tm, N_EXPERTS, LANES), jnp.int32)],
        scratch_shapes=[pltpu.VMEM((N_EXPERTS, LANES), F32)],
        compiler_params=pltpu.CompilerParams(dimension_semantics=("arbitrary", "arbitrary"),
                                             vmem_limit_bytes=VMEM_LIMIT),
        name="mix",
    )(attn_out, rwkv_out, x2d, mod, wo_a, wo_r, ln1, w_router, b_router)


RUN_PIECES = tuple(2 ** b for b in range(int(math.log2(MOE_TILE)), -1, -1))
SUBLANES = 8


def _to_tiles(ref, x):
    n = x.shape[0]
    for c in range(SUBLANES):
        ref[pl.ds(c, n, stride=SUBLANES), :] = x[:, c * LANES:(c + 1) * LANES]


def _from_tiles(ref):
    n = ref.shape[0] // SUBLANES
    return jnp.concatenate([ref[pl.ds(c, n, stride=SUBLANES), :] for c in range(SUBLANES)], axis=1)


def _run_copies(n, local, local_start, remote, remote_start, sem, to_remote):
    off = 0
    for piece in RUN_PIECES:
        take = (n & piece) != 0

        @pl.when(take)
        def _(off=off, piece=piece):
            lo = pl.multiple_of((local_start + off) * SUBLANES, SUBLANES)
            ro = pl.multiple_of((remote_start + off) * SUBLANES, SUBLANES)
            loc = local.at[pl.ds(lo, piece * SUBLANES)]
            rem = remote.at[pl.ds(ro, piece * SUBLANES)]
            src, dst = (loc, rem) if to_remote else (rem, loc)
            pltpu.make_async_copy(src, dst, sem).start()

        off = off + (n & piece)


def _dispatch_kernel(tcnt_ref, lstart_ref, gstart_ref, pad_ref, pad_start_ref, n_used_ref, lpos_ref, h2_ref,
                     xs_ref, xbuf, zbuf, sems):
    i = pl.program_id(0)
    tm = h2_ref.shape[0]
    n_loc = TOP_K * tm
    n_blocks = xs_ref.shape[0] // (MOE_BLOCK * SUBLANES)
    zero_sem = sems.at[2]

    @pl.when(i == 0)
    def _():
        zbuf[...] = jnp.zeros_like(zbuf)

        def zero_pad(e, carry):
            _run_copies(pad_ref[e], zbuf, 0, xs_ref, pad_start_ref[e], zero_sem, True)
            return carry

        def zero_tail(b, carry):
            @pl.when(b >= n_used_ref[0])
            def _():
                start = pl.multiple_of(b * (MOE_BLOCK * SUBLANES), MOE_BLOCK * SUBLANES)
                pltpu.make_async_copy(zbuf, xs_ref.at[pl.ds(start, MOE_BLOCK * SUBLANES)], zero_sem).start()
            return carry

        lax.fori_loop(0, N_EXPERTS, zero_pad, 0)
        lax.fori_loop(n_blocks - N_EXPERTS, n_blocks, zero_tail, 0)

    slot = lax.broadcasted_iota(jnp.int32, (n_loc, tm), 0)
    lpos = lpos_ref[...]
    perm = jnp.zeros((n_loc, tm), F32)
    for k in range(TOP_K):
        perm = perm + (slot == lpos[k:k + 1, :]).astype(F32)
    perm = perm.astype(BF16)

    def wait_tile(s):
        pltpu.make_async_copy(xbuf.at[s], xs_ref.at[pl.ds(0, n_loc * SUBLANES)], sems.at[s]).wait()

    cur = i % 2

    @pl.when(i >= 2)
    def _():
        wait_tile(cur)

    _to_tiles(xbuf.at[cur], _dot(perm, h2_ref[...]))

    def issue(e, carry):
        idx = i * N_EXPERTS + e
        _run_copies(tcnt_ref[idx], xbuf.at[cur], lstart_ref[idx], xs_ref, gstart_ref[idx], sems.at[cur], True)
        return carry

    lax.fori_loop(0, N_EXPERTS, issue, 0)

    @pl.when(i == pl.num_programs(0) - 1)
    def _():
        wait_tile(cur)

        @pl.when(i >= 1)
        def _():
            wait_tile(1 - cur)

        n_zero = N_EXPERTS * MOE_BLOCK * SUBLANES
        pltpu.make_async_copy(xs_ref.at[pl.ds(0, n_zero)], xs_ref.at[pl.ds(0, n_zero)], zero_sem).wait()


def _dispatch(tcnt, lstart, gstart, pad, pad_start, n_used, lpos, h2, n_rows, tm):
    T, D = h2.shape
    grid_spec = pltpu.PrefetchScalarGridSpec(
        num_scalar_prefetch=6,
        grid=(T // tm,),
        in_specs=[pl.BlockSpec((TOP_K, tm), lambda i, *_: (0, i)),
                  pl.BlockSpec((tm, D), lambda i, *_: (i, 0))],
        out_specs=pl.BlockSpec(memory_space=pl.ANY),
        scratch_shapes=[pltpu.VMEM((2, TOP_K * tm * SUBLANES, LANES), F32),
                        pltpu.VMEM((MOE_BLOCK * SUBLANES, LANES), F32),
                        pltpu.SemaphoreType.DMA((3,))],
    )
    return pl.pallas_call(
        _dispatch_kernel,
        grid_spec=grid_spec,
        out_shape=jax.ShapeDtypeStruct((n_rows * SUBLANES, LANES), F32),
        compiler_params=pltpu.CompilerParams(dimension_semantics=("arbitrary",),
                                             vmem_limit_bytes=VMEM_LIMIT),
        name="dispatch",
    )(tcnt, lstart, gstart, pad, pad_start, n_used, lpos, h2)


def _experts_kernel(blk_e_ref, n_used_ref, next_e_ref, xs_ref, wgu_hbm, bgu_ref, wd_hbm, bd_ref, ys_ref,
                    wgu_f32, wd_f32, wgu_bf, wd_bf, sems):
    i = pl.program_id(0)
    d_ff = wd_bf.shape[0]
    used = i < n_used_ref[0]
    e = blk_e_ref[i]
    new_expert = (i == 0) | (e != blk_e_ref[jnp.maximum(i - 1, 0)])

    def weight_copies(ex):
        return (pltpu.make_async_copy(wgu_hbm.at[ex], wgu_f32, sems.at[0]),
                pltpu.make_async_copy(wd_hbm.at[ex], wd_f32, sems.at[1]))

    @pl.when(i == 0)
    def _():
        for cp in weight_copies(e):
            cp.start()

    @pl.when(used & new_expert)
    def _():
        for cp in weight_copies(e):
            cp.wait()
        wgu_bf[...] = wgu_f32[...].astype(BF16)
        wd_bf[...] = wd_f32[...].astype(BF16)
        nxt = next_e_ref[i]

        @pl.when(nxt >= 0)
        def _():
            for cp in weight_copies(nxt):
                cp.start()

    @pl.when(used)
    def _():
        xb = _from_tiles(xs_ref).astype(BF16)
        gu = _dot(xb, wgu_bf[...]) + bgu_ref[0]
        gate = jnp.minimum(gu[:, :d_ff], SWIGLU_LIMIT)
        up = jnp.clip(gu[:, d_ff:], -SWIGLU_LIMIT, SWIGLU_LIMIT)
        act = (up + 1.0) * (gate * _sigmoid(SWIGLU_ALPHA * gate))
        _to_tiles(ys_ref, _dot(act.astype(BF16), wd_bf[...]) + bd_ref[0])

    @pl.when(i >= n_used_ref[0])
    def _():
        ys_ref[...] = jnp.zeros_like(ys_ref)


def _experts(blk_e, n_used, next_e, xs, wgu, bgu, wd, bd):
    d_ff, D = wd.shape[1], wd.shape[2]
    n_blocks = xs.shape[0] // (MOE_BLOCK * SUBLANES)
    blk = (MOE_BLOCK * SUBLANES, LANES)

    def last_used(i, n_used_ref):
        return jnp.minimum(i, jnp.maximum(n_used_ref[0] - 1, 0))

    def row_map(i, blk_e_ref, n_used_ref, next_e_ref):
        return (last_used(i, n_used_ref), 0)

    def exp_map(i, blk_e_ref, n_used_ref, next_e_ref):
        return (blk_e_ref[last_used(i, n_used_ref)], 0, 0)

    grid_spec = pltpu.PrefetchScalarGridSpec(
        num_scalar_prefetch=3,
        grid=(n_blocks,),
        in_specs=[pl.BlockSpec(blk, row_map),
                  pl.BlockSpec(memory_space=pl.ANY),
                  pl.BlockSpec((1, 1, 2 * d_ff), exp_map),
                  pl.BlockSpec(memory_space=pl.ANY),
                  pl.BlockSpec((1, 1, D), exp_map)],
        out_specs=pl.BlockSpec(blk, lambda i, *_: (i, 0)),
        scratch_shapes=[pltpu.VMEM((D, 2 * d_ff), F32), pltpu.VMEM((d_ff, D), F32),
                        pltpu.VMEM((D, 2 * d_ff), BF16), pltpu.VMEM((d_ff, D), BF16),
                        pltpu.SemaphoreType.DMA((2,))],
    )
    return pl.pallas_call(
        _experts_kernel,
        grid_spec=grid_spec,
        out_shape=jax.ShapeDtypeStruct(xs.shape, F32),
        compiler_params=pltpu.CompilerParams(dimension_semantics=("arbitrary",),
                                             vmem_limit_bytes=VMEM_LIMIT),
        name="experts",
    )(blk_e, n_used, next_e, xs, wgu, bgu, wd, bd)


def _combine_kernel(tcnt_ref, lstart_ref, gstart_ref, ys_ref, lpos_ref, gt_ref, x1_ref, mod_ref, ln_ref,
                    o_ref, buf, sems):
    nt = pl.num_programs(1)
    n_tiles = pl.num_programs(0) * nt
    i = pl.program_id(0) * nt + pl.program_id(1)
    tm = x1_ref.shape[0]
    n_loc = TOP_K * tm

    def fetch(tile, s):
        def issue(e, carry):
            idx = tile * N_EXPERTS + e
            _run_copies(tcnt_ref[idx], buf.at[s], lstart_ref[idx], ys_ref, gstart_ref[idx], sems.at[s], False)
            return carry

        lax.fori_loop(0, N_EXPERTS, issue, 0)

    @pl.when(i == 0)
    def _():
        fetch(i, 0)

    @pl.when(i + 1 < n_tiles)
    def _():
        fetch(i + 1, (i + 1) % 2)

    slot = lax.broadcasted_iota(jnp.int32, (tm, n_loc), 1)
    lpos = lpos_ref[...]
    gt = gt_ref[...]
    pick = jnp.zeros((tm, n_loc), F32)
    for k in range(TOP_K):
        pick = pick + jnp.where(slot == lpos[:, k:k + 1], gt[:, k:k + 1], 0.0)
    cur = i % 2
    pltpu.make_async_copy(ys_ref.at[pl.ds(0, n_loc * SUBLANES)], buf.at[cur], sems.at[cur]).wait()
    y = _dot(pick.astype(BF16), _from_tiles(buf.at[cur]).astype(BF16))
    z = DEEPNORM_ALPHA * x1_ref[...] + (1.0 + mod_ref[0][5:6, :]) * y
    o_ref[...] = _layer_norm(z) * ln_ref[0:1, :] + ln_ref[1:2, :]


def _combine(tcnt, lstart, gstart, ys, lpos_t, gates_t, x1, mod, ln2, B, S, tm):
    T, D = x1.shape
    nt = S // tm
    tok = lambda b, j, *_: (b * nt + j, 0)
    grid_spec = pltpu.PrefetchScalarGridSpec(
        num_scalar_prefetch=3,
        grid=(B, nt),
        in_specs=[pl.BlockSpec(memory_space=pl.ANY),
                  pl.BlockSpec((tm, TOP_K), tok),
                  pl.BlockSpec((tm, TOP_K), tok),
                  pl.BlockSpec((tm, D), tok),
                  pl.BlockSpec((1, 6, D), lambda b, j, *_: (b, 0, 0)),
                  pl.BlockSpec((2, D), lambda b, j, *_: (0, 0))],
        out_specs=pl.BlockSpec((tm, D), tok),
        scratch_shapes=[pltpu.VMEM((2, TOP_K * tm * SUBLANES, LANES), F32),
                        pltpu.SemaphoreType.DMA((2,))],
    )
    return pl.pallas_call(
        _combine_kernel,
        grid_spec=grid_spec,
        out_shape=jax.ShapeDtypeStruct((T, D), F32),
        compiler_params=pltpu.CompilerParams(dimension_semantics=("arbitrary", "arbitrary"),
                                             vmem_limit_bytes=VMEM_LIMIT),
        name="combine",
    )(tcnt, lstart, gstart, ys, lpos_t, gates_t, x1, mod, ln2)


def _pad_rows(w, rows):
    return jnp.pad(w, ((0, rows - w.shape[0]), (0, 0)))


def _pad_cols(w, cols):
    return jnp.pad(w, ((0, 0), (0, cols - w.shape[1])))


def _layer(x, c, positions, w_ada, b_ada, w_in, shift_mu, rwkv_w0, rwkv_w2, rwkv_a0, rwkv_a2, rwkv_g2,
           rwkv_k_k, rwkv_k_a, rwkv_r_k, rwkv_ln_w, rwkv_ln_b, attn_sinks, w_out, ln1_g, ln1_b,
           w_router, b_router, w_gate_up, b_gate_up, w_down, b_down, ln2_g, ln2_b):
    B, S, D = x.shape
    T = B * S

    q0, k0, v0 = 0, ATTN_WIDTH, ATTN_WIDTH + KV_WIDTH
    r0 = ATTN_WIDTH + 2 * KV_WIDTH
    heads = lambda base: [w_in[:, base + h * HEAD_DIM: base + (h + 1) * HEAD_DIM] for h in range(N_KV_HEADS)]
    dup = lambda hs: [w for w in hs for _ in range(2)]
    w_attn = jnp.concatenate([w_in[:, q0:q0 + ATTN_WIDTH]] + dup(heads(k0)) + dup(heads(v0)), axis=1).astype(BF16)
    lora0 = r0 + 3 * RWKV_WIDTH
    lora = (DECAY_LORA, AAA_LORA, GATE_LORA)
    pieces_w = [w_in[:, r0:lora0]]
    pieces_mu = [shift_mu[None, 0:3 * RWKV_WIDTH]]
    off = lora0
    for n in lora:
        pieces_w.append(_pad_cols(w_in[:, off:off + n], LANES))
        pieces_mu.append(_pad_cols(shift_mu[None, off - r0:off - r0 + n], LANES))
        off += n
    w_rwkv = jnp.concatenate(pieces_w, axis=1).astype(BF16)
    mu = jnp.concatenate(pieces_mu, axis=1)
    inv_freq = ROPE_THETA ** (-jnp.arange(0, ROT_DIM, 2, dtype=F32) / ROT_DIM)
    lane_p = jnp.arange(LANES) % HEAD_DIM
    n_freq = ROT_DIM // 2
    rot_tab = jnp.zeros((8, LANES), F32)
    rot_tab = rot_tab.at[0].set(jnp.where(lane_p < ROT_DIM, 0.0, 1.0))
    rot_tab = rot_tab.at[1].set(jnp.where(lane_p < n_freq, -1.0, 0.0))
    rot_tab = rot_tab.at[2].set(jnp.where((lane_p >= n_freq) & (lane_p < ROT_DIM), 1.0, 0.0))
    freq_tab = jnp.broadcast_to(inv_freq[:, None], (n_freq, LANES))
    lane_freq = (jnp.arange(n_freq)[:, None] == (lane_p % n_freq)[None, :]) & (lane_p < ROT_DIM)[None, :]
    zeros = jnp.zeros_like(lane_freq)
    expand = jnp.concatenate([jnp.concatenate([lane_freq, zeros], axis=1),
                              jnp.concatenate([zeros, lane_freq], axis=1)], axis=0).astype(BF16)
    vecs = jnp.stack([rwkv_w0, rwkv_a0, rwkv_k_k, rwkv_k_a, rwkv_r_k.reshape(-1), rwkv_ln_w, rwkv_ln_b,
                      jnp.zeros_like(rwkv_w0)])
    w2 = _pad_rows(rwkv_w2, LANES).astype(BF16)
    a2 = _pad_rows(rwkv_a2, LANES).astype(BF16)
    g2 = _pad_rows(rwkv_g2, LANES).astype(BF16)
    wo_a = w_out[:ATTN_WIDTH].astype(BF16)
    wo_r = w_out[ATTN_WIDTH:].astype(BF16)
    w_r_hi = w_router.astype(BF16)
    w_r_lo = (w_router - w_r_hi.astype(F32)).astype(BF16)
    w_r = jnp.concatenate([_pad_cols(w_r_hi, LANES), _pad_cols(w_r_lo, LANES)], axis=1)
    b_r = jnp.concatenate([b_router, jnp.full((LANES - N_EXPERTS,), NEG_INF, F32)])[None, :]

    mod = _mod(c, w_ada, b_ada).reshape(B, 6, D)
    qkv, rw = _inproj(x, positions, mod, w_attn, w_rwkv, mu, rot_tab, freq_tab, expand, min(INPROJ_TILE, S))
    attn_out = _attention(qkv, attn_sinks, B, S, min(ATTN_STEP_BLOCKS, S // ATTN_BLOCK))
    rwkv_out = _rwkv(rw, vecs, w2, a2, g2, B, S, min(RWKV_STEP, S))

    mtile = min(MOE_TILE, S)
    x1, h2, top_i, gates, rank, cnt, tbase = _mix(attn_out, rwkv_out, x.reshape(T, D), mod, wo_a, wo_r,
                                                   jnp.stack([ln1_g, ln1_b]), w_r, b_r, B, S, mtile)

    counts = cnt[:, 0]
    padded = (counts + MOE_BLOCK - 1) // MOE_BLOCK * MOE_BLOCK
    pend = jnp.cumsum(padded)
    pstart = pend - padded
    n_blocks = T * TOP_K // MOE_BLOCK + N_EXPERTS
    blk_row = jnp.arange(n_blocks, dtype=jnp.int32) * MOE_BLOCK
    blk_e = jnp.minimum(jnp.sum((pend[None, :] <= blk_row[:, None]).astype(jnp.int32), axis=1), N_EXPERTS - 1)
    n_used = (pend[-1:] // MOE_BLOCK).astype(jnp.int32)
    tb = tbase[:, :, 0]
    tcnt = jnp.concatenate([tb[1:], counts[None]], axis=0) - tb
    lstart = jnp.cumsum(tcnt, axis=1) - tcnt
    gstart = pstart[None, :] + tb
    shift = jnp.repeat(jnp.transpose(lstart - tb), mtile, axis=1)
    experts = jnp.arange(N_EXPERTS, dtype=jnp.int32)
    lpos = rank + jnp.sum(jnp.where(top_i[None] == experts[:, None, None], shift[:, None, :], 0), axis=0)
    flat = lambda a: a.reshape(-1).astype(jnp.int32)

    xs = _dispatch(flat(tcnt), flat(lstart), flat(gstart), flat(padded - counts), flat(pstart + counts), n_used,
                   lpos, h2, n_blocks * MOE_BLOCK, mtile)
    later_with_rows = (experts[None, :] > experts[:, None]) & (counts[None, :] > 0)
    next_of = jnp.min(jnp.where(later_with_rows, experts[None, :], N_EXPERTS), axis=1)
    next_of = jnp.where(next_of < N_EXPERTS, next_of, -1)
    next_e = jnp.sum(jnp.where(blk_e[:, None] == experts[None, :], next_of[None, :], 0), axis=1).astype(jnp.int32)
    ys = _experts(blk_e, n_used, next_e, xs, w_gate_up, b_gate_up[:, None, :], w_down, b_down[:, None, :])
    out = _combine(flat(tcnt), flat(lstart), flat(gstart), ys, jnp.transpose(lpos), jnp.transpose(gates), x1, mod,
                   jnp.stack([ln2_g, ln2_b]), B, S, mtile)
    return out.reshape(B, S, D)


def kernel(x, c, positions, w_ada, b_ada, w_in, shift_mu, rwkv_w0, rwkv_w2, rwkv_a0, rwkv_a2, rwkv_g2,
           rwkv_k_k, rwkv_k_a, rwkv_r_k, rwkv_ln_w, rwkv_ln_b, attn_sinks, w_out, ln1_g, ln1_b,
           w_router, b_router, w_gate_up, b_gate_up, w_down, b_down, ln2_g, ln2_b):
    for l in range(DEPTH):
        x = _layer(x, c, positions, w_ada[l], b_ada[l], w_in[l], shift_mu[l], rwkv_w0[l], rwkv_w2[l],
                   rwkv_a0[l], rwkv_a2[l], rwkv_g2[l], rwkv_k_k[l], rwkv_k_a[l], rwkv_r_k[l], rwkv_ln_w[l],
                   rwkv_ln_b[l], attn_sinks[l], w_out[l], ln1_g[l], ln1_b[l], w_router[l], b_router[l],
                   w_gate_up[l], b_gate_up[l], w_down[l], b_down[l], ln2_g[l], ln2_b[l])
    return x
```

```python
import functools
import math

import jax
import jax.numpy as jnp
from jax import lax
from jax.experimental import pallas as pl
from jax.experimental.pallas import tpu as pltpu

F32 = jnp.float32
BF16 = jnp.bfloat16

HEAD_DIM = 64
N_ATTN_HEADS = 8
N_KV_HEADS = 2
N_RWKV_HEADS = 8
ATTN_WIDTH = N_ATTN_HEADS * HEAD_DIM
KV_WIDTH = N_KV_HEADS * HEAD_DIM
RWKV_WIDTH = N_RWKV_HEADS * HEAD_DIM
ATTN_BLOCK = 128
ROT_DIM = HEAD_DIM // 4
ROPE_THETA = 500000.0
DECAY_LORA = 32
AAA_LORA = 32
GATE_LORA = 96
N_EXPERTS = 32
TOP_K = 4
SWIGLU_LIMIT = 7.0
SWIGLU_ALPHA = 1.702
LN_EPS = 1e-5
RWKV_GN_EPS = 64e-5
NEG_INF = -1e30
DEPTH = 1
DEEPNORM_ALPHA = (2 * DEPTH) ** 0.25

LANES = 128
RWKV_CHUNK = 64
RWKV_STEP = 512
INPROJ_TILE = 512
RWKV_PROJ_CHUNK = 512
MOE_BLOCK = 512
MOE_TILE = 256
ATTN_PROJ = ATTN_WIDTH + 4 * KV_WIDTH
RWKV_PROJ = 3 * RWKV_WIDTH + 3 * LANES
VMEM_LIMIT = 48 * 1024 * 1024


def _dot(a, b):
    return jnp.dot(a, b, preferred_element_type=F32)


def _dot_nt(a, b):
    return lax.dot_general(a, b, (((1,), (1,)), ((), ())), preferred_element_type=F32)


def _dot_tn(a, b):
    return lax.dot_general(a, b, (((0,), (0,)), ((), ())), preferred_element_type=F32)


def _split3(x):
    h = x.astype(BF16)
    r1 = x - h.astype(F32)
    m = r1.astype(BF16)
    lo = (r1 - m.astype(F32)).astype(BF16)
    return h, m, lo


def _dot_exact_lhs(m_bf16, x):
    h, m, lo = _split3(x)
    return _dot(m_bf16, h) + _dot(m_bf16, m) + _dot(m_bf16, lo)


def _layer_norm(x):
    mu = jnp.mean(x, axis=-1, keepdims=True)
    xc = x - mu
    var = jnp.mean(xc * xc, axis=-1, keepdims=True)
    return xc * lax.rsqrt(var + LN_EPS)


def _sigmoid(x):
    return 1.0 / (1.0 + jnp.exp(-x))


def _mod_kernel(c_ref, w_ref, b_ref, o_ref):
    c = c_ref[...]
    s = c * _sigmoid(c)
    o_ref[...] = jnp.dot(s, w_ref[...], preferred_element_type=F32,
                         precision=lax.Precision.HIGHEST) + b_ref[...]


def _mod(c, w_ada, b_ada):
    B, D = c.shape
    n = w_ada.shape[1] // D
    return pl.pallas_call(
        _mod_kernel,
        grid=(n,),
        in_specs=[pl.BlockSpec((B, D), lambda i: (0, 0)),
                  pl.BlockSpec((D, D), lambda i: (0, i)),
                  pl.BlockSpec((1, D), lambda i: (0, i))],
        out_specs=pl.BlockSpec((B, D), lambda i: (0, i)),
        out_shape=jax.ShapeDtypeStruct((B, n * D), F32),
        compiler_params=pltpu.CompilerParams(dimension_semantics=("arbitrary",),
                                             vmem_limit_bytes=VMEM_LIMIT),
        name="mod",
    )(c, w_ada, b_ada.reshape(1, -1))


def _inproj_kernel(x_ref, pos_ref, mod_ref, wa_ref, wr_ref, mu_ref, rt_ref, fq_ref, ex_ref, sink_ref,
                   at_ref, rw_ref, qkv_ref, kv_prev_ref, carry_ref):
    j = pl.program_id(1)

    @pl.when(j == 0)
    def _():
        kv_prev_ref[...] = jnp.zeros_like(kv_prev_ref)

    x = x_ref[0]
    tm = x.shape[0]
    mod = mod_ref[0]
    h = _layer_norm(x) * (1.0 + mod[1:2, :]) + mod[0:1, :]
    hb = h.astype(BF16)

    pa = _dot(hb, wa_ref[...])
    ang = pos_ref[0].astype(F32) * fq_ref[:, 0:1]
    pieces = _split3(jnp.concatenate([jnp.cos(ang), jnp.sin(ang)], axis=0))
    trig = _dot_tn(pieces[0], ex_ref[...]) + _dot_tn(pieces[1], ex_ref[...]) + _dot_tn(pieces[2], ex_ref[...])
    cs = trig[:, 0:LANES] + rt_ref[0:1, :]
    sn = trig[:, LANES:2 * LANES]
    m_lo = rt_ref[1:2, :]
    m_hi = rt_ref[2:3, :]
    n_q = ATTN_WIDTH // LANES
    n_rot = (ATTN_WIDTH + 2 * KV_WIDTH) // LANES
    for ch in range(n_rot):
        t = pa[:, ch * LANES:(ch + 1) * LANES]
        if ch < n_q:
            t = t * (1.0 / math.sqrt(HEAD_DIM))
        up = pltpu.roll(t, LANES - ROT_DIM // 2, 1)
        dn = pltpu.roll(t, ROT_DIM // 2, 1)
        o = t * cs + sn * (m_lo * up + m_hi * dn)
        qkv_ref[:, ch * LANES:(ch + 1) * LANES] = o.astype(BF16)
    qkv_ref[:, n_rot * LANES:] = pa[:, n_rot * LANES:].astype(BF16)

    row = lax.broadcasted_iota(jnp.int32, (tm, 1), 0)

    def rwkv_columns(c0, c1):
        def emit():
            pr = _dot(hb, wr_ref[:, c0:c1])
            prev = pltpu.roll(pr, 1, 0)
            carry = jnp.where(j == 0, 0.0, carry_ref[:, c0:c1])
            prev = jnp.where(row == 0, carry, prev)
            carry_ref[:, c0:c1] = pr[tm - 1:tm, :]
            rw_ref[:, c0:c1] = pr + (prev - pr) * mu_ref[:, c0:c1]
        return emit

    bounds = list(range(0, RWKV_PROJ, RWKV_PROJ_CHUNK)) + [RWKV_PROJ]
    kv_w = 2 * KV_WIDTH
    _attn_body(qkv_ref.at[:, pl.ds(0, ATTN_WIDTH)], qkv_ref.at[:, pl.ds(ATTN_WIDTH, kv_w)],
               kv_prev_ref.at[:, pl.ds(0, kv_w)], qkv_ref.at[:, pl.ds(ATTN_WIDTH + kv_w, kv_w)],
               kv_prev_ref.at[:, pl.ds(kv_w, kv_w)], sink_ref, at_ref, j > 0,
               interleave=[rwkv_columns(c0, c1) for c0, c1 in zip(bounds[:-1], bounds[1:])])
    kv_prev_ref[...] = qkv_ref[tm - ATTN_BLOCK:tm, ATTN_WIDTH:]


def _inproj(x, positions, mod, w_attn, w_rwkv, mu, rot_tab, freq_tab, expand, sinks, tm):
    B, S, D = x.shape
    nt = S // tm
    return pl.pallas_call(
        _inproj_kernel,
        grid=(B, nt),
        in_specs=[pl.BlockSpec((1, tm, D), lambda b, j: (b, j, 0)),
                  pl.BlockSpec((1, 1, tm), lambda b, j: (b, 0, j)),
                  pl.BlockSpec((1, 6, D), lambda b, j: (b, 0, 0)),
                  pl.BlockSpec((D, ATTN_PROJ), lambda b, j: (0, 0)),
                  pl.BlockSpec((D, RWKV_PROJ), lambda b, j: (0, 0)),
                  pl.BlockSpec((1, RWKV_PROJ), lambda b, j: (0, 0)),
                  pl.BlockSpec((8, LANES), lambda b, j: (0, 0)),
                  pl.BlockSpec(freq_tab.shape, lambda b, j: (0, 0)),
                  pl.BlockSpec(expand.shape, lambda b, j: (0, 0)),
                  pl.BlockSpec(memory_space=pltpu.SMEM)],
        out_specs=[pl.BlockSpec((tm, ATTN_WIDTH), lambda b, j: (b * nt + j, 0)),
                   pl.BlockSpec((tm, RWKV_PROJ), lambda b, j: (b * nt + j, 0))],
        out_shape=[jax.ShapeDtypeStruct((B * S, ATTN_WIDTH), BF16),
                   jax.ShapeDtypeStruct((B * S, RWKV_PROJ), F32)],
        scratch_shapes=[pltpu.VMEM((tm, ATTN_PROJ), BF16),
                        pltpu.VMEM((ATTN_BLOCK, ATTN_PROJ - ATTN_WIDTH), BF16),
                        pltpu.VMEM((1, RWKV_PROJ), F32)],
        compiler_params=pltpu.CompilerParams(dimension_semantics=("arbitrary", "arbitrary"),
                                             vmem_limit_bytes=VMEM_LIMIT),
        name="inproj",
    )(x, positions.reshape(B, 1, S), mod, w_attn, w_rwkv, mu, rot_tab, freq_tab, expand, sinks)


def _attn_body(q_ref, kc_ref, kp_ref, vc_ref, vp_ref, sink_ref, o_ref, has_prev, interleave=()):
    blk = ATTN_BLOCK
    n_sub = q_ref.shape[0] // blk
    qi = lax.broadcasted_iota(jnp.int32, (blk, 2 * blk), 0)
    kj = lax.broadcasted_iota(jnp.int32, (blk, 2 * blk), 1)
    band = (kj > qi) & (kj <= qi + blk)
    first = band & ((kj >= blk) | has_prev)
    lane = lax.broadcasted_iota(jnp.int32, (1, LANES), 1)
    lo = (lane < HEAD_DIM).astype(BF16)
    hi = (lane >= HEAD_DIM).astype(BF16)
    halves = {}
    for u in range(n_sub):
        for g in range(N_KV_HEADS):
            sl = slice(g * LANES, (g + 1) * LANES)
            prev_k = kp_ref[:, sl] if u == 0 else kc_ref[(u - 1) * blk:u * blk, sl]
            prev_v = vp_ref[:, sl] if u == 0 else vc_ref[(u - 1) * blk:u * blk, sl]
            kcat = jnp.concatenate([prev_k, kc_ref[u * blk:(u + 1) * blk, sl]], axis=0)
            vcat = jnp.concatenate([prev_v, vc_ref[u * blk:(u + 1) * blk, sl]], axis=0)
            halves[u, g] = ((kcat * lo, vcat * lo), (kcat * hi, vcat * hi))
    units = [(u, c, half) for u in range(n_sub) for c in range(ATTN_WIDTH // LANES) for half in range(2)]
    scores = [_dot_nt(q_ref[u * blk:(u + 1) * blk, c * LANES:(c + 1) * LANES], halves[u, c // 2][half][0])
              for u, c, half in units]
    probs, denoms = [], []
    every = max(1, len(units) // max(1, len(interleave)))
    for n, ((u, c, half), s) in enumerate(zip(units, scores)):
        if n % every == 0 and n // every < len(interleave):
            interleave[n // every]()
        sink = sink_ref[2 * c + half]
        s = jnp.where(first if u == 0 else band, s, NEG_INF)
        m = jnp.maximum(jnp.max(s, axis=-1, keepdims=True), sink)
        p = jnp.exp(s - m)
        denoms.append(jnp.sum(p, axis=-1, keepdims=True) + jnp.exp(sink - m))
        probs.append(p.astype(BF16))
    outs = [_dot(p, halves[u, c // 2][half][1]) / d for (u, c, half), p, d in zip(units, probs, denoms)]
    for n, (u, c, half) in enumerate(units):
        if half == 0:
            o_ref[u * blk:(u + 1) * blk, c * LANES:(c + 1) * LANES] = (outs[n] + outs[n + 1]).astype(BF16)


def _rwkv_kernel(rw_ref, vec_ref, w2_ref, a2_ref, g2_ref, o_ref, state_ref, *, n_chunk):
    j = pl.program_id(1)
    C = RWKV_CHUNK
    W = RWKV_WIDTH
    n_pair = W // LANES

    @pl.when(j == 0)
    def _():
        state_ref[...] = jnp.zeros_like(state_ref)

    w0 = vec_ref[0:1, :]
    a0 = vec_ref[1:2, :]
    k_k = vec_ref[2:3, :]
    k_a = vec_ref[3:4, :]
    r_k = vec_ref[4:5, :]
    ln_w = vec_ref[5:6, :]
    ln_b = vec_ref[6:7, :]

    r = rw_ref[:, 0:W]
    k = rw_ref[:, W:2 * W]
    v = rw_ref[:, 2 * W:3 * W]
    wl = rw_ref[:, 3 * W:3 * W + LANES]
    al = rw_ref[:, 3 * W + LANES:3 * W + 2 * LANES]
    gl = rw_ref[:, 3 * W + 2 * LANES:3 * W + 3 * LANES]

    ri = lax.broadcasted_iota(jnp.int32, (LANES, LANES), 0)
    ci = lax.broadcasted_iota(jnp.int32, (LANES, LANES), 1)
    same = (ri // HEAD_DIM) == (ci // HEAD_DIM)
    strict = same & ((ri % HEAD_DIM) > (ci % HEAD_DIM))
    incl = same & ((ri % HEAD_DIM) >= (ci % HEAD_DIM))
    lane = lax.broadcasted_iota(jnp.int32, (1, LANES), 1)
    m0 = (lane < HEAD_DIM).astype(F32)
    m1 = 1.0 - m0
    tri = (lax.broadcasted_iota(jnp.int32, (C, C), 0) >= lax.broadcasted_iota(jnp.int32, (C, C), 1)).astype(BF16)

    def head_sum(xv):
        outs = []
        for p in range(n_pair):
            xp = xv[:, p * LANES:(p + 1) * LANES]
            s0 = jnp.sum(xp * m0, axis=1, keepdims=True)
            s1 = jnp.sum(xp * m1, axis=1, keepdims=True)
            outs.append(s0 * m0 + s1 * m1)
        return jnp.concatenate(outs, axis=1)

    def stack2(xp):
        return jnp.concatenate([xp * m0, xp * m1], axis=0)

    z = w0 + _dot(jnp.tanh(wl).astype(BF16), w2_ref[...])
    lw = -math.exp(-0.5) * _sigmoid(z)
    a = _sigmoid(a0 + _dot(al.astype(BF16), a2_ref[...]))
    g = _dot(_sigmoid(gl).astype(BF16), g2_ref[...])
    kk = k * k_k
    kkn = kk / jnp.maximum(jnp.sqrt(head_sum(kk * kk)), 1e-12)
    k2 = k * (1.0 + (a - 1.0) * k_a)
    av = -kkn
    bv = kkn * a
    bonus = head_sum(r * k2 * r_k) * v

    eye = (ri == ci).astype(F32)
    bf = lambda t: t.astype(BF16)

    pre = []
    for c in range(n_chunk):
        rows = slice(c * C, (c + 1) * C)
        lwc = lw[rows]
        cw = _dot_exact_lhs(tri, lwc)
        cwl = cw[C - 1:C, :]
        e_in = jnp.exp(cw)
        e_neg = jnp.exp(-cw)
        e_rem = jnp.exp(cwl - cw)
        wc = jnp.exp(cwl)
        Rt = r[rows] * e_in
        At = av[rows] * jnp.exp(cw - lwc)
        Bb = bv[rows] * e_neg
        Kb = k2[rows] * e_neg
        Bh = bv[rows] * e_rem
        Kh = k2[rows] * e_rem
        vc = v[rows]
        for p in range(n_pair):
            sl = slice(p * LANES, (p + 1) * LANES)
            pre.append(dict(At=At[:, sl], Rt=Rt[:, sl], Bb=Bb[:, sl], Kb=Kb[:, sl], Bh=Bh[:, sl], Kh=Kh[:, sl],
                            v=vc[:, sl], wc=wc[:, sl]))

    for u in pre:
        u["at_bd"] = stack2(u["At"])
        lhs = bf(jnp.concatenate([u["at_bd"], stack2(u["Rt"])], axis=0))
        rhs = bf(jnp.concatenate([stack2(u["Bb"]), stack2(u["Kb"])], axis=0))
        u["G"] = _dot_nt(lhs, rhs)
    for u in pre:
        G = u.pop("G")
        u["a_ab"] = jnp.where(strict, G[0:2 * C, 0:2 * C], 0.0)
        u["a_ak"] = bf(jnp.where(strict, G[0:2 * C, 2 * C:4 * C], 0.0))
        u["a_rb"] = bf(jnp.where(incl, G[2 * C:4 * C, 0:2 * C], 0.0))
        u["a_rk"] = bf(jnp.where(incl, G[2 * C:4 * C, 2 * C:4 * C], 0.0))
        u["v_bd"] = bf(stack2(u["v"]))
    for u in pre:
        xb = bf(u["a_ab"])
        u["P"] = eye + u.pop("a_ab")
        u["X"] = _dot(xb, xb)
        u["M0"] = _dot(u["a_ak"], u["v_bd"])
    for _ in range(int(math.log2(C)) - 2):
        for u in pre:
            Wm = _dot(bf(u["X"]), bf(jnp.concatenate([u["P"], u["X"]], axis=1)))
            u["P"] = u["P"] + Wm[:, 0:LANES]
            u["X"] = Wm[:, LANES:2 * LANES]
    for u in pre:
        u["P"] = bf(u["P"] + _dot(bf(u.pop("X")), bf(u["P"])))
    for u in pre:
        u["M1"] = _dot(u["P"], bf(u.pop("M0")))
        u["Q"] = bf(_dot(u["a_rb"], u["P"]))
        u["PtB"] = _dot_tn(u["P"], bf(stack2(u["Bh"])))
    for u in pre:
        M1 = u.pop("M1")
        u["Y0"] = _dot(jnp.concatenate([u["a_rb"], u["a_rk"]], axis=1),
                       jnp.concatenate([bf(M1), u["v_bd"]], axis=0))
        u["Tm"] = bf(_dot_tn(bf(u["at_bd"]), bf(u.pop("PtB"))))
        m1_pair = M1[0:C] + M1[C:2 * C]
        cst = _dot_tn(bf(jnp.concatenate([m1_pair, u["v"]], axis=0)),
                      bf(jnp.concatenate([u["Bh"], u["Kh"]], axis=0)))
        u["cst"] = jnp.where(same, cst, 0.0)
        u["ar"] = bf(jnp.concatenate([u["At"], u["Rt"]], axis=0))

    states = [state_ref[p] for p in range(n_pair)]
    for c in range(n_chunk):
        rows = slice(c * C, (c + 1) * C)
        us = pre[c * n_pair:(c + 1) * n_pair]
        sbs = [bf(S) for S in states]
        zs = [_dot_nt(u["ar"], sb) for u, sb in zip(us, sbs)]
        new_states = [S * u["wc"] + _dot(sb, u["Tm"]) + u["cst"] for u, S, sb in zip(us, states, sbs)]
        ybds = [stack2(Z[C:2 * C]) + _dot(u["Q"], bf(stack2(Z[0:C]))) + u["Y0"] for u, Z in zip(us, zs)]
        ys = [y_bd[0:C] + y_bd[C:2 * C] for y_bd in ybds]
        states = new_states
        y = jnp.concatenate(ys, axis=1)
        mu = head_sum(y) * (1.0 / HEAD_DIM)
        yc = y - mu
        var = head_sum(yc * yc) * (1.0 / HEAD_DIM)
        yn = yc * lax.rsqrt(var + RWKV_GN_EPS) * ln_w + ln_b
        o_ref[rows, :] = ((yn + bonus[rows]) * g[rows]).astype(BF16)
    for p in range(n_pair):
        state_ref[p] = states[p]


def _rwkv(rw, vecs, w2, a2, g2, B, S, lb):
    nt = S // lb
    return pl.pallas_call(
        functools.partial(_rwkv_kernel, n_chunk=lb // RWKV_CHUNK),
        grid=(B, nt),
        in_specs=[pl.BlockSpec((lb, RWKV_PROJ), lambda b, j: (b * nt + j, 0)),
                  pl.BlockSpec((8, RWKV_WIDTH), lambda b, j: (0, 0)),
                  pl.BlockSpec((LANES, RWKV_WIDTH), lambda b, j: (0, 0)),
                  pl.BlockSpec((LANES, RWKV_WIDTH), lambda b, j: (0, 0)),
                  pl.BlockSpec((LANES, RWKV_WIDTH), lambda b, j: (0, 0))],
        out_specs=pl.BlockSpec((lb, RWKV_WIDTH), lambda b, j: (b * nt + j, 0)),
        out_shape=jax.ShapeDtypeStruct((B * S, RWKV_WIDTH), BF16),
        scratch_shapes=[pltpu.VMEM((RWKV_WIDTH // LANES, LANES, LANES), F32)],
        compiler_params=pltpu.CompilerParams(dimension_semantics=("arbitrary", "arbitrary"),
                                             vmem_limit_bytes=VMEM_LIMIT),
        name="rwkv",
    )(rw, vecs, w2, a2, g2)


def _mix_kernel(at_ref, rk_ref, x_ref, mod_ref, wo_a_ref, wo_r_ref, ln_ref, wr_ref, br_ref,
                x1_ref, h2_ref, ti_ref, gt_ref, rank_ref, cnt_ref, tb_ref, base_ref):
    first = (pl.program_id(0) == 0) & (pl.program_id(1) == 0)

    @pl.when(first)
    def _():
        base_ref[...] = jnp.zeros_like(base_ref)

    mod = mod_ref[0]
    y = _dot(at_ref[...], wo_a_ref[...]) + _dot(rk_ref[...], wo_r_ref[...])
    x = x_ref[...]
    tm = x.shape[0]
    x1 = _layer_norm(DEEPNORM_ALPHA * x + (1.0 + mod[2:3, :]) * y) * ln_ref[0:1, :] + ln_ref[1:2, :]
    h2 = _layer_norm(x1) * (1.0 + mod[4:5, :]) + mod[3:4, :]
    x1_ref[...] = x1
    h_hi = h2.astype(BF16)
    h2_ref[...] = h_hi

    h_lo = (h2 - h_hi.astype(F32)).astype(BF16)
    part = _dot(h_hi, wr_ref[...])
    logits = part[:, 0:LANES] + part[:, LANES:2 * LANES] + _dot(h_lo, wr_ref[:, 0:LANES]) + br_ref[...]
    lt = jnp.transpose(logits)[0:N_EXPERTS, :]
    erow = lax.broadcasted_iota(jnp.int32, (N_EXPERTS, tm), 0).astype(F32)
    cur = lt
    vals, idxs = [], []
    for _ in range(TOP_K):
        m = jnp.max(cur, axis=0, keepdims=True)
        idx = jnp.min(jnp.where(cur == m, erow, float(N_EXPERTS)), axis=0, keepdims=True)
        vals.append(m)
        idxs.append(idx)
        cur = jnp.where(erow == idx, -jnp.inf, cur)
    tv = jnp.concatenate(vals, axis=0)
    e = jnp.exp(tv - tv[0:1, :])
    gt_ref[...] = e / jnp.sum(e, axis=0, keepdims=True)
    ti_ref[...] = jnp.concatenate(idxs, axis=0).astype(jnp.int32)

    onehot = jnp.zeros((N_EXPERTS, tm), F32)
    for idx in idxs:
        onehot = onehot + (erow == idx).astype(F32)
    before = (lax.broadcasted_iota(jnp.int32, (tm, tm), 0)
              < lax.broadcasted_iota(jnp.int32, (tm, tm), 1)).astype(BF16)
    tot = base_ref[:, 0:1] + _dot(onehot.astype(BF16), before)
    ranks = [jnp.sum(jnp.where(erow == idx, tot, 0.0), axis=0, keepdims=True) for idx in idxs]
    rank_ref[...] = jnp.concatenate(ranks, axis=0).astype(jnp.int32)
    tb_ref[0] = base_ref[...].astype(jnp.int32)
    base_ref[...] = base_ref[...] + jnp.sum(onehot, axis=1, keepdims=True)
    cnt_ref[...] = base_ref[...].astype(jnp.int32)


def _mix(attn_out, rwkv_out, x2d, mod, wo_a, wo_r, ln1, w_router, b_router, B, S, tm):
    D = x2d.shape[1]
    nt = S // tm
    T = B * S
    tok = lambda b, j: (b * nt + j, 0)
    col = lambda b, j: (0, b * nt + j)
    fixed = lambda b, j: (0, 0)
    return pl.pallas_call(
        _mix_kernel,
        grid=(B, nt),
        in_specs=[pl.BlockSpec((tm, ATTN_WIDTH), tok),
                  pl.BlockSpec((tm, RWKV_WIDTH), tok),
                  pl.BlockSpec((tm, D), tok),
                  pl.BlockSpec((1, 6, D), lambda b, j: (b, 0, 0)),
                  pl.BlockSpec((ATTN_WIDTH, D), fixed),
                  pl.BlockSpec((RWKV_WIDTH, D), fixed),
                  pl.BlockSpec((2, D), fixed),
                  pl.BlockSpec((D, 2 * LANES), fixed),
                  pl.BlockSpec((1, LANES), fixed)],
        out_specs=[pl.BlockSpec((tm, D), tok),
                   pl.BlockSpec((tm, D), tok),
                   pl.BlockSpec((TOP_K, tm), col),
                   pl.BlockSpec((TOP_K, tm), col),
                   pl.BlockSpec((TOP_K, tm), col),
                   pl.BlockSpec((N_EXPERTS, LANES), fixed),
                   pl.BlockSpec((1, N_EXPERTS, LANES), lambda b, j: (b * nt + j, 0, 0))],
        out_shape=[jax.ShapeDtypeStruct((T, D), F32),
                   jax.ShapeDtypeStruct((T, D), BF16),
                   jax.ShapeDtypeStruct((TOP_K, T), jnp.int32),
                   jax.ShapeDtypeStruct((TOP_K, T), F32),
                   jax.ShapeDtypeStruct((TOP_K, T), jnp.int32),
                   jax.ShapeDtypeStruct((N_EXPERTS, LANES), jnp.int32),
                   jax.ShapeDtypeStruct((T // tm, N_EXPERTS, LANES), jnp.int32)],
        scratch_shapes=[pltpu.VMEM((N_EXPERTS, LANES), F32)],
        compiler_params=pltpu.CompilerParams(dimension_semantics=("arbitrary", "arbitrary"),
                                             vmem_limit_bytes=VMEM_LIMIT),
        name="mix",
    )(attn_out, rwkv_out, x2d, mod, wo_a, wo_r, ln1, w_router, b_router)


RUN_PIECES = tuple(2 ** b for b in range(int(math.log2(MOE_TILE)), -1, -1))
SUBLANES = 8


def _to_tiles(ref, x):
    n = x.shape[0]
    for c in range(SUBLANES):
        ref[pl.ds(c, n, stride=SUBLANES), :] = x[:, c * LANES:(c + 1) * LANES]


def _from_tiles(ref):
    n = ref.shape[0] // SUBLANES
    return jnp.concatenate([ref[pl.ds(c, n, stride=SUBLANES), :] for c in range(SUBLANES)], axis=1)


def _run_copies(n, local, local_start, remote, remote_start, sem, to_remote):
    off = 0
    for piece in RUN_PIECES:
        take = (n & piece) != 0

        @pl.when(take)
        def _(off=off, piece=piece):
            lo = pl.multiple_of((local_start + off) * SUBLANES, SUBLANES)
            ro = pl.multiple_of((remote_start + off) * SUBLANES, SUBLANES)
            loc = local.at[pl.ds(lo, piece * SUBLANES)]
            rem = remote.at[pl.ds(ro, piece * SUBLANES)]
            src, dst = (loc, rem) if to_remote else (rem, loc)
            pltpu.make_async_copy(src, dst, sem).start()

        off = off + (n & piece)


def _dispatch_kernel(tcnt_ref, lstart_ref, gstart_ref, pad_ref, pad_start_ref, n_used_ref, lpos_ref, h2_ref,
                     xs_ref, xbuf, zbuf, sems):
    i = pl.program_id(0)
    tm = h2_ref.shape[0]
    n_loc = TOP_K * tm
    n_blocks = xs_ref.shape[0] // (MOE_BLOCK * SUBLANES)
    zero_sem = sems.at[2]

    @pl.when(i == 0)
    def _():
        zbuf[...] = jnp.zeros_like(zbuf)

        def zero_pad(e, carry):
            _run_copies(pad_ref[e], zbuf, 0, xs_ref, pad_start_ref[e], zero_sem, True)
            return carry

        def zero_tail(b, carry):
            @pl.when(b >= n_used_ref[0])
            def _():
                start = pl.multiple_of(b * (MOE_BLOCK * SUBLANES), MOE_BLOCK * SUBLANES)
                pltpu.make_async_copy(zbuf, xs_ref.at[pl.ds(start, MOE_BLOCK * SUBLANES)], zero_sem).start()
            return carry

        lax.fori_loop(0, N_EXPERTS, zero_pad, 0)
        lax.fori_loop(n_blocks - N_EXPERTS, n_blocks, zero_tail, 0)

    slot = lax.broadcasted_iota(jnp.int32, (n_loc, tm), 0)
    lpos = lpos_ref[...]
    perm = jnp.zeros((n_loc, tm), F32)
    for k in range(TOP_K):
        perm = perm + (slot == lpos[k:k + 1, :]).astype(F32)
    perm = perm.astype(BF16)

    def wait_tile(s):
        pltpu.make_async_copy(xbuf.at[s], xs_ref.at[pl.ds(0, n_loc * SUBLANES)], sems.at[s]).wait()

    cur = i % 2

    @pl.when(i >= 2)
    def _():
        wait_tile(cur)

    _to_tiles(xbuf.at[cur], _dot(perm, h2_ref[...]))

    def issue(e, carry):
        idx = i * N_EXPERTS + e
        _run_copies(tcnt_ref[idx], xbuf.at[cur], lstart_ref[idx], xs_ref, gstart_ref[idx], sems.at[cur], True)
        return carry

    lax.fori_loop(0, N_EXPERTS, issue, 0)

    @pl.when(i == pl.num_programs(0) - 1)
    def _():
        wait_tile(cur)

        @pl.when(i >= 1)
        def _():
            wait_tile(1 - cur)

        n_zero = N_EXPERTS * MOE_BLOCK * SUBLANES
        pltpu.make_async_copy(xs_ref.at[pl.ds(0, n_zero)], xs_ref.at[pl.ds(0, n_zero)], zero_sem).wait()


def _dispatch(tcnt, lstart, gstart, pad, pad_start, n_used, lpos, h2, n_rows, tm):
    T, D = h2.shape
    grid_spec = pltpu.PrefetchScalarGridSpec(
        num_scalar_prefetch=6,
        grid=(T // tm,),
        in_specs=[pl.BlockSpec((TOP_K, tm), lambda i, *_: (0, i)),
                  pl.BlockSpec((tm, D), lambda i, *_: (i, 0))],
        out_specs=pl.BlockSpec(memory_space=pl.ANY),
        scratch_shapes=[pltpu.VMEM((2, TOP_K * tm * SUBLANES, LANES), F32),
                        pltpu.VMEM((MOE_BLOCK * SUBLANES, LANES), F32),
                        pltpu.SemaphoreType.DMA((3,))],
    )
    return pl.pallas_call(
        _dispatch_kernel,
        grid_spec=grid_spec,
        out_shape=jax.ShapeDtypeStruct((n_rows * SUBLANES, LANES), F32),
        compiler_params=pltpu.CompilerParams(dimension_semantics=("arbitrary",),
                                             vmem_limit_bytes=VMEM_LIMIT),
        name="dispatch",
    )(tcnt, lstart, gstart, pad, pad_start, n_used, lpos, h2)


def _experts_kernel(blk_e_ref, n_used_ref, next_e_ref, xs_ref, wgu_hbm, bgu_ref, wd_hbm, bd_ref, ys_ref,
                    wgu_f32, wd_f32, wgu_bf, wd_bf, sems):
    i = pl.program_id(0)
    d_ff = wd_bf.shape[0]
    used = i < n_used_ref[0]
    e = blk_e_ref[i]
    new_expert = (i == 0) | (e != blk_e_ref[jnp.maximum(i - 1, 0)])

    def weight_copies(ex):
        return (pltpu.make_async_copy(wgu_hbm.at[ex], wgu_f32, sems.at[0]),
                pltpu.make_async_copy(wd_hbm.at[ex], wd_f32, sems.at[1]))

    @pl.when(i == 0)
    def _():
        for cp in weight_copies(e):
            cp.start()

    @pl.when(used & new_expert)
    def _():
        for cp in weight_copies(e):
            cp.wait()
        wgu_bf[...] = wgu_f32[...].astype(BF16)
        wd_bf[...] = wd_f32[...].astype(BF16)
        nxt = next_e_ref[i]

        @pl.when(nxt >= 0)
        def _():
            for cp in weight_copies(nxt):
                cp.start()

    @pl.when(used)
    def _():
        xb = _from_tiles(xs_ref).astype(BF16)
        gu = _dot(xb, wgu_bf[...]) + bgu_ref[0]
        gate = jnp.minimum(gu[:, :d_ff], SWIGLU_LIMIT)
        up = jnp.clip(gu[:, d_ff:], -SWIGLU_LIMIT, SWIGLU_LIMIT)
        act = (up + 1.0) * (gate * _sigmoid(SWIGLU_ALPHA * gate))
        _to_tiles(ys_ref, _dot(act.astype(BF16), wd_bf[...]) + bd_ref[0])

    @pl.when(i >= n_used_ref[0])
    def _():
        ys_ref[...] = jnp.zeros_like(ys_ref)


def _experts(blk_e, n_used, next_e, xs, wgu, bgu, wd, bd):
    d_ff, D = wd.shape[1], wd.shape[2]
    n_blocks = xs.shape[0] // (MOE_BLOCK * SUBLANES)
    blk = (MOE_BLOCK * SUBLANES, LANES)

    def last_used(i, n_used_ref):
        return jnp.minimum(i, jnp.maximum(n_used_ref[0] - 1, 0))

    def row_map(i, blk_e_ref, n_used_ref, next_e_ref):
        return (last_used(i, n_used_ref), 0)

    def exp_map(i, blk_e_ref, n_used_ref, next_e_ref):
        return (blk_e_ref[last_used(i, n_used_ref)], 0, 0)

    grid_spec = pltpu.PrefetchScalarGridSpec(
        num_scalar_prefetch=3,
        grid=(n_blocks,),
        in_specs=[pl.BlockSpec(blk, row_map),
                  pl.BlockSpec(memory_space=pl.ANY),
                  pl.BlockSpec((1, 1, 2 * d_ff), exp_map),
                  pl.BlockSpec(memory_space=pl.ANY),
                  pl.BlockSpec((1, 1, D), exp_map)],
        out_specs=pl.BlockSpec(blk, lambda i, *_: (i, 0)),
        scratch_shapes=[pltpu.VMEM((D, 2 * d_ff), F32), pltpu.VMEM((d_ff, D), F32),
                        pltpu.VMEM((D, 2 * d_ff), BF16), pltpu.VMEM((d_ff, D), BF16),
                        pltpu.SemaphoreType.DMA((2,))],
    )
    return pl.pallas_call(
        _experts_kernel,
        grid_spec=grid_spec,
        out_shape=jax.ShapeDtypeStruct(xs.shape, F32),
        compiler_params=pltpu.CompilerParams(dimension_semantics=("arbitrary",),
                                             vmem_limit_bytes=VMEM_LIMIT),
        name="experts",
    )(blk_e, n_used, next_e, xs, wgu, bgu, wd, bd)


def _combine_kernel(tcnt_ref, lstart_ref, gstart_ref, ys_ref, lpos_ref, gt_ref, x1_ref, mod_ref, ln_ref,
                    o_ref, buf, sems):
    nt = pl.num_programs(1)
    n_tiles = pl.num_programs(0) * nt
    i = pl.program_id(0) * nt + pl.program_id(1)
    tm = x1_ref.shape[0]
    n_loc = TOP_K * tm

    def fetch(tile, s):
        def issue(e, carry):
            idx = tile * N_EXPERTS + e
            _run_copies(tcnt_ref[idx], buf.at[s], lstart_ref[idx], ys_ref, gstart_ref[idx], sems.at[s], False)
            return carry

        lax.fori_loop(0, N_EXPERTS, issue, 0)

    @pl.when(i == 0)
    def _():
        fetch(i, 0)

    @pl.when(i + 1 < n_tiles)
    def _():
        fetch(i + 1, (i + 1) % 2)

    slot = lax.broadcasted_iota(jnp.int32, (tm, n_loc), 1)
    lpos = lpos_ref[...]
    gt = gt_ref[...]
    pick = jnp.zeros((tm, n_loc), F32)
    for k in range(TOP_K):
        pick = pick + jnp.where(slot == lpos[:, k:k + 1], gt[:, k:k + 1], 0.0)
    cur = i % 2
    pltpu.make_async_copy(ys_ref.at[pl.ds(0, n_loc * SUBLANES)], buf.at[cur], sems.at[cur]).wait()
    y = _dot(pick.astype(BF16), _from_tiles(buf.at[cur]).astype(BF16))
    z = DEEPNORM_ALPHA * x1_ref[...] + (1.0 + mod_ref[0][5:6, :]) * y
    o_ref[...] = _layer_norm(z) * ln_ref[0:1, :] + ln_ref[1:2, :]


def _combine(tcnt, lstart, gstart, ys, lpos_t, gates_t, x1, mod, ln2, B, S, tm):
    T, D = x1.shape
    nt = S // tm
    tok = lambda b, j, *_: (b * nt + j, 0)
    grid_spec = pltpu.PrefetchScalarGridSpec(
        num_scalar_prefetch=3,
        grid=(B, nt),
        in_specs=[pl.BlockSpec(memory_space=pl.ANY),
                  pl.BlockSpec((tm, TOP_K), tok),
                  pl.BlockSpec((tm, TOP_K), tok),
                  pl.BlockSpec((tm, D), tok),
                  pl.BlockSpec((1, 6, D), lambda b, j, *_: (b, 0, 0)),
                  pl.BlockSpec((2, D), lambda b, j, *_: (0, 0))],
        out_specs=pl.BlockSpec((tm, D), tok),
        scratch_shapes=[pltpu.VMEM((2, TOP_K * tm * SUBLANES, LANES), F32),
                        pltpu.SemaphoreType.DMA((2,))],
    )
    return pl.pallas_call(
        _combine_kernel,
        grid_spec=grid_spec,
        out_shape=jax.ShapeDtypeStruct((T, D), F32),
        compiler_params=pltpu.CompilerParams(dimension_semantics=("arbitrary", "arbitrary"),
                                             vmem_limit_bytes=VMEM_LIMIT),
        name="combine",
    )(tcnt, lstart, gstart, ys, lpos_t, gates_t, x1, mod, ln2)


def _pad_rows(w, rows):
    return jnp.pad(w, ((0, rows - w.shape[0]), (0, 0)))


def _pad_cols(w, cols):
    return jnp.pad(w, ((0, 0), (0, cols - w.shape[1])))


def _layer(x, c, positions, w_ada, b_ada, w_in, shift_mu, rwkv_w0, rwkv_w2, rwkv_a0, rwkv_a2, rwkv_g2,
           rwkv_k_k, rwkv_k_a, rwkv_r_k, rwkv_ln_w, rwkv_ln_b, attn_sinks, w_out, ln1_g, ln1_b,
           w_router, b_router, w_gate_up, b_gate_up, w_down, b_down, ln2_g, ln2_b):
    B, S, D = x.shape
    T = B * S

    q0, k0, v0 = 0, ATTN_WIDTH, ATTN_WIDTH + KV_WIDTH
    r0 = ATTN_WIDTH + 2 * KV_WIDTH
    heads = lambda base: [w_in[:, base + h * HEAD_DIM: base + (h + 1) * HEAD_DIM] for h in range(N_KV_HEADS)]
    dup = lambda hs: [w for w in hs for _ in range(2)]
    w_attn = jnp.concatenate([w_in[:, q0:q0 + ATTN_WIDTH]] + dup(heads(k0)) + dup(heads(v0)), axis=1).astype(BF16)
    lora0 = r0 + 3 * RWKV_WIDTH
    lora = (DECAY_LORA, AAA_LORA, GATE_LORA)
    pieces_w = [w_in[:, r0:lora0]]
    pieces_mu = [shift_mu[None, 0:3 * RWKV_WIDTH]]
    off = lora0
    for n in lora:
        pieces_w.append(_pad_cols(w_in[:, off:off + n], LANES))
        pieces_mu.append(_pad_cols(shift_mu[None, off - r0:off - r0 + n], LANES))
        off += n
    w_rwkv = jnp.concatenate(pieces_w, axis=1).astype(BF16)
    mu = jnp.concatenate(pieces_mu, axis=1)
    inv_freq = ROPE_THETA ** (-jnp.arange(0, ROT_DIM, 2, dtype=F32) / ROT_DIM)
    lane_p = jnp.arange(LANES) % HEAD_DIM
    n_freq = ROT_DIM // 2
    rot_tab = jnp.zeros((8, LANES), F32)
    rot_tab = rot_tab.at[0].set(jnp.where(lane_p < ROT_DIM, 0.0, 1.0))
    rot_tab = rot_tab.at[1].set(jnp.where(lane_p < n_freq, -1.0, 0.0))
    rot_tab = rot_tab.at[2].set(jnp.where((lane_p >= n_freq) & (lane_p < ROT_DIM), 1.0, 0.0))
    freq_tab = jnp.broadcast_to(inv_freq[:, None], (n_freq, LANES))
    lane_freq = (jnp.arange(n_freq)[:, None] == (lane_p % n_freq)[None, :]) & (lane_p < ROT_DIM)[None, :]
    zeros = jnp.zeros_like(lane_freq)
    expand = jnp.concatenate([jnp.concatenate([lane_freq, zeros], axis=1),
                              jnp.concatenate([zeros, lane_freq], axis=1)], axis=0).astype(BF16)
    vecs = jnp.stack([rwkv_w0, rwkv_a0, rwkv_k_k, rwkv_k_a, rwkv_r_k.reshape(-1), rwkv_ln_w, rwkv_ln_b,
                      jnp.zeros_like(rwkv_w0)])
    w2 = _pad_rows(rwkv_w2, LANES).astype(BF16)
    a2 = _pad_rows(rwkv_a2, LANES).astype(BF16)
    g2 = _pad_rows(rwkv_g2, LANES).astype(BF16)
    wo_a = w_out[:ATTN_WIDTH].astype(BF16)
    wo_r = w_out[ATTN_WIDTH:].astype(BF16)
    w_r_hi = w_router.astype(BF16)
    w_r_lo = (w_router - w_r_hi.astype(F32)).astype(BF16)
    w_r = jnp.concatenate([_pad_cols(w_r_hi, LANES), _pad_cols(w_r_lo, LANES)], axis=1)
    b_r = jnp.concatenate([b_router, jnp.full((LANES - N_EXPERTS,), NEG_INF, F32)])[None, :]

    mod = _mod(c, w_ada, b_ada).reshape(B, 6, D)
    attn_out, rw = _inproj(x, positions, mod, w_attn, w_rwkv, mu, rot_tab, freq_tab, expand, attn_sinks,
                           min(INPROJ_TILE, S))
    rwkv_out = _rwkv(rw, vecs, w2, a2, g2, B, S, min(RWKV_STEP, S))

    mtile = min(MOE_TILE, S)
    x1, h2, top_i, gates, rank, cnt, tbase = _mix(attn_out, rwkv_out, x.reshape(T, D), mod, wo_a, wo_r,
                                                   jnp.stack([ln1_g, ln1_b]), w_r, b_r, B, S, mtile)

    counts = cnt[:, 0]
    padded = (counts + MOE_BLOCK - 1) // MOE_BLOCK * MOE_BLOCK
    pend = jnp.cumsum(padded)
    pstart = pend - padded
    n_blocks = T * TOP_K // MOE_BLOCK + N_EXPERTS
    blk_row = jnp.arange(n_blocks, dtype=jnp.int32) * MOE_BLOCK
    blk_e = jnp.minimum(jnp.sum((pend[None, :] <= blk_row[:, None]).astype(jnp.int32), axis=1), N_EXPERTS - 1)
    n_used = (pend[-1:] // MOE_BLOCK).astype(jnp.int32)
    tb = tbase[:, :, 0]
    tcnt = jnp.concatenate([tb[1:], counts[None]], axis=0) - tb
    lstart = jnp.cumsum(tcnt, axis=1) - tcnt
    gstart = pstart[None, :] + tb
    shift = jnp.repeat(jnp.transpose(lstart - tb), mtile, axis=1)
    experts = jnp.arange(N_EXPERTS, dtype=jnp.int32)
    lpos = rank + jnp.sum(jnp.where(top_i[None] == experts[:, None, None], shift[:, None, :], 0), axis=0)
    flat = lambda a: a.reshape(-1).astype(jnp.int32)

    xs = _dispatch(flat(tcnt), flat(lstart), flat(gstart), flat(padded - counts), flat(pstart + counts), n_used,
                   lpos, h2, n_blocks * MOE_BLOCK, mtile)
    later_with_rows = (experts[None, :] > experts[:, None]) & (counts[None, :] > 0)
    next_of = jnp.min(jnp.where(later_with_rows, experts[None, :], N_EXPERTS), axis=1)
    next_of = jnp.where(next_of < N_EXPERTS, next_of, -1)
    next_e = jnp.sum(jnp.where(blk_e[:, None] == experts[None, :], next_of[None, :], 0), axis=1).astype(jnp.int32)
    ys = _experts(blk_e, n_used, next_e, xs, w_gate_up, b_gate_up[:, None, :], w_down, b_down[:, None, :])
    out = _combine(flat(tcnt), flat(lstart), flat(gstart), ys, jnp.transpose(lpos), jnp.transpose(gates), x1, mod,
                   jnp.stack([ln2_g, ln2_b]), B, S, mtile)
    return out.reshape(B, S, D)


def kernel(x, c, positions, w_ada, b_ada, w_in, shift_mu, rwkv_w0, rwkv_w2, rwkv_a0, rwkv_a2, rwkv_g2,
           rwkv_k_k, rwkv_k_a, rwkv_r_k, rwkv_ln_w, rwkv_ln_b, attn_sinks, w_out, ln1_g, ln1_b,
           w_router, b_router, w_gate_up, b_gate_up, w_down, b_down, ln2_g, ln2_b):
    for l in range(DEPTH):
        x = _layer(x, c, positions, w_ada[l], b_ada[l], w_in[l], shift_mu[l], rwkv_w0[l], rwkv_w2[l],
                   rwkv_a0[l], rwkv_a2[l], rwkv_g2[l], rwkv_k_k[l], rwkv_k_a[l], rwkv_r_k[l], rwkv_ln_w[l],
                   rwkv_ln_b[l], attn_sinks[l], w_out[l], ln1_g[l], ln1_b[l], w_router[l], b_router[l],
                   w_gate_up[l], b_gate_up[l], w_down[l], b_down[l], ln2_g[l], ln2_b[l])
    return x
```

```python
import functools
import math

import jax
import jax.numpy as jnp
from jax import lax
from jax.experimental import pallas as pl
from jax.experimental.pallas import tpu as pltpu

F32 = jnp.float32
BF16 = jnp.bfloat16

HEAD_DIM = 64
N_ATTN_HEADS = 8
N_KV_HEADS = 2
N_RWKV_HEADS = 8
ATTN_WIDTH = N_ATTN_HEADS * HEAD_DIM
KV_WIDTH = N_KV_HEADS * HEAD_DIM
RWKV_WIDTH = N_RWKV_HEADS * HEAD_DIM
ATTN_BLOCK = 128
ROT_DIM = HEAD_DIM // 4
ROPE_THETA = 500000.0
DECAY_LORA = 32
AAA_LORA = 32
GATE_LORA = 96
N_EXPERTS = 32
TOP_K = 4
SWIGLU_LIMIT = 7.0
SWIGLU_ALPHA = 1.702
LN_EPS = 1e-5
RWKV_GN_EPS = 64e-5
NEG_INF = -1e30
DEPTH = 1
DEEPNORM_ALPHA = (2 * DEPTH) ** 0.25

LANES = 128
RWKV_CHUNK = 64
RWKV_STEP = 512
INPROJ_TILE = 512
RWKV_PROJ_CHUNK = 512
MOE_BLOCK = 512
MOE_TILE = 256
ATTN_PROJ = ATTN_WIDTH + 4 * KV_WIDTH
RWKV_PROJ = 3 * RWKV_WIDTH + 3 * LANES
VMEM_LIMIT = 48 * 1024 * 1024


def _dot(a, b):
    return jnp.dot(a, b, preferred_element_type=F32)


def _dot_nt(a, b):
    return lax.dot_general(a, b, (((1,), (1,)), ((), ())), preferred_element_type=F32)


def _dot_tn(a, b):
    return lax.dot_general(a, b, (((0,), (0,)), ((), ())), preferred_element_type=F32)


def _split3(x):
    h = x.astype(BF16)
    r1 = x - h.astype(F32)
    m = r1.astype(BF16)
    lo = (r1 - m.astype(F32)).astype(BF16)
    return h, m, lo


def _dot_exact_lhs(m_bf16, x):
    h, m, lo = _split3(x)
    return _dot(m_bf16, h) + _dot(m_bf16, m) + _dot(m_bf16, lo)


def _layer_norm(x):
    mu = jnp.mean(x, axis=-1, keepdims=True)
    xc = x - mu
    var = jnp.mean(xc * xc, axis=-1, keepdims=True)
    return xc * lax.rsqrt(var + LN_EPS)


def _sigmoid(x):
    return 1.0 / (1.0 + jnp.exp(-x))


def _mod_kernel(c_ref, w_ref, b_ref, o_ref):
    c = c_ref[...]
    s = c * _sigmoid(c)
    o_ref[...] = jnp.dot(s, w_ref[...], preferred_element_type=F32,
                         precision=lax.Precision.HIGHEST) + b_ref[...]


def _mod(c, w_ada, b_ada):
    B, D = c.shape
    n = w_ada.shape[1] // D
    return pl.pallas_call(
        _mod_kernel,
        grid=(n,),
        in_specs=[pl.BlockSpec((B, D), lambda i: (0, 0)),
                  pl.BlockSpec((D, D), lambda i: (0, i)),
                  pl.BlockSpec((1, D), lambda i: (0, i))],
        out_specs=pl.BlockSpec((B, D), lambda i: (0, i)),
        out_shape=jax.ShapeDtypeStruct((B, n * D), F32),
        compiler_params=pltpu.CompilerParams(dimension_semantics=("arbitrary",),
                                             vmem_limit_bytes=VMEM_LIMIT),
        name="mod",
    )(c, w_ada, b_ada.reshape(1, -1))


def _inproj_kernel(x_ref, pos_ref, mod_ref, wa_ref, wr_ref, mu_ref, rt_ref, fq_ref, ex_ref, sink_ref,
                   at_ref, rw_ref, qkv_ref, kv_prev_ref, carry_ref):
    j = pl.program_id(1)

    @pl.when(j == 0)
    def _():
        kv_prev_ref[...] = jnp.zeros_like(kv_prev_ref)

    x = x_ref[0]
    tm = x.shape[0]
    mod = mod_ref[0]
    h = _layer_norm(x) * (1.0 + mod[1:2, :]) + mod[0:1, :]
    hb = h.astype(BF16)

    pa = _dot(hb, wa_ref[...])
    ang = pos_ref[0].astype(F32) * fq_ref[:, 0:1]
    pieces = _split3(jnp.concatenate([jnp.cos(ang), jnp.sin(ang)], axis=0))
    trig = _dot_tn(pieces[0], ex_ref[...]) + _dot_tn(pieces[1], ex_ref[...]) + _dot_tn(pieces[2], ex_ref[...])
    cs = trig[:, 0:LANES] + rt_ref[0:1, :]
    sn = trig[:, LANES:2 * LANES]
    m_lo = rt_ref[1:2, :]
    m_hi = rt_ref[2:3, :]
    n_q = ATTN_WIDTH // LANES
    n_rot = (ATTN_WIDTH + 2 * KV_WIDTH) // LANES
    for ch in range(n_rot):
        t = pa[:, ch * LANES:(ch + 1) * LANES]
        if ch < n_q:
            t = t * (1.0 / math.sqrt(HEAD_DIM))
        up = pltpu.roll(t, LANES - ROT_DIM // 2, 1)
        dn = pltpu.roll(t, ROT_DIM // 2, 1)
        o = t * cs + sn * (m_lo * up + m_hi * dn)
        qkv_ref[:, ch * LANES:(ch + 1) * LANES] = o.astype(BF16)
    qkv_ref[:, n_rot * LANES:] = pa[:, n_rot * LANES:].astype(BF16)

    row = lax.broadcasted_iota(jnp.int32, (tm, 1), 0)

    def rwkv_columns(c0, c1):
        def emit():
            pr = _dot(hb, wr_ref[:, c0:c1])
            prev = pltpu.roll(pr, 1, 0)
            carry = jnp.where(j == 0, 0.0, carry_ref[:, c0:c1])
            prev = jnp.where(row == 0, carry, prev)
            carry_ref[:, c0:c1] = pr[tm - 1:tm, :]
            rw_ref[:, c0:c1] = pr + (prev - pr) * mu_ref[:, c0:c1]
        return emit

    bounds = list(range(0, RWKV_PROJ, RWKV_PROJ_CHUNK)) + [RWKV_PROJ]
    kv_w = 2 * KV_WIDTH
    _attn_body(qkv_ref.at[:, pl.ds(0, ATTN_WIDTH)], qkv_ref.at[:, pl.ds(ATTN_WIDTH, kv_w)],
               kv_prev_ref.at[:, pl.ds(0, kv_w)], qkv_ref.at[:, pl.ds(ATTN_WIDTH + kv_w, kv_w)],
               kv_prev_ref.at[:, pl.ds(kv_w, kv_w)], sink_ref, at_ref, j > 0,
               interleave=[rwkv_columns(c0, c1) for c0, c1 in zip(bounds[:-1], bounds[1:])])
    kv_prev_ref[...] = qkv_ref[tm - ATTN_BLOCK:tm, ATTN_WIDTH:]


def _inproj(x, positions, mod, w_attn, w_rwkv, mu, rot_tab, freq_tab, expand, sinks, tm):
    B, S, D = x.shape
    nt = S // tm
    return pl.pallas_call(
        _inproj_kernel,
        grid=(B, nt),
        in_specs=[pl.BlockSpec((1, tm, D), lambda b, j: (b, j, 0)),
                  pl.BlockSpec((1, 1, tm), lambda b, j: (b, 0, j)),
                  pl.BlockSpec((1, 6, D), lambda b, j: (b, 0, 0)),
                  pl.BlockSpec((D, ATTN_PROJ), lambda b, j: (0, 0)),
                  pl.BlockSpec((D, RWKV_PROJ), lambda b, j: (0, 0)),
                  pl.BlockSpec((1, RWKV_PROJ), lambda b, j: (0, 0)),
                  pl.BlockSpec((8, LANES), lambda b, j: (0, 0)),
                  pl.BlockSpec(freq_tab.shape, lambda b, j: (0, 0)),
                  pl.BlockSpec(expand.shape, lambda b, j: (0, 0)),
                  pl.BlockSpec(memory_space=pltpu.SMEM)],
        out_specs=[pl.BlockSpec((tm, ATTN_WIDTH), lambda b, j: (b * nt + j, 0)),
                   pl.BlockSpec((tm, RWKV_PROJ), lambda b, j: (b * nt + j, 0))],
        out_shape=[jax.ShapeDtypeStruct((B * S, ATTN_WIDTH), BF16),
                   jax.ShapeDtypeStruct((B * S, RWKV_PROJ), F32)],
        scratch_shapes=[pltpu.VMEM((tm, ATTN_PROJ), BF16),
                        pltpu.VMEM((ATTN_BLOCK, ATTN_PROJ - ATTN_WIDTH), BF16),
                        pltpu.VMEM((1, RWKV_PROJ), F32)],
        compiler_params=pltpu.CompilerParams(dimension_semantics=("arbitrary", "arbitrary"),
                                             vmem_limit_bytes=VMEM_LIMIT),
        name="inproj",
    )(x, positions.reshape(B, 1, S), mod, w_attn, w_rwkv, mu, rot_tab, freq_tab, expand, sinks)


def _attn_body(q_ref, kc_ref, kp_ref, vc_ref, vp_ref, sink_ref, o_ref, has_prev, interleave=()):
    blk = ATTN_BLOCK
    n_sub = q_ref.shape[0] // blk
    qi = lax.broadcasted_iota(jnp.int32, (blk, 2 * blk), 0)
    kj = lax.broadcasted_iota(jnp.int32, (blk, 2 * blk), 1)
    band = (kj > qi) & (kj <= qi + blk)
    first = band & ((kj >= blk) | has_prev)
    lane = lax.broadcasted_iota(jnp.int32, (1, LANES), 1)
    lo = (lane < HEAD_DIM).astype(BF16)
    hi = (lane >= HEAD_DIM).astype(BF16)
    halves = {}
    for u in range(n_sub):
        for g in range(N_KV_HEADS):
            sl = slice(g * LANES, (g + 1) * LANES)
            prev_k = kp_ref[:, sl] if u == 0 else kc_ref[(u - 1) * blk:u * blk, sl]
            prev_v = vp_ref[:, sl] if u == 0 else vc_ref[(u - 1) * blk:u * blk, sl]
            kcat = jnp.concatenate([prev_k, kc_ref[u * blk:(u + 1) * blk, sl]], axis=0)
            vcat = jnp.concatenate([prev_v, vc_ref[u * blk:(u + 1) * blk, sl]], axis=0)
            halves[u, g] = ((kcat * lo, vcat * lo), (kcat * hi, vcat * hi))
    units = [(u, c, half) for u in range(n_sub) for c in range(ATTN_WIDTH // LANES) for half in range(2)]
    scores = [_dot_nt(q_ref[u * blk:(u + 1) * blk, c * LANES:(c + 1) * LANES], halves[u, c // 2][half][0])
              for u, c, half in units]
    probs, denoms = [], []
    every = max(1, len(units) // max(1, len(interleave)))
    for n, ((u, c, half), s) in enumerate(zip(units, scores)):
        if n % every == 0 and n // every < len(interleave):
            interleave[n // every]()
        sink = sink_ref[2 * c + half]
        s = jnp.where(first if u == 0 else band, s, NEG_INF)
        m = jnp.maximum(jnp.max(s, axis=-1, keepdims=True), sink)
        p = jnp.exp(s - m)
        denoms.append(jnp.sum(p, axis=-1, keepdims=True) + jnp.exp(sink - m))
        probs.append(p.astype(BF16))
    outs = [_dot(p, halves[u, c // 2][half][1]) / d for (u, c, half), p, d in zip(units, probs, denoms)]
    for n, (u, c, half) in enumerate(units):
        if half == 0:
            o_ref[u * blk:(u + 1) * blk, c * LANES:(c + 1) * LANES] = (outs[n] + outs[n + 1]).astype(BF16)


def _rwkv_kernel(rw_ref, vec_ref, w2_ref, a2_ref, g2_ref, o_ref, state_ref, *, n_chunk):
    j = pl.program_id(1)
    C = RWKV_CHUNK
    W = RWKV_WIDTH
    n_pair = W // LANES

    @pl.when(j == 0)
    def _():
        state_ref[...] = jnp.zeros_like(state_ref)

    w0 = vec_ref[0:1, :]
    a0 = vec_ref[1:2, :]
    k_k = vec_ref[2:3, :]
    k_a = vec_ref[3:4, :]
    r_k = vec_ref[4:5, :]
    ln_w = vec_ref[5:6, :]
    ln_b = vec_ref[6:7, :]

    r = rw_ref[:, 0:W]
    k = rw_ref[:, W:2 * W]
    v = rw_ref[:, 2 * W:3 * W]
    wl = rw_ref[:, 3 * W:3 * W + LANES]
    al = rw_ref[:, 3 * W + LANES:3 * W + 2 * LANES]
    gl = rw_ref[:, 3 * W + 2 * LANES:3 * W + 3 * LANES]

    ri = lax.broadcasted_iota(jnp.int32, (LANES, LANES), 0)
    ci = lax.broadcasted_iota(jnp.int32, (LANES, LANES), 1)
    same = (ri // HEAD_DIM) == (ci // HEAD_DIM)
    strict = same & ((ri % HEAD_DIM) > (ci % HEAD_DIM))
    incl = same & ((ri % HEAD_DIM) >= (ci % HEAD_DIM))
    lane = lax.broadcasted_iota(jnp.int32, (1, LANES), 1)
    m0 = (lane < HEAD_DIM).astype(F32)
    m1 = 1.0 - m0
    tri = (lax.broadcasted_iota(jnp.int32, (C, C), 0) >= lax.broadcasted_iota(jnp.int32, (C, C), 1)).astype(BF16)

    def head_sum(xv):
        outs = []
        for p in range(n_pair):
            xp = xv[:, p * LANES:(p + 1) * LANES]
            s0 = jnp.sum(xp * m0, axis=1, keepdims=True)
            s1 = jnp.sum(xp * m1, axis=1, keepdims=True)
            outs.append(s0 * m0 + s1 * m1)
        return jnp.concatenate(outs, axis=1)

    def stack2(xp):
        return jnp.concatenate([xp * m0, xp * m1], axis=0)

    z = w0 + _dot(jnp.tanh(wl).astype(BF16), w2_ref[...])
    lw = -math.exp(-0.5) * _sigmoid(z)
    a = _sigmoid(a0 + _dot(al.astype(BF16), a2_ref[...]))
    g = _dot(_sigmoid(gl).astype(BF16), g2_ref[...])
    kk = k * k_k
    kkn = kk / jnp.maximum(jnp.sqrt(head_sum(kk * kk)), 1e-12)
    k2 = k * (1.0 + (a - 1.0) * k_a)
    av = -kkn
    bv = kkn * a
    bonus = head_sum(r * k2 * r_k) * v

    eye = (ri == ci).astype(F32)
    bf = lambda t: t.astype(BF16)

    pre = []
    for c in range(n_chunk):
        rows = slice(c * C, (c + 1) * C)
        lwc = lw[rows]
        cw = _dot_exact_lhs(tri, lwc)
        cwl = cw[C - 1:C, :]
        e_in = jnp.exp(cw)
        e_neg = jnp.exp(-cw)
        e_rem = jnp.exp(cwl - cw)
        wc = jnp.exp(cwl)
        Rt = r[rows] * e_in
        At = av[rows] * jnp.exp(cw - lwc)
        Bb = bv[rows] * e_neg
        Kb = k2[rows] * e_neg
        Bh = bv[rows] * e_rem
        Kh = k2[rows] * e_rem
        vc = v[rows]
        for p in range(n_pair):
            sl = slice(p * LANES, (p + 1) * LANES)
            pre.append(dict(At=At[:, sl], Rt=Rt[:, sl], Bb=Bb[:, sl], Kb=Kb[:, sl], Bh=Bh[:, sl], Kh=Kh[:, sl],
                            v=vc[:, sl], wc=wc[:, sl]))

    for u in pre:
        u["at_bd"] = stack2(u["At"])
        lhs = bf(jnp.concatenate([u["at_bd"], stack2(u["Rt"])], axis=0))
        rhs = bf(jnp.concatenate([stack2(u["Bb"]), stack2(u["Kb"])], axis=0))
        u["G"] = _dot_nt(lhs, rhs)
    for u in pre:
        G = u.pop("G")
        u["a_ab"] = jnp.where(strict, G[0:2 * C, 0:2 * C], 0.0)
        u["a_ak"] = bf(jnp.where(strict, G[0:2 * C, 2 * C:4 * C], 0.0))
        u["a_rb"] = bf(jnp.where(incl, G[2 * C:4 * C, 0:2 * C], 0.0))
        u["a_rk"] = bf(jnp.where(incl, G[2 * C:4 * C, 2 * C:4 * C], 0.0))
        u["v_bd"] = bf(stack2(u["v"]))
    for u in pre:
        xb = bf(u["a_ab"])
        u["P"] = eye + u.pop("a_ab")
        u["X"] = _dot(xb, xb)
        u["M0"] = _dot(u["a_ak"], u["v_bd"])
    for _ in range(int(math.log2(C)) - 2):
        for u in pre:
            xb = bf(u["X"])
            Wm = _dot(xb, jnp.concatenate([bf(u["P"]), xb], axis=1))
            u["P"] = u["P"] + Wm[:, 0:LANES]
            u["X"] = Wm[:, LANES:2 * LANES]
    for u in pre:
        u["P"] = bf(u["P"] + _dot(bf(u.pop("X")), bf(u["P"])))
    for u in pre:
        u["M1"] = _dot(u["P"], bf(u.pop("M0")))
        u["Q"] = bf(_dot(u["a_rb"], u["P"]))
        u["PtB"] = _dot_tn(u["P"], bf(stack2(u["Bh"])))
    for u in pre:
        M1 = u.pop("M1")
        u["Y0"] = _dot(jnp.concatenate([u["a_rb"], u["a_rk"]], axis=1),
                       jnp.concatenate([bf(M1), u["v_bd"]], axis=0))
        u["Tm"] = bf(_dot_tn(bf(u["at_bd"]), bf(u.pop("PtB"))))
        m1_pair = M1[0:C] + M1[C:2 * C]
        cst = _dot_tn(bf(jnp.concatenate([m1_pair, u["v"]], axis=0)),
                      bf(jnp.concatenate([u["Bh"], u["Kh"]], axis=0)))
        u["cst"] = jnp.where(same, cst, 0.0)
        u["ar"] = bf(jnp.concatenate([u["At"], u["Rt"]], axis=0))

    states = [state_ref[p] for p in range(n_pair)]
    for c in range(n_chunk):
        rows = slice(c * C, (c + 1) * C)
        us = pre[c * n_pair:(c + 1) * n_pair]
        sbs = [bf(S) for S in states]
        zs = [_dot_nt(u["ar"], sb) for u, sb in zip(us, sbs)]
        new_states = [S * u["wc"] + _dot(sb, u["Tm"]) + u["cst"] for u, S, sb in zip(us, states, sbs)]
        ybds = [stack2(Z[C:2 * C]) + _dot(u["Q"], bf(stack2(Z[0:C]))) + u["Y0"] for u, Z in zip(us, zs)]
        ys = [y_bd[0:C] + y_bd[C:2 * C] for y_bd in ybds]
        states = new_states
        y = jnp.concatenate(ys, axis=1)
        mu = head_sum(y) * (1.0 / HEAD_DIM)
        yc = y - mu
        var = head_sum(yc * yc) * (1.0 / HEAD_DIM)
        yn = yc * lax.rsqrt(var + RWKV_GN_EPS) * ln_w + ln_b
        o_ref[rows, :] = ((yn + bonus[rows]) * g[rows]).astype(BF16)
    for p in range(n_pair):
        state_ref[p] = states[p]


def _rwkv(rw, vecs, w2, a2, g2, B, S, lb):
    nt = S // lb
    return pl.pallas_call(
        functools.partial(_rwkv_kernel, n_chunk=lb // RWKV_CHUNK),
        grid=(B, nt),
        in_specs=[pl.BlockSpec((lb, RWKV_PROJ), lambda b, j: (b * nt + j, 0)),
                  pl.BlockSpec((8, RWKV_WIDTH), lambda b, j: (0, 0)),
                  pl.BlockSpec((LANES, RWKV_WIDTH), lambda b, j: (0, 0)),
                  pl.BlockSpec((LANES, RWKV_WIDTH), lambda b, j: (0, 0)),
                  pl.BlockSpec((LANES, RWKV_WIDTH), lambda b, j: (0, 0))],
        out_specs=pl.BlockSpec((lb, RWKV_WIDTH), lambda b, j: (b * nt + j, 0)),
        out_shape=jax.ShapeDtypeStruct((B * S, RWKV_WIDTH), BF16),
        scratch_shapes=[pltpu.VMEM((RWKV_WIDTH // LANES, LANES, LANES), F32)],
        compiler_params=pltpu.CompilerParams(dimension_semantics=("arbitrary", "arbitrary"),
                                             vmem_limit_bytes=VMEM_LIMIT),
        name="rwkv",
    )(rw, vecs, w2, a2, g2)


def _mix_kernel(at_ref, rk_ref, x_ref, mod_ref, wo_a_ref, wo_r_ref, ln_ref, wr_ref, br_ref,
                x1_ref, h2_ref, ti_ref, gt_ref, rank_ref, cnt_ref, tb_ref, base_ref):
    first = (pl.program_id(0) == 0) & (pl.program_id(1) == 0)

    @pl.when(first)
    def _():
        base_ref[...] = jnp.zeros_like(base_ref)

    mod = mod_ref[0]
    y = _dot(at_ref[...], wo_a_ref[...]) + _dot(rk_ref[...], wo_r_ref[...])
    x = x_ref[...]
    tm = x.shape[0]
    x1 = _layer_norm(DEEPNORM_ALPHA * x + (1.0 + mod[2:3, :]) * y) * ln_ref[0:1, :] + ln_ref[1:2, :]
    h2 = _layer_norm(x1) * (1.0 + mod[4:5, :]) + mod[3:4, :]
    x1_ref[...] = x1
    h_hi = h2.astype(BF16)
    h2_ref[...] = h_hi

    h_lo = (h2 - h_hi.astype(F32)).astype(BF16)
    part = _dot(h_hi, wr_ref[...])
    logits = part[:, 0:LANES] + part[:, LANES:2 * LANES] + _dot(h_lo, wr_ref[:, 0:LANES]) + br_ref[...]
    lt = jnp.transpose(logits)[0:N_EXPERTS, :]
    erow = lax.broadcasted_iota(jnp.int32, (N_EXPERTS, tm), 0).astype(F32)
    cur = lt
    vals, idxs = [], []
    for _ in range(TOP_K):
        m = jnp.max(cur, axis=0, keepdims=True)
        idx = jnp.min(jnp.where(cur == m, erow, float(N_EXPERTS)), axis=0, keepdims=True)
        vals.append(m)
        idxs.append(idx)
        cur = jnp.where(erow == idx, -jnp.inf, cur)
    tv = jnp.concatenate(vals, axis=0)
    e = jnp.exp(tv - tv[0:1, :])
    gt_ref[...] = e / jnp.sum(e, axis=0, keepdims=True)
    ti_ref[...] = jnp.concatenate(idxs, axis=0).astype(jnp.int32)

    onehot = jnp.zeros((N_EXPERTS, tm), F32)
    for idx in idxs:
        onehot = onehot + (erow == idx).astype(F32)
    before = (lax.broadcasted_iota(jnp.int32, (tm, tm), 0)
              < lax.broadcasted_iota(jnp.int32, (tm, tm), 1)).astype(BF16)
    tot = base_ref[:, 0:1] + _dot(onehot.astype(BF16), before)
    ranks = [jnp.sum(jnp.where(erow == idx, tot, 0.0), axis=0, keepdims=True) for idx in idxs]
    rank_ref[...] = jnp.concatenate(ranks, axis=0).astype(jnp.int32)
    tb_ref[0] = base_ref[...].astype(jnp.int32)
    base_ref[...] = base_ref[...] + jnp.sum(onehot, axis=1, keepdims=True)
    cnt_ref[...] = base_ref[...].astype(jnp.int32)


def _mix(attn_out, rwkv_out, x2d, mod, wo_a, wo_r, ln1, w_router, b_router, B, S, tm):
    D = x2d.shape[1]
    nt = S // tm
    T = B * S
    tok = lambda b, j: (b * nt + j, 0)
    col = lambda b, j: (0, b * nt + j)
    fixed = lambda b, j: (0, 0)
    return pl.pallas_call(
        _mix_kernel,
        grid=(B, nt),
        in_specs=[pl.BlockSpec((tm, ATTN_WIDTH), tok),
                  pl.BlockSpec((tm, RWKV_WIDTH), tok),
                  pl.BlockSpec((tm, D), tok),
                  pl.BlockSpec((1, 6, D), lambda b, j: (b, 0, 0)),
                  pl.BlockSpec((ATTN_WIDTH, D), fixed),
                  pl.BlockSpec((RWKV_WIDTH, D), fixed),
                  pl.BlockSpec((2, D), fixed),
                  pl.BlockSpec((D, 2 * LANES), fixed),
                  pl.BlockSpec((1, LANES), fixed)],
        out_specs=[pl.BlockSpec((tm, D), tok),
                   pl.BlockSpec((tm, D), tok),
                   pl.BlockSpec((TOP_K, tm), col),
                   pl.BlockSpec((TOP_K, tm), col),
                   pl.BlockSpec((TOP_K, tm), col),
                   pl.BlockSpec((N_EXPERTS, LANES), fixed),
                   pl.BlockSpec((1, N_EXPERTS, LANES), lambda b, j: (b * nt + j, 0, 0))],
        out_shape=[jax.ShapeDtypeStruct((T, D), F32),
                   jax.ShapeDtypeStruct((T, D), BF16),
                   jax.ShapeDtypeStruct((TOP_K, T), jnp.int32),
                   jax.ShapeDtypeStruct((TOP_K, T), F32),
                   jax.ShapeDtypeStruct((TOP_K, T), jnp.int32),
                   jax.ShapeDtypeStruct((N_EXPERTS, LANES), jnp.int32),
                   jax.ShapeDtypeStruct((T // tm, N_EXPERTS, LANES), jnp.int32)],
        scratch_shapes=[pltpu.VMEM((N_EXPERTS, LANES), F32)],
        compiler_params=pltpu.CompilerParams(dimension_semantics=("arbitrary", "arbitrary"),
                                             vmem_limit_bytes=VMEM_LIMIT),
        name="mix",
    )(attn_out, rwkv_out, x2d, mod, wo_a, wo_r, ln1, w_router, b_router)


RUN_PIECES = tuple(2 ** b for b in range(int(math.log2(MOE_TILE)), -1, -1))
SUBLANES = 8


def _to_tiles(ref, x):
    n = x.shape[0]
    for c in range(SUBLANES):
        ref[pl.ds(c, n, stride=SUBLANES), :] = x[:, c * LANES:(c + 1) * LANES]


def _from_tiles(ref):
    n = ref.shape[0] // SUBLANES
    return jnp.concatenate([ref[pl.ds(c, n, stride=SUBLANES), :] for c in range(SUBLANES)], axis=1)


def _run_copies(n, local, local_start, remote, remote_start, sem, to_remote):
    off = 0
    for piece in RUN_PIECES:
        take = (n & piece) != 0

        @pl.when(take)
        def _(off=off, piece=piece):
            lo = pl.multiple_of((local_start + off) * SUBLANES, SUBLANES)
            ro = pl.multiple_of((remote_start + off) * SUBLANES, SUBLANES)
            loc = local.at[pl.ds(lo, piece * SUBLANES)]
            rem = remote.at[pl.ds(ro, piece * SUBLANES)]
            src, dst = (loc, rem) if to_remote else (rem, loc)
            pltpu.make_async_copy(src, dst, sem).start()

        off = off + (n & piece)


def _dispatch_kernel(tcnt_ref, lstart_ref, gstart_ref, pad_ref, pad_start_ref, n_used_ref, lpos_ref, h2_ref,
                     xs_ref, xbuf, zbuf, sems):
    i = pl.program_id(0)
    tm = h2_ref.shape[0]
    n_loc = TOP_K * tm
    n_blocks = xs_ref.shape[0] // (MOE_BLOCK * SUBLANES)
    zero_sem = sems.at[2]

    @pl.when(i == 0)
    def _():
        zbuf[...] = jnp.zeros_like(zbuf)

        def zero_pad(e, carry):
            _run_copies(pad_ref[e], zbuf, 0, xs_ref, pad_start_ref[e], zero_sem, True)
            return carry

        def zero_tail(b, carry):
            @pl.when(b >= n_used_ref[0])
            def _():
                start = pl.multiple_of(b * (MOE_BLOCK * SUBLANES), MOE_BLOCK * SUBLANES)
                pltpu.make_async_copy(zbuf, xs_ref.at[pl.ds(start, MOE_BLOCK * SUBLANES)], zero_sem).start()
            return carry

        lax.fori_loop(0, N_EXPERTS, zero_pad, 0)
        lax.fori_loop(n_blocks - N_EXPERTS, n_blocks, zero_tail, 0)

    slot = lax.broadcasted_iota(jnp.int32, (n_loc, tm), 0)
    lpos = lpos_ref[...]
    perm = jnp.zeros((n_loc, tm), F32)
    for k in range(TOP_K):
        perm = jnp.where(slot == lpos[k:k + 1, :], 1.0, perm)
    perm = perm.astype(BF16)

    def wait_tile(s):
        pltpu.make_async_copy(xbuf.at[s], xs_ref.at[pl.ds(0, n_loc * SUBLANES)], sems.at[s]).wait()

    cur = i % 2

    @pl.when(i >= 2)
    def _():
        wait_tile(cur)

    _to_tiles(xbuf.at[cur], _dot(perm, h2_ref[...]))

    def issue(e, carry):
        idx = i * N_EXPERTS + e
        _run_copies(tcnt_ref[idx], xbuf.at[cur], lstart_ref[idx], xs_ref, gstart_ref[idx], sems.at[cur], True)
        return carry

    lax.fori_loop(0, N_EXPERTS, issue, 0)

    @pl.when(i == pl.num_programs(0) - 1)
    def _():
        wait_tile(cur)

        @pl.when(i >= 1)
        def _():
            wait_tile(1 - cur)

        n_zero = N_EXPERTS * MOE_BLOCK * SUBLANES
        pltpu.make_async_copy(xs_ref.at[pl.ds(0, n_zero)], xs_ref.at[pl.ds(0, n_zero)], zero_sem).wait()


def _dispatch(tcnt, lstart, gstart, pad, pad_start, n_used, lpos, h2, n_rows, tm):
    T, D = h2.shape
    grid_spec = pltpu.PrefetchScalarGridSpec(
        num_scalar_prefetch=6,
        grid=(T // tm,),
        in_specs=[pl.BlockSpec((TOP_K, tm), lambda i, *_: (0, i)),
                  pl.BlockSpec((tm, D), lambda i, *_: (i, 0))],
        out_specs=pl.BlockSpec(memory_space=pl.ANY),
        scratch_shapes=[pltpu.VMEM((2, TOP_K * tm * SUBLANES, LANES), F32),
                        pltpu.VMEM((MOE_BLOCK * SUBLANES, LANES), F32),
                        pltpu.SemaphoreType.DMA((3,))],
    )
    return pl.pallas_call(
        _dispatch_kernel,
        grid_spec=grid_spec,
        out_shape=jax.ShapeDtypeStruct((n_rows * SUBLANES, LANES), F32),
        compiler_params=pltpu.CompilerParams(dimension_semantics=("arbitrary",),
                                             vmem_limit_bytes=VMEM_LIMIT),
        name="dispatch",
    )(tcnt, lstart, gstart, pad, pad_start, n_used, lpos, h2)


def _experts_kernel(blk_e_ref, n_used_ref, next_e_ref, xs_ref, wgu_hbm, bgu_ref, wd_hbm, bd_ref, ys_ref,
                    wgu_f32, wd_f32, wgu_bf, wd_bf, sems):
    i = pl.program_id(0)
    d_ff = wd_bf.shape[0]
    used = i < n_used_ref[0]
    e = blk_e_ref[i]
    new_expert = (i == 0) | (e != blk_e_ref[jnp.maximum(i - 1, 0)])

    def weight_copies(ex):
        return (pltpu.make_async_copy(wgu_hbm.at[ex], wgu_f32, sems.at[0]),
                pltpu.make_async_copy(wd_hbm.at[ex], wd_f32, sems.at[1]))

    @pl.when(i == 0)
    def _():
        for cp in weight_copies(e):
            cp.start()

    @pl.when(used & new_expert)
    def _():
        for cp in weight_copies(e):
            cp.wait()
        wgu_bf[...] = wgu_f32[...].astype(BF16)
        wd_bf[...] = wd_f32[...].astype(BF16)
        nxt = next_e_ref[i]

        @pl.when(nxt >= 0)
        def _():
            for cp in weight_copies(nxt):
                cp.start()

    @pl.when(used)
    def _():
        xb = _from_tiles(xs_ref).astype(BF16)
        gu = _dot(xb, wgu_bf[...]) + bgu_ref[0]
        gate = jnp.minimum(gu[:, :d_ff], SWIGLU_LIMIT)
        up = jnp.clip(gu[:, d_ff:], -SWIGLU_LIMIT, SWIGLU_LIMIT)
        act = (up + 1.0) * (gate * _sigmoid(SWIGLU_ALPHA * gate))
        _to_tiles(ys_ref, _dot(act.astype(BF16), wd_bf[...]) + bd_ref[0])

    @pl.when(i >= n_used_ref[0])
    def _():
        ys_ref[...] = jnp.zeros_like(ys_ref)


def _experts(blk_e, n_used, next_e, xs, wgu, bgu, wd, bd):
    d_ff, D = wd.shape[1], wd.shape[2]
    n_blocks = xs.shape[0] // (MOE_BLOCK * SUBLANES)
    blk = (MOE_BLOCK * SUBLANES, LANES)

    def last_used(i, n_used_ref):
        return jnp.minimum(i, jnp.maximum(n_used_ref[0] - 1, 0))

    def row_map(i, blk_e_ref, n_used_ref, next_e_ref):
        return (last_used(i, n_used_ref), 0)

    def exp_map(i, blk_e_ref, n_used_ref, next_e_ref):
        return (blk_e_ref[last_used(i, n_used_ref)], 0, 0)

    grid_spec = pltpu.PrefetchScalarGridSpec(
        num_scalar_prefetch=3,
        grid=(n_blocks,),
        in_specs=[pl.BlockSpec(blk, row_map),
                  pl.BlockSpec(memory_space=pl.ANY),
                  pl.BlockSpec((1, 1, 2 * d_ff), exp_map),
                  pl.BlockSpec(memory_space=pl.ANY),
                  pl.BlockSpec((1, 1, D), exp_map)],
        out_specs=pl.BlockSpec(blk, lambda i, *_: (i, 0)),
        scratch_shapes=[pltpu.VMEM((D, 2 * d_ff), F32), pltpu.VMEM((d_ff, D), F32),
                        pltpu.VMEM((D, 2 * d_ff), BF16), pltpu.VMEM((d_ff, D), BF16),
                        pltpu.SemaphoreType.DMA((2,))],
    )
    return pl.pallas_call(
        _experts_kernel,
        grid_spec=grid_spec,
        out_shape=jax.ShapeDtypeStruct(xs.shape, F32),
        compiler_params=pltpu.CompilerParams(dimension_semantics=("arbitrary",),
                                             vmem_limit_bytes=VMEM_LIMIT),
        name="experts",
    )(blk_e, n_used, next_e, xs, wgu, bgu, wd, bd)


def _combine_kernel(tcnt_ref, lstart_ref, gstart_ref, ys_ref, lpos_ref, gt_ref, x1_ref, mod_ref, ln_ref,
                    o_ref, buf, sems):
    nt = pl.num_programs(1)
    n_tiles = pl.num_programs(0) * nt
    i = pl.program_id(0) * nt + pl.program_id(1)
    tm = x1_ref.shape[0]
    n_loc = TOP_K * tm

    def fetch(tile, s):
        def issue(e, carry):
            idx = tile * N_EXPERTS + e
            _run_copies(tcnt_ref[idx], buf.at[s], lstart_ref[idx], ys_ref, gstart_ref[idx], sems.at[s], False)
            return carry

        lax.fori_loop(0, N_EXPERTS, issue, 0)

    @pl.when(i == 0)
    def _():
        fetch(i, 0)

    @pl.when(i + 1 < n_tiles)
    def _():
        fetch(i + 1, (i + 1) % 2)

    slot = lax.broadcasted_iota(jnp.int32, (tm, n_loc), 1)
    lpos = lpos_ref[...]
    gt = gt_ref[...]
    pick = jnp.zeros((tm, n_loc), F32)
    for k in range(TOP_K):
        pick = jnp.where(slot == lpos[:, k:k + 1], gt[:, k:k + 1], pick)
    cur = i % 2
    pltpu.make_async_copy(ys_ref.at[pl.ds(0, n_loc * SUBLANES)], buf.at[cur], sems.at[cur]).wait()
    y = _dot(pick.astype(BF16), _from_tiles(buf.at[cur]).astype(BF16))
    z = DEEPNORM_ALPHA * x1_ref[...] + (1.0 + mod_ref[0][5:6, :]) * y
    o_ref[...] = _layer_norm(z) * ln_ref[0:1, :] + ln_ref[1:2, :]


def _combine(tcnt, lstart, gstart, ys, lpos_t, gates_t, x1, mod, ln2, B, S, tm):
    T, D = x1.shape
    nt = S // tm
    tok = lambda b, j, *_: (b * nt + j, 0)
    grid_spec = pltpu.PrefetchScalarGridSpec(
        num_scalar_prefetch=3,
        grid=(B, nt),
        in_specs=[pl.BlockSpec(memory_space=pl.ANY),
                  pl.BlockSpec((tm, TOP_K), tok),
                  pl.BlockSpec((tm, TOP_K), tok),
                  pl.BlockSpec((tm, D), tok),
                  pl.BlockSpec((1, 6, D), lambda b, j, *_: (b, 0, 0)),
                  pl.BlockSpec((2, D), lambda b, j, *_: (0, 0))],
        out_specs=pl.BlockSpec((tm, D), tok),
        scratch_shapes=[pltpu.VMEM((2, TOP_K * tm * SUBLANES, LANES), F32),
                        pltpu.SemaphoreType.DMA((2,))],
    )
    return pl.pallas_call(
        _combine_kernel,
        grid_spec=grid_spec,
        out_shape=jax.ShapeDtypeStruct((T, D), F32),
        compiler_params=pltpu.CompilerParams(dimension_semantics=("arbitrary", "arbitrary"),
                                             vmem_limit_bytes=VMEM_LIMIT),
        name="combine",
    )(tcnt, lstart, gstart, ys, lpos_t, gates_t, x1, mod, ln2)


def _pad_rows(w, rows):
    return jnp.pad(w, ((0, rows - w.shape[0]), (0, 0)))


def _pad_cols(w, cols):
    return jnp.pad(w, ((0, 0), (0, cols - w.shape[1])))


def _layer(x, c, positions, w_ada, b_ada, w_in, shift_mu, rwkv_w0, rwkv_w2, rwkv_a0, rwkv_a2, rwkv_g2,
           rwkv_k_k, rwkv_k_a, rwkv_r_k, rwkv_ln_w, rwkv_ln_b, attn_sinks, w_out, ln1_g, ln1_b,
           w_router, b_router, w_gate_up, b_gate_up, w_down, b_down, ln2_g, ln2_b):
    B, S, D = x.shape
    T = B * S

    q0, k0, v0 = 0, ATTN_WIDTH, ATTN_WIDTH + KV_WIDTH
    r0 = ATTN_WIDTH + 2 * KV_WIDTH
    heads = lambda base: [w_in[:, base + h * HEAD_DIM: base + (h + 1) * HEAD_DIM] for h in range(N_KV_HEADS)]
    dup = lambda hs: [w for w in hs for _ in range(2)]
    w_attn = jnp.concatenate([w_in[:, q0:q0 + ATTN_WIDTH]] + dup(heads(k0)) + dup(heads(v0)), axis=1).astype(BF16)
    lora0 = r0 + 3 * RWKV_WIDTH
    lora = (DECAY_LORA, AAA_LORA, GATE_LORA)
    pieces_w = [w_in[:, r0:lora0]]
    pieces_mu = [shift_mu[None, 0:3 * RWKV_WIDTH]]
    off = lora0
    for n in lora:
        pieces_w.append(_pad_cols(w_in[:, off:off + n], LANES))
        pieces_mu.append(_pad_cols(shift_mu[None, off - r0:off - r0 + n], LANES))
        off += n
    w_rwkv = jnp.concatenate(pieces_w, axis=1).astype(BF16)
    mu = jnp.concatenate(pieces_mu, axis=1)
    inv_freq = ROPE_THETA ** (-jnp.arange(0, ROT_DIM, 2, dtype=F32) / ROT_DIM)
    lane_p = jnp.arange(LANES) % HEAD_DIM
    n_freq = ROT_DIM // 2
    rot_tab = jnp.zeros((8, LANES), F32)
    rot_tab = rot_tab.at[0].set(jnp.where(lane_p < ROT_DIM, 0.0, 1.0))
    rot_tab = rot_tab.at[1].set(jnp.where(lane_p < n_freq, -1.0, 0.0))
    rot_tab = rot_tab.at[2].set(jnp.where((lane_p >= n_freq) & (lane_p < ROT_DIM), 1.0, 0.0))
    freq_tab = jnp.broadcast_to(inv_freq[:, None], (n_freq, LANES))
    lane_freq = (jnp.arange(n_freq)[:, None] == (lane_p % n_freq)[None, :]) & (lane_p < ROT_DIM)[None, :]
    zeros = jnp.zeros_like(lane_freq)
    expand = jnp.concatenate([jnp.concatenate([lane_freq, zeros], axis=1),
                              jnp.concatenate([zeros, lane_freq], axis=1)], axis=0).astype(BF16)
    vecs = jnp.stack([rwkv_w0, rwkv_a0, rwkv_k_k, rwkv_k_a, rwkv_r_k.reshape(-1), rwkv_ln_w, rwkv_ln_b,
                      jnp.zeros_like(rwkv_w0)])
    w2 = _pad_rows(rwkv_w2, LANES).astype(BF16)
    a2 = _pad_rows(rwkv_a2, LANES).astype(BF16)
    g2 = _pad_rows(rwkv_g2, LANES).astype(BF16)
    wo_a = w_out[:ATTN_WIDTH].astype(BF16)
    wo_r = w_out[ATTN_WIDTH:].astype(BF16)
    w_r_hi = w_router.astype(BF16)
    w_r_lo = (w_router - w_r_hi.astype(F32)).astype(BF16)
    w_r = jnp.concatenate([_pad_cols(w_r_hi, LANES), _pad_cols(w_r_lo, LANES)], axis=1)
    b_r = jnp.concatenate([b_router, jnp.full((LANES - N_EXPERTS,), NEG_INF, F32)])[None, :]

    mod = _mod(c, w_ada, b_ada).reshape(B, 6, D)
    attn_out, rw = _inproj(x, positions, mod, w_attn, w_rwkv, mu, rot_tab, freq_tab, expand, attn_sinks,
                           min(INPROJ_TILE, S))
    rwkv_out = _rwkv(rw, vecs, w2, a2, g2, B, S, min(RWKV_STEP, S))

    mtile = min(MOE_TILE, S)
    x1, h2, top_i, gates, rank, cnt, tbase = _mix(attn_out, rwkv_out, x.reshape(T, D), mod, wo_a, wo_r,
                                                   jnp.stack([ln1_g, ln1_b]), w_r, b_r, B, S, mtile)

    counts = cnt[:, 0]
    padded = (counts + MOE_BLOCK - 1) // MOE_BLOCK * MOE_BLOCK
    pend = jnp.cumsum(padded)
    pstart = pend - padded
    n_blocks = T * TOP_K // MOE_BLOCK + N_EXPERTS
    blk_row = jnp.arange(n_blocks, dtype=jnp.int32) * MOE_BLOCK
    blk_e = jnp.minimum(jnp.sum((pend[None, :] <= blk_row[:, None]).astype(jnp.int32), axis=1), N_EXPERTS - 1)
    n_used = (pend[-1:] // MOE_BLOCK).astype(jnp.int32)
    tb = tbase[:, :, 0]
    tcnt = jnp.concatenate([tb[1:], counts[None]], axis=0) - tb
    lstart = jnp.cumsum(tcnt, axis=1) - tcnt
    gstart = pstart[None, :] + tb
    shift = jnp.repeat(jnp.transpose(lstart - tb), mtile, axis=1)
    experts = jnp.arange(N_EXPERTS, dtype=jnp.int32)
    lpos = rank + jnp.sum(jnp.where(top_i[None] == experts[:, None, None], shift[:, None, :], 0), axis=0)
    flat = lambda a: a.reshape(-1).astype(jnp.int32)

    xs = _dispatch(flat(tcnt), flat(lstart), flat(gstart), flat(padded - counts), flat(pstart + counts), n_used,
                   lpos, h2, n_blocks * MOE_BLOCK, mtile)
    later_with_rows = (experts[None, :] > experts[:, None]) & (counts[None, :] > 0)
    next_of = jnp.min(jnp.where(later_with_rows, experts[None, :], N_EXPERTS), axis=1)
    next_of = jnp.where(next_of < N_EXPERTS, next_of, -1)
    next_e = jnp.sum(jnp.where(blk_e[:, None] == experts[None, :], next_of[None, :], 0), axis=1).astype(jnp.int32)
    ys = _experts(blk_e, n_used, next_e, xs, w_gate_up, b_gate_up[:, None, :], w_down, b_down[:, None, :])
    out = _combine(flat(tcnt), flat(lstart), flat(gstart), ys, jnp.transpose(lpos), jnp.transpose(gates), x1, mod,
                   jnp.stack([ln2_g, ln2_b]), B, S, mtile)
    return out.reshape(B, S, D)


def kernel(x, c, positions, w_ada, b_ada, w_in, shift_mu, rwkv_w0, rwkv_w2, rwkv_a0, rwkv_a2, rwkv_g2,
           rwkv_k_k, rwkv_k_a, rwkv_r_k, rwkv_ln_w, rwkv_ln_b, attn_sinks, w_out, ln1_g, ln1_b,
           w_router, b_router, w_gate_up, b_gate_up, w_down, b_down, ln2_g, ln2_b):
    for l in range(DEPTH):
        x = _layer(x, c, positions, w_ada[l], b_ada[l], w_in[l], shift_mu[l], rwkv_w0[l], rwkv_w2[l],
                   rwkv_a0[l], rwkv_a2[l], rwkv_g2[l], rwkv_k_k[l], rwkv_k_a[l], rwkv_r_k[l], rwkv_ln_w[l],
                   rwkv_ln_b[l], attn_sinks[l], w_out[l], ln1_g[l], ln1_b[l], w_router[l], b_router[l],
                   w_gate_up[l], b_gate_up[l], w_down[l], b_down[l], ln2_g[l], ln2_b[l])
    return x
```

```python
import functools
import math

import jax
import jax.numpy as jnp
from jax import lax
from jax.experimental import pallas as pl
from jax.experimental.pallas import tpu as pltpu

F32 = jnp.float32
BF16 = jnp.bfloat16

HEAD_DIM = 64
N_ATTN_HEADS = 8
N_KV_HEADS = 2
N_RWKV_HEADS = 8
ATTN_WIDTH = N_ATTN_HEADS * HEAD_DIM
KV_WIDTH = N_KV_HEADS * HEAD_DIM
RWKV_WIDTH = N_RWKV_HEADS * HEAD_DIM
ATTN_BLOCK = 128
ROT_DIM = HEAD_DIM // 4
ROPE_THETA = 500000.0
DECAY_LORA = 32
AAA_LORA = 32
GATE_LORA = 96
N_EXPERTS = 32
TOP_K = 4
SWIGLU_LIMIT = 7.0
SWIGLU_ALPHA = 1.702
LN_EPS = 1e-5
RWKV_GN_EPS = 64e-5
NEG_INF = -1e30
DEPTH = 1
DEEPNORM_ALPHA = (2 * DEPTH) ** 0.25

LANES = 128
RWKV_CHUNK = 64
RWKV_STEP = 512
INPROJ_TILE = 512
RWKV_PROJ_CHUNK = 512
MOE_BLOCK = 512
MOE_TILE = 256
MIX_TILES = 4
ATTN_PROJ = ATTN_WIDTH + 4 * KV_WIDTH
RWKV_PROJ = 3 * RWKV_WIDTH + 3 * LANES
VMEM_LIMIT = 48 * 1024 * 1024


def _dot(a, b):
    return jnp.dot(a, b, preferred_element_type=F32)


def _dot_nt(a, b):
    return lax.dot_general(a, b, (((1,), (1,)), ((), ())), preferred_element_type=F32)


def _dot_tn(a, b):
    return lax.dot_general(a, b, (((0,), (0,)), ((), ())), preferred_element_type=F32)


def _split3(x):
    h = x.astype(BF16)
    r1 = x - h.astype(F32)
    m = r1.astype(BF16)
    lo = (r1 - m.astype(F32)).astype(BF16)
    return h, m, lo


def _dot_exact_lhs(m_bf16, x):
    h, m, lo = _split3(x)
    return _dot(m_bf16, h) + _dot(m_bf16, m) + _dot(m_bf16, lo)


def _layer_norm(x):
    mu = jnp.mean(x, axis=-1, keepdims=True)
    xc = x - mu
    var = jnp.mean(xc * xc, axis=-1, keepdims=True)
    return xc * lax.rsqrt(var + LN_EPS)


def _sigmoid(x):
    return 1.0 / (1.0 + jnp.exp(-x))


def _mod_kernel(c_ref, w_ref, b_ref, o_ref):
    c = c_ref[...]
    s = c * _sigmoid(c)
    o_ref[...] = jnp.dot(s, w_ref[...], preferred_element_type=F32,
                         precision=lax.Precision.HIGHEST) + b_ref[...]


def _mod(c, w_ada, b_ada):
    B, D = c.shape
    n = w_ada.shape[1] // D
    return pl.pallas_call(
        _mod_kernel,
        grid=(n,),
        in_specs=[pl.BlockSpec((B, D), lambda i: (0, 0)),
                  pl.BlockSpec((D, D), lambda i: (0, i)),
                  pl.BlockSpec((1, D), lambda i: (0, i))],
        out_specs=pl.BlockSpec((B, D), lambda i: (0, i)),
        out_shape=jax.ShapeDtypeStruct((B, n * D), F32),
        compiler_params=pltpu.CompilerParams(dimension_semantics=("arbitrary",),
                                             vmem_limit_bytes=VMEM_LIMIT),
        name="mod",
    )(c, w_ada, b_ada.reshape(1, -1))


def _inproj_kernel(x_ref, pos_ref, mod_ref, wa_ref, wr_ref, mu_ref, rt_ref, fq_ref, ex_ref, sink_ref,
                   at_ref, rw_ref, qkv_ref, kv_prev_ref, carry_ref):
    j = pl.program_id(1)

    @pl.when(j == 0)
    def _():
        kv_prev_ref[...] = jnp.zeros_like(kv_prev_ref)

    x = x_ref[0]
    tm = x.shape[0]
    mod = mod_ref[0]
    h = _layer_norm(x) * (1.0 + mod[1:2, :]) + mod[0:1, :]
    hb = h.astype(BF16)

    pa = _dot(hb, wa_ref[...])
    ang = pos_ref[0].astype(F32) * fq_ref[:, 0:1]
    pieces = _split3(jnp.concatenate([jnp.cos(ang), jnp.sin(ang)], axis=0))
    trig = _dot_tn(pieces[0], ex_ref[...]) + _dot_tn(pieces[1], ex_ref[...]) + _dot_tn(pieces[2], ex_ref[...])
    cs = trig[:, 0:LANES] + rt_ref[0:1, :]
    sn = trig[:, LANES:2 * LANES]
    m_lo = rt_ref[1:2, :]
    m_hi = rt_ref[2:3, :]
    n_q = ATTN_WIDTH // LANES
    n_rot = (ATTN_WIDTH + 2 * KV_WIDTH) // LANES
    for ch in range(n_rot):
        t = pa[:, ch * LANES:(ch + 1) * LANES]
        if ch < n_q:
            t = t * (1.0 / math.sqrt(HEAD_DIM))
        up = pltpu.roll(t, LANES - ROT_DIM // 2, 1)
        dn = pltpu.roll(t, ROT_DIM // 2, 1)
        o = t * cs + sn * (m_lo * up + m_hi * dn)
        qkv_ref[:, ch * LANES:(ch + 1) * LANES] = o.astype(BF16)
    qkv_ref[:, n_rot * LANES:] = pa[:, n_rot * LANES:].astype(BF16)

    row = lax.broadcasted_iota(jnp.int32, (tm, 1), 0)

    def rwkv_columns(c0, c1):
        def emit():
            pr = _dot(hb, wr_ref[:, c0:c1])
            prev = pltpu.roll(pr, 1, 0)
            carry = jnp.where(j == 0, 0.0, carry_ref[:, c0:c1])
            prev = jnp.where(row == 0, carry, prev)
            carry_ref[:, c0:c1] = pr[tm - 1:tm, :]
            rw_ref[:, c0:c1] = pr + (prev - pr) * mu_ref[:, c0:c1]
        return emit

    bounds = list(range(0, RWKV_PROJ, RWKV_PROJ_CHUNK)) + [RWKV_PROJ]
    kv_w = 2 * KV_WIDTH
    _attn_body(qkv_ref.at[:, pl.ds(0, ATTN_WIDTH)], qkv_ref.at[:, pl.ds(ATTN_WIDTH, kv_w)],
               kv_prev_ref.at[:, pl.ds(0, kv_w)], qkv_ref.at[:, pl.ds(ATTN_WIDTH + kv_w, kv_w)],
               kv_prev_ref.at[:, pl.ds(kv_w, kv_w)], sink_ref, at_ref, j > 0,
               interleave=[rwkv_columns(c0, c1) for c0, c1 in zip(bounds[:-1], bounds[1:])])
    kv_prev_ref[...] = qkv_ref[tm - ATTN_BLOCK:tm, ATTN_WIDTH:]


def _inproj(x, positions, mod, w_attn, w_rwkv, mu, rot_tab, freq_tab, expand, sinks, tm):
    B, S, D = x.shape
    nt = S // tm
    return pl.pallas_call(
        _inproj_kernel,
        grid=(B, nt),
        in_specs=[pl.BlockSpec((1, tm, D), lambda b, j: (b, j, 0)),
                  pl.BlockSpec((1, 1, tm), lambda b, j: (b, 0, j)),
                  pl.BlockSpec((1, 6, D), lambda b, j: (b, 0, 0)),
                  pl.BlockSpec((D, ATTN_PROJ), lambda b, j: (0, 0)),
                  pl.BlockSpec((D, RWKV_PROJ), lambda b, j: (0, 0)),
                  pl.BlockSpec((1, RWKV_PROJ), lambda b, j: (0, 0)),
                  pl.BlockSpec((8, LANES), lambda b, j: (0, 0)),
                  pl.BlockSpec(freq_tab.shape, lambda b, j: (0, 0)),
                  pl.BlockSpec(expand.shape, lambda b, j: (0, 0)),
                  pl.BlockSpec(memory_space=pltpu.SMEM)],
        out_specs=[pl.BlockSpec((tm, ATTN_WIDTH), lambda b, j: (b * nt + j, 0)),
                   pl.BlockSpec((tm, RWKV_PROJ), lambda b, j: (b * nt + j, 0))],
        out_shape=[jax.ShapeDtypeStruct((B * S, ATTN_WIDTH), BF16),
                   jax.ShapeDtypeStruct((B * S, RWKV_PROJ), F32)],
        scratch_shapes=[pltpu.VMEM((tm, ATTN_PROJ), BF16),
                        pltpu.VMEM((ATTN_BLOCK, ATTN_PROJ - ATTN_WIDTH), BF16),
                        pltpu.VMEM((1, RWKV_PROJ), F32)],
        compiler_params=pltpu.CompilerParams(dimension_semantics=("arbitrary", "arbitrary"),
                                             vmem_limit_bytes=VMEM_LIMIT),
        name="inproj",
    )(x, positions.reshape(B, 1, S), mod, w_attn, w_rwkv, mu, rot_tab, freq_tab, expand, sinks)


def _attn_body(q_ref, kc_ref, kp_ref, vc_ref, vp_ref, sink_ref, o_ref, has_prev, interleave=()):
    blk = ATTN_BLOCK
    n_sub = q_ref.shape[0] // blk
    qi = lax.broadcasted_iota(jnp.int32, (blk, 2 * blk), 0)
    kj = lax.broadcasted_iota(jnp.int32, (blk, 2 * blk), 1)
    band = (kj > qi) & (kj <= qi + blk)
    first = band & ((kj >= blk) | has_prev)
    lane = lax.broadcasted_iota(jnp.int32, (1, LANES), 1)
    lo = (lane < HEAD_DIM).astype(BF16)
    hi = (lane >= HEAD_DIM).astype(BF16)
    halves = {}
    for u in range(n_sub):
        for g in range(N_KV_HEADS):
            sl = slice(g * LANES, (g + 1) * LANES)
            prev_k = kp_ref[:, sl] if u == 0 else kc_ref[(u - 1) * blk:u * blk, sl]
            prev_v = vp_ref[:, sl] if u == 0 else vc_ref[(u - 1) * blk:u * blk, sl]
            kcat = jnp.concatenate([prev_k, kc_ref[u * blk:(u + 1) * blk, sl]], axis=0)
            vcat = jnp.concatenate([prev_v, vc_ref[u * blk:(u + 1) * blk, sl]], axis=0)
            halves[u, g] = ((kcat * lo, vcat * lo), (kcat * hi, vcat * hi))
    units = [(u, c, half) for u in range(n_sub) for c in range(ATTN_WIDTH // LANES) for half in range(2)]
    scores = [_dot_nt(q_ref[u * blk:(u + 1) * blk, c * LANES:(c + 1) * LANES], halves[u, c // 2][half][0])
              for u, c, half in units]
    probs, denoms = [], []
    every = max(1, len(units) // max(1, len(interleave)))
    for n, ((u, c, half), s) in enumerate(zip(units, scores)):
        if n % every == 0 and n // every < len(interleave):
            interleave[n // every]()
        sink = sink_ref[2 * c + half]
        s = jnp.where(first if u == 0 else band, s, NEG_INF)
        m = jnp.maximum(jnp.max(s, axis=-1, keepdims=True), sink)
        p = jnp.exp(s - m)
        denoms.append(jnp.sum(p, axis=-1, keepdims=True) + jnp.exp(sink - m))
        probs.append(p.astype(BF16))
    outs = [_dot(p, halves[u, c // 2][half][1]) / d for (u, c, half), p, d in zip(units, probs, denoms)]
    for n, (u, c, half) in enumerate(units):
        if half == 0:
            o_ref[u * blk:(u + 1) * blk, c * LANES:(c + 1) * LANES] = (outs[n] + outs[n + 1]).astype(BF16)


def _rwkv_kernel(rw_ref, vec_ref, w2_ref, a2_ref, g2_ref, o_ref, state_ref, *, n_chunk):
    j = pl.program_id(1)
    C = RWKV_CHUNK
    W = RWKV_WIDTH
    n_pair = W // LANES

    @pl.when(j == 0)
    def _():
        state_ref[...] = jnp.zeros_like(state_ref)

    w0 = vec_ref[0:1, :]
    a0 = vec_ref[1:2, :]
    k_k = vec_ref[2:3, :]
    k_a = vec_ref[3:4, :]
    r_k = vec_ref[4:5, :]
    ln_w = vec_ref[5:6, :]
    ln_b = vec_ref[6:7, :]

    r = rw_ref[:, 0:W]
    k = rw_ref[:, W:2 * W]
    v = rw_ref[:, 2 * W:3 * W]
    wl = rw_ref[:, 3 * W:3 * W + LANES]
    al = rw_ref[:, 3 * W + LANES:3 * W + 2 * LANES]
    gl = rw_ref[:, 3 * W + 2 * LANES:3 * W + 3 * LANES]

    ri = lax.broadcasted_iota(jnp.int32, (LANES, LANES), 0)
    ci = lax.broadcasted_iota(jnp.int32, (LANES, LANES), 1)
    same = (ri // HEAD_DIM) == (ci // HEAD_DIM)
    strict = same & ((ri % HEAD_DIM) > (ci % HEAD_DIM))
    incl = same & ((ri % HEAD_DIM) >= (ci % HEAD_DIM))
    lane = lax.broadcasted_iota(jnp.int32, (1, LANES), 1)
    m0 = (lane < HEAD_DIM).astype(F32)
    m1 = 1.0 - m0
    tri = (lax.broadcasted_iota(jnp.int32, (C, C), 0) >= lax.broadcasted_iota(jnp.int32, (C, C), 1)).astype(BF16)

    def head_sum(xv):
        outs = []
        for p in range(n_pair):
            xp = xv[:, p * LANES:(p + 1) * LANES]
            s0 = jnp.sum(xp * m0, axis=1, keepdims=True)
            s1 = jnp.sum(xp * m1, axis=1, keepdims=True)
            outs.append(s0 * m0 + s1 * m1)
        return jnp.concatenate(outs, axis=1)

    def stack2(xp):
        return jnp.concatenate([xp * m0, xp * m1], axis=0)

    z = w0 + _dot(jnp.tanh(wl).astype(BF16), w2_ref[...])
    lw = -math.exp(-0.5) * _sigmoid(z)
    a = _sigmoid(a0 + _dot(al.astype(BF16), a2_ref[...]))
    g = _dot(_sigmoid(gl).astype(BF16), g2_ref[...])
    kk = k * k_k
    kkn = kk / jnp.maximum(jnp.sqrt(head_sum(kk * kk)), 1e-12)
    k2 = k * (1.0 + (a - 1.0) * k_a)
    av = -kkn
    bv = kkn * a
    bonus = head_sum(r * k2 * r_k) * v

    eye = (ri == ci).astype(F32)
    bf = lambda t: t.astype(BF16)

    pre = []
    for c in range(n_chunk):
        rows = slice(c * C, (c + 1) * C)
        lwc = lw[rows]
        cw = _dot_exact_lhs(tri, lwc)
        cwl = cw[C - 1:C, :]
        e_in = jnp.exp(cw)
        e_neg = jnp.exp(-cw)
        e_rem = jnp.exp(cwl - cw)
        wc = jnp.exp(cwl)
        Rt = r[rows] * e_in
        At = av[rows] * jnp.exp(cw - lwc)
        Bb = bv[rows] * e_neg
        Kb = k2[rows] * e_neg
        Bh = bv[rows] * e_rem
        Kh = k2[rows] * e_rem
        vc = v[rows]
        for p in range(n_pair):
            sl = slice(p * LANES, (p + 1) * LANES)
            pre.append(dict(At=At[:, sl], Rt=Rt[:, sl], Bb=Bb[:, sl], Kb=Kb[:, sl], Bh=Bh[:, sl], Kh=Kh[:, sl],
                            v=vc[:, sl], wc=wc[:, sl]))

    for u in pre:
        u["at_bd"] = stack2(u["At"])
        lhs = bf(jnp.concatenate([u["at_bd"], stack2(u["Rt"])], axis=0))
        rhs = bf(jnp.concatenate([stack2(u["Bb"]), stack2(u["Kb"])], axis=0))
        u["G"] = _dot_nt(lhs, rhs)
    for u in pre:
        G = u.pop("G")
        u["a_ab"] = jnp.where(strict, G[0:2 * C, 0:2 * C], 0.0)
        u["a_ak"] = bf(jnp.where(strict, G[0:2 * C, 2 * C:4 * C], 0.0))
        u["a_rb"] = bf(jnp.where(incl, G[2 * C:4 * C, 0:2 * C], 0.0))
        u["a_rk"] = bf(jnp.where(incl, G[2 * C:4 * C, 2 * C:4 * C], 0.0))
        u["v_bd"] = bf(stack2(u["v"]))
    for u in pre:
        xb = bf(u["a_ab"])
        u["P"] = eye + u.pop("a_ab")
        u["X"] = _dot(xb, xb)
        u["M0"] = _dot(u["a_ak"], u["v_bd"])
    for _ in range(int(math.log2(C)) - 2):
        for u in pre:
            xb = bf(u["X"])
            Wm = _dot(xb, jnp.concatenate([bf(u["P"]), xb], axis=1))
            u["P"] = u["P"] + Wm[:, 0:LANES]
            u["X"] = Wm[:, LANES:2 * LANES]
    for u in pre:
        u["P"] = bf(u["P"] + _dot(bf(u.pop("X")), bf(u["P"])))
    for u in pre:
        u["M1"] = _dot(u["P"], bf(u.pop("M0")))
        u["Q"] = bf(_dot(u["a_rb"], u["P"]))
        u["PtB"] = _dot_tn(u["P"], bf(stack2(u["Bh"])))
    for u in pre:
        M1 = u.pop("M1")
        u["Y0"] = _dot(jnp.concatenate([u["a_rb"], u["a_rk"]], axis=1),
                       jnp.concatenate([bf(M1), u["v_bd"]], axis=0))
        u["Tm"] = bf(_dot_tn(bf(u["at_bd"]), bf(u.pop("PtB"))))
        m1_pair = M1[0:C] + M1[C:2 * C]
        cst = _dot_tn(bf(jnp.concatenate([m1_pair, u["v"]], axis=0)),
                      bf(jnp.concatenate([u["Bh"], u["Kh"]], axis=0)))
        u["cst"] = jnp.where(same, cst, 0.0)
        u["ar"] = bf(jnp.concatenate([u["At"], u["Rt"]], axis=0))

    states = [state_ref[p] for p in range(n_pair)]
    for c in range(n_chunk):
        rows = slice(c * C, (c + 1) * C)
        us = pre[c * n_pair:(c + 1) * n_pair]
        sbs = [bf(S) for S in states]
        zs = [_dot_nt(u["ar"], sb) for u, sb in zip(us, sbs)]
        new_states = [S * u["wc"] + _dot(sb, u["Tm"]) + u["cst"] for u, S, sb in zip(us, states, sbs)]
        ybds = [stack2(Z[C:2 * C]) + _dot(u["Q"], bf(stack2(Z[0:C]))) + u["Y0"] for u, Z in zip(us, zs)]
        ys = [y_bd[0:C] + y_bd[C:2 * C] for y_bd in ybds]
        states = new_states
        y = jnp.concatenate(ys, axis=1)
        mu = head_sum(y) * (1.0 / HEAD_DIM)
        yc = y - mu
        var = head_sum(yc * yc) * (1.0 / HEAD_DIM)
        yn = yc * lax.rsqrt(var + RWKV_GN_EPS) * ln_w + ln_b
        o_ref[rows, :] = ((yn + bonus[rows]) * g[rows]).astype(BF16)
    for p in range(n_pair):
        state_ref[p] = states[p]


def _rwkv(rw, vecs, w2, a2, g2, B, S, lb):
    nt = S // lb
    return pl.pallas_call(
        functools.partial(_rwkv_kernel, n_chunk=lb // RWKV_CHUNK),
        grid=(B, nt),
        in_specs=[pl.BlockSpec((lb, RWKV_PROJ), lambda b, j: (b * nt + j, 0)),
                  pl.BlockSpec((8, RWKV_WIDTH), lambda b, j: (0, 0)),
                  pl.BlockSpec((LANES, RWKV_WIDTH), lambda b, j: (0, 0)),
                  pl.BlockSpec((LANES, RWKV_WIDTH), lambda b, j: (0, 0)),
                  pl.BlockSpec((LANES, RWKV_WIDTH), lambda b, j: (0, 0))],
        out_specs=pl.BlockSpec((lb, RWKV_WIDTH), lambda b, j: (b * nt + j, 0)),
        out_shape=jax.ShapeDtypeStruct((B * S, RWKV_WIDTH), BF16),
        scratch_shapes=[pltpu.VMEM((RWKV_WIDTH // LANES, LANES, LANES), F32)],
        compiler_params=pltpu.CompilerParams(dimension_semantics=("arbitrary", "arbitrary"),
                                             vmem_limit_bytes=VMEM_LIMIT),
        name="rwkv",
    )(rw, vecs, w2, a2, g2)


def _mix_kernel(at_ref, rk_ref, x_ref, mod_ref, wo_a_ref, wo_r_ref, ln_ref, wr_ref, br_ref,
                x1_ref, h2_ref, ti_ref, gt_ref, rank_ref, cnt_ref, tb_ref, base_ref):
    first = (pl.program_id(0) == 0) & (pl.program_id(1) == 0)

    @pl.when(first)
    def _():
        base_ref[...] = jnp.zeros_like(base_ref)

    mod = mod_ref[0]
    n_grp = tb_ref.shape[0]
    mt = x_ref.shape[0] // n_grp
    groups = [slice(g * mt, (g + 1) * mt) for g in range(n_grp)]
    ys = [_dot(at_ref[g, :], wo_a_ref[...]) + _dot(rk_ref[g, :], wo_r_ref[...]) for g in groups]
    logits = []
    for g, y in zip(groups, ys):
        x1 = _layer_norm(DEEPNORM_ALPHA * x_ref[g, :] + (1.0 + mod[2:3, :]) * y) * ln_ref[0:1, :] + ln_ref[1:2, :]
        h2 = _layer_norm(x1) * (1.0 + mod[4:5, :]) + mod[3:4, :]
        x1_ref[g, :] = x1
        h_hi = h2.astype(BF16)
        h2_ref[g, :] = h_hi
        h_lo = (h2 - h_hi.astype(F32)).astype(BF16)
        part = _dot(h_hi, wr_ref[...])
        logits.append(part[:, 0:LANES] + part[:, LANES:2 * LANES] + _dot(h_lo, wr_ref[:, 0:LANES]) + br_ref[...])

    erow = lax.broadcasted_iota(jnp.int32, (N_EXPERTS, mt), 0).astype(F32)
    before = (lax.broadcasted_iota(jnp.int32, (mt, mt), 0)
              < lax.broadcasted_iota(jnp.int32, (mt, mt), 1)).astype(BF16)
    base = base_ref[...]
    for n, (g, lg) in enumerate(zip(groups, logits)):
        cur = jnp.transpose(lg)[0:N_EXPERTS, :]
        vals, idxs = [], []
        for _ in range(TOP_K):
            m = jnp.max(cur, axis=0, keepdims=True)
            idx = jnp.min(jnp.where(cur == m, erow, float(N_EXPERTS)), axis=0, keepdims=True)
            vals.append(m)
            idxs.append(idx)
            cur = jnp.where(erow == idx, -jnp.inf, cur)
        tv = jnp.concatenate(vals, axis=0)
        e = jnp.exp(tv - tv[0:1, :])
        gt_ref[:, g] = e / jnp.sum(e, axis=0, keepdims=True)
        ti_ref[:, g] = jnp.concatenate(idxs, axis=0).astype(jnp.int32)

        onehot = jnp.zeros((N_EXPERTS, mt), F32)
        for idx in idxs:
            onehot = onehot + (erow == idx).astype(F32)
        tot = base[:, 0:1] + _dot(onehot.astype(BF16), before)
        ranks = [jnp.sum(jnp.where(erow == idx, tot, 0.0), axis=0, keepdims=True) for idx in idxs]
        rank_ref[:, g] = jnp.concatenate(ranks, axis=0).astype(jnp.int32)
        tb_ref[n] = base.astype(jnp.int32)
        base = base + jnp.sum(onehot, axis=1, keepdims=True)
    base_ref[...] = base
    cnt_ref[...] = base.astype(jnp.int32)


def _mix(attn_out, rwkv_out, x2d, mod, wo_a, wo_r, ln1, w_router, b_router, B, S, mtile, n_grp):
    D = x2d.shape[1]
    tm = mtile * n_grp
    nt = S // tm
    T = B * S
    tok = lambda b, j: (b * nt + j, 0)
    col = lambda b, j: (0, b * nt + j)
    fixed = lambda b, j: (0, 0)
    return pl.pallas_call(
        _mix_kernel,
        grid=(B, nt),
        in_specs=[pl.BlockSpec((tm, ATTN_WIDTH), tok),
                  pl.BlockSpec((tm, RWKV_WIDTH), tok),
                  pl.BlockSpec((tm, D), tok),
                  pl.BlockSpec((1, 6, D), lambda b, j: (b, 0, 0)),
                  pl.BlockSpec((ATTN_WIDTH, D), fixed),
                  pl.BlockSpec((RWKV_WIDTH, D), fixed),
                  pl.BlockSpec((2, D), fixed),
                  pl.BlockSpec((D, 2 * LANES), fixed),
                  pl.BlockSpec((1, LANES), fixed)],
        out_specs=[pl.BlockSpec((tm, D), tok),
                   pl.BlockSpec((tm, D), tok),
                   pl.BlockSpec((TOP_K, tm), col),
                   pl.BlockSpec((TOP_K, tm), col),
                   pl.BlockSpec((TOP_K, tm), col),
                   pl.BlockSpec((N_EXPERTS, LANES), fixed),
                   pl.BlockSpec((n_grp, N_EXPERTS, LANES), lambda b, j: (b * nt + j, 0, 0))],
        out_shape=[jax.ShapeDtypeStruct((T, D), F32),
                   jax.ShapeDtypeStruct((T, D), BF16),
                   jax.ShapeDtypeStruct((TOP_K, T), jnp.int32),
                   jax.ShapeDtypeStruct((TOP_K, T), F32),
                   jax.ShapeDtypeStruct((TOP_K, T), jnp.int32),
                   jax.ShapeDtypeStruct((N_EXPERTS, LANES), jnp.int32),
                   jax.ShapeDtypeStruct((T // mtile, N_EXPERTS, LANES), jnp.int32)],
        scratch_shapes=[pltpu.VMEM((N_EXPERTS, LANES), F32)],
        compiler_params=pltpu.CompilerParams(dimension_semantics=("arbitrary", "arbitrary"),
                                             vmem_limit_bytes=VMEM_LIMIT),
        name="mix",
    )(attn_out, rwkv_out, x2d, mod, wo_a, wo_r, ln1, w_router, b_router)


RUN_PIECES = tuple(2 ** b for b in range(int(math.log2(MOE_TILE)), -1, -1))
SUBLANES = 8


def _to_tiles(ref, x):
    n = x.shape[0]
    for c in range(SUBLANES):
        ref[pl.ds(c, n, stride=SUBLANES), :] = x[:, c * LANES:(c + 1) * LANES]


def _from_tiles(ref):
    n = ref.shape[0] // SUBLANES
    return jnp.concatenate([ref[pl.ds(c, n, stride=SUBLANES), :] for c in range(SUBLANES)], axis=1)


def _run_copies(n, local, local_start, remote, remote_start, sem, to_remote):
    off = 0
    for piece in RUN_PIECES:
        take = (n & piece) != 0

        @pl.when(take)
        def _(off=off, piece=piece):
            lo = pl.multiple_of((local_start + off) * SUBLANES, SUBLANES)
            ro = pl.multiple_of((remote_start + off) * SUBLANES, SUBLANES)
            loc = local.at[pl.ds(lo, piece * SUBLANES)]
            rem = remote.at[pl.ds(ro, piece * SUBLANES)]
            src, dst = (loc, rem) if to_remote else (rem, loc)
            pltpu.make_async_copy(src, dst, sem).start()

        off = off + (n & piece)


def _dispatch_kernel(tcnt_ref, lstart_ref, gstart_ref, pad_ref, pad_start_ref, n_used_ref, lpos_ref, h2_ref,
                     xs_ref, xbuf, zbuf, sems):
    i = pl.program_id(0)
    tm = h2_ref.shape[0]
    n_loc = TOP_K * tm
    n_blocks = xs_ref.shape[0] // (MOE_BLOCK * SUBLANES)
    zero_sem = sems.at[2]

    @pl.when(i == 0)
    def _():
        zbuf[...] = jnp.zeros_like(zbuf)

        def zero_pad(e, carry):
            _run_copies(pad_ref[e], zbuf, 0, xs_ref, pad_start_ref[e], zero_sem, True)
            return carry

        def zero_tail(b, carry):
            @pl.when(b >= n_used_ref[0])
            def _():
                start = pl.multiple_of(b * (MOE_BLOCK * SUBLANES), MOE_BLOCK * SUBLANES)
                pltpu.make_async_copy(zbuf, xs_ref.at[pl.ds(start, MOE_BLOCK * SUBLANES)], zero_sem).start()
            return carry

        lax.fori_loop(0, N_EXPERTS, zero_pad, 0)
        lax.fori_loop(n_blocks - N_EXPERTS, n_blocks, zero_tail, 0)

    slot = lax.broadcasted_iota(jnp.int32, (n_loc, tm), 0)
    lpos = lpos_ref[...]
    perm = jnp.zeros((n_loc, tm), F32)
    for k in range(TOP_K):
        perm = jnp.where(slot == lpos[k:k + 1, :], 1.0, perm)
    perm = perm.astype(BF16)

    def wait_tile(s):
        pltpu.make_async_copy(xbuf.at[s], xs_ref.at[pl.ds(0, n_loc * SUBLANES)], sems.at[s]).wait()

    cur = i % 2

    @pl.when(i >= 2)
    def _():
        wait_tile(cur)

    _to_tiles(xbuf.at[cur], _dot(perm, h2_ref[...]))

    def issue(e, carry):
        idx = i * N_EXPERTS + e
        _run_copies(tcnt_ref[idx], xbuf.at[cur], lstart_ref[idx], xs_ref, gstart_ref[idx], sems.at[cur], True)
        return carry

    lax.fori_loop(0, N_EXPERTS, issue, 0)

    @pl.when(i == pl.num_programs(0) - 1)
    def _():
        wait_tile(cur)

        @pl.when(i >= 1)
        def _():
            wait_tile(1 - cur)

        n_zero = N_EXPERTS * MOE_BLOCK * SUBLANES
        pltpu.make_async_copy(xs_ref.at[pl.ds(0, n_zero)], xs_ref.at[pl.ds(0, n_zero)], zero_sem).wait()


def _dispatch(tcnt, lstart, gstart, pad, pad_start, n_used, lpos, h2, n_rows, tm):
    T, D = h2.shape
    grid_spec = pltpu.PrefetchScalarGridSpec(
        num_scalar_prefetch=6,
        grid=(T // tm,),
        in_specs=[pl.BlockSpec((TOP_K, tm), lambda i, *_: (0, i)),
                  pl.BlockSpec((tm, D), lambda i, *_: (i, 0))],
        out_specs=pl.BlockSpec(memory_space=pl.ANY),
        scratch_shapes=[pltpu.VMEM((2, TOP_K * tm * SUBLANES, LANES), F32),
                        pltpu.VMEM((MOE_BLOCK * SUBLANES, LANES), F32),
                        pltpu.SemaphoreType.DMA((3,))],
    )
    return pl.pallas_call(
        _dispatch_kernel,
        grid_spec=grid_spec,
        out_shape=jax.ShapeDtypeStruct((n_rows * SUBLANES, LANES), F32),
        compiler_params=pltpu.CompilerParams(dimension_semantics=("arbitrary",),
                                             vmem_limit_bytes=VMEM_LIMIT),
        name="dispatch",
    )(tcnt, lstart, gstart, pad, pad_start, n_used, lpos, h2)


def _experts_kernel(blk_e_ref, n_used_ref, next_e_ref, xs_ref, wgu_hbm, bgu_ref, wd_hbm, bd_ref, ys_ref,
                    wgu_f32, wd_f32, wgu_bf, wd_bf, sems):
    i = pl.program_id(0)
    d_ff = wd_bf.shape[0]
    used = i < n_used_ref[0]
    e = blk_e_ref[i]
    new_expert = (i == 0) | (e != blk_e_ref[jnp.maximum(i - 1, 0)])

    def weight_copies(ex):
        return (pltpu.make_async_copy(wgu_hbm.at[ex], wgu_f32, sems.at[0]),
                pltpu.make_async_copy(wd_hbm.at[ex], wd_f32, sems.at[1]))

    @pl.when(i == 0)
    def _():
        for cp in weight_copies(e):
            cp.start()

    @pl.when(used & new_expert)
    def _():
        for cp in weight_copies(e):
            cp.wait()
        wgu_bf[...] = wgu_f32[...].astype(BF16)
        wd_bf[...] = wd_f32[...].astype(BF16)
        nxt = next_e_ref[i]

        @pl.when(nxt >= 0)
        def _():
            for cp in weight_copies(nxt):
                cp.start()

    @pl.when(used)
    def _():
        xb = _from_tiles(xs_ref).astype(BF16)
        gu = _dot(xb, wgu_bf[...]) + bgu_ref[0]
        gate = jnp.minimum(gu[:, :d_ff], SWIGLU_LIMIT)
        up = jnp.clip(gu[:, d_ff:], -SWIGLU_LIMIT, SWIGLU_LIMIT)
        act = (up + 1.0) * (gate * _sigmoid(SWIGLU_ALPHA * gate))
        _to_tiles(ys_ref, _dot(act.astype(BF16), wd_bf[...]) + bd_ref[0])

    @pl.when(i >= n_used_ref[0])
    def _():
        ys_ref[...] = jnp.zeros_like(ys_ref)


def _experts(blk_e, n_used, next_e, xs, wgu, bgu, wd, bd):
    d_ff, D = wd.shape[1], wd.shape[2]
    n_blocks = xs.shape[0] // (MOE_BLOCK * SUBLANES)
    blk = (MOE_BLOCK * SUBLANES, LANES)

    def last_used(i, n_used_ref):
        return jnp.minimum(i, jnp.maximum(n_used_ref[0] - 1, 0))

    def row_map(i, blk_e_ref, n_used_ref, next_e_ref):
        return (last_used(i, n_used_ref), 0)

    def exp_map(i, blk_e_ref, n_used_ref, next_e_ref):
        return (blk_e_ref[last_used(i, n_used_ref)], 0, 0)

    grid_spec = pltpu.PrefetchScalarGridSpec(
        num_scalar_prefetch=3,
        grid=(n_blocks,),
        in_specs=[pl.BlockSpec(blk, row_map),
                  pl.BlockSpec(memory_space=pl.ANY),
                  pl.BlockSpec((1, 1, 2 * d_ff), exp_map),
                  pl.BlockSpec(memory_space=pl.ANY),
                  pl.BlockSpec((1, 1, D), exp_map)],
        out_specs=pl.BlockSpec(blk, lambda i, *_: (i, 0)),
        scratch_shapes=[pltpu.VMEM((D, 2 * d_ff), F32), pltpu.VMEM((d_ff, D), F32),
                        pltpu.VMEM((D, 2 * d_ff), BF16), pltpu.VMEM((d_ff, D), BF16),
                        pltpu.SemaphoreType.DMA((2,))],
    )
    return pl.pallas_call(
        _experts_kernel,
        grid_spec=grid_spec,
        out_shape=jax.ShapeDtypeStruct(xs.shape, F32),
        compiler_params=pltpu.CompilerParams(dimension_semantics=("arbitrary",),
                                             vmem_limit_bytes=VMEM_LIMIT),
        name="experts",
    )(blk_e, n_used, next_e, xs, wgu, bgu, wd, bd)


def _combine_kernel(tcnt_ref, lstart_ref, gstart_ref, ys_ref, lpos_ref, gt_ref, x1_ref, mod_ref, ln_ref,
                    o_ref, buf, sems):
    nt = pl.num_programs(1)
    n_tiles = pl.num_programs(0) * nt
    i = pl.program_id(0) * nt + pl.program_id(1)
    tm = x1_ref.shape[0]
    n_loc = TOP_K * tm

    def fetch(tile, s):
        def issue(e, carry):
            idx = tile * N_EXPERTS + e
            _run_copies(tcnt_ref[idx], buf.at[s], lstart_ref[idx], ys_ref, gstart_ref[idx], sems.at[s], False)
            return carry

        lax.fori_loop(0, N_EXPERTS, issue, 0)

    @pl.when(i == 0)
    def _():
        fetch(i, 0)

    @pl.when(i + 1 < n_tiles)
    def _():
        fetch(i + 1, (i + 1) % 2)

    slot = lax.broadcasted_iota(jnp.int32, (tm, n_loc), 1)
    lpos = lpos_ref[...]
    gt = gt_ref[...]
    pick = jnp.zeros((tm, n_loc), F32)
    for k in range(TOP_K):
        pick = jnp.where(slot == lpos[:, k:k + 1], gt[:, k:k + 1], pick)
    cur = i % 2
    pltpu.make_async_copy(ys_ref.at[pl.ds(0, n_loc * SUBLANES)], buf.at[cur], sems.at[cur]).wait()
    y = _dot(pick.astype(BF16), _from_tiles(buf.at[cur]).astype(BF16))
    z = DEEPNORM_ALPHA * x1_ref[...] + (1.0 + mod_ref[0][5:6, :]) * y
    o_ref[...] = _layer_norm(z) * ln_ref[0:1, :] + ln_ref[1:2, :]


def _combine(tcnt, lstart, gstart, ys, lpos_t, gates_t, x1, mod, ln2, B, S, tm):
    T, D = x1.shape
    nt = S // tm
    tok = lambda b, j, *_: (b * nt + j, 0)
    grid_spec = pltpu.PrefetchScalarGridSpec(
        num_scalar_prefetch=3,
        grid=(B, nt),
        in_specs=[pl.BlockSpec(memory_space=pl.ANY),
                  pl.BlockSpec((tm, TOP_K), tok),
                  pl.BlockSpec((tm, TOP_K), tok),
                  pl.BlockSpec((tm, D), tok),
                  pl.BlockSpec((1, 6, D), lambda b, j, *_: (b, 0, 0)),
                  pl.BlockSpec((2, D), lambda b, j, *_: (0, 0))],
        out_specs=pl.BlockSpec((tm, D), tok),
        scratch_shapes=[pltpu.VMEM((2, TOP_K * tm * SUBLANES, LANES), F32),
                        pltpu.SemaphoreType.DMA((2,))],
    )
    return pl.pallas_call(
        _combine_kernel,
        grid_spec=grid_spec,
        out_shape=jax.ShapeDtypeStruct((T, D), F32),
        compiler_params=pltpu.CompilerParams(dimension_semantics=("arbitrary", "arbitrary"),
                                             vmem_limit_bytes=VMEM_LIMIT),
        name="combine",
    )(tcnt, lstart, gstart, ys, lpos_t, gates_t, x1, mod, ln2)


def _pad_rows(w, rows):
    return jnp.pad(w, ((0, rows - w.shape[0]), (0, 0)))


def _pad_cols(w, cols):
    return jnp.pad(w, ((0, 0), (0, cols - w.shape[1])))


def _layer(x, c, positions, w_ada, b_ada, w_in, shift_mu, rwkv_w0, rwkv_w2, rwkv_a0, rwkv_a2, rwkv_g2,
           rwkv_k_k, rwkv_k_a, rwkv_r_k, rwkv_ln_w, rwkv_ln_b, attn_sinks, w_out, ln1_g, ln1_b,
           w_router, b_router, w_gate_up, b_gate_up, w_down, b_down, ln2_g, ln2_b):
    B, S, D = x.shape
    T = B * S

    q0, k0, v0 = 0, ATTN_WIDTH, ATTN_WIDTH + KV_WIDTH
    r0 = ATTN_WIDTH + 2 * KV_WIDTH
    heads = lambda base: [w_in[:, base + h * HEAD_DIM: base + (h + 1) * HEAD_DIM] for h in range(N_KV_HEADS)]
    dup = lambda hs: [w for w in hs for _ in range(2)]
    w_attn = jnp.concatenate([w_in[:, q0:q0 + ATTN_WIDTH]] + dup(heads(k0)) + dup(heads(v0)), axis=1).astype(BF16)
    lora0 = r0 + 3 * RWKV_WIDTH
    lora = (DECAY_LORA, AAA_LORA, GATE_LORA)
    pieces_w = [w_in[:, r0:lora0]]
    pieces_mu = [shift_mu[None, 0:3 * RWKV_WIDTH]]
    off = lora0
    for n in lora:
        pieces_w.append(_pad_cols(w_in[:, off:off + n], LANES))
        pieces_mu.append(_pad_cols(shift_mu[None, off - r0:off - r0 + n], LANES))
        off += n
    w_rwkv = jnp.concatenate(pieces_w, axis=1).astype(BF16)
    mu = jnp.concatenate(pieces_mu, axis=1)
    inv_freq = ROPE_THETA ** (-jnp.arange(0, ROT_DIM, 2, dtype=F32) / ROT_DIM)
    lane_p = jnp.arange(LANES) % HEAD_DIM
    n_freq = ROT_DIM // 2
    rot_tab = jnp.zeros((8, LANES), F32)
    rot_tab = rot_tab.at[0].set(jnp.where(lane_p < ROT_DIM, 0.0, 1.0))
    rot_tab = rot_tab.at[1].set(jnp.where(lane_p < n_freq, -1.0, 0.0))
    rot_tab = rot_tab.at[2].set(jnp.where((lane_p >= n_freq) & (lane_p < ROT_DIM), 1.0, 0.0))
    freq_tab = jnp.broadcast_to(inv_freq[:, None], (n_freq, LANES))
    lane_freq = (jnp.arange(n_freq)[:, None] == (lane_p % n_freq)[None, :]) & (lane_p < ROT_DIM)[None, :]
    zeros = jnp.zeros_like(lane_freq)
    expand = jnp.concatenate([jnp.concatenate([lane_freq, zeros], axis=1),
                              jnp.concatenate([zeros, lane_freq], axis=1)], axis=0).astype(BF16)
    vecs = jnp.stack([rwkv_w0, rwkv_a0, rwkv_k_k, rwkv_k_a, rwkv_r_k.reshape(-1), rwkv_ln_w, rwkv_ln_b,
                      jnp.zeros_like(rwkv_w0)])
    w2 = _pad_rows(rwkv_w2, LANES).astype(BF16)
    a2 = _pad_rows(rwkv_a2, LANES).astype(BF16)
    g2 = _pad_rows(rwkv_g2, LANES).astype(BF16)
    wo_a = w_out[:ATTN_WIDTH].astype(BF16)
    wo_r = w_out[ATTN_WIDTH:].astype(BF16)
    w_r_hi = w_router.astype(BF16)
    w_r_lo = (w_router - w_r_hi.astype(F32)).astype(BF16)
    w_r = jnp.concatenate([_pad_cols(w_r_hi, LANES), _pad_cols(w_r_lo, LANES)], axis=1)
    b_r = jnp.concatenate([b_router, jnp.full((LANES - N_EXPERTS,), NEG_INF, F32)])[None, :]

    mod = _mod(c, w_ada, b_ada).reshape(B, 6, D)
    attn_out, rw = _inproj(x, positions, mod, w_attn, w_rwkv, mu, rot_tab, freq_tab, expand, attn_sinks,
                           min(INPROJ_TILE, S))
    rwkv_out = _rwkv(rw, vecs, w2, a2, g2, B, S, min(RWKV_STEP, S))

    mtile = min(MOE_TILE, S)
    x1, h2, top_i, gates, rank, cnt, tbase = _mix(attn_out, rwkv_out, x.reshape(T, D), mod, wo_a, wo_r,
                                                   jnp.stack([ln1_g, ln1_b]), w_r, b_r, B, S, mtile,
                                                   max(1, min(MIX_TILES, S // mtile)))

    counts = cnt[:, 0]
    padded = (counts + MOE_BLOCK - 1) // MOE_BLOCK * MOE_BLOCK
    pend = jnp.cumsum(padded)
    pstart = pend - padded
    n_blocks = T * TOP_K // MOE_BLOCK + N_EXPERTS
    blk_row = jnp.arange(n_blocks, dtype=jnp.int32) * MOE_BLOCK
    blk_e = jnp.minimum(jnp.sum((pend[None, :] <= blk_row[:, None]).astype(jnp.int32), axis=1), N_EXPERTS - 1)
    n_used = (pend[-1:] // MOE_BLOCK).astype(jnp.int32)
    tb = tbase[:, :, 0]
    tcnt = jnp.concatenate([tb[1:], counts[None]], axis=0) - tb
    lstart = jnp.cumsum(tcnt, axis=1) - tcnt
    gstart = pstart[None, :] + tb
    shift = jnp.repeat(jnp.transpose(lstart - tb), mtile, axis=1)
    experts = jnp.arange(N_EXPERTS, dtype=jnp.int32)
    lpos = rank + jnp.sum(jnp.where(top_i[None] == experts[:, None, None], shift[:, None, :], 0), axis=0)
    flat = lambda a: a.reshape(-1).astype(jnp.int32)

    xs = _dispatch(flat(tcnt), flat(lstart), flat(gstart), flat(padded - counts), flat(pstart + counts), n_used,
                   lpos, h2, n_blocks * MOE_BLOCK, mtile)
    later_with_rows = (experts[None, :] > experts[:, None]) & (counts[None, :] > 0)
    next_of = jnp.min(jnp.where(later_with_rows, experts[None, :], N_EXPERTS), axis=1)
    next_of = jnp.where(next_of < N_EXPERTS, next_of, -1)
    next_e = jnp.sum(jnp.where(blk_e[:, None] == experts[None, :], next_of[None, :], 0), axis=1).astype(jnp.int32)
    ys = _experts(blk_e, n_used, next_e, xs, w_gate_up, b_gate_up[:, None, :], w_down, b_down[:, None, :])
    out = _combine(flat(tcnt), flat(lstart), flat(gstart), ys, jnp.transpose(lpos), jnp.transpose(gates), x1, mod,
                   jnp.stack([ln2_g, ln2_b]), B, S, mtile)
    return out.reshape(B, S, D)


def kernel(x, c, positions, w_ada, b_ada, w_in, shift_mu, rwkv_w0, rwkv_w2, rwkv_a0, rwkv_a2, rwkv_g2,
           rwkv_k_k, rwkv_k_a, rwkv_r_k, rwkv_ln_w, rwkv_ln_b, attn_sinks, w_out, ln1_g, ln1_b,
           w_router, b_router, w_gate_up, b_gate_up, w_down, b_down, ln2_g, ln2_b):
    for l in range(DEPTH):
        x = _layer(x, c, positions, w_ada[l], b_ada[l], w_in[l], shift_mu[l], rwkv_w0[l], rwkv_w2[l],
                   rwkv_a0[l], rwkv_a2[l], rwkv_g2[l], rwkv_k_k[l], rwkv_k_a[l], rwkv_r_k[l], rwkv_ln_w[l],
                   rwkv_ln_b[l], attn_sinks[l], w_out[l], ln1_g[l], ln1_b[l], w_router[l], b_router[l],
                   w_gate_up[l], b_gate_up[l], w_down[l], b_down[l], ln2_g[l], ln2_b[l])
    return x
```

```python
import functools
import math

import jax
import jax.numpy as jnp
from jax import lax
from jax.experimental import pallas as pl
from jax.experimental.pallas import tpu as pltpu

F32 = jnp.float32
BF16 = jnp.bfloat16

HEAD_DIM = 64
N_ATTN_HEADS = 8
N_KV_HEADS = 2
N_RWKV_HEADS = 8
ATTN_WIDTH = N_ATTN_HEADS * HEAD_DIM
KV_WIDTH = N_KV_HEADS * HEAD_DIM
RWKV_WIDTH = N_RWKV_HEADS * HEAD_DIM
ATTN_BLOCK = 128
ROT_DIM = HEAD_DIM // 4
ROPE_THETA = 500000.0
DECAY_LORA = 32
AAA_LORA = 32
GATE_LORA = 96
N_EXPERTS = 32
TOP_K = 4
SWIGLU_LIMIT = 7.0
SWIGLU_ALPHA = 1.702
LN_EPS = 1e-5
RWKV_GN_EPS = 64e-5
NEG_INF = -1e30
DEPTH = 1
DEEPNORM_ALPHA = (2 * DEPTH) ** 0.25

LANES = 128
RWKV_CHUNK = 64
RWKV_STEP = 512
INPROJ_TILE = 512
RWKV_PROJ_CHUNK = 512
MOE_BLOCK = 512
MOE_TILE = 256
MIX_TILES = 4
COMBINE_TILES = 2
ATTN_PROJ = ATTN_WIDTH + 4 * KV_WIDTH
RWKV_PROJ = 3 * RWKV_WIDTH + 3 * LANES
VMEM_LIMIT = 48 * 1024 * 1024


def _dot(a, b):
    return jnp.dot(a, b, preferred_element_type=F32)


def _dot_nt(a, b):
    return lax.dot_general(a, b, (((1,), (1,)), ((), ())), preferred_element_type=F32)


def _dot_tn(a, b):
    return lax.dot_general(a, b, (((0,), (0,)), ((), ())), preferred_element_type=F32)


def _split3(x):
    h = x.astype(BF16)
    r1 = x - h.astype(F32)
    m = r1.astype(BF16)
    lo = (r1 - m.astype(F32)).astype(BF16)
    return h, m, lo


def _dot_exact_lhs(m_bf16, x):
    h, m, lo = _split3(x)
    return _dot(m_bf16, h) + _dot(m_bf16, m) + _dot(m_bf16, lo)


def _layer_norm(x):
    mu = jnp.mean(x, axis=-1, keepdims=True)
    xc = x - mu
    var = jnp.mean(xc * xc, axis=-1, keepdims=True)
    return xc * lax.rsqrt(var + LN_EPS)


def _sigmoid(x):
    return 1.0 / (1.0 + jnp.exp(-x))


def _mod_kernel(c_ref, w_ref, b_ref, o_ref):
    c = c_ref[...]
    s = c * _sigmoid(c)
    o_ref[...] = jnp.dot(s, w_ref[...], preferred_element_type=F32,
                         precision=lax.Precision.HIGHEST) + b_ref[...]


def _mod(c, w_ada, b_ada):
    B, D = c.shape
    n = w_ada.shape[1] // D
    return pl.pallas_call(
        _mod_kernel,
        grid=(n,),
        in_specs=[pl.BlockSpec((B, D), lambda i: (0, 0)),
                  pl.BlockSpec((D, D), lambda i: (0, i)),
                  pl.BlockSpec((1, D), lambda i: (0, i))],
        out_specs=pl.BlockSpec((B, D), lambda i: (0, i)),
        out_shape=jax.ShapeDtypeStruct((B, n * D), F32),
        compiler_params=pltpu.CompilerParams(dimension_semantics=("arbitrary",),
                                             vmem_limit_bytes=VMEM_LIMIT),
        name="mod",
    )(c, w_ada, b_ada.reshape(1, -1))


def _inproj_kernel(x_ref, pos_ref, mod_ref, wa_ref, wr_ref, mu_ref, rt_ref, fq_ref, ex_ref, sink_ref,
                   at_ref, rw_ref, qkv_ref, kv_prev_ref, carry_ref):
    j = pl.program_id(1)

    @pl.when(j == 0)
    def _():
        kv_prev_ref[...] = jnp.zeros_like(kv_prev_ref)

    x = x_ref[0]
    tm = x.shape[0]
    mod = mod_ref[0]
    h = _layer_norm(x) * (1.0 + mod[1:2, :]) + mod[0:1, :]
    hb = h.astype(BF16)

    pa = _dot(hb, wa_ref[...])
    ang = pos_ref[0].astype(F32) * fq_ref[:, 0:1]
    pieces = _split3(jnp.concatenate([jnp.cos(ang), jnp.sin(ang)], axis=0))
    trig = _dot_tn(pieces[0], ex_ref[...]) + _dot_tn(pieces[1], ex_ref[...]) + _dot_tn(pieces[2], ex_ref[...])
    cs = trig[:, 0:LANES] + rt_ref[0:1, :]
    sn = trig[:, LANES:2 * LANES]
    m_lo = rt_ref[1:2, :]
    m_hi = rt_ref[2:3, :]
    n_q = ATTN_WIDTH // LANES
    n_rot = (ATTN_WIDTH + 2 * KV_WIDTH) // LANES
    for ch in range(n_rot):
        t = pa[:, ch * LANES:(ch + 1) * LANES]
        if ch < n_q:
            t = t * (1.0 / math.sqrt(HEAD_DIM))
        up = pltpu.roll(t, LANES - ROT_DIM // 2, 1)
        dn = pltpu.roll(t, ROT_DIM // 2, 1)
        o = t * cs + sn * (m_lo * up + m_hi * dn)
        qkv_ref[:, ch * LANES:(ch + 1) * LANES] = o.astype(BF16)
    qkv_ref[:, n_rot * LANES:] = pa[:, n_rot * LANES:].astype(BF16)

    row = lax.broadcasted_iota(jnp.int32, (tm, 1), 0)

    def rwkv_columns(c0, c1):
        def emit():
            pr = _dot(hb, wr_ref[:, c0:c1])
            prev = pltpu.roll(pr, 1, 0)
            carry = jnp.where(j == 0, 0.0, carry_ref[:, c0:c1])
            prev = jnp.where(row == 0, carry, prev)
            carry_ref[:, c0:c1] = pr[tm - 1:tm, :]
            rw_ref[:, c0:c1] = pr + (prev - pr) * mu_ref[:, c0:c1]
        return emit

    bounds = list(range(0, RWKV_PROJ, RWKV_PROJ_CHUNK)) + [RWKV_PROJ]
    kv_w = 2 * KV_WIDTH
    _attn_body(qkv_ref.at[:, pl.ds(0, ATTN_WIDTH)], qkv_ref.at[:, pl.ds(ATTN_WIDTH, kv_w)],
               kv_prev_ref.at[:, pl.ds(0, kv_w)], qkv_ref.at[:, pl.ds(ATTN_WIDTH + kv_w, kv_w)],
               kv_prev_ref.at[:, pl.ds(kv_w, kv_w)], sink_ref, at_ref, j > 0,
               interleave=[rwkv_columns(c0, c1) for c0, c1 in zip(bounds[:-1], bounds[1:])])
    kv_prev_ref[...] = qkv_ref[tm - ATTN_BLOCK:tm, ATTN_WIDTH:]


def _inproj(x, positions, mod, w_attn, w_rwkv, mu, rot_tab, freq_tab, expand, sinks, tm):
    B, S, D = x.shape
    nt = S // tm
    return pl.pallas_call(
        _inproj_kernel,
        grid=(B, nt),
        in_specs=[pl.BlockSpec((1, tm, D), lambda b, j: (b, j, 0)),
                  pl.BlockSpec((1, 1, tm), lambda b, j: (b, 0, j)),
                  pl.BlockSpec((1, 6, D), lambda b, j: (b, 0, 0)),
                  pl.BlockSpec((D, ATTN_PROJ), lambda b, j: (0, 0)),
                  pl.BlockSpec((D, RWKV_PROJ), lambda b, j: (0, 0)),
                  pl.BlockSpec((1, RWKV_PROJ), lambda b, j: (0, 0)),
                  pl.BlockSpec((8, LANES), lambda b, j: (0, 0)),
                  pl.BlockSpec(freq_tab.shape, lambda b, j: (0, 0)),
                  pl.BlockSpec(expand.shape, lambda b, j: (0, 0)),
                  pl.BlockSpec(memory_space=pltpu.SMEM)],
        out_specs=[pl.BlockSpec((tm, ATTN_WIDTH), lambda b, j: (b * nt + j, 0)),
                   pl.BlockSpec((tm, RWKV_PROJ), lambda b, j: (b * nt + j, 0))],
        out_shape=[jax.ShapeDtypeStruct((B * S, ATTN_WIDTH), BF16),
                   jax.ShapeDtypeStruct((B * S, RWKV_PROJ), F32)],
        scratch_shapes=[pltpu.VMEM((tm, ATTN_PROJ), BF16),
                        pltpu.VMEM((ATTN_BLOCK, ATTN_PROJ - ATTN_WIDTH), BF16),
                        pltpu.VMEM((1, RWKV_PROJ), F32)],
        compiler_params=pltpu.CompilerParams(dimension_semantics=("arbitrary", "arbitrary"),
                                             vmem_limit_bytes=VMEM_LIMIT),
        name="inproj",
    )(x, positions.reshape(B, 1, S), mod, w_attn, w_rwkv, mu, rot_tab, freq_tab, expand, sinks)


def _attn_body(q_ref, kc_ref, kp_ref, vc_ref, vp_ref, sink_ref, o_ref, has_prev, interleave=()):
    blk = ATTN_BLOCK
    n_sub = q_ref.shape[0] // blk
    qi = lax.broadcasted_iota(jnp.int32, (blk, 2 * blk), 0)
    kj = lax.broadcasted_iota(jnp.int32, (blk, 2 * blk), 1)
    band = (kj > qi) & (kj <= qi + blk)
    first = band & ((kj >= blk) | has_prev)
    lane = lax.broadcasted_iota(jnp.int32, (1, LANES), 1)
    lo = (lane < HEAD_DIM).astype(BF16)
    hi = (lane >= HEAD_DIM).astype(BF16)
    halves = {}
    for u in range(n_sub):
        for g in range(N_KV_HEADS):
            sl = slice(g * LANES, (g + 1) * LANES)
            prev_k = kp_ref[:, sl] if u == 0 else kc_ref[(u - 1) * blk:u * blk, sl]
            prev_v = vp_ref[:, sl] if u == 0 else vc_ref[(u - 1) * blk:u * blk, sl]
            kcat = jnp.concatenate([prev_k, kc_ref[u * blk:(u + 1) * blk, sl]], axis=0)
            vcat = jnp.concatenate([prev_v, vc_ref[u * blk:(u + 1) * blk, sl]], axis=0)
            halves[u, g] = ((kcat * lo, vcat * lo), (kcat * hi, vcat * hi))
    units = [(u, c, half) for u in range(n_sub) for c in range(ATTN_WIDTH // LANES) for half in range(2)]
    scores = [_dot_nt(q_ref[u * blk:(u + 1) * blk, c * LANES:(c + 1) * LANES], halves[u, c // 2][half][0])
              for u, c, half in units]
    probs, denoms = [], []
    every = max(1, len(units) // max(1, len(interleave)))
    for n, ((u, c, half), s) in enumerate(zip(units, scores)):
        if n % every == 0 and n // every < len(interleave):
            interleave[n // every]()
        sink = sink_ref[2 * c + half]
        s = jnp.where(first if u == 0 else band, s, NEG_INF)
        m = jnp.maximum(jnp.max(s, axis=-1, keepdims=True), sink)
        p = jnp.exp(s - m)
        denoms.append(jnp.sum(p, axis=-1, keepdims=True) + jnp.exp(sink - m))
        probs.append(p.astype(BF16))
    outs = [_dot(p, halves[u, c // 2][half][1]) / d for (u, c, half), p, d in zip(units, probs, denoms)]
    for n, (u, c, half) in enumerate(units):
        if half == 0:
            o_ref[u * blk:(u + 1) * blk, c * LANES:(c + 1) * LANES] = (outs[n] + outs[n + 1]).astype(BF16)


def _rwkv_kernel(rw_ref, vec_ref, w2_ref, a2_ref, g2_ref, o_ref, state_ref, *, n_chunk):
    j = pl.program_id(1)
    C = RWKV_CHUNK
    W = RWKV_WIDTH
    n_pair = W // LANES

    @pl.when(j == 0)
    def _():
        state_ref[...] = jnp.zeros_like(state_ref)

    w0 = vec_ref[0:1, :]
    a0 = vec_ref[1:2, :]
    k_k = vec_ref[2:3, :]
    k_a = vec_ref[3:4, :]
    r_k = vec_ref[4:5, :]
    ln_w = vec_ref[5:6, :]
    ln_b = vec_ref[6:7, :]

    r = rw_ref[:, 0:W]
    k = rw_ref[:, W:2 * W]
    v = rw_ref[:, 2 * W:3 * W]
    wl = rw_ref[:, 3 * W:3 * W + LANES]
    al = rw_ref[:, 3 * W + LANES:3 * W + 2 * LANES]
    gl = rw_ref[:, 3 * W + 2 * LANES:3 * W + 3 * LANES]

    ri = lax.broadcasted_iota(jnp.int32, (LANES, LANES), 0)
    ci = lax.broadcasted_iota(jnp.int32, (LANES, LANES), 1)
    same = (ri // HEAD_DIM) == (ci // HEAD_DIM)
    strict = same & ((ri % HEAD_DIM) > (ci % HEAD_DIM))
    incl = same & ((ri % HEAD_DIM) >= (ci % HEAD_DIM))
    lane = lax.broadcasted_iota(jnp.int32, (1, LANES), 1)
    m0 = (lane < HEAD_DIM).astype(F32)
    m1 = 1.0 - m0
    tri = (lax.broadcasted_iota(jnp.int32, (C, C), 0) >= lax.broadcasted_iota(jnp.int32, (C, C), 1)).astype(BF16)

    def head_sum(xv):
        outs = []
        for p in range(n_pair):
            xp = xv[:, p * LANES:(p + 1) * LANES]
            s0 = jnp.sum(xp * m0, axis=1, keepdims=True)
            s1 = jnp.sum(xp * m1, axis=1, keepdims=True)
            outs.append(s0 * m0 + s1 * m1)
        return jnp.concatenate(outs, axis=1)

    def stack2(xp):
        return jnp.concatenate([xp * m0, xp * m1], axis=0)

    z = w0 + _dot(jnp.tanh(wl).astype(BF16), w2_ref[...])
    lw = -math.exp(-0.5) * _sigmoid(z)
    a = _sigmoid(a0 + _dot(al.astype(BF16), a2_ref[...]))
    g = _dot(_sigmoid(gl).astype(BF16), g2_ref[...])
    kk = k * k_k
    kkn = kk / jnp.maximum(jnp.sqrt(head_sum(kk * kk)), 1e-12)
    k2 = k * (1.0 + (a - 1.0) * k_a)
    av = -kkn
    bv = kkn * a
    bonus = head_sum(r * k2 * r_k) * v

    eye = (ri == ci).astype(F32)
    bf = lambda t: t.astype(BF16)

    pre = []
    for c in range(n_chunk):
        rows = slice(c * C, (c + 1) * C)
        lwc = lw[rows]
        cw = _dot_exact_lhs(tri, lwc)
        cwl = cw[C - 1:C, :]
        e_in = jnp.exp(cw)
        e_neg = jnp.exp(-cw)
        e_rem = jnp.exp(cwl - cw)
        wc = jnp.exp(cwl)
        Rt = r[rows] * e_in
        At = av[rows] * jnp.exp(cw - lwc)
        Bb = bv[rows] * e_neg
        Kb = k2[rows] * e_neg
        Bh = bv[rows] * e_rem
        Kh = k2[rows] * e_rem
        vc = v[rows]
        for p in range(n_pair):
            sl = slice(p * LANES, (p + 1) * LANES)
            pre.append(dict(At=At[:, sl], Rt=Rt[:, sl], Bb=Bb[:, sl], Kb=Kb[:, sl], Bh=Bh[:, sl], Kh=Kh[:, sl],
                            v=vc[:, sl], wc=wc[:, sl]))

    for u in pre:
        u["at_bd"] = stack2(u["At"])
        lhs = bf(jnp.concatenate([u["at_bd"], stack2(u["Rt"])], axis=0))
        rhs = bf(jnp.concatenate([stack2(u["Bb"]), stack2(u["Kb"])], axis=0))
        u["G"] = _dot_nt(lhs, rhs)
    for u in pre:
        G = u.pop("G")
        u["a_ab"] = jnp.where(strict, G[0:2 * C, 0:2 * C], 0.0)
        u["a_ak"] = bf(jnp.where(strict, G[0:2 * C, 2 * C:4 * C], 0.0))
        u["a_rb"] = bf(jnp.where(incl, G[2 * C:4 * C, 0:2 * C], 0.0))
        u["a_rk"] = bf(jnp.where(incl, G[2 * C:4 * C, 2 * C:4 * C], 0.0))
        u["v_bd"] = bf(stack2(u["v"]))
    for u in pre:
        xb = bf(u["a_ab"])
        u["P"] = eye + u.pop("a_ab")
        u["X"] = _dot(xb, xb)
        u["M0"] = _dot(u["a_ak"], u["v_bd"])
    for _ in range(int(math.log2(C)) - 2):
        for u in pre:
            xb = bf(u["X"])
            Wm = _dot(xb, jnp.concatenate([bf(u["P"]), xb], axis=1))
            u["P"] = u["P"] + Wm[:, 0:LANES]
            u["X"] = Wm[:, LANES:2 * LANES]
    for u in pre:
        u["P"] = bf(u["P"] + _dot(bf(u.pop("X")), bf(u["P"])))
    for u in pre:
        u["M1"] = _dot(u["P"], bf(u.pop("M0")))
        u["Q"] = bf(_dot(u["a_rb"], u["P"]))
        u["PtB"] = _dot_tn(u["P"], bf(stack2(u["Bh"])))
    for u in pre:
        M1 = u.pop("M1")
        u["Y0"] = _dot(jnp.concatenate([u["a_rb"], u["a_rk"]], axis=1),
                       jnp.concatenate([bf(M1), u["v_bd"]], axis=0))
        u["Tm"] = bf(_dot_tn(bf(u["at_bd"]), bf(u.pop("PtB"))))
        m1_pair = M1[0:C] + M1[C:2 * C]
        cst = _dot_tn(bf(jnp.concatenate([m1_pair, u["v"]], axis=0)),
                      bf(jnp.concatenate([u["Bh"], u["Kh"]], axis=0)))
        u["cst"] = jnp.where(same, cst, 0.0)
        u["ar"] = bf(jnp.concatenate([u["At"], u["Rt"]], axis=0))

    states = [state_ref[p] for p in range(n_pair)]
    for c in range(n_chunk):
        rows = slice(c * C, (c + 1) * C)
        us = pre[c * n_pair:(c + 1) * n_pair]
        sbs = [bf(S) for S in states]
        zs = [_dot_nt(u["ar"], sb) for u, sb in zip(us, sbs)]
        new_states = [S * u["wc"] + _dot(sb, u["Tm"]) + u["cst"] for u, S, sb in zip(us, states, sbs)]
        ybds = [stack2(Z[C:2 * C]) + _dot(u["Q"], bf(stack2(Z[0:C]))) + u["Y0"] for u, Z in zip(us, zs)]
        ys = [y_bd[0:C] + y_bd[C:2 * C] for y_bd in ybds]
        states = new_states
        y = jnp.concatenate(ys, axis=1)
        mu = head_sum(y) * (1.0 / HEAD_DIM)
        yc = y - mu
        var = head_sum(yc * yc) * (1.0 / HEAD_DIM)
        yn = yc * lax.rsqrt(var + RWKV_GN_EPS) * ln_w + ln_b
        o_ref[rows, :] = ((yn + bonus[rows]) * g[rows]).astype(BF16)
    for p in range(n_pair):
        state_ref[p] = states[p]


def _rwkv(rw, vecs, w2, a2, g2, B, S, lb):
    nt = S // lb
    return pl.pallas_call(
        functools.partial(_rwkv_kernel, n_chunk=lb // RWKV_CHUNK),
        grid=(B, nt),
        in_specs=[pl.BlockSpec((lb, RWKV_PROJ), lambda b, j: (b * nt + j, 0)),
                  pl.BlockSpec((8, RWKV_WIDTH), lambda b, j: (0, 0)),
                  pl.BlockSpec((LANES, RWKV_WIDTH), lambda b, j: (0, 0)),
                  pl.BlockSpec((LANES, RWKV_WIDTH), lambda b, j: (0, 0)),
                  pl.BlockSpec((LANES, RWKV_WIDTH), lambda b, j: (0, 0))],
        out_specs=pl.BlockSpec((lb, RWKV_WIDTH), lambda b, j: (b * nt + j, 0)),
        out_shape=jax.ShapeDtypeStruct((B * S, RWKV_WIDTH), BF16),
        scratch_shapes=[pltpu.VMEM((RWKV_WIDTH // LANES, LANES, LANES), F32)],
        compiler_params=pltpu.CompilerParams(dimension_semantics=("arbitrary", "arbitrary"),
                                             vmem_limit_bytes=VMEM_LIMIT),
        name="rwkv",
    )(rw, vecs, w2, a2, g2)


def _mix_kernel(at_ref, rk_ref, x_ref, mod_ref, wo_a_ref, wo_r_ref, ln_ref, wr_ref, br_ref,
                x1_ref, h2_ref, ti_ref, gt_ref, rank_ref, cnt_ref, tb_ref, base_ref):
    first = (pl.program_id(0) == 0) & (pl.program_id(1) == 0)

    @pl.when(first)
    def _():
        base_ref[...] = jnp.zeros_like(base_ref)

    mod = mod_ref[0]
    n_grp = tb_ref.shape[0]
    mt = x_ref.shape[0] // n_grp
    groups = [slice(g * mt, (g + 1) * mt) for g in range(n_grp)]
    ys = [_dot(at_ref[g, :], wo_a_ref[...]) + _dot(rk_ref[g, :], wo_r_ref[...]) for g in groups]
    logits = []
    for g, y in zip(groups, ys):
        x1 = _layer_norm(DEEPNORM_ALPHA * x_ref[g, :] + (1.0 + mod[2:3, :]) * y) * ln_ref[0:1, :] + ln_ref[1:2, :]
        h2 = _layer_norm(x1) * (1.0 + mod[4:5, :]) + mod[3:4, :]
        x1_ref[g, :] = x1
        h_hi = h2.astype(BF16)
        h2_ref[g, :] = h_hi
        h_lo = (h2 - h_hi.astype(F32)).astype(BF16)
        part = _dot(h_hi, wr_ref[...])
        logits.append(part[:, 0:LANES] + part[:, LANES:2 * LANES] + _dot(h_lo, wr_ref[:, 0:LANES]) + br_ref[...])

    erow = lax.broadcasted_iota(jnp.int32, (N_EXPERTS, mt), 0).astype(F32)
    before = (lax.broadcasted_iota(jnp.int32, (mt, mt), 0)
              < lax.broadcasted_iota(jnp.int32, (mt, mt), 1)).astype(BF16)
    base = base_ref[...]
    for n, (g, lg) in enumerate(zip(groups, logits)):
        cur = jnp.transpose(lg)[0:N_EXPERTS, :]
        vals, idxs = [], []
        for _ in range(TOP_K):
            m = jnp.max(cur, axis=0, keepdims=True)
            idx = jnp.min(jnp.where(cur == m, erow, float(N_EXPERTS)), axis=0, keepdims=True)
            vals.append(m)
            idxs.append(idx)
            cur = jnp.where(erow == idx, -jnp.inf, cur)
        tv = jnp.concatenate(vals, axis=0)
        e = jnp.exp(tv - tv[0:1, :])
        gt_ref[:, g] = e / jnp.sum(e, axis=0, keepdims=True)
        ti_ref[:, g] = jnp.concatenate(idxs, axis=0).astype(jnp.int32)

        onehot = jnp.zeros((N_EXPERTS, mt), F32)
        for idx in idxs:
            onehot = onehot + (erow == idx).astype(F32)
        tot = base[:, 0:1] + _dot(onehot.astype(BF16), before)
        ranks = [jnp.sum(jnp.where(erow == idx, tot, 0.0), axis=0, keepdims=True) for idx in idxs]
        rank_ref[:, g] = jnp.concatenate(ranks, axis=0).astype(jnp.int32)
        tb_ref[n] = base.astype(jnp.int32)
        base = base + jnp.sum(onehot, axis=1, keepdims=True)
    base_ref[...] = base
    cnt_ref[...] = base.astype(jnp.int32)


def _mix(attn_out, rwkv_out, x2d, mod, wo_a, wo_r, ln1, w_router, b_router, B, S, mtile, n_grp):
    D = x2d.shape[1]
    tm = mtile * n_grp
    nt = S // tm
    T = B * S
    tok = lambda b, j: (b * nt + j, 0)
    col = lambda b, j: (0, b * nt + j)
    fixed = lambda b, j: (0, 0)
    return pl.pallas_call(
        _mix_kernel,
        grid=(B, nt),
        in_specs=[pl.BlockSpec((tm, ATTN_WIDTH), tok),
                  pl.BlockSpec((tm, RWKV_WIDTH), tok),
                  pl.BlockSpec((tm, D), tok),
                  pl.BlockSpec((1, 6, D), lambda b, j: (b, 0, 0)),
                  pl.BlockSpec((ATTN_WIDTH, D), fixed),
                  pl.BlockSpec((RWKV_WIDTH, D), fixed),
                  pl.BlockSpec((2, D), fixed),
                  pl.BlockSpec((D, 2 * LANES), fixed),
                  pl.BlockSpec((1, LANES), fixed)],
        out_specs=[pl.BlockSpec((tm, D), tok),
                   pl.BlockSpec((tm, D), tok),
                   pl.BlockSpec((TOP_K, tm), col),
                   pl.BlockSpec((TOP_K, tm), col),
                   pl.BlockSpec((TOP_K, tm), col),
                   pl.BlockSpec((N_EXPERTS, LANES), fixed),
                   pl.BlockSpec((n_grp, N_EXPERTS, LANES), lambda b, j: (b * nt + j, 0, 0))],
        out_shape=[jax.ShapeDtypeStruct((T, D), F32),
                   jax.ShapeDtypeStruct((T, D), BF16),
                   jax.ShapeDtypeStruct((TOP_K, T), jnp.int32),
                   jax.ShapeDtypeStruct((TOP_K, T), F32),
                   jax.ShapeDtypeStruct((TOP_K, T), jnp.int32),
                   jax.ShapeDtypeStruct((N_EXPERTS, LANES), jnp.int32),
                   jax.ShapeDtypeStruct((T // mtile, N_EXPERTS, LANES), jnp.int32)],
        scratch_shapes=[pltpu.VMEM((N_EXPERTS, LANES), F32)],
        compiler_params=pltpu.CompilerParams(dimension_semantics=("arbitrary", "arbitrary"),
                                             vmem_limit_bytes=VMEM_LIMIT),
        name="mix",
    )(attn_out, rwkv_out, x2d, mod, wo_a, wo_r, ln1, w_router, b_router)


RUN_PIECES = tuple(2 ** b for b in range(int(math.log2(MOE_TILE)), -1, -1))
SUBLANES = 8


def _to_tiles(ref, x):
    n = x.shape[0]
    for c in range(SUBLANES):
        ref[pl.ds(c, n, stride=SUBLANES), :] = x[:, c * LANES:(c + 1) * LANES]


def _from_tiles(ref):
    n = ref.shape[0] // SUBLANES
    return jnp.concatenate([ref[pl.ds(c, n, stride=SUBLANES), :] for c in range(SUBLANES)], axis=1)


def _run_copies(n, local, local_start, remote, remote_start, sem, to_remote):
    off = 0
    for piece in RUN_PIECES:
        take = (n & piece) != 0

        @pl.when(take)
        def _(off=off, piece=piece):
            lo = pl.multiple_of((local_start + off) * SUBLANES, SUBLANES)
            ro = pl.multiple_of((remote_start + off) * SUBLANES, SUBLANES)
            loc = local.at[pl.ds(lo, piece * SUBLANES)]
            rem = remote.at[pl.ds(ro, piece * SUBLANES)]
            src, dst = (loc, rem) if to_remote else (rem, loc)
            pltpu.make_async_copy(src, dst, sem).start()

        off = off + (n & piece)


def _dispatch_kernel(tcnt_ref, lstart_ref, gstart_ref, pad_ref, pad_start_ref, n_used_ref, lpos_ref, h2_ref,
                     xs_ref, xbuf, zbuf, sems):
    i = pl.program_id(0)
    tm = h2_ref.shape[0]
    n_loc = TOP_K * tm
    n_blocks = xs_ref.shape[0] // (MOE_BLOCK * SUBLANES)
    zero_sem = sems.at[2]

    @pl.when(i == 0)
    def _():
        zbuf[...] = jnp.zeros_like(zbuf)

        def zero_pad(e, carry):
            _run_copies(pad_ref[e], zbuf, 0, xs_ref, pad_start_ref[e], zero_sem, True)
            return carry

        def zero_tail(b, carry):
            @pl.when(b >= n_used_ref[0])
            def _():
                start = pl.multiple_of(b * (MOE_BLOCK * SUBLANES), MOE_BLOCK * SUBLANES)
                pltpu.make_async_copy(zbuf, xs_ref.at[pl.ds(start, MOE_BLOCK * SUBLANES)], zero_sem).start()
            return carry

        lax.fori_loop(0, N_EXPERTS, zero_pad, 0)
        lax.fori_loop(n_blocks - N_EXPERTS, n_blocks, zero_tail, 0)

    slot = lax.broadcasted_iota(jnp.int32, (n_loc, tm), 0)
    lpos = lpos_ref[...]
    perm = jnp.zeros((n_loc, tm), F32)
    for k in range(TOP_K):
        perm = jnp.where(slot == lpos[k:k + 1, :], 1.0, perm)
    perm = perm.astype(BF16)

    def wait_tile(s):
        pltpu.make_async_copy(xbuf.at[s], xs_ref.at[pl.ds(0, n_loc * SUBLANES)], sems.at[s]).wait()

    cur = i % 2

    @pl.when(i >= 2)
    def _():
        wait_tile(cur)

    _to_tiles(xbuf.at[cur], _dot(perm, h2_ref[...]))

    def issue(e, carry):
        idx = i * N_EXPERTS + e
        _run_copies(tcnt_ref[idx], xbuf.at[cur], lstart_ref[idx], xs_ref, gstart_ref[idx], sems.at[cur], True)
        return carry

    lax.fori_loop(0, N_EXPERTS, issue, 0)

    @pl.when(i == pl.num_programs(0) - 1)
    def _():
        wait_tile(cur)

        @pl.when(i >= 1)
        def _():
            wait_tile(1 - cur)

        n_zero = N_EXPERTS * MOE_BLOCK * SUBLANES
        pltpu.make_async_copy(xs_ref.at[pl.ds(0, n_zero)], xs_ref.at[pl.ds(0, n_zero)], zero_sem).wait()


def _dispatch(tcnt, lstart, gstart, pad, pad_start, n_used, lpos, h2, n_rows, tm):
    T, D = h2.shape
    grid_spec = pltpu.PrefetchScalarGridSpec(
        num_scalar_prefetch=6,
        grid=(T // tm,),
        in_specs=[pl.BlockSpec((TOP_K, tm), lambda i, *_: (0, i)),
                  pl.BlockSpec((tm, D), lambda i, *_: (i, 0))],
        out_specs=pl.BlockSpec(memory_space=pl.ANY),
        scratch_shapes=[pltpu.VMEM((2, TOP_K * tm * SUBLANES, LANES), F32),
                        pltpu.VMEM((MOE_BLOCK * SUBLANES, LANES), F32),
                        pltpu.SemaphoreType.DMA((3,))],
    )
    return pl.pallas_call(
        _dispatch_kernel,
        grid_spec=grid_spec,
        out_shape=jax.ShapeDtypeStruct((n_rows * SUBLANES, LANES), F32),
        compiler_params=pltpu.CompilerParams(dimension_semantics=("arbitrary",),
                                             vmem_limit_bytes=VMEM_LIMIT),
        name="dispatch",
    )(tcnt, lstart, gstart, pad, pad_start, n_used, lpos, h2)


def _experts_kernel(blk_e_ref, n_used_ref, next_e_ref, xs_ref, wgu_hbm, bgu_ref, wd_hbm, bd_ref, ys_ref,
                    wgu_f32, wd_f32, wgu_bf, wd_bf, sems):
    i = pl.program_id(0)
    d_ff = wd_bf.shape[0]
    used = i < n_used_ref[0]
    e = blk_e_ref[i]
    new_expert = (i == 0) | (e != blk_e_ref[jnp.maximum(i - 1, 0)])

    def weight_copies(ex):
        return (pltpu.make_async_copy(wgu_hbm.at[ex], wgu_f32, sems.at[0]),
                pltpu.make_async_copy(wd_hbm.at[ex], wd_f32, sems.at[1]))

    @pl.when(i == 0)
    def _():
        for cp in weight_copies(e):
            cp.start()

    @pl.when(used & new_expert)
    def _():
        for cp in weight_copies(e):
            cp.wait()
        wgu_bf[...] = wgu_f32[...].astype(BF16)
        wd_bf[...] = wd_f32[...].astype(BF16)
        nxt = next_e_ref[i]

        @pl.when(nxt >= 0)
        def _():
            for cp in weight_copies(nxt):
                cp.start()

    @pl.when(used)
    def _():
        xb = _from_tiles(xs_ref).astype(BF16)
        gu = _dot(xb, wgu_bf[...]) + bgu_ref[0]
        gate = jnp.minimum(gu[:, :d_ff], SWIGLU_LIMIT)
        up = jnp.clip(gu[:, d_ff:], -SWIGLU_LIMIT, SWIGLU_LIMIT)
        act = (up + 1.0) * (gate * _sigmoid(SWIGLU_ALPHA * gate))
        _to_tiles(ys_ref, _dot(act.astype(BF16), wd_bf[...]) + bd_ref[0])

    @pl.when(i >= n_used_ref[0])
    def _():
        ys_ref[...] = jnp.zeros_like(ys_ref)


def _experts(blk_e, n_used, next_e, xs, wgu, bgu, wd, bd):
    d_ff, D = wd.shape[1], wd.shape[2]
    n_blocks = xs.shape[0] // (MOE_BLOCK * SUBLANES)
    blk = (MOE_BLOCK * SUBLANES, LANES)

    def last_used(i, n_used_ref):
        return jnp.minimum(i, jnp.maximum(n_used_ref[0] - 1, 0))

    def row_map(i, blk_e_ref, n_used_ref, next_e_ref):
        return (last_used(i, n_used_ref), 0)

    def exp_map(i, blk_e_ref, n_used_ref, next_e_ref):
        return (blk_e_ref[last_used(i, n_used_ref)], 0, 0)

    grid_spec = pltpu.PrefetchScalarGridSpec(
        num_scalar_prefetch=3,
        grid=(n_blocks,),
        in_specs=[pl.BlockSpec(blk, row_map),
                  pl.BlockSpec(memory_space=pl.ANY),
                  pl.BlockSpec((1, 1, 2 * d_ff), exp_map),
                  pl.BlockSpec(memory_space=pl.ANY),
                  pl.BlockSpec((1, 1, D), exp_map)],
        out_specs=pl.BlockSpec(blk, lambda i, *_: (i, 0)),
        scratch_shapes=[pltpu.VMEM((D, 2 * d_ff), F32), pltpu.VMEM((d_ff, D), F32),
                        pltpu.VMEM((D, 2 * d_ff), BF16), pltpu.VMEM((d_ff, D), BF16),
                        pltpu.SemaphoreType.DMA((2,))],
    )
    return pl.pallas_call(
        _experts_kernel,
        grid_spec=grid_spec,
        out_shape=jax.ShapeDtypeStruct(xs.shape, F32),
        compiler_params=pltpu.CompilerParams(dimension_semantics=("arbitrary",),
                                             vmem_limit_bytes=VMEM_LIMIT),
        name="experts",
    )(blk_e, n_used, next_e, xs, wgu, bgu, wd, bd)


def _combine_kernel(tcnt_ref, lstart_ref, gstart_ref, ys_ref, lpos_ref, gt_ref, x1_ref, mod_ref, ln_ref,
                    o_ref, buf, sems):
    nt = pl.num_programs(1)
    n_steps = pl.num_programs(0) * nt
    i = pl.program_id(0) * nt + pl.program_id(1)
    n_grp = buf.shape[0] // 2
    tm = x1_ref.shape[0] // n_grp
    n_loc = TOP_K * tm

    def fetch(tile, s):
        def issue(e, carry):
            idx = tile * N_EXPERTS + e
            _run_copies(tcnt_ref[idx], buf.at[s], lstart_ref[idx], ys_ref, gstart_ref[idx], sems.at[s], False)
            return carry

        lax.fori_loop(0, N_EXPERTS, issue, 0)

    @pl.when(i == 0)
    def _():
        for g in range(n_grp):
            fetch(g, g)

    @pl.when(i + 1 < n_steps)
    def _():
        for g in range(n_grp):
            fetch((i + 1) * n_grp + g, ((i + 1) % 2) * n_grp + g)

    slot = lax.broadcasted_iota(jnp.int32, (tm, n_loc), 1)
    groups = [slice(g * tm, (g + 1) * tm) for g in range(n_grp)]
    picks = []
    for rows in groups:
        lpos = lpos_ref[rows, :]
        gt = gt_ref[rows, :]
        pick = jnp.zeros((tm, n_loc), F32)
        for k in range(TOP_K):
            pick = jnp.where(slot == lpos[:, k:k + 1], gt[:, k:k + 1], pick)
        picks.append(pick.astype(BF16))
    cur = (i % 2) * n_grp
    ys = []
    for g, pick in enumerate(picks):
        pltpu.make_async_copy(ys_ref.at[pl.ds(0, n_loc * SUBLANES)], buf.at[cur + g], sems.at[cur + g]).wait()
        ys.append(_dot(pick, _from_tiles(buf.at[cur + g]).astype(BF16)))
    for rows, y in zip(groups, ys):
        z = DEEPNORM_ALPHA * x1_ref[rows, :] + (1.0 + mod_ref[0][5:6, :]) * y
        o_ref[rows, :] = _layer_norm(z) * ln_ref[0:1, :] + ln_ref[1:2, :]


def _combine(tcnt, lstart, gstart, ys, lpos_t, gates_t, x1, mod, ln2, B, S, mtile, n_grp):
    T, D = x1.shape
    tm = mtile * n_grp
    nt = S // tm
    tok = lambda b, j, *_: (b * nt + j, 0)
    grid_spec = pltpu.PrefetchScalarGridSpec(
        num_scalar_prefetch=3,
        grid=(B, nt),
        in_specs=[pl.BlockSpec(memory_space=pl.ANY),
                  pl.BlockSpec((tm, TOP_K), tok),
                  pl.BlockSpec((tm, TOP_K), tok),
                  pl.BlockSpec((tm, D), tok),
                  pl.BlockSpec((1, 6, D), lambda b, j, *_: (b, 0, 0)),
                  pl.BlockSpec((2, D), lambda b, j, *_: (0, 0))],
        out_specs=pl.BlockSpec((tm, D), tok),
        scratch_shapes=[pltpu.VMEM((2 * n_grp, TOP_K * mtile * SUBLANES, LANES), F32),
                        pltpu.SemaphoreType.DMA((2 * n_grp,))],
    )
    return pl.pallas_call(
        _combine_kernel,
        grid_spec=grid_spec,
        out_shape=jax.ShapeDtypeStruct((T, D), F32),
        compiler_params=pltpu.CompilerParams(dimension_semantics=("arbitrary", "arbitrary"),
                                             vmem_limit_bytes=VMEM_LIMIT),
        name="combine",
    )(tcnt, lstart, gstart, ys, lpos_t, gates_t, x1, mod, ln2)


def _pad_rows(w, rows):
    return jnp.pad(w, ((0, rows - w.shape[0]), (0, 0)))


def _pad_cols(w, cols):
    return jnp.pad(w, ((0, 0), (0, cols - w.shape[1])))


def _layer(x, c, positions, w_ada, b_ada, w_in, shift_mu, rwkv_w0, rwkv_w2, rwkv_a0, rwkv_a2, rwkv_g2,
           rwkv_k_k, rwkv_k_a, rwkv_r_k, rwkv_ln_w, rwkv_ln_b, attn_sinks, w_out, ln1_g, ln1_b,
           w_router, b_router, w_gate_up, b_gate_up, w_down, b_down, ln2_g, ln2_b):
    B, S, D = x.shape
    T = B * S

    q0, k0, v0 = 0, ATTN_WIDTH, ATTN_WIDTH + KV_WIDTH
    r0 = ATTN_WIDTH + 2 * KV_WIDTH
    heads = lambda base: [w_in[:, base + h * HEAD_DIM: base + (h + 1) * HEAD_DIM] for h in range(N_KV_HEADS)]
    dup = lambda hs: [w for w in hs for _ in range(2)]
    w_attn = jnp.concatenate([w_in[:, q0:q0 + ATTN_WIDTH]] + dup(heads(k0)) + dup(heads(v0)), axis=1).astype(BF16)
    lora0 = r0 + 3 * RWKV_WIDTH
    lora = (DECAY_LORA, AAA_LORA, GATE_LORA)
    pieces_w = [w_in[:, r0:lora0]]
    pieces_mu = [shift_mu[None, 0:3 * RWKV_WIDTH]]
    off = lora0
    for n in lora:
        pieces_w.append(_pad_cols(w_in[:, off:off + n], LANES))
        pieces_mu.append(_pad_cols(shift_mu[None, off - r0:off - r0 + n], LANES))
        off += n
    w_rwkv = jnp.concatenate(pieces_w, axis=1).astype(BF16)
    mu = jnp.concatenate(pieces_mu, axis=1)
    inv_freq = ROPE_THETA ** (-jnp.arange(0, ROT_DIM, 2, dtype=F32) / ROT_DIM)
    lane_p = jnp.arange(LANES) % HEAD_DIM
    n_freq = ROT_DIM // 2
    rot_tab = jnp.zeros((8, LANES), F32)
    rot_tab = rot_tab.at[0].set(jnp.where(lane_p < ROT_DIM, 0.0, 1.0))
    rot_tab = rot_tab.at[1].set(jnp.where(lane_p < n_freq, -1.0, 0.0))
    rot_tab = rot_tab.at[2].set(jnp.where((lane_p >= n_freq) & (lane_p < ROT_DIM), 1.0, 0.0))
    freq_tab = jnp.broadcast_to(inv_freq[:, None], (n_freq, LANES))
    lane_freq = (jnp.arange(n_freq)[:, None] == (lane_p % n_freq)[None, :]) & (lane_p < ROT_DIM)[None, :]
    zeros = jnp.zeros_like(lane_freq)
    expand = jnp.concatenate([jnp.concatenate([lane_freq, zeros], axis=1),
                              jnp.concatenate([zeros, lane_freq], axis=1)], axis=0).astype(BF16)
    vecs = jnp.stack([rwkv_w0, rwkv_a0, rwkv_k_k, rwkv_k_a, rwkv_r_k.reshape(-1), rwkv_ln_w, rwkv_ln_b,
                      jnp.zeros_like(rwkv_w0)])
    w2 = _pad_rows(rwkv_w2, LANES).astype(BF16)
    a2 = _pad_rows(rwkv_a2, LANES).astype(BF16)
    g2 = _pad_rows(rwkv_g2, LANES).astype(BF16)
    wo_a = w_out[:ATTN_WIDTH].astype(BF16)
    wo_r = w_out[ATTN_WIDTH:].astype(BF16)
    w_r_hi = w_router.astype(BF16)
    w_r_lo = (w_router - w_r_hi.astype(F32)).astype(BF16)
    w_r = jnp.concatenate([_pad_cols(w_r_hi, LANES), _pad_cols(w_r_lo, LANES)], axis=1)
    b_r = jnp.concatenate([b_router, jnp.full((LANES - N_EXPERTS,), NEG_INF, F32)])[None, :]

    mod = _mod(c, w_ada, b_ada).reshape(B, 6, D)
    attn_out, rw = _inproj(x, positions, mod, w_attn, w_rwkv, mu, rot_tab, freq_tab, expand, attn_sinks,
                           min(INPROJ_TILE, S))
    rwkv_out = _rwkv(rw, vecs, w2, a2, g2, B, S, min(RWKV_STEP, S))

    mtile = min(MOE_TILE, S)
    x1, h2, top_i, gates, rank, cnt, tbase = _mix(attn_out, rwkv_out, x.reshape(T, D), mod, wo_a, wo_r,
                                                   jnp.stack([ln1_g, ln1_b]), w_r, b_r, B, S, mtile,
                                                   max(1, min(MIX_TILES, S // mtile)))

    counts = cnt[:, 0]
    padded = (counts + MOE_BLOCK - 1) // MOE_BLOCK * MOE_BLOCK
    pend = jnp.cumsum(padded)
    pstart = pend - padded
    n_blocks = T * TOP_K // MOE_BLOCK + N_EXPERTS
    blk_row = jnp.arange(n_blocks, dtype=jnp.int32) * MOE_BLOCK
    blk_e = jnp.minimum(jnp.sum((pend[None, :] <= blk_row[:, None]).astype(jnp.int32), axis=1), N_EXPERTS - 1)
    n_used = (pend[-1:] // MOE_BLOCK).astype(jnp.int32)
    tb = tbase[:, :, 0]
    tcnt = jnp.concatenate([tb[1:], counts[None]], axis=0) - tb
    lstart = jnp.cumsum(tcnt, axis=1) - tcnt
    gstart = pstart[None, :] + tb
    shift = jnp.repeat(jnp.transpose(lstart - tb), mtile, axis=1)
    experts = jnp.arange(N_EXPERTS, dtype=jnp.int32)
    lpos = rank + jnp.sum(jnp.where(top_i[None] == experts[:, None, None], shift[:, None, :], 0), axis=0)
    flat = lambda a: a.reshape(-1).astype(jnp.int32)

    xs = _dispatch(flat(tcnt), flat(lstart), flat(gstart), flat(padded - counts), flat(pstart + counts), n_used,
                   lpos, h2, n_blocks * MOE_BLOCK, mtile)
    later_with_rows = (experts[None, :] > experts[:, None]) & (counts[None, :] > 0)
    next_of = jnp.min(jnp.where(later_with_rows, experts[None, :], N_EXPERTS), axis=1)
    next_of = jnp.where(next_of < N_EXPERTS, next_of, -1)
    next_e = jnp.sum(jnp.where(blk_e[:, None] == experts[None, :], next_of[None, :], 0), axis=1).astype(jnp.int32)
    ys = _experts(blk_e, n_used, next_e, xs, w_gate_up, b_gate_up[:, None, :], w_down, b_down[:, None, :])
    out = _combine(flat(tcnt), flat(lstart), flat(gstart), ys, jnp.transpose(lpos), jnp.transpose(gates), x1, mod,
                   jnp.stack([ln2_g, ln2_b]), B, S, mtile, max(1, min(COMBINE_TILES, S // mtile)))
    return out.reshape(B, S, D)


def kernel(x, c, positions, w_ada, b_ada, w_in, shift_mu, rwkv_w0, rwkv_w2, rwkv_a0, rwkv_a2, rwkv_g2,
           rwkv_k_k, rwkv_k_a, rwkv_r_k, rwkv_ln_w, rwkv_ln_b, attn_sinks, w_out, ln1_g, ln1_b,
           w_router, b_router, w_gate_up, b_gate_up, w_down, b_down, ln2_g, ln2_b):
    for l in range(DEPTH):
        x = _layer(x, c, positions, w_ada[l], b_ada[l], w_in[l], shift_mu[l], rwkv_w0[l], rwkv_w2[l],
                   rwkv_a0[l], rwkv_a2[l], rwkv_g2[l], rwkv_k_k[l], rwkv_k_a[l], rwkv_r_k[l], rwkv_ln_w[l],
                   rwkv_ln_b[l], attn_sinks[l], w_out[l], ln1_g[l], ln1_b[l], w_router[l], b_router[l],
                   w_gate_up[l], b_gate_up[l], w_down[l], b_down[l], ln2_g[l], ln2_b[l])
    return x
```

```python
import functools
import math

import jax
import jax.numpy as jnp
from jax import lax
from jax.experimental import pallas as pl
from jax.experimental.pallas import tpu as pltpu

F32 = jnp.float32
BF16 = jnp.bfloat16

HEAD_DIM = 64
N_ATTN_HEADS = 8
N_KV_HEADS = 2
N_RWKV_HEADS = 8
ATTN_WIDTH = N_ATTN_HEADS * HEAD_DIM
KV_WIDTH = N_KV_HEADS * HEAD_DIM
RWKV_WIDTH = N_RWKV_HEADS * HEAD_DIM
ATTN_BLOCK = 128
ROT_DIM = HEAD_DIM // 4
ROPE_THETA = 500000.0
DECAY_LORA = 32
AAA_LORA = 32
GATE_LORA = 96
N_EXPERTS = 32
TOP_K = 4
SWIGLU_LIMIT = 7.0
SWIGLU_ALPHA = 1.702
LN_EPS = 1e-5
RWKV_GN_EPS = 64e-5
NEG_INF = -1e30
DEPTH = 1
DEEPNORM_ALPHA = (2 * DEPTH) ** 0.25

LANES = 128
RWKV_CHUNK = 64
RWKV_STEP = 512
INPROJ_TILE = 512
RWKV_PROJ_CHUNK = 512
MOE_BLOCK = 512
MOE_TILE = 256
MIX_TILES = 4
ATTN_PROJ = ATTN_WIDTH + 4 * KV_WIDTH
RWKV_PROJ = 3 * RWKV_WIDTH + 3 * LANES
VMEM_LIMIT = 48 * 1024 * 1024


def _dot(a, b):
    return jnp.dot(a, b, preferred_element_type=F32)


def _dot_nt(a, b):
    return lax.dot_general(a, b, (((1,), (1,)), ((), ())), preferred_element_type=F32)


def _dot_tn(a, b):
    return lax.dot_general(a, b, (((0,), (0,)), ((), ())), preferred_element_type=F32)


def _split3(x):
    h = x.astype(BF16)
    r1 = x - h.astype(F32)
    m = r1.astype(BF16)
    lo = (r1 - m.astype(F32)).astype(BF16)
    return h, m, lo


def _dot_exact_lhs(m_bf16, x):
    h, m, lo = _split3(x)
    return _dot(m_bf16, h) + _dot(m_bf16, m) + _dot(m_bf16, lo)


def _layer_norm(x):
    mu = jnp.mean(x, axis=-1, keepdims=True)
    xc = x - mu
    var = jnp.mean(xc * xc, axis=-1, keepdims=True)
    return xc * lax.rsqrt(var + LN_EPS)


def _sigmoid(x):
    return 1.0 / (1.0 + jnp.exp(-x))


def _mod_kernel(c_ref, w_ref, b_ref, o_ref):
    c = c_ref[...]
    s = c * _sigmoid(c)
    o_ref[...] = jnp.dot(s, w_ref[...], preferred_element_type=F32,
                         precision=lax.Precision.HIGHEST) + b_ref[...]


def _mod(c, w_ada, b_ada):
    B, D = c.shape
    n = w_ada.shape[1] // D
    return pl.pallas_call(
        _mod_kernel,
        grid=(n,),
        in_specs=[pl.BlockSpec((B, D), lambda i: (0, 0)),
                  pl.BlockSpec((D, D), lambda i: (0, i)),
                  pl.BlockSpec((1, D), lambda i: (0, i))],
        out_specs=pl.BlockSpec((B, D), lambda i: (0, i)),
        out_shape=jax.ShapeDtypeStruct((B, n * D), F32),
        compiler_params=pltpu.CompilerParams(dimension_semantics=("arbitrary",),
                                             vmem_limit_bytes=VMEM_LIMIT),
        name="mod",
    )(c, w_ada, b_ada.reshape(1, -1))


def _inproj_kernel(x_ref, pos_ref, mod_ref, wa_ref, wr_ref, mu_ref, rt_ref, fq_ref, ex_ref, sink_ref,
                   at_ref, rw_ref, qkv_ref, kv_prev_ref, carry_ref):
    j = pl.program_id(1)

    @pl.when(j == 0)
    def _():
        kv_prev_ref[...] = jnp.zeros_like(kv_prev_ref)

    x = x_ref[0]
    tm = x.shape[0]
    mod = mod_ref[0]
    h = _layer_norm(x) * (1.0 + mod[1:2, :]) + mod[0:1, :]
    hb = h.astype(BF16)

    pa = _dot(hb, wa_ref[...])
    ang = pos_ref[0].astype(F32) * fq_ref[:, 0:1]
    pieces = _split3(jnp.concatenate([jnp.cos(ang), jnp.sin(ang)], axis=0))
    trig = _dot_tn(pieces[0], ex_ref[...]) + _dot_tn(pieces[1], ex_ref[...]) + _dot_tn(pieces[2], ex_ref[...])
    cs = trig[:, 0:LANES] + rt_ref[0:1, :]
    sn = trig[:, LANES:2 * LANES]
    m_lo = rt_ref[1:2, :]
    m_hi = rt_ref[2:3, :]
    n_q = ATTN_WIDTH // LANES
    n_rot = (ATTN_WIDTH + 2 * KV_WIDTH) // LANES
    for ch in range(n_rot):
        t = pa[:, ch * LANES:(ch + 1) * LANES]
        if ch < n_q:
            t = t * (1.0 / math.sqrt(HEAD_DIM))
        up = pltpu.roll(t, LANES - ROT_DIM // 2, 1)
        dn = pltpu.roll(t, ROT_DIM // 2, 1)
        o = t * cs + sn * (m_lo * up + m_hi * dn)
        qkv_ref[:, ch * LANES:(ch + 1) * LANES] = o.astype(BF16)
    qkv_ref[:, n_rot * LANES:] = pa[:, n_rot * LANES:].astype(BF16)

    row = lax.broadcasted_iota(jnp.int32, (tm, 1), 0)

    def rwkv_columns(c0, c1):
        def emit():
            pr = _dot(hb, wr_ref[:, c0:c1])
            prev = pltpu.roll(pr, 1, 0)
            carry = jnp.where(j == 0, 0.0, carry_ref[:, c0:c1])
            prev = jnp.where(row == 0, carry, prev)
            carry_ref[:, c0:c1] = pr[tm - 1:tm, :]
            rw_ref[:, c0:c1] = pr + (prev - pr) * mu_ref[:, c0:c1]
        return emit

    bounds = list(range(0, RWKV_PROJ, RWKV_PROJ_CHUNK)) + [RWKV_PROJ]
    kv_w = 2 * KV_WIDTH
    _attn_body(qkv_ref.at[:, pl.ds(0, ATTN_WIDTH)], qkv_ref.at[:, pl.ds(ATTN_WIDTH, kv_w)],
               kv_prev_ref.at[:, pl.ds(0, kv_w)], qkv_ref.at[:, pl.ds(ATTN_WIDTH + kv_w, kv_w)],
               kv_prev_ref.at[:, pl.ds(kv_w, kv_w)], sink_ref, at_ref, j > 0,
               interleave=[rwkv_columns(c0, c1) for c0, c1 in zip(bounds[:-1], bounds[1:])])
    kv_prev_ref[...] = qkv_ref[tm - ATTN_BLOCK:tm, ATTN_WIDTH:]


def _inproj(x, positions, mod, w_attn, w_rwkv, mu, rot_tab, freq_tab, expand, sinks, tm):
    B, S, D = x.shape
    nt = S // tm
    return pl.pallas_call(
        _inproj_kernel,
        grid=(B, nt),
        in_specs=[pl.BlockSpec((1, tm, D), lambda b, j: (b, j, 0)),
                  pl.BlockSpec((1, 1, tm), lambda b, j: (b, 0, j)),
                  pl.BlockSpec((1, 6, D), lambda b, j: (b, 0, 0)),
                  pl.BlockSpec((D, ATTN_PROJ), lambda b, j: (0, 0)),
                  pl.BlockSpec((D, RWKV_PROJ), lambda b, j: (0, 0)),
                  pl.BlockSpec((1, RWKV_PROJ), lambda b, j: (0, 0)),
                  pl.BlockSpec((8, LANES), lambda b, j: (0, 0)),
                  pl.BlockSpec(freq_tab.shape, lambda b, j: (0, 0)),
                  pl.BlockSpec(expand.shape, lambda b, j: (0, 0)),
                  pl.BlockSpec(memory_space=pltpu.SMEM)],
        out_specs=[pl.BlockSpec((tm, ATTN_WIDTH), lambda b, j: (b * nt + j, 0)),
                   pl.BlockSpec((tm, RWKV_PROJ), lambda b, j: (b * nt + j, 0))],
        out_shape=[jax.ShapeDtypeStruct((B * S, ATTN_WIDTH), BF16),
                   jax.ShapeDtypeStruct((B * S, RWKV_PROJ), F32)],
        scratch_shapes=[pltpu.VMEM((tm, ATTN_PROJ), BF16),
                        pltpu.VMEM((ATTN_BLOCK, ATTN_PROJ - ATTN_WIDTH), BF16),
                        pltpu.VMEM((1, RWKV_PROJ), F32)],
        compiler_params=pltpu.CompilerParams(dimension_semantics=("arbitrary", "arbitrary"),
                                             vmem_limit_bytes=VMEM_LIMIT),
        name="inproj",
    )(x, positions.reshape(B, 1, S), mod, w_attn, w_rwkv, mu, rot_tab, freq_tab, expand, sinks)


def _attn_body(q_ref, kc_ref, kp_ref, vc_ref, vp_ref, sink_ref, o_ref, has_prev, interleave=()):
    blk = ATTN_BLOCK
    n_sub = q_ref.shape[0] // blk
    qi = lax.broadcasted_iota(jnp.int32, (blk, 2 * blk), 0)
    kj = lax.broadcasted_iota(jnp.int32, (blk, 2 * blk), 1)
    band = (kj > qi) & (kj <= qi + blk)
    first = band & ((kj >= blk) | has_prev)
    lane = lax.broadcasted_iota(jnp.int32, (1, LANES), 1)
    lo = (lane < HEAD_DIM).astype(BF16)
    hi = (lane >= HEAD_DIM).astype(BF16)
    halves = {}
    for u in range(n_sub):
        for g in range(N_KV_HEADS):
            sl = slice(g * LANES, (g + 1) * LANES)
            prev_k = kp_ref[:, sl] if u == 0 else kc_ref[(u - 1) * blk:u * blk, sl]
            prev_v = vp_ref[:, sl] if u == 0 else vc_ref[(u - 1) * blk:u * blk, sl]
            kcat = jnp.concatenate([prev_k, kc_ref[u * blk:(u + 1) * blk, sl]], axis=0)
            vcat = jnp.concatenate([prev_v, vc_ref[u * blk:(u + 1) * blk, sl]], axis=0)
            halves[u, g] = ((kcat * lo, vcat * lo), (kcat * hi, vcat * hi))
    units = [(u, c, half) for u in range(n_sub) for c in range(ATTN_WIDTH // LANES) for half in range(2)]
    scores = [_dot_nt(q_ref[u * blk:(u + 1) * blk, c * LANES:(c + 1) * LANES], halves[u, c // 2][half][0])
              for u, c, half in units]
    probs, denoms = [], []
    every = max(1, len(units) // max(1, len(interleave)))
    for n, ((u, c, half), s) in enumerate(zip(units, scores)):
        if n % every == 0 and n // every < len(interleave):
            interleave[n // every]()
        sink = sink_ref[2 * c + half]
        s = jnp.where(first if u == 0 else band, s, NEG_INF)
        m = jnp.maximum(jnp.max(s, axis=-1, keepdims=True), sink)
        p = jnp.exp(s - m)
        denoms.append(jnp.sum(p, axis=-1, keepdims=True) + jnp.exp(sink - m))
        probs.append(p.astype(BF16))
    outs = [_dot(p, halves[u, c // 2][half][1]) / d for (u, c, half), p, d in zip(units, probs, denoms)]
    for n, (u, c, half) in enumerate(units):
        if half == 0:
            o_ref[u * blk:(u + 1) * blk, c * LANES:(c + 1) * LANES] = (outs[n] + outs[n + 1]).astype(BF16)


def _rwkv_kernel(rw_ref, vec_ref, w2_ref, a2_ref, g2_ref, o_ref, state_ref, *, n_chunk):
    j = pl.program_id(1)
    C = RWKV_CHUNK
    W = RWKV_WIDTH
    n_pair = W // LANES

    @pl.when(j == 0)
    def _():
        state_ref[...] = jnp.zeros_like(state_ref)

    w0 = vec_ref[0:1, :]
    a0 = vec_ref[1:2, :]
    k_k = vec_ref[2:3, :]
    k_a = vec_ref[3:4, :]
    r_k = vec_ref[4:5, :]
    ln_w = vec_ref[5:6, :]
    ln_b = vec_ref[6:7, :]

    r = rw_ref[:, 0:W]
    k = rw_ref[:, W:2 * W]
    v = rw_ref[:, 2 * W:3 * W]
    wl = rw_ref[:, 3 * W:3 * W + LANES]
    al = rw_ref[:, 3 * W + LANES:3 * W + 2 * LANES]
    gl = rw_ref[:, 3 * W + 2 * LANES:3 * W + 3 * LANES]

    ri = lax.broadcasted_iota(jnp.int32, (LANES, LANES), 0)
    ci = lax.broadcasted_iota(jnp.int32, (LANES, LANES), 1)
    same = (ri // HEAD_DIM) == (ci // HEAD_DIM)
    strict = same & ((ri % HEAD_DIM) > (ci % HEAD_DIM))
    incl = same & ((ri % HEAD_DIM) >= (ci % HEAD_DIM))
    lane = lax.broadcasted_iota(jnp.int32, (1, LANES), 1)
    m0 = (lane < HEAD_DIM).astype(F32)
    m1 = 1.0 - m0
    tri = (lax.broadcasted_iota(jnp.int32, (C, C), 0) >= lax.broadcasted_iota(jnp.int32, (C, C), 1)).astype(BF16)

    def head_sum(xv):
        outs = []
        for p in range(n_pair):
            xp = xv[:, p * LANES:(p + 1) * LANES]
            s0 = jnp.sum(xp * m0, axis=1, keepdims=True)
            s1 = jnp.sum(xp * m1, axis=1, keepdims=True)
            outs.append(s0 * m0 + s1 * m1)
        return jnp.concatenate(outs, axis=1)

    def stack2(xp):
        return jnp.concatenate([xp * m0, xp * m1], axis=0)

    z = w0 + _dot(jnp.tanh(wl).astype(BF16), w2_ref[...])
    lw = -math.exp(-0.5) * _sigmoid(z)
    a = _sigmoid(a0 + _dot(al.astype(BF16), a2_ref[...]))
    g = _dot(_sigmoid(gl).astype(BF16), g2_ref[...])
    kk = k * k_k
    kkn = kk / jnp.maximum(jnp.sqrt(head_sum(kk * kk)), 1e-12)
    k2 = k * (1.0 + (a - 1.0) * k_a)
    av = -kkn
    bv = kkn * a
    bonus = head_sum(r * k2 * r_k) * v

    eye = (ri == ci).astype(F32)
    bf = lambda t: t.astype(BF16)

    pre = []
    for c in range(n_chunk):
        rows = slice(c * C, (c + 1) * C)
        lwc = lw[rows]
        cw = _dot_exact_lhs(tri, lwc)
        cwl = cw[C - 1:C, :]
        e_in = jnp.exp(cw)
        e_neg = jnp.exp(-cw)
        e_rem = jnp.exp(cwl - cw)
        wc = jnp.exp(cwl)
        Rt = r[rows] * e_in
        At = av[rows] * jnp.exp(cw - lwc)
        Bb = bv[rows] * e_neg
        Kb = k2[rows] * e_neg
        Bh = bv[rows] * e_rem
        Kh = k2[rows] * e_rem
        vc = v[rows]
        for p in range(n_pair):
            sl = slice(p * LANES, (p + 1) * LANES)
            pre.append(dict(At=At[:, sl], Rt=Rt[:, sl], Bb=Bb[:, sl], Kb=Kb[:, sl], Bh=Bh[:, sl], Kh=Kh[:, sl],
                            v=vc[:, sl], wc=wc[:, sl]))

    for u in pre:
        u["at_bd"] = stack2(u["At"])
        lhs = bf(jnp.concatenate([u["at_bd"], stack2(u["Rt"])], axis=0))
        rhs = bf(jnp.concatenate([stack2(u["Bb"]), stack2(u["Kb"])], axis=0))
        u["G"] = _dot_nt(lhs, rhs)
    for u in pre:
        G = u.pop("G")
        u["a_ab"] = jnp.where(strict, G[0:2 * C, 0:2 * C], 0.0)
        u["a_ak"] = bf(jnp.where(strict, G[0:2 * C, 2 * C:4 * C], 0.0))
        u["a_rb"] = bf(jnp.where(incl, G[2 * C:4 * C, 0:2 * C], 0.0))
        u["a_rk"] = bf(jnp.where(incl, G[2 * C:4 * C, 2 * C:4 * C], 0.0))
        u["v_bd"] = bf(stack2(u["v"]))
    for u in pre:
        xb = bf(u["a_ab"])
        u["P"] = eye + u.pop("a_ab")
        u["X"] = _dot(xb, xb)
        u["M0"] = _dot(u["a_ak"], u["v_bd"])
    for _ in range(int(math.log2(C)) - 2):
        for u in pre:
            xb = bf(u["X"])
            Wm = _dot(xb, jnp.concatenate([bf(u["P"]), xb], axis=1))
            u["P"] = u["P"] + Wm[:, 0:LANES]
            u["X"] = Wm[:, LANES:2 * LANES]
    for u in pre:
        u["P"] = bf(u["P"] + _dot(bf(u.pop("X")), bf(u["P"])))
    for u in pre:
        u["M1"] = _dot(u["P"], bf(u.pop("M0")))
        u["Q"] = bf(_dot(u["a_rb"], u["P"]))
        u["PtB"] = _dot_tn(u["P"], bf(stack2(u["Bh"])))
    for u in pre:
        M1 = u.pop("M1")
        u["Y0"] = _dot(jnp.concatenate([u["a_rb"], u["a_rk"]], axis=1),
                       jnp.concatenate([bf(M1), u["v_bd"]], axis=0))
        u["Tm"] = bf(_dot_tn(bf(u["at_bd"]), bf(u.pop("PtB"))))
        m1_pair = M1[0:C] + M1[C:2 * C]
        cst = _dot_tn(bf(jnp.concatenate([m1_pair, u["v"]], axis=0)),
                      bf(jnp.concatenate([u["Bh"], u["Kh"]], axis=0)))
        u["cst"] = jnp.where(same, cst, 0.0)
        u["ar"] = bf(jnp.concatenate([u["At"], u["Rt"]], axis=0))

    states = [state_ref[p] for p in range(n_pair)]
    for c in range(n_chunk):
        rows = slice(c * C, (c + 1) * C)
        us = pre[c * n_pair:(c + 1) * n_pair]
        sbs = [bf(S) for S in states]
        zs = [_dot_nt(u["ar"], sb) for u, sb in zip(us, sbs)]
        new_states = [S * u["wc"] + _dot(sb, u["Tm"]) + u["cst"] for u, S, sb in zip(us, states, sbs)]
        ybds = [stack2(Z[C:2 * C]) + _dot(u["Q"], bf(stack2(Z[0:C]))) + u["Y0"] for u, Z in zip(us, zs)]
        ys = [y_bd[0:C] + y_bd[C:2 * C] for y_bd in ybds]
        states = new_states
        y = jnp.concatenate(ys, axis=1)
        mu = head_sum(y) * (1.0 / HEAD_DIM)
        yc = y - mu
        var = head_sum(yc * yc) * (1.0 / HEAD_DIM)
        yn = yc * lax.rsqrt(var + RWKV_GN_EPS) * ln_w + ln_b
        o_ref[rows, :] = ((yn + bonus[rows]) * g[rows]).astype(BF16)
    for p in range(n_pair):
        state_ref[p] = states[p]


def _rwkv(rw, vecs, w2, a2, g2, B, S, lb):
    nt = S // lb
    return pl.pallas_call(
        functools.partial(_rwkv_kernel, n_chunk=lb // RWKV_CHUNK),
        grid=(B, nt),
        in_specs=[pl.BlockSpec((lb, RWKV_PROJ), lambda b, j: (b * nt + j, 0)),
                  pl.BlockSpec((8, RWKV_WIDTH), lambda b, j: (0, 0)),
                  pl.BlockSpec((LANES, RWKV_WIDTH), lambda b, j: (0, 0)),
                  pl.BlockSpec((LANES, RWKV_WIDTH), lambda b, j: (0, 0)),
                  pl.BlockSpec((LANES, RWKV_WIDTH), lambda b, j: (0, 0))],
        out_specs=pl.BlockSpec((lb, RWKV_WIDTH), lambda b, j: (b * nt + j, 0)),
        out_shape=jax.ShapeDtypeStruct((B * S, RWKV_WIDTH), BF16),
        scratch_shapes=[pltpu.VMEM((RWKV_WIDTH // LANES, LANES, LANES), F32)],
        compiler_params=pltpu.CompilerParams(dimension_semantics=("arbitrary", "arbitrary"),
                                             vmem_limit_bytes=VMEM_LIMIT),
        name="rwkv",
    )(rw, vecs, w2, a2, g2)


def _mix_kernel(at_ref, rk_ref, x_ref, mod_ref, wo_a_ref, wo_r_ref, ln_ref, wr_ref, br_ref,
                x1_ref, h2_ref, ti_ref, gt_ref, rank_ref, cnt_ref, tb_ref, base_ref):
    first = (pl.program_id(0) == 0) & (pl.program_id(1) == 0)

    @pl.when(first)
    def _():
        base_ref[...] = jnp.zeros_like(base_ref)

    mod = mod_ref[0]
    n_grp = tb_ref.shape[0]
    mt = x_ref.shape[0] // n_grp
    groups = [slice(g * mt, (g + 1) * mt) for g in range(n_grp)]
    ys = [_dot(at_ref[g, :], wo_a_ref[...]) + _dot(rk_ref[g, :], wo_r_ref[...]) for g in groups]
    logits = []
    for g, y in zip(groups, ys):
        x1 = _layer_norm(DEEPNORM_ALPHA * x_ref[g, :] + (1.0 + mod[2:3, :]) * y) * ln_ref[0:1, :] + ln_ref[1:2, :]
        h2 = _layer_norm(x1) * (1.0 + mod[4:5, :]) + mod[3:4, :]
        x1_ref[g, :] = x1
        h_hi = h2.astype(BF16)
        h2_ref[g, :] = h_hi
        h_lo = (h2 - h_hi.astype(F32)).astype(BF16)
        part = _dot(h_hi, wr_ref[...])
        logits.append(part[:, 0:LANES] + part[:, LANES:2 * LANES] + _dot(h_lo, wr_ref[:, 0:LANES]) + br_ref[...])

    erow = lax.broadcasted_iota(jnp.int32, (N_EXPERTS, mt), 0).astype(F32)
    before = (lax.broadcasted_iota(jnp.int32, (mt, mt), 0)
              < lax.broadcasted_iota(jnp.int32, (mt, mt), 1)).astype(BF16)
    base = base_ref[...]
    for n, (g, lg) in enumerate(zip(groups, logits)):
        cur = jnp.transpose(lg)[0:N_EXPERTS, :]
        vals, idxs = [], []
        for _ in range(TOP_K):
            m = jnp.max(cur, axis=0, keepdims=True)
            idx = jnp.min(jnp.where(cur == m, erow, float(N_EXPERTS)), axis=0, keepdims=True)
            vals.append(m)
            idxs.append(idx)
            cur = jnp.where(erow == idx, -jnp.inf, cur)
        tv = jnp.concatenate(vals, axis=0)
        e = jnp.exp(tv - tv[0:1, :])
        gt_ref[:, g] = e / jnp.sum(e, axis=0, keepdims=True)
        ti_ref[:, g] = jnp.concatenate(idxs, axis=0).astype(jnp.int32)

        onehot = jnp.zeros((N_EXPERTS, mt), F32)
        for idx in idxs:
            onehot = onehot + (erow == idx).astype(F32)
        tot = base[:, 0:1] + _dot(onehot.astype(BF16), before)
        ranks = [jnp.sum(jnp.where(erow == idx, tot, 0.0), axis=0, keepdims=True) for idx in idxs]
        rank_ref[:, g] = jnp.concatenate(ranks, axis=0).astype(jnp.int32)
        tb_ref[n] = base.astype(jnp.int32)
        base = base + jnp.sum(onehot, axis=1, keepdims=True)
    base_ref[...] = base
    cnt_ref[...] = base.astype(jnp.int32)


def _mix(attn_out, rwkv_out, x2d, mod, wo_a, wo_r, ln1, w_router, b_router, B, S, mtile, n_grp):
    D = x2d.shape[1]
    tm = mtile * n_grp
    nt = S // tm
    T = B * S
    tok = lambda b, j: (b * nt + j, 0)
    col = lambda b, j: (0, b * nt + j)
    fixed = lambda b, j: (0, 0)
    return pl.pallas_call(
        _mix_kernel,
        grid=(B, nt),
        in_specs=[pl.BlockSpec((tm, ATTN_WIDTH), tok),
                  pl.BlockSpec((tm, RWKV_WIDTH), tok),
                  pl.BlockSpec((tm, D), tok),
                  pl.BlockSpec((1, 6, D), lambda b, j: (b, 0, 0)),
                  pl.BlockSpec((ATTN_WIDTH, D), fixed),
                  pl.BlockSpec((RWKV_WIDTH, D), fixed),
                  pl.BlockSpec((2, D), fixed),
                  pl.BlockSpec((D, 2 * LANES), fixed),
                  pl.BlockSpec((1, LANES), fixed)],
        out_specs=[pl.BlockSpec((tm, D), tok),
                   pl.BlockSpec((tm, D), tok),
                   pl.BlockSpec((TOP_K, tm), col),
                   pl.BlockSpec((TOP_K, tm), col),
                   pl.BlockSpec((TOP_K, tm), col),
                   pl.BlockSpec((N_EXPERTS, LANES), fixed),
                   pl.BlockSpec((n_grp, N_EXPERTS, LANES), lambda b, j: (b * nt + j, 0, 0))],
        out_shape=[jax.ShapeDtypeStruct((T, D), F32),
                   jax.ShapeDtypeStruct((T, D), BF16),
                   jax.ShapeDtypeStruct((TOP_K, T), jnp.int32),
                   jax.ShapeDtypeStruct((TOP_K, T), F32),
                   jax.ShapeDtypeStruct((TOP_K, T), jnp.int32),
                   jax.ShapeDtypeStruct((N_EXPERTS, LANES), jnp.int32),
                   jax.ShapeDtypeStruct((T // mtile, N_EXPERTS, LANES), jnp.int32)],
        scratch_shapes=[pltpu.VMEM((N_EXPERTS, LANES), F32)],
        compiler_params=pltpu.CompilerParams(dimension_semantics=("arbitrary", "arbitrary"),
                                             vmem_limit_bytes=VMEM_LIMIT),
        name="mix",
    )(attn_out, rwkv_out, x2d, mod, wo_a, wo_r, ln1, w_router, b_router)


RUN_PIECES = tuple(2 ** b for b in range(int(math.log2(MOE_TILE)), -1, -1))
SUBLANES = 8


def _to_tiles(ref, x):
    n = x.shape[0]
    for c in range(SUBLANES):
        ref[pl.ds(c, n, stride=SUBLANES), :] = x[:, c * LANES:(c + 1) * LANES]


def _from_tiles(ref):
    n = ref.shape[0] // SUBLANES
    return jnp.concatenate([ref[pl.ds(c, n, stride=SUBLANES), :] for c in range(SUBLANES)], axis=1)


def _run_copies(n, local, local_start, remote, remote_start, sem, to_remote):
    off = 0
    for piece in RUN_PIECES:
        take = (n & piece) != 0

        @pl.when(take)
        def _(off=off, piece=piece):
            lo = pl.multiple_of((local_start + off) * SUBLANES, SUBLANES)
            ro = pl.multiple_of((remote_start + off) * SUBLANES, SUBLANES)
            loc = local.at[pl.ds(lo, piece * SUBLANES)]
            rem = remote.at[pl.ds(ro, piece * SUBLANES)]
            src, dst = (loc, rem) if to_remote else (rem, loc)
            pltpu.make_async_copy(src, dst, sem).start()

        off = off + (n & piece)


def _dispatch_kernel(tcnt_ref, lstart_ref, gstart_ref, pad_ref, pad_start_ref, n_used_ref, lpos_ref, h2_ref,
                     xs_ref, xbuf, zbuf, sems):
    i = pl.program_id(0)
    tm = h2_ref.shape[0]
    n_loc = TOP_K * tm
    n_blocks = xs_ref.shape[0] // (MOE_BLOCK * SUBLANES)
    zero_sem = sems.at[2]

    @pl.when(i == 0)
    def _():
        zbuf[...] = jnp.zeros_like(zbuf)

        def zero_pad(e, carry):
            _run_copies(pad_ref[e], zbuf, 0, xs_ref, pad_start_ref[e], zero_sem, True)
            return carry

        def zero_tail(b, carry):
            @pl.when(b >= n_used_ref[0])
            def _():
                start = pl.multiple_of(b * (MOE_BLOCK * SUBLANES), MOE_BLOCK * SUBLANES)
                pltpu.make_async_copy(zbuf, xs_ref.at[pl.ds(start, MOE_BLOCK * SUBLANES)], zero_sem).start()
            return carry

        lax.fori_loop(0, N_EXPERTS, zero_pad, 0)
        lax.fori_loop(n_blocks - N_EXPERTS, n_blocks, zero_tail, 0)

    slot = lax.broadcasted_iota(jnp.int32, (n_loc, tm), 0)
    lpos = lpos_ref[...]
    perm = jnp.zeros((n_loc, tm), F32)
    for k in range(TOP_K):
        perm = jnp.where(slot == lpos[k:k + 1, :], 1.0, perm)
    perm = perm.astype(BF16)

    def wait_tile(s):
        pltpu.make_async_copy(xbuf.at[s], xs_ref.at[pl.ds(0, n_loc * SUBLANES)], sems.at[s]).wait()

    cur = i % 2

    @pl.when(i >= 2)
    def _():
        wait_tile(cur)

    _to_tiles(xbuf.at[cur], _dot(perm, h2_ref[...]))

    def issue(e, carry):
        idx = i * N_EXPERTS + e
        _run_copies(tcnt_ref[idx], xbuf.at[cur], lstart_ref[idx], xs_ref, gstart_ref[idx], sems.at[cur], True)
        return carry

    lax.fori_loop(0, N_EXPERTS, issue, 0)

    @pl.when(i == pl.num_programs(0) - 1)
    def _():
        wait_tile(cur)

        @pl.when(i >= 1)
        def _():
            wait_tile(1 - cur)

        n_zero = N_EXPERTS * MOE_BLOCK * SUBLANES
        pltpu.make_async_copy(xs_ref.at[pl.ds(0, n_zero)], xs_ref.at[pl.ds(0, n_zero)], zero_sem).wait()


def _dispatch(tcnt, lstart, gstart, pad, pad_start, n_used, lpos, h2, n_rows, tm):
    T, D = h2.shape
    grid_spec = pltpu.PrefetchScalarGridSpec(
        num_scalar_prefetch=6,
        grid=(T // tm,),
        in_specs=[pl.BlockSpec((TOP_K, tm), lambda i, *_: (0, i)),
                  pl.BlockSpec((tm, D), lambda i, *_: (i, 0))],
        out_specs=pl.BlockSpec(memory_space=pl.ANY),
        scratch_shapes=[pltpu.VMEM((2, TOP_K * tm * SUBLANES, LANES), F32),
                        pltpu.VMEM((MOE_BLOCK * SUBLANES, LANES), F32),
                        pltpu.SemaphoreType.DMA((3,))],
    )
    return pl.pallas_call(
        _dispatch_kernel,
        grid_spec=grid_spec,
        out_shape=jax.ShapeDtypeStruct((n_rows * SUBLANES, LANES), F32),
        compiler_params=pltpu.CompilerParams(dimension_semantics=("arbitrary",),
                                             vmem_limit_bytes=VMEM_LIMIT),
        name="dispatch",
    )(tcnt, lstart, gstart, pad, pad_start, n_used, lpos, h2)


def _experts_kernel(blk_e_ref, n_used_ref, next_e_ref, xs_ref, wgu_hbm, bgu_ref, wd_hbm, bd_ref, ys_ref,
                    wgu_f32, wd_f32, wgu_bf, wd_bf, sems):
    i = pl.program_id(0)
    d_ff = wd_bf.shape[0]
    used = i < n_used_ref[0]
    e = blk_e_ref[i]
    new_expert = (i == 0) | (e != blk_e_ref[jnp.maximum(i - 1, 0)])

    def weight_copies(ex):
        return (pltpu.make_async_copy(wgu_hbm.at[ex], wgu_f32, sems.at[0]),
                pltpu.make_async_copy(wd_hbm.at[ex], wd_f32, sems.at[1]))

    @pl.when(i == 0)
    def _():
        for cp in weight_copies(e):
            cp.start()

    @pl.when(used & new_expert)
    def _():
        for cp in weight_copies(e):
            cp.wait()
        wgu_bf[...] = wgu_f32[...].astype(BF16)
        wd_bf[...] = wd_f32[...].astype(BF16)
        nxt = next_e_ref[i]

        @pl.when(nxt >= 0)
        def _():
            for cp in weight_copies(nxt):
                cp.start()

    @pl.when(used)
    def _():
        xb = _from_tiles(xs_ref).astype(BF16)
        gu = _dot(xb, wgu_bf[...]) + bgu_ref[0]
        gate = jnp.minimum(gu[:, :d_ff], SWIGLU_LIMIT)
        up = jnp.clip(gu[:, d_ff:], -SWIGLU_LIMIT, SWIGLU_LIMIT)
        act = (up + 1.0) * (gate * _sigmoid(SWIGLU_ALPHA * gate))
        _to_tiles(ys_ref, _dot(act.astype(BF16), wd_bf[...]) + bd_ref[0])

    @pl.when(i >= n_used_ref[0])
    def _():
        ys_ref[...] = jnp.zeros_like(ys_ref)


def _experts(blk_e, n_used, next_e, xs, wgu, bgu, wd, bd):
    d_ff, D = wd.shape[1], wd.shape[2]
    n_blocks = xs.shape[0] // (MOE_BLOCK * SUBLANES)
    blk = (MOE_BLOCK * SUBLANES, LANES)

    def last_used(i, n_used_ref):
        return jnp.minimum(i, jnp.maximum(n_used_ref[0] - 1, 0))

    def row_map(i, blk_e_ref, n_used_ref, next_e_ref):
        return (last_used(i, n_used_ref), 0)

    def exp_map(i, blk_e_ref, n_used_ref, next_e_ref):
        return (blk_e_ref[last_used(i, n_used_ref)], 0, 0)

    grid_spec = pltpu.PrefetchScalarGridSpec(
        num_scalar_prefetch=3,
        grid=(n_blocks,),
        in_specs=[pl.BlockSpec(blk, row_map),
                  pl.BlockSpec(memory_space=pl.ANY),
                  pl.BlockSpec((1, 1, 2 * d_ff), exp_map),
                  pl.BlockSpec(memory_space=pl.ANY),
                  pl.BlockSpec((1, 1, D), exp_map)],
        out_specs=pl.BlockSpec(blk, lambda i, *_: (i, 0)),
        scratch_shapes=[pltpu.VMEM((D, 2 * d_ff), F32), pltpu.VMEM((d_ff, D), F32),
                        pltpu.VMEM((D, 2 * d_ff), BF16), pltpu.VMEM((d_ff, D), BF16),
                        pltpu.SemaphoreType.DMA((2,))],
    )
    return pl.pallas_call(
        _experts_kernel,
        grid_spec=grid_spec,
        out_shape=jax.ShapeDtypeStruct(xs.shape, F32),
        compiler_params=pltpu.CompilerParams(dimension_semantics=("arbitrary",),
                                             vmem_limit_bytes=VMEM_LIMIT),
        name="experts",
    )(blk_e, n_used, next_e, xs, wgu, bgu, wd, bd)


def _combine_kernel(tcnt_ref, lstart_ref, gstart_ref, ys_ref, lpos_ref, gt_ref, x1_ref, mod_ref, ln_ref,
                    o_ref, buf, sems):
    nt = pl.num_programs(1)
    n_tiles = pl.num_programs(0) * nt
    i = pl.program_id(0) * nt + pl.program_id(1)
    tm = x1_ref.shape[0]
    n_loc = TOP_K * tm

    def fetch(tile, s):
        def issue(e, carry):
            idx = tile * N_EXPERTS + e
            _run_copies(tcnt_ref[idx], buf.at[s], lstart_ref[idx], ys_ref, gstart_ref[idx], sems.at[s], False)
            return carry

        lax.fori_loop(0, N_EXPERTS, issue, 0)

    @pl.when(i == 0)
    def _():
        fetch(i, 0)

    @pl.when(i + 1 < n_tiles)
    def _():
        fetch(i + 1, (i + 1) % 2)

    slot = lax.broadcasted_iota(jnp.int32, (n_loc, tm), 0)
    lpos = lpos_ref[...]
    gt = gt_ref[...]
    pick = jnp.zeros((n_loc, tm), F32)
    for k in range(TOP_K):
        pick = jnp.where(slot == lpos[k:k + 1, :], gt[k:k + 1, :], pick)
    cur = i % 2
    pltpu.make_async_copy(ys_ref.at[pl.ds(0, n_loc * SUBLANES)], buf.at[cur], sems.at[cur]).wait()
    y = _dot_tn(pick.astype(BF16), _from_tiles(buf.at[cur]).astype(BF16))
    z = DEEPNORM_ALPHA * x1_ref[...] + (1.0 + mod_ref[0][5:6, :]) * y
    o_ref[...] = _layer_norm(z) * ln_ref[0:1, :] + ln_ref[1:2, :]


def _combine(tcnt, lstart, gstart, ys, lpos, gates, x1, mod, ln2, B, S, tm):
    T, D = x1.shape
    nt = S // tm
    tok = lambda b, j, *_: (b * nt + j, 0)
    col = lambda b, j, *_: (0, b * nt + j)
    grid_spec = pltpu.PrefetchScalarGridSpec(
        num_scalar_prefetch=3,
        grid=(B, nt),
        in_specs=[pl.BlockSpec(memory_space=pl.ANY),
                  pl.BlockSpec((TOP_K, tm), col),
                  pl.BlockSpec((TOP_K, tm), col),
                  pl.BlockSpec((tm, D), tok),
                  pl.BlockSpec((1, 6, D), lambda b, j, *_: (b, 0, 0)),
                  pl.BlockSpec((2, D), lambda b, j, *_: (0, 0))],
        out_specs=pl.BlockSpec((tm, D), tok),
        scratch_shapes=[pltpu.VMEM((2, TOP_K * tm * SUBLANES, LANES), F32),
                        pltpu.SemaphoreType.DMA((2,))],
    )
    return pl.pallas_call(
        _combine_kernel,
        grid_spec=grid_spec,
        out_shape=jax.ShapeDtypeStruct((T, D), F32),
        compiler_params=pltpu.CompilerParams(dimension_semantics=("arbitrary", "arbitrary"),
                                             vmem_limit_bytes=VMEM_LIMIT),
        name="combine",
    )(tcnt, lstart, gstart, ys, lpos, gates, x1, mod, ln2)


def _pad_rows(w, rows):
    return jnp.pad(w, ((0, rows - w.shape[0]), (0, 0)))


def _pad_cols(w, cols):
    return jnp.pad(w, ((0, 0), (0, cols - w.shape[1])))


def _layer(x, c, positions, w_ada, b_ada, w_in, shift_mu, rwkv_w0, rwkv_w2, rwkv_a0, rwkv_a2, rwkv_g2,
           rwkv_k_k, rwkv_k_a, rwkv_r_k, rwkv_ln_w, rwkv_ln_b, attn_sinks, w_out, ln1_g, ln1_b,
           w_router, b_router, w_gate_up, b_gate_up, w_down, b_down, ln2_g, ln2_b):
    B, S, D = x.shape
    T = B * S

    q0, k0, v0 = 0, ATTN_WIDTH, ATTN_WIDTH + KV_WIDTH
    r0 = ATTN_WIDTH + 2 * KV_WIDTH
    heads = lambda base: [w_in[:, base + h * HEAD_DIM: base + (h + 1) * HEAD_DIM] for h in range(N_KV_HEADS)]
    dup = lambda hs: [w for w in hs for _ in range(2)]
    w_attn = jnp.concatenate([w_in[:, q0:q0 + ATTN_WIDTH]] + dup(heads(k0)) + dup(heads(v0)), axis=1).astype(BF16)
    lora0 = r0 + 3 * RWKV_WIDTH
    lora = (DECAY_LORA, AAA_LORA, GATE_LORA)
    pieces_w = [w_in[:, r0:lora0]]
    pieces_mu = [shift_mu[None, 0:3 * RWKV_WIDTH]]
    off = lora0
    for n in lora:
        pieces_w.append(_pad_cols(w_in[:, off:off + n], LANES))
        pieces_mu.append(_pad_cols(shift_mu[None, off - r0:off - r0 + n], LANES))
        off += n
    w_rwkv = jnp.concatenate(pieces_w, axis=1).astype(BF16)
    mu = jnp.concatenate(pieces_mu, axis=1)
    inv_freq = ROPE_THETA ** (-jnp.arange(0, ROT_DIM, 2, dtype=F32) / ROT_DIM)
    lane_p = jnp.arange(LANES) % HEAD_DIM
    n_freq = ROT_DIM // 2
    rot_tab = jnp.zeros((8, LANES), F32)
    rot_tab = rot_tab.at[0].set(jnp.where(lane_p < ROT_DIM, 0.0, 1.0))
    rot_tab = rot_tab.at[1].set(jnp.where(lane_p < n_freq, -1.0, 0.0))
    rot_tab = rot_tab.at[2].set(jnp.where((lane_p >= n_freq) & (lane_p < ROT_DIM), 1.0, 0.0))
    freq_tab = jnp.broadcast_to(inv_freq[:, None], (n_freq, LANES))
    lane_freq = (jnp.arange(n_freq)[:, None] == (lane_p % n_freq)[None, :]) & (lane_p < ROT_DIM)[None, :]
    zeros = jnp.zeros_like(lane_freq)
    expand = jnp.concatenate([jnp.concatenate([lane_freq, zeros], axis=1),
                              jnp.concatenate([zeros, lane_freq], axis=1)], axis=0).astype(BF16)
    vecs = jnp.stack([rwkv_w0, rwkv_a0, rwkv_k_k, rwkv_k_a, rwkv_r_k.reshape(-1), rwkv_ln_w, rwkv_ln_b,
                      jnp.zeros_like(rwkv_w0)])
    w2 = _pad_rows(rwkv_w2, LANES).astype(BF16)
    a2 = _pad_rows(rwkv_a2, LANES).astype(BF16)
    g2 = _pad_rows(rwkv_g2, LANES).astype(BF16)
    wo_a = w_out[:ATTN_WIDTH].astype(BF16)
    wo_r = w_out[ATTN_WIDTH:].astype(BF16)
    w_r_hi = w_router.astype(BF16)
    w_r_lo = (w_router - w_r_hi.astype(F32)).astype(BF16)
    w_r = jnp.concatenate([_pad_cols(w_r_hi, LANES), _pad_cols(w_r_lo, LANES)], axis=1)
    b_r = jnp.concatenate([b_router, jnp.full((LANES - N_EXPERTS,), NEG_INF, F32)])[None, :]

    mod = _mod(c, w_ada, b_ada).reshape(B, 6, D)
    attn_out, rw = _inproj(x, positions, mod, w_attn, w_rwkv, mu, rot_tab, freq_tab, expand, attn_sinks,
                           min(INPROJ_TILE, S))
    rwkv_out = _rwkv(rw, vecs, w2, a2, g2, B, S, min(RWKV_STEP, S))

    mtile = min(MOE_TILE, S)
    x1, h2, top_i, gates, rank, cnt, tbase = _mix(attn_out, rwkv_out, x.reshape(T, D), mod, wo_a, wo_r,
                                                   jnp.stack([ln1_g, ln1_b]), w_r, b_r, B, S, mtile,
                                                   max(1, min(MIX_TILES, S // mtile)))

    counts = cnt[:, 0]
    padded = (counts + MOE_BLOCK - 1) // MOE_BLOCK * MOE_BLOCK
    pend = jnp.cumsum(padded)
    pstart = pend - padded
    n_blocks = T * TOP_K // MOE_BLOCK + N_EXPERTS
    blk_row = jnp.arange(n_blocks, dtype=jnp.int32) * MOE_BLOCK
    blk_e = jnp.minimum(jnp.sum((pend[None, :] <= blk_row[:, None]).astype(jnp.int32), axis=1), N_EXPERTS - 1)
    n_used = (pend[-1:] // MOE_BLOCK).astype(jnp.int32)
    tb = tbase[:, :, 0]
    tcnt = jnp.concatenate([tb[1:], counts[None]], axis=0) - tb
    lstart = jnp.cumsum(tcnt, axis=1) - tcnt
    gstart = pstart[None, :] + tb
    shift = jnp.repeat(jnp.transpose(lstart - tb), mtile, axis=1)
    experts = jnp.arange(N_EXPERTS, dtype=jnp.int32)
    lpos = rank + jnp.sum(jnp.where(top_i[None] == experts[:, None, None], shift[:, None, :], 0), axis=0)
    flat = lambda a: a.reshape(-1).astype(jnp.int32)

    xs = _dispatch(flat(tcnt), flat(lstart), flat(gstart), flat(padded - counts), flat(pstart + counts), n_used,
                   lpos, h2, n_blocks * MOE_BLOCK, mtile)
    later_with_rows = (experts[None, :] > experts[:, None]) & (counts[None, :] > 0)
    next_of = jnp.min(jnp.where(later_with_rows, experts[None, :], N_EXPERTS), axis=1)
    next_of = jnp.where(next_of < N_EXPERTS, next_of, -1)
    next_e = jnp.sum(jnp.where(blk_e[:, None] == experts[None, :], next_of[None, :], 0), axis=1).astype(jnp.int32)
    ys = _experts(blk_e, n_used, next_e, xs, w_gate_up, b_gate_up[:, None, :], w_down, b_down[:, None, :])
    out = _combine(flat(tcnt), flat(lstart), flat(gstart), ys, lpos, gates, x1, mod, jnp.stack([ln2_g, ln2_b]),
                   B, S, mtile)
    return out.reshape(B, S, D)


def kernel(x, c, positions, w_ada, b_ada, w_in, shift_mu, rwkv_w0, rwkv_w2, rwkv_a0, rwkv_a2, rwkv_g2,
           rwkv_k_k, rwkv_k_a, rwkv_r_k, rwkv_ln_w, rwkv_ln_b, attn_sinks, w_out, ln1_g, ln1_b,
           w_router, b_router, w_gate_up, b_gate_up, w_down, b_down, ln2_g, ln2_b):
    for l in range(DEPTH):
        x = _layer(x, c, positions, w_ada[l], b_ada[l], w_in[l], shift_mu[l], rwkv_w0[l], rwkv_w2[l],
                   rwkv_a0[l], rwkv_a2[l], rwkv_g2[l], rwkv_k_k[l], rwkv_k_a[l], rwkv_r_k[l], rwkv_ln_w[l],
                   rwkv_ln_b[l], attn_sinks[l], w_out[l], ln1_g[l], ln1_b[l], w_router[l], b_router[l],
                   w_gate_up[l], b_gate_up[l], w_down[l], b_down[l], ln2_g[l], ln2_b[l])
    return x
```

```python
import functools
import math

import jax
import jax.numpy as jnp
from jax import lax
from jax.experimental import pallas as pl
from jax.experimental.pallas import tpu as pltpu

F32 = jnp.float32
BF16 = jnp.bfloat16

HEAD_DIM = 64
N_ATTN_HEADS = 8
N_KV_HEADS = 2
N_RWKV_HEADS = 8
ATTN_WIDTH = N_ATTN_HEADS * HEAD_DIM
KV_WIDTH = N_KV_HEADS * HEAD_DIM
RWKV_WIDTH = N_RWKV_HEADS * HEAD_DIM
ATTN_BLOCK = 128
ROT_DIM = HEAD_DIM // 4
ROPE_THETA = 500000.0
DECAY_LORA = 32
AAA_LORA = 32
GATE_LORA = 96
N_EXPERTS = 32
TOP_K = 4
SWIGLU_LIMIT = 7.0
SWIGLU_ALPHA = 1.702
LN_EPS = 1e-5
RWKV_GN_EPS = 64e-5
NEG_INF = -1e30
DEPTH = 1
DEEPNORM_ALPHA = (2 * DEPTH) ** 0.25

LANES = 128
RWKV_CHUNK = 64
RWKV_STEP = 512
INPROJ_TILE = 512
RWKV_PROJ_CHUNK = 512
RWKV_PROJ_EARLY = 2
MOE_BLOCK = 512
MOE_TILE = 256
MIX_TILES = 4
ATTN_PROJ = ATTN_WIDTH + 4 * KV_WIDTH
RWKV_PROJ = 3 * RWKV_WIDTH + 3 * LANES
VMEM_LIMIT = 48 * 1024 * 1024


def _dot(a, b):
    return jnp.dot(a, b, preferred_element_type=F32)


def _dot_nt(a, b):
    return lax.dot_general(a, b, (((1,), (1,)), ((), ())), preferred_element_type=F32)


def _dot_tn(a, b):
    return lax.dot_general(a, b, (((0,), (0,)), ((), ())), preferred_element_type=F32)


def _split3(x):
    h = x.astype(BF16)
    r1 = x - h.astype(F32)
    m = r1.astype(BF16)
    lo = (r1 - m.astype(F32)).astype(BF16)
    return h, m, lo


def _dot_exact_lhs(m_bf16, x):
    h, m, lo = _split3(x)
    return _dot(m_bf16, h) + _dot(m_bf16, m) + _dot(m_bf16, lo)


def _layer_norm(x):
    mu = jnp.mean(x, axis=-1, keepdims=True)
    xc = x - mu
    var = jnp.mean(xc * xc, axis=-1, keepdims=True)
    return xc * lax.rsqrt(var + LN_EPS)


def _sigmoid(x):
    return 1.0 / (1.0 + jnp.exp(-x))


def _mod_kernel(c_ref, w_ref, b_ref, o_ref):
    c = c_ref[...]
    s = c * _sigmoid(c)
    o_ref[...] = jnp.dot(s, w_ref[...], preferred_element_type=F32,
                         precision=lax.Precision.HIGHEST) + b_ref[...]


def _mod(c, w_ada, b_ada):
    B, D = c.shape
    n = w_ada.shape[1] // D
    return pl.pallas_call(
        _mod_kernel,
        grid=(n,),
        in_specs=[pl.BlockSpec((B, D), lambda i: (0, 0)),
                  pl.BlockSpec((D, D), lambda i: (0, i)),
                  pl.BlockSpec((1, D), lambda i: (0, i))],
        out_specs=pl.BlockSpec((B, D), lambda i: (0, i)),
        out_shape=jax.ShapeDtypeStruct((B, n * D), F32),
        compiler_params=pltpu.CompilerParams(dimension_semantics=("arbitrary",),
                                             vmem_limit_bytes=VMEM_LIMIT),
        name="mod",
    )(c, w_ada, b_ada.reshape(1, -1))


def _inproj_kernel(x_ref, pos_ref, mod_ref, wa_ref, wr_ref, mu_ref, rt_ref, fq_ref, ex_ref, sink_ref,
                   at_ref, rw_ref, qkv_ref, kv_prev_ref, carry_ref):
    j = pl.program_id(1)

    @pl.when(j == 0)
    def _():
        kv_prev_ref[...] = jnp.zeros_like(kv_prev_ref)

    x = x_ref[0]
    tm = x.shape[0]
    mod = mod_ref[0]
    h = _layer_norm(x) * (1.0 + mod[1:2, :]) + mod[0:1, :]
    hb = h.astype(BF16)

    pa = _dot(hb, wa_ref[...])
    ang = pos_ref[0].astype(F32) * fq_ref[:, 0:1]
    pieces = _split3(jnp.concatenate([jnp.cos(ang), jnp.sin(ang)], axis=0))
    trig = _dot_tn(pieces[0], ex_ref[...]) + _dot_tn(pieces[1], ex_ref[...]) + _dot_tn(pieces[2], ex_ref[...])
    cs = trig[:, 0:LANES] + rt_ref[0:1, :]
    sn = trig[:, LANES:2 * LANES]
    m_lo = rt_ref[1:2, :]
    m_hi = rt_ref[2:3, :]
    n_q = ATTN_WIDTH // LANES
    n_rot = (ATTN_WIDTH + 2 * KV_WIDTH) // LANES

    row = lax.broadcasted_iota(jnp.int32, (tm, 1), 0)

    def rwkv_columns(c0, c1):
        def emit():
            pr = _dot(hb, wr_ref[:, c0:c1])
            prev = pltpu.roll(pr, 1, 0)
            carry = jnp.where(j == 0, 0.0, carry_ref[:, c0:c1])
            prev = jnp.where(row == 0, carry, prev)
            carry_ref[:, c0:c1] = pr[tm - 1:tm, :]
            rw_ref[:, c0:c1] = pr + (prev - pr) * mu_ref[:, c0:c1]
        return emit

    bounds = list(range(0, RWKV_PROJ, RWKV_PROJ_CHUNK)) + [RWKV_PROJ]
    rwkv_chunks = [rwkv_columns(c0, c1) for c0, c1 in zip(bounds[:-1], bounds[1:])]
    for chunk in rwkv_chunks[:RWKV_PROJ_EARLY]:
        chunk()
    for ch in range(n_rot):
        t = pa[:, ch * LANES:(ch + 1) * LANES]
        if ch < n_q:
            t = t * (1.0 / math.sqrt(HEAD_DIM))
        up = pltpu.roll(t, LANES - ROT_DIM // 2, 1)
        dn = pltpu.roll(t, ROT_DIM // 2, 1)
        o = t * cs + sn * (m_lo * up + m_hi * dn)
        qkv_ref[:, ch * LANES:(ch + 1) * LANES] = o.astype(BF16)
    qkv_ref[:, n_rot * LANES:] = pa[:, n_rot * LANES:].astype(BF16)

    kv_w = 2 * KV_WIDTH
    _attn_body(qkv_ref.at[:, pl.ds(0, ATTN_WIDTH)], qkv_ref.at[:, pl.ds(ATTN_WIDTH, kv_w)],
               kv_prev_ref.at[:, pl.ds(0, kv_w)], qkv_ref.at[:, pl.ds(ATTN_WIDTH + kv_w, kv_w)],
               kv_prev_ref.at[:, pl.ds(kv_w, kv_w)], sink_ref, at_ref, j > 0, interleave=rwkv_chunks[RWKV_PROJ_EARLY:])
    kv_prev_ref[...] = qkv_ref[tm - ATTN_BLOCK:tm, ATTN_WIDTH:]


def _inproj(x, positions, mod, w_attn, w_rwkv, mu, rot_tab, freq_tab, expand, sinks, tm):
    B, S, D = x.shape
    nt = S // tm
    return pl.pallas_call(
        _inproj_kernel,
        grid=(B, nt),
        in_specs=[pl.BlockSpec((1, tm, D), lambda b, j: (b, j, 0)),
                  pl.BlockSpec((1, 1, tm), lambda b, j: (b, 0, j)),
                  pl.BlockSpec((1, 6, D), lambda b, j: (b, 0, 0)),
                  pl.BlockSpec((D, ATTN_PROJ), lambda b, j: (0, 0)),
                  pl.BlockSpec((D, RWKV_PROJ), lambda b, j: (0, 0)),
                  pl.BlockSpec((1, RWKV_PROJ), lambda b, j: (0, 0)),
                  pl.BlockSpec((8, LANES), lambda b, j: (0, 0)),
                  pl.BlockSpec(freq_tab.shape, lambda b, j: (0, 0)),
                  pl.BlockSpec(expand.shape, lambda b, j: (0, 0)),
                  pl.BlockSpec(memory_space=pltpu.SMEM)],
        out_specs=[pl.BlockSpec((tm, ATTN_WIDTH), lambda b, j: (b * nt + j, 0)),
                   pl.BlockSpec((tm, RWKV_PROJ), lambda b, j: (b * nt + j, 0))],
        out_shape=[jax.ShapeDtypeStruct((B * S, ATTN_WIDTH), BF16),
                   jax.ShapeDtypeStruct((B * S, RWKV_PROJ), F32)],
        scratch_shapes=[pltpu.VMEM((tm, ATTN_PROJ), BF16),
                        pltpu.VMEM((ATTN_BLOCK, ATTN_PROJ - ATTN_WIDTH), BF16),
                        pltpu.VMEM((1, RWKV_PROJ), F32)],
        compiler_params=pltpu.CompilerParams(dimension_semantics=("arbitrary", "arbitrary"),
                                             vmem_limit_bytes=VMEM_LIMIT),
        name="inproj",
    )(x, positions.reshape(B, 1, S), mod, w_attn, w_rwkv, mu, rot_tab, freq_tab, expand, sinks)


def _attn_body(q_ref, kc_ref, kp_ref, vc_ref, vp_ref, sink_ref, o_ref, has_prev, interleave=()):
    blk = ATTN_BLOCK
    n_sub = q_ref.shape[0] // blk
    qi = lax.broadcasted_iota(jnp.int32, (blk, 2 * blk), 0)
    kj = lax.broadcasted_iota(jnp.int32, (blk, 2 * blk), 1)
    band = (kj > qi) & (kj <= qi + blk)
    first = band & ((kj >= blk) | has_prev)
    lane = lax.broadcasted_iota(jnp.int32, (1, LANES), 1)
    lo = (lane < HEAD_DIM).astype(BF16)
    hi = (lane >= HEAD_DIM).astype(BF16)
    halves = {}
    for u in range(n_sub):
        for g in range(N_KV_HEADS):
            sl = slice(g * LANES, (g + 1) * LANES)
            prev_k = kp_ref[:, sl] if u == 0 else kc_ref[(u - 1) * blk:u * blk, sl]
            prev_v = vp_ref[:, sl] if u == 0 else vc_ref[(u - 1) * blk:u * blk, sl]
            kcat = jnp.concatenate([prev_k, kc_ref[u * blk:(u + 1) * blk, sl]], axis=0)
            vcat = jnp.concatenate([prev_v, vc_ref[u * blk:(u + 1) * blk, sl]], axis=0)
            halves[u, g] = ((kcat * lo, vcat * lo), (kcat * hi, vcat * hi))
    units = [(u, c, half) for u in range(n_sub) for c in range(ATTN_WIDTH // LANES) for half in range(2)]
    scores = [_dot_nt(q_ref[u * blk:(u + 1) * blk, c * LANES:(c + 1) * LANES], halves[u, c // 2][half][0])
              for u, c, half in units]
    probs, denoms = [], []
    every = max(1, len(units) // max(1, len(interleave)))
    for n, ((u, c, half), s) in enumerate(zip(units, scores)):
        if n % every == 0 and n // every < len(interleave):
            interleave[n // every]()
        sink = sink_ref[2 * c + half]
        s = jnp.where(first if u == 0 else band, s, NEG_INF)
        m = jnp.maximum(jnp.max(s, axis=-1, keepdims=True), sink)
        p = jnp.exp(s - m)
        denoms.append(jnp.sum(p, axis=-1, keepdims=True) + jnp.exp(sink - m))
        probs.append(p.astype(BF16))
    outs = [_dot(p, halves[u, c // 2][half][1]) / d for (u, c, half), p, d in zip(units, probs, denoms)]
    for n, (u, c, half) in enumerate(units):
        if half == 0:
            o_ref[u * blk:(u + 1) * blk, c * LANES:(c + 1) * LANES] = (outs[n] + outs[n + 1]).astype(BF16)


def _rwkv_kernel(rw_ref, vec_ref, w2_ref, a2_ref, g2_ref, o_ref, state_ref, *, n_chunk):
    j = pl.program_id(1)
    C = RWKV_CHUNK
    W = RWKV_WIDTH
    n_pair = W // LANES

    @pl.when(j == 0)
    def _():
        state_ref[...] = jnp.zeros_like(state_ref)

    w0 = vec_ref[0:1, :]
    a0 = vec_ref[1:2, :]
    k_k = vec_ref[2:3, :]
    k_a = vec_ref[3:4, :]
    r_k = vec_ref[4:5, :]
    ln_w = vec_ref[5:6, :]
    ln_b = vec_ref[6:7, :]

    r = rw_ref[:, 0:W]
    k = rw_ref[:, W:2 * W]
    v = rw_ref[:, 2 * W:3 * W]
    wl = rw_ref[:, 3 * W:3 * W + LANES]
    al = rw_ref[:, 3 * W + LANES:3 * W + 2 * LANES]
    gl = rw_ref[:, 3 * W + 2 * LANES:3 * W + 3 * LANES]

    ri = lax.broadcasted_iota(jnp.int32, (LANES, LANES), 0)
    ci = lax.broadcasted_iota(jnp.int32, (LANES, LANES), 1)
    same = (ri // HEAD_DIM) == (ci // HEAD_DIM)
    strict = same & ((ri % HEAD_DIM) > (ci % HEAD_DIM))
    incl = same & ((ri % HEAD_DIM) >= (ci % HEAD_DIM))
    lane = lax.broadcasted_iota(jnp.int32, (1, LANES), 1)
    m0 = (lane < HEAD_DIM).astype(F32)
    m1 = 1.0 - m0
    tri = (lax.broadcasted_iota(jnp.int32, (C, C), 0) >= lax.broadcasted_iota(jnp.int32, (C, C), 1)).astype(BF16)

    def head_sum(xv):
        outs = []
        for p in range(n_pair):
            xp = xv[:, p * LANES:(p + 1) * LANES]
            s0 = jnp.sum(xp * m0, axis=1, keepdims=True)
            s1 = jnp.sum(xp * m1, axis=1, keepdims=True)
            outs.append(s0 * m0 + s1 * m1)
        return jnp.concatenate(outs, axis=1)

    def stack2(xp):
        return jnp.concatenate([xp * m0, xp * m1], axis=0)

    z = w0 + _dot(jnp.tanh(wl).astype(BF16), w2_ref[...])
    lw = -math.exp(-0.5) * _sigmoid(z)
    a = _sigmoid(a0 + _dot(al.astype(BF16), a2_ref[...]))
    g = _dot(_sigmoid(gl).astype(BF16), g2_ref[...])
    kk = k * k_k
    kkn = kk / jnp.maximum(jnp.sqrt(head_sum(kk * kk)), 1e-12)
    k2 = k * (1.0 + (a - 1.0) * k_a)
    av = -kkn
    bv = kkn * a
    bonus = head_sum(r * k2 * r_k) * v

    eye = (ri == ci).astype(F32)
    bf = lambda t: t.astype(BF16)

    pre = []
    for c in range(n_chunk):
        rows = slice(c * C, (c + 1) * C)
        lwc = lw[rows]
        cw = _dot_exact_lhs(tri, lwc)
        cwl = cw[C - 1:C, :]
        e_in = jnp.exp(cw)
        e_neg = jnp.exp(-cw)
        e_rem = jnp.exp(cwl - cw)
        wc = jnp.exp(cwl)
        Rt = r[rows] * e_in
        At = av[rows] * jnp.exp(cw - lwc)
        Bb = bv[rows] * e_neg
        Kb = k2[rows] * e_neg
        Bh = bv[rows] * e_rem
        Kh = k2[rows] * e_rem
        vc = v[rows]
        for p in range(n_pair):
            sl = slice(p * LANES, (p + 1) * LANES)
            pre.append(dict(At=At[:, sl], Rt=Rt[:, sl], Bb=Bb[:, sl], Kb=Kb[:, sl], Bh=Bh[:, sl], Kh=Kh[:, sl],
                            v=vc[:, sl], wc=wc[:, sl]))

    for u in pre:
        u["at_bd"] = stack2(u["At"])
        lhs = bf(jnp.concatenate([u["at_bd"], stack2(u["Rt"])], axis=0))
        rhs = bf(jnp.concatenate([stack2(u["Bb"]), stack2(u["Kb"])], axis=0))
        u["G"] = _dot_nt(lhs, rhs)
    for u in pre:
        G = u.pop("G")
        u["a_ab"] = jnp.where(strict, G[0:2 * C, 0:2 * C], 0.0)
        u["a_ak"] = bf(jnp.where(strict, G[0:2 * C, 2 * C:4 * C], 0.0))
        u["a_rb"] = bf(jnp.where(incl, G[2 * C:4 * C, 0:2 * C], 0.0))
        u["a_rk"] = bf(jnp.where(incl, G[2 * C:4 * C, 2 * C:4 * C], 0.0))
        u["v_bd"] = bf(stack2(u["v"]))
    for u in pre:
        xb = bf(u["a_ab"])
        u["P"] = eye + u.pop("a_ab")
        u["X"] = _dot(xb, xb)
        u["M0"] = _dot(u["a_ak"], u["v_bd"])
    for _ in range(int(math.log2(C)) - 2):
        for u in pre:
            xb = bf(u["X"])
            Wm = _dot(xb, jnp.concatenate([bf(u["P"]), xb], axis=1))
            u["P"] = u["P"] + Wm[:, 0:LANES]
            u["X"] = Wm[:, LANES:2 * LANES]
    for u in pre:
        u["P"] = bf(u["P"] + _dot(bf(u.pop("X")), bf(u["P"])))
    for u in pre:
        u["M1"] = _dot(u["P"], bf(u.pop("M0")))
        u["Q"] = bf(_dot(u["a_rb"], u["P"]))
        u["PtB"] = _dot_tn(u["P"], bf(stack2(u["Bh"])))
    for u in pre:
        M1 = u.pop("M1")
        u["Y0"] = _dot(jnp.concatenate([u["a_rb"], u["a_rk"]], axis=1),
                       jnp.concatenate([bf(M1), u["v_bd"]], axis=0))
        u["Tm"] = bf(_dot_tn(bf(u["at_bd"]), bf(u.pop("PtB"))))
        m1_pair = M1[0:C] + M1[C:2 * C]
        cst = _dot_tn(bf(jnp.concatenate([m1_pair, u["v"]], axis=0)),
                      bf(jnp.concatenate([u["Bh"], u["Kh"]], axis=0)))
        u["cst"] = jnp.where(same, cst, 0.0)
        u["ar"] = bf(jnp.concatenate([u["At"], u["Rt"]], axis=0))

    states = [state_ref[p] for p in range(n_pair)]
    for c in range(n_chunk):
        rows = slice(c * C, (c + 1) * C)
        us = pre[c * n_pair:(c + 1) * n_pair]
        sbs = [bf(S) for S in states]
        zs = [_dot_nt(u["ar"], sb) for u, sb in zip(us, sbs)]
        new_states = [S * u["wc"] + _dot(sb, u["Tm"]) + u["cst"] for u, S, sb in zip(us, states, sbs)]
        ybds = [stack2(Z[C:2 * C]) + _dot(u["Q"], bf(stack2(Z[0:C]))) + u["Y0"] for u, Z in zip(us, zs)]
        ys = [y_bd[0:C] + y_bd[C:2 * C] for y_bd in ybds]
        states = new_states
        y = jnp.concatenate(ys, axis=1)
        mu = head_sum(y) * (1.0 / HEAD_DIM)
        yc = y - mu
        var = head_sum(yc * yc) * (1.0 / HEAD_DIM)
        yn = yc * lax.rsqrt(var + RWKV_GN_EPS) * ln_w + ln_b
        o_ref[rows, :] = ((yn + bonus[rows]) * g[rows]).astype(BF16)
    for p in range(n_pair):
        state_ref[p] = states[p]


def _rwkv(rw, vecs, w2, a2, g2, B, S, lb):
    nt = S // lb
    return pl.pallas_call(
        functools.partial(_rwkv_kernel, n_chunk=lb // RWKV_CHUNK),
        grid=(B, nt),
        in_specs=[pl.BlockSpec((lb, RWKV_PROJ), lambda b, j: (b * nt + j, 0)),
                  pl.BlockSpec((8, RWKV_WIDTH), lambda b, j: (0, 0)),
                  pl.BlockSpec((LANES, RWKV_WIDTH), lambda b, j: (0, 0)),
                  pl.BlockSpec((LANES, RWKV_WIDTH), lambda b, j: (0, 0)),
                  pl.BlockSpec((LANES, RWKV_WIDTH), lambda b, j: (0, 0))],
        out_specs=pl.BlockSpec((lb, RWKV_WIDTH), lambda b, j: (b * nt + j, 0)),
        out_shape=jax.ShapeDtypeStruct((B * S, RWKV_WIDTH), BF16),
        scratch_shapes=[pltpu.VMEM((RWKV_WIDTH // LANES, LANES, LANES), F32)],
        compiler_params=pltpu.CompilerParams(dimension_semantics=("arbitrary", "arbitrary"),
                                             vmem_limit_bytes=VMEM_LIMIT),
        name="rwkv",
    )(rw, vecs, w2, a2, g2)


def _mix_kernel(at_ref, rk_ref, x_ref, mod_ref, wo_a_ref, wo_r_ref, ln_ref, wr_ref, br_ref,
                x1_ref, h2_ref, ti_ref, gt_ref, rank_ref, cnt_ref, tb_ref, base_ref):
    first = (pl.program_id(0) == 0) & (pl.program_id(1) == 0)

    @pl.when(first)
    def _():
        base_ref[...] = jnp.zeros_like(base_ref)

    mod = mod_ref[0]
    n_grp = tb_ref.shape[0]
    mt = x_ref.shape[0] // n_grp
    groups = [slice(g * mt, (g + 1) * mt) for g in range(n_grp)]
    ys = [_dot(at_ref[g, :], wo_a_ref[...]) + _dot(rk_ref[g, :], wo_r_ref[...]) for g in groups]
    logits = []
    for g, y in zip(groups, ys):
        x1 = _layer_norm(DEEPNORM_ALPHA * x_ref[g, :] + (1.0 + mod[2:3, :]) * y) * ln_ref[0:1, :] + ln_ref[1:2, :]
        h2 = _layer_norm(x1) * (1.0 + mod[4:5, :]) + mod[3:4, :]
        x1_ref[g, :] = x1
        h_hi = h2.astype(BF16)
        h2_ref[g, :] = h_hi
        h_lo = (h2 - h_hi.astype(F32)).astype(BF16)
        part = _dot(h_hi, wr_ref[...])
        logits.append(part[:, 0:LANES] + part[:, LANES:2 * LANES] + _dot(h_lo, wr_ref[:, 0:LANES]) + br_ref[...])

    erow = lax.broadcasted_iota(jnp.int32, (N_EXPERTS, mt), 0).astype(F32)
    before = (lax.broadcasted_iota(jnp.int32, (mt, mt), 0)
              < lax.broadcasted_iota(jnp.int32, (mt, mt), 1)).astype(BF16)
    base = base_ref[...]
    for n, (g, lg) in enumerate(zip(groups, logits)):
        cur = jnp.transpose(lg)[0:N_EXPERTS, :]
        vals, idxs = [], []
        for _ in range(TOP_K):
            m = jnp.max(cur, axis=0, keepdims=True)
            idx = jnp.min(jnp.where(cur == m, erow, float(N_EXPERTS)), axis=0, keepdims=True)
            vals.append(m)
            idxs.append(idx)
            cur = jnp.where(erow == idx, -jnp.inf, cur)
        tv = jnp.concatenate(vals, axis=0)
        e = jnp.exp(tv - tv[0:1, :])
        gt_ref[:, g] = e / jnp.sum(e, axis=0, keepdims=True)
        ti_ref[:, g] = jnp.concatenate(idxs, axis=0).astype(jnp.int32)

        onehot = jnp.zeros((N_EXPERTS, mt), F32)
        for idx in idxs:
            onehot = onehot + (erow == idx).astype(F32)
        tot = base[:, 0:1] + _dot(onehot.astype(BF16), before)
        ranks = [jnp.sum(jnp.where(erow == idx, tot, 0.0), axis=0, keepdims=True) for idx in idxs]
        rank_ref[:, g] = jnp.concatenate(ranks, axis=0).astype(jnp.int32)
        tb_ref[n] = base.astype(jnp.int32)
        base = base + jnp.sum(onehot, axis=1, keepdims=True)
    base_ref[...] = base
    cnt_ref[...] = base.astype(jnp.int32)


def _mix(attn_out, rwkv_out, x2d, mod, wo_a, wo_r, ln1, w_router, b_router, B, S, mtile, n_grp):
    D = x2d.shape[1]
    tm = mtile * n_grp
    nt = S // tm
    T = B * S
    tok = lambda b, j: (b * nt + j, 0)
    col = lambda b, j: (0, b * nt + j)
    fixed = lambda b, j: (0, 0)
    return pl.pallas_call(
        _mix_kernel,
        grid=(B, nt),
        in_specs=[pl.BlockSpec((tm, ATTN_WIDTH), tok),
                  pl.BlockSpec((tm, RWKV_WIDTH), tok),
                  pl.BlockSpec((tm, D), tok),
                  pl.BlockSpec((1, 6, D), lambda b, j: (b, 0, 0)),
                  pl.BlockSpec((ATTN_WIDTH, D), fixed),
                  pl.BlockSpec((RWKV_WIDTH, D), fixed),
                  pl.BlockSpec((2, D), fixed),
                  pl.BlockSpec((D, 2 * LANES), fixed),
                  pl.BlockSpec((1, LANES), fixed)],
        out_specs=[pl.BlockSpec((tm, D), tok),
                   pl.BlockSpec((tm, D), tok),
                   pl.BlockSpec((TOP_K, tm), col),
                   pl.BlockSpec((TOP_K, tm), col),
                   pl.BlockSpec((TOP_K, tm), col),
                   pl.BlockSpec((N_EXPERTS, LANES), fixed),
                   pl.BlockSpec((n_grp, N_EXPERTS, LANES), lambda b, j: (b * nt + j, 0, 0))],
        out_shape=[jax.ShapeDtypeStruct((T, D), F32),
                   jax.ShapeDtypeStruct((T, D), BF16),
                   jax.ShapeDtypeStruct((TOP_K, T), jnp.int32),
                   jax.ShapeDtypeStruct((TOP_K, T), F32),
                   jax.ShapeDtypeStruct((TOP_K, T), jnp.int32),
                   jax.ShapeDtypeStruct((N_EXPERTS, LANES), jnp.int32),
                   jax.ShapeDtypeStruct((T // mtile, N_EXPERTS, LANES), jnp.int32)],
        scratch_shapes=[pltpu.VMEM((N_EXPERTS, LANES), F32)],
        compiler_params=pltpu.CompilerParams(dimension_semantics=("arbitrary", "arbitrary"),
                                             vmem_limit_bytes=VMEM_LIMIT),
        name="mix",
    )(attn_out, rwkv_out, x2d, mod, wo_a, wo_r, ln1, w_router, b_router)


RUN_PIECES = tuple(2 ** b for b in range(int(math.log2(MOE_TILE)), -1, -1))
SUBLANES = 8


def _to_tiles(ref, x):
    n = x.shape[0]
    for c in range(SUBLANES):
        ref[pl.ds(c, n, stride=SUBLANES), :] = x[:, c * LANES:(c + 1) * LANES]


def _from_tiles(ref):
    n = ref.shape[0] // SUBLANES
    return jnp.concatenate([ref[pl.ds(c, n, stride=SUBLANES), :] for c in range(SUBLANES)], axis=1)


def _run_copies(n, local, local_start, remote, remote_start, sem, to_remote):
    off = 0
    for piece in RUN_PIECES:
        take = (n & piece) != 0

        @pl.when(take)
        def _(off=off, piece=piece):
            lo = pl.multiple_of((local_start + off) * SUBLANES, SUBLANES)
            ro = pl.multiple_of((remote_start + off) * SUBLANES, SUBLANES)
            loc = local.at[pl.ds(lo, piece * SUBLANES)]
            rem = remote.at[pl.ds(ro, piece * SUBLANES)]
            src, dst = (loc, rem) if to_remote else (rem, loc)
            pltpu.make_async_copy(src, dst, sem).start()

        off = off + (n & piece)


def _dispatch_kernel(tcnt_ref, lstart_ref, gstart_ref, pad_ref, pad_start_ref, n_used_ref, lpos_ref, h2_ref,
                     xs_ref, xbuf, zbuf, sems):
    i = pl.program_id(0)
    tm = h2_ref.shape[0]
    n_loc = TOP_K * tm
    n_blocks = xs_ref.shape[0] // (MOE_BLOCK * SUBLANES)
    zero_sem = sems.at[2]

    @pl.when(i == 0)
    def _():
        zbuf[...] = jnp.zeros_like(zbuf)

        def zero_pad(e, carry):
            _run_copies(pad_ref[e], zbuf, 0, xs_ref, pad_start_ref[e], zero_sem, True)
            return carry

        def zero_tail(b, carry):
            @pl.when(b >= n_used_ref[0])
            def _():
                start = pl.multiple_of(b * (MOE_BLOCK * SUBLANES), MOE_BLOCK * SUBLANES)
                pltpu.make_async_copy(zbuf, xs_ref.at[pl.ds(start, MOE_BLOCK * SUBLANES)], zero_sem).start()
            return carry

        lax.fori_loop(0, N_EXPERTS, zero_pad, 0)
        lax.fori_loop(n_blocks - N_EXPERTS, n_blocks, zero_tail, 0)

    slot = lax.broadcasted_iota(jnp.int32, (n_loc, tm), 0)
    lpos = lpos_ref[...]
    perm = jnp.zeros((n_loc, tm), F32)
    for k in range(TOP_K):
        perm = jnp.where(slot == lpos[k:k + 1, :], 1.0, perm)
    perm = perm.astype(BF16)

    def wait_tile(s):
        pltpu.make_async_copy(xbuf.at[s], xs_ref.at[pl.ds(0, n_loc * SUBLANES)], sems.at[s]).wait()

    cur = i % 2

    @pl.when(i >= 2)
    def _():
        wait_tile(cur)

    _to_tiles(xbuf.at[cur], _dot(perm, h2_ref[...]))

    def issue(e, carry):
        idx = i * N_EXPERTS + e
        _run_copies(tcnt_ref[idx], xbuf.at[cur], lstart_ref[idx], xs_ref, gstart_ref[idx], sems.at[cur], True)
        return carry

    lax.fori_loop(0, N_EXPERTS, issue, 0)

    @pl.when(i == pl.num_programs(0) - 1)
    def _():
        wait_tile(cur)

        @pl.when(i >= 1)
        def _():
            wait_tile(1 - cur)

        n_zero = N_EXPERTS * MOE_BLOCK * SUBLANES
        pltpu.make_async_copy(xs_ref.at[pl.ds(0, n_zero)], xs_ref.at[pl.ds(0, n_zero)], zero_sem).wait()


def _dispatch(tcnt, lstart, gstart, pad, pad_start, n_used, lpos, h2, n_rows, tm):
    T, D = h2.shape
    grid_spec = pltpu.PrefetchScalarGridSpec(
        num_scalar_prefetch=6,
        grid=(T // tm,),
        in_specs=[pl.BlockSpec((TOP_K, tm), lambda i, *_: (0, i)),
                  pl.BlockSpec((tm, D), lambda i, *_: (i, 0))],
        out_specs=pl.BlockSpec(memory_space=pl.ANY),
        scratch_shapes=[pltpu.VMEM((2, TOP_K * tm * SUBLANES, LANES), F32),
                        pltpu.VMEM((MOE_BLOCK * SUBLANES, LANES), F32),
                        pltpu.SemaphoreType.DMA((3,))],
    )
    return pl.pallas_call(
        _dispatch_kernel,
        grid_spec=grid_spec,
        out_shape=jax.ShapeDtypeStruct((n_rows * SUBLANES, LANES), F32),
        compiler_params=pltpu.CompilerParams(dimension_semantics=("arbitrary",),
                                             vmem_limit_bytes=VMEM_LIMIT),
        name="dispatch",
    )(tcnt, lstart, gstart, pad, pad_start, n_used, lpos, h2)


def _experts_kernel(blk_e_ref, n_used_ref, next_e_ref, xs_ref, wgu_hbm, bgu_ref, wd_hbm, bd_ref, ys_ref,
                    wgu_f32, wd_f32, wgu_bf, wd_bf, sems):
    i = pl.program_id(0)
    d_ff = wd_bf.shape[0]
    used = i < n_used_ref[0]
    e = blk_e_ref[i]
    new_expert = (i == 0) | (e != blk_e_ref[jnp.maximum(i - 1, 0)])

    def weight_copies(ex):
        return (pltpu.make_async_copy(wgu_hbm.at[ex], wgu_f32, sems.at[0]),
                pltpu.make_async_copy(wd_hbm.at[ex], wd_f32, sems.at[1]))

    @pl.when(i == 0)
    def _():
        for cp in weight_copies(e):
            cp.start()

    @pl.when(used & new_expert)
    def _():
        for cp in weight_copies(e):
            cp.wait()
        wgu_bf[...] = wgu_f32[...].astype(BF16)
        wd_bf[...] = wd_f32[...].astype(BF16)
        nxt = next_e_ref[i]

        @pl.when(nxt >= 0)
        def _():
            for cp in weight_copies(nxt):
                cp.start()

    @pl.when(used)
    def _():
        xb = _from_tiles(xs_ref).astype(BF16)
        gu = _dot(xb, wgu_bf[...]) + bgu_ref[0]
        gate = jnp.minimum(gu[:, :d_ff], SWIGLU_LIMIT)
        up = jnp.clip(gu[:, d_ff:], -SWIGLU_LIMIT, SWIGLU_LIMIT)
        act = (up + 1.0) * (gate * _sigmoid(SWIGLU_ALPHA * gate))
        _to_tiles(ys_ref, _dot(act.astype(BF16), wd_bf[...]) + bd_ref[0])

    @pl.when(i >= n_used_ref[0])
    def _():
        ys_ref[...] = jnp.zeros_like(ys_ref)


def _experts(blk_e, n_used, next_e, xs, wgu, bgu, wd, bd):
    d_ff, D = wd.shape[1], wd.shape[2]
    n_blocks = xs.shape[0] // (MOE_BLOCK * SUBLANES)
    blk = (MOE_BLOCK * SUBLANES, LANES)

    def last_used(i, n_used_ref):
        return jnp.minimum(i, jnp.maximum(n_used_ref[0] - 1, 0))

    def row_map(i, blk_e_ref, n_used_ref, next_e_ref):
        return (last_used(i, n_used_ref), 0)

    def exp_map(i, blk_e_ref, n_used_ref, next_e_ref):
        return (blk_e_ref[last_used(i, n_used_ref)], 0, 0)

    grid_spec = pltpu.PrefetchScalarGridSpec(
        num_scalar_prefetch=3,
        grid=(n_blocks,),
        in_specs=[pl.BlockSpec(blk, row_map),
                  pl.BlockSpec(memory_space=pl.ANY),
                  pl.BlockSpec((1, 1, 2 * d_ff), exp_map),
                  pl.BlockSpec(memory_space=pl.ANY),
                  pl.BlockSpec((1, 1, D), exp_map)],
        out_specs=pl.BlockSpec(blk, lambda i, *_: (i, 0)),
        scratch_shapes=[pltpu.VMEM((D, 2 * d_ff), F32), pltpu.VMEM((d_ff, D), F32),
                        pltpu.VMEM((D, 2 * d_ff), BF16), pltpu.VMEM((d_ff, D), BF16),
                        pltpu.SemaphoreType.DMA((2,))],
    )
    return pl.pallas_call(
        _experts_kernel,
        grid_spec=grid_spec,
        out_shape=jax.ShapeDtypeStruct(xs.shape, F32),
        compiler_params=pltpu.CompilerParams(dimension_semantics=("arbitrary",),
                                             vmem_limit_bytes=VMEM_LIMIT),
        name="experts",
    )(blk_e, n_used, next_e, xs, wgu, bgu, wd, bd)


def _combine_kernel(tcnt_ref, lstart_ref, gstart_ref, ys_ref, lpos_ref, gt_ref, x1_ref, mod_ref, ln_ref,
                    o_ref, buf, sems):
    nt = pl.num_programs(1)
    n_tiles = pl.num_programs(0) * nt
    i = pl.program_id(0) * nt + pl.program_id(1)
    tm = x1_ref.shape[0]
    n_loc = TOP_K * tm

    def fetch(tile, s):
        def issue(e, carry):
            idx = tile * N_EXPERTS + e
            _run_copies(tcnt_ref[idx], buf.at[s], lstart_ref[idx], ys_ref, gstart_ref[idx], sems.at[s], False)
            return carry

        lax.fori_loop(0, N_EXPERTS, issue, 0)

    @pl.when(i == 0)
    def _():
        fetch(i, 0)

    @pl.when(i + 1 < n_tiles)
    def _():
        fetch(i + 1, (i + 1) % 2)

    slot = lax.broadcasted_iota(jnp.int32, (n_loc, tm), 0)
    lpos = lpos_ref[...]
    gt = gt_ref[...]
    pick = jnp.zeros((n_loc, tm), F32)
    for k in range(TOP_K):
        pick = jnp.where(slot == lpos[k:k + 1, :], gt[k:k + 1, :], pick)
    cur = i % 2
    pltpu.make_async_copy(ys_ref.at[pl.ds(0, n_loc * SUBLANES)], buf.at[cur], sems.at[cur]).wait()
    y = _dot_tn(pick.astype(BF16), _from_tiles(buf.at[cur]).astype(BF16))
    z = DEEPNORM_ALPHA * x1_ref[...] + (1.0 + mod_ref[0][5:6, :]) * y
    o_ref[...] = _layer_norm(z) * ln_ref[0:1, :] + ln_ref[1:2, :]


def _combine(tcnt, lstart, gstart, ys, lpos, gates, x1, mod, ln2, B, S, tm):
    T, D = x1.shape
    nt = S // tm
    tok = lambda b, j, *_: (b * nt + j, 0)
    col = lambda b, j, *_: (0, b * nt + j)
    grid_spec = pltpu.PrefetchScalarGridSpec(
        num_scalar_prefetch=3,
        grid=(B, nt),
        in_specs=[pl.BlockSpec(memory_space=pl.ANY),
                  pl.BlockSpec((TOP_K, tm), col),
                  pl.BlockSpec((TOP_K, tm), col),
                  pl.BlockSpec((tm, D), tok),
                  pl.BlockSpec((1, 6, D), lambda b, j, *_: (b, 0, 0)),
                  pl.BlockSpec((2, D), lambda b, j, *_: (0, 0))],
        out_specs=pl.BlockSpec((tm, D), tok),
        scratch_shapes=[pltpu.VMEM((2, TOP_K * tm * SUBLANES, LANES), F32),
                        pltpu.SemaphoreType.DMA((2,))],
    )
    return pl.pallas_call(
        _combine_kernel,
        grid_spec=grid_spec,
        out_shape=jax.ShapeDtypeStruct((T, D), F32),
        compiler_params=pltpu.CompilerParams(dimension_semantics=("arbitrary", "arbitrary"),
                                             vmem_limit_bytes=VMEM_LIMIT),
        name="combine",
    )(tcnt, lstart, gstart, ys, lpos, gates, x1, mod, ln2)


def _pad_rows(w, rows):
    return jnp.pad(w, ((0, rows - w.shape[0]), (0, 0)))


def _pad_cols(w, cols):
    return jnp.pad(w, ((0, 0), (0, cols - w.shape[1])))


def _layer(x, c, positions, w_ada, b_ada, w_in, shift_mu, rwkv_w0, rwkv_w2, rwkv_a0, rwkv_a2, rwkv_g2,
           rwkv_k_k, rwkv_k_a, rwkv_r_k, rwkv_ln_w, rwkv_ln_b, attn_sinks, w_out, ln1_g, ln1_b,
           w_router, b_router, w_gate_up, b_gate_up, w_down, b_down, ln2_g, ln2_b):
    B, S, D = x.shape
    T = B * S

    q0, k0, v0 = 0, ATTN_WIDTH, ATTN_WIDTH + KV_WIDTH
    r0 = ATTN_WIDTH + 2 * KV_WIDTH
    heads = lambda base: [w_in[:, base + h * HEAD_DIM: base + (h + 1) * HEAD_DIM] for h in range(N_KV_HEADS)]
    dup = lambda hs: [w for w in hs for _ in range(2)]
    w_attn = jnp.concatenate([w_in[:, q0:q0 + ATTN_WIDTH]] + dup(heads(k0)) + dup(heads(v0)), axis=1).astype(BF16)
    lora0 = r0 + 3 * RWKV_WIDTH
    lora = (DECAY_LORA, AAA_LORA, GATE_LORA)
    pieces_w = [w_in[:, r0:lora0]]
    pieces_mu = [shift_mu[None, 0:3 * RWKV_WIDTH]]
    off = lora0
    for n in lora:
        pieces_w.append(_pad_cols(w_in[:, off:off + n], LANES))
        pieces_mu.append(_pad_cols(shift_mu[None, off - r0:off - r0 + n], LANES))
        off += n
    w_rwkv = jnp.concatenate(pieces_w, axis=1).astype(BF16)
    mu = jnp.concatenate(pieces_mu, axis=1)
    inv_freq = ROPE_THETA ** (-jnp.arange(0, ROT_DIM, 2, dtype=F32) / ROT_DIM)
    lane_p = jnp.arange(LANES) % HEAD_DIM
    n_freq = ROT_DIM // 2
    rot_tab = jnp.zeros((8, LANES), F32)
    rot_tab = rot_tab.at[0].set(jnp.where(lane_p < ROT_DIM, 0.0, 1.0))
    rot_tab = rot_tab.at[1].set(jnp.where(lane_p < n_freq, -1.0, 0.0))
    rot_tab = rot_tab.at[2].set(jnp.where((lane_p >= n_freq) & (lane_p < ROT_DIM), 1.0, 0.0))
    freq_tab = jnp.broadcast_to(inv_freq[:, None], (n_freq, LANES))
    lane_freq = (jnp.arange(n_freq)[:, None] == (lane_p % n_freq)[None, :]) & (lane_p < ROT_DIM)[None, :]
    zeros = jnp.zeros_like(lane_freq)
    expand = jnp.concatenate([jnp.concatenate([lane_freq, zeros], axis=1),
                              jnp.concatenate([zeros, lane_freq], axis=1)], axis=0).astype(BF16)
    vecs = jnp.stack([rwkv_w0, rwkv_a0, rwkv_k_k, rwkv_k_a, rwkv_r_k.reshape(-1), rwkv_ln_w, rwkv_ln_b,
                      jnp.zeros_like(rwkv_w0)])
    w2 = _pad_rows(rwkv_w2, LANES).astype(BF16)
    a2 = _pad_rows(rwkv_a2, LANES).astype(BF16)
    g2 = _pad_rows(rwkv_g2, LANES).astype(BF16)
    wo_a = w_out[:ATTN_WIDTH].astype(BF16)
    wo_r = w_out[ATTN_WIDTH:].astype(BF16)
    w_r_hi = w_router.astype(BF16)
    w_r_lo = (w_router - w_r_hi.astype(F32)).astype(BF16)
    w_r = jnp.concatenate([_pad_cols(w_r_hi, LANES), _pad_cols(w_r_lo, LANES)], axis=1)
    b_r = jnp.concatenate([b_router, jnp.full((LANES - N_EXPERTS,), NEG_INF, F32)])[None, :]

    mod = _mod(c, w_ada, b_ada).reshape(B, 6, D)
    attn_out, rw = _inproj(x, positions, mod, w_attn, w_rwkv, mu, rot_tab, freq_tab, expand, attn_sinks,
                           min(INPROJ_TILE, S))
    rwkv_out = _rwkv(rw, vecs, w2, a2, g2, B, S, min(RWKV_STEP, S))

    mtile = min(MOE_TILE, S)
    x1, h2, top_i, gates, rank, cnt, tbase = _mix(attn_out, rwkv_out, x.reshape(T, D), mod, wo_a, wo_r,
                                                   jnp.stack([ln1_g, ln1_b]), w_r, b_r, B, S, mtile,
                                                   max(1, min(MIX_TILES, S // mtile)))

    counts = cnt[:, 0]
    padded = (counts + MOE_BLOCK - 1) // MOE_BLOCK * MOE_BLOCK
    pend = jnp.cumsum(padded)
    pstart = pend - padded
    n_blocks = T * TOP_K // MOE_BLOCK + N_EXPERTS
    blk_row = jnp.arange(n_blocks, dtype=jnp.int32) * MOE_BLOCK
    blk_e = jnp.minimum(jnp.sum((pend[None, :] <= blk_row[:, None]).astype(jnp.int32), axis=1), N_EXPERTS - 1)
    n_used = (pend[-1:] // MOE_BLOCK).astype(jnp.int32)
    tb = tbase[:, :, 0]
    tcnt = jnp.concatenate([tb[1:], counts[None]], axis=0) - tb
    lstart = jnp.cumsum(tcnt, axis=1) - tcnt
    gstart = pstart[None, :] + tb
    shift = jnp.repeat(jnp.transpose(lstart - tb), mtile, axis=1)
    experts = jnp.arange(N_EXPERTS, dtype=jnp.int32)
    lpos = rank + jnp.sum(jnp.where(top_i[None] == experts[:, None, None], shift[:, None, :], 0), axis=0)
    flat = lambda a: a.reshape(-1).astype(jnp.int32)

    xs = _dispatch(flat(tcnt), flat(lstart), flat(gstart), flat(padded - counts), flat(pstart + counts), n_used,
                   lpos, h2, n_blocks * MOE_BLOCK, mtile)
    later_with_rows = (experts[None, :] > experts[:, None]) & (counts[None, :] > 0)
    next_of = jnp.min(jnp.where(later_with_rows, experts[None, :], N_EXPERTS), axis=1)
    next_of = jnp.where(next_of < N_EXPERTS, next_of, -1)
    next_e = jnp.sum(jnp.where(blk_e[:, None] == experts[None, :], next_of[None, :], 0), axis=1).astype(jnp.int32)
    ys = _experts(blk_e, n_used, next_e, xs, w_gate_up, b_gate_up[:, None, :], w_down, b_down[:, None, :])
    out = _combine(flat(tcnt), flat(lstart), flat(gstart), ys, lpos, gates, x1, mod, jnp.stack([ln2_g, ln2_b]),
                   B, S, mtile)
    return out.reshape(B, S, D)


def kernel(x, c, positions, w_ada, b_ada, w_in, shift_mu, rwkv_w0, rwkv_w2, rwkv_a0, rwkv_a2, rwkv_g2,
           rwkv_k_k, rwkv_k_a, rwkv_r_k, rwkv_ln_w, rwkv_ln_b, attn_sinks, w_out, ln1_g, ln1_b,
           w_router, b_router, w_gate_up, b_gate_up, w_down, b_down, ln2_g, ln2_b):
    for l in range(DEPTH):
        x = _layer(x, c, positions, w_ada[l], b_ada[l], w_in[l], shift_mu[l], rwkv_w0[l], rwkv_w2[l],
                   rwkv_a0[l], rwkv_a2[l], rwkv_g2[l], rwkv_k_k[l], rwkv_k_a[l], rwkv_r_k[l], rwkv_ln_w[l],
                   rwkv_ln_b[l], attn_sinks[l], w_out[l], ln1_g[l], ln1_b[l], w_router[l], b_router[l],
                   w_gate_up[l], b_gate_up[l], w_down[l], b_down[l], ln2_g[l], ln2_b[l])
    return x
```

```python
import functools
import math

import jax
import jax.numpy as jnp
from jax import lax
from jax.experimental import pallas as pl
from jax.experimental.pallas import tpu as pltpu

F32 = jnp.float32
BF16 = jnp.bfloat16

HEAD_DIM = 64
N_ATTN_HEADS = 8
N_KV_HEADS = 2
N_RWKV_HEADS = 8
ATTN_WIDTH = N_ATTN_HEADS * HEAD_DIM
KV_WIDTH = N_KV_HEADS * HEAD_DIM
RWKV_WIDTH = N_RWKV_HEADS * HEAD_DIM
ATTN_BLOCK = 128
ROT_DIM = HEAD_DIM // 4
ROPE_THETA = 500000.0
DECAY_LORA = 32
AAA_LORA = 32
GATE_LORA = 96
N_EXPERTS = 32
TOP_K = 4
SWIGLU_LIMIT = 7.0
SWIGLU_ALPHA = 1.702
LN_EPS = 1e-5
RWKV_GN_EPS = 64e-5
NEG_INF = -1e30
DEPTH = 1
DEEPNORM_ALPHA = (2 * DEPTH) ** 0.25

LANES = 128
RWKV_CHUNK = 64
RWKV_STEP = 512
INPROJ_TILE = 512
RWKV_PROJ_CHUNK = 512
RWKV_PROJ_EARLY = 2
MOE_BLOCK = 512
MOE_TILE = 256
MIX_TILES = 4
ATTN_PROJ = ATTN_WIDTH + 4 * KV_WIDTH
RWKV_PROJ = 3 * RWKV_WIDTH + 3 * LANES
VMEM_LIMIT = 48 * 1024 * 1024


def _dot(a, b):
    return jnp.dot(a, b, preferred_element_type=F32)


def _dot_nt(a, b):
    return lax.dot_general(a, b, (((1,), (1,)), ((), ())), preferred_element_type=F32)


def _dot_tn(a, b):
    return lax.dot_general(a, b, (((0,), (0,)), ((), ())), preferred_element_type=F32)


def _split3(x):
    h = x.astype(BF16)
    r1 = x - h.astype(F32)
    m = r1.astype(BF16)
    lo = (r1 - m.astype(F32)).astype(BF16)
    return h, m, lo


def _dot_exact_lhs(m_bf16, x):
    h, m, lo = _split3(x)
    return _dot(m_bf16, h) + _dot(m_bf16, m) + _dot(m_bf16, lo)


def _layer_norm(x):
    mu = jnp.mean(x, axis=-1, keepdims=True)
    xc = x - mu
    var = jnp.mean(xc * xc, axis=-1, keepdims=True)
    return xc * lax.rsqrt(var + LN_EPS)


def _sigmoid(x):
    return 1.0 / (1.0 + jnp.exp(-x))


def _mod_kernel(c_ref, w_ref, b_ref, o_ref):
    c = c_ref[...]
    s = c * _sigmoid(c)
    o_ref[...] = jnp.dot(s, w_ref[...], preferred_element_type=F32,
                         precision=lax.Precision.HIGHEST) + b_ref[...]


def _mod(c, w_ada, b_ada):
    B, D = c.shape
    n = w_ada.shape[1] // D
    return pl.pallas_call(
        _mod_kernel,
        grid=(n,),
        in_specs=[pl.BlockSpec((B, D), lambda i: (0, 0)),
                  pl.BlockSpec((D, D), lambda i: (0, i)),
                  pl.BlockSpec((1, D), lambda i: (0, i))],
        out_specs=pl.BlockSpec((B, D), lambda i: (0, i)),
        out_shape=jax.ShapeDtypeStruct((B, n * D), F32),
        compiler_params=pltpu.CompilerParams(dimension_semantics=("arbitrary",),
                                             vmem_limit_bytes=VMEM_LIMIT),
        name="mod",
    )(c, w_ada, b_ada.reshape(1, -1))


def _inproj_kernel(x_ref, pos_ref, mod_ref, wa_ref, wr_ref, mu_ref, rt_ref, fq_ref, ex_ref, sink_ref,
                   at_ref, rw_ref, qkv_ref, kv_prev_ref, carry_ref):
    j = pl.program_id(1)

    @pl.when(j == 0)
    def _():
        kv_prev_ref[...] = jnp.zeros_like(kv_prev_ref)

    x = x_ref[0]
    tm = x.shape[0]
    mod = mod_ref[0]
    h = _layer_norm(x) * (1.0 + mod[1:2, :]) + mod[0:1, :]
    hb = h.astype(BF16)

    pa = _dot(hb, wa_ref[...])
    ang = pos_ref[0].astype(F32) * fq_ref[:, 0:1]
    pieces = _split3(jnp.concatenate([jnp.cos(ang), jnp.sin(ang)], axis=0))
    trig = _dot_tn(pieces[0], ex_ref[...]) + _dot_tn(pieces[1], ex_ref[...]) + _dot_tn(pieces[2], ex_ref[...])
    cs = trig[:, 0:LANES] + rt_ref[0:1, :]
    sn = trig[:, LANES:2 * LANES]
    m_lo = rt_ref[1:2, :]
    m_hi = rt_ref[2:3, :]
    n_q = ATTN_WIDTH // LANES
    n_rot = (ATTN_WIDTH + 2 * KV_WIDTH) // LANES

    row = lax.broadcasted_iota(jnp.int32, (tm, 1), 0)

    def rwkv_columns(c0, c1):
        def emit():
            pr = _dot(hb, wr_ref[:, c0:c1])
            prev = pltpu.roll(pr, 1, 0)
            carry = jnp.where(j == 0, 0.0, carry_ref[:, c0:c1])
            prev = jnp.where(row == 0, carry, prev)
            carry_ref[:, c0:c1] = pr[tm - 1:tm, :]
            rw_ref[:, c0:c1] = pr + (prev - pr) * mu_ref[:, c0:c1]
        return emit

    bounds = list(range(0, RWKV_PROJ, RWKV_PROJ_CHUNK)) + [RWKV_PROJ]
    rwkv_chunks = [rwkv_columns(c0, c1) for c0, c1 in zip(bounds[:-1], bounds[1:])]
    for chunk in rwkv_chunks[:RWKV_PROJ_EARLY]:
        chunk()
    for ch in range(n_rot):
        t = pa[:, ch * LANES:(ch + 1) * LANES]
        if ch < n_q:
            t = t * (1.0 / math.sqrt(HEAD_DIM))
        up = pltpu.roll(t, LANES - ROT_DIM // 2, 1)
        dn = pltpu.roll(t, ROT_DIM // 2, 1)
        o = t * cs + sn * (m_lo * up + m_hi * dn)
        qkv_ref[:, ch * LANES:(ch + 1) * LANES] = o.astype(BF16)
    qkv_ref[:, n_rot * LANES:] = pa[:, n_rot * LANES:].astype(BF16)

    kv_w = 2 * KV_WIDTH
    _attn_body(qkv_ref.at[:, pl.ds(0, ATTN_WIDTH)], qkv_ref.at[:, pl.ds(ATTN_WIDTH, kv_w)],
               kv_prev_ref.at[:, pl.ds(0, kv_w)], qkv_ref.at[:, pl.ds(ATTN_WIDTH + kv_w, kv_w)],
               kv_prev_ref.at[:, pl.ds(kv_w, kv_w)], sink_ref, at_ref, j > 0, interleave=rwkv_chunks[RWKV_PROJ_EARLY:])
    kv_prev_ref[...] = qkv_ref[tm - ATTN_BLOCK:tm, ATTN_WIDTH:]


def _inproj(x, positions, mod, w_attn, w_rwkv, mu, rot_tab, freq_tab, expand, sinks, tm):
    B, S, D = x.shape
    nt = S // tm
    return pl.pallas_call(
        _inproj_kernel,
        grid=(B, nt),
        in_specs=[pl.BlockSpec((1, tm, D), lambda b, j: (b, j, 0)),
                  pl.BlockSpec((1, 1, tm), lambda b, j: (b, 0, j)),
                  pl.BlockSpec((1, 6, D), lambda b, j: (b, 0, 0)),
                  pl.BlockSpec((D, ATTN_PROJ), lambda b, j: (0, 0)),
                  pl.BlockSpec((D, RWKV_PROJ), lambda b, j: (0, 0)),
                  pl.BlockSpec((1, RWKV_PROJ), lambda b, j: (0, 0)),
                  pl.BlockSpec((8, LANES), lambda b, j: (0, 0)),
                  pl.BlockSpec(freq_tab.shape, lambda b, j: (0, 0)),
                  pl.BlockSpec(expand.shape, lambda b, j: (0, 0)),
                  pl.BlockSpec(memory_space=pltpu.SMEM)],
        out_specs=[pl.BlockSpec((tm, ATTN_WIDTH), lambda b, j: (b * nt + j, 0)),
                   pl.BlockSpec((tm, RWKV_PROJ), lambda b, j: (b * nt + j, 0))],
        out_shape=[jax.ShapeDtypeStruct((B * S, ATTN_WIDTH), BF16),
                   jax.ShapeDtypeStruct((B * S, RWKV_PROJ), F32)],
        scratch_shapes=[pltpu.VMEM((tm, ATTN_PROJ), BF16),
                        pltpu.VMEM((ATTN_BLOCK, ATTN_PROJ - ATTN_WIDTH), BF16),
                        pltpu.VMEM((1, RWKV_PROJ), F32)],
        compiler_params=pltpu.CompilerParams(dimension_semantics=("arbitrary", "arbitrary"),
                                             vmem_limit_bytes=VMEM_LIMIT),
        name="inproj",
    )(x, positions.reshape(B, 1, S), mod, w_attn, w_rwkv, mu, rot_tab, freq_tab, expand, sinks)


def _attn_body(q_ref, kc_ref, kp_ref, vc_ref, vp_ref, sink_ref, o_ref, has_prev, interleave=()):
    blk = ATTN_BLOCK
    n_sub = q_ref.shape[0] // blk
    qi = lax.broadcasted_iota(jnp.int32, (blk, 2 * blk), 0)
    kj = lax.broadcasted_iota(jnp.int32, (blk, 2 * blk), 1)
    band = (kj > qi) & (kj <= qi + blk)
    first = band & ((kj >= blk) | has_prev)
    lane = lax.broadcasted_iota(jnp.int32, (1, LANES), 1)
    lo = (lane < HEAD_DIM).astype(BF16)
    hi = (lane >= HEAD_DIM).astype(BF16)
    halves = {}
    for u in range(n_sub):
        for g in range(N_KV_HEADS):
            sl = slice(g * LANES, (g + 1) * LANES)
            prev_k = kp_ref[:, sl] if u == 0 else kc_ref[(u - 1) * blk:u * blk, sl]
            prev_v = vp_ref[:, sl] if u == 0 else vc_ref[(u - 1) * blk:u * blk, sl]
            kcat = jnp.concatenate([prev_k, kc_ref[u * blk:(u + 1) * blk, sl]], axis=0)
            vcat = jnp.concatenate([prev_v, vc_ref[u * blk:(u + 1) * blk, sl]], axis=0)
            halves[u, g] = ((kcat * lo, vcat * lo), (kcat * hi, vcat * hi))
    units = [(u, c, half) for u in range(n_sub) for c in range(ATTN_WIDTH // LANES) for half in range(2)]
    scores = [_dot_nt(q_ref[u * blk:(u + 1) * blk, c * LANES:(c + 1) * LANES], halves[u, c // 2][half][0])
              for u, c, half in units]
    probs, denoms = [], []
    every = max(1, len(units) // max(1, len(interleave)))
    for n, ((u, c, half), s) in enumerate(zip(units, scores)):
        if n % every == 0 and n // every < len(interleave):
            interleave[n // every]()
        sink = sink_ref[2 * c + half]
        s = jnp.where(first if u == 0 else band, s, NEG_INF)
        m = jnp.maximum(jnp.max(s, axis=-1, keepdims=True), sink)
        p = jnp.exp(s - m)
        denoms.append(jnp.sum(p, axis=-1, keepdims=True) + jnp.exp(sink - m))
        probs.append(p.astype(BF16))
    outs = [_dot(p, halves[u, c // 2][half][1]) / d for (u, c, half), p, d in zip(units, probs, denoms)]
    for n, (u, c, half) in enumerate(units):
        if half == 0:
            o_ref[u * blk:(u + 1) * blk, c * LANES:(c + 1) * LANES] = (outs[n] + outs[n + 1]).astype(BF16)


def _rwkv_kernel(rw_ref, vec_ref, w2_ref, a2_ref, g2_ref, o_ref, state_ref, *, n_chunk):
    j = pl.program_id(1)
    C = RWKV_CHUNK
    W = RWKV_WIDTH
    n_pair = W // LANES

    @pl.when(j == 0)
    def _():
        state_ref[...] = jnp.zeros_like(state_ref)

    w0 = vec_ref[0:1, :]
    a0 = vec_ref[1:2, :]
    k_k = vec_ref[2:3, :]
    k_a = vec_ref[3:4, :]
    r_k = vec_ref[4:5, :]
    ln_w = vec_ref[5:6, :]
    ln_b = vec_ref[6:7, :]

    r = rw_ref[:, 0:W]
    k = rw_ref[:, W:2 * W]
    v = rw_ref[:, 2 * W:3 * W]
    wl = rw_ref[:, 3 * W:3 * W + LANES]
    al = rw_ref[:, 3 * W + LANES:3 * W + 2 * LANES]
    gl = rw_ref[:, 3 * W + 2 * LANES:3 * W + 3 * LANES]

    ri = lax.broadcasted_iota(jnp.int32, (LANES, LANES), 0)
    ci = lax.broadcasted_iota(jnp.int32, (LANES, LANES), 1)
    same = (ri // HEAD_DIM) == (ci // HEAD_DIM)
    strict = same & ((ri % HEAD_DIM) > (ci % HEAD_DIM))
    incl = same & ((ri % HEAD_DIM) >= (ci % HEAD_DIM))
    lane = lax.broadcasted_iota(jnp.int32, (1, LANES), 1)
    m0 = (lane < HEAD_DIM).astype(F32)
    m1 = 1.0 - m0
    tri = (lax.broadcasted_iota(jnp.int32, (C, C), 0) >= lax.broadcasted_iota(jnp.int32, (C, C), 1)).astype(BF16)

    def head_sum(xv):
        outs = []
        for p in range(n_pair):
            xp = xv[:, p * LANES:(p + 1) * LANES]
            s0 = jnp.sum(xp * m0, axis=1, keepdims=True)
            s1 = jnp.sum(xp * m1, axis=1, keepdims=True)
            outs.append(s0 * m0 + s1 * m1)
        return jnp.concatenate(outs, axis=1)

    def stack2(xp):
        return jnp.concatenate([xp * m0, xp * m1], axis=0)

    z = w0 + _dot(jnp.tanh(wl).astype(BF16), w2_ref[...])
    lw = -math.exp(-0.5) * _sigmoid(z)
    a = _sigmoid(a0 + _dot(al.astype(BF16), a2_ref[...]))
    g = _dot(_sigmoid(gl).astype(BF16), g2_ref[...])
    kk = k * k_k
    kkn = kk / jnp.maximum(jnp.sqrt(head_sum(kk * kk)), 1e-12)
    k2 = k * (1.0 + (a - 1.0) * k_a)
    av = -kkn
    bv = kkn * a
    bonus = head_sum(r * k2 * r_k) * v

    eye = (ri == ci).astype(F32)
    bf = lambda t: t.astype(BF16)

    pre = []
    for c in range(n_chunk):
        rows = slice(c * C, (c + 1) * C)
        lwc = lw[rows]
        cw = _dot_exact_lhs(tri, lwc)
        cwl = cw[C - 1:C, :]
        e_in = jnp.exp(cw)
        e_neg = jnp.exp(-cw)
        e_rem = jnp.exp(cwl - cw)
        wc = jnp.exp(cwl)
        Rt = r[rows] * e_in
        At = av[rows] * jnp.exp(cw - lwc)
        Bb = bv[rows] * e_neg
        Kb = k2[rows] * e_neg
        Bh = bv[rows] * e_rem
        Kh = k2[rows] * e_rem
        vc = v[rows]
        for p in range(n_pair):
            sl = slice(p * LANES, (p + 1) * LANES)
            pre.append(dict(At=At[:, sl], Rt=Rt[:, sl], Bb=Bb[:, sl], Kb=Kb[:, sl], Bh=Bh[:, sl], Kh=Kh[:, sl],
                            v=vc[:, sl], wc=wc[:, sl]))

    for u in pre:
        u["at_bd"] = stack2(u["At"])
        lhs = bf(jnp.concatenate([u["at_bd"], stack2(u["Rt"])], axis=0))
        rhs = bf(jnp.concatenate([stack2(u["Bb"]), stack2(u["Kb"])], axis=0))
        u["G"] = _dot_nt(lhs, rhs)
    for u in pre:
        G = u.pop("G")
        u["a_ab"] = jnp.where(strict, G[0:2 * C, 0:2 * C], 0.0)
        u["a_ak"] = bf(jnp.where(strict, G[0:2 * C, 2 * C:4 * C], 0.0))
        u["a_rb"] = bf(jnp.where(incl, G[2 * C:4 * C, 0:2 * C], 0.0))
        u["a_rk"] = bf(jnp.where(incl, G[2 * C:4 * C, 2 * C:4 * C], 0.0))
        u["v_bd"] = bf(stack2(u["v"]))
    for u in pre:
        xb = bf(u["a_ab"])
        u["P"] = eye + u.pop("a_ab")
        u["X"] = _dot(xb, xb)
        u["M0"] = _dot(u["a_ak"], u["v_bd"])
    for _ in range(int(math.log2(C)) - 2):
        for u in pre:
            xb = bf(u["X"])
            Wm = _dot(xb, jnp.concatenate([bf(u["P"]), xb], axis=1))
            u["P"] = u["P"] + Wm[:, 0:LANES]
            u["X"] = Wm[:, LANES:2 * LANES]
    for u in pre:
        u["P"] = bf(u["P"] + _dot(bf(u.pop("X")), bf(u["P"])))
    for u in pre:
        u["M1"] = _dot(u["P"], bf(u.pop("M0")))
        u["Q"] = bf(_dot(u["a_rb"], u["P"]))
        u["PtB"] = _dot_tn(u["P"], bf(stack2(u["Bh"])))
    for u in pre:
        M1 = u.pop("M1")
        u["Y0"] = _dot(jnp.concatenate([u["a_rb"], u["a_rk"]], axis=1),
                       jnp.concatenate([bf(M1), u["v_bd"]], axis=0))
        u["Tm"] = bf(_dot_tn(bf(u["at_bd"]), bf(u.pop("PtB"))))
        m1_pair = M1[0:C] + M1[C:2 * C]
        cst = _dot_tn(bf(jnp.concatenate([m1_pair, u["v"]], axis=0)),
                      bf(jnp.concatenate([u["Bh"], u["Kh"]], axis=0)))
        u["cst"] = jnp.where(same, cst, 0.0)
        u["ar"] = bf(jnp.concatenate([u["At"], u["Rt"]], axis=0))

    states = [state_ref[p] for p in range(n_pair)]
    for c in range(n_chunk):
        rows = slice(c * C, (c + 1) * C)
        us = pre[c * n_pair:(c + 1) * n_pair]
        sbs = [bf(S) for S in states]
        zs = [_dot_nt(u["ar"], sb) for u, sb in zip(us, sbs)]
        new_states = [S * u["wc"] + _dot(sb, u["Tm"]) + u["cst"] for u, S, sb in zip(us, states, sbs)]
        ybds = [stack2(Z[C:2 * C]) + _dot(u["Q"], bf(stack2(Z[0:C]))) + u["Y0"] for u, Z in zip(us, zs)]
        ys = [y_bd[0:C] + y_bd[C:2 * C] for y_bd in ybds]
        states = new_states
        y = jnp.concatenate(ys, axis=1)
        mu = head_sum(y) * (1.0 / HEAD_DIM)
        yc = y - mu
        var = head_sum(yc * yc) * (1.0 / HEAD_DIM)
        yn = yc * lax.rsqrt(var + RWKV_GN_EPS) * ln_w + ln_b
        o_ref[rows, :] = ((yn + bonus[rows]) * g[rows]).astype(BF16)
    for p in range(n_pair):
        state_ref[p] = states[p]


def _rwkv(rw, vecs, w2, a2, g2, B, S, lb):
    nt = S // lb
    return pl.pallas_call(
        functools.partial(_rwkv_kernel, n_chunk=lb // RWKV_CHUNK),
        grid=(B, nt),
        in_specs=[pl.BlockSpec((lb, RWKV_PROJ), lambda b, j: (b * nt + j, 0)),
                  pl.BlockSpec((8, RWKV_WIDTH), lambda b, j: (0, 0)),
                  pl.BlockSpec((LANES, RWKV_WIDTH), lambda b, j: (0, 0)),
                  pl.BlockSpec((LANES, RWKV_WIDTH), lambda b, j: (0, 0)),
                  pl.BlockSpec((LANES, RWKV_WIDTH), lambda b, j: (0, 0))],
        out_specs=pl.BlockSpec((lb, RWKV_WIDTH), lambda b, j: (b * nt + j, 0)),
        out_shape=jax.ShapeDtypeStruct((B * S, RWKV_WIDTH), BF16),
        scratch_shapes=[pltpu.VMEM((RWKV_WIDTH // LANES, LANES, LANES), F32)],
        compiler_params=pltpu.CompilerParams(dimension_semantics=("arbitrary", "arbitrary"),
                                             vmem_limit_bytes=VMEM_LIMIT),
        name="rwkv",
    )(rw, vecs, w2, a2, g2)


def _mix_kernel(at_ref, rk_ref, x_ref, mod_ref, wo_a_ref, wo_r_ref, ln_ref, wr_ref, br_ref,
                x1_ref, h2_ref, ti_ref, gt_ref, rank_ref, cnt_ref, tb_ref, base_ref):
    first = (pl.program_id(0) == 0) & (pl.program_id(1) == 0)

    @pl.when(first)
    def _():
        base_ref[...] = jnp.zeros_like(base_ref)

    mod = mod_ref[0]
    n_grp = tb_ref.shape[0]
    mt = x_ref.shape[0] // n_grp
    groups = [slice(g * mt, (g + 1) * mt) for g in range(n_grp)]
    ys = [_dot(at_ref[g, :], wo_a_ref[...]) + _dot(rk_ref[g, :], wo_r_ref[...]) for g in groups]
    logits = []
    for g, y in zip(groups, ys):
        x1 = _layer_norm(DEEPNORM_ALPHA * x_ref[g, :] + (1.0 + mod[2:3, :]) * y) * ln_ref[0:1, :] + ln_ref[1:2, :]
        h2 = _layer_norm(x1) * (1.0 + mod[4:5, :]) + mod[3:4, :]
        x1_ref[g, :] = x1
        h_hi = h2.astype(BF16)
        h2_ref[g, :] = h_hi
        h_lo = (h2 - h_hi.astype(F32)).astype(BF16)
        part = _dot(h_hi, wr_ref[...])
        logits.append(part[:, 0:LANES] + part[:, LANES:2 * LANES] + _dot(h_lo, wr_ref[:, 0:LANES]) + br_ref[...])

    erow = lax.broadcasted_iota(jnp.int32, (N_EXPERTS, mt), 0).astype(F32)
    before = (lax.broadcasted_iota(jnp.int32, (mt, mt), 0)
              < lax.broadcasted_iota(jnp.int32, (mt, mt), 1)).astype(BF16)
    base = base_ref[...]
    for n, (g, lg) in enumerate(zip(groups, logits)):
        cur = jnp.transpose(lg)[0:N_EXPERTS, :]
        vals, idxs = [], []
        for _ in range(TOP_K):
            m = jnp.max(cur, axis=0, keepdims=True)
            idx = jnp.min(jnp.where(cur == m, erow, float(N_EXPERTS)), axis=0, keepdims=True)
            vals.append(m)
            idxs.append(idx)
            cur = jnp.where(erow == idx, -jnp.inf, cur)
        tv = jnp.concatenate(vals, axis=0)
        e = jnp.exp(tv - tv[0:1, :])
        gt_ref[:, g] = e / jnp.sum(e, axis=0, keepdims=True)
        ti_ref[:, g] = jnp.concatenate(idxs, axis=0).astype(jnp.int32)

        onehot = jnp.zeros((N_EXPERTS, mt), F32)
        for idx in idxs:
            onehot = onehot + (erow == idx).astype(F32)
        tot = base[:, 0:1] + _dot(onehot.astype(BF16), before)
        ranks = [jnp.sum(jnp.where(erow == idx, tot, 0.0), axis=0, keepdims=True) for idx in idxs]
        rank_ref[:, g] = jnp.concatenate(ranks, axis=0).astype(jnp.int32)
        tb_ref[n] = base.astype(jnp.int32)
        base = base + jnp.sum(onehot, axis=1, keepdims=True)
    base_ref[...] = base
    cnt_ref[...] = base.astype(jnp.int32)


def _mix(attn_out, rwkv_out, x2d, mod, wo_a, wo_r, ln1, w_router, b_router, B, S, mtile, n_grp):
    D = x2d.shape[1]
    tm = mtile * n_grp
    nt = S // tm
    T = B * S
    tok = lambda b, j: (b * nt + j, 0)
    col = lambda b, j: (0, b * nt + j)
    fixed = lambda b, j: (0, 0)
    return pl.pallas_call(
        _mix_kernel,
        grid=(B, nt),
        in_specs=[pl.BlockSpec((tm, ATTN_WIDTH), tok),
                  pl.BlockSpec((tm, RWKV_WIDTH), tok),
                  pl.BlockSpec((tm, D), tok),
                  pl.BlockSpec((1, 6, D), lambda b, j: (b, 0, 0)),
                  pl.BlockSpec((ATTN_WIDTH, D), fixed),
                  pl.BlockSpec((RWKV_WIDTH, D), fixed),
                  pl.BlockSpec((2, D), fixed),
                  pl.BlockSpec((D, 2 * LANES), fixed),
                  pl.BlockSpec((1, LANES), fixed)],
        out_specs=[pl.BlockSpec((tm, D), tok),
                   pl.BlockSpec((tm, D), tok),
                   pl.BlockSpec((TOP_K, tm), col),
                   pl.BlockSpec((TOP_K, tm), col),
                   pl.BlockSpec((TOP_K, tm), col),
                   pl.BlockSpec((N_EXPERTS, LANES), fixed),
                   pl.BlockSpec((n_grp, N_EXPERTS, LANES), lambda b, j: (b * nt + j, 0, 0))],
        out_shape=[jax.ShapeDtypeStruct((T, D), F32),
                   jax.ShapeDtypeStruct((T, D), BF16),
                   jax.ShapeDtypeStruct((TOP_K, T), jnp.int32),
                   jax.ShapeDtypeStruct((TOP_K, T), F32),
                   jax.ShapeDtypeStruct((TOP_K, T), jnp.int32),
                   jax.ShapeDtypeStruct((N_EXPERTS, LANES), jnp.int32),
                   jax.ShapeDtypeStruct((T // mtile, N_EXPERTS, LANES), jnp.int32)],
        scratch_shapes=[pltpu.VMEM((N_EXPERTS, LANES), F32)],
        compiler_params=pltpu.CompilerParams(dimension_semantics=("arbitrary", "arbitrary"),
                                             vmem_limit_bytes=VMEM_LIMIT),
        name="mix",
    )(attn_out, rwkv_out, x2d, mod, wo_a, wo_r, ln1, w_router, b_router)


RUN_PIECES = tuple(2 ** b for b in range(int(math.log2(MOE_TILE)), -1, -1))
SUBLANES = 8


def _to_tiles(ref, x):
    n = x.shape[0]
    for c in range(SUBLANES):
        ref[pl.ds(c, n, stride=SUBLANES), :] = x[:, c * LANES:(c + 1) * LANES]


def _from_tiles(ref):
    n = ref.shape[0] // SUBLANES
    return jnp.concatenate([ref[pl.ds(c, n, stride=SUBLANES), :] for c in range(SUBLANES)], axis=1)


def _run_copies(n, local, local_start, remote, remote_start, sem, to_remote):
    off = 0
    for order, piece in enumerate(RUN_PIECES):
        take = (n & piece) != 0

        @pl.when(take)
        def _(off=off, piece=piece, order=order):
            lo = pl.multiple_of((local_start + off) * SUBLANES, SUBLANES)
            ro = pl.multiple_of((remote_start + off) * SUBLANES, SUBLANES)
            loc = local.at[pl.ds(lo, piece * SUBLANES)]
            rem = remote.at[pl.ds(ro, piece * SUBLANES)]
            src, dst = (loc, rem) if to_remote else (rem, loc)
            pltpu.make_async_copy(src, dst, sem).start(priority=order % 2)

        off = off + (n & piece)


def _dispatch_kernel(tcnt_ref, lstart_ref, gstart_ref, pad_ref, pad_start_ref, n_used_ref, lpos_ref, h2_ref,
                     xs_ref, xbuf, zbuf, sems):
    i = pl.program_id(0)
    tm = h2_ref.shape[0]
    n_loc = TOP_K * tm
    n_blocks = xs_ref.shape[0] // (MOE_BLOCK * SUBLANES)
    zero_sem = sems.at[2]

    @pl.when(i == 0)
    def _():
        zbuf[...] = jnp.zeros_like(zbuf)

        def zero_pad(e, carry):
            _run_copies(pad_ref[e], zbuf, 0, xs_ref, pad_start_ref[e], zero_sem, True)
            return carry

        def zero_tail(b, carry):
            @pl.when(b >= n_used_ref[0])
            def _():
                start = pl.multiple_of(b * (MOE_BLOCK * SUBLANES), MOE_BLOCK * SUBLANES)
                pltpu.make_async_copy(zbuf, xs_ref.at[pl.ds(start, MOE_BLOCK * SUBLANES)], zero_sem).start()
            return carry

        lax.fori_loop(0, N_EXPERTS, zero_pad, 0)
        lax.fori_loop(n_blocks - N_EXPERTS, n_blocks, zero_tail, 0)

    slot = lax.broadcasted_iota(jnp.int32, (n_loc, tm), 0)
    lpos = lpos_ref[...]
    perm = jnp.zeros((n_loc, tm), F32)
    for k in range(TOP_K):
        perm = jnp.where(slot == lpos[k:k + 1, :], 1.0, perm)
    perm = perm.astype(BF16)

    def wait_tile(s):
        pltpu.make_async_copy(xbuf.at[s], xs_ref.at[pl.ds(0, n_loc * SUBLANES)], sems.at[s]).wait()

    cur = i % 2

    @pl.when(i >= 2)
    def _():
        wait_tile(cur)

    _to_tiles(xbuf.at[cur], _dot(perm, h2_ref[...]))

    def issue(e, carry):
        idx = i * N_EXPERTS + e
        _run_copies(tcnt_ref[idx], xbuf.at[cur], lstart_ref[idx], xs_ref, gstart_ref[idx], sems.at[cur], True)
        return carry

    lax.fori_loop(0, N_EXPERTS, issue, 0)

    @pl.when(i == pl.num_programs(0) - 1)
    def _():
        wait_tile(cur)

        @pl.when(i >= 1)
        def _():
            wait_tile(1 - cur)

        n_zero = N_EXPERTS * MOE_BLOCK * SUBLANES
        pltpu.make_async_copy(xs_ref.at[pl.ds(0, n_zero)], xs_ref.at[pl.ds(0, n_zero)], zero_sem).wait()


def _dispatch(tcnt, lstart, gstart, pad, pad_start, n_used, lpos, h2, n_rows, tm):
    T, D = h2.shape
    grid_spec = pltpu.PrefetchScalarGridSpec(
        num_scalar_prefetch=6,
        grid=(T // tm,),
        in_specs=[pl.BlockSpec((TOP_K, tm), lambda i, *_: (0, i)),
                  pl.BlockSpec((tm, D), lambda i, *_: (i, 0))],
        out_specs=pl.BlockSpec(memory_space=pl.ANY),
        scratch_shapes=[pltpu.VMEM((2, TOP_K * tm * SUBLANES, LANES), F32),
                        pltpu.VMEM((MOE_BLOCK * SUBLANES, LANES), F32),
                        pltpu.SemaphoreType.DMA((3,))],
    )
    return pl.pallas_call(
        _dispatch_kernel,
        grid_spec=grid_spec,
        out_shape=jax.ShapeDtypeStruct((n_rows * SUBLANES, LANES), F32),
        compiler_params=pltpu.CompilerParams(dimension_semantics=("arbitrary",),
                                             vmem_limit_bytes=VMEM_LIMIT),
        name="dispatch",
    )(tcnt, lstart, gstart, pad, pad_start, n_used, lpos, h2)


def _experts_kernel(blk_e_ref, n_used_ref, next_e_ref, xs_ref, wgu_hbm, bgu_ref, wd_hbm, bd_ref, ys_ref,
                    wgu_f32, wd_f32, wgu_bf, wd_bf, sems):
    i = pl.program_id(0)
    d_ff = wd_bf.shape[0]
    used = i < n_used_ref[0]
    e = blk_e_ref[i]
    new_expert = (i == 0) | (e != blk_e_ref[jnp.maximum(i - 1, 0)])

    def weight_copies(ex):
        return (pltpu.make_async_copy(wgu_hbm.at[ex], wgu_f32, sems.at[0]),
                pltpu.make_async_copy(wd_hbm.at[ex], wd_f32, sems.at[1]))

    @pl.when(i == 0)
    def _():
        for cp in weight_copies(e):
            cp.start()

    @pl.when(used & new_expert)
    def _():
        for cp in weight_copies(e):
            cp.wait()
        wgu_bf[...] = wgu_f32[...].astype(BF16)
        wd_bf[...] = wd_f32[...].astype(BF16)
        nxt = next_e_ref[i]

        @pl.when(nxt >= 0)
        def _():
            for cp in weight_copies(nxt):
                cp.start()

    @pl.when(used)
    def _():
        xb = _from_tiles(xs_ref).astype(BF16)
        gu = _dot(xb, wgu_bf[...]) + bgu_ref[0]
        gate = jnp.minimum(gu[:, :d_ff], SWIGLU_LIMIT)
        up = jnp.clip(gu[:, d_ff:], -SWIGLU_LIMIT, SWIGLU_LIMIT)
        act = (up + 1.0) * (gate * _sigmoid(SWIGLU_ALPHA * gate))
        _to_tiles(ys_ref, _dot(act.astype(BF16), wd_bf[...]) + bd_ref[0])

    @pl.when(i >= n_used_ref[0])
    def _():
        ys_ref[...] = jnp.zeros_like(ys_ref)


def _experts(blk_e, n_used, next_e, xs, wgu, bgu, wd, bd):
    d_ff, D = wd.shape[1], wd.shape[2]
    n_blocks = xs.shape[0] // (MOE_BLOCK * SUBLANES)
    blk = (MOE_BLOCK * SUBLANES, LANES)

    def last_used(i, n_used_ref):
        return jnp.minimum(i, jnp.maximum(n_used_ref[0] - 1, 0))

    def row_map(i, blk_e_ref, n_used_ref, next_e_ref):
        return (last_used(i, n_used_ref), 0)

    def exp_map(i, blk_e_ref, n_used_ref, next_e_ref):
        return (blk_e_ref[last_used(i, n_used_ref)], 0, 0)

    grid_spec = pltpu.PrefetchScalarGridSpec(
        num_scalar_prefetch=3,
        grid=(n_blocks,),
        in_specs=[pl.BlockSpec(blk, row_map),
                  pl.BlockSpec(memory_space=pl.ANY),
                  pl.BlockSpec((1, 1, 2 * d_ff), exp_map),
                  pl.BlockSpec(memory_space=pl.ANY),
                  pl.BlockSpec((1, 1, D), exp_map)],
        out_specs=pl.BlockSpec(blk, lambda i, *_: (i, 0)),
        scratch_shapes=[pltpu.VMEM((D, 2 * d_ff), F32), pltpu.VMEM((d_ff, D), F32),
                        pltpu.VMEM((D, 2 * d_ff), BF16), pltpu.VMEM((d_ff, D), BF16),
                        pltpu.SemaphoreType.DMA((2,))],
    )
    return pl.pallas_call(
        _experts_kernel,
        grid_spec=grid_spec,
        out_shape=jax.ShapeDtypeStruct(xs.shape, F32),
        compiler_params=pltpu.CompilerParams(dimension_semantics=("arbitrary",),
                                             vmem_limit_bytes=VMEM_LIMIT),
        name="experts",
    )(blk_e, n_used, next_e, xs, wgu, bgu, wd, bd)


def _combine_kernel(tcnt_ref, lstart_ref, gstart_ref, ys_ref, lpos_ref, gt_ref, x1_ref, mod_ref, ln_ref,
                    o_ref, buf, sems):
    nt = pl.num_programs(1)
    n_tiles = pl.num_programs(0) * nt
    i = pl.program_id(0) * nt + pl.program_id(1)
    tm = x1_ref.shape[0]
    n_loc = TOP_K * tm

    def fetch(tile, s):
        def issue(e, carry):
            idx = tile * N_EXPERTS + e
            _run_copies(tcnt_ref[idx], buf.at[s], lstart_ref[idx], ys_ref, gstart_ref[idx], sems.at[s], False)
            return carry

        lax.fori_loop(0, N_EXPERTS, issue, 0)

    @pl.when(i == 0)
    def _():
        fetch(i, 0)

    @pl.when(i + 1 < n_tiles)
    def _():
        fetch(i + 1, (i + 1) % 2)

    slot = lax.broadcasted_iota(jnp.int32, (n_loc, tm), 0)
    lpos = lpos_ref[...]
    gt = gt_ref[...]
    pick = jnp.zeros((n_loc, tm), F32)
    for k in range(TOP_K):
        pick = jnp.where(slot == lpos[k:k + 1, :], gt[k:k + 1, :], pick)
    cur = i % 2
    pltpu.make_async_copy(ys_ref.at[pl.ds(0, n_loc * SUBLANES)], buf.at[cur], sems.at[cur]).wait()
    y = _dot_tn(pick.astype(BF16), _from_tiles(buf.at[cur]).astype(BF16))
    z = DEEPNORM_ALPHA * x1_ref[...] + (1.0 + mod_ref[0][5:6, :]) * y
    o_ref[...] = _layer_norm(z) * ln_ref[0:1, :] + ln_ref[1:2, :]


def _combine(tcnt, lstart, gstart, ys, lpos, gates, x1, mod, ln2, B, S, tm):
    T, D = x1.shape
    nt = S // tm
    tok = lambda b, j, *_: (b * nt + j, 0)
    col = lambda b, j, *_: (0, b * nt + j)
    grid_spec = pltpu.PrefetchScalarGridSpec(
        num_scalar_prefetch=3,
        grid=(B, nt),
        in_specs=[pl.BlockSpec(memory_space=pl.ANY),
                  pl.BlockSpec((TOP_K, tm), col),
                  pl.BlockSpec((TOP_K, tm), col),
                  pl.BlockSpec((tm, D), tok),
                  pl.BlockSpec((1, 6, D), lambda b, j, *_: (b, 0, 0)),
                  pl.BlockSpec((2, D), lambda b, j, *_: (0, 0))],
        out_specs=pl.BlockSpec((tm, D), tok),
        scratch_shapes=[pltpu.VMEM((2, TOP_K * tm * SUBLANES, LANES), F32),
                        pltpu.SemaphoreType.DMA((2,))],
    )
    return pl.pallas_call(
        _combine_kernel,
        grid_spec=grid_spec,
        out_shape=jax.ShapeDtypeStruct((T, D), F32),
        compiler_params=pltpu.CompilerParams(dimension_semantics=("arbitrary", "arbitrary"),
                                             vmem_limit_bytes=VMEM_LIMIT),
        name="combine",
    )(tcnt, lstart, gstart, ys, lpos, gates, x1, mod, ln2)


def _pad_rows(w, rows):
    return jnp.pad(w, ((0, rows - w.shape[0]), (0, 0)))


def _pad_cols(w, cols):
    return jnp.pad(w, ((0, 0), (0, cols - w.shape[1])))


def _layer(x, c, positions, w_ada, b_ada, w_in, shift_mu, rwkv_w0, rwkv_w2, rwkv_a0, rwkv_a2, rwkv_g2,
           rwkv_k_k, rwkv_k_a, rwkv_r_k, rwkv_ln_w, rwkv_ln_b, attn_sinks, w_out, ln1_g, ln1_b,
           w_router, b_router, w_gate_up, b_gate_up, w_down, b_down, ln2_g, ln2_b):
    B, S, D = x.shape
    T = B * S

    q0, k0, v0 = 0, ATTN_WIDTH, ATTN_WIDTH + KV_WIDTH
    r0 = ATTN_WIDTH + 2 * KV_WIDTH
    heads = lambda base: [w_in[:, base + h * HEAD_DIM: base + (h + 1) * HEAD_DIM] for h in range(N_KV_HEADS)]
    dup = lambda hs: [w for w in hs for _ in range(2)]
    w_attn = jnp.concatenate([w_in[:, q0:q0 + ATTN_WIDTH]] + dup(heads(k0)) + dup(heads(v0)), axis=1).astype(BF16)
    lora0 = r0 + 3 * RWKV_WIDTH
    lora = (DECAY_LORA, AAA_LORA, GATE_LORA)
    pieces_w = [w_in[:, r0:lora0]]
    pieces_mu = [shift_mu[None, 0:3 * RWKV_WIDTH]]
    off = lora0
    for n in lora:
        pieces_w.append(_pad_cols(w_in[:, off:off + n], LANES))
        pieces_mu.append(_pad_cols(shift_mu[None, off - r0:off - r0 + n], LANES))
        off += n
    w_rwkv = jnp.concatenate(pieces_w, axis=1).astype(BF16)
    mu = jnp.concatenate(pieces_mu, axis=1)
    inv_freq = ROPE_THETA ** (-jnp.arange(0, ROT_DIM, 2, dtype=F32) / ROT_DIM)
    lane_p = jnp.arange(LANES) % HEAD_DIM
    n_freq = ROT_DIM // 2
    rot_tab = jnp.zeros((8, LANES), F32)
    rot_tab = rot_tab.at[0].set(jnp.where(lane_p < ROT_DIM, 0.0, 1.0))
    rot_tab = rot_tab.at[1].set(jnp.where(lane_p < n_freq, -1.0, 0.0))
    rot_tab = rot_tab.at[2].set(jnp.where((lane_p >= n_freq) & (lane_p < ROT_DIM), 1.0, 0.0))
    freq_tab = jnp.broadcast_to(inv_freq[:, None], (n_freq, LANES))
    lane_freq = (jnp.arange(n_freq)[:, None] == (lane_p % n_freq)[None, :]) & (lane_p < ROT_DIM)[None, :]
    zeros = jnp.zeros_like(lane_freq)
    expand = jnp.concatenate([jnp.concatenate([lane_freq, zeros], axis=1),
                              jnp.concatenate([zeros, lane_freq], axis=1)], axis=0).astype(BF16)
    vecs = jnp.stack([rwkv_w0, rwkv_a0, rwkv_k_k, rwkv_k_a, rwkv_r_k.reshape(-1), rwkv_ln_w, rwkv_ln_b,
                      jnp.zeros_like(rwkv_w0)])
    w2 = _pad_rows(rwkv_w2, LANES).astype(BF16)
    a2 = _pad_rows(rwkv_a2, LANES).astype(BF16)
    g2 = _pad_rows(rwkv_g2, LANES).astype(BF16)
    wo_a = w_out[:ATTN_WIDTH].astype(BF16)
    wo_r = w_out[ATTN_WIDTH:].astype(BF16)
    w_r_hi = w_router.astype(BF16)
    w_r_lo = (w_router - w_r_hi.astype(F32)).astype(BF16)
    w_r = jnp.concatenate([_pad_cols(w_r_hi, LANES), _pad_cols(w_r_lo, LANES)], axis=1)
    b_r = jnp.concatenate([b_router, jnp.full((LANES - N_EXPERTS,), NEG_INF, F32)])[None, :]

    mod = _mod(c, w_ada, b_ada).reshape(B, 6, D)
    attn_out, rw = _inproj(x, positions, mod, w_attn, w_rwkv, mu, rot_tab, freq_tab, expand, attn_sinks,
                           min(INPROJ_TILE, S))
    rwkv_out = _rwkv(rw, vecs, w2, a2, g2, B, S, min(RWKV_STEP, S))

    mtile = min(MOE_TILE, S)
    x1, h2, top_i, gates, rank, cnt, tbase = _mix(attn_out, rwkv_out, x.reshape(T, D), mod, wo_a, wo_r,
                                                   jnp.stack([ln1_g, ln1_b]), w_r, b_r, B, S, mtile,
                                                   max(1, min(MIX_TILES, S // mtile)))

    counts = cnt[:, 0]
    padded = (counts + MOE_BLOCK - 1) // MOE_BLOCK * MOE_BLOCK
    pend = jnp.cumsum(padded)
    pstart = pend - padded
    n_blocks = T * TOP_K // MOE_BLOCK + N_EXPERTS
    blk_row = jnp.arange(n_blocks, dtype=jnp.int32) * MOE_BLOCK
    blk_e = jnp.minimum(jnp.sum((pend[None, :] <= blk_row[:, None]).astype(jnp.int32), axis=1), N_EXPERTS - 1)
    n_used = (pend[-1:] // MOE_BLOCK).astype(jnp.int32)
    tb = tbase[:, :, 0]
    tcnt = jnp.concatenate([tb[1:], counts[None]], axis=0) - tb
    lstart = jnp.cumsum(tcnt, axis=1) - tcnt
    gstart = pstart[None, :] + tb
    shift = jnp.repeat(jnp.transpose(lstart - tb), mtile, axis=1)
    experts = jnp.arange(N_EXPERTS, dtype=jnp.int32)
    lpos = rank + jnp.sum(jnp.where(top_i[None] == experts[:, None, None], shift[:, None, :], 0), axis=0)
    flat = lambda a: a.reshape(-1).astype(jnp.int32)

    xs = _dispatch(flat(tcnt), flat(lstart), flat(gstart), flat(padded - counts), flat(pstart + counts), n_used,
                   lpos, h2, n_blocks * MOE_BLOCK, mtile)
    later_with_rows = (experts[None, :] > experts[:, None]) & (counts[None, :] > 0)
    next_of = jnp.min(jnp.where(later_with_rows, experts[None, :], N_EXPERTS), axis=1)
    next_of = jnp.where(next_of < N_EXPERTS, next_of, -1)
    next_e = jnp.sum(jnp.where(blk_e[:, None] == experts[None, :], next_of[None, :], 0), axis=1).astype(jnp.int32)
    ys = _experts(blk_e, n_used, next_e, xs, w_gate_up, b_gate_up[:, None, :], w_down, b_down[:, None, :])
    out = _combine(flat(tcnt), flat(lstart), flat(gstart), ys, lpos, gates, x1, mod, jnp.stack([ln2_g, ln2_b]),
                   B, S, mtile)
    return out.reshape(B, S, D)


def kernel(x, c, positions, w_ada, b_ada, w_in, shift_mu, rwkv_w0, rwkv_w2, rwkv_a0, rwkv_a2, rwkv_g2,
           rwkv_k_k, rwkv_k_a, rwkv_r_k, rwkv_ln_w, rwkv_ln_b, attn_sinks, w_out, ln1_g, ln1_b,
           w_router, b_router, w_gate_up, b_gate_up, w_down, b_down, ln2_g, ln2_b):
    for l in range(DEPTH):
        x = _layer(x, c, positions, w_ada[l], b_ada[l], w_in[l], shift_mu[l], rwkv_w0[l], rwkv_w2[l],
                   rwkv_a0[l], rwkv_a2[l], rwkv_g2[l], rwkv_k_k[l], rwkv_k_a[l], rwkv_r_k[l], rwkv_ln_w[l],
                   rwkv_ln_b[l], attn_sinks[l], w_out[l], ln1_g[l], ln1_b[l], w_router[l], b_router[l],
                   w_gate_up[l], b_gate_up[l], w_down[l], b_down[l], ln2_g[l], ln2_b[l])
    return x
```
